```python
import jax, jax.numpy as jnp
from jax import lax
import numpy as np

D_MODEL = 1024
BATCH = 4
SEQ = 8192
DEPTH = 1

NSA_HEADS = 8
NSA_HEAD_DIM = 64
NSA_WIDTH = NSA_HEADS * NSA_HEAD_DIM
NSA_KV_HEADS = 2
NSA_HEADS_PER_KV = NSA_HEADS // NSA_KV_HEADS
NSA_KV_WIDTH = NSA_KV_HEADS * NSA_HEAD_DIM
CMP_BLOCK = 32
CMP_STRIDE = 16
CMP_HIDDEN = 256
SEL_BLOCK = 64
N_SELECT = 16
WINDOW = 512
Q_BLOCK = 128
HGRN_HEADS = 4
HGRN_KEY_DIM = 128
HGRN_VAL_DIM = 128
HGRN_WIDTH = HGRN_HEADS * HGRN_VAL_DIM
HGRN_CHUNK = 64
MIX_WIDTH = NSA_WIDTH + HGRN_WIDTH
N_EXPERTS = 256
TOP_K = 8
N_GROUPS = 8
TOPK_GROUPS = 4
EXPERT_FF = 256
SHARED_FF = 256
ROUTED_SCALE = 2.5
EXPERT_ROWS = 128
RMS_EPS = 1e-6
BIG = 1e9
IN_SPLITS = (NSA_WIDTH,) + (NSA_KV_WIDTH,) * 6 + (NSA_HEADS * 3,) + (HGRN_HEADS * HGRN_KEY_DIM, HGRN_HEADS * HGRN_KEY_DIM, HGRN_WIDTH, HGRN_WIDTH)
IN_COLS = sum(IN_SPLITS)

kernel_name = 'hymba_nsa_hgrn2_moe_adaln_block'


def rms_norm(x, w):
    xf = x.astype(jnp.float32)
    y = xf * lax.rsqrt(jnp.mean(xf * xf, axis=-1, keepdims=True) + RMS_EPS)
    return (y * w.astype(jnp.float32)).astype(x.dtype)


def masked_softmax(s, valid):
    s = jnp.where(valid, s, -jnp.inf)
    m = jnp.max(s, axis=-1, keepdims=True)
    m = jnp.where(jnp.isfinite(m), m, 0.0)
    e = jnp.exp(s - m)
    return e / jnp.maximum(jnp.sum(e, axis=-1, keepdims=True), 1e-30)


def alibi_slopes(n):
    return jnp.asarray(np.array([2.0 ** (-8.0 * (i + 1) / n) for i in range(n)], dtype=np.float32))


def compress_blocks(t, pos, w1, b1, w2):
    b, s, g, d = t.shape
    ch = t.reshape(b, s // CMP_STRIDE, CMP_STRIDE, g, d)
    blk = jnp.concatenate([ch[:, :-1], ch[:, 1:]], axis=2) + pos[None, None, :, None, :]
    flat = blk.transpose(0, 1, 3, 2, 4).reshape(b, s // CMP_STRIDE - 1, g, CMP_BLOCK * d)
    return jax.nn.gelu(flat @ w1 + b1) @ w2


def nsa_mixer(q, kc_raw, vc_raw, ks, vs, kw, vw, gate_logits, q_norm_w, k_norm_w,
              cmp_pos, cmp_w1, cmp_b1, cmp_w2, out_norm_w):
    b, s, _ = q.shape
    g_, hg, hd = NSA_KV_HEADS, NSA_HEADS_PER_KV, NSA_HEAD_DIM
    n_cmp = s // CMP_STRIDE - 1
    n_slc = s // SEL_BLOCK
    n_top = min(N_SELECT, n_slc)
    n_sel_tok = n_top * SEL_BLOCK
    scale = hd ** -0.5
    q = rms_norm(q.reshape(b, s, NSA_HEADS, hd), q_norm_w)
    kc = rms_norm(compress_blocks(kc_raw.reshape(b, s, g_, hd), cmp_pos[0], cmp_w1[0], cmp_b1[0], cmp_w2[0]), k_norm_w[0])
    vc = compress_blocks(vc_raw.reshape(b, s, g_, hd), cmp_pos[1], cmp_w1[1], cmp_b1[1], cmp_w2[1])
    ks_blk = rms_norm(ks.reshape(b, s, g_, hd), k_norm_w[1]).reshape(b, n_slc, SEL_BLOCK, g_, hd).transpose(0, 3, 1, 2, 4)
    vs_blk = vs.reshape(b, n_slc, SEL_BLOCK, g_, hd).transpose(0, 3, 1, 2, 4)
    pad = jnp.zeros((b, WINDOW, g_, hd), kw.dtype)
    kw_pad = jnp.concatenate([pad, rms_norm(kw.reshape(b, s, g_, hd), k_norm_w[2])], axis=1)
    vw_pad = jnp.concatenate([pad, vw.reshape(b, s, g_, hd)], axis=1)
    gates = jax.nn.sigmoid(gate_logits.astype(jnp.float32)).reshape(b, s, g_, hg, 3)
    slopes = alibi_slopes(NSA_HEADS).reshape(g_, hg)[None, :, :, None, None]
    cmp_end = jnp.arange(n_cmp) * CMP_STRIDE + CMP_BLOCK - 1
    blk_ids = jnp.arange(n_slc)
    ci = jnp.arange(n_cmp)[:, None] * CMP_STRIDE
    overlap = ((ci < (blk_ids[None, :] + 1) * SEL_BLOCK) & (ci + CMP_BLOCK > blk_ids[None, :] * SEL_BLOCK)).astype(jnp.float32)
    bi = jnp.arange(b)[:, None, None, None]
    gi = jnp.arange(g_)[None, :, None, None]
    tok_in_blk = jnp.arange(SEL_BLOCK)

    def query_block(qb_idx):
        q0 = qb_idx * Q_BLOCK
        t = q0 + jnp.arange(Q_BLOCK)
        qb = lax.dynamic_slice_in_dim(q, q0, Q_BLOCK, axis=1).reshape(b, Q_BLOCK, g_, hg, hd)
        gb = lax.dynamic_slice_in_dim(gates, q0, Q_BLOCK, axis=1)
        dist_c = t[:, None] - cmp_end[None, :]
        s_c = jnp.einsum('bqghd,bngd->bghqn', qb, kc).astype(jnp.float32) * scale - slopes * dist_c
        p_c = masked_softmax(s_c, dist_c >= 0)
        o_c = jnp.einsum('bghqn,bngd->bqghd', p_c.astype(vc.dtype), vc)
        imp = jnp.einsum('bghqn,nj->bgqj', p_c, overlap)
        cur = (t // SEL_BLOCK)[:, None]
        forced = (blk_ids[None, :] == 0) | (blk_ids[None, :] == cur) | (blk_ids[None, :] == cur - 1)
        causal_blk = blk_ids[None, :] * SEL_BLOCK <= t[:, None]
        rank = jnp.where(forced, BIG, jnp.where(causal_blk, imp, -BIG))
        _, idx = lax.top_k(rank, n_top)
        k_sel = ks_blk[bi, gi, idx].reshape(b, g_, Q_BLOCK, n_sel_tok, hd)
        v_sel = vs_blk[bi, gi, idx].reshape(b, g_, Q_BLOCK, n_sel_tok, hd)
        pos = (idx[..., None] * SEL_BLOCK + tok_in_blk).reshape(b, g_, Q_BLOCK, n_sel_tok)
        dist_s = (t[None, None, :, None] - pos)[:, :, None]
        s_s = jnp.einsum('bqghd,bgqkd->bghqk', qb, k_sel).astype(jnp.float32) * scale - slopes * dist_s
        p_s = masked_softmax(s_s, dist_s >= 0)
        o_s = jnp.einsum('bghqk,bgqkd->bqghd', p_s.astype(v_sel.dtype), v_sel)
        kwb = lax.dynamic_slice_in_dim(kw_pad, q0, Q_BLOCK + WINDOW, axis=1)
        vwb = lax.dynamic_slice_in_dim(vw_pad, q0, Q_BLOCK + WINDOW, axis=1)
        src = q0 - WINDOW + jnp.arange(Q_BLOCK + WINDOW)
        dist_w = t[:, None] - src[None, :]
        valid_w = (dist_w >= 0) & (dist_w < WINDOW) & (src >= 0)[None, :]
        s_w = jnp.einsum('bqghd,bkgd->bghqk', qb, kwb).astype(jnp.float32) * scale - slopes * dist_w
        p_w = masked_softmax(s_w, valid_w)
        o_w = jnp.einsum('bghqk,bkgd->bqghd', p_w.astype(vwb.dtype), vwb)
        out = gb[..., 0:1] * o_c + gb[..., 1:2] * o_s + gb[..., 2:3] * o_w
        return out.astype(q.dtype)

    o = lax.map(query_block, jnp.arange(s // Q_BLOCK))
    o = o.transpose(1, 0, 2, 3, 4, 5).reshape(b, s, NSA_HEADS, hd)
    return rms_norm(o, out_norm_w.reshape(NSA_HEADS, hd)).reshape(b, s, NSA_WIDTH)


def hgrn2_mixer(q_r, f_r, i_r, g_r, lb, out_norm_w):
    b, s, _ = q_r.shape
    n_chunk = s // HGRN_CHUNK
    q = jax.nn.silu(q_r.astype(jnp.float32)) * HGRN_KEY_DIM ** -0.5
    f = lb + (1.0 - lb) * jax.nn.sigmoid(f_r.astype(jnp.float32))
    k = 1.0 - f
    logf = jnp.log(f)
    v = i_r.astype(jnp.float32)

    def to_chunks(a, d):
        return a.reshape(b, n_chunk, HGRN_CHUNK, HGRN_HEADS, d).transpose(1, 0, 3, 2, 4)

    causal = jnp.tril(jnp.ones((HGRN_CHUNK, HGRN_CHUNK), bool))

    def step(state, inp):
        qc, kc, vc, gc = inp
        cum = jnp.cumsum(gc, axis=2)
        o_inter = jnp.einsum('bhtk,bhkv->bhtv', qc * jnp.exp(cum), state)
        diff = cum[:, :, :, None, :] - cum[:, :, None, :, :]
        decay = jnp.exp(jnp.where(causal[:, :, None], diff, -jnp.inf))
        scores = jnp.einsum('bhtsk,bhsk->bhts', qc[:, :, :, None, :] * decay, kc)
        o_intra = jnp.einsum('bhts,bhsv->bhtv', scores, vc)
        last = cum[:, :, -1:, :]
        state = jnp.exp(last[:, :, 0, :])[..., None] * state + jnp.einsum('bhsk,bhsv->bhkv', kc * jnp.exp(last - cum), vc)
        return state, o_inter + o_intra

    state0 = jnp.zeros((b, HGRN_HEADS, HGRN_KEY_DIM, HGRN_VAL_DIM), jnp.float32)
    _, o = lax.scan(step, state0, (to_chunks(q, HGRN_KEY_DIM), to_chunks(k, HGRN_KEY_DIM),
                                   to_chunks(v, HGRN_VAL_DIM), to_chunks(logf, HGRN_KEY_DIM)))
    o = o.transpose(1, 0, 3, 2, 4).reshape(b, s, HGRN_HEADS, HGRN_VAL_DIM)
    o = o * jax.nn.sigmoid(g_r.astype(jnp.float32)).reshape(b, s, HGRN_HEADS, HGRN_VAL_DIM)
    o = rms_norm(o, out_norm_w.reshape(HGRN_HEADS, HGRN_VAL_DIM))
    return o.reshape(b, s, HGRN_WIDTH).astype(q_r.dtype)


def swiglu(x, wg, wu, wd):
    return (jax.nn.silu(x @ wg) * (x @ wu)) @ wd


def route(h, router_w, router_bias):
    t = h.shape[0]
    scores = jax.nn.sigmoid((h @ router_w).astype(jnp.float32))
    biased = scores + router_bias.astype(jnp.float32)
    grp_score = lax.top_k(biased.reshape(t, N_GROUPS, N_EXPERTS // N_GROUPS), 2)[0].sum(-1)
    _, top_grp = lax.top_k(grp_score, TOPK_GROUPS)
    grp_mask = jax.nn.one_hot(top_grp, N_GROUPS, dtype=jnp.float32).sum(1) > 0
    exp_mask = jnp.repeat(grp_mask, N_EXPERTS // N_GROUPS, axis=1)
    _, idx = lax.top_k(jnp.where(exp_mask, biased, -jnp.inf), TOP_K)
    w = jnp.take_along_axis(scores, idx, axis=1)
    w = w / jnp.sum(w, axis=-1, keepdims=True) * ROUTED_SCALE
    return idx, w


def routed_experts(h, idx, w, w_gate, w_up, w_down):
    t, d = h.shape
    n = t * TOP_K
    n_blk = (n + EXPERT_ROWS - 1) // EXPERT_ROWS + N_EXPERTS
    flat_e = idx.reshape(-1).astype(jnp.int32)
    flat_tok = jnp.repeat(jnp.arange(t, dtype=jnp.int32), TOP_K)
    flat_w = w.reshape(-1)
    counts = jnp.bincount(flat_e, length=N_EXPERTS)
    padded = (counts + EXPERT_ROWS - 1) // EXPERT_ROWS * EXPERT_ROWS
    start = jnp.cumsum(counts) - counts
    pad_end = jnp.cumsum(padded)
    pad_start = pad_end - padded
    order = jnp.argsort(flat_e)
    e_sorted = flat_e[order]
    dest = pad_start[e_sorted] + jnp.arange(n, dtype=jnp.int32) - start[e_sorted]
    rows = jnp.zeros(n_blk * EXPERT_ROWS, jnp.int32).at[dest].set(flat_tok[order])
    wts = jnp.zeros(n_blk * EXPERT_ROWS, jnp.float32).at[dest].set(flat_w[order])
    blk_e = jnp.minimum(jnp.searchsorted(pad_end, jnp.arange(n_blk, dtype=jnp.int32) * EXPERT_ROWS, side='right'), N_EXPERTS - 1)

    def step(acc, xs):
        r, wt, e = xs
        y = swiglu(h[r], w_gate[e], w_up[e], w_down[e])
        return acc.at[r].add(y * wt[:, None].astype(y.dtype)), None

    acc, _ = lax.scan(step, jnp.zeros((t, d), h.dtype),
                      (rows.reshape(n_blk, EXPERT_ROWS), wts.reshape(n_blk, EXPERT_ROWS), blk_e))
    return acc


def setup_inputs(seed: int = 0) -> dict:
    key = jax.random.key(seed)
    ks = jax.random.split(key, 26)
    f32 = jnp.float32

    def nrm(k, shape, scale):
        return jax.random.normal(k, shape, f32) * scale

    def gain(k, shape):
        return 1.0 + nrm(k, shape, 0.02)

    return {
        'x': nrm(ks[0], (BATCH, SEQ, D_MODEL), 1.0),
        'c': nrm(ks[1], (BATCH, D_MODEL), 1.0),
        'ada_w': nrm(ks[2], (DEPTH, D_MODEL, 6 * D_MODEL), 0.5 * D_MODEL ** -0.5),
        'ada_b': nrm(ks[3], (DEPTH, 6 * D_MODEL), 0.02),
        'norm1_w': gain(ks[4], (DEPTH, D_MODEL)),
        'norm2_w': gain(ks[5], (DEPTH, D_MODEL)),
        'w_in': nrm(ks[6], (DEPTH, D_MODEL, IN_COLS), D_MODEL ** -0.5),
        'q_norm_w': gain(ks[7], (DEPTH, NSA_HEAD_DIM)),
        'k_norm_w': gain(ks[8], (DEPTH, 3, NSA_HEAD_DIM)),
        'cmp_pos': nrm(ks[9], (DEPTH, 2, CMP_BLOCK, NSA_HEAD_DIM), 0.2),
        'cmp_w1': nrm(ks[10], (DEPTH, 2, CMP_BLOCK * NSA_HEAD_DIM, CMP_HIDDEN), (CMP_BLOCK * NSA_HEAD_DIM) ** -0.5),
        'cmp_b1': nrm(ks[11], (DEPTH, 2, CMP_HIDDEN), 0.02),
        'cmp_w2': nrm(ks[12], (DEPTH, 2, CMP_HIDDEN, NSA_HEAD_DIM), CMP_HIDDEN ** -0.5),
        'attn_out_norm_w': gain(ks[13], (DEPTH, NSA_WIDTH)),
        'hgrn_lb_param': nrm(ks[14], (DEPTH + 1, HGRN_HEADS * HGRN_KEY_DIM), 0.5),
        'rec_out_norm_w': gain(ks[15], (DEPTH, HGRN_WIDTH)),
        'w_out': nrm(ks[16], (DEPTH, MIX_WIDTH, D_MODEL), MIX_WIDTH ** -0.5),
        'router_w': nrm(ks[17], (DEPTH, D_MODEL, N_EXPERTS), D_MODEL ** -0.5),
        'router_bias': nrm(ks[18], (DEPTH, N_EXPERTS), 0.01),
        'exp_w_gate': nrm(ks[19], (DEPTH, N_EXPERTS, D_MODEL, EXPERT_FF), D_MODEL ** -0.5),
        'exp_w_up': nrm(ks[20], (DEPTH, N_EXPERTS, D_MODEL, EXPERT_FF), D_MODEL ** -0.5),
        'exp_w_down': nrm(ks[21], (DEPTH, N_EXPERTS, EXPERT_FF, D_MODEL), EXPERT_FF ** -0.5),
        'shared_w_gate': nrm(ks[22], (DEPTH, D_MODEL, SHARED_FF), D_MODEL ** -0.5),
        'shared_w_up': nrm(ks[23], (DEPTH, D_MODEL, SHARED_FF), D_MODEL ** -0.5),
        'shared_w_down': nrm(ks[24], (DEPTH, SHARED_FF, D_MODEL), SHARED_FF ** -0.5),
    }


def reference(x, c, ada_w, ada_b, norm1_w, norm2_w, w_in, q_norm_w, k_norm_w, cmp_pos, cmp_w1,
              cmp_b1, cmp_w2, attn_out_norm_w, hgrn_lb_param, rec_out_norm_w, w_out, router_w,
              router_bias, exp_w_gate, exp_w_up, exp_w_down, shared_w_gate, shared_w_up, shared_w_down):
    b, s, d = x.shape
    lower_bounds = jnp.cumsum(jax.nn.softmax(hgrn_lb_param.astype(jnp.float32), axis=0), axis=0)
    split_at = [int(v) for v in np.cumsum(IN_SPLITS)[:-1]]
    cond = jax.nn.silu(c)
    for l in range(DEPTH):
        mod = cond @ ada_w[l] + ada_b[l]
        sh1, sc1, gt1, sh2, sc2, gt2 = [m[:, None, :] for m in jnp.split(mod, 6, axis=-1)]
        h = rms_norm(x, norm1_w[l]) * (1 + sc1) + sh1
        proj = h @ w_in[l]
        (q, kc, vc, ksl, vsl, kwn, vwn, gl, hq, hf, hi, hg) = jnp.split(proj, split_at, axis=-1)
        attn = nsa_mixer(q, kc, vc, ksl, vsl, kwn, vwn, gl, q_norm_w[l], k_norm_w[l], cmp_pos[l],
                         cmp_w1[l], cmp_b1[l], cmp_w2[l], attn_out_norm_w[l])
        rec = hgrn2_mixer(hq, hf, hi, hg, lower_bounds[l], rec_out_norm_w[l])
        mixed = jnp.concatenate([attn, rec], axis=-1) @ w_out[l]
        x = x + gt1 * mixed
        h2 = (rms_norm(x, norm2_w[l]) * (1 + sc2) + sh2).reshape(b * s, d)
        idx, wts = route(h2, router_w[l], router_bias[l])
        ffn = swiglu(h2, shared_w_gate[l], shared_w_up[l], shared_w_down[l]) + routed_experts(
            h2, idx, wts, exp_w_gate[l], exp_w_up[l], exp_w_down[l])
        x = x + gt2 * ffn.reshape(b, s, d)
    return x
```

```python
import functools

import numpy as np
import jax
import jax.numpy as jnp
from jax import lax
from jax.experimental import pallas as pl
from jax.experimental.pallas import tpu as pltpu

F32 = jnp.float32
BF16 = jnp.bfloat16
I32 = jnp.int32

D_MODEL = 1024
NSA_HEADS = 8
HEAD_DIM = 64
NSA_WIDTH = NSA_HEADS * HEAD_DIM
KV_HEADS = 2
HEADS_PER_KV = NSA_HEADS // KV_HEADS
KV_WIDTH = KV_HEADS * HEAD_DIM
CMP_BLOCK = 32
CMP_STRIDE = 16
CMP_HIDDEN = 256
SEL_BLOCK = 64
N_SELECT = 16
WINDOW = 512
HGRN_HEADS = 4
HGRN_DIM = 128
HGRN_WIDTH = HGRN_HEADS * HGRN_DIM
HGRN_CHUNK = 64
HGRN_SUB = 16
N_EXPERTS = 256
TOP_K = 8
N_GROUPS = 8
GROUP_SIZE = N_EXPERTS // N_GROUPS
TOPK_GROUPS = 4
EXPERT_FF = 256
SHARED_FF = 256
ROUTED_SCALE = 2.5
RMS_EPS = 1e-6
BIG = 1e9
GATE_PAD = 128
PROJ_COLS = NSA_WIDTH + 6 * KV_WIDTH + GATE_PAD + 4 * HGRN_WIDTH

VMEM_LIMIT = 56 * 1024 * 1024

TQ = 128
TK = 512
EXPERT_BLOCK = 256
HIGHEST = lax.Precision.HIGHEST


def _cparams(*sem):
    return pltpu.CompilerParams(dimension_semantics=sem, vmem_limit_bytes=VMEM_LIMIT)


def _sigmoid(x):
    return 1.0 / (1.0 + jnp.exp(-x))


def _dot_nt(a, b):
    return lax.dot_general(a, b, (((1,), (1,)), ((), ())), preferred_element_type=F32)


def _dot(a, b, **kw):
    return jnp.dot(a, b, preferred_element_type=F32, **kw)


def _split_dot(a_bf16_exact, x):
    hi = x.astype(BF16)
    lo = (x - hi.astype(F32)).astype(BF16)
    return _dot(a_bf16_exact, hi) + _dot(a_bf16_exact, lo)


def _mod_kernel(c_ref, w_ref, b_ref, o_ref):
    c = c_ref[...]
    cond = c * _sigmoid(c)
    o_ref[...] = _dot(cond, w_ref[...], precision=HIGHEST) + b_ref[...]


def _mod(c, ada_w, ada_b):
    b, d = c.shape
    rows = 8
    c_pad = jnp.zeros((rows, d), F32).at[:b].set(c)
    n = ada_w.shape[1]
    out = pl.pallas_call(
        _mod_kernel,
        grid=(n // d,),
        in_specs=[pl.BlockSpec((rows, d), lambda j: (0, 0)),
                  pl.BlockSpec((d, d), lambda j: (0, j)),
                  pl.BlockSpec((1, d), lambda j: (0, j))],
        out_specs=pl.BlockSpec((rows, d), lambda j: (0, j)),
        out_shape=jax.ShapeDtypeStruct((rows, n), F32),
        compiler_params=_cparams("parallel"),
        name="mod",
    )(c_pad, ada_w, ada_b.reshape(1, n))
    return out[:b]


def _head_rms(t, w):
    return t * lax.rsqrt(jnp.mean(t * t, axis=-1, keepdims=True) + RMS_EPS) * w


def _inproj_kernel(x_ref, sc_ref, sh_ref, n1_ref, w_ref, qnw_ref, knw_ref, lbp_ref,
                   q_ref, kcr_ref, vcr_ref, ks_ref, vst_ref, kw_ref, vwt_ref, gt_ref,
                   hq_ref, hk_ref, hlf_ref, hv_ref, hg_ref):
    x = x_ref[0]
    ms = jnp.mean(x * x, axis=-1, keepdims=True)
    h = x * lax.rsqrt(ms + RMS_EPS) * n1_ref[...] * (1.0 + sc_ref[0]) + sh_ref[0]
    p = _dot(h.astype(BF16), w_ref[...])

    qnw = qnw_ref[...]
    for hd in range(NSA_HEADS):
        t = p[:, hd * HEAD_DIM:(hd + 1) * HEAD_DIM]
        q_ref[0, hd] = (_head_rms(t, qnw) * (HEAD_DIM ** -0.5)).astype(BF16)

    o = NSA_WIDTH
    kcr_ref[0] = p[:, o:o + KV_WIDTH]
    vcr_ref[0] = p[:, o + KV_WIDTH:o + 2 * KV_WIDTH]
    ks = p[:, o + 2 * KV_WIDTH:o + 3 * KV_WIDTH]
    vs = p[:, o + 3 * KV_WIDTH:o + 4 * KV_WIDTH]
    kw = p[:, o + 4 * KV_WIDTH:o + 5 * KV_WIDTH]
    vw = p[:, o + 5 * KV_WIDTH:o + 6 * KV_WIDTH]
    for g in range(KV_HEADS):
        sl = slice(g * HEAD_DIM, (g + 1) * HEAD_DIM)
        ks_ref[0, g] = _head_rms(ks[:, sl], knw_ref[1:2, :]).astype(BF16)
        kw_ref[0, g] = _head_rms(kw[:, sl], knw_ref[2:3, :]).astype(BF16)
    vst = vs.T.astype(BF16)
    vwt = vw.T.astype(BF16)
    for g in range(KV_HEADS):
        vst_ref[0, g] = vst[g * HEAD_DIM:(g + 1) * HEAD_DIM, :]
        vwt_ref[0, g] = vwt[g * HEAD_DIM:(g + 1) * HEAD_DIM, :]

    o = NSA_WIDTH + 6 * KV_WIDTH
    gates = _sigmoid(p[:, o:o + GATE_PAD])
    gt_ref[0] = gates.T[:NSA_HEADS * 3, :]

    o = o + GATE_PAD
    hq = p[:, o:o + HGRN_WIDTH]
    hf = p[:, o + HGRN_WIDTH:o + 2 * HGRN_WIDTH]
    hi = p[:, o + 2 * HGRN_WIDTH:o + 3 * HGRN_WIDTH]
    hg = p[:, o + 3 * HGRN_WIDTH:o + 4 * HGRN_WIDTH]
    lbp = lbp_ref[...]
    e = jnp.exp(lbp - jnp.max(lbp, axis=0, keepdims=True))
    lb = e[0:1, :] / jnp.sum(e, axis=0, keepdims=True)
    f = lb + (1.0 - lb) * _sigmoid(hf)
    hq_ref[0] = hq * _sigmoid(hq) * (HGRN_DIM ** -0.5)
    hk_ref[0] = 1.0 - f
    hlf_ref[0] = jnp.log(f)
    hv_ref[0] = hi
    hg_ref[0] = _sigmoid(hg)


def _inproj(x, sc1, sh1, norm1_w, w_cat, q_norm_w, k_norm_w, lb_param, tm):
    b, s, d = x.shape
    row = lambda bi, i: (bi, i, 0)
    per_b = lambda bi, i: (bi, 0, 0)
    fixed2 = lambda bi, i: (0, 0)
    out_shape = (
        jax.ShapeDtypeStruct((b, NSA_HEADS, s, HEAD_DIM), BF16),
        jax.ShapeDtypeStruct((b, s, KV_WIDTH), F32),
        jax.ShapeDtypeStruct((b, s, KV_WIDTH), F32),
        jax.ShapeDtypeStruct((b, KV_HEADS, s, HEAD_DIM), BF16),
        jax.ShapeDtypeStruct((b, KV_HEADS, HEAD_DIM, s), BF16),
        jax.ShapeDtypeStruct((b, KV_HEADS, s, HEAD_DIM), BF16),
        jax.ShapeDtypeStruct((b, KV_HEADS, HEAD_DIM, s), BF16),
        jax.ShapeDtypeStruct((b, NSA_HEADS * 3, s), F32),
    ) + tuple(jax.ShapeDtypeStruct((b, s, HGRN_WIDTH), F32) for _ in range(5))
    hm = lambda n, w: pl.BlockSpec((1, n, tm, w), lambda bi, i: (bi, 0, i, 0))
    hmt = lambda n, w: pl.BlockSpec((1, n, w, tm), lambda bi, i: (bi, 0, 0, i))
    out_specs = (
        hm(NSA_HEADS, HEAD_DIM),
        pl.BlockSpec((1, tm, KV_WIDTH), row),
        pl.BlockSpec((1, tm, KV_WIDTH), row),
        hm(KV_HEADS, HEAD_DIM), hmt(KV_HEADS, HEAD_DIM),
        hm(KV_HEADS, HEAD_DIM), hmt(KV_HEADS, HEAD_DIM),
        pl.BlockSpec((1, NSA_HEADS * 3, tm), lambda bi, i: (bi, 0, i)),
    ) + tuple(pl.BlockSpec((1, tm, HGRN_WIDTH), row) for _ in range(5))
    return pl.pallas_call(
        _inproj_kernel,
        grid=(b, s // tm),
        in_specs=[pl.BlockSpec((1, tm, d), row),
                  pl.BlockSpec((1, 1, d), per_b),
                  pl.BlockSpec((1, 1, d), per_b),
                  pl.BlockSpec((1, d), fixed2),
                  pl.BlockSpec((d, PROJ_COLS), fixed2),
                  pl.BlockSpec((1, HEAD_DIM), fixed2),
                  pl.BlockSpec((3, HEAD_DIM), fixed2),
                  pl.BlockSpec(lb_param.shape, fixed2)],
        out_specs=out_specs,
        out_shape=out_shape,
        compiler_params=_cparams("parallel", "parallel"),
        name="inproj",
    )(x, sc1, sh1, norm1_w, w_cat, q_norm_w, k_norm_w, lb_param)


def _gelu_tanh(x):
    return 0.5 * x * (1.0 + jnp.tanh(0.7978845608028654 * (x + 0.044715 * x * x * x)))


def _compress_kernel(kch_ref, vch_ref, pos_ref, wa_ref, wb_ref, b1_ref, w2_ref, knw_ref,
                     kc_ref, vct_ref):
    n = kch_ref.shape[1]
    outs = []
    for br, ch_ref in enumerate((kch_ref, vch_ref)):
        ch = ch_ref[0]
        a = _dot((ch + pos_ref[br, 0:1, :]).astype(BF16), wa_ref[br])
        bm = _dot((ch + pos_ref[br, 1:2, :]).astype(BF16), wb_ref[br])
        pre = a + pltpu.roll(bm, n - 1, 0) + b1_ref[br]
        hid = _gelu_tanh(pre).astype(BF16)
        outs.append([_dot(hid[:, g * CMP_HIDDEN:(g + 1) * CMP_HIDDEN], w2_ref[br]) for g in range(KV_HEADS)])
    for g in range(KV_HEADS):
        kc_ref[0, g] = _head_rms(outs[0][g], knw_ref[0:1, :]).astype(BF16)
    vct = jnp.concatenate(outs[1], axis=1).T.astype(BF16)
    for g in range(KV_HEADS):
        vct_ref[0, g] = vct[g * HEAD_DIM:(g + 1) * HEAD_DIM, :]


def _compress(kc_raw, vc_raw, cmp_pos, cmp_w1, cmp_b1, cmp_w2, k_norm_w):
    b, s, _ = kc_raw.shape
    n = s // CMP_STRIDE
    half = CMP_STRIDE
    cw = CMP_STRIDE * KV_WIDTH
    kch = kc_raw.reshape(b, n, cw)
    vch = vc_raw.reshape(b, n, cw)
    pos = cmp_pos.reshape(2, 2, half, 1, HEAD_DIM)
    pos = jnp.broadcast_to(pos, (2, 2, half, KV_HEADS, HEAD_DIM)).reshape(2, 2, cw)
    w1 = cmp_w1.reshape(2, 2, half, HEAD_DIM, CMP_HIDDEN)
    eye = jnp.eye(KV_HEADS, dtype=F32)
    wfull = jnp.einsum('rhjdn,gk->rhjgdkn', w1, eye).reshape(2, 2, cw, KV_HEADS * CMP_HIDDEN).astype(BF16)
    b1 = jnp.tile(cmp_b1.reshape(2, 1, CMP_HIDDEN), (1, 1, KV_HEADS))
    fix = lambda r: (lambda bi: (0,) * r)
    return pl.pallas_call(
        _compress_kernel,
        grid=(b,),
        in_specs=[pl.BlockSpec((1, n, cw), lambda bi: (bi, 0, 0)),
                  pl.BlockSpec((1, n, cw), lambda bi: (bi, 0, 0)),
                  pl.BlockSpec((2, 2, cw), fix(3)),
                  pl.BlockSpec((2, cw, KV_HEADS * CMP_HIDDEN), fix(3)),
                  pl.BlockSpec((2, cw, KV_HEADS * CMP_HIDDEN), fix(3)),
                  pl.BlockSpec((2, 1, KV_HEADS * CMP_HIDDEN), fix(3)),
                  pl.BlockSpec((2, CMP_HIDDEN, HEAD_DIM), fix(3)),
                  pl.BlockSpec((3, HEAD_DIM), fix(2))],
        out_specs=(pl.BlockSpec((1, KV_HEADS, n, HEAD_DIM), lambda bi: (bi, 0, 0, 0)),
                   pl.BlockSpec((1, KV_HEADS, HEAD_DIM, n), lambda bi: (bi, 0, 0, 0))),
        out_shape=(jax.ShapeDtypeStruct((b, KV_HEADS, n, HEAD_DIM), BF16),
                   jax.ShapeDtypeStruct((b, KV_HEADS, HEAD_DIM, n), BF16)),
        compiler_params=_cparams("parallel"),
        name="compress",
    )(kch, vch, pos, wfull[:, 0], wfull[:, 1], b1, cmp_w2.astype(BF16), k_norm_w)


def _nsa_kernel(q_ref, kc_ref, vct_ref, ks_ref, vst_ref, kw_ref, vwt_ref, gt_ref, slope_ref,
                ovl_ref, onw_ref, o_ref, sel_scr, *, n_top):
    qi = pl.program_id(2)
    q0 = qi * TQ
    ncols = HEADS_PER_KV * TQ
    q = q_ref[0].reshape(ncols, HEAD_DIM)
    slope = slope_ref[0]
    nc = kc_ref.shape[2]
    ns = ovl_ref.shape[0]

    def col_t(rows):
        return q0 + (lax.broadcasted_iota(I32, (rows, ncols), 1) & (TQ - 1))

    s = _dot_nt(kc_ref[0, 0], q)
    dist = col_t(nc) - (lax.broadcasted_iota(I32, (nc, ncols), 0) * CMP_STRIDE + (CMP_BLOCK - 1))
    s = jnp.where(dist >= 0, s - slope * dist.astype(F32), -jnp.inf)
    m = jnp.max(s, axis=0, keepdims=True)
    m = jnp.where(m == -jnp.inf, 0.0, m)
    e = jnp.exp(s - m)
    p = e / jnp.maximum(jnp.sum(e, axis=0, keepdims=True), 1e-30)
    o_c = _dot(vct_ref[0, 0], p.astype(BF16))

    psum = p[:, 0:TQ]
    for hh in range(1, HEADS_PER_KV):
        psum = psum + p[:, hh * TQ:(hh + 1) * TQ]
    imp = _split_dot(ovl_ref[...], psum)
    blk = lax.broadcasted_iota(I32, (ns, TQ), 0)
    tq = q0 + lax.broadcasted_iota(I32, (ns, TQ), 1)
    cur = tq >> 6
    forced = (blk == 0) | (blk == cur) | (blk == cur - 1)
    rank = jnp.where(forced, BIG, jnp.where(blk * SEL_BLOCK <= tq, imp, -BIG))

    blkf = blk.astype(F32)

    def topk_body(_, carry):
        r, sel = carry
        mx = jnp.max(r, axis=0, keepdims=True)
        first = jnp.min(jnp.where(r == mx, blkf, float(ns)), axis=0, keepdims=True)
        hit = blkf == first
        return jnp.where(hit, -jnp.inf, r), jnp.where(hit, 1.0, sel)

    _, sel = lax.fori_loop(0, n_top, topk_body, (rank, jnp.zeros((ns, TQ), F32)))
    sel_scr[...] = sel

    tcol = col_t(TK)
    krow = lax.broadcasted_iota(I32, (TK, ncols), 0)
    blocks_per_tile = TK // SEL_BLOCK

    def sel_body(j, carry):
        m_run, l_run, acc = carry
        k0 = pl.multiple_of(j * TK, TK)
        sc = _dot_nt(ks_ref[0, 0, pl.ds(k0, TK), :], q)
        dist = tcol - (k0 + krow)
        rows = sel_scr[pl.ds(pl.multiple_of(j * blocks_per_tile, blocks_per_tile), blocks_per_tile), :]
        m128 = jnp.concatenate(
            [jnp.broadcast_to(rows[bb:bb + 1, :], (SEL_BLOCK, TQ)) for bb in range(blocks_per_tile)], axis=0)
        mask = (jnp.concatenate([m128] * HEADS_PER_KV, axis=1) > 0.5) & (dist >= 0)
        sc = jnp.where(mask, sc - slope * dist.astype(F32), -1e30)
        m_new = jnp.maximum(m_run, jnp.max(sc, axis=0, keepdims=True))
        alpha = jnp.exp(m_run - m_new)
        ex = jnp.exp(sc - m_new)
        l_new = alpha * l_run + jnp.sum(ex, axis=0, keepdims=True)
        acc = alpha * acc + _dot(vst_ref[0, 0, :, pl.ds(k0, TK)], ex.astype(BF16))
        return m_new, l_new, acc

    n_tiles = (q0 + TQ + TK - 1) // TK
    init = (jnp.full((1, ncols), -1e30, F32), jnp.zeros((1, ncols), F32), jnp.zeros((HEAD_DIM, ncols), F32))
    _, l_s, acc_s = lax.fori_loop(0, n_tiles, sel_body, init)
    o_s = acc_s / l_s

    nw = WINDOW + TQ
    start = pl.multiple_of(jnp.maximum(q0 - WINDOW, 0), TQ)
    sw = _dot_nt(kw_ref[0, 0, pl.ds(start, nw), :], q)
    dist = col_t(nw) - (start + lax.broadcasted_iota(I32, (nw, ncols), 0))
    sw = jnp.where((dist >= 0) & (dist < WINDOW), sw - slope * dist.astype(F32), -jnp.inf)
    ew = jnp.exp(sw - jnp.max(sw, axis=0, keepdims=True))
    o_w = _dot(vwt_ref[0, 0, :, pl.ds(start, nw)], ew.astype(BF16)) / jnp.sum(ew, axis=0, keepdims=True)

    gt = gt_ref[0, 0]
    outs = []
    for hh in range(HEADS_PER_KV):
        cs = slice(hh * TQ, (hh + 1) * TQ)
        o = (gt[3 * hh:3 * hh + 1, :] * o_c[:, cs] + gt[3 * hh + 1:3 * hh + 2, :] * o_s[:, cs]
             + gt[3 * hh + 2:3 * hh + 3, :] * o_w[:, cs])
        o = o * lax.rsqrt(jnp.mean(o * o, axis=0, keepdims=True) + RMS_EPS) * onw_ref[0, hh]
        outs.append(o)
    o_ref[0] = jnp.concatenate(outs, axis=0).T


def _nsa(q, kc, vct, ks, vst, kw, vwt, gates_t, attn_out_norm_w):
    b, _, s, _ = q.shape
    nc = kc.shape[2]
    ns = s // SEL_BLOCK
    n_top = min(N_SELECT, ns)
    ncols = HEADS_PER_KV * TQ
    slopes = np.array([2.0 ** (-8.0 * (i + 1) / NSA_HEADS) for i in range(NSA_HEADS)], np.float32)
    slope_cols = jnp.asarray(np.repeat(slopes.reshape(KV_HEADS, HEADS_PER_KV, 1), TQ, axis=2).reshape(KV_HEADS, 1, ncols))
    ci = np.arange(nc)[None, :] * CMP_STRIDE
    bj = np.arange(ns)[:, None]
    ovl = ((ci < (bj + 1) * SEL_BLOCK) & (ci + CMP_BLOCK > bj * SEL_BLOCK) & (np.arange(nc)[None, :] < nc - 1))
    ovl = jnp.asarray(ovl.astype(np.float32)).astype(BF16)
    onw = jnp.broadcast_to(attn_out_norm_w.reshape(KV_HEADS, HEADS_PER_KV, HEAD_DIM, 1),
                           (KV_HEADS, HEADS_PER_KV, HEAD_DIM, TQ))
    gt = gates_t.reshape(b, KV_HEADS, HEADS_PER_KV * 3, s)
    per_bg = lambda bi, g, i: (bi, g, 0, 0)
    return pl.pallas_call(
        functools.partial(_nsa_kernel, n_top=n_top),
        grid=(b, KV_HEADS, s // TQ),
        in_specs=[pl.BlockSpec((1, HEADS_PER_KV, TQ, HEAD_DIM), lambda bi, g, i: (bi, g, i, 0)),
                  pl.BlockSpec((1, 1, nc, HEAD_DIM), per_bg),
                  pl.BlockSpec((1, 1, HEAD_DIM, nc), per_bg),
                  pl.BlockSpec((1, 1, s, HEAD_DIM), per_bg),
                  pl.BlockSpec((1, 1, HEAD_DIM, s), per_bg),
                  pl.BlockSpec((1, 1, s, HEAD_DIM), per_bg),
                  pl.BlockSpec((1, 1, HEAD_DIM, s), per_bg),
                  pl.BlockSpec((1, 1, HEADS_PER_KV * 3, TQ), lambda bi, g, i: (bi, g, 0, i)),
                  pl.BlockSpec((1, 1, ncols), lambda bi, g, i: (g, 0, 0)),
                  pl.BlockSpec((ns, nc), lambda bi, g, i: (0, 0)),
                  pl.BlockSpec((1, HEADS_PER_KV, HEAD_DIM, TQ), lambda bi, g, i: (g, 0, 0, 0))],
        out_specs=pl.BlockSpec((1, TQ, HEADS_PER_KV * HEAD_DIM), lambda bi, g, i: (bi, i, g)),
        out_shape=jax.ShapeDtypeStruct((b, s, NSA_WIDTH), F32),
        scratch_shapes=[pltpu.VMEM((ns, TQ), F32)],
        compiler_params=_cparams("parallel", "parallel", "arbitrary"),
        name="nsa",
    )(q, kc, vct, ks, vst, kw, vwt, gt, slope_cols, ovl, onw)


def _hgrn_kernel(q_ref, k_ref, lf_ref, v_ref, g_ref, onw_ref, o_ref, state_scr, *, n_chunks):
    c = HGRN_CHUNK

    @pl.when(pl.program_id(2) == 0)
    def _():
        state_scr[...] = jnp.zeros_like(state_scr)

    ri = lax.broadcasted_iota(I32, (c, c), 0)
    ci = lax.broadcasted_iota(I32, (c, c), 1)
    tril = (ri >= ci).astype(F32)
    rsub = ri // HGRN_SUB
    rin = ri & (HGRN_SUB - 1)

    def chunk(ck, state_t):
        r0 = pl.multiple_of(ck * c, c)
        q = q_ref[0, pl.ds(r0, c), :]
        k = k_ref[0, pl.ds(r0, c), :]
        lf = lf_ref[0, pl.ds(r0, c), :]
        v = v_ref[0, pl.ds(r0, c), :]
        cum = _dot(tril, lf, precision=HIGHEST)
        o = _dot_nt((q * jnp.exp(cum)).astype(BF16), state_t.astype(BF16))
        scores = jnp.zeros((c, c), F32)
        for i in range(1, c // HGRN_SUB):
            ref_row = cum[i * HGRN_SUB - 1:i * HGRN_SUB, :]
            qs = q * jnp.exp(jnp.minimum(cum - ref_row, 0.0))
            kd = k * jnp.exp(jnp.minimum(ref_row - cum, 0.0))
            blk = _dot_nt(qs.astype(BF16), kd.astype(BF16))
            scores = jnp.where((rsub == i) & (ci < i * HGRN_SUB), blk, scores)
        for d in range(HGRN_SUB):
            if d == 0:
                w = jnp.sum(q * k, axis=-1, keepdims=True)
            else:
                ksh = pltpu.roll(k, d, 0)
                csh = pltpu.roll(cum, d, 0)
                w = jnp.sum(q * ksh * jnp.exp(jnp.minimum(cum - csh, 0.0)), axis=-1, keepdims=True)
            scores = jnp.where((ri - ci == d) & (rin >= d), w, scores)
        o = o + _dot(scores.astype(BF16), v.astype(BF16))
        last = cum[c - 1:c, :]
        kd = (k * jnp.exp(last - cum)).astype(BF16)
        state_t = state_t * jnp.exp(last) + _dot(v.T.astype(BF16), kd)
        o = o * g_ref[0, pl.ds(r0, c), :]
        o = o * lax.rsqrt(jnp.mean(o * o, axis=-1, keepdims=True) + RMS_EPS) * onw_ref[0]
        o_ref[0, pl.ds(r0, c), :] = o
        return state_t

    state_scr[...] = lax.fori_loop(0, n_chunks, chunk, state_scr[...])


def _hgrn(hq, hk, hlf, hv, hg, rec_out_norm_w, rows):
    b, s, _ = hq.shape
    blk = pl.BlockSpec((1, rows, HGRN_DIM), lambda bi, h, i: (bi, i, h))
    return pl.pallas_call(
        functools.partial(_hgrn_kernel, n_chunks=rows // HGRN_CHUNK),
        grid=(b, HGRN_HEADS, s // rows),
        in_specs=[blk, blk, blk, blk, blk,
                  pl.BlockSpec((1, 1, HGRN_DIM), lambda bi, h, i: (h, 0, 0))],
        out_specs=blk,
        out_shape=jax.ShapeDtypeStruct((b, s, HGRN_WIDTH), F32),
        scratch_shapes=[pltpu.VMEM((HGRN_DIM, HGRN_DIM), F32)],
        compiler_params=_cparams("parallel", "parallel", "arbitrary"),
        name="hgrn",
    )(hq, hk, hlf, hv, hg, rec_out_norm_w.reshape(HGRN_HEADS, 1, HGRN_DIM))


def _outproj_kernel(x_ref, a_ref, r_ref, wa_ref, wr_ref, gt_ref, sc_ref, sh_ref, n2_ref, x1_ref, h2_ref):
    mixed = _dot(a_ref[0].astype(BF16), wa_ref[...]) + _dot(r_ref[0].astype(BF16), wr_ref[...])
    x1 = x_ref[0] + gt_ref[0] * mixed
    x1_ref[0] = x1
    ms = jnp.mean(x1 * x1, axis=-1, keepdims=True)
    h2_ref[0] = x1 * lax.rsqrt(ms + RMS_EPS) * n2_ref[...] * (1.0 + sc_ref[0]) + sh_ref[0]


def _outproj(x, attn, rec, w_out, gt1, sc2, sh2, norm2_w, tm):
    b, s, d = x.shape
    row = lambda bi, i: (bi, i, 0)
    per_b = lambda bi, i: (bi, 0, 0)
    fixed2 = lambda bi, i: (0, 0)
    w = w_out.astype(BF16)
    return pl.pallas_call(
        _outproj_kernel,
        grid=(b, s // tm),
        in_specs=[pl.BlockSpec((1, tm, d), row),
                  pl.BlockSpec((1, tm, NSA_WIDTH), row),
                  pl.BlockSpec((1, tm, HGRN_WIDTH), row),
                  pl.BlockSpec((NSA_WIDTH, d), fixed2),
                  pl.BlockSpec((HGRN_WIDTH, d), fixed2),
                  pl.BlockSpec((1, 1, d), per_b),
                  pl.BlockSpec((1, 1, d), per_b),
                  pl.BlockSpec((1, 1, d), per_b),
                  pl.BlockSpec((1, d), fixed2)],
        out_specs=(pl.BlockSpec((1, tm, d), row), pl.BlockSpec((1, tm, d), row)),
        out_shape=(jax.ShapeDtypeStruct((b, s, d), F32), jax.ShapeDtypeStruct((b, s, d), F32)),
        compiler_params=_cparams("parallel", "parallel"),
        name="outproj",
    )(x, attn, rec, w[:NSA_WIDTH], w[NSA_WIDTH:], gt1, sc2, sh2, norm2_w)


def _mixer(x, c, ada_w, ada_b, norm1_w, norm2_w, w_in, q_norm_w, k_norm_w, cmp_pos, cmp_w1, cmp_b1, cmp_w2,
           attn_out_norm_w, hgrn_lb_param, rec_out_norm_w, w_out):
    b, s, d = x.shape
    mod = _mod(c, ada_w, ada_b)
    sh1, sc1, gt1, sh2, sc2, gt2 = [m.reshape(b, 1, d) for m in jnp.split(mod, 6, axis=-1)]
    o = NSA_WIDTH + 6 * KV_WIDTH
    w_cat = jnp.concatenate([w_in[:, :o], w_in[:, o:o + NSA_HEADS * 3],
                             jnp.zeros((d, GATE_PAD - NSA_HEADS * 3), w_in.dtype),
                             w_in[:, o + NSA_HEADS * 3:]], axis=1).astype(BF16)
    tm = min(256, s)
    (q, kc_raw, vc_raw, ks, vst, kw, vwt, gates_t, hq, hk, hlf, hv, hg) = _inproj(
        x, sc1, sh1, norm1_w.reshape(1, d), w_cat, q_norm_w.reshape(1, HEAD_DIM), k_norm_w, hgrn_lb_param, tm)
    kc, vct = _compress(kc_raw, vc_raw, cmp_pos, cmp_w1, cmp_b1, cmp_w2, k_norm_w)
    attn = _nsa(q, kc, vct, ks, vst, kw, vwt, gates_t, attn_out_norm_w)
    rec = _hgrn(hq, hk, hlf, hv, hg, rec_out_norm_w, min(512, s))
    x1, h2 = _outproj(x, attn, rec, w_out, gt1, sc2, sh2, norm2_w.reshape(1, d), tm)
    return x1, h2, gt2


def _router_kernel(h_ref, rwt_ref, bias_ref, tri_ref, ones_ref, idx_ref, w_ref, rank_ref, cnt_ref, carry_scr, *, tr):
    @pl.when(pl.program_id(0) == 0)
    def _():
        carry_scr[...] = jnp.zeros_like(carry_scr)

    h = h_ref[...]
    h_hi = h.astype(BF16)
    h_lo = (h - h_hi.astype(F32)).astype(BF16)
    logits = _dot_nt(rwt_ref[0], h_hi) + _dot_nt(rwt_ref[1], h_hi) + _dot_nt(rwt_ref[0], h_lo)
    scores = _sigmoid(logits)
    biased = scores + bias_ref[...]
    neg = -jnp.inf

    gs = []
    for g in range(N_GROUPS):
        sub = biased[g * GROUP_SIZE:(g + 1) * GROUP_SIZE, :]
        m1 = jnp.max(sub, axis=0, keepdims=True)
        dup = jnp.sum((sub == m1).astype(F32), axis=0, keepdims=True)
        m2 = jnp.max(jnp.where(sub < m1, sub, neg), axis=0, keepdims=True)
        gs.append(m1 + jnp.where(dup >= 2.0, m1, m2))
    parts = []
    for g in range(N_GROUPS):
        beaten = jnp.zeros_like(gs[g])
        for g2 in range(N_GROUPS):
            if g2 != g:
                beats = (gs[g2] >= gs[g]) if g2 < g else (gs[g2] > gs[g])
                beaten = beaten + beats.astype(F32)
        sub = biased[g * GROUP_SIZE:(g + 1) * GROUP_SIZE, :]
        parts.append(jnp.where(beaten < float(TOPK_GROUPS), sub, neg))
    cand = jnp.concatenate(parts, axis=0)

    rowf = lax.broadcasted_iota(I32, (N_EXPERTS, tr), 0).astype(F32)
    idx_rows, w_rows, hits = [], [], []
    multi = jnp.zeros((N_EXPERTS, tr), F32)
    for _ in range(TOP_K):
        mx = jnp.max(cand, axis=0, keepdims=True)
        first = jnp.min(jnp.where(cand == mx, rowf, float(N_EXPERTS)), axis=0, keepdims=True)
        hit = rowf == first
        idx_rows.append(first)
        w_rows.append(jnp.sum(jnp.where(hit, scores, 0.0), axis=0, keepdims=True))
        cand = jnp.where(hit, neg, cand)
        multi = jnp.where(hit, 1.0, multi)
    w = jnp.concatenate(w_rows, axis=0)
    w_ref[...] = w / jnp.sum(w, axis=0, keepdims=True) * ROUTED_SCALE
    idx = jnp.concatenate(idx_rows, axis=0)
    idx_ref[...] = idx.astype(I32)

    carry = carry_scr[...]
    mb = multi.astype(BF16)
    before = _dot(mb, tri_ref[...]) + jnp.concatenate([carry] * (tr // 128), axis=1)
    rank_rows = [jnp.sum(jnp.where(rowf == idx_rows[k], before, 0.0), axis=0, keepdims=True) for k in range(TOP_K)]
    rank_ref[...] = jnp.concatenate(rank_rows, axis=0).astype(I32)
    carry = carry + _dot(mb, ones_ref[...])
    carry_scr[...] = carry
    cnt_ref[...] = carry


def _router(h2, router_w, router_bias, tr):
    t, d = h2.shape
    tri = jnp.asarray(np.triu(np.ones((tr, tr), np.float32), 1)).astype(BF16)
    ones = jnp.ones((tr, 128), BF16)
    tok = pl.BlockSpec((TOP_K, tr), lambda i: (0, i))
    fixed = lambda i: (0, 0)
    rwt = router_w.T
    rwt_hi = rwt.astype(BF16)
    rwt_split = jnp.stack([rwt_hi, (rwt - rwt_hi.astype(F32)).astype(BF16)])
    return pl.pallas_call(
        functools.partial(_router_kernel, tr=tr),
        grid=(t // tr,),
        in_specs=[pl.BlockSpec((tr, d), lambda i: (i, 0)),
                  pl.BlockSpec((2, N_EXPERTS, d), lambda i: (0, 0, 0)),
                  pl.BlockSpec((N_EXPERTS, 1), fixed),
                  pl.BlockSpec((tr, tr), fixed),
                  pl.BlockSpec((tr, 128), fixed)],
        out_specs=(tok, tok, tok, pl.BlockSpec((N_EXPERTS, 128), fixed)),
        out_shape=(jax.ShapeDtypeStruct((TOP_K, t), I32), jax.ShapeDtypeStruct((TOP_K, t), F32),
                   jax.ShapeDtypeStruct((TOP_K, t), I32), jax.ShapeDtypeStruct((N_EXPERTS, 128), F32)),
        scratch_shapes=[pltpu.VMEM((N_EXPERTS, 128), F32)],
        compiler_params=_cparams("arbitrary"),
        name="router",
    )(h2, rwt_split, router_bias.reshape(N_EXPERTS, 1), tri, ones)


def _row_copy(src, s_row, dst, d_row, sem):
    return pltpu.make_async_copy(src.at[pl.ds(s_row, 1)], dst.at[pl.ds(d_row, 1)], sem)


def _dispatch_kernel(dest_ref, h_ref, zeros_ref, xs_ref, sem, *, td):
    del zeros_ref
    t0 = pl.program_id(0) * td

    def issue(t, _):
        for k in range(TOP_K):
            _row_copy(h_ref, t0 + t, xs_ref, dest_ref[k, t], sem).start()
        return 0

    lax.fori_loop(0, td, issue, 0)

    def drain(t, _):
        for k in range(TOP_K):
            _row_copy(h_ref, 0, xs_ref, 0, sem).wait()
        return 0

    lax.fori_loop(0, td, drain, 0)


def _dispatch(h2, dest, n_rows, td):
    t, d = h2.shape
    zeros = jnp.zeros((n_rows, d), h2.dtype)
    return pl.pallas_call(
        functools.partial(_dispatch_kernel, td=td),
        grid=(t // td,),
        in_specs=[pl.BlockSpec((TOP_K, td), lambda i: (0, i), memory_space=pltpu.SMEM),
                  pl.BlockSpec(memory_space=pl.ANY),
                  pl.BlockSpec(memory_space=pl.ANY)],
        out_specs=pl.BlockSpec(memory_space=pl.ANY),
        out_shape=jax.ShapeDtypeStruct((n_rows, d), h2.dtype),
        scratch_shapes=[pltpu.SemaphoreType.DMA],
        input_output_aliases={2: 0},
        compiler_params=pltpu.CompilerParams(dimension_semantics=("arbitrary",), has_side_effects=True),
        name="dispatch",
    )(dest, h2, zeros)


def _experts_kernel(be_ref, nu_ref, xs_ref, wg_ref, wu_ref, wd_ref, ys_ref):
    i = pl.program_id(0)

    @pl.when(i < nu_ref[0])
    def _():
        xb = xs_ref[...].astype(BF16)
        g = _dot(xb, wg_ref[0].astype(BF16))
        u = _dot(xb, wu_ref[0].astype(BF16))
        act = (g * _sigmoid(g) * u).astype(BF16)
        ys_ref[...] = _dot(act, wd_ref[0].astype(BF16))

    @pl.when(i >= nu_ref[0])
    def _():
        ys_ref[...] = jnp.zeros_like(ys_ref)


def _experts(xs, blk_e, n_used, w_gate, w_up, w_down):
    n_rows, d = xs.shape
    nblk = n_rows // EXPERT_BLOCK
    row_map = lambda i, be, nu: (jnp.minimum(i, nu[0] - 1), 0)
    w_map = lambda i, be, nu: (be[i], 0, 0)
    return pl.pallas_call(
        _experts_kernel,
        grid_spec=pltpu.PrefetchScalarGridSpec(
            num_scalar_prefetch=2,
            grid=(nblk,),
            in_specs=[pl.BlockSpec((EXPERT_BLOCK, d), row_map),
                      pl.BlockSpec((1, d, EXPERT_FF), w_map),
                      pl.BlockSpec((1, d, EXPERT_FF), w_map),
                      pl.BlockSpec((1, EXPERT_FF, d), w_map)],
            out_specs=pl.BlockSpec((EXPERT_BLOCK, d), lambda i, be, nu: (i, 0))),
        out_shape=jax.ShapeDtypeStruct((n_rows, d), F32),
        compiler_params=_cparams("arbitrary"),
        name="experts",
    )(blk_e, n_used, xs, w_gate, w_up, w_down)


def _combine_kernel(dest_ref, x1_ref, h_ref, w_ref, gt_ref, sg_ref, su_ref, sd_ref, ys_ref, o_ref, buf, sem, *, tc):
    def issue(t, _):
        for k in range(TOP_K):
            _row_copy(ys_ref, dest_ref[k, t], buf.at[k], t, sem).start()
        return 0

    lax.fori_loop(0, tc, issue, 0)

    hb = h_ref[...].astype(BF16)
    g = _dot(hb, sg_ref[...])
    u = _dot(hb, su_ref[...])
    ffn = _dot((g * _sigmoid(g) * u).astype(BF16), sd_ref[...])

    def drain(t, _):
        for k in range(TOP_K):
            _row_copy(ys_ref, 0, buf.at[k], 0, sem).wait()
        return 0

    lax.fori_loop(0, tc, drain, 0)

    w = w_ref[...]
    for k in range(TOP_K):
        ffn = ffn + w[:, k:k + 1] * buf[k]
    o_ref[...] = x1_ref[...] + gt_ref[0] * ffn


def _combine(x1, h2, w_tok, gt2, dest, ys, sg, su, sd, seq, tc):
    t, d = x1.shape
    row = lambda i: (i, 0)
    fixed = lambda i: (0, 0)
    return pl.pallas_call(
        functools.partial(_combine_kernel, tc=tc),
        grid=(t // tc,),
        in_specs=[pl.BlockSpec((TOP_K, tc), lambda i: (0, i), memory_space=pltpu.SMEM),
                  pl.BlockSpec((tc, d), row),
                  pl.BlockSpec((tc, d), row),
                  pl.BlockSpec((tc, TOP_K), row),
                  pl.BlockSpec((1, 1, d), lambda i: ((i * tc) // seq, 0, 0)),
                  pl.BlockSpec((d, SHARED_FF), fixed),
                  pl.BlockSpec((d, SHARED_FF), fixed),
                  pl.BlockSpec((SHARED_FF, d), fixed),
                  pl.BlockSpec(memory_space=pl.ANY)],
        out_specs=pl.BlockSpec((tc, d), row),
        out_shape=jax.ShapeDtypeStruct((t, d), F32),
        scratch_shapes=[pltpu.VMEM((TOP_K, tc, d), F32), pltpu.SemaphoreType.DMA],
        compiler_params=_cparams("arbitrary"),
        name="combine",
    )(dest, x1, h2, w_tok, gt2, sg.astype(BF16), su.astype(BF16), sd.astype(BF16), ys)


def _moe_parts(x1, h2, gt2, router_w, router_bias, w_gate, w_up, w_down, sg, su, sd):
    b, s, d = x1.shape
    t = b * s
    h2 = h2.reshape(t, d)
    idx, w, rank, cnt = _router(h2, router_w, router_bias, min(256, t))
    counts = cnt[:, 0].astype(I32)
    padded = (counts + EXPERT_BLOCK - 1) // EXPERT_BLOCK * EXPERT_BLOCK
    pad_end = jnp.cumsum(padded)
    pad_start = pad_end - padded
    dest = pad_start[idx] + rank
    n_rows = t * TOP_K + N_EXPERTS * EXPERT_BLOCK
    nblk = n_rows // EXPERT_BLOCK
    n_used = (pad_end[-1:] // EXPERT_BLOCK).astype(I32)
    blk_e = jnp.searchsorted(pad_end, jnp.arange(nblk, dtype=I32) * EXPERT_BLOCK, side='right')
    blk_e = jnp.minimum(blk_e, N_EXPERTS - 1).astype(I32)
    blk_e = jnp.where(jnp.arange(nblk) < n_used[0], blk_e, blk_e[jnp.maximum(n_used[0] - 1, 0)])
    xs = _dispatch(h2, dest, n_rows, min(1024, t))
    ys = _experts(xs, blk_e, n_used, w_gate, w_up, w_down)
    out = _combine(x1.reshape(t, d), h2, w.T, gt2, dest, ys, sg, su, sd, s, min(128, t))
    return out.reshape(b, s, d), dict(idx=idx, w=w, rank=rank, cnt=cnt, dest=dest)


def kernel(x, c, ada_w, ada_b, norm1_w, norm2_w, w_in, q_norm_w, k_norm_w, cmp_pos, cmp_w1, cmp_b1, cmp_w2, attn_out_norm_w, hgrn_lb_param, rec_out_norm_w, w_out, router_w, router_bias, exp_w_gate, exp_w_up, exp_w_down, shared_w_gate, shared_w_up, shared_w_down):
    assert ada_w.shape[0] == 1, "one layer"
    assert x.shape[0] <= 8 and x.shape[1] % TK == 0 and x.shape[1] >= WINDOW + TQ
    l = 0
    x1, h2, gt2 = _mixer(x, c, ada_w[l], ada_b[l], norm1_w[l], norm2_w[l], w_in[l], q_norm_w[l], k_norm_w[l],
                         cmp_pos[l], cmp_w1[l], cmp_b1[l], cmp_w2[l], attn_out_norm_w[l], hgrn_lb_param,
                         rec_out_norm_w[l], w_out[l])
    out, _ = _moe_parts(x1, h2, gt2, router_w[l], router_bias[l], exp_w_gate[l], exp_w_up[l], exp_w_down[l],
                        shared_w_gate[l], shared_w_up[l], shared_w_down[l])
    return out
```

```python
import functools

import numpy as np
import jax
import jax.numpy as jnp
from jax import lax
from jax.experimental import pallas as pl
from jax.experimental.pallas import tpu as pltpu

F32 = jnp.float32
BF16 = jnp.bfloat16
I32 = jnp.int32

D_MODEL = 1024
NSA_HEADS = 8
HEAD_DIM = 64
NSA_WIDTH = NSA_HEADS * HEAD_DIM
KV_HEADS = 2
HEADS_PER_KV = NSA_HEADS // KV_HEADS
KV_WIDTH = KV_HEADS * HEAD_DIM
CMP_BLOCK = 32
CMP_STRIDE = 16
CMP_HIDDEN = 256
SEL_BLOCK = 64
N_SELECT = 16
WINDOW = 512
HGRN_HEADS = 4
HGRN_DIM = 128
HGRN_WIDTH = HGRN_HEADS * HGRN_DIM
HGRN_CHUNK = 64
HGRN_SUB = 16
N_EXPERTS = 256
TOP_K = 8
N_GROUPS = 8
GROUP_SIZE = N_EXPERTS // N_GROUPS
TOPK_GROUPS = 4
EXPERT_FF = 256
SHARED_FF = 256
ROUTED_SCALE = 2.5
RMS_EPS = 1e-6
BIG = 1e9
GATE_PAD = 128
PROJ_COLS = NSA_WIDTH + 6 * KV_WIDTH + GATE_PAD + 4 * HGRN_WIDTH

VMEM_LIMIT = 56 * 1024 * 1024

TQ = 128
TK = 512
EXPERT_BLOCK = 256
HIGHEST = lax.Precision.HIGHEST


def _cparams(*sem):
    return pltpu.CompilerParams(dimension_semantics=sem, vmem_limit_bytes=VMEM_LIMIT)


def _sigmoid(x):
    return 1.0 / (1.0 + jnp.exp(-x))


def _dot_nt(a, b):
    return lax.dot_general(a, b, (((1,), (1,)), ((), ())), preferred_element_type=F32)


def _dot(a, b, **kw):
    return jnp.dot(a, b, preferred_element_type=F32, **kw)


def _split_dot(a_bf16_exact, x):
    hi = x.astype(BF16)
    lo = (x - hi.astype(F32)).astype(BF16)
    return _dot(a_bf16_exact, hi) + _dot(a_bf16_exact, lo)


def _mod_kernel(c_ref, w_ref, b_ref, o_ref):
    c = c_ref[...]
    cond = c * _sigmoid(c)
    o_ref[...] = _dot(cond, w_ref[...], precision=HIGHEST) + b_ref[...]


def _mod(c, ada_w, ada_b):
    b, d = c.shape
    rows = 8
    c_pad = jnp.zeros((rows, d), F32).at[:b].set(c)
    n = ada_w.shape[1]
    out = pl.pallas_call(
        _mod_kernel,
        grid=(n // d,),
        in_specs=[pl.BlockSpec((rows, d), lambda j: (0, 0)),
                  pl.BlockSpec((d, d), lambda j: (0, j)),
                  pl.BlockSpec((1, d), lambda j: (0, j))],
        out_specs=pl.BlockSpec((rows, d), lambda j: (0, j)),
        out_shape=jax.ShapeDtypeStruct((rows, n), F32),
        compiler_params=_cparams("parallel"),
        name="mod",
    )(c_pad, ada_w, ada_b.reshape(1, n))
    return out[:b]


def _head_rms(t, w):
    return t * lax.rsqrt(jnp.mean(t * t, axis=-1, keepdims=True) + RMS_EPS) * w


def _inproj_kernel(x_ref, sc_ref, sh_ref, n1_ref, w_ref, qnw_ref, knw_ref, lbp_ref,
                   q_ref, kcr_ref, vcr_ref, ks_ref, vst_ref, kw_ref, vwt_ref, gt_ref,
                   hq_ref, hk_ref, hlf_ref, hv_ref, hg_ref):
    x = x_ref[0]
    ms = jnp.mean(x * x, axis=-1, keepdims=True)
    h = x * lax.rsqrt(ms + RMS_EPS) * n1_ref[...] * (1.0 + sc_ref[0]) + sh_ref[0]
    p = _dot(h.astype(BF16), w_ref[...])

    qnw = qnw_ref[...]
    for hd in range(NSA_HEADS):
        t = p[:, hd * HEAD_DIM:(hd + 1) * HEAD_DIM]
        q_ref[0, hd] = (_head_rms(t, qnw) * (HEAD_DIM ** -0.5)).astype(BF16)

    o = NSA_WIDTH
    kcr_ref[0] = p[:, o:o + KV_WIDTH]
    vcr_ref[0] = p[:, o + KV_WIDTH:o + 2 * KV_WIDTH]
    ks = p[:, o + 2 * KV_WIDTH:o + 3 * KV_WIDTH]
    vs = p[:, o + 3 * KV_WIDTH:o + 4 * KV_WIDTH]
    kw = p[:, o + 4 * KV_WIDTH:o + 5 * KV_WIDTH]
    vw = p[:, o + 5 * KV_WIDTH:o + 6 * KV_WIDTH]
    for g in range(KV_HEADS):
        sl = slice(g * HEAD_DIM, (g + 1) * HEAD_DIM)
        ks_ref[0, g] = _head_rms(ks[:, sl], knw_ref[1:2, :]).astype(BF16)
        kw_ref[0, g] = _head_rms(kw[:, sl], knw_ref[2:3, :]).astype(BF16)
    vst = vs.T.astype(BF16)
    vwt = vw.T.astype(BF16)
    for g in range(KV_HEADS):
        vst_ref[0, g] = vst[g * HEAD_DIM:(g + 1) * HEAD_DIM, :]
        vwt_ref[0, g] = vwt[g * HEAD_DIM:(g + 1) * HEAD_DIM, :]

    o = NSA_WIDTH + 6 * KV_WIDTH
    gates = _sigmoid(p[:, o:o + GATE_PAD])
    gt_ref[0] = gates.T[:NSA_HEADS * 3, :]

    o = o + GATE_PAD
    hq = p[:, o:o + HGRN_WIDTH]
    hf = p[:, o + HGRN_WIDTH:o + 2 * HGRN_WIDTH]
    hi = p[:, o + 2 * HGRN_WIDTH:o + 3 * HGRN_WIDTH]
    hg = p[:, o + 3 * HGRN_WIDTH:o + 4 * HGRN_WIDTH]
    lbp = lbp_ref[...]
    e = jnp.exp(lbp - jnp.max(lbp, axis=0, keepdims=True))
    lb = e[0:1, :] / jnp.sum(e, axis=0, keepdims=True)
    f = lb + (1.0 - lb) * _sigmoid(hf)
    hq_ref[0] = hq * _sigmoid(hq) * (HGRN_DIM ** -0.5)
    hk_ref[0] = 1.0 - f
    hlf_ref[0] = jnp.log(f)
    hv_ref[0] = hi
    hg_ref[0] = _sigmoid(hg)


def _inproj(x, sc1, sh1, norm1_w, w_cat, q_norm_w, k_norm_w, lb_param, tm):
    b, s, d = x.shape
    row = lambda bi, i: (bi, i, 0)
    per_b = lambda bi, i: (bi, 0, 0)
    fixed2 = lambda bi, i: (0, 0)
    out_shape = (
        jax.ShapeDtypeStruct((b, NSA_HEADS, s, HEAD_DIM), BF16),
        jax.ShapeDtypeStruct((b, s, KV_WIDTH), F32),
        jax.ShapeDtypeStruct((b, s, KV_WIDTH), F32),
        jax.ShapeDtypeStruct((b, KV_HEADS, s, HEAD_DIM), BF16),
        jax.ShapeDtypeStruct((b, KV_HEADS, HEAD_DIM, s), BF16),
        jax.ShapeDtypeStruct((b, KV_HEADS, s, HEAD_DIM), BF16),
        jax.ShapeDtypeStruct((b, KV_HEADS, HEAD_DIM, s), BF16),
        jax.ShapeDtypeStruct((b, NSA_HEADS * 3, s), F32),
    ) + tuple(jax.ShapeDtypeStruct((b, s, HGRN_WIDTH), F32) for _ in range(5))
    hm = lambda n, w: pl.BlockSpec((1, n, tm, w), lambda bi, i: (bi, 0, i, 0))
    hmt = lambda n, w: pl.BlockSpec((1, n, w, tm), lambda bi, i: (bi, 0, 0, i))
    out_specs = (
        hm(NSA_HEADS, HEAD_DIM),
        pl.BlockSpec((1, tm, KV_WIDTH), row),
        pl.BlockSpec((1, tm, KV_WIDTH), row),
        hm(KV_HEADS, HEAD_DIM), hmt(KV_HEADS, HEAD_DIM),
        hm(KV_HEADS, HEAD_DIM), hmt(KV_HEADS, HEAD_DIM),
        pl.BlockSpec((1, NSA_HEADS * 3, tm), lambda bi, i: (bi, 0, i)),
    ) + tuple(pl.BlockSpec((1, tm, HGRN_WIDTH), row) for _ in range(5))
    return pl.pallas_call(
        _inproj_kernel,
        grid=(b, s // tm),
        in_specs=[pl.BlockSpec((1, tm, d), row),
                  pl.BlockSpec((1, 1, d), per_b),
                  pl.BlockSpec((1, 1, d), per_b),
                  pl.BlockSpec((1, d), fixed2),
                  pl.BlockSpec((d, PROJ_COLS), fixed2),
                  pl.BlockSpec((1, HEAD_DIM), fixed2),
                  pl.BlockSpec((3, HEAD_DIM), fixed2),
                  pl.BlockSpec(lb_param.shape, fixed2)],
        out_specs=out_specs,
        out_shape=out_shape,
        compiler_params=_cparams("parallel", "parallel"),
        name="inproj",
    )(x, sc1, sh1, norm1_w, w_cat, q_norm_w, k_norm_w, lb_param)


def _gelu_tanh(x):
    return 0.5 * x * (1.0 + jnp.tanh(0.7978845608028654 * (x + 0.044715 * x * x * x)))


def _compress_kernel(kch_ref, vch_ref, pos_ref, wa_ref, wb_ref, b1_ref, w2_ref, knw_ref,
                     kc_ref, vct_ref):
    n = kch_ref.shape[1]
    outs = []
    for br, ch_ref in enumerate((kch_ref, vch_ref)):
        ch = ch_ref[0]
        a = _dot((ch + pos_ref[br, 0:1, :]).astype(BF16), wa_ref[br])
        bm = _dot((ch + pos_ref[br, 1:2, :]).astype(BF16), wb_ref[br])
        pre = a + pltpu.roll(bm, n - 1, 0) + b1_ref[br]
        hid = _gelu_tanh(pre).astype(BF16)
        outs.append([_dot(hid[:, g * CMP_HIDDEN:(g + 1) * CMP_HIDDEN], w2_ref[br]) for g in range(KV_HEADS)])
    for g in range(KV_HEADS):
        kc_ref[0, g] = _head_rms(outs[0][g], knw_ref[0:1, :]).astype(BF16)
    vct = jnp.concatenate(outs[1], axis=1).T.astype(BF16)
    for g in range(KV_HEADS):
        vct_ref[0, g] = vct[g * HEAD_DIM:(g + 1) * HEAD_DIM, :]


def _compress(kc_raw, vc_raw, cmp_pos, cmp_w1, cmp_b1, cmp_w2, k_norm_w):
    b, s, _ = kc_raw.shape
    n = s // CMP_STRIDE
    half = CMP_STRIDE
    cw = CMP_STRIDE * KV_WIDTH
    kch = kc_raw.reshape(b, n, cw)
    vch = vc_raw.reshape(b, n, cw)
    pos = cmp_pos.reshape(2, 2, half, 1, HEAD_DIM)
    pos = jnp.broadcast_to(pos, (2, 2, half, KV_HEADS, HEAD_DIM)).reshape(2, 2, cw)
    w1 = cmp_w1.reshape(2, 2, half, HEAD_DIM, CMP_HIDDEN)
    eye = jnp.eye(KV_HEADS, dtype=F32)
    wfull = jnp.einsum('rhjdn,gk->rhjgdkn', w1, eye).reshape(2, 2, cw, KV_HEADS * CMP_HIDDEN).astype(BF16)
    b1 = jnp.tile(cmp_b1.reshape(2, 1, CMP_HIDDEN), (1, 1, KV_HEADS))
    fix = lambda r: (lambda bi: (0,) * r)
    return pl.pallas_call(
        _compress_kernel,
        grid=(b,),
        in_specs=[pl.BlockSpec((1, n, cw), lambda bi: (bi, 0, 0)),
                  pl.BlockSpec((1, n, cw), lambda bi: (bi, 0, 0)),
                  pl.BlockSpec((2, 2, cw), fix(3)),
                  pl.BlockSpec((2, cw, KV_HEADS * CMP_HIDDEN), fix(3)),
                  pl.BlockSpec((2, cw, KV_HEADS * CMP_HIDDEN), fix(3)),
                  pl.BlockSpec((2, 1, KV_HEADS * CMP_HIDDEN), fix(3)),
                  pl.BlockSpec((2, CMP_HIDDEN, HEAD_DIM), fix(3)),
                  pl.BlockSpec((3, HEAD_DIM), fix(2))],
        out_specs=(pl.BlockSpec((1, KV_HEADS, n, HEAD_DIM), lambda bi: (bi, 0, 0, 0)),
                   pl.BlockSpec((1, KV_HEADS, HEAD_DIM, n), lambda bi: (bi, 0, 0, 0))),
        out_shape=(jax.ShapeDtypeStruct((b, KV_HEADS, n, HEAD_DIM), BF16),
                   jax.ShapeDtypeStruct((b, KV_HEADS, HEAD_DIM, n), BF16)),
        compiler_params=_cparams("parallel"),
        name="compress",
    )(kch, vch, pos, wfull[:, 0], wfull[:, 1], b1, cmp_w2.astype(BF16), k_norm_w)


def _nsa_kernel(q_ref, kc_ref, vct_ref, ks_ref, vst_ref, kw_ref, vwt_ref, gt_ref, slope_ref,
                ovl_ref, onw_ref, o_ref, sel_scr, *, n_top):
    qi = pl.program_id(2)
    q0 = qi * TQ
    ncols = HEADS_PER_KV * TQ
    q = q_ref[0].reshape(ncols, HEAD_DIM)
    slope = slope_ref[0]
    nc = kc_ref.shape[2]
    ns = ovl_ref.shape[0]

    def col_t(rows):
        return q0 + (lax.broadcasted_iota(I32, (rows, ncols), 1) & (TQ - 1))

    s = _dot_nt(kc_ref[0, 0], q)
    dist = col_t(nc) - (lax.broadcasted_iota(I32, (nc, ncols), 0) * CMP_STRIDE + (CMP_BLOCK - 1))
    s = jnp.where(dist >= 0, s - slope * dist.astype(F32), -jnp.inf)
    m = jnp.max(s, axis=0, keepdims=True)
    m = jnp.where(m == -jnp.inf, 0.0, m)
    e = jnp.exp(s - m)
    p = e / jnp.maximum(jnp.sum(e, axis=0, keepdims=True), 1e-30)
    o_c = _dot(vct_ref[0, 0], p.astype(BF16))

    psum = p[:, 0:TQ]
    for hh in range(1, HEADS_PER_KV):
        psum = psum + p[:, hh * TQ:(hh + 1) * TQ]
    imp = _split_dot(ovl_ref[...], psum)
    blk = lax.broadcasted_iota(I32, (ns, TQ), 0)
    tq = q0 + lax.broadcasted_iota(I32, (ns, TQ), 1)
    cur = tq >> 6
    forced = (blk == 0) | (blk == cur) | (blk == cur - 1)
    rank = jnp.where(forced, BIG, jnp.where(blk * SEL_BLOCK <= tq, imp, -BIG))

    blkf = blk.astype(F32)

    def topk_body(_, carry):
        r, sel = carry
        mx = jnp.max(r, axis=0, keepdims=True)
        first = jnp.min(jnp.where(r == mx, blkf, float(ns)), axis=0, keepdims=True)
        hit = blkf == first
        return jnp.where(hit, -jnp.inf, r), jnp.where(hit, 1.0, sel)

    _, sel = lax.fori_loop(0, n_top, topk_body, (rank, jnp.zeros((ns, TQ), F32)))
    sel_scr[...] = sel

    tcol = col_t(TK)
    krow = lax.broadcasted_iota(I32, (TK, ncols), 0)
    blocks_per_tile = TK // SEL_BLOCK

    def sel_body(j, carry):
        m_run, l_run, acc = carry
        k0 = pl.multiple_of(j * TK, TK)
        sc = _dot_nt(ks_ref[0, 0, pl.ds(k0, TK), :], q)
        dist = tcol - (k0 + krow)
        rows = sel_scr[pl.ds(pl.multiple_of(j * blocks_per_tile, blocks_per_tile), blocks_per_tile), :]
        m128 = jnp.concatenate(
            [jnp.broadcast_to(rows[bb:bb + 1, :], (SEL_BLOCK, TQ)) for bb in range(blocks_per_tile)], axis=0)
        mask = (jnp.concatenate([m128] * HEADS_PER_KV, axis=1) > 0.5) & (dist >= 0)
        sc = jnp.where(mask, sc - slope * dist.astype(F32), -1e30)
        m_new = jnp.maximum(m_run, jnp.max(sc, axis=0, keepdims=True))
        alpha = jnp.exp(m_run - m_new)
        ex = jnp.exp(sc - m_new)
        l_new = alpha * l_run + jnp.sum(ex, axis=0, keepdims=True)
        acc = alpha * acc + _dot(vst_ref[0, 0, :, pl.ds(k0, TK)], ex.astype(BF16))
        return m_new, l_new, acc

    n_tiles = (q0 + TQ + TK - 1) // TK
    init = (jnp.full((1, ncols), -1e30, F32), jnp.zeros((1, ncols), F32), jnp.zeros((HEAD_DIM, ncols), F32))
    _, l_s, acc_s = lax.fori_loop(0, n_tiles, sel_body, init)
    o_s = acc_s / l_s

    nw = WINDOW + TQ
    start = pl.multiple_of(jnp.maximum(q0 - WINDOW, 0), TQ)
    sw = _dot_nt(kw_ref[0, 0, pl.ds(start, nw), :], q)
    dist = col_t(nw) - (start + lax.broadcasted_iota(I32, (nw, ncols), 0))
    sw = jnp.where((dist >= 0) & (dist < WINDOW), sw - slope * dist.astype(F32), -jnp.inf)
    ew = jnp.exp(sw - jnp.max(sw, axis=0, keepdims=True))
    o_w = _dot(vwt_ref[0, 0, :, pl.ds(start, nw)], ew.astype(BF16)) / jnp.sum(ew, axis=0, keepdims=True)

    gt = gt_ref[0, 0]
    outs = []
    for hh in range(HEADS_PER_KV):
        cs = slice(hh * TQ, (hh + 1) * TQ)
        o = (gt[3 * hh:3 * hh + 1, :] * o_c[:, cs] + gt[3 * hh + 1:3 * hh + 2, :] * o_s[:, cs]
             + gt[3 * hh + 2:3 * hh + 3, :] * o_w[:, cs])
        o = o * lax.rsqrt(jnp.mean(o * o, axis=0, keepdims=True) + RMS_EPS) * onw_ref[0, hh]
        outs.append(o)
    o_ref[0] = jnp.concatenate(outs, axis=0).T


def _nsa(q, kc, vct, ks, vst, kw, vwt, gates_t, attn_out_norm_w):
    b, _, s, _ = q.shape
    nc = kc.shape[2]
    ns = s // SEL_BLOCK
    n_top = min(N_SELECT, ns)
    ncols = HEADS_PER_KV * TQ
    slopes = np.array([2.0 ** (-8.0 * (i + 1) / NSA_HEADS) for i in range(NSA_HEADS)], np.float32)
    slope_cols = jnp.asarray(np.repeat(slopes.reshape(KV_HEADS, HEADS_PER_KV, 1), TQ, axis=2).reshape(KV_HEADS, 1, ncols))
    ci = np.arange(nc)[None, :] * CMP_STRIDE
    bj = np.arange(ns)[:, None]
    ovl = ((ci < (bj + 1) * SEL_BLOCK) & (ci + CMP_BLOCK > bj * SEL_BLOCK) & (np.arange(nc)[None, :] < nc - 1))
    ovl = jnp.asarray(ovl.astype(np.float32)).astype(BF16)
    onw = jnp.broadcast_to(attn_out_norm_w.reshape(KV_HEADS, HEADS_PER_KV, HEAD_DIM, 1),
                           (KV_HEADS, HEADS_PER_KV, HEAD_DIM, TQ))
    gt = gates_t.reshape(b, KV_HEADS, HEADS_PER_KV * 3, s)
    per_bg = lambda bi, g, i: (bi, g, 0, 0)
    return pl.pallas_call(
        functools.partial(_nsa_kernel, n_top=n_top),
        grid=(b, KV_HEADS, s // TQ),
        in_specs=[pl.BlockSpec((1, HEADS_PER_KV, TQ, HEAD_DIM), lambda bi, g, i: (bi, g, i, 0)),
                  pl.BlockSpec((1, 1, nc, HEAD_DIM), per_bg),
                  pl.BlockSpec((1, 1, HEAD_DIM, nc), per_bg),
                  pl.BlockSpec((1, 1, s, HEAD_DIM), per_bg),
                  pl.BlockSpec((1, 1, HEAD_DIM, s), per_bg),
                  pl.BlockSpec((1, 1, s, HEAD_DIM), per_bg),
                  pl.BlockSpec((1, 1, HEAD_DIM, s), per_bg),
                  pl.BlockSpec((1, 1, HEADS_PER_KV * 3, TQ), lambda bi, g, i: (bi, g, 0, i)),
                  pl.BlockSpec((1, 1, ncols), lambda bi, g, i: (g, 0, 0)),
                  pl.BlockSpec((ns, nc), lambda bi, g, i: (0, 0)),
                  pl.BlockSpec((1, HEADS_PER_KV, HEAD_DIM, TQ), lambda bi, g, i: (g, 0, 0, 0))],
        out_specs=pl.BlockSpec((1, TQ, HEADS_PER_KV * HEAD_DIM), lambda bi, g, i: (bi, i, g)),
        out_shape=jax.ShapeDtypeStruct((b, s, NSA_WIDTH), F32),
        scratch_shapes=[pltpu.VMEM((ns, TQ), F32)],
        compiler_params=_cparams("parallel", "parallel", "arbitrary"),
        name="nsa",
    )(q, kc, vct, ks, vst, kw, vwt, gt, slope_cols, ovl, onw)


def _hgrn_kernel(q_ref, k_ref, lf_ref, v_ref, g_ref, onw_ref, o_ref, state_scr, *, n_chunks):
    c = HGRN_CHUNK

    @pl.when(pl.program_id(2) == 0)
    def _():
        state_scr[...] = jnp.zeros_like(state_scr)

    ri = lax.broadcasted_iota(I32, (c, c), 0)
    ci = lax.broadcasted_iota(I32, (c, c), 1)
    tril = (ri >= ci).astype(F32)
    rsub = ri // HGRN_SUB
    rin = ri & (HGRN_SUB - 1)

    def chunk(ck, state_t):
        r0 = pl.multiple_of(ck * c, c)
        q = q_ref[0, pl.ds(r0, c), :]
        k = k_ref[0, pl.ds(r0, c), :]
        lf = lf_ref[0, pl.ds(r0, c), :]
        v = v_ref[0, pl.ds(r0, c), :]
        cum = _dot(tril, lf, precision=HIGHEST)
        o = _dot_nt((q * jnp.exp(cum)).astype(BF16), state_t.astype(BF16))
        scores = jnp.zeros((c, c), F32)
        for i in range(1, c // HGRN_SUB):
            ref_row = cum[i * HGRN_SUB - 1:i * HGRN_SUB, :]
            qs = q * jnp.exp(jnp.minimum(cum - ref_row, 0.0))
            kd = k * jnp.exp(jnp.minimum(ref_row - cum, 0.0))
            blk = _dot_nt(qs.astype(BF16), kd.astype(BF16))
            scores = jnp.where((rsub == i) & (ci < i * HGRN_SUB), blk, scores)
        for d in range(HGRN_SUB):
            if d == 0:
                w = jnp.sum(q * k, axis=-1, keepdims=True)
            else:
                ksh = pltpu.roll(k, d, 0)
                csh = pltpu.roll(cum, d, 0)
                w = jnp.sum(q * ksh * jnp.exp(jnp.minimum(cum - csh, 0.0)), axis=-1, keepdims=True)
            scores = jnp.where((ri - ci == d) & (rin >= d), w, scores)
        o = o + _dot(scores.astype(BF16), v.astype(BF16))
        last = cum[c - 1:c, :]
        kd = (k * jnp.exp(last - cum)).astype(BF16)
        state_t = state_t * jnp.exp(last) + _dot(v.T.astype(BF16), kd)
        o = o * g_ref[0, pl.ds(r0, c), :]
        o = o * lax.rsqrt(jnp.mean(o * o, axis=-1, keepdims=True) + RMS_EPS) * onw_ref[0]
        o_ref[0, pl.ds(r0, c), :] = o
        return state_t

    state_scr[...] = lax.fori_loop(0, n_chunks, chunk, state_scr[...])


def _hgrn(hq, hk, hlf, hv, hg, rec_out_norm_w, rows):
    b, s, _ = hq.shape
    blk = pl.BlockSpec((1, rows, HGRN_DIM), lambda bi, h, i: (bi, i, h))
    return pl.pallas_call(
        functools.partial(_hgrn_kernel, n_chunks=rows // HGRN_CHUNK),
        grid=(b, HGRN_HEADS, s // rows),
        in_specs=[blk, blk, blk, blk, blk,
                  pl.BlockSpec((1, 1, HGRN_DIM), lambda bi, h, i: (h, 0, 0))],
        out_specs=blk,
        out_shape=jax.ShapeDtypeStruct((b, s, HGRN_WIDTH), F32),
        scratch_shapes=[pltpu.VMEM((HGRN_DIM, HGRN_DIM), F32)],
        compiler_params=_cparams("parallel", "parallel", "arbitrary"),
        name="hgrn",
    )(hq, hk, hlf, hv, hg, rec_out_norm_w.reshape(HGRN_HEADS, 1, HGRN_DIM))


def _outproj_kernel(x_ref, a_ref, r_ref, wa_ref, wr_ref, gt_ref, sc_ref, sh_ref, n2_ref, x1_ref, h2_ref, h2p_ref):
    mixed = _dot(a_ref[0].astype(BF16), wa_ref[...]) + _dot(r_ref[0].astype(BF16), wr_ref[...])
    x1 = x_ref[0] + gt_ref[0] * mixed
    x1_ref[0] = x1
    ms = jnp.mean(x1 * x1, axis=-1, keepdims=True)
    h2 = x1 * lax.rsqrt(ms + RMS_EPS) * n2_ref[...] * (1.0 + sc_ref[0]) + sh_ref[0]
    h2_ref[0] = h2
    h2p_ref[0] = _pack_bf16_pair(h2[:, :D_MODEL // 2], h2[:, D_MODEL // 2:])


def _outproj(x, attn, rec, w_out, gt1, sc2, sh2, norm2_w, tm):
    b, s, d = x.shape
    row = lambda bi, i: (bi, i, 0)
    per_b = lambda bi, i: (bi, 0, 0)
    fixed2 = lambda bi, i: (0, 0)
    w = w_out.astype(BF16)
    return pl.pallas_call(
        _outproj_kernel,
        grid=(b, s // tm),
        in_specs=[pl.BlockSpec((1, tm, d), row),
                  pl.BlockSpec((1, tm, NSA_WIDTH), row),
                  pl.BlockSpec((1, tm, HGRN_WIDTH), row),
                  pl.BlockSpec((NSA_WIDTH, d), fixed2),
                  pl.BlockSpec((HGRN_WIDTH, d), fixed2),
                  pl.BlockSpec((1, 1, d), per_b),
                  pl.BlockSpec((1, 1, d), per_b),
                  pl.BlockSpec((1, 1, d), per_b),
                  pl.BlockSpec((1, d), fixed2)],
        out_specs=(pl.BlockSpec((1, tm, d), row), pl.BlockSpec((1, tm, d), row), pl.BlockSpec((1, tm, d // 2), row)),
        out_shape=(jax.ShapeDtypeStruct((b, s, d), F32), jax.ShapeDtypeStruct((b, s, d), F32),
                   jax.ShapeDtypeStruct((b, s, d // 2), jnp.uint32)),
        compiler_params=_cparams("parallel", "parallel"),
        name="outproj",
    )(x, attn, rec, w[:NSA_WIDTH], w[NSA_WIDTH:], gt1, sc2, sh2, norm2_w)


def _mixer(x, c, ada_w, ada_b, norm1_w, norm2_w, w_in, q_norm_w, k_norm_w, cmp_pos, cmp_w1, cmp_b1, cmp_w2,
           attn_out_norm_w, hgrn_lb_param, rec_out_norm_w, w_out):
    b, s, d = x.shape
    mod = _mod(c, ada_w, ada_b)
    sh1, sc1, gt1, sh2, sc2, gt2 = [m.reshape(b, 1, d) for m in jnp.split(mod, 6, axis=-1)]
    o = NSA_WIDTH + 6 * KV_WIDTH
    w_cat = jnp.concatenate([w_in[:, :o], w_in[:, o:o + NSA_HEADS * 3],
                             jnp.zeros((d, GATE_PAD - NSA_HEADS * 3), w_in.dtype),
                             w_in[:, o + NSA_HEADS * 3:]], axis=1).astype(BF16)
    tm = min(256, s)
    (q, kc_raw, vc_raw, ks, vst, kw, vwt, gates_t, hq, hk, hlf, hv, hg) = _inproj(
        x, sc1, sh1, norm1_w.reshape(1, d), w_cat, q_norm_w.reshape(1, HEAD_DIM), k_norm_w, hgrn_lb_param, tm)
    kc, vct = _compress(kc_raw, vc_raw, cmp_pos, cmp_w1, cmp_b1, cmp_w2, k_norm_w)
    attn = _nsa(q, kc, vct, ks, vst, kw, vwt, gates_t, attn_out_norm_w)
    rec = _hgrn(hq, hk, hlf, hv, hg, rec_out_norm_w, min(512, s))
    x1, h2, h2p = _outproj(x, attn, rec, w_out, gt1, sc2, sh2, norm2_w.reshape(1, d), tm)
    return x1, h2, h2p, gt2


def _router_kernel(h_ref, rwt_ref, bias_ref, tri_ref, ones_ref, idx_ref, w_ref, rank_ref, cnt_ref, carry_scr, *, tr):
    @pl.when(pl.program_id(0) == 0)
    def _():
        carry_scr[...] = jnp.zeros_like(carry_scr)

    h = h_ref[...]
    h_hi = h.astype(BF16)
    h_lo = (h - h_hi.astype(F32)).astype(BF16)
    logits = _dot_nt(rwt_ref[0], h_hi) + _dot_nt(rwt_ref[1], h_hi) + _dot_nt(rwt_ref[0], h_lo)
    scores = _sigmoid(logits)
    biased = scores + bias_ref[...]
    neg = -jnp.inf

    gs = []
    for g in range(N_GROUPS):
        sub = biased[g * GROUP_SIZE:(g + 1) * GROUP_SIZE, :]
        m1 = jnp.max(sub, axis=0, keepdims=True)
        dup = jnp.sum((sub == m1).astype(F32), axis=0, keepdims=True)
        m2 = jnp.max(jnp.where(sub < m1, sub, neg), axis=0, keepdims=True)
        gs.append(m1 + jnp.where(dup >= 2.0, m1, m2))
    parts = []
    for g in range(N_GROUPS):
        beaten = jnp.zeros_like(gs[g])
        for g2 in range(N_GROUPS):
            if g2 != g:
                beats = (gs[g2] >= gs[g]) if g2 < g else (gs[g2] > gs[g])
                beaten = beaten + beats.astype(F32)
        sub = biased[g * GROUP_SIZE:(g + 1) * GROUP_SIZE, :]
        parts.append(jnp.where(beaten < float(TOPK_GROUPS), sub, neg))
    cand = jnp.concatenate(parts, axis=0)

    rowf = lax.broadcasted_iota(I32, (N_EXPERTS, tr), 0).astype(F32)
    idx_rows, w_rows, hits = [], [], []
    multi = jnp.zeros((N_EXPERTS, tr), F32)
    for _ in range(TOP_K):
        mx = jnp.max(cand, axis=0, keepdims=True)
        first = jnp.min(jnp.where(cand == mx, rowf, float(N_EXPERTS)), axis=0, keepdims=True)
        hit = rowf == first
        idx_rows.append(first)
        w_rows.append(jnp.sum(jnp.where(hit, scores, 0.0), axis=0, keepdims=True))
        cand = jnp.where(hit, neg, cand)
        multi = jnp.where(hit, 1.0, multi)
    w = jnp.concatenate(w_rows, axis=0)
    w_ref[...] = w / jnp.sum(w, axis=0, keepdims=True) * ROUTED_SCALE
    idx = jnp.concatenate(idx_rows, axis=0)
    idx_ref[...] = idx.astype(I32)

    carry = carry_scr[...]
    mb = multi.astype(BF16)
    before = _dot(mb, tri_ref[...]) + jnp.concatenate([carry] * (tr // 128), axis=1)
    rank_rows = [jnp.sum(jnp.where(rowf == idx_rows[k], before, 0.0), axis=0, keepdims=True) for k in range(TOP_K)]
    rank_ref[...] = jnp.concatenate(rank_rows, axis=0).astype(I32)
    carry = carry + _dot(mb, ones_ref[...])
    carry_scr[...] = carry
    cnt_ref[...] = carry


def _router(h2, router_w, router_bias, tr):
    t, d = h2.shape
    tri = jnp.asarray(np.triu(np.ones((tr, tr), np.float32), 1)).astype(BF16)
    ones = jnp.ones((tr, 128), BF16)
    tok = pl.BlockSpec((TOP_K, tr), lambda i: (0, i))
    fixed = lambda i: (0, 0)
    rwt = router_w.T
    rwt_hi = rwt.astype(BF16)
    rwt_split = jnp.stack([rwt_hi, (rwt - rwt_hi.astype(F32)).astype(BF16)])
    return pl.pallas_call(
        functools.partial(_router_kernel, tr=tr),
        grid=(t // tr,),
        in_specs=[pl.BlockSpec((tr, d), lambda i: (i, 0)),
                  pl.BlockSpec((2, N_EXPERTS, d), lambda i: (0, 0, 0)),
                  pl.BlockSpec((N_EXPERTS, 1), fixed),
                  pl.BlockSpec((tr, tr), fixed),
                  pl.BlockSpec((tr, 128), fixed)],
        out_specs=(tok, tok, tok, pl.BlockSpec((N_EXPERTS, 128), fixed)),
        out_shape=(jax.ShapeDtypeStruct((TOP_K, t), I32), jax.ShapeDtypeStruct((TOP_K, t), F32),
                   jax.ShapeDtypeStruct((TOP_K, t), I32), jax.ShapeDtypeStruct((N_EXPERTS, 128), F32)),
        scratch_shapes=[pltpu.VMEM((N_EXPERTS, 128), F32)],
        compiler_params=_cparams("arbitrary"),
        name="router",
    )(h2, rwt_split, router_bias.reshape(N_EXPERTS, 1), tri, ones)


def _pack_bf16_pair(a, b):
    ua = lax.bitcast_convert_type(a.astype(BF16).astype(F32), jnp.uint32)
    ub = lax.bitcast_convert_type(b.astype(BF16).astype(F32), jnp.uint32)
    return ua | (ub >> 16)


def _unpack_bf16_pair(w):
    a = lax.bitcast_convert_type(w & jnp.uint32(0xFFFF0000), F32)
    b = lax.bitcast_convert_type(w << 16, F32)
    return a, b


def _row_copy(src, s_row, dst, d_row, sem):
    return pltpu.make_async_copy(src.at[pl.ds(s_row, 1)], dst.at[pl.ds(d_row, 1)], sem)


def _dispatch_kernel(ps_ref, cnt_ref, idx_ref, rank_ref, h_ref, xs_ref, zbuf, sem, zsem, *, td):
    def issue(t, _):
        for k in range(TOP_K):
            _row_copy(h_ref, t, xs_ref, ps_ref[idx_ref[k, t]] + rank_ref[k, t], sem).start()
        return 0

    lax.fori_loop(0, td, issue, 0)

    def pad_rows(e, wait):
        pos = ps_ref[e] + cnt_ref[e]
        rem = (EXPERT_BLOCK - cnt_ref[e] % EXPERT_BLOCK) % EXPERT_BLOCK

        def one(r, _):
            cp = _row_copy(zbuf, 0, xs_ref, 0 if wait else pos + r, zsem)
            if wait:
                cp.wait()
            else:
                cp.start()
            return 0

        lax.fori_loop(0, rem, one, 0)
        return 0

    @pl.when(pl.program_id(0) == 0)
    def _():
        zbuf[...] = jnp.zeros_like(zbuf)
        lax.fori_loop(0, N_EXPERTS, lambda e, _: pad_rows(e, False), 0)
        lax.fori_loop(0, N_EXPERTS, lambda e, _: pad_rows(e, True), 0)

    def drain(t, _):
        for k in range(TOP_K):
            _row_copy(h_ref, 0, xs_ref, 0, sem).wait()
        return 0

    lax.fori_loop(0, td, drain, 0)


def _dispatch(h2p, pad_start, counts, idx, rank, n_rows, td):
    t, dw = h2p.shape
    tok = pl.BlockSpec((TOP_K, td), lambda i, ps, cn: (0, i), memory_space=pltpu.SMEM)
    return pl.pallas_call(
        functools.partial(_dispatch_kernel, td=td),
        grid_spec=pltpu.PrefetchScalarGridSpec(
            num_scalar_prefetch=2,
            grid=(t // td,),
            in_specs=[tok, tok, pl.BlockSpec((td, dw), lambda i, ps, cn: (i, 0))],
            out_specs=pl.BlockSpec(memory_space=pl.ANY),
            scratch_shapes=[pltpu.VMEM((8, dw), h2p.dtype),
                            pltpu.SemaphoreType.DMA, pltpu.SemaphoreType.DMA]),
        out_shape=jax.ShapeDtypeStruct((n_rows, dw), h2p.dtype),
        compiler_params=pltpu.CompilerParams(dimension_semantics=("arbitrary",), has_side_effects=True),
        name="dispatch",
    )(pad_start, counts, idx, rank, h2p)


def _experts_kernel(be_ref, nu_ref, xs_ref, wg_ref, wu_ref, wd_ref, ys_ref):
    i = pl.program_id(0)
    half = D_MODEL // 2

    @pl.when(i < nu_ref[0])
    def _():
        xa, xb = _unpack_bf16_pair(xs_ref[...])
        xa, xb = xa.astype(BF16), xb.astype(BF16)
        g = _dot(xa, wg_ref[0, :half].astype(BF16)) + _dot(xb, wg_ref[0, half:].astype(BF16))
        u = _dot(xa, wu_ref[0, :half].astype(BF16)) + _dot(xb, wu_ref[0, half:].astype(BF16))
        act = (g * _sigmoid(g) * u).astype(BF16)
        y = _dot(act, wd_ref[0].astype(BF16))
        ys_ref[...] = _pack_bf16_pair(y[:, :half], y[:, half:])

    @pl.when(i >= nu_ref[0])
    def _():
        ys_ref[...] = jnp.zeros_like(ys_ref)


def _experts(xs, blk_e, n_used, w_gate, w_up, w_down):
    n_rows, dw = xs.shape
    d = w_gate.shape[1]
    nblk = n_rows // EXPERT_BLOCK
    row_map = lambda i, be, nu: (jnp.minimum(i, nu[0] - 1), 0)
    w_map = lambda i, be, nu: (be[i], 0, 0)
    return pl.pallas_call(
        _experts_kernel,
        grid_spec=pltpu.PrefetchScalarGridSpec(
            num_scalar_prefetch=2,
            grid=(nblk,),
            in_specs=[pl.BlockSpec((EXPERT_BLOCK, dw), row_map),
                      pl.BlockSpec((1, d, EXPERT_FF), w_map),
                      pl.BlockSpec((1, d, EXPERT_FF), w_map),
                      pl.BlockSpec((1, EXPERT_FF, d), w_map)],
            out_specs=pl.BlockSpec((EXPERT_BLOCK, dw), lambda i, be, nu: (i, 0))),
        out_shape=jax.ShapeDtypeStruct((n_rows, dw), xs.dtype),
        compiler_params=_cparams("arbitrary"),
        name="experts",
    )(blk_e, n_used, xs, w_gate, w_up, w_down)


def _combine_kernel(ps_ref, idx_ref, rank_ref, x1_ref, h_ref, w_ref, gt_ref, sg_ref, su_ref, sd_ref, ys_ref,
                    o_ref, buf, sem, *, tc):
    half = D_MODEL // 2

    def issue(t, _):
        for k in range(TOP_K):
            _row_copy(ys_ref, ps_ref[idx_ref[k, t]] + rank_ref[k, t], buf.at[k], t, sem).start()
        return 0

    lax.fori_loop(0, tc, issue, 0)

    hb = h_ref[...].astype(BF16)
    g = _dot(hb, sg_ref[...])
    u = _dot(hb, su_ref[...])
    ffn = _dot((g * _sigmoid(g) * u).astype(BF16), sd_ref[...])

    def drain(t, _):
        for k in range(TOP_K):
            _row_copy(ys_ref, 0, buf.at[k], 0, sem).wait()
        return 0

    lax.fori_loop(0, tc, drain, 0)

    w = w_ref[...]
    ra = jnp.zeros((tc, half), F32)
    rb = jnp.zeros((tc, half), F32)
    for k in range(TOP_K):
        ya, yb = _unpack_bf16_pair(buf[k])
        ra = ra + w[:, k:k + 1] * ya
        rb = rb + w[:, k:k + 1] * yb
    ffn = ffn + jnp.concatenate([ra, rb], axis=1)
    o_ref[...] = x1_ref[...] + gt_ref[0] * ffn


def _combine(x1, h2, w_tok, gt2, pad_start, idx, rank, ys, sg, su, sd, seq, tc):
    t, d = x1.shape
    row = lambda i, ps: (i, 0)
    fixed = lambda i, ps: (0, 0)
    tok = pl.BlockSpec((TOP_K, tc), lambda i, ps: (0, i), memory_space=pltpu.SMEM)
    return pl.pallas_call(
        functools.partial(_combine_kernel, tc=tc),
        grid_spec=pltpu.PrefetchScalarGridSpec(
            num_scalar_prefetch=1,
            grid=(t // tc,),
            in_specs=[tok, tok,
                      pl.BlockSpec((tc, d), row),
                      pl.BlockSpec((tc, d), row),
                      pl.BlockSpec((tc, TOP_K), row),
                      pl.BlockSpec((1, 1, d), lambda i, ps: ((i * tc) // seq, 0, 0)),
                      pl.BlockSpec((d, SHARED_FF), fixed),
                      pl.BlockSpec((d, SHARED_FF), fixed),
                      pl.BlockSpec((SHARED_FF, d), fixed),
                      pl.BlockSpec(memory_space=pl.ANY)],
            out_specs=pl.BlockSpec((tc, d), row),
            scratch_shapes=[pltpu.VMEM((TOP_K, tc, d // 2), jnp.uint32), pltpu.SemaphoreType.DMA]),
        out_shape=jax.ShapeDtypeStruct((t, d), F32),
        compiler_params=_cparams("arbitrary"),
        name="combine",
    )(pad_start, idx, rank, x1, h2, w_tok, gt2, sg.astype(BF16), su.astype(BF16), sd.astype(BF16), ys)


def _moe_parts(x1, h2, h2p, gt2, router_w, router_bias, w_gate, w_up, w_down, sg, su, sd):
    b, s, d = x1.shape
    t = b * s
    h2 = h2.reshape(t, d)
    idx, w, rank, cnt = _router(h2, router_w, router_bias, min(256, t))
    counts = cnt[:, 0].astype(I32)
    padded = (counts + EXPERT_BLOCK - 1) // EXPERT_BLOCK * EXPERT_BLOCK
    pad_end = jnp.cumsum(padded)
    pad_start = pad_end - padded
    n_rows = t * TOP_K + N_EXPERTS * EXPERT_BLOCK
    nblk = n_rows // EXPERT_BLOCK
    n_used = (pad_end[-1:] // EXPERT_BLOCK).astype(I32)
    blk_e = jnp.searchsorted(pad_end, jnp.arange(nblk, dtype=I32) * EXPERT_BLOCK, side='right')
    blk_e = jnp.minimum(blk_e, N_EXPERTS - 1).astype(I32)
    blk_e = jnp.where(jnp.arange(nblk) < n_used[0], blk_e, blk_e[jnp.maximum(n_used[0] - 1, 0)])
    xs = _dispatch(h2p.reshape(t, d // 2), pad_start.astype(I32), counts, idx, rank, n_rows, min(1024, t))
    ys = _experts(xs, blk_e, n_used, w_gate, w_up, w_down)
    out = _combine(x1.reshape(t, d), h2, w.T, gt2, pad_start.astype(I32), idx, rank, ys, sg, su, sd, s, min(128, t))
    return out.reshape(b, s, d), dict(idx=idx, w=w, rank=rank, cnt=cnt)


def kernel(x, c, ada_w, ada_b, norm1_w, norm2_w, w_in, q_norm_w, k_norm_w, cmp_pos, cmp_w1, cmp_b1, cmp_w2, attn_out_norm_w, hgrn_lb_param, rec_out_norm_w, w_out, router_w, router_bias, exp_w_gate, exp_w_up, exp_w_down, shared_w_gate, shared_w_up, shared_w_down):
    assert ada_w.shape[0] == 1, "one layer"
    assert x.shape[0] <= 8 and x.shape[1] % TK == 0 and x.shape[1] >= WINDOW + TQ
    l = 0
    x1, h2, h2p, gt2 = _mixer(x, c, ada_w[l], ada_b[l], norm1_w[l], norm2_w[l], w_in[l], q_norm_w[l], k_norm_w[l],
                         cmp_pos[l], cmp_w1[l], cmp_b1[l], cmp_w2[l], attn_out_norm_w[l], hgrn_lb_param,
                         rec_out_norm_w[l], w_out[l])
    out, _ = _moe_parts(x1, h2, h2p, gt2, router_w[l], router_bias[l], exp_w_gate[l], exp_w_up[l], exp_w_down[l],
                        shared_w_gate[l], shared_w_up[l], shared_w_down[l])
    return out
```

```python
import functools

import numpy as np
import jax
import jax.numpy as jnp
from jax import lax
from jax.experimental import pallas as pl
from jax.experimental.pallas import tpu as pltpu

F32 = jnp.float32
BF16 = jnp.bfloat16
I32 = jnp.int32

D_MODEL = 1024
NSA_HEADS = 8
HEAD_DIM = 64
NSA_WIDTH = NSA_HEADS * HEAD_DIM
KV_HEADS = 2
HEADS_PER_KV = NSA_HEADS // KV_HEADS
KV_WIDTH = KV_HEADS * HEAD_DIM
CMP_BLOCK = 32
CMP_STRIDE = 16
CMP_HIDDEN = 256
SEL_BLOCK = 64
N_SELECT = 16
WINDOW = 512
HGRN_HEADS = 4
HGRN_DIM = 128
HGRN_WIDTH = HGRN_HEADS * HGRN_DIM
HGRN_CHUNK = 64
HGRN_SUB = 16
N_EXPERTS = 256
TOP_K = 8
N_GROUPS = 8
GROUP_SIZE = N_EXPERTS // N_GROUPS
TOPK_GROUPS = 4
EXPERT_FF = 256
SHARED_FF = 256
ROUTED_SCALE = 2.5
RMS_EPS = 1e-6
BIG = 1e9
LOG2E = 1.4426950408889634
GATE_PAD = 128
PROJ_COLS = NSA_WIDTH + 6 * KV_WIDTH + GATE_PAD + 4 * HGRN_WIDTH

VMEM_LIMIT = 56 * 1024 * 1024

TQ = 128
TK = 512
EXPERT_BLOCK = 256
HIGHEST = lax.Precision.HIGHEST


def _cparams(*sem):
    return pltpu.CompilerParams(dimension_semantics=sem, vmem_limit_bytes=VMEM_LIMIT)


def _sigmoid(x):
    return 1.0 / (1.0 + jnp.exp(-x))


def _dot_nt(a, b):
    return lax.dot_general(a, b, (((1,), (1,)), ((), ())), preferred_element_type=F32)


def _dot(a, b, **kw):
    return jnp.dot(a, b, preferred_element_type=F32, **kw)


def _split_dot(a_bf16_exact, x):
    hi = x.astype(BF16)
    lo = (x - hi.astype(F32)).astype(BF16)
    return _dot(a_bf16_exact, hi) + _dot(a_bf16_exact, lo)


def _mod_kernel(c_ref, w_ref, b_ref, o_ref):
    c = c_ref[...]
    cond = c * _sigmoid(c)
    o_ref[...] = _dot(cond, w_ref[...], precision=HIGHEST) + b_ref[...]


def _mod(c, ada_w, ada_b):
    b, d = c.shape
    rows = 8
    c_pad = jnp.zeros((rows, d), F32).at[:b].set(c)
    n = ada_w.shape[1]
    out = pl.pallas_call(
        _mod_kernel,
        grid=(n // d,),
        in_specs=[pl.BlockSpec((rows, d), lambda j: (0, 0)),
                  pl.BlockSpec((d, d), lambda j: (0, j)),
                  pl.BlockSpec((1, d), lambda j: (0, j))],
        out_specs=pl.BlockSpec((rows, d), lambda j: (0, j)),
        out_shape=jax.ShapeDtypeStruct((rows, n), F32),
        compiler_params=_cparams("parallel"),
        name="mod",
    )(c_pad, ada_w, ada_b.reshape(1, n))
    return out[:b]


def _head_rms(t, w):
    return t * lax.rsqrt(jnp.mean(t * t, axis=-1, keepdims=True) + RMS_EPS) * w


def _pos_digits(pos):
    lane = lax.broadcasted_iota(I32, pos.shape, 1)
    d0 = (lane == 0) | (lane == 3) | (lane == 6)
    d1 = (lane == 1) | (lane == 4) | (lane == 7)
    d2 = (lane == 2) | (lane == 5) | (lane == 8)
    dig = jnp.where(d0, pos >> 12, jnp.where(d1, (pos >> 6) & 63, jnp.where(d2, pos & 63, 0)))
    return dig.astype(F32)


def _inproj_kernel(x_ref, sc_ref, sh_ref, n1_ref, w_ref, qnw_ref, knw_ref, lbp_ref, qaug_ref,
                   q_ref, kcr_ref, vcr_ref, ks_ref, vst_ref, kw_ref, vwt_ref, gt_ref,
                   hq_ref, hk_ref, hlf_ref, hv_ref, hg_ref):
    x = x_ref[0]
    ms = jnp.mean(x * x, axis=-1, keepdims=True)
    h = x * lax.rsqrt(ms + RMS_EPS) * n1_ref[...] * (1.0 + sc_ref[0]) + sh_ref[0]
    p = _dot(h.astype(BF16), w_ref[...])
    tm = x.shape[0]

    qnw = qnw_ref[...]
    for hd in range(NSA_HEADS):
        t = p[:, hd * HEAD_DIM:(hd + 1) * HEAD_DIM]
        qn = _head_rms(t, qnw) * (HEAD_DIM ** -0.5 * LOG2E)
        qa = jnp.broadcast_to(qaug_ref[hd:hd + 1, :], (tm, HEAD_DIM))
        q_ref[0, hd] = jnp.concatenate([qn, qa], axis=1).astype(BF16)
    kaug = _pos_digits(pl.program_id(1) * tm + lax.broadcasted_iota(I32, (tm, HEAD_DIM), 0))

    o = NSA_WIDTH
    kcr_ref[0] = p[:, o:o + KV_WIDTH]
    vcr_ref[0] = p[:, o + KV_WIDTH:o + 2 * KV_WIDTH]
    ks = p[:, o + 2 * KV_WIDTH:o + 3 * KV_WIDTH]
    vs = p[:, o + 3 * KV_WIDTH:o + 4 * KV_WIDTH]
    kw = p[:, o + 4 * KV_WIDTH:o + 5 * KV_WIDTH]
    vw = p[:, o + 5 * KV_WIDTH:o + 6 * KV_WIDTH]
    for g in range(KV_HEADS):
        sl = slice(g * HEAD_DIM, (g + 1) * HEAD_DIM)
        ks_ref[0, g] = jnp.concatenate([_head_rms(ks[:, sl], knw_ref[1:2, :]), kaug], axis=1).astype(BF16)
        kw_ref[0, g] = jnp.concatenate([_head_rms(kw[:, sl], knw_ref[2:3, :]), kaug], axis=1).astype(BF16)
    vst = vs.T.astype(BF16)
    vwt = vw.T.astype(BF16)
    for g in range(KV_HEADS):
        vst_ref[0, g] = vst[g * HEAD_DIM:(g + 1) * HEAD_DIM, :]
        vwt_ref[0, g] = vwt[g * HEAD_DIM:(g + 1) * HEAD_DIM, :]

    o = NSA_WIDTH + 6 * KV_WIDTH
    gates = _sigmoid(p[:, o:o + GATE_PAD])
    gt_ref[0] = gates.T[:NSA_HEADS * 3, :]

    o = o + GATE_PAD
    hq = p[:, o:o + HGRN_WIDTH]
    hf = p[:, o + HGRN_WIDTH:o + 2 * HGRN_WIDTH]
    hi = p[:, o + 2 * HGRN_WIDTH:o + 3 * HGRN_WIDTH]
    hg = p[:, o + 3 * HGRN_WIDTH:o + 4 * HGRN_WIDTH]
    lbp = lbp_ref[...]
    e = jnp.exp(lbp - jnp.max(lbp, axis=0, keepdims=True))
    lb = e[0:1, :] / jnp.sum(e, axis=0, keepdims=True)
    f = lb + (1.0 - lb) * _sigmoid(hf)
    hq_ref[0] = hq * _sigmoid(hq) * (HGRN_DIM ** -0.5)
    hk_ref[0] = 1.0 - f
    hlf_ref[0] = jnp.log(f)
    hv_ref[0] = hi
    hg_ref[0] = _sigmoid(hg)


def _inproj(x, sc1, sh1, norm1_w, w_cat, q_norm_w, k_norm_w, lb_param, tm):
    b, s, d = x.shape
    row = lambda bi, i: (bi, i, 0)
    per_b = lambda bi, i: (bi, 0, 0)
    fixed2 = lambda bi, i: (0, 0)
    aw = 2 * HEAD_DIM
    rest = np.array([2.0 ** (-8.0 * (i + 1) / NSA_HEADS) for i in range(NSA_HEADS)], np.float64) * LOG2E
    qaug = np.zeros((NSA_HEADS, HEAD_DIM), np.float32)
    for i in range(3):
        term = rest.astype(np.float32).astype(BF16).astype(np.float64)
        rest = rest - term
        for dgt, wgt in enumerate((4096.0, 64.0, 1.0)):
            qaug[:, 3 * i + dgt] = term * wgt
    assert np.all(qaug == qaug.astype(BF16).astype(np.float32))
    out_shape = (
        jax.ShapeDtypeStruct((b, NSA_HEADS, s, aw), BF16),
        jax.ShapeDtypeStruct((b, s, KV_WIDTH), F32),
        jax.ShapeDtypeStruct((b, s, KV_WIDTH), F32),
        jax.ShapeDtypeStruct((b, KV_HEADS, s, aw), BF16),
        jax.ShapeDtypeStruct((b, KV_HEADS, HEAD_DIM, s), BF16),
        jax.ShapeDtypeStruct((b, KV_HEADS, s, aw), BF16),
        jax.ShapeDtypeStruct((b, KV_HEADS, HEAD_DIM, s), BF16),
        jax.ShapeDtypeStruct((b, NSA_HEADS * 3, s), F32),
    ) + tuple(jax.ShapeDtypeStruct((b, s, HGRN_WIDTH), F32) for _ in range(5))
    hm = lambda n, w: pl.BlockSpec((1, n, tm, w), lambda bi, i: (bi, 0, i, 0))
    hmt = lambda n, w: pl.BlockSpec((1, n, w, tm), lambda bi, i: (bi, 0, 0, i))
    out_specs = (
        hm(NSA_HEADS, aw),
        pl.BlockSpec((1, tm, KV_WIDTH), row),
        pl.BlockSpec((1, tm, KV_WIDTH), row),
        hm(KV_HEADS, aw), hmt(KV_HEADS, HEAD_DIM),
        hm(KV_HEADS, aw), hmt(KV_HEADS, HEAD_DIM),
        pl.BlockSpec((1, NSA_HEADS * 3, tm), lambda bi, i: (bi, 0, i)),
    ) + tuple(pl.BlockSpec((1, tm, HGRN_WIDTH), row) for _ in range(5))
    return pl.pallas_call(
        _inproj_kernel,
        grid=(b, s // tm),
        in_specs=[pl.BlockSpec((1, tm, d), row),
                  pl.BlockSpec((1, 1, d), per_b),
                  pl.BlockSpec((1, 1, d), per_b),
                  pl.BlockSpec((1, d), fixed2),
                  pl.BlockSpec((d, PROJ_COLS), fixed2),
                  pl.BlockSpec((1, HEAD_DIM), fixed2),
                  pl.BlockSpec((3, HEAD_DIM), fixed2),
                  pl.BlockSpec(lb_param.shape, fixed2),
                  pl.BlockSpec((NSA_HEADS, HEAD_DIM), fixed2)],
        out_specs=out_specs,
        out_shape=out_shape,
        compiler_params=_cparams("parallel", "parallel"),
        name="inproj",
    )(x, sc1, sh1, norm1_w, w_cat, q_norm_w, k_norm_w, lb_param, jnp.asarray(qaug))


def _gelu_tanh(x):
    return 0.5 * x * (1.0 + jnp.tanh(0.7978845608028654 * (x + 0.044715 * x * x * x)))


def _compress_kernel(kch_ref, vch_ref, pos_ref, wa_ref, wb_ref, b1_ref, w2_ref, knw_ref,
                     kc_ref, vct_ref):
    n = kch_ref.shape[1]
    outs = []
    for br, ch_ref in enumerate((kch_ref, vch_ref)):
        ch = ch_ref[0]
        a = _dot((ch + pos_ref[br, 0:1, :]).astype(BF16), wa_ref[br])
        bm = _dot((ch + pos_ref[br, 1:2, :]).astype(BF16), wb_ref[br])
        pre = a + pltpu.roll(bm, n - 1, 0) + b1_ref[br]
        hid = _gelu_tanh(pre).astype(BF16)
        outs.append([_dot(hid[:, g * CMP_HIDDEN:(g + 1) * CMP_HIDDEN], w2_ref[br]) for g in range(KV_HEADS)])
    end_digits = _pos_digits(lax.broadcasted_iota(I32, (n, HEAD_DIM), 0) * CMP_STRIDE + (CMP_BLOCK - 1))
    for g in range(KV_HEADS):
        kc_ref[0, g] = jnp.concatenate([_head_rms(outs[0][g], knw_ref[0:1, :]), end_digits], axis=1).astype(BF16)
    vct = jnp.concatenate(outs[1], axis=1).T.astype(BF16)
    for g in range(KV_HEADS):
        vct_ref[0, g] = vct[g * HEAD_DIM:(g + 1) * HEAD_DIM, :]


def _compress(kc_raw, vc_raw, cmp_pos, cmp_w1, cmp_b1, cmp_w2, k_norm_w):
    b, s, _ = kc_raw.shape
    n = s // CMP_STRIDE
    half = CMP_STRIDE
    cw = CMP_STRIDE * KV_WIDTH
    kch = kc_raw.reshape(b, n, cw)
    vch = vc_raw.reshape(b, n, cw)
    pos = cmp_pos.reshape(2, 2, half, 1, HEAD_DIM)
    pos = jnp.broadcast_to(pos, (2, 2, half, KV_HEADS, HEAD_DIM)).reshape(2, 2, cw)
    w1 = cmp_w1.reshape(2, 2, half, HEAD_DIM, CMP_HIDDEN)
    eye = jnp.eye(KV_HEADS, dtype=F32)
    wfull = jnp.einsum('rhjdn,gk->rhjgdkn', w1, eye).reshape(2, 2, cw, KV_HEADS * CMP_HIDDEN).astype(BF16)
    b1 = jnp.tile(cmp_b1.reshape(2, 1, CMP_HIDDEN), (1, 1, KV_HEADS))
    fix = lambda r: (lambda bi: (0,) * r)
    return pl.pallas_call(
        _compress_kernel,
        grid=(b,),
        in_specs=[pl.BlockSpec((1, n, cw), lambda bi: (bi, 0, 0)),
                  pl.BlockSpec((1, n, cw), lambda bi: (bi, 0, 0)),
                  pl.BlockSpec((2, 2, cw), fix(3)),
                  pl.BlockSpec((2, cw, KV_HEADS * CMP_HIDDEN), fix(3)),
                  pl.BlockSpec((2, cw, KV_HEADS * CMP_HIDDEN), fix(3)),
                  pl.BlockSpec((2, 1, KV_HEADS * CMP_HIDDEN), fix(3)),
                  pl.BlockSpec((2, CMP_HIDDEN, HEAD_DIM), fix(3)),
                  pl.BlockSpec((3, HEAD_DIM), fix(2))],
        out_specs=(pl.BlockSpec((1, KV_HEADS, n, 2 * HEAD_DIM), lambda bi: (bi, 0, 0, 0)),
                   pl.BlockSpec((1, KV_HEADS, HEAD_DIM, n), lambda bi: (bi, 0, 0, 0))),
        out_shape=(jax.ShapeDtypeStruct((b, KV_HEADS, n, 2 * HEAD_DIM), BF16),
                   jax.ShapeDtypeStruct((b, KV_HEADS, HEAD_DIM, n), BF16)),
        compiler_params=_cparams("parallel"),
        name="compress",
    )(kch, vch, pos, wfull[:, 0], wfull[:, 1], b1, cmp_w2.astype(BF16), k_norm_w)


def _nsa_kernel(q_ref, kc_ref, vct_ref, ks_ref, vst_ref, kw_ref, vwt_ref, gt_ref, cdiff_ref, wdiff_ref,
                ovl_ref, oh_ref, onw_ref, o_ref, *, n_top):
    q0 = pl.program_id(2) * TQ
    ncols = HEADS_PER_KV * TQ
    q = q_ref[0].reshape(ncols, 2 * HEAD_DIM)
    ns = ovl_ref.shape[0]

    s = jnp.where(cdiff_ref[...] <= q0, _dot_nt(kc_ref[0, 0], q), -jnp.inf)
    m = jnp.max(s, axis=0, keepdims=True)
    m = jnp.where(m == -jnp.inf, 0.0, m)
    e = jnp.exp2(s - m)
    p = e / jnp.maximum(jnp.sum(e, axis=0, keepdims=True), 1e-30)
    o_c = _dot(vct_ref[0, 0], p.astype(BF16))

    nw = WINDOW + TQ
    start = pl.multiple_of(jnp.maximum(q0 - WINDOW, 0), TQ)
    dist = wdiff_ref[...] + (q0 - start)
    sw = jnp.where((dist >= 0) & (dist < WINDOW), _dot_nt(kw_ref[0, 0, pl.ds(start, nw), :], q), -jnp.inf)
    ew = jnp.exp2(sw - jnp.max(sw, axis=0, keepdims=True))
    o_w = _dot(vwt_ref[0, 0, :, pl.ds(start, nw)], ew.astype(BF16)) / jnp.sum(ew, axis=0, keepdims=True)

    psum = p[:, 0:TQ]
    for hh in range(1, HEADS_PER_KV):
        psum = psum + p[:, hh * TQ:(hh + 1) * TQ]
    imp = _split_dot(ovl_ref[...], psum)
    blk = lax.broadcasted_iota(I32, (ns, TQ), 0)
    tq = q0 + lax.broadcasted_iota(I32, (ns, TQ), 1)
    cur = tq >> 6
    forced = (blk == 0) | (blk == cur) | (blk == cur - 1)
    rank = jnp.where(forced, BIG, jnp.where(blk * SEL_BLOCK <= tq, imp, -BIG))

    blkf = blk.astype(F32)

    def topk_body(_, carry):
        r, bias = carry
        mx = jnp.max(r, axis=0, keepdims=True)
        first = jnp.min(jnp.where(r == mx, blkf, float(ns)), axis=0, keepdims=True)
        hit = blkf == first
        return jnp.where(hit, -jnp.inf, r), jnp.where(hit, 0.0, bias)

    _, bias = lax.fori_loop(0, n_top, topk_body, (rank, jnp.full((ns, TQ), -1e30, F32)))

    if ns < 128:
        bias = jnp.concatenate([bias, jnp.zeros((128 - ns, TQ), F32)], axis=0)
    bias_t = bias.T.astype(BF16)
    qq = jnp.concatenate([q, jnp.concatenate([bias_t] * HEADS_PER_KV, axis=0)], axis=1)
    ones_rows = jnp.ones((16, TK), BF16)

    def scores(j):
        k0 = pl.multiple_of(j * TK, TK)
        kk = jnp.concatenate([ks_ref[0, 0, pl.ds(k0, TK), :], oh_ref[pl.ds(k0, TK), :]], axis=1)
        return _dot_nt(kk, qq)

    def update(sc, j, carry):
        m_run, acc = carry
        k0 = pl.multiple_of(j * TK, TK)
        m_new = jnp.maximum(m_run, jnp.max(sc, axis=0, keepdims=True))
        ex = jnp.exp2(sc - m_new).astype(BF16)
        va = jnp.concatenate([vst_ref[0, 0, :, pl.ds(k0, TK)], ones_rows], axis=0)
        return m_new, jnp.exp2(m_run - m_new) * acc + _dot(va, ex)

    def pair(jj, carry):
        sa, sb = scores(2 * jj), scores(2 * jj + 1)
        return update(sb, 2 * jj + 1, update(sa, 2 * jj, carry))

    n_past = q0 // TK
    carry = (jnp.full((1, ncols), -1e30, F32), jnp.zeros((HEAD_DIM + 16, ncols), F32))
    carry = lax.fori_loop(0, n_past // 2, pair, carry)
    carry = lax.fori_loop((n_past // 2) * 2, n_past, lambda j, c: update(scores(j), j, c), carry)
    sc_last = jnp.where(wdiff_ref[0:TK, :] + (q0 - n_past * TK) >= 0, scores(n_past), -1e30)
    _, acc_s = update(sc_last, n_past, carry)
    o_s = acc_s[0:HEAD_DIM] / acc_s[HEAD_DIM:HEAD_DIM + 1]

    gt = gt_ref[0, 0]
    outs = []
    for hh in range(HEADS_PER_KV):
        cs = slice(hh * TQ, (hh + 1) * TQ)
        o = (gt[3 * hh:3 * hh + 1, :] * o_c[:, cs] + gt[3 * hh + 1:3 * hh + 2, :] * o_s[:, cs]
             + gt[3 * hh + 2:3 * hh + 3, :] * o_w[:, cs])
        o = o * lax.rsqrt(jnp.mean(o * o, axis=0, keepdims=True) + RMS_EPS) * onw_ref[0, hh]
        outs.append(o)
    o_ref[0] = jnp.concatenate(outs, axis=0).T


def _nsa(q, kc, vct, ks, vst, kw, vwt, gates_t, attn_out_norm_w):
    b, _, s, aw = q.shape
    nc = kc.shape[2]
    ns = s // SEL_BLOCK
    n_top = min(N_SELECT, ns)
    ncols = HEADS_PER_KV * TQ
    nw = WINDOW + TQ
    tl = np.arange(ncols)[None, :] & (TQ - 1)
    cdiff = jnp.asarray((np.arange(nc)[:, None] * CMP_STRIDE + (CMP_BLOCK - 1) - tl).astype(np.int32))
    wdiff = jnp.asarray((tl - np.arange(nw)[:, None]).astype(np.int32))
    ci = np.arange(nc)[None, :] * CMP_STRIDE
    bj = np.arange(ns)[:, None]
    ovl = ((ci < (bj + 1) * SEL_BLOCK) & (ci + CMP_BLOCK > bj * SEL_BLOCK) & (np.arange(nc)[None, :] < nc - 1))
    ovl = jnp.asarray(ovl.astype(np.float32)).astype(BF16)
    assert ns <= 128
    onehot = (np.arange(s)[:, None] // SEL_BLOCK == np.arange(128)[None, :])
    onehot = jnp.asarray(onehot.astype(np.float32)).astype(BF16)
    onw = jnp.broadcast_to(attn_out_norm_w.reshape(KV_HEADS, HEADS_PER_KV, HEAD_DIM, 1),
                           (KV_HEADS, HEADS_PER_KV, HEAD_DIM, TQ))
    gt = gates_t.reshape(b, KV_HEADS, HEADS_PER_KV * 3, s)
    per_bg = lambda bi, g, i: (bi, g, 0, 0)
    fixed = lambda bi, g, i: (0, 0)
    return pl.pallas_call(
        functools.partial(_nsa_kernel, n_top=n_top),
        grid=(b, KV_HEADS, s // TQ),
        in_specs=[pl.BlockSpec((1, HEADS_PER_KV, TQ, aw), lambda bi, g, i: (bi, g, i, 0)),
                  pl.BlockSpec((1, 1, nc, aw), per_bg),
                  pl.BlockSpec((1, 1, HEAD_DIM, nc), per_bg),
                  pl.BlockSpec((1, 1, s, aw), per_bg),
                  pl.BlockSpec((1, 1, HEAD_DIM, s), per_bg),
                  pl.BlockSpec((1, 1, s, aw), per_bg),
                  pl.BlockSpec((1, 1, HEAD_DIM, s), per_bg),
                  pl.BlockSpec((1, 1, HEADS_PER_KV * 3, TQ), lambda bi, g, i: (bi, g, 0, i)),
                  pl.BlockSpec((nc, ncols), fixed),
                  pl.BlockSpec((nw, ncols), fixed),
                  pl.BlockSpec((ns, nc), fixed),
                  pl.BlockSpec((s, 128), fixed),
                  pl.BlockSpec((1, HEADS_PER_KV, HEAD_DIM, TQ), lambda bi, g, i: (g, 0, 0, 0))],
        out_specs=pl.BlockSpec((1, TQ, HEADS_PER_KV * HEAD_DIM), lambda bi, g, i: (bi, i, g)),
        out_shape=jax.ShapeDtypeStruct((b, s, NSA_WIDTH), F32),
        compiler_params=_cparams("parallel", "parallel", "arbitrary"),
        name="nsa",
    )(q, kc, vct, ks, vst, kw, vwt, gt, cdiff, wdiff, ovl, onehot, onw)


def _hgrn_kernel(q_ref, k_ref, lf_ref, v_ref, g_ref, onw_ref, o_ref, state_scr, *, n_chunks):
    c = HGRN_CHUNK

    @pl.when(pl.program_id(1) == 0)
    def _():
        state_scr[...] = jnp.zeros_like(state_scr)

    ri = lax.broadcasted_iota(I32, (c, c), 0)
    ci = lax.broadcasted_iota(I32, (c, c), 1)
    tril = (ri >= ci).astype(F32)
    rsub = ri // HGRN_SUB
    rin = ri & (HGRN_SUB - 1)

    def head_chunk(r0, hd, state_t):
        cols = slice(hd * HGRN_DIM, (hd + 1) * HGRN_DIM)
        q = q_ref[0, pl.ds(r0, c), cols]
        k = k_ref[0, pl.ds(r0, c), cols]
        lf = lf_ref[0, pl.ds(r0, c), cols]
        v = v_ref[0, pl.ds(r0, c), cols]
        cum = _dot(tril, lf, precision=HIGHEST)
        o = _dot_nt((q * jnp.exp(cum)).astype(BF16), state_t.astype(BF16))
        scores = jnp.zeros((c, c), F32)
        for i in range(1, c // HGRN_SUB):
            ref_row = cum[i * HGRN_SUB - 1:i * HGRN_SUB, :]
            qs = q * jnp.exp(jnp.minimum(cum - ref_row, 0.0))
            kd = k * jnp.exp(jnp.minimum(ref_row - cum, 0.0))
            blk = _dot_nt(qs.astype(BF16), kd.astype(BF16))
            scores = jnp.where((rsub == i) & (ci < i * HGRN_SUB), blk, scores)
        for d in range(HGRN_SUB):
            if d == 0:
                w = jnp.sum(q * k, axis=-1, keepdims=True)
            else:
                ksh = pltpu.roll(k, d, 0)
                csh = pltpu.roll(cum, d, 0)
                w = jnp.sum(q * ksh * jnp.exp(jnp.minimum(cum - csh, 0.0)), axis=-1, keepdims=True)
            scores = jnp.where((ri - ci == d) & (rin >= d), w, scores)
        o = o + _dot(scores.astype(BF16), v.astype(BF16))
        last = cum[c - 1:c, :]
        kd = (k * jnp.exp(last - cum)).astype(BF16)
        state_t = state_t * jnp.exp(last) + _dot(v.T.astype(BF16), kd)
        o = o * g_ref[0, pl.ds(r0, c), cols]
        o = o * lax.rsqrt(jnp.mean(o * o, axis=-1, keepdims=True) + RMS_EPS) * onw_ref[:, cols]
        o_ref[0, pl.ds(r0, c), cols] = o
        return state_t

    def chunk(ck, states):
        r0 = pl.multiple_of(ck * c, c)
        return tuple(head_chunk(r0, hd, states[hd]) for hd in range(HGRN_HEADS))

    states = lax.fori_loop(0, n_chunks, chunk, tuple(state_scr[hd] for hd in range(HGRN_HEADS)))
    for hd in range(HGRN_HEADS):
        state_scr[hd] = states[hd]


def _hgrn(hq, hk, hlf, hv, hg, rec_out_norm_w, rows):
    b, s, _ = hq.shape
    blk = pl.BlockSpec((1, rows, HGRN_WIDTH), lambda bi, i: (bi, i, 0))
    return pl.pallas_call(
        functools.partial(_hgrn_kernel, n_chunks=rows // HGRN_CHUNK),
        grid=(b, s // rows),
        in_specs=[blk, blk, blk, blk, blk,
                  pl.BlockSpec((1, HGRN_WIDTH), lambda bi, i: (0, 0))],
        out_specs=blk,
        out_shape=jax.ShapeDtypeStruct((b, s, HGRN_WIDTH), F32),
        scratch_shapes=[pltpu.VMEM((HGRN_HEADS, HGRN_DIM, HGRN_DIM), F32)],
        compiler_params=_cparams("parallel", "arbitrary"),
        name="hgrn",
    )(hq, hk, hlf, hv, hg, rec_out_norm_w.reshape(1, HGRN_WIDTH))


def _outproj_kernel(x_ref, a_ref, r_ref, wa_ref, wr_ref, gt_ref, sc_ref, sh_ref, n2_ref, x1_ref, h2_ref, h2p_ref):
    mixed = _dot(a_ref[0].astype(BF16), wa_ref[...]) + _dot(r_ref[0].astype(BF16), wr_ref[...])
    x1 = x_ref[0] + gt_ref[0] * mixed
    x1_ref[0] = x1
    ms = jnp.mean(x1 * x1, axis=-1, keepdims=True)
    h2 = x1 * lax.rsqrt(ms + RMS_EPS) * n2_ref[...] * (1.0 + sc_ref[0]) + sh_ref[0]
    h2_ref[0] = h2
    h2p_ref[0] = _pack_bf16_pair(h2[:, :D_MODEL // 2], h2[:, D_MODEL // 2:])


def _outproj(x, attn, rec, w_out, gt1, sc2, sh2, norm2_w, tm):
    b, s, d = x.shape
    row = lambda bi, i: (bi, i, 0)
    per_b = lambda bi, i: (bi, 0, 0)
    fixed2 = lambda bi, i: (0, 0)
    w = w_out.astype(BF16)
    return pl.pallas_call(
        _outproj_kernel,
        grid=(b, s // tm),
        in_specs=[pl.BlockSpec((1, tm, d), row),
                  pl.BlockSpec((1, tm, NSA_WIDTH), row),
                  pl.BlockSpec((1, tm, HGRN_WIDTH), row),
                  pl.BlockSpec((NSA_WIDTH, d), fixed2),
                  pl.BlockSpec((HGRN_WIDTH, d), fixed2),
                  pl.BlockSpec((1, 1, d), per_b),
                  pl.BlockSpec((1, 1, d), per_b),
                  pl.BlockSpec((1, 1, d), per_b),
                  pl.BlockSpec((1, d), fixed2)],
        out_specs=(pl.BlockSpec((1, tm, d), row), pl.BlockSpec((1, tm, d), row), pl.BlockSpec((1, tm, d // 2), row)),
        out_shape=(jax.ShapeDtypeStruct((b, s, d), F32), jax.ShapeDtypeStruct((b, s, d), F32),
                   jax.ShapeDtypeStruct((b, s, d // 2), jnp.uint32)),
        compiler_params=_cparams("parallel", "parallel"),
        name="outproj",
    )(x, attn, rec, w[:NSA_WIDTH], w[NSA_WIDTH:], gt1, sc2, sh2, norm2_w)


def _mixer(x, c, ada_w, ada_b, norm1_w, norm2_w, w_in, q_norm_w, k_norm_w, cmp_pos, cmp_w1, cmp_b1, cmp_w2,
           attn_out_norm_w, hgrn_lb_param, rec_out_norm_w, w_out):
    b, s, d = x.shape
    mod = _mod(c, ada_w, ada_b)
    sh1, sc1, gt1, sh2, sc2, gt2 = [m.reshape(b, 1, d) for m in jnp.split(mod, 6, axis=-1)]
    o = NSA_WIDTH + 6 * KV_WIDTH
    w_cat = jnp.concatenate([w_in[:, :o], w_in[:, o:o + NSA_HEADS * 3],
                             jnp.zeros((d, GATE_PAD - NSA_HEADS * 3), w_in.dtype),
                             w_in[:, o + NSA_HEADS * 3:]], axis=1).astype(BF16)
    tm = min(256, s)
    (q, kc_raw, vc_raw, ks, vst, kw, vwt, gates_t, hq, hk, hlf, hv, hg) = _inproj(
        x, sc1, sh1, norm1_w.reshape(1, d), w_cat, q_norm_w.reshape(1, HEAD_DIM), k_norm_w, hgrn_lb_param, tm)
    kc, vct = _compress(kc_raw, vc_raw, cmp_pos, cmp_w1, cmp_b1, cmp_w2, k_norm_w)
    attn = _nsa(q, kc, vct, ks, vst, kw, vwt, gates_t, attn_out_norm_w)
    rec = _hgrn(hq, hk, hlf, hv, hg, rec_out_norm_w, min(512, s))
    x1, h2, h2p = _outproj(x, attn, rec, w_out, gt1, sc2, sh2, norm2_w.reshape(1, d), tm)
    return x1, h2, h2p, gt2


def _router_kernel(h_ref, rwt_ref, bias_ref, tri_ref, ones_ref, idx_ref, w_ref, rank_ref, cnt_ref, carry_scr, *, tr):
    @pl.when(pl.program_id(0) == 0)
    def _():
        carry_scr[...] = jnp.zeros_like(carry_scr)

    h = h_ref[...]
    h_hi = h.astype(BF16)
    h_lo = (h - h_hi.astype(F32)).astype(BF16)
    logits = _dot_nt(rwt_ref[0], h_hi) + _dot_nt(rwt_ref[1], h_hi) + _dot_nt(rwt_ref[0], h_lo)
    scores = _sigmoid(logits)
    biased = scores + bias_ref[...]
    neg = -jnp.inf

    gs = []
    for g in range(N_GROUPS):
        sub = biased[g * GROUP_SIZE:(g + 1) * GROUP_SIZE, :]
        m1 = jnp.max(sub, axis=0, keepdims=True)
        dup = jnp.sum((sub == m1).astype(F32), axis=0, keepdims=True)
        m2 = jnp.max(jnp.where(sub < m1, sub, neg), axis=0, keepdims=True)
        gs.append(m1 + jnp.where(dup >= 2.0, m1, m2))
    parts = []
    for g in range(N_GROUPS):
        beaten = jnp.zeros_like(gs[g])
        for g2 in range(N_GROUPS):
            if g2 != g:
                beats = (gs[g2] >= gs[g]) if g2 < g else (gs[g2] > gs[g])
                beaten = beaten + beats.astype(F32)
        sub = biased[g * GROUP_SIZE:(g + 1) * GROUP_SIZE, :]
        parts.append(jnp.where(beaten < float(TOPK_GROUPS), sub, neg))
    cand = jnp.concatenate(parts, axis=0)

    rowf = lax.broadcasted_iota(I32, (N_EXPERTS, tr), 0).astype(F32)
    idx_rows, w_rows, hits = [], [], []
    multi = jnp.zeros((N_EXPERTS, tr), F32)
    for _ in range(TOP_K):
        mx = jnp.max(cand, axis=0, keepdims=True)
        first = jnp.min(jnp.where(cand == mx, rowf, float(N_EXPERTS)), axis=0, keepdims=True)
        hit = rowf == first
        idx_rows.append(first)
        w_rows.append(jnp.sum(jnp.where(hit, scores, 0.0), axis=0, keepdims=True))
        cand = jnp.where(hit, neg, cand)
        multi = jnp.where(hit, 1.0, multi)
    w = jnp.concatenate(w_rows, axis=0)
    w_ref[...] = w / jnp.sum(w, axis=0, keepdims=True) * ROUTED_SCALE
    idx = jnp.concatenate(idx_rows, axis=0)
    idx_ref[...] = idx.astype(I32)

    carry = carry_scr[...]
    mb = multi.astype(BF16)
    before = _dot(mb, tri_ref[...]) + jnp.concatenate([carry] * (tr // 128), axis=1)
    rank_rows = [jnp.sum(jnp.where(rowf == idx_rows[k], before, 0.0), axis=0, keepdims=True) for k in range(TOP_K)]
    rank_ref[...] = jnp.concatenate(rank_rows, axis=0).astype(I32)
    carry = carry + _dot(mb, ones_ref[...])
    carry_scr[...] = carry
    cnt_ref[...] = carry


def _router(h2, router_w, router_bias, tr):
    t, d = h2.shape
    tri = jnp.asarray(np.triu(np.ones((tr, tr), np.float32), 1)).astype(BF16)
    ones = jnp.ones((tr, 128), BF16)
    tok = pl.BlockSpec((TOP_K, tr), lambda i: (0, i))
    fixed = lambda i: (0, 0)
    rwt = router_w.T
    rwt_hi = rwt.astype(BF16)
    rwt_split = jnp.stack([rwt_hi, (rwt - rwt_hi.astype(F32)).astype(BF16)])
    return pl.pallas_call(
        functools.partial(_router_kernel, tr=tr),
        grid=(t // tr,),
        in_specs=[pl.BlockSpec((tr, d), lambda i: (i, 0)),
                  pl.BlockSpec((2, N_EXPERTS, d), lambda i: (0, 0, 0)),
                  pl.BlockSpec((N_EXPERTS, 1), fixed),
                  pl.BlockSpec((tr, tr), fixed),
                  pl.BlockSpec((tr, 128), fixed)],
        out_specs=(tok, tok, tok, pl.BlockSpec((N_EXPERTS, 128), fixed)),
        out_shape=(jax.ShapeDtypeStruct((TOP_K, t), I32), jax.ShapeDtypeStruct((TOP_K, t), F32),
                   jax.ShapeDtypeStruct((TOP_K, t), I32), jax.ShapeDtypeStruct((N_EXPERTS, 128), F32)),
        scratch_shapes=[pltpu.VMEM((N_EXPERTS, 128), F32)],
        compiler_params=_cparams("arbitrary"),
        name="router",
    )(h2, rwt_split, router_bias.reshape(N_EXPERTS, 1), tri, ones)


def _pack_bf16_pair(a, b):
    ua = lax.bitcast_convert_type(a.astype(BF16).astype(F32), jnp.uint32)
    ub = lax.bitcast_convert_type(b.astype(BF16).astype(F32), jnp.uint32)
    return ua | (ub >> 16)


def _unpack_bf16_pair(w):
    a = lax.bitcast_convert_type(w & jnp.uint32(0xFFFF0000), F32)
    b = lax.bitcast_convert_type(w << 16, F32)
    return a, b


def _row_copy(src, s_row, dst, d_row, sem):
    return pltpu.make_async_copy(src.at[pl.ds(s_row, 1)], dst.at[pl.ds(d_row, 1)], sem)


def _dispatch_kernel(ps_ref, cnt_ref, idx_ref, rank_ref, h_ref, xs_ref, zbuf, sem, zsem, *, td):
    def issue(t, _):
        for k in range(TOP_K):
            _row_copy(h_ref, t, xs_ref, ps_ref[idx_ref[k, t]] + rank_ref[k, t], sem).start()
        return 0

    lax.fori_loop(0, td, issue, 0)

    def pad_rows(e, wait):
        pos = ps_ref[e] + cnt_ref[e]
        rem = (EXPERT_BLOCK - cnt_ref[e] % EXPERT_BLOCK) % EXPERT_BLOCK

        def one(r, _):
            cp = _row_copy(zbuf, 0, xs_ref, 0 if wait else pos + r, zsem)
            if wait:
                cp.wait()
            else:
                cp.start()
            return 0

        lax.fori_loop(0, rem, one, 0)
        return 0

    @pl.when(pl.program_id(0) == 0)
    def _():
        zbuf[...] = jnp.zeros_like(zbuf)
        lax.fori_loop(0, N_EXPERTS, lambda e, _: pad_rows(e, False), 0)
        lax.fori_loop(0, N_EXPERTS, lambda e, _: pad_rows(e, True), 0)

    def drain(t, _):
        for k in range(TOP_K):
            _row_copy(h_ref, 0, xs_ref, 0, sem).wait()
        return 0

    lax.fori_loop(0, td, drain, 0)


def _dispatch(h2p, pad_start, counts, idx, rank, n_rows, td):
    t, dw = h2p.shape
    tok = pl.BlockSpec((TOP_K, td), lambda i, ps, cn: (0, i), memory_space=pltpu.SMEM)
    return pl.pallas_call(
        functools.partial(_dispatch_kernel, td=td),
        grid_spec=pltpu.PrefetchScalarGridSpec(
            num_scalar_prefetch=2,
            grid=(t // td,),
            in_specs=[tok, tok, pl.BlockSpec((td, dw), lambda i, ps, cn: (i, 0))],
            out_specs=pl.BlockSpec(memory_space=pl.ANY),
            scratch_shapes=[pltpu.VMEM((8, dw), h2p.dtype),
                            pltpu.SemaphoreType.DMA, pltpu.SemaphoreType.DMA]),
        out_shape=jax.ShapeDtypeStruct((n_rows, dw), h2p.dtype),
        compiler_params=pltpu.CompilerParams(dimension_semantics=("arbitrary",), has_side_effects=True),
        name="dispatch",
    )(pad_start, counts, idx, rank, h2p)


def _experts_kernel(be_ref, nu_ref, xs_ref, wg_ref, wu_ref, wd_ref, ys_ref):
    i = pl.program_id(0)
    half = D_MODEL // 2

    @pl.when(i < nu_ref[0])
    def _():
        xa, xb = _unpack_bf16_pair(xs_ref[...])
        xa, xb = xa.astype(BF16), xb.astype(BF16)
        g = _dot(xa, wg_ref[0, :half].astype(BF16)) + _dot(xb, wg_ref[0, half:].astype(BF16))
        u = _dot(xa, wu_ref[0, :half].astype(BF16)) + _dot(xb, wu_ref[0, half:].astype(BF16))
        act = (g * _sigmoid(g) * u).astype(BF16)
        y = _dot(act, wd_ref[0].astype(BF16))
        ys_ref[...] = _pack_bf16_pair(y[:, :half], y[:, half:])

    @pl.when(i >= nu_ref[0])
    def _():
        ys_ref[...] = jnp.zeros_like(ys_ref)


def _experts(xs, blk_e, n_used, w_gate, w_up, w_down):
    n_rows, dw = xs.shape
    d = w_gate.shape[1]
    nblk = n_rows // EXPERT_BLOCK
    row_map = lambda i, be, nu: (jnp.minimum(i, nu[0] - 1), 0)
    w_map = lambda i, be, nu: (be[i], 0, 0)
    return pl.pallas_call(
        _experts_kernel,
        grid_spec=pltpu.PrefetchScalarGridSpec(
            num_scalar_prefetch=2,
            grid=(nblk,),
            in_specs=[pl.BlockSpec((EXPERT_BLOCK, dw), row_map),
                      pl.BlockSpec((1, d, EXPERT_FF), w_map),
                      pl.BlockSpec((1, d, EXPERT_FF), w_map),
                      pl.BlockSpec((1, EXPERT_FF, d), w_map)],
            out_specs=pl.BlockSpec((EXPERT_BLOCK, dw), lambda i, be, nu: (i, 0))),
        out_shape=jax.ShapeDtypeStruct((n_rows, dw), xs.dtype),
        compiler_params=_cparams("arbitrary"),
        name="experts",
    )(blk_e, n_used, xs, w_gate, w_up, w_down)


def _combine_kernel(ps_ref, idx_ref, rank_ref, x1_ref, h_ref, w_ref, gt_ref, sg_ref, su_ref, sd_ref, ys_ref,
                    o_ref, buf, sem, *, tc):
    half = D_MODEL // 2

    def issue(t, _):
        for k in range(TOP_K):
            _row_copy(ys_ref, ps_ref[idx_ref[k, t]] + rank_ref[k, t], buf.at[k], t, sem).start()
        return 0

    lax.fori_loop(0, tc, issue, 0)

    hb = h_ref[...].astype(BF16)
    g = _dot(hb, sg_ref[...])
    u = _dot(hb, su_ref[...])
    ffn = _dot((g * _sigmoid(g) * u).astype(BF16), sd_ref[...])

    def drain(t, _):
        for k in range(TOP_K):
            _row_copy(ys_ref, 0, buf.at[k], 0, sem).wait()
        return 0

    lax.fori_loop(0, tc, drain, 0)

    w = w_ref[...]
    ra = jnp.zeros((tc, half), F32)
    rb = jnp.zeros((tc, half), F32)
    for k in range(TOP_K):
        ya, yb = _unpack_bf16_pair(buf[k])
        ra = ra + w[:, k:k + 1] * ya
        rb = rb + w[:, k:k + 1] * yb
    ffn = ffn + jnp.concatenate([ra, rb], axis=1)
    o_ref[...] = x1_ref[...] + gt_ref[0] * ffn


def _combine(x1, h2, w_tok, gt2, pad_start, idx, rank, ys, sg, su, sd, seq, tc):
    t, d = x1.shape
    row = lambda i, ps: (i, 0)
    fixed = lambda i, ps: (0, 0)
    tok = pl.BlockSpec((TOP_K, tc), lambda i, ps: (0, i), memory_space=pltpu.SMEM)
    return pl.pallas_call(
        functools.partial(_combine_kernel, tc=tc),
        grid_spec=pltpu.PrefetchScalarGridSpec(
            num_scalar_prefetch=1,
            grid=(t // tc,),
            in_specs=[tok, tok,
                      pl.BlockSpec((tc, d), row),
                      pl.BlockSpec((tc, d), row),
                      pl.BlockSpec((tc, TOP_K), row),
                      pl.BlockSpec((1, 1, d), lambda i, ps: ((i * tc) // seq, 0, 0)),
                      pl.BlockSpec((d, SHARED_FF), fixed),
                      pl.BlockSpec((d, SHARED_FF), fixed),
                      pl.BlockSpec((SHARED_FF, d), fixed),
                      pl.BlockSpec(memory_space=pl.ANY)],
            out_specs=pl.BlockSpec((tc, d), row),
            scratch_shapes=[pltpu.VMEM((TOP_K, tc, d // 2), jnp.uint32), pltpu.SemaphoreType.DMA]),
        out_shape=jax.ShapeDtypeStruct((t, d), F32),
        compiler_params=_cparams("arbitrary"),
        name="combine",
    )(pad_start, idx, rank, x1, h2, w_tok, gt2, sg.astype(BF16), su.astype(BF16), sd.astype(BF16), ys)


def _moe_parts(x1, h2, h2p, gt2, router_w, router_bias, w_gate, w_up, w_down, sg, su, sd):
    b, s, d = x1.shape
    t = b * s
    h2 = h2.reshape(t, d)
    idx, w, rank, cnt = _router(h2, router_w, router_bias, min(256, t))
    counts = cnt[:, 0].astype(I32)
    padded = (counts + EXPERT_BLOCK - 1) // EXPERT_BLOCK * EXPERT_BLOCK
    pad_end = jnp.cumsum(padded)
    pad_start = pad_end - padded
    n_rows = t * TOP_K + N_EXPERTS * EXPERT_BLOCK
    nblk = n_rows // EXPERT_BLOCK
    n_used = (pad_end[-1:] // EXPERT_BLOCK).astype(I32)
    blk_e = jnp.searchsorted(pad_end, jnp.arange(nblk, dtype=I32) * EXPERT_BLOCK, side='right')
    blk_e = jnp.minimum(blk_e, N_EXPERTS - 1).astype(I32)
    blk_e = jnp.where(jnp.arange(nblk) < n_used[0], blk_e, blk_e[jnp.maximum(n_used[0] - 1, 0)])
    xs = _dispatch(h2p.reshape(t, d // 2), pad_start.astype(I32), counts, idx, rank, n_rows, min(1024, t))
    ys = _experts(xs, blk_e, n_used, w_gate, w_up, w_down)
    out = _combine(x1.reshape(t, d), h2, w.T, gt2, pad_start.astype(I32), idx, rank, ys, sg, su, sd, s, min(128, t))
    return out.reshape(b, s, d), dict(idx=idx, w=w, rank=rank, cnt=cnt)


def kernel(x, c, ada_w, ada_b, norm1_w, norm2_w, w_in, q_norm_w, k_norm_w, cmp_pos, cmp_w1, cmp_b1, cmp_w2, attn_out_norm_w, hgrn_lb_param, rec_out_norm_w, w_out, router_w, router_bias, exp_w_gate, exp_w_up, exp_w_down, shared_w_gate, shared_w_up, shared_w_down):
    assert ada_w.shape[0] == 1, "one layer"
    assert x.shape[0] <= 8 and x.shape[1] % TK == 0 and x.shape[1] >= WINDOW + TQ
    l = 0
    x1, h2, h2p, gt2 = _mixer(x, c, ada_w[l], ada_b[l], norm1_w[l], norm2_w[l], w_in[l], q_norm_w[l], k_norm_w[l],
                         cmp_pos[l], cmp_w1[l], cmp_b1[l], cmp_w2[l], attn_out_norm_w[l], hgrn_lb_param,
                         rec_out_norm_w[l], w_out[l])
    out, _ = _moe_parts(x1, h2, h2p, gt2, router_w[l], router_bias[l], exp_w_gate[l], exp_w_up[l], exp_w_down[l],
                        shared_w_gate[l], shared_w_up[l], shared_w_down[l])
    return out
```

```python
import functools

import numpy as np
import jax
import jax.numpy as jnp
from jax import lax
from jax.experimental import pallas as pl
from jax.experimental.pallas import tpu as pltpu
from jax.experimental.pallas import tpu_sc as plsc

F32 = jnp.float32
BF16 = jnp.bfloat16
I32 = jnp.int32

D_MODEL = 1024
NSA_HEADS = 8
HEAD_DIM = 64
NSA_WIDTH = NSA_HEADS * HEAD_DIM
KV_HEADS = 2
HEADS_PER_KV = NSA_HEADS // KV_HEADS
KV_WIDTH = KV_HEADS * HEAD_DIM
CMP_BLOCK = 32
CMP_STRIDE = 16
CMP_HIDDEN = 256
SEL_BLOCK = 64
N_SELECT = 16
WINDOW = 512
HGRN_HEADS = 4
HGRN_DIM = 128
HGRN_WIDTH = HGRN_HEADS * HGRN_DIM
HGRN_CHUNK = 64
HGRN_SUB = 16
N_EXPERTS = 256
TOP_K = 8
N_GROUPS = 8
GROUP_SIZE = N_EXPERTS // N_GROUPS
TOPK_GROUPS = 4
EXPERT_FF = 256
SHARED_FF = 256
ROUTED_SCALE = 2.5
RMS_EPS = 1e-6
BIG = 1e9
LOG2E = 1.4426950408889634
GATE_PAD = 128
PROJ_COLS = NSA_WIDTH + 6 * KV_WIDTH + GATE_PAD + 4 * HGRN_WIDTH

VMEM_LIMIT = 56 * 1024 * 1024

TQ = 128
TK = 512
EXPERT_BLOCK = 256
HIGHEST = lax.Precision.HIGHEST


def _cparams(*sem):
    return pltpu.CompilerParams(dimension_semantics=sem, vmem_limit_bytes=VMEM_LIMIT)


def _sigmoid(x):
    return 1.0 / (1.0 + jnp.exp(-x))


def _dot_nt(a, b):
    return lax.dot_general(a, b, (((1,), (1,)), ((), ())), preferred_element_type=F32)


def _dot(a, b, **kw):
    return jnp.dot(a, b, preferred_element_type=F32, **kw)


def _split_dot(a_bf16_exact, x):
    hi = x.astype(BF16)
    lo = (x - hi.astype(F32)).astype(BF16)
    return _dot(a_bf16_exact, hi) + _dot(a_bf16_exact, lo)


def _mod_kernel(c_ref, w_ref, b_ref, o_ref):
    c = c_ref[...]
    cond = c * _sigmoid(c)
    o_ref[...] = _dot(cond, w_ref[...], precision=HIGHEST) + b_ref[...]


def _mod(c, ada_w, ada_b):
    b, d = c.shape
    rows = 8
    c_pad = jnp.zeros((rows, d), F32).at[:b].set(c)
    n = ada_w.shape[1]
    out = pl.pallas_call(
        _mod_kernel,
        grid=(n // d,),
        in_specs=[pl.BlockSpec((rows, d), lambda j: (0, 0)),
                  pl.BlockSpec((d, d), lambda j: (0, j)),
                  pl.BlockSpec((1, d), lambda j: (0, j))],
        out_specs=pl.BlockSpec((rows, d), lambda j: (0, j)),
        out_shape=jax.ShapeDtypeStruct((rows, n), F32),
        compiler_params=_cparams("parallel"),
        name="mod",
    )(c_pad, ada_w, ada_b.reshape(1, n))
    return out[:b]


def _head_rms(t, w):
    return t * lax.rsqrt(jnp.mean(t * t, axis=-1, keepdims=True) + RMS_EPS) * w


def _pos_digits(pos):
    lane = lax.broadcasted_iota(I32, pos.shape, 1)
    d0 = (lane == 0) | (lane == 3) | (lane == 6)
    d1 = (lane == 1) | (lane == 4) | (lane == 7)
    d2 = (lane == 2) | (lane == 5) | (lane == 8)
    dig = jnp.where(d0, pos >> 12, jnp.where(d1, (pos >> 6) & 63, jnp.where(d2, pos & 63, 0)))
    return dig.astype(F32)


def _inproj_kernel(x_ref, sc_ref, sh_ref, n1_ref, w_ref, qnw_ref, knw_ref, lbp_ref, qaug_ref,
                   q_ref, kcr_ref, vcr_ref, ks_ref, vst_ref, kw_ref, vwt_ref, gt_ref,
                   hq_ref, hk_ref, hlf_ref, hv_ref, hg_ref):
    x = x_ref[0]
    ms = jnp.mean(x * x, axis=-1, keepdims=True)
    h = x * lax.rsqrt(ms + RMS_EPS) * n1_ref[...] * (1.0 + sc_ref[0]) + sh_ref[0]
    p = _dot(h.astype(BF16), w_ref[...])
    tm = x.shape[0]

    qnw = qnw_ref[...]
    for hd in range(NSA_HEADS):
        t = p[:, hd * HEAD_DIM:(hd + 1) * HEAD_DIM]
        qn = _head_rms(t, qnw) * (HEAD_DIM ** -0.5 * LOG2E)
        qa = jnp.broadcast_to(qaug_ref[hd:hd + 1, :], (tm, HEAD_DIM))
        q_ref[0, hd] = jnp.concatenate([qn, qa], axis=1).astype(BF16)
    kaug = _pos_digits(pl.program_id(1) * tm + lax.broadcasted_iota(I32, (tm, HEAD_DIM), 0))

    o = NSA_WIDTH
    kcr_ref[0] = p[:, o:o + KV_WIDTH]
    vcr_ref[0] = p[:, o + KV_WIDTH:o + 2 * KV_WIDTH]
    ks = p[:, o + 2 * KV_WIDTH:o + 3 * KV_WIDTH]
    vs = p[:, o + 3 * KV_WIDTH:o + 4 * KV_WIDTH]
    kw = p[:, o + 4 * KV_WIDTH:o + 5 * KV_WIDTH]
    vw = p[:, o + 5 * KV_WIDTH:o + 6 * KV_WIDTH]
    for g in range(KV_HEADS):
        sl = slice(g * HEAD_DIM, (g + 1) * HEAD_DIM)
        ks_ref[0, g] = jnp.concatenate([_head_rms(ks[:, sl], knw_ref[1:2, :]), kaug], axis=1).astype(BF16)
        kw_ref[0, g] = jnp.concatenate([_head_rms(kw[:, sl], knw_ref[2:3, :]), kaug], axis=1).astype(BF16)
    vst = vs.T.astype(BF16)
    vwt = vw.T.astype(BF16)
    for g in range(KV_HEADS):
        vst_ref[0, g] = vst[g * HEAD_DIM:(g + 1) * HEAD_DIM, :]
        vwt_ref[0, g] = vwt[g * HEAD_DIM:(g + 1) * HEAD_DIM, :]

    o = NSA_WIDTH + 6 * KV_WIDTH
    gates = _sigmoid(p[:, o:o + GATE_PAD])
    gt_ref[0] = gates.T[:NSA_HEADS * 3, :]

    o = o + GATE_PAD
    hq = p[:, o:o + HGRN_WIDTH]
    hf = p[:, o + HGRN_WIDTH:o + 2 * HGRN_WIDTH]
    hi = p[:, o + 2 * HGRN_WIDTH:o + 3 * HGRN_WIDTH]
    hg = p[:, o + 3 * HGRN_WIDTH:o + 4 * HGRN_WIDTH]
    lbp = lbp_ref[...]
    e = jnp.exp(lbp - jnp.max(lbp, axis=0, keepdims=True))
    lb = e[0:1, :] / jnp.sum(e, axis=0, keepdims=True)
    f = lb + (1.0 - lb) * _sigmoid(hf)
    hq_ref[0] = hq * _sigmoid(hq) * (HGRN_DIM ** -0.5)
    hk_ref[0] = 1.0 - f
    hlf_ref[0] = jnp.log(f)
    hv_ref[0] = hi
    hg_ref[0] = _sigmoid(hg)


def _inproj(x, sc1, sh1, norm1_w, w_cat, q_norm_w, k_norm_w, lb_param, tm):
    b, s, d = x.shape
    row = lambda bi, i: (bi, i, 0)
    per_b = lambda bi, i: (bi, 0, 0)
    fixed2 = lambda bi, i: (0, 0)
    aw = 2 * HEAD_DIM
    rest = np.array([2.0 ** (-8.0 * (i + 1) / NSA_HEADS) for i in range(NSA_HEADS)], np.float64) * LOG2E
    qaug = np.zeros((NSA_HEADS, HEAD_DIM), np.float32)
    for i in range(3):
        term = rest.astype(np.float32).astype(BF16).astype(np.float64)
        rest = rest - term
        for dgt, wgt in enumerate((4096.0, 64.0, 1.0)):
            qaug[:, 3 * i + dgt] = term * wgt
    assert np.all(qaug == qaug.astype(BF16).astype(np.float32))
    out_shape = (
        jax.ShapeDtypeStruct((b, NSA_HEADS, s, aw), BF16),
        jax.ShapeDtypeStruct((b, s, KV_WIDTH), F32),
        jax.ShapeDtypeStruct((b, s, KV_WIDTH), F32),
        jax.ShapeDtypeStruct((b, KV_HEADS, s, aw), BF16),
        jax.ShapeDtypeStruct((b, KV_HEADS, HEAD_DIM, s), BF16),
        jax.ShapeDtypeStruct((b, KV_HEADS, s, aw), BF16),
        jax.ShapeDtypeStruct((b, KV_HEADS, HEAD_DIM, s), BF16),
        jax.ShapeDtypeStruct((b, NSA_HEADS * 3, s), F32),
    ) + tuple(jax.ShapeDtypeStruct((b, s, HGRN_WIDTH), F32) for _ in range(5))
    hm = lambda n, w: pl.BlockSpec((1, n, tm, w), lambda bi, i: (bi, 0, i, 0))
    hmt = lambda n, w: pl.BlockSpec((1, n, w, tm), lambda bi, i: (bi, 0, 0, i))
    out_specs = (
        hm(NSA_HEADS, aw),
        pl.BlockSpec((1, tm, KV_WIDTH), row),
        pl.BlockSpec((1, tm, KV_WIDTH), row),
        hm(KV_HEADS, aw), hmt(KV_HEADS, HEAD_DIM),
        hm(KV_HEADS, aw), hmt(KV_HEADS, HEAD_DIM),
        pl.BlockSpec((1, NSA_HEADS * 3, tm), lambda bi, i: (bi, 0, i)),
    ) + tuple(pl.BlockSpec((1, tm, HGRN_WIDTH), row) for _ in range(5))
    return pl.pallas_call(
        _inproj_kernel,
        grid=(b, s // tm),
        in_specs=[pl.BlockSpec((1, tm, d), row),
                  pl.BlockSpec((1, 1, d), per_b),
                  pl.BlockSpec((1, 1, d), per_b),
                  pl.BlockSpec((1, d), fixed2),
                  pl.BlockSpec((d, PROJ_COLS), fixed2),
                  pl.BlockSpec((1, HEAD_DIM), fixed2),
                  pl.BlockSpec((3, HEAD_DIM), fixed2),
                  pl.BlockSpec(lb_param.shape, fixed2),
                  pl.BlockSpec((NSA_HEADS, HEAD_DIM), fixed2)],
        out_specs=out_specs,
        out_shape=out_shape,
        compiler_params=_cparams("parallel", "parallel"),
        name="inproj",
    )(x, sc1, sh1, norm1_w, w_cat, q_norm_w, k_norm_w, lb_param, jnp.asarray(qaug))


def _gelu_tanh(x):
    return 0.5 * x * (1.0 + jnp.tanh(0.7978845608028654 * (x + 0.044715 * x * x * x)))


def _compress_kernel(kch_ref, vch_ref, pos_ref, wa_ref, wb_ref, b1_ref, w2_ref, knw_ref,
                     kc_ref, vct_ref):
    n = kch_ref.shape[1]
    outs = []
    for br, ch_ref in enumerate((kch_ref, vch_ref)):
        ch = ch_ref[0]
        a = _dot((ch + pos_ref[br, 0:1, :]).astype(BF16), wa_ref[br])
        bm = _dot((ch + pos_ref[br, 1:2, :]).astype(BF16), wb_ref[br])
        pre = a + pltpu.roll(bm, n - 1, 0) + b1_ref[br]
        hid = _gelu_tanh(pre).astype(BF16)
        outs.append([_dot(hid[:, g * CMP_HIDDEN:(g + 1) * CMP_HIDDEN], w2_ref[br]) for g in range(KV_HEADS)])
    end_digits = _pos_digits(lax.broadcasted_iota(I32, (n, HEAD_DIM), 0) * CMP_STRIDE + (CMP_BLOCK - 1))
    for g in range(KV_HEADS):
        kc_ref[0, g] = jnp.concatenate([_head_rms(outs[0][g], knw_ref[0:1, :]), end_digits], axis=1).astype(BF16)
    vct = jnp.concatenate(outs[1], axis=1).T.astype(BF16)
    for g in range(KV_HEADS):
        vct_ref[0, g] = vct[g * HEAD_DIM:(g + 1) * HEAD_DIM, :]


def _compress(kc_raw, vc_raw, cmp_pos, cmp_w1, cmp_b1, cmp_w2, k_norm_w):
    b, s, _ = kc_raw.shape
    n = s // CMP_STRIDE
    half = CMP_STRIDE
    cw = CMP_STRIDE * KV_WIDTH
    kch = kc_raw.reshape(b, n, cw)
    vch = vc_raw.reshape(b, n, cw)
    pos = cmp_pos.reshape(2, 2, half, 1, HEAD_DIM)
    pos = jnp.broadcast_to(pos, (2, 2, half, KV_HEADS, HEAD_DIM)).reshape(2, 2, cw)
    w1 = cmp_w1.reshape(2, 2, half, HEAD_DIM, CMP_HIDDEN)
    eye = jnp.eye(KV_HEADS, dtype=F32)
    wfull = jnp.einsum('rhjdn,gk->rhjgdkn', w1, eye).reshape(2, 2, cw, KV_HEADS * CMP_HIDDEN).astype(BF16)
    b1 = jnp.tile(cmp_b1.reshape(2, 1, CMP_HIDDEN), (1, 1, KV_HEADS))
    fix = lambda r: (lambda bi: (0,) * r)
    return pl.pallas_call(
        _compress_kernel,
        grid=(b,),
        in_specs=[pl.BlockSpec((1, n, cw), lambda bi: (bi, 0, 0)),
                  pl.BlockSpec((1, n, cw), lambda bi: (bi, 0, 0)),
                  pl.BlockSpec((2, 2, cw), fix(3)),
                  pl.BlockSpec((2, cw, KV_HEADS * CMP_HIDDEN), fix(3)),
                  pl.BlockSpec((2, cw, KV_HEADS * CMP_HIDDEN), fix(3)),
                  pl.BlockSpec((2, 1, KV_HEADS * CMP_HIDDEN), fix(3)),
                  pl.BlockSpec((2, CMP_HIDDEN, HEAD_DIM), fix(3)),
                  pl.BlockSpec((3, HEAD_DIM), fix(2))],
        out_specs=(pl.BlockSpec((1, KV_HEADS, n, 2 * HEAD_DIM), lambda bi: (bi, 0, 0, 0)),
                   pl.BlockSpec((1, KV_HEADS, HEAD_DIM, n), lambda bi: (bi, 0, 0, 0))),
        out_shape=(jax.ShapeDtypeStruct((b, KV_HEADS, n, 2 * HEAD_DIM), BF16),
                   jax.ShapeDtypeStruct((b, KV_HEADS, HEAD_DIM, n), BF16)),
        compiler_params=_cparams("parallel"),
        name="compress",
    )(kch, vch, pos, wfull[:, 0], wfull[:, 1], b1, cmp_w2.astype(BF16), k_norm_w)


def _nsa_kernel(q_ref, kc_ref, vct_ref, ks_ref, vst_ref, kw_ref, vwt_ref, gt_ref, cdiff_ref, wdiff_ref,
                ovl_ref, oh_ref, onw_ref, o_ref, *, n_top):
    q0 = pl.program_id(2) * TQ
    ncols = HEADS_PER_KV * TQ
    q = q_ref[0].reshape(ncols, 2 * HEAD_DIM)
    ns = ovl_ref.shape[0]

    s = jnp.where(cdiff_ref[...] <= q0, _dot_nt(kc_ref[0, 0], q), -jnp.inf)
    m = jnp.max(s, axis=0, keepdims=True)
    m = jnp.where(m == -jnp.inf, 0.0, m)
    e = jnp.exp2(s - m)
    p = e / jnp.maximum(jnp.sum(e, axis=0, keepdims=True), 1e-30)
    o_c = _dot(vct_ref[0, 0], p.astype(BF16))

    nw = WINDOW + TQ
    start = pl.multiple_of(jnp.maximum(q0 - WINDOW, 0), TQ)
    dist = wdiff_ref[...] + (q0 - start)
    sw = jnp.where((dist >= 0) & (dist < WINDOW), _dot_nt(kw_ref[0, 0, pl.ds(start, nw), :], q), -jnp.inf)
    ew = jnp.exp2(sw - jnp.max(sw, axis=0, keepdims=True))
    o_w = _dot(vwt_ref[0, 0, :, pl.ds(start, nw)], ew.astype(BF16)) / jnp.sum(ew, axis=0, keepdims=True)

    psum = p[:, 0:TQ]
    for hh in range(1, HEADS_PER_KV):
        psum = psum + p[:, hh * TQ:(hh + 1) * TQ]
    imp = _split_dot(ovl_ref[...], psum)
    blk = lax.broadcasted_iota(I32, (ns, TQ), 0)
    tq = q0 + lax.broadcasted_iota(I32, (ns, TQ), 1)
    cur = tq >> 6
    forced = (blk == 0) | (blk == cur) | (blk == cur - 1)
    rank = jnp.where(forced, BIG, jnp.where(blk * SEL_BLOCK <= tq, imp, -BIG))

    blkf = blk.astype(F32)

    def topk_body(_, carry):
        r, bias = carry
        mx = jnp.max(r, axis=0, keepdims=True)
        first = jnp.min(jnp.where(r == mx, blkf, float(ns)), axis=0, keepdims=True)
        hit = blkf == first
        return jnp.where(hit, -jnp.inf, r), jnp.where(hit, 0.0, bias)

    _, bias = lax.fori_loop(0, n_top, topk_body, (rank, jnp.full((ns, TQ), -1e30, F32)))

    if ns < 128:
        bias = jnp.concatenate([bias, jnp.zeros((128 - ns, TQ), F32)], axis=0)
    bias_t = bias.T.astype(BF16)
    qq = jnp.concatenate([q, jnp.concatenate([bias_t] * HEADS_PER_KV, axis=0)], axis=1)
    ones_rows = jnp.ones((16, TK), BF16)

    def scores(j):
        k0 = pl.multiple_of(j * TK, TK)
        kk = jnp.concatenate([ks_ref[0, 0, pl.ds(k0, TK), :], oh_ref[pl.ds(k0, TK), :]], axis=1)
        return _dot_nt(kk, qq)

    def update(sc, j, carry):
        m_run, acc = carry
        k0 = pl.multiple_of(j * TK, TK)
        m_new = jnp.maximum(m_run, jnp.max(sc, axis=0, keepdims=True))
        ex = jnp.exp2(sc - m_new).astype(BF16)
        va = jnp.concatenate([vst_ref[0, 0, :, pl.ds(k0, TK)], ones_rows], axis=0)
        return m_new, jnp.exp2(m_run - m_new) * acc + _dot(va, ex)

    def pair(jj, carry):
        sa, sb = scores(2 * jj), scores(2 * jj + 1)
        return update(sb, 2 * jj + 1, update(sa, 2 * jj, carry))

    n_past = q0 // TK
    carry = (jnp.full((1, ncols), -1e30, F32), jnp.zeros((HEAD_DIM + 16, ncols), F32))
    carry = lax.fori_loop(0, n_past // 2, pair, carry)
    carry = lax.fori_loop((n_past // 2) * 2, n_past, lambda j, c: update(scores(j), j, c), carry)
    sc_last = jnp.where(wdiff_ref[0:TK, :] + (q0 - n_past * TK) >= 0, scores(n_past), -1e30)
    _, acc_s = update(sc_last, n_past, carry)
    o_s = acc_s[0:HEAD_DIM] / acc_s[HEAD_DIM:HEAD_DIM + 1]

    gt = gt_ref[0, 0]
    outs = []
    for hh in range(HEADS_PER_KV):
        cs = slice(hh * TQ, (hh + 1) * TQ)
        o = (gt[3 * hh:3 * hh + 1, :] * o_c[:, cs] + gt[3 * hh + 1:3 * hh + 2, :] * o_s[:, cs]
             + gt[3 * hh + 2:3 * hh + 3, :] * o_w[:, cs])
        o = o * lax.rsqrt(jnp.mean(o * o, axis=0, keepdims=True) + RMS_EPS) * onw_ref[0, hh]
        outs.append(o)
    o_ref[0] = jnp.concatenate(outs, axis=0).T


def _nsa(q, kc, vct, ks, vst, kw, vwt, gates_t, attn_out_norm_w):
    b, _, s, aw = q.shape
    nc = kc.shape[2]
    ns = s // SEL_BLOCK
    n_top = min(N_SELECT, ns)
    ncols = HEADS_PER_KV * TQ
    nw = WINDOW + TQ
    tl = np.arange(ncols)[None, :] & (TQ - 1)
    cdiff = jnp.asarray((np.arange(nc)[:, None] * CMP_STRIDE + (CMP_BLOCK - 1) - tl).astype(np.int32))
    wdiff = jnp.asarray((tl - np.arange(nw)[:, None]).astype(np.int32))
    ci = np.arange(nc)[None, :] * CMP_STRIDE
    bj = np.arange(ns)[:, None]
    ovl = ((ci < (bj + 1) * SEL_BLOCK) & (ci + CMP_BLOCK > bj * SEL_BLOCK) & (np.arange(nc)[None, :] < nc - 1))
    ovl = jnp.asarray(ovl.astype(np.float32)).astype(BF16)
    assert ns <= 128
    onehot = (np.arange(s)[:, None] // SEL_BLOCK == np.arange(128)[None, :])
    onehot = jnp.asarray(onehot.astype(np.float32)).astype(BF16)
    onw = jnp.broadcast_to(attn_out_norm_w.reshape(KV_HEADS, HEADS_PER_KV, HEAD_DIM, 1),
                           (KV_HEADS, HEADS_PER_KV, HEAD_DIM, TQ))
    gt = gates_t.reshape(b, KV_HEADS, HEADS_PER_KV * 3, s)
    per_bg = lambda bi, g, i: (bi, g, 0, 0)
    fixed = lambda bi, g, i: (0, 0)
    return pl.pallas_call(
        functools.partial(_nsa_kernel, n_top=n_top),
        grid=(b, KV_HEADS, s // TQ),
        in_specs=[pl.BlockSpec((1, HEADS_PER_KV, TQ, aw), lambda bi, g, i: (bi, g, i, 0)),
                  pl.BlockSpec((1, 1, nc, aw), per_bg),
                  pl.BlockSpec((1, 1, HEAD_DIM, nc), per_bg),
                  pl.BlockSpec((1, 1, s, aw), per_bg),
                  pl.BlockSpec((1, 1, HEAD_DIM, s), per_bg),
                  pl.BlockSpec((1, 1, s, aw), per_bg),
                  pl.BlockSpec((1, 1, HEAD_DIM, s), per_bg),
                  pl.BlockSpec((1, 1, HEADS_PER_KV * 3, TQ), lambda bi, g, i: (bi, g, 0, i)),
                  pl.BlockSpec((nc, ncols), fixed),
                  pl.BlockSpec((nw, ncols), fixed),
                  pl.BlockSpec((ns, nc), fixed),
                  pl.BlockSpec((s, 128), fixed),
                  pl.BlockSpec((1, HEADS_PER_KV, HEAD_DIM, TQ), lambda bi, g, i: (g, 0, 0, 0))],
        out_specs=pl.BlockSpec((1, TQ, HEADS_PER_KV * HEAD_DIM), lambda bi, g, i: (bi, i, g)),
        out_shape=jax.ShapeDtypeStruct((b, s, NSA_WIDTH), F32),
        compiler_params=_cparams("parallel", "parallel", "arbitrary"),
        name="nsa",
    )(q, kc, vct, ks, vst, kw, vwt, gt, cdiff, wdiff, ovl, onehot, onw)


def _hgrn_kernel(q_ref, k_ref, lf_ref, v_ref, g_ref, onw_ref, o_ref, state_scr, *, n_chunks):
    c = HGRN_CHUNK

    @pl.when(pl.program_id(1) == 0)
    def _():
        state_scr[...] = jnp.zeros_like(state_scr)

    ri = lax.broadcasted_iota(I32, (c, c), 0)
    ci = lax.broadcasted_iota(I32, (c, c), 1)
    tril = (ri >= ci).astype(F32)
    rsub = ri // HGRN_SUB
    rin = ri & (HGRN_SUB - 1)

    def head_chunk(r0, hd, state_t):
        cols = slice(hd * HGRN_DIM, (hd + 1) * HGRN_DIM)
        q = q_ref[0, pl.ds(r0, c), cols]
        k = k_ref[0, pl.ds(r0, c), cols]
        lf = lf_ref[0, pl.ds(r0, c), cols]
        v = v_ref[0, pl.ds(r0, c), cols]
        cum = _dot(tril, lf, precision=HIGHEST)
        o = _dot_nt((q * jnp.exp(cum)).astype(BF16), state_t.astype(BF16))
        scores = jnp.zeros((c, c), F32)
        for i in range(1, c // HGRN_SUB):
            ref_row = cum[i * HGRN_SUB - 1:i * HGRN_SUB, :]
            qs = q * jnp.exp(jnp.minimum(cum - ref_row, 0.0))
            kd = k * jnp.exp(jnp.minimum(ref_row - cum, 0.0))
            blk = _dot_nt(qs.astype(BF16), kd.astype(BF16))
            scores = jnp.where((rsub == i) & (ci < i * HGRN_SUB), blk, scores)
        for d in range(HGRN_SUB):
            if d == 0:
                w = jnp.sum(q * k, axis=-1, keepdims=True)
            else:
                ksh = pltpu.roll(k, d, 0)
                csh = pltpu.roll(cum, d, 0)
                w = jnp.sum(q * ksh * jnp.exp(jnp.minimum(cum - csh, 0.0)), axis=-1, keepdims=True)
            scores = jnp.where((ri - ci == d) & (rin >= d), w, scores)
        o = o + _dot(scores.astype(BF16), v.astype(BF16))
        last = cum[c - 1:c, :]
        kd = (k * jnp.exp(last - cum)).astype(BF16)
        state_t = state_t * jnp.exp(last) + _dot(v.T.astype(BF16), kd)
        o = o * g_ref[0, pl.ds(r0, c), cols]
        o = o * lax.rsqrt(jnp.mean(o * o, axis=-1, keepdims=True) + RMS_EPS) * onw_ref[:, cols]
        o_ref[0, pl.ds(r0, c), cols] = o
        return state_t

    def chunk(ck, states):
        r0 = pl.multiple_of(ck * c, c)
        return tuple(head_chunk(r0, hd, states[hd]) for hd in range(HGRN_HEADS))

    states = lax.fori_loop(0, n_chunks, chunk, tuple(state_scr[hd] for hd in range(HGRN_HEADS)))
    for hd in range(HGRN_HEADS):
        state_scr[hd] = states[hd]


def _hgrn(hq, hk, hlf, hv, hg, rec_out_norm_w, rows):
    b, s, _ = hq.shape
    blk = pl.BlockSpec((1, rows, HGRN_WIDTH), lambda bi, i: (bi, i, 0))
    return pl.pallas_call(
        functools.partial(_hgrn_kernel, n_chunks=rows // HGRN_CHUNK),
        grid=(b, s // rows),
        in_specs=[blk, blk, blk, blk, blk,
                  pl.BlockSpec((1, HGRN_WIDTH), lambda bi, i: (0, 0))],
        out_specs=blk,
        out_shape=jax.ShapeDtypeStruct((b, s, HGRN_WIDTH), F32),
        scratch_shapes=[pltpu.VMEM((HGRN_HEADS, HGRN_DIM, HGRN_DIM), F32)],
        compiler_params=_cparams("parallel", "arbitrary"),
        name="hgrn",
    )(hq, hk, hlf, hv, hg, rec_out_norm_w.reshape(1, HGRN_WIDTH))


def _outproj_kernel(x_ref, a_ref, r_ref, wa_ref, wr_ref, gt_ref, sc_ref, sh_ref, n2_ref, x1_ref, h2_ref, h2p_ref):
    mixed = _dot(a_ref[0].astype(BF16), wa_ref[...]) + _dot(r_ref[0].astype(BF16), wr_ref[...])
    x1 = x_ref[0] + gt_ref[0] * mixed
    x1_ref[0] = x1
    ms = jnp.mean(x1 * x1, axis=-1, keepdims=True)
    h2 = x1 * lax.rsqrt(ms + RMS_EPS) * n2_ref[...] * (1.0 + sc_ref[0]) + sh_ref[0]
    h2_ref[0] = h2
    h2p_ref[0] = _pack_bf16_pair(h2[:, :D_MODEL // 2], h2[:, D_MODEL // 2:])


def _outproj(x, attn, rec, w_out, gt1, sc2, sh2, norm2_w, tm):
    b, s, d = x.shape
    row = lambda bi, i: (bi, i, 0)
    per_b = lambda bi, i: (bi, 0, 0)
    fixed2 = lambda bi, i: (0, 0)
    w = w_out.astype(BF16)
    return pl.pallas_call(
        _outproj_kernel,
        grid=(b, s // tm),
        in_specs=[pl.BlockSpec((1, tm, d), row),
                  pl.BlockSpec((1, tm, NSA_WIDTH), row),
                  pl.BlockSpec((1, tm, HGRN_WIDTH), row),
                  pl.BlockSpec((NSA_WIDTH, d), fixed2),
                  pl.BlockSpec((HGRN_WIDTH, d), fixed2),
                  pl.BlockSpec((1, 1, d), per_b),
                  pl.BlockSpec((1, 1, d), per_b),
                  pl.BlockSpec((1, 1, d), per_b),
                  pl.BlockSpec((1, d), fixed2)],
        out_specs=(pl.BlockSpec((1, tm, d), row), pl.BlockSpec((1, tm, d), row), pl.BlockSpec((1, tm, d // 2), row)),
        out_shape=(jax.ShapeDtypeStruct((b, s, d), F32), jax.ShapeDtypeStruct((b, s, d), F32),
                   jax.ShapeDtypeStruct((b, s, d // 2), jnp.uint32)),
        compiler_params=_cparams("parallel", "parallel"),
        name="outproj",
    )(x, attn, rec, w[:NSA_WIDTH], w[NSA_WIDTH:], gt1, sc2, sh2, norm2_w)


def _mixer(x, c, ada_w, ada_b, norm1_w, norm2_w, w_in, q_norm_w, k_norm_w, cmp_pos, cmp_w1, cmp_b1, cmp_w2,
           attn_out_norm_w, hgrn_lb_param, rec_out_norm_w, w_out):
    b, s, d = x.shape
    mod = _mod(c, ada_w, ada_b)
    sh1, sc1, gt1, sh2, sc2, gt2 = [m.reshape(b, 1, d) for m in jnp.split(mod, 6, axis=-1)]
    o = NSA_WIDTH + 6 * KV_WIDTH
    w_cat = jnp.concatenate([w_in[:, :o], w_in[:, o:o + NSA_HEADS * 3],
                             jnp.zeros((d, GATE_PAD - NSA_HEADS * 3), w_in.dtype),
                             w_in[:, o + NSA_HEADS * 3:]], axis=1).astype(BF16)
    tm = min(256, s)
    (q, kc_raw, vc_raw, ks, vst, kw, vwt, gates_t, hq, hk, hlf, hv, hg) = _inproj(
        x, sc1, sh1, norm1_w.reshape(1, d), w_cat, q_norm_w.reshape(1, HEAD_DIM), k_norm_w, hgrn_lb_param, tm)
    kc, vct = _compress(kc_raw, vc_raw, cmp_pos, cmp_w1, cmp_b1, cmp_w2, k_norm_w)
    attn = _nsa(q, kc, vct, ks, vst, kw, vwt, gates_t, attn_out_norm_w)
    rec = _hgrn(hq, hk, hlf, hv, hg, rec_out_norm_w, min(512, s))
    x1, h2, h2p = _outproj(x, attn, rec, w_out, gt1, sc2, sh2, norm2_w.reshape(1, d), tm)
    return x1, h2, h2p, gt2


def _router_kernel(h_ref, rwt_ref, bias_ref, tri_ref, ones_ref, idx_ref, w_ref, rank_ref, cnt_ref, carry_scr, *, tr):
    @pl.when(pl.program_id(0) == 0)
    def _():
        carry_scr[...] = jnp.zeros_like(carry_scr)

    h = h_ref[...]
    h_hi = h.astype(BF16)
    h_lo = (h - h_hi.astype(F32)).astype(BF16)
    logits = _dot_nt(rwt_ref[0], h_hi) + _dot_nt(rwt_ref[1], h_hi) + _dot_nt(rwt_ref[0], h_lo)
    scores = _sigmoid(logits)
    biased = scores + bias_ref[...]
    neg = -jnp.inf

    gs = []
    for g in range(N_GROUPS):
        sub = biased[g * GROUP_SIZE:(g + 1) * GROUP_SIZE, :]
        m1 = jnp.max(sub, axis=0, keepdims=True)
        dup = jnp.sum((sub == m1).astype(F32), axis=0, keepdims=True)
        m2 = jnp.max(jnp.where(sub < m1, sub, neg), axis=0, keepdims=True)
        gs.append(m1 + jnp.where(dup >= 2.0, m1, m2))
    parts = []
    for g in range(N_GROUPS):
        beaten = jnp.zeros_like(gs[g])
        for g2 in range(N_GROUPS):
            if g2 != g:
                beats = (gs[g2] >= gs[g]) if g2 < g else (gs[g2] > gs[g])
                beaten = beaten + beats.astype(F32)
        sub = biased[g * GROUP_SIZE:(g + 1) * GROUP_SIZE, :]
        parts.append(jnp.where(beaten < float(TOPK_GROUPS), sub, neg))
    cand = jnp.concatenate(parts, axis=0)

    rowf = lax.broadcasted_iota(I32, (N_EXPERTS, tr), 0).astype(F32)
    idx_rows, w_rows, hits = [], [], []
    multi = jnp.zeros((N_EXPERTS, tr), F32)
    for _ in range(TOP_K):
        mx = jnp.max(cand, axis=0, keepdims=True)
        first = jnp.min(jnp.where(cand == mx, rowf, float(N_EXPERTS)), axis=0, keepdims=True)
        hit = rowf == first
        idx_rows.append(first)
        w_rows.append(jnp.sum(jnp.where(hit, scores, 0.0), axis=0, keepdims=True))
        cand = jnp.where(hit, neg, cand)
        multi = jnp.where(hit, 1.0, multi)
    w = jnp.concatenate(w_rows, axis=0)
    w_ref[...] = w / jnp.sum(w, axis=0, keepdims=True) * ROUTED_SCALE
    idx = jnp.concatenate(idx_rows, axis=0)
    idx_ref[...] = idx.astype(I32)

    carry = carry_scr[...]
    mb = multi.astype(BF16)
    before = _dot(mb, tri_ref[...]) + jnp.concatenate([carry] * (tr // 128), axis=1)
    rank_rows = [jnp.sum(jnp.where(rowf == idx_rows[k], before, 0.0), axis=0, keepdims=True) for k in range(TOP_K)]
    rank_ref[...] = jnp.concatenate(rank_rows, axis=0).astype(I32)
    carry = carry + _dot(mb, ones_ref[...])
    carry_scr[...] = carry
    cnt_ref[...] = carry


def _router(h2, router_w, router_bias, tr):
    t, d = h2.shape
    tri = jnp.asarray(np.triu(np.ones((tr, tr), np.float32), 1)).astype(BF16)
    ones = jnp.ones((tr, 128), BF16)
    tok = pl.BlockSpec((TOP_K, tr), lambda i: (0, i))
    fixed = lambda i: (0, 0)
    rwt = router_w.T
    rwt_hi = rwt.astype(BF16)
    rwt_split = jnp.stack([rwt_hi, (rwt - rwt_hi.astype(F32)).astype(BF16)])
    return pl.pallas_call(
        functools.partial(_router_kernel, tr=tr),
        grid=(t // tr,),
        in_specs=[pl.BlockSpec((tr, d), lambda i: (i, 0)),
                  pl.BlockSpec((2, N_EXPERTS, d), lambda i: (0, 0, 0)),
                  pl.BlockSpec((N_EXPERTS, 1), fixed),
                  pl.BlockSpec((tr, tr), fixed),
                  pl.BlockSpec((tr, 128), fixed)],
        out_specs=(tok, tok, tok, pl.BlockSpec((N_EXPERTS, 128), fixed)),
        out_shape=(jax.ShapeDtypeStruct((TOP_K, t), I32), jax.ShapeDtypeStruct((TOP_K, t), F32),
                   jax.ShapeDtypeStruct((TOP_K, t), I32), jax.ShapeDtypeStruct((N_EXPERTS, 128), F32)),
        scratch_shapes=[pltpu.VMEM((N_EXPERTS, 128), F32)],
        compiler_params=_cparams("arbitrary"),
        name="router",
    )(h2, rwt_split, router_bias.reshape(N_EXPERTS, 1), tri, ones)


def _pack_bf16_pair(a, b):
    ua = lax.bitcast_convert_type(a.astype(BF16).astype(F32), jnp.uint32)
    ub = lax.bitcast_convert_type(b.astype(BF16).astype(F32), jnp.uint32)
    return ua | (ub >> 16)


def _unpack_bf16_pair(w):
    a = lax.bitcast_convert_type(w & jnp.uint32(0xFFFF0000), F32)
    b = lax.bitcast_convert_type(w << 16, F32)
    return a, b


def _slot_kernel(ps_ref, idx_ref, rank_ref, slot_ref):
    idx = idx_ref[...]

    def body(e, acc):
        return jnp.where(idx == e, ps_ref[e], acc)

    slot_ref[...] = lax.fori_loop(0, N_EXPERTS, body, jnp.zeros_like(idx)) + rank_ref[...]


def _slots(pad_start, idx, rank, tt):
    t = idx.shape[1]
    tok = pl.BlockSpec((TOP_K, tt), lambda i, ps: (0, i))
    return pl.pallas_call(
        _slot_kernel,
        grid_spec=pltpu.PrefetchScalarGridSpec(num_scalar_prefetch=1, grid=(t // tt,),
                                               in_specs=[tok, tok], out_specs=tok),
        out_shape=jax.ShapeDtypeStruct((TOP_K, t), I32),
        compiler_params=_cparams("parallel"),
        name="slots",
    )(pad_start, idx, rank)


SC_CORES = 2
SC_SUBCORES = 16
SC_CHUNK = 64


def _sc_mesh():
    return plsc.VectorSubcoreMesh(core_axis_name="c", subcore_axis_name="s")


def _sc_dispatch(h2p, slot_chunks, n_rows):
    t, dw = h2p.shape
    per = slot_chunks.shape[0] // (SC_CORES * SC_SUBCORES)

    def body(h_hbm, slot_hbm, xs_hbm, idx_v, rows_v, sem):
        wid = lax.axis_index("s") * SC_CORES + lax.axis_index("c")

        @pl.loop(0, per)
        def _(c):
            ch = wid * per + c
            pltpu.sync_copy(slot_hbm.at[ch], idx_v)
            pltpu.sync_copy(h_hbm.at[pl.ds(ch * SC_CHUNK, SC_CHUNK)], rows_v)
            copies = [pltpu.async_copy(rows_v, xs_hbm.at[idx_v.at[k]], sem) for k in range(TOP_K)]
            for cp in copies:
                cp.wait()

    return pl.kernel(
        body, out_type=jax.ShapeDtypeStruct((n_rows, dw), h2p.dtype), mesh=_sc_mesh(),
        scratch_types=[pltpu.VMEM((TOP_K, SC_CHUNK), I32), pltpu.VMEM((SC_CHUNK, dw), h2p.dtype),
                       pltpu.SemaphoreType.DMA],
    )(h2p, slot_chunks)


def _sc_gather(ys, slot_chunks, t):
    dw = ys.shape[1]
    per = slot_chunks.shape[0] // (SC_CORES * SC_SUBCORES)

    def body(ys_hbm, slot_hbm, yg_hbm, idx_v, rows_v, sem):
        wid = lax.axis_index("s") * SC_CORES + lax.axis_index("c")

        @pl.loop(0, per)
        def _(c):
            ch = wid * per + c
            pltpu.sync_copy(slot_hbm.at[ch], idx_v)
            for k in range(TOP_K):
                pltpu.async_copy(ys_hbm.at[idx_v.at[k]], rows_v, sem).wait()
                pltpu.sync_copy(rows_v, yg_hbm.at[k, pl.ds(ch * SC_CHUNK, SC_CHUNK)])

    return pl.kernel(
        body, out_type=jax.ShapeDtypeStruct((TOP_K, t, dw), ys.dtype), mesh=_sc_mesh(),
        scratch_types=[pltpu.VMEM((TOP_K, SC_CHUNK), I32), pltpu.VMEM((SC_CHUNK, dw), ys.dtype),
                       pltpu.SemaphoreType.DMA],
    )(ys, slot_chunks)


def _experts_kernel(be_ref, nu_ref, bv_ref, xs_ref, wg_ref, wu_ref, wd_ref, ys_ref):
    i = pl.program_id(0)
    half = D_MODEL // 2

    @pl.when(i < nu_ref[0])
    def _():
        live = lax.broadcasted_iota(I32, xs_ref.shape, 0) < bv_ref[i]
        xa, xb = _unpack_bf16_pair(jnp.where(live, xs_ref[...], jnp.uint32(0)))
        xa, xb = xa.astype(BF16), xb.astype(BF16)
        g = _dot(xa, wg_ref[0, :half].astype(BF16)) + _dot(xb, wg_ref[0, half:].astype(BF16))
        u = _dot(xa, wu_ref[0, :half].astype(BF16)) + _dot(xb, wu_ref[0, half:].astype(BF16))
        act = (g * _sigmoid(g) * u).astype(BF16)
        y = _dot(act, wd_ref[0].astype(BF16))
        ys_ref[...] = _pack_bf16_pair(y[:, :half], y[:, half:])

    @pl.when(i >= nu_ref[0])
    def _():
        ys_ref[...] = jnp.zeros_like(ys_ref)


def _experts(xs, blk_e, n_used, blk_valid, w_gate, w_up, w_down):
    n_rows, dw = xs.shape
    d = w_gate.shape[1]
    nblk = n_rows // EXPERT_BLOCK
    row_map = lambda i, be, nu, bv: (jnp.minimum(i, nu[0] - 1), 0)
    w_map = lambda i, be, nu, bv: (be[i], 0, 0)
    return pl.pallas_call(
        _experts_kernel,
        grid_spec=pltpu.PrefetchScalarGridSpec(
            num_scalar_prefetch=3,
            grid=(nblk,),
            in_specs=[pl.BlockSpec((EXPERT_BLOCK, dw), row_map),
                      pl.BlockSpec((1, d, EXPERT_FF), w_map),
                      pl.BlockSpec((1, d, EXPERT_FF), w_map),
                      pl.BlockSpec((1, EXPERT_FF, d), w_map)],
            out_specs=pl.BlockSpec((EXPERT_BLOCK, dw), lambda i, be, nu, bv: (i, 0))),
        out_shape=jax.ShapeDtypeStruct((n_rows, dw), xs.dtype),
        compiler_params=_cparams("arbitrary"),
        name="experts",
    )(blk_e, n_used, blk_valid, xs, w_gate, w_up, w_down)


def _combine_kernel(x1_ref, h_ref, w_ref, gt_ref, sg_ref, su_ref, sd_ref, yg_ref, o_ref):
    tc = x1_ref.shape[0]
    half = D_MODEL // 2
    hb = h_ref[...].astype(BF16)
    g = _dot(hb, sg_ref[...])
    u = _dot(hb, su_ref[...])
    ffn = _dot((g * _sigmoid(g) * u).astype(BF16), sd_ref[...])

    w = w_ref[...]
    ra = jnp.zeros((tc, half), F32)
    rb = jnp.zeros((tc, half), F32)
    for k in range(TOP_K):
        ya, yb = _unpack_bf16_pair(yg_ref[k])
        ra = ra + w[:, k:k + 1] * ya
        rb = rb + w[:, k:k + 1] * yb
    ffn = ffn + jnp.concatenate([ra, rb], axis=1)
    o_ref[...] = x1_ref[...] + gt_ref[0] * ffn


def _combine(x1, h2, w_tok, gt2, yg, sg, su, sd, seq, tc):
    t, d = x1.shape
    row = lambda i: (i, 0)
    fixed = lambda i: (0, 0)
    return pl.pallas_call(
        _combine_kernel,
        grid=(t // tc,),
        in_specs=[pl.BlockSpec((tc, d), row),
                  pl.BlockSpec((tc, d), row),
                  pl.BlockSpec((tc, TOP_K), row),
                  pl.BlockSpec((1, 1, d), lambda i: ((i * tc) // seq, 0, 0)),
                  pl.BlockSpec((d, SHARED_FF), fixed),
                  pl.BlockSpec((d, SHARED_FF), fixed),
                  pl.BlockSpec((SHARED_FF, d), fixed),
                  pl.BlockSpec((TOP_K, tc, d // 2), lambda i: (0, i, 0))],
        out_specs=pl.BlockSpec((tc, d), row),
        out_shape=jax.ShapeDtypeStruct((t, d), F32),
        compiler_params=_cparams("parallel"),
        name="combine",
    )(x1, h2, w_tok, gt2, sg.astype(BF16), su.astype(BF16), sd.astype(BF16), yg)


def _moe_parts(x1, h2, h2p, gt2, router_w, router_bias, w_gate, w_up, w_down, sg, su, sd):
    b, s, d = x1.shape
    t = b * s
    h2 = h2.reshape(t, d)
    idx, w, rank, cnt = _router(h2, router_w, router_bias, min(256, t))
    counts = cnt[:, 0].astype(I32)
    padded = (counts + EXPERT_BLOCK - 1) // EXPERT_BLOCK * EXPERT_BLOCK
    pad_end = jnp.cumsum(padded)
    pad_start = pad_end - padded
    n_rows = t * TOP_K + N_EXPERTS * EXPERT_BLOCK
    nblk = n_rows // EXPERT_BLOCK
    n_used = (pad_end[-1:] // EXPERT_BLOCK).astype(I32)
    blk_e = jnp.searchsorted(pad_end, jnp.arange(nblk, dtype=I32) * EXPERT_BLOCK, side='right')
    blk_e = jnp.minimum(blk_e, N_EXPERTS - 1).astype(I32)
    blk_e = jnp.where(jnp.arange(nblk) < n_used[0], blk_e, blk_e[jnp.maximum(n_used[0] - 1, 0)])
    blk_start = jnp.arange(nblk, dtype=I32) * EXPERT_BLOCK
    blk_valid = jnp.clip(counts[blk_e] - (blk_start - pad_start[blk_e]), 0, EXPERT_BLOCK).astype(I32)
    slot = _slots(pad_start.astype(I32), idx, rank, min(2048, t))
    slot_chunks = slot.reshape(TOP_K, t // SC_CHUNK, SC_CHUNK).transpose(1, 0, 2)
    xs = _sc_dispatch(h2p.reshape(t, d // 2), slot_chunks, n_rows)
    ys = _experts(xs, blk_e, n_used, blk_valid, w_gate, w_up, w_down)
    yg = _sc_gather(ys, slot_chunks, t)
    out = _combine(x1.reshape(t, d), h2, w.T, gt2, yg, sg, su, sd, s, min(256, t))
    return out.reshape(b, s, d), dict(idx=idx, w=w, rank=rank, cnt=cnt)


def kernel(x, c, ada_w, ada_b, norm1_w, norm2_w, w_in, q_norm_w, k_norm_w, cmp_pos, cmp_w1, cmp_b1, cmp_w2, attn_out_norm_w, hgrn_lb_param, rec_out_norm_w, w_out, router_w, router_bias, exp_w_gate, exp_w_up, exp_w_down, shared_w_gate, shared_w_up, shared_w_down):
    assert ada_w.shape[0] == 1, "one layer"
    assert x.shape[0] <= 8 and x.shape[1] % TK == 0 and x.shape[1] >= WINDOW + TQ
    l = 0
    x1, h2, h2p, gt2 = _mixer(x, c, ada_w[l], ada_b[l], norm1_w[l], norm2_w[l], w_in[l], q_norm_w[l], k_norm_w[l],
                         cmp_pos[l], cmp_w1[l], cmp_b1[l], cmp_w2[l], attn_out_norm_w[l], hgrn_lb_param,
                         rec_out_norm_w[l], w_out[l])
    out, _ = _moe_parts(x1, h2, h2p, gt2, router_w[l], router_bias[l], exp_w_gate[l], exp_w_up[l], exp_w_down[l],
                        shared_w_gate[l], shared_w_up[l], shared_w_down[l])
    return out
```

```python
import functools

import numpy as np
import jax
import jax.numpy as jnp
from jax import lax
from jax.experimental import pallas as pl
from jax.experimental.pallas import tpu as pltpu
from jax.experimental.pallas import tpu_sc as plsc

F32 = jnp.float32
BF16 = jnp.bfloat16
I32 = jnp.int32

D_MODEL = 1024
NSA_HEADS = 8
HEAD_DIM = 64
NSA_WIDTH = NSA_HEADS * HEAD_DIM
KV_HEADS = 2
HEADS_PER_KV = NSA_HEADS // KV_HEADS
KV_WIDTH = KV_HEADS * HEAD_DIM
CMP_BLOCK = 32
CMP_STRIDE = 16
CMP_HIDDEN = 256
SEL_BLOCK = 64
N_SELECT = 16
WINDOW = 512
HGRN_HEADS = 4
HGRN_DIM = 128
HGRN_WIDTH = HGRN_HEADS * HGRN_DIM
HGRN_CHUNK = 64
HGRN_SUB = 16
N_EXPERTS = 256
TOP_K = 8
N_GROUPS = 8
GROUP_SIZE = N_EXPERTS // N_GROUPS
TOPK_GROUPS = 4
EXPERT_FF = 256
SHARED_FF = 256
ROUTED_SCALE = 2.5
RMS_EPS = 1e-6
BIG = 1e9
LOG2E = 1.4426950408889634
GATE_PAD = 128
PROJ_COLS = NSA_WIDTH + 6 * KV_WIDTH + GATE_PAD + 4 * HGRN_WIDTH

VMEM_LIMIT = 56 * 1024 * 1024

TQ = 128
TK = 512
EXPERT_BLOCK = 256
HIGHEST = lax.Precision.HIGHEST


def _cparams(*sem):
    return pltpu.CompilerParams(dimension_semantics=sem, vmem_limit_bytes=VMEM_LIMIT)


def _sigmoid(x):
    return 1.0 / (1.0 + jnp.exp(-x))


def _dot_nt(a, b):
    return lax.dot_general(a, b, (((1,), (1,)), ((), ())), preferred_element_type=F32)


def _dot(a, b, **kw):
    return jnp.dot(a, b, preferred_element_type=F32, **kw)


def _split_dot(a_bf16_exact, x):
    hi = x.astype(BF16)
    lo = (x - hi.astype(F32)).astype(BF16)
    return _dot(a_bf16_exact, hi) + _dot(a_bf16_exact, lo)


def _mod_kernel(c_ref, w_ref, b_ref, o_ref):
    c = c_ref[...]
    cond = c * _sigmoid(c)
    o_ref[...] = _dot(cond, w_ref[...], precision=HIGHEST) + b_ref[...]


def _mod(c, ada_w, ada_b):
    b, d = c.shape
    rows = 8
    c_pad = jnp.zeros((rows, d), F32).at[:b].set(c)
    n = ada_w.shape[1]
    out = pl.pallas_call(
        _mod_kernel,
        grid=(n // d,),
        in_specs=[pl.BlockSpec((rows, d), lambda j: (0, 0)),
                  pl.BlockSpec((d, d), lambda j: (0, j)),
                  pl.BlockSpec((1, d), lambda j: (0, j))],
        out_specs=pl.BlockSpec((rows, d), lambda j: (0, j)),
        out_shape=jax.ShapeDtypeStruct((rows, n), F32),
        compiler_params=_cparams("parallel"),
        name="mod",
    )(c_pad, ada_w, ada_b.reshape(1, n))
    return out[:b]


def _head_rms(t, w):
    return t * lax.rsqrt(jnp.mean(t * t, axis=-1, keepdims=True) + RMS_EPS) * w


def _pos_digits(pos):
    lane = lax.broadcasted_iota(I32, pos.shape, 1)
    d0 = (lane == 0) | (lane == 3) | (lane == 6)
    d1 = (lane == 1) | (lane == 4) | (lane == 7)
    d2 = (lane == 2) | (lane == 5) | (lane == 8)
    dig = jnp.where(d0, pos >> 12, jnp.where(d1, (pos >> 6) & 63, jnp.where(d2, pos & 63, 0)))
    return dig.astype(F32)


def _inproj_kernel(x_ref, sc_ref, sh_ref, n1_ref, w_ref, qnw_ref, knw_ref, lbp_ref, qaug_ref,
                   q_ref, kcr_ref, vcr_ref, ks_ref, vst_ref, kw_ref, vwt_ref, gt_ref,
                   hq_ref, hk_ref, hlf_ref, hv_ref, hg_ref):
    x = x_ref[0]
    ms = jnp.mean(x * x, axis=-1, keepdims=True)
    h = x * lax.rsqrt(ms + RMS_EPS) * n1_ref[...] * (1.0 + sc_ref[0]) + sh_ref[0]
    p = _dot(h.astype(BF16), w_ref[...])
    tm = x.shape[0]

    qnw = qnw_ref[...]
    for hd in range(NSA_HEADS):
        t = p[:, hd * HEAD_DIM:(hd + 1) * HEAD_DIM]
        qn = _head_rms(t, qnw) * (HEAD_DIM ** -0.5 * LOG2E)
        qa = jnp.broadcast_to(qaug_ref[hd:hd + 1, :], (tm, HEAD_DIM))
        q_ref[0, hd] = jnp.concatenate([qn, qa], axis=1).astype(BF16)
    kaug = _pos_digits(pl.program_id(1) * tm + lax.broadcasted_iota(I32, (tm, HEAD_DIM), 0))

    o = NSA_WIDTH
    kcr_ref[0] = p[:, o:o + KV_WIDTH]
    vcr_ref[0] = p[:, o + KV_WIDTH:o + 2 * KV_WIDTH]
    ks = p[:, o + 2 * KV_WIDTH:o + 3 * KV_WIDTH]
    vs = p[:, o + 3 * KV_WIDTH:o + 4 * KV_WIDTH]
    kw = p[:, o + 4 * KV_WIDTH:o + 5 * KV_WIDTH]
    vw = p[:, o + 5 * KV_WIDTH:o + 6 * KV_WIDTH]
    for g in range(KV_HEADS):
        sl = slice(g * HEAD_DIM, (g + 1) * HEAD_DIM)
        ks_ref[0, g] = jnp.concatenate([_head_rms(ks[:, sl], knw_ref[1:2, :]), kaug], axis=1).astype(BF16)
        kw_ref[0, g] = jnp.concatenate([_head_rms(kw[:, sl], knw_ref[2:3, :]), kaug], axis=1).astype(BF16)
    vst = vs.T.astype(BF16)
    vwt = vw.T.astype(BF16)
    for g in range(KV_HEADS):
        vst_ref[0, g] = vst[g * HEAD_DIM:(g + 1) * HEAD_DIM, :]
        vwt_ref[0, g] = vwt[g * HEAD_DIM:(g + 1) * HEAD_DIM, :]

    o = NSA_WIDTH + 6 * KV_WIDTH
    gates = _sigmoid(p[:, o:o + GATE_PAD])
    gt_ref[0] = gates.T[:NSA_HEADS * 3, :]

    o = o + GATE_PAD
    hq = p[:, o:o + HGRN_WIDTH]
    hf = p[:, o + HGRN_WIDTH:o + 2 * HGRN_WIDTH]
    hi = p[:, o + 2 * HGRN_WIDTH:o + 3 * HGRN_WIDTH]
    hg = p[:, o + 3 * HGRN_WIDTH:o + 4 * HGRN_WIDTH]
    lbp = lbp_ref[...]
    e = jnp.exp(lbp - jnp.max(lbp, axis=0, keepdims=True))
    lb = e[0:1, :] / jnp.sum(e, axis=0, keepdims=True)
    f = lb + (1.0 - lb) * _sigmoid(hf)
    hq_ref[0] = hq * _sigmoid(hq) * (HGRN_DIM ** -0.5)
    hk_ref[0] = 1.0 - f
    hlf_ref[0] = jnp.log(f)
    hv_ref[0] = hi
    hg_ref[0] = _sigmoid(hg)


def _inproj(x, sc1, sh1, norm1_w, w_cat, q_norm_w, k_norm_w, lb_param, tm):
    b, s, d = x.shape
    row = lambda bi, i: (bi, i, 0)
    per_b = lambda bi, i: (bi, 0, 0)
    fixed2 = lambda bi, i: (0, 0)
    aw = 2 * HEAD_DIM
    rest = np.array([2.0 ** (-8.0 * (i + 1) / NSA_HEADS) for i in range(NSA_HEADS)], np.float64) * LOG2E
    qaug = np.zeros((NSA_HEADS, HEAD_DIM), np.float32)
    for i in range(3):
        term = rest.astype(np.float32).astype(BF16).astype(np.float64)
        rest = rest - term
        for dgt, wgt in enumerate((4096.0, 64.0, 1.0)):
            qaug[:, 3 * i + dgt] = term * wgt
    assert np.all(qaug == qaug.astype(BF16).astype(np.float32))
    out_shape = (
        jax.ShapeDtypeStruct((b, NSA_HEADS, s, aw), BF16),
        jax.ShapeDtypeStruct((b, s, KV_WIDTH), F32),
        jax.ShapeDtypeStruct((b, s, KV_WIDTH), F32),
        jax.ShapeDtypeStruct((b, KV_HEADS, s, aw), BF16),
        jax.ShapeDtypeStruct((b, KV_HEADS, HEAD_DIM, s), BF16),
        jax.ShapeDtypeStruct((b, KV_HEADS, s, aw), BF16),
        jax.ShapeDtypeStruct((b, KV_HEADS, HEAD_DIM, s), BF16),
        jax.ShapeDtypeStruct((b, NSA_HEADS * 3, s), F32),
    ) + tuple(jax.ShapeDtypeStruct((b, s, HGRN_WIDTH), F32) for _ in range(5))
    hm = lambda n, w: pl.BlockSpec((1, n, tm, w), lambda bi, i: (bi, 0, i, 0))
    hmt = lambda n, w: pl.BlockSpec((1, n, w, tm), lambda bi, i: (bi, 0, 0, i))
    out_specs = (
        hm(NSA_HEADS, aw),
        pl.BlockSpec((1, tm, KV_WIDTH), row),
        pl.BlockSpec((1, tm, KV_WIDTH), row),
        hm(KV_HEADS, aw), hmt(KV_HEADS, HEAD_DIM),
        hm(KV_HEADS, aw), hmt(KV_HEADS, HEAD_DIM),
        pl.BlockSpec((1, NSA_HEADS * 3, tm), lambda bi, i: (bi, 0, i)),
    ) + tuple(pl.BlockSpec((1, tm, HGRN_WIDTH), row) for _ in range(5))
    return pl.pallas_call(
        _inproj_kernel,
        grid=(b, s // tm),
        in_specs=[pl.BlockSpec((1, tm, d), row),
                  pl.BlockSpec((1, 1, d), per_b),
                  pl.BlockSpec((1, 1, d), per_b),
                  pl.BlockSpec((1, d), fixed2),
                  pl.BlockSpec((d, PROJ_COLS), fixed2),
                  pl.BlockSpec((1, HEAD_DIM), fixed2),
                  pl.BlockSpec((3, HEAD_DIM), fixed2),
                  pl.BlockSpec(lb_param.shape, fixed2),
                  pl.BlockSpec((NSA_HEADS, HEAD_DIM), fixed2)],
        out_specs=out_specs,
        out_shape=out_shape,
        compiler_params=_cparams("parallel", "parallel"),
        name="inproj",
    )(x, sc1, sh1, norm1_w, w_cat, q_norm_w, k_norm_w, lb_param, jnp.asarray(qaug))


def _gelu_tanh(x):
    return 0.5 * x * (1.0 + jnp.tanh(0.7978845608028654 * (x + 0.044715 * x * x * x)))


def _compress_kernel(kch_ref, vch_ref, pos_ref, wa_ref, wb_ref, b1_ref, w2_ref, knw_ref,
                     kc_ref, vct_ref):
    n = kch_ref.shape[1]
    outs = []
    for br, ch_ref in enumerate((kch_ref, vch_ref)):
        ch = ch_ref[0]
        a = _dot((ch + pos_ref[br, 0:1, :]).astype(BF16), wa_ref[br])
        bm = _dot((ch + pos_ref[br, 1:2, :]).astype(BF16), wb_ref[br])
        pre = a + pltpu.roll(bm, n - 1, 0) + b1_ref[br]
        hid = _gelu_tanh(pre).astype(BF16)
        outs.append([_dot(hid[:, g * CMP_HIDDEN:(g + 1) * CMP_HIDDEN], w2_ref[br]) for g in range(KV_HEADS)])
    end_digits = _pos_digits(lax.broadcasted_iota(I32, (n, HEAD_DIM), 0) * CMP_STRIDE + (CMP_BLOCK - 1))
    for g in range(KV_HEADS):
        kc_ref[0, g] = jnp.concatenate([_head_rms(outs[0][g], knw_ref[0:1, :]), end_digits], axis=1).astype(BF16)
    vct = jnp.concatenate(outs[1], axis=1).T.astype(BF16)
    for g in range(KV_HEADS):
        vct_ref[0, g] = vct[g * HEAD_DIM:(g + 1) * HEAD_DIM, :]


def _compress(kc_raw, vc_raw, cmp_pos, cmp_w1, cmp_b1, cmp_w2, k_norm_w):
    b, s, _ = kc_raw.shape
    n = s // CMP_STRIDE
    half = CMP_STRIDE
    cw = CMP_STRIDE * KV_WIDTH
    kch = kc_raw.reshape(b, n, cw)
    vch = vc_raw.reshape(b, n, cw)
    pos = cmp_pos.reshape(2, 2, half, 1, HEAD_DIM)
    pos = jnp.broadcast_to(pos, (2, 2, half, KV_HEADS, HEAD_DIM)).reshape(2, 2, cw)
    w1 = cmp_w1.reshape(2, 2, half, HEAD_DIM, CMP_HIDDEN)
    eye = jnp.eye(KV_HEADS, dtype=F32)
    wfull = jnp.einsum('rhjdn,gk->rhjgdkn', w1, eye).reshape(2, 2, cw, KV_HEADS * CMP_HIDDEN).astype(BF16)
    b1 = jnp.tile(cmp_b1.reshape(2, 1, CMP_HIDDEN), (1, 1, KV_HEADS))
    fix = lambda r: (lambda bi: (0,) * r)
    return pl.pallas_call(
        _compress_kernel,
        grid=(b,),
        in_specs=[pl.BlockSpec((1, n, cw), lambda bi: (bi, 0, 0)),
                  pl.BlockSpec((1, n, cw), lambda bi: (bi, 0, 0)),
                  pl.BlockSpec((2, 2, cw), fix(3)),
                  pl.BlockSpec((2, cw, KV_HEADS * CMP_HIDDEN), fix(3)),
                  pl.BlockSpec((2, cw, KV_HEADS * CMP_HIDDEN), fix(3)),
                  pl.BlockSpec((2, 1, KV_HEADS * CMP_HIDDEN), fix(3)),
                  pl.BlockSpec((2, CMP_HIDDEN, HEAD_DIM), fix(3)),
                  pl.BlockSpec((3, HEAD_DIM), fix(2))],
        out_specs=(pl.BlockSpec((1, KV_HEADS, n, 2 * HEAD_DIM), lambda bi: (bi, 0, 0, 0)),
                   pl.BlockSpec((1, KV_HEADS, HEAD_DIM, n), lambda bi: (bi, 0, 0, 0))),
        out_shape=(jax.ShapeDtypeStruct((b, KV_HEADS, n, 2 * HEAD_DIM), BF16),
                   jax.ShapeDtypeStruct((b, KV_HEADS, HEAD_DIM, n), BF16)),
        compiler_params=_cparams("parallel"),
        name="compress",
    )(kch, vch, pos, wfull[:, 0], wfull[:, 1], b1, cmp_w2.astype(BF16), k_norm_w)


def _nsa_kernel(q_ref, kc_ref, vct_ref, ks_ref, vst_ref, kw_ref, vwt_ref, gt_ref, cdiff_ref, wdiff_ref,
                ovl_ref, oh_ref, onw_ref, o_ref, buf_a, buf_b, m_scr, acc_scr, *, n_top):
    q0 = pl.program_id(2) * TQ
    ncols = HEADS_PER_KV * TQ
    q = q_ref[0].reshape(ncols, 2 * HEAD_DIM)
    ns = ovl_ref.shape[0]

    s = jnp.where(cdiff_ref[...] <= q0, _dot_nt(kc_ref[0, 0], q), -jnp.inf)
    m = jnp.max(s, axis=0, keepdims=True)
    m = jnp.where(m == -jnp.inf, 0.0, m)
    e = jnp.exp2(s - m)
    p = e / jnp.maximum(jnp.sum(e, axis=0, keepdims=True), 1e-30)
    o_c = _dot(vct_ref[0, 0], p.astype(BF16))

    nw = WINDOW + TQ
    start = pl.multiple_of(jnp.maximum(q0 - WINDOW, 0), TQ)
    dist = wdiff_ref[...] + (q0 - start)
    sw = jnp.where((dist >= 0) & (dist < WINDOW), _dot_nt(kw_ref[0, 0, pl.ds(start, nw), :], q), -jnp.inf)
    ew = jnp.exp2(sw - jnp.max(sw, axis=0, keepdims=True))
    o_w = _dot(vwt_ref[0, 0, :, pl.ds(start, nw)], ew.astype(BF16)) / jnp.sum(ew, axis=0, keepdims=True)

    psum = p[:, 0:TQ]
    for hh in range(1, HEADS_PER_KV):
        psum = psum + p[:, hh * TQ:(hh + 1) * TQ]
    imp = _split_dot(ovl_ref[...], psum)
    blk = lax.broadcasted_iota(I32, (ns, TQ), 0)
    tq = q0 + lax.broadcasted_iota(I32, (ns, TQ), 1)
    cur = tq >> 6
    forced = (blk == 0) | (blk == cur) | (blk == cur - 1)
    rank = jnp.where(forced, BIG, jnp.where(blk * SEL_BLOCK <= tq, imp, -BIG))

    blkf = blk.astype(F32)

    def topk_body(_, carry):
        r, bias = carry
        mx = jnp.max(r, axis=0, keepdims=True)
        first = jnp.min(jnp.where(r == mx, blkf, float(ns)), axis=0, keepdims=True)
        hit = blkf == first
        return jnp.where(hit, -jnp.inf, r), jnp.where(hit, 0.0, bias)

    _, bias = lax.fori_loop(0, n_top, topk_body, (rank, jnp.full((ns, TQ), -1e30, F32)))

    if ns < 128:
        bias = jnp.concatenate([bias, jnp.zeros((128 - ns, TQ), F32)], axis=0)
    bias_t = bias.T.astype(BF16)
    qq = jnp.concatenate([q, jnp.concatenate([bias_t] * HEADS_PER_KV, axis=0)], axis=1)
    ones_rows = jnp.ones((16, TK), BF16)

    def scores(j):
        k0 = pl.multiple_of(j * TK, TK)
        kk = jnp.concatenate([ks_ref[0, 0, pl.ds(k0, TK), :], oh_ref[pl.ds(k0, TK), :]], axis=1)
        return _dot_nt(kk, qq)

    def consume(buf, j, causal):
        sc = buf[...]
        if causal:
            sc = jnp.where(wdiff_ref[0:TK, :] + (q0 - j * TK) >= 0, sc, -1e30)
        k0 = pl.multiple_of(j * TK, TK)
        m_run = m_scr[...]
        m_new = jnp.maximum(m_run, jnp.max(sc, axis=0, keepdims=True))
        ex = jnp.exp2(sc - m_new).astype(BF16)
        va = jnp.concatenate([vst_ref[0, 0, :, pl.ds(k0, TK)], ones_rows], axis=0)
        acc_scr[...] = jnp.exp2(m_run - m_new) * acc_scr[...] + _dot(va, ex)
        m_scr[...] = m_new

    n_past = q0 // TK
    m_scr[...] = jnp.full((1, ncols), -1e30, F32)
    acc_scr[...] = jnp.zeros((HEAD_DIM + 16, ncols), F32)
    buf_a[...] = scores(0)

    def two_tiles(i, _):
        buf_b[...] = scores(2 * i + 1)
        consume(buf_a, 2 * i, False)
        buf_a[...] = scores(2 * i + 2)
        consume(buf_b, 2 * i + 1, False)
        return 0

    lax.fori_loop(0, n_past // 2, two_tiles, 0)

    @pl.when(n_past % 2 == 1)
    def _():
        buf_b[...] = scores(n_past)
        consume(buf_a, n_past - 1, False)
        consume(buf_b, n_past, True)

    @pl.when(n_past % 2 == 0)
    def _():
        consume(buf_a, n_past, True)

    o_s = acc_scr[0:HEAD_DIM, :] / acc_scr[HEAD_DIM:HEAD_DIM + 1, :]

    gt = gt_ref[0, 0]
    outs = []
    for hh in range(HEADS_PER_KV):
        cs = slice(hh * TQ, (hh + 1) * TQ)
        o = (gt[3 * hh:3 * hh + 1, :] * o_c[:, cs] + gt[3 * hh + 1:3 * hh + 2, :] * o_s[:, cs]
             + gt[3 * hh + 2:3 * hh + 3, :] * o_w[:, cs])
        o = o * lax.rsqrt(jnp.mean(o * o, axis=0, keepdims=True) + RMS_EPS) * onw_ref[0, hh]
        outs.append(o)
    o_ref[0] = jnp.concatenate(outs, axis=0).T


def _nsa(q, kc, vct, ks, vst, kw, vwt, gates_t, attn_out_norm_w):
    b, _, s, aw = q.shape
    nc = kc.shape[2]
    ns = s // SEL_BLOCK
    n_top = min(N_SELECT, ns)
    ncols = HEADS_PER_KV * TQ
    nw = WINDOW + TQ
    tl = np.arange(ncols)[None, :] & (TQ - 1)
    cdiff = jnp.asarray((np.arange(nc)[:, None] * CMP_STRIDE + (CMP_BLOCK - 1) - tl).astype(np.int32))
    wdiff = jnp.asarray((tl - np.arange(nw)[:, None]).astype(np.int32))
    ci = np.arange(nc)[None, :] * CMP_STRIDE
    bj = np.arange(ns)[:, None]
    ovl = ((ci < (bj + 1) * SEL_BLOCK) & (ci + CMP_BLOCK > bj * SEL_BLOCK) & (np.arange(nc)[None, :] < nc - 1))
    ovl = jnp.asarray(ovl.astype(np.float32)).astype(BF16)
    assert ns <= 128
    onehot = (np.arange(s)[:, None] // SEL_BLOCK == np.arange(128)[None, :])
    onehot = jnp.asarray(onehot.astype(np.float32)).astype(BF16)
    onw = jnp.broadcast_to(attn_out_norm_w.reshape(KV_HEADS, HEADS_PER_KV, HEAD_DIM, 1),
                           (KV_HEADS, HEADS_PER_KV, HEAD_DIM, TQ))
    gt = gates_t.reshape(b, KV_HEADS, HEADS_PER_KV * 3, s)
    per_bg = lambda bi, g, i: (bi, g, 0, 0)
    fixed = lambda bi, g, i: (0, 0)
    return pl.pallas_call(
        functools.partial(_nsa_kernel, n_top=n_top),
        grid=(b, KV_HEADS, s // TQ),
        in_specs=[pl.BlockSpec((1, HEADS_PER_KV, TQ, aw), lambda bi, g, i: (bi, g, i, 0)),
                  pl.BlockSpec((1, 1, nc, aw), per_bg),
                  pl.BlockSpec((1, 1, HEAD_DIM, nc), per_bg),
                  pl.BlockSpec((1, 1, s, aw), per_bg),
                  pl.BlockSpec((1, 1, HEAD_DIM, s), per_bg),
                  pl.BlockSpec((1, 1, s, aw), per_bg),
                  pl.BlockSpec((1, 1, HEAD_DIM, s), per_bg),
                  pl.BlockSpec((1, 1, HEADS_PER_KV * 3, TQ), lambda bi, g, i: (bi, g, 0, i)),
                  pl.BlockSpec((nc, ncols), fixed),
                  pl.BlockSpec((nw, ncols), fixed),
                  pl.BlockSpec((ns, nc), fixed),
                  pl.BlockSpec((s, 128), fixed),
                  pl.BlockSpec((1, HEADS_PER_KV, HEAD_DIM, TQ), lambda bi, g, i: (g, 0, 0, 0))],
        out_specs=pl.BlockSpec((1, TQ, HEADS_PER_KV * HEAD_DIM), lambda bi, g, i: (bi, i, g)),
        out_shape=jax.ShapeDtypeStruct((b, s, NSA_WIDTH), F32),
        scratch_shapes=[pltpu.VMEM((TK, ncols), F32), pltpu.VMEM((TK, ncols), F32),
                        pltpu.VMEM((1, ncols), F32), pltpu.VMEM((HEAD_DIM + 16, ncols), F32)],
        compiler_params=_cparams("parallel", "parallel", "arbitrary"),
        name="nsa",
    )(q, kc, vct, ks, vst, kw, vwt, gt, cdiff, wdiff, ovl, onehot, onw)


def _hgrn_kernel(q_ref, k_ref, lf_ref, v_ref, g_ref, onw_ref, o_ref, state_scr, *, n_chunks):
    c = HGRN_CHUNK

    @pl.when(pl.program_id(1) == 0)
    def _():
        state_scr[...] = jnp.zeros_like(state_scr)

    ri = lax.broadcasted_iota(I32, (c, c), 0)
    ci = lax.broadcasted_iota(I32, (c, c), 1)
    tril = (ri >= ci).astype(F32)
    rsub = ri // HGRN_SUB
    rin = ri & (HGRN_SUB - 1)

    def head_chunk(r0, hd, state_t):
        cols = slice(hd * HGRN_DIM, (hd + 1) * HGRN_DIM)
        q = q_ref[0, pl.ds(r0, c), cols]
        k = k_ref[0, pl.ds(r0, c), cols]
        lf = lf_ref[0, pl.ds(r0, c), cols]
        v = v_ref[0, pl.ds(r0, c), cols]
        cum = _dot(tril, lf, precision=HIGHEST)
        o = _dot_nt((q * jnp.exp(cum)).astype(BF16), state_t.astype(BF16))
        scores = jnp.zeros((c, c), F32)
        for i in range(1, c // HGRN_SUB):
            ref_row = cum[i * HGRN_SUB - 1:i * HGRN_SUB, :]
            qs = q * jnp.exp(jnp.minimum(cum - ref_row, 0.0))
            kd = k * jnp.exp(jnp.minimum(ref_row - cum, 0.0))
            blk = _dot_nt(qs.astype(BF16), kd.astype(BF16))
            scores = jnp.where((rsub == i) & (ci < i * HGRN_SUB), blk, scores)
        for d in range(HGRN_SUB):
            if d == 0:
                w = jnp.sum(q * k, axis=-1, keepdims=True)
            else:
                ksh = pltpu.roll(k, d, 0)
                csh = pltpu.roll(cum, d, 0)
                w = jnp.sum(q * ksh * jnp.exp(jnp.minimum(cum - csh, 0.0)), axis=-1, keepdims=True)
            scores = jnp.where((ri - ci == d) & (rin >= d), w, scores)
        o = o + _dot(scores.astype(BF16), v.astype(BF16))
        last = cum[c - 1:c, :]
        kd = (k * jnp.exp(last - cum)).astype(BF16)
        state_t = state_t * jnp.exp(last) + _dot(v.T.astype(BF16), kd)
        o = o * g_ref[0, pl.ds(r0, c), cols]
        o = o * lax.rsqrt(jnp.mean(o * o, axis=-1, keepdims=True) + RMS_EPS) * onw_ref[:, cols]
        o_ref[0, pl.ds(r0, c), cols] = o
        return state_t

    def chunk(ck, states):
        r0 = pl.multiple_of(ck * c, c)
        return tuple(head_chunk(r0, hd, states[hd]) for hd in range(HGRN_HEADS))

    states = lax.fori_loop(0, n_chunks, chunk, tuple(state_scr[hd] for hd in range(HGRN_HEADS)))
    for hd in range(HGRN_HEADS):
        state_scr[hd] = states[hd]


def _hgrn(hq, hk, hlf, hv, hg, rec_out_norm_w, rows):
    b, s, _ = hq.shape
    blk = pl.BlockSpec((1, rows, HGRN_WIDTH), lambda bi, i: (bi, i, 0))
    return pl.pallas_call(
        functools.partial(_hgrn_kernel, n_chunks=rows // HGRN_CHUNK),
        grid=(b, s // rows),
        in_specs=[blk, blk, blk, blk, blk,
                  pl.BlockSpec((1, HGRN_WIDTH), lambda bi, i: (0, 0))],
        out_specs=blk,
        out_shape=jax.ShapeDtypeStruct((b, s, HGRN_WIDTH), F32),
        scratch_shapes=[pltpu.VMEM((HGRN_HEADS, HGRN_DIM, HGRN_DIM), F32)],
        compiler_params=_cparams("parallel", "arbitrary"),
        name="hgrn",
    )(hq, hk, hlf, hv, hg, rec_out_norm_w.reshape(1, HGRN_WIDTH))


def _outproj_kernel(x_ref, a_ref, r_ref, wa_ref, wr_ref, gt_ref, sc_ref, sh_ref, n2_ref, x1_ref, h2_ref, h2p_ref):
    mixed = _dot(a_ref[0].astype(BF16), wa_ref[...]) + _dot(r_ref[0].astype(BF16), wr_ref[...])
    x1 = x_ref[0] + gt_ref[0] * mixed
    x1_ref[0] = x1
    ms = jnp.mean(x1 * x1, axis=-1, keepdims=True)
    h2 = x1 * lax.rsqrt(ms + RMS_EPS) * n2_ref[...] * (1.0 + sc_ref[0]) + sh_ref[0]
    h2_ref[0] = h2
    h2p_ref[0] = _pack_bf16_pair(h2[:, :D_MODEL // 2], h2[:, D_MODEL // 2:])


def _outproj(x, attn, rec, w_out, gt1, sc2, sh2, norm2_w, tm):
    b, s, d = x.shape
    row = lambda bi, i: (bi, i, 0)
    per_b = lambda bi, i: (bi, 0, 0)
    fixed2 = lambda bi, i: (0, 0)
    w = w_out.astype(BF16)
    return pl.pallas_call(
        _outproj_kernel,
        grid=(b, s // tm),
        in_specs=[pl.BlockSpec((1, tm, d), row),
                  pl.BlockSpec((1, tm, NSA_WIDTH), row),
                  pl.BlockSpec((1, tm, HGRN_WIDTH), row),
                  pl.BlockSpec((NSA_WIDTH, d), fixed2),
                  pl.BlockSpec((HGRN_WIDTH, d), fixed2),
                  pl.BlockSpec((1, 1, d), per_b),
                  pl.BlockSpec((1, 1, d), per_b),
                  pl.BlockSpec((1, 1, d), per_b),
                  pl.BlockSpec((1, d), fixed2)],
        out_specs=(pl.BlockSpec((1, tm, d), row), pl.BlockSpec((1, tm, d), row), pl.BlockSpec((1, tm, d // 2), row)),
        out_shape=(jax.ShapeDtypeStruct((b, s, d), F32), jax.ShapeDtypeStruct((b, s, d), F32),
                   jax.ShapeDtypeStruct((b, s, d // 2), jnp.uint32)),
        compiler_params=_cparams("parallel", "parallel"),
        name="outproj",
    )(x, attn, rec, w[:NSA_WIDTH], w[NSA_WIDTH:], gt1, sc2, sh2, norm2_w)


def _mixer(x, c, ada_w, ada_b, norm1_w, norm2_w, w_in, q_norm_w, k_norm_w, cmp_pos, cmp_w1, cmp_b1, cmp_w2,
           attn_out_norm_w, hgrn_lb_param, rec_out_norm_w, w_out):
    b, s, d = x.shape
    mod = _mod(c, ada_w, ada_b)
    sh1, sc1, gt1, sh2, sc2, gt2 = [m.reshape(b, 1, d) for m in jnp.split(mod, 6, axis=-1)]
    o = NSA_WIDTH + 6 * KV_WIDTH
    w_cat = jnp.concatenate([w_in[:, :o], w_in[:, o:o + NSA_HEADS * 3],
                             jnp.zeros((d, GATE_PAD - NSA_HEADS * 3), w_in.dtype),
                             w_in[:, o + NSA_HEADS * 3:]], axis=1).astype(BF16)
    tm = min(256, s)
    (q, kc_raw, vc_raw, ks, vst, kw, vwt, gates_t, hq, hk, hlf, hv, hg) = _inproj(
        x, sc1, sh1, norm1_w.reshape(1, d), w_cat, q_norm_w.reshape(1, HEAD_DIM), k_norm_w, hgrn_lb_param, tm)
    kc, vct = _compress(kc_raw, vc_raw, cmp_pos, cmp_w1, cmp_b1, cmp_w2, k_norm_w)
    attn = _nsa(q, kc, vct, ks, vst, kw, vwt, gates_t, attn_out_norm_w)
    rec = _hgrn(hq, hk, hlf, hv, hg, rec_out_norm_w, min(512, s))
    x1, h2, h2p = _outproj(x, attn, rec, w_out, gt1, sc2, sh2, norm2_w.reshape(1, d), tm)
    return x1, h2, h2p, gt2


def _router_kernel(h_ref, rwt_ref, bias_ref, tri_ref, ones_ref, idx_ref, w_ref, rank_ref, cnt_ref, carry_scr, *, tr):
    @pl.when(pl.program_id(0) == 0)
    def _():
        carry_scr[...] = jnp.zeros_like(carry_scr)

    h = h_ref[...]
    h_hi = h.astype(BF16)
    h_lo = (h - h_hi.astype(F32)).astype(BF16)
    logits = _dot_nt(rwt_ref[0], h_hi) + _dot_nt(rwt_ref[1], h_hi) + _dot_nt(rwt_ref[0], h_lo)
    scores = _sigmoid(logits)
    biased = scores + bias_ref[...]
    neg = -jnp.inf

    gs = []
    for g in range(N_GROUPS):
        sub = biased[g * GROUP_SIZE:(g + 1) * GROUP_SIZE, :]
        m1 = jnp.max(sub, axis=0, keepdims=True)
        dup = jnp.sum((sub == m1).astype(F32), axis=0, keepdims=True)
        m2 = jnp.max(jnp.where(sub < m1, sub, neg), axis=0, keepdims=True)
        gs.append(m1 + jnp.where(dup >= 2.0, m1, m2))
    parts = []
    for g in range(N_GROUPS):
        beaten = jnp.zeros_like(gs[g])
        for g2 in range(N_GROUPS):
            if g2 != g:
                beats = (gs[g2] >= gs[g]) if g2 < g else (gs[g2] > gs[g])
                beaten = beaten + beats.astype(F32)
        sub = biased[g * GROUP_SIZE:(g + 1) * GROUP_SIZE, :]
        parts.append(jnp.where(beaten < float(TOPK_GROUPS), sub, neg))
    cand = jnp.concatenate(parts, axis=0)

    rowf = lax.broadcasted_iota(I32, (N_EXPERTS, tr), 0).astype(F32)
    idx_rows, w_rows, hits = [], [], []
    multi = jnp.zeros((N_EXPERTS, tr), F32)
    for _ in range(TOP_K):
        mx = jnp.max(cand, axis=0, keepdims=True)
        first = jnp.min(jnp.where(cand == mx, rowf, float(N_EXPERTS)), axis=0, keepdims=True)
        hit = rowf == first
        idx_rows.append(first)
        w_rows.append(jnp.sum(jnp.where(hit, scores, 0.0), axis=0, keepdims=True))
        cand = jnp.where(hit, neg, cand)
        multi = jnp.where(hit, 1.0, multi)
    w = jnp.concatenate(w_rows, axis=0)
    w_ref[...] = w / jnp.sum(w, axis=0, keepdims=True) * ROUTED_SCALE
    idx = jnp.concatenate(idx_rows, axis=0)
    idx_ref[...] = idx.astype(I32)

    carry = carry_scr[...]
    mb = multi.astype(BF16)
    before = _dot(mb, tri_ref[...]) + jnp.concatenate([carry] * (tr // 128), axis=1)
    rank_rows = [jnp.sum(jnp.where(rowf == idx_rows[k], before, 0.0), axis=0, keepdims=True) for k in range(TOP_K)]
    rank_ref[...] = jnp.concatenate(rank_rows, axis=0).astype(I32)
    carry = carry + _dot(mb, ones_ref[...])
    carry_scr[...] = carry
    cnt_ref[...] = carry


def _router(h2, router_w, router_bias, tr):
    t, d = h2.shape
    tri = jnp.asarray(np.triu(np.ones((tr, tr), np.float32), 1)).astype(BF16)
    ones = jnp.ones((tr, 128), BF16)
    tok = pl.BlockSpec((TOP_K, tr), lambda i: (0, i))
    fixed = lambda i: (0, 0)
    rwt = router_w.T
    rwt_hi = rwt.astype(BF16)
    rwt_split = jnp.stack([rwt_hi, (rwt - rwt_hi.astype(F32)).astype(BF16)])
    return pl.pallas_call(
        functools.partial(_router_kernel, tr=tr),
        grid=(t // tr,),
        in_specs=[pl.BlockSpec((tr, d), lambda i: (i, 0)),
                  pl.BlockSpec((2, N_EXPERTS, d), lambda i: (0, 0, 0)),
                  pl.BlockSpec((N_EXPERTS, 1), fixed),
                  pl.BlockSpec((tr, tr), fixed),
                  pl.BlockSpec((tr, 128), fixed)],
        out_specs=(tok, tok, tok, pl.BlockSpec((N_EXPERTS, 128), fixed)),
        out_shape=(jax.ShapeDtypeStruct((TOP_K, t), I32), jax.ShapeDtypeStruct((TOP_K, t), F32),
                   jax.ShapeDtypeStruct((TOP_K, t), I32), jax.ShapeDtypeStruct((N_EXPERTS, 128), F32)),
        scratch_shapes=[pltpu.VMEM((N_EXPERTS, 128), F32)],
        compiler_params=_cparams("arbitrary"),
        name="router",
    )(h2, rwt_split, router_bias.reshape(N_EXPERTS, 1), tri, ones)


def _pack_bf16_pair(a, b):
    ua = lax.bitcast_convert_type(a.astype(BF16).astype(F32), jnp.uint32)
    ub = lax.bitcast_convert_type(b.astype(BF16).astype(F32), jnp.uint32)
    return ua | (ub >> 16)


def _unpack_bf16_pair(w):
    a = lax.bitcast_convert_type(w & jnp.uint32(0xFFFF0000), F32)
    b = lax.bitcast_convert_type(w << 16, F32)
    return a, b


def _slot_kernel(ps_ref, idx_ref, rank_ref, slot_ref):
    idx = idx_ref[...]

    def body(e, acc):
        return jnp.where(idx == e, ps_ref[e], acc)

    slot_ref[...] = lax.fori_loop(0, N_EXPERTS, body, jnp.zeros_like(idx)) + rank_ref[...]


def _slots(pad_start, idx, rank, tt):
    t = idx.shape[1]
    tok = pl.BlockSpec((TOP_K, tt), lambda i, ps: (0, i))
    return pl.pallas_call(
        _slot_kernel,
        grid_spec=pltpu.PrefetchScalarGridSpec(num_scalar_prefetch=1, grid=(t // tt,),
                                               in_specs=[tok, tok], out_specs=tok),
        out_shape=jax.ShapeDtypeStruct((TOP_K, t), I32),
        compiler_params=_cparams("parallel"),
        name="slots",
    )(pad_start, idx, rank)


SC_CORES = 2
SC_SUBCORES = 16
SC_CHUNK = 64


def _sc_mesh():
    return plsc.VectorSubcoreMesh(core_axis_name="c", subcore_axis_name="s")


def _sc_dispatch(h2p, slot_chunks, n_rows):
    t, dw = h2p.shape
    per = slot_chunks.shape[0] // (SC_CORES * SC_SUBCORES)

    def body(h_hbm, slot_hbm, xs_hbm, idx_v, rows_v, sem):
        wid = lax.axis_index("s") * SC_CORES + lax.axis_index("c")

        @pl.loop(0, per)
        def _(c):
            ch = wid * per + c
            pltpu.sync_copy(slot_hbm.at[ch], idx_v)
            pltpu.sync_copy(h_hbm.at[pl.ds(ch * SC_CHUNK, SC_CHUNK)], rows_v)
            copies = [pltpu.async_copy(rows_v, xs_hbm.at[idx_v.at[k]], sem) for k in range(TOP_K)]
            for cp in copies:
                cp.wait()

    return pl.kernel(
        body, out_type=jax.ShapeDtypeStruct((n_rows, dw), h2p.dtype), mesh=_sc_mesh(),
        scratch_types=[pltpu.VMEM((TOP_K, SC_CHUNK), I32), pltpu.VMEM((SC_CHUNK, dw), h2p.dtype),
                       pltpu.SemaphoreType.DMA],
    )(h2p, slot_chunks)


def _sc_gather(ys, slot_chunks, t):
    dw = ys.shape[1]
    per = slot_chunks.shape[0] // (SC_CORES * SC_SUBCORES)

    def body(ys_hbm, slot_hbm, yg_hbm, idx_v, rows_v, sem):
        wid = lax.axis_index("s") * SC_CORES + lax.axis_index("c")

        @pl.loop(0, per)
        def _(c):
            ch = wid * per + c
            pltpu.sync_copy(slot_hbm.at[ch], idx_v)
            for k in range(TOP_K):
                pltpu.async_copy(ys_hbm.at[idx_v.at[k]], rows_v, sem).wait()
                pltpu.sync_copy(rows_v, yg_hbm.at[k, pl.ds(ch * SC_CHUNK, SC_CHUNK)])

    return pl.kernel(
        body, out_type=jax.ShapeDtypeStruct((TOP_K, t, dw), ys.dtype), mesh=_sc_mesh(),
        scratch_types=[pltpu.VMEM((TOP_K, SC_CHUNK), I32), pltpu.VMEM((SC_CHUNK, dw), ys.dtype),
                       pltpu.SemaphoreType.DMA],
    )(ys, slot_chunks)


def _experts_kernel(be_ref, nu_ref, bv_ref, xs_ref, wg_ref, wu_ref, wd_ref, ys_ref):
    i = pl.program_id(0)
    half = D_MODEL // 2

    @pl.when(i < nu_ref[0])
    def _():
        live = lax.broadcasted_iota(I32, xs_ref.shape, 0) < bv_ref[i]
        xa, xb = _unpack_bf16_pair(jnp.where(live, xs_ref[...], jnp.uint32(0)))
        xa, xb = xa.astype(BF16), xb.astype(BF16)
        g = _dot(xa, wg_ref[0, :half].astype(BF16)) + _dot(xb, wg_ref[0, half:].astype(BF16))
        u = _dot(xa, wu_ref[0, :half].astype(BF16)) + _dot(xb, wu_ref[0, half:].astype(BF16))
        act = (g * _sigmoid(g) * u).astype(BF16)
        y = _dot(act, wd_ref[0].astype(BF16))
        ys_ref[...] = _pack_bf16_pair(y[:, :half], y[:, half:])

    @pl.when(i >= nu_ref[0])
    def _():
        ys_ref[...] = jnp.zeros_like(ys_ref)


def _experts(xs, blk_e, n_used, blk_valid, w_gate, w_up, w_down):
    n_rows, dw = xs.shape
    d = w_gate.shape[1]
    nblk = n_rows // EXPERT_BLOCK
    row_map = lambda i, be, nu, bv: (jnp.minimum(i, nu[0] - 1), 0)
    w_map = lambda i, be, nu, bv: (be[i], 0, 0)
    return pl.pallas_call(
        _experts_kernel,
        grid_spec=pltpu.PrefetchScalarGridSpec(
            num_scalar_prefetch=3,
            grid=(nblk,),
            in_specs=[pl.BlockSpec((EXPERT_BLOCK, dw), row_map),
                      pl.BlockSpec((1, d, EXPERT_FF), w_map),
                      pl.BlockSpec((1, d, EXPERT_FF), w_map),
                      pl.BlockSpec((1, EXPERT_FF, d), w_map)],
            out_specs=pl.BlockSpec((EXPERT_BLOCK, dw), lambda i, be, nu, bv: (i, 0))),
        out_shape=jax.ShapeDtypeStruct((n_rows, dw), xs.dtype),
        compiler_params=_cparams("arbitrary"),
        name="experts",
    )(blk_e, n_used, blk_valid, xs, w_gate, w_up, w_down)


def _combine_kernel(x1_ref, h_ref, w_ref, gt_ref, sg_ref, su_ref, sd_ref, yg_ref, o_ref):
    tc = x1_ref.shape[0]
    half = D_MODEL // 2
    hb = h_ref[...].astype(BF16)
    g = _dot(hb, sg_ref[...])
    u = _dot(hb, su_ref[...])
    ffn = _dot((g * _sigmoid(g) * u).astype(BF16), sd_ref[...])

    w = w_ref[...]
    ra = jnp.zeros((tc, half), F32)
    rb = jnp.zeros((tc, half), F32)
    for k in range(TOP_K):
        ya, yb = _unpack_bf16_pair(yg_ref[k])
        ra = ra + w[:, k:k + 1] * ya
        rb = rb + w[:, k:k + 1] * yb
    ffn = ffn + jnp.concatenate([ra, rb], axis=1)
    o_ref[...] = x1_ref[...] + gt_ref[0] * ffn


def _combine(x1, h2, w_tok, gt2, yg, sg, su, sd, seq, tc):
    t, d = x1.shape
    row = lambda i: (i, 0)
    fixed = lambda i: (0, 0)
    return pl.pallas_call(
        _combine_kernel,
        grid=(t // tc,),
        in_specs=[pl.BlockSpec((tc, d), row),
                  pl.BlockSpec((tc, d), row),
                  pl.BlockSpec((tc, TOP_K), row),
                  pl.BlockSpec((1, 1, d), lambda i: ((i * tc) // seq, 0, 0)),
                  pl.BlockSpec((d, SHARED_FF), fixed),
                  pl.BlockSpec((d, SHARED_FF), fixed),
                  pl.BlockSpec((SHARED_FF, d), fixed),
                  pl.BlockSpec((TOP_K, tc, d // 2), lambda i: (0, i, 0))],
        out_specs=pl.BlockSpec((tc, d), row),
        out_shape=jax.ShapeDtypeStruct((t, d), F32),
        compiler_params=_cparams("parallel"),
        name="combine",
    )(x1, h2, w_tok, gt2, sg.astype(BF16), su.astype(BF16), sd.astype(BF16), yg)


def _moe_parts(x1, h2, h2p, gt2, router_w, router_bias, w_gate, w_up, w_down, sg, su, sd):
    b, s, d = x1.shape
    t = b * s
    h2 = h2.reshape(t, d)
    idx, w, rank, cnt = _router(h2, router_w, router_bias, min(256, t))
    counts = cnt[:, 0].astype(I32)
    padded = (counts + EXPERT_BLOCK - 1) // EXPERT_BLOCK * EXPERT_BLOCK
    pad_end = jnp.cumsum(padded)
    pad_start = pad_end - padded
    n_rows = t * TOP_K + N_EXPERTS * EXPERT_BLOCK
    nblk = n_rows // EXPERT_BLOCK
    n_used = (pad_end[-1:] // EXPERT_BLOCK).astype(I32)
    blk_e = jnp.searchsorted(pad_end, jnp.arange(nblk, dtype=I32) * EXPERT_BLOCK, side='right')
    blk_e = jnp.minimum(blk_e, N_EXPERTS - 1).astype(I32)
    blk_e = jnp.where(jnp.arange(nblk) < n_used[0], blk_e, blk_e[jnp.maximum(n_used[0] - 1, 0)])
    blk_start = jnp.arange(nblk, dtype=I32) * EXPERT_BLOCK
    blk_valid = jnp.clip(counts[blk_e] - (blk_start - pad_start[blk_e]), 0, EXPERT_BLOCK).astype(I32)
    slot = _slots(pad_start.astype(I32), idx, rank, min(2048, t))
    slot_chunks = slot.reshape(TOP_K, t // SC_CHUNK, SC_CHUNK).transpose(1, 0, 2)
    xs = _sc_dispatch(h2p.reshape(t, d // 2), slot_chunks, n_rows)
    ys = _experts(xs, blk_e, n_used, blk_valid, w_gate, w_up, w_down)
    yg = _sc_gather(ys, slot_chunks, t)
    out = _combine(x1.reshape(t, d), h2, w.T, gt2, yg, sg, su, sd, s, min(256, t))
    return out.reshape(b, s, d), dict(idx=idx, w=w, rank=rank, cnt=cnt)


def kernel(x, c, ada_w, ada_b, norm1_w, norm2_w, w_in, q_norm_w, k_norm_w, cmp_pos, cmp_w1, cmp_b1, cmp_w2, attn_out_norm_w, hgrn_lb_param, rec_out_norm_w, w_out, router_w, router_bias, exp_w_gate, exp_w_up, exp_w_down, shared_w_gate, shared_w_up, shared_w_down):
    assert ada_w.shape[0] == 1, "one layer"
    assert x.shape[0] <= 8 and x.shape[1] % TK == 0 and x.shape[1] >= WINDOW + TQ
    l = 0
    x1, h2, h2p, gt2 = _mixer(x, c, ada_w[l], ada_b[l], norm1_w[l], norm2_w[l], w_in[l], q_norm_w[l], k_norm_w[l],
                         cmp_pos[l], cmp_w1[l], cmp_b1[l], cmp_w2[l], attn_out_norm_w[l], hgrn_lb_param,
                         rec_out_norm_w[l], w_out[l])
    out, _ = _moe_parts(x1, h2, h2p, gt2, router_w[l], router_bias[l], exp_w_gate[l], exp_w_up[l], exp_w_down[l],
                        shared_w_gate[l], shared_w_up[l], shared_w_down[l])
    return out
```

```python
import functools

import numpy as np
import jax
import jax.numpy as jnp
from jax import lax
from jax.experimental import pallas as pl
from jax.experimental.pallas import tpu as pltpu
from jax.experimental.pallas import tpu_sc as plsc

F32 = jnp.float32
BF16 = jnp.bfloat16
I32 = jnp.int32

D_MODEL = 1024
NSA_HEADS = 8
HEAD_DIM = 64
NSA_WIDTH = NSA_HEADS * HEAD_DIM
KV_HEADS = 2
HEADS_PER_KV = NSA_HEADS // KV_HEADS
KV_WIDTH = KV_HEADS * HEAD_DIM
CMP_BLOCK = 32
CMP_STRIDE = 16
CMP_HIDDEN = 256
SEL_BLOCK = 64
N_SELECT = 16
WINDOW = 512
HGRN_HEADS = 4
HGRN_DIM = 128
HGRN_WIDTH = HGRN_HEADS * HGRN_DIM
HGRN_CHUNK = 64
HGRN_SUB = 16
N_EXPERTS = 256
TOP_K = 8
N_GROUPS = 8
GROUP_SIZE = N_EXPERTS // N_GROUPS
TOPK_GROUPS = 4
EXPERT_FF = 256
SHARED_FF = 256
ROUTED_SCALE = 2.5
RMS_EPS = 1e-6
BIG = 1e9
LOG2E = 1.4426950408889634
GATE_PAD = 128
PROJ_COLS = NSA_WIDTH + 6 * KV_WIDTH + GATE_PAD + 4 * HGRN_WIDTH

VMEM_LIMIT = 56 * 1024 * 1024

TQ = 128
TK = 512
EXPERT_BLOCK = 512
HIGHEST = lax.Precision.HIGHEST


def _cparams(*sem):
    return pltpu.CompilerParams(dimension_semantics=sem, vmem_limit_bytes=VMEM_LIMIT)


def _sigmoid(x):
    return 1.0 / (1.0 + jnp.exp(-x))


def _dot_nt(a, b):
    return lax.dot_general(a, b, (((1,), (1,)), ((), ())), preferred_element_type=F32)


def _dot(a, b, **kw):
    return jnp.dot(a, b, preferred_element_type=F32, **kw)


def _split_dot(a_bf16_exact, x):
    hi = x.astype(BF16)
    lo = (x - hi.astype(F32)).astype(BF16)
    return _dot(a_bf16_exact, hi) + _dot(a_bf16_exact, lo)


def _mod_kernel(c_ref, w_ref, b_ref, o_ref):
    c = c_ref[...]
    cond = c * _sigmoid(c)
    o_ref[...] = _dot(cond, w_ref[...], precision=HIGHEST) + b_ref[...]


def _mod(c, ada_w, ada_b):
    b, d = c.shape
    rows = 8
    c_pad = jnp.zeros((rows, d), F32).at[:b].set(c)
    n = ada_w.shape[1]
    out = pl.pallas_call(
        _mod_kernel,
        grid=(n // d,),
        in_specs=[pl.BlockSpec((rows, d), lambda j: (0, 0)),
                  pl.BlockSpec((d, d), lambda j: (0, j)),
                  pl.BlockSpec((1, d), lambda j: (0, j))],
        out_specs=pl.BlockSpec((rows, d), lambda j: (0, j)),
        out_shape=jax.ShapeDtypeStruct((rows, n), F32),
        compiler_params=_cparams("parallel"),
        name="mod",
    )(c_pad, ada_w, ada_b.reshape(1, n))
    return out[:b]


def _head_rms(t, w):
    return t * lax.rsqrt(jnp.mean(t * t, axis=-1, keepdims=True) + RMS_EPS) * w


def _pos_digits(pos):
    lane = lax.broadcasted_iota(I32, pos.shape, 1)
    d0 = (lane == 0) | (lane == 3) | (lane == 6)
    d1 = (lane == 1) | (lane == 4) | (lane == 7)
    d2 = (lane == 2) | (lane == 5) | (lane == 8)
    dig = jnp.where(d0, pos >> 12, jnp.where(d1, (pos >> 6) & 63, jnp.where(d2, pos & 63, 0)))
    return dig.astype(F32)


def _inproj_kernel(x_ref, sc_ref, sh_ref, n1_ref, w_ref, qnw_ref, knw_ref, lbp_ref, qaug_ref,
                   q_ref, kcr_ref, vcr_ref, ks_ref, vst_ref, kw_ref, vwt_ref, gt_ref,
                   hq_ref, hk_ref, hlf_ref, hv_ref, hg_ref):
    x = x_ref[0]
    ms = jnp.mean(x * x, axis=-1, keepdims=True)
    h = x * lax.rsqrt(ms + RMS_EPS) * n1_ref[...] * (1.0 + sc_ref[0]) + sh_ref[0]
    p = _dot(h.astype(BF16), w_ref[...])
    tm = x.shape[0]

    qnw = qnw_ref[...]
    for hd in range(NSA_HEADS):
        t = p[:, hd * HEAD_DIM:(hd + 1) * HEAD_DIM]
        qn = _head_rms(t, qnw) * (HEAD_DIM ** -0.5 * LOG2E)
        qa = jnp.broadcast_to(qaug_ref[hd:hd + 1, :], (tm, HEAD_DIM))
        q_ref[0, hd] = jnp.concatenate([qn, qa], axis=1).astype(BF16)
    kaug = _pos_digits(pl.program_id(1) * tm + lax.broadcasted_iota(I32, (tm, HEAD_DIM), 0))

    o = NSA_WIDTH
    kcr_ref[0] = p[:, o:o + KV_WIDTH]
    vcr_ref[0] = p[:, o + KV_WIDTH:o + 2 * KV_WIDTH]
    ks = p[:, o + 2 * KV_WIDTH:o + 3 * KV_WIDTH]
    vs = p[:, o + 3 * KV_WIDTH:o + 4 * KV_WIDTH]
    kw = p[:, o + 4 * KV_WIDTH:o + 5 * KV_WIDTH]
    vw = p[:, o + 5 * KV_WIDTH:o + 6 * KV_WIDTH]
    for g in range(KV_HEADS):
        sl = slice(g * HEAD_DIM, (g + 1) * HEAD_DIM)
        ks_ref[0, g] = jnp.concatenate([_head_rms(ks[:, sl], knw_ref[1:2, :]), kaug], axis=1).astype(BF16)
        kw_ref[0, g] = jnp.concatenate([_head_rms(kw[:, sl], knw_ref[2:3, :]), kaug], axis=1).astype(BF16)
    vst = vs.T.astype(BF16)
    vwt = vw.T.astype(BF16)
    for g in range(KV_HEADS):
        vst_ref[0, g] = vst[g * HEAD_DIM:(g + 1) * HEAD_DIM, :]
        vwt_ref[0, g] = vwt[g * HEAD_DIM:(g + 1) * HEAD_DIM, :]

    o = NSA_WIDTH + 6 * KV_WIDTH
    gates = _sigmoid(p[:, o:o + GATE_PAD])
    gt_ref[0] = gates.T[:NSA_HEADS * 3, :]

    o = o + GATE_PAD
    hq = p[:, o:o + HGRN_WIDTH]
    hf = p[:, o + HGRN_WIDTH:o + 2 * HGRN_WIDTH]
    hi = p[:, o + 2 * HGRN_WIDTH:o + 3 * HGRN_WIDTH]
    hg = p[:, o + 3 * HGRN_WIDTH:o + 4 * HGRN_WIDTH]
    lbp = lbp_ref[...]
    e = jnp.exp(lbp - jnp.max(lbp, axis=0, keepdims=True))
    lb = e[0:1, :] / jnp.sum(e, axis=0, keepdims=True)
    f = lb + (1.0 - lb) * _sigmoid(hf)
    hq_ref[0] = hq * _sigmoid(hq) * (HGRN_DIM ** -0.5)
    hk_ref[0] = 1.0 - f
    hlf_ref[0] = jnp.log(f)
    hv_ref[0] = hi
    hg_ref[0] = _sigmoid(hg)


def _inproj(x, sc1, sh1, norm1_w, w_cat, q_norm_w, k_norm_w, lb_param, tm):
    b, s, d = x.shape
    row = lambda bi, i: (bi, i, 0)
    per_b = lambda bi, i: (bi, 0, 0)
    fixed2 = lambda bi, i: (0, 0)
    aw = 2 * HEAD_DIM
    rest = np.array([2.0 ** (-8.0 * (i + 1) / NSA_HEADS) for i in range(NSA_HEADS)], np.float64) * LOG2E
    qaug = np.zeros((NSA_HEADS, HEAD_DIM), np.float32)
    for i in range(3):
        term = rest.astype(np.float32).astype(BF16).astype(np.float64)
        rest = rest - term
        for dgt, wgt in enumerate((4096.0, 64.0, 1.0)):
            qaug[:, 3 * i + dgt] = term * wgt
    assert np.all(qaug == qaug.astype(BF16).astype(np.float32))
    out_shape = (
        jax.ShapeDtypeStruct((b, NSA_HEADS, s, aw), BF16),
        jax.ShapeDtypeStruct((b, s, KV_WIDTH), F32),
        jax.ShapeDtypeStruct((b, s, KV_WIDTH), F32),
        jax.ShapeDtypeStruct((b, KV_HEADS, s, aw), BF16),
        jax.ShapeDtypeStruct((b, KV_HEADS, HEAD_DIM, s), BF16),
        jax.ShapeDtypeStruct((b, KV_HEADS, s, aw), BF16),
        jax.ShapeDtypeStruct((b, KV_HEADS, HEAD_DIM, s), BF16),
        jax.ShapeDtypeStruct((b, NSA_HEADS * 3, s), F32),
    ) + tuple(jax.ShapeDtypeStruct((b, s, HGRN_WIDTH), F32) for _ in range(5))
    hm = lambda n, w: pl.BlockSpec((1, n, tm, w), lambda bi, i: (bi, 0, i, 0))
    hmt = lambda n, w: pl.BlockSpec((1, n, w, tm), lambda bi, i: (bi, 0, 0, i))
    out_specs = (
        hm(NSA_HEADS, aw),
        pl.BlockSpec((1, tm, KV_WIDTH), row),
        pl.BlockSpec((1, tm, KV_WIDTH), row),
        hm(KV_HEADS, aw), hmt(KV_HEADS, HEAD_DIM),
        hm(KV_HEADS, aw), hmt(KV_HEADS, HEAD_DIM),
        pl.BlockSpec((1, NSA_HEADS * 3, tm), lambda bi, i: (bi, 0, i)),
    ) + tuple(pl.BlockSpec((1, tm, HGRN_WIDTH), row) for _ in range(5))
    return pl.pallas_call(
        _inproj_kernel,
        grid=(b, s // tm),
        in_specs=[pl.BlockSpec((1, tm, d), row),
                  pl.BlockSpec((1, 1, d), per_b),
                  pl.BlockSpec((1, 1, d), per_b),
                  pl.BlockSpec((1, d), fixed2),
                  pl.BlockSpec((d, PROJ_COLS), fixed2),
                  pl.BlockSpec((1, HEAD_DIM), fixed2),
                  pl.BlockSpec((3, HEAD_DIM), fixed2),
                  pl.BlockSpec(lb_param.shape, fixed2),
                  pl.BlockSpec((NSA_HEADS, HEAD_DIM), fixed2)],
        out_specs=out_specs,
        out_shape=out_shape,
        compiler_params=_cparams("parallel", "parallel"),
        name="inproj",
    )(x, sc1, sh1, norm1_w, w_cat, q_norm_w, k_norm_w, lb_param, jnp.asarray(qaug))


def _gelu_tanh(x):
    return 0.5 * x * (1.0 + jnp.tanh(0.7978845608028654 * (x + 0.044715 * x * x * x)))


def _compress_kernel(kch_ref, vch_ref, pos_ref, wa_ref, wb_ref, b1_ref, w2_ref, knw_ref,
                     kc_ref, vct_ref):
    n = kch_ref.shape[1]
    outs = []
    for br, ch_ref in enumerate((kch_ref, vch_ref)):
        ch = ch_ref[0]
        a = _dot((ch + pos_ref[br, 0:1, :]).astype(BF16), wa_ref[br])
        bm = _dot((ch + pos_ref[br, 1:2, :]).astype(BF16), wb_ref[br])
        pre = a + pltpu.roll(bm, n - 1, 0) + b1_ref[br]
        hid = _gelu_tanh(pre).astype(BF16)
        outs.append([_dot(hid[:, g * CMP_HIDDEN:(g + 1) * CMP_HIDDEN], w2_ref[br]) for g in range(KV_HEADS)])
    end_digits = _pos_digits(lax.broadcasted_iota(I32, (n, HEAD_DIM), 0) * CMP_STRIDE + (CMP_BLOCK - 1))
    for g in range(KV_HEADS):
        kc_ref[0, g] = jnp.concatenate([_head_rms(outs[0][g], knw_ref[0:1, :]), end_digits], axis=1).astype(BF16)
    vct = jnp.concatenate(outs[1], axis=1).T.astype(BF16)
    for g in range(KV_HEADS):
        vct_ref[0, g] = vct[g * HEAD_DIM:(g + 1) * HEAD_DIM, :]


def _compress(kc_raw, vc_raw, cmp_pos, cmp_w1, cmp_b1, cmp_w2, k_norm_w):
    b, s, _ = kc_raw.shape
    n = s // CMP_STRIDE
    half = CMP_STRIDE
    cw = CMP_STRIDE * KV_WIDTH
    kch = kc_raw.reshape(b, n, cw)
    vch = vc_raw.reshape(b, n, cw)
    pos = cmp_pos.reshape(2, 2, half, 1, HEAD_DIM)
    pos = jnp.broadcast_to(pos, (2, 2, half, KV_HEADS, HEAD_DIM)).reshape(2, 2, cw)
    w1 = cmp_w1.reshape(2, 2, half, HEAD_DIM, CMP_HIDDEN)
    eye = jnp.eye(KV_HEADS, dtype=F32)
    wfull = jnp.einsum('rhjdn,gk->rhjgdkn', w1, eye).reshape(2, 2, cw, KV_HEADS * CMP_HIDDEN).astype(BF16)
    b1 = jnp.tile(cmp_b1.reshape(2, 1, CMP_HIDDEN), (1, 1, KV_HEADS))
    fix = lambda r: (lambda bi: (0,) * r)
    return pl.pallas_call(
        _compress_kernel,
        grid=(b,),
        in_specs=[pl.BlockSpec((1, n, cw), lambda bi: (bi, 0, 0)),
                  pl.BlockSpec((1, n, cw), lambda bi: (bi, 0, 0)),
                  pl.BlockSpec((2, 2, cw), fix(3)),
                  pl.BlockSpec((2, cw, KV_HEADS * CMP_HIDDEN), fix(3)),
                  pl.BlockSpec((2, cw, KV_HEADS * CMP_HIDDEN), fix(3)),
                  pl.BlockSpec((2, 1, KV_HEADS * CMP_HIDDEN), fix(3)),
                  pl.BlockSpec((2, CMP_HIDDEN, HEAD_DIM), fix(3)),
                  pl.BlockSpec((3, HEAD_DIM), fix(2))],
        out_specs=(pl.BlockSpec((1, KV_HEADS, n, 2 * HEAD_DIM), lambda bi: (bi, 0, 0, 0)),
                   pl.BlockSpec((1, KV_HEADS, HEAD_DIM, n), lambda bi: (bi, 0, 0, 0))),
        out_shape=(jax.ShapeDtypeStruct((b, KV_HEADS, n, 2 * HEAD_DIM), BF16),
                   jax.ShapeDtypeStruct((b, KV_HEADS, HEAD_DIM, n), BF16)),
        compiler_params=_cparams("parallel"),
        name="compress",
    )(kch, vch, pos, wfull[:, 0], wfull[:, 1], b1, cmp_w2.astype(BF16), k_norm_w)


def _nsa_kernel(q_ref, kc_ref, vct_ref, ks_ref, vst_ref, kw_ref, vwt_ref, gt_ref, cdiff_ref, wdiff_ref,
                ovl_ref, oh_ref, onw_ref, o_ref, buf_a, buf_b, m_scr, acc_scr, *, n_top):
    q0 = pl.program_id(2) * TQ
    ncols = HEADS_PER_KV * TQ
    q = q_ref[0].reshape(ncols, 2 * HEAD_DIM)
    ns = ovl_ref.shape[0]

    s = jnp.where(cdiff_ref[...] <= q0, _dot_nt(kc_ref[0, 0], q), -jnp.inf)
    m = jnp.max(s, axis=0, keepdims=True)
    m = jnp.where(m == -jnp.inf, 0.0, m)
    e = jnp.exp2(s - m)
    p = e / jnp.maximum(jnp.sum(e, axis=0, keepdims=True), 1e-30)
    o_c = _dot(vct_ref[0, 0], p.astype(BF16))

    nw = WINDOW + TQ
    start = pl.multiple_of(jnp.maximum(q0 - WINDOW, 0), TQ)
    dist = wdiff_ref[...] + (q0 - start)
    sw = jnp.where((dist >= 0) & (dist < WINDOW), _dot_nt(kw_ref[0, 0, pl.ds(start, nw), :], q), -jnp.inf)
    ew = jnp.exp2(sw - jnp.max(sw, axis=0, keepdims=True))
    o_w = _dot(vwt_ref[0, 0, :, pl.ds(start, nw)], ew.astype(BF16)) / jnp.sum(ew, axis=0, keepdims=True)

    psum = p[:, 0:TQ]
    for hh in range(1, HEADS_PER_KV):
        psum = psum + p[:, hh * TQ:(hh + 1) * TQ]
    imp = _split_dot(ovl_ref[...], psum)
    blk = lax.broadcasted_iota(I32, (ns, TQ), 0)
    tq = q0 + lax.broadcasted_iota(I32, (ns, TQ), 1)
    cur = tq >> 6
    forced = (blk == 0) | (blk == cur) | (blk == cur - 1)
    rank = jnp.where(forced, BIG, jnp.where(blk * SEL_BLOCK <= tq, imp, -BIG))

    blkf = blk.astype(F32)

    bias = jnp.full((ns, TQ), -1e30, F32)
    for _ in range(n_top):
        mx = jnp.max(rank, axis=0, keepdims=True)
        first = jnp.min(jnp.where(rank == mx, blkf, float(ns)), axis=0, keepdims=True)
        hit = blkf == first
        rank = jnp.where(hit, -jnp.inf, rank)
        bias = jnp.where(hit, 0.0, bias)

    if ns < 128:
        bias = jnp.concatenate([bias, jnp.zeros((128 - ns, TQ), F32)], axis=0)
    bias_t = bias.T.astype(BF16)
    qq = jnp.concatenate([q, jnp.concatenate([bias_t] * HEADS_PER_KV, axis=0)], axis=1)
    ones_rows = jnp.ones((16, TK), BF16)

    def scores(j):
        k0 = pl.multiple_of(j * TK, TK)
        kk = jnp.concatenate([ks_ref[0, 0, pl.ds(k0, TK), :], oh_ref[pl.ds(k0, TK), :]], axis=1)
        return _dot_nt(kk, qq)

    def consume(buf, j, causal):
        sc = buf[...]
        if causal:
            sc = jnp.where(wdiff_ref[0:TK, :] + (q0 - j * TK) >= 0, sc, -1e30)
        k0 = pl.multiple_of(j * TK, TK)
        m_run = m_scr[...]
        m_new = jnp.maximum(m_run, jnp.max(sc, axis=0, keepdims=True))
        ex = jnp.exp2(sc - m_new).astype(BF16)
        va = jnp.concatenate([vst_ref[0, 0, :, pl.ds(k0, TK)], ones_rows], axis=0)
        acc_scr[...] = jnp.exp2(m_run - m_new) * acc_scr[...] + _dot(va, ex)
        m_scr[...] = m_new

    n_past = q0 // TK
    m_scr[...] = jnp.full((1, ncols), -1e30, F32)
    acc_scr[...] = jnp.zeros((HEAD_DIM + 16, ncols), F32)
    buf_a[...] = scores(0)

    def two_tiles(i, _):
        buf_b[...] = scores(2 * i + 1)
        consume(buf_a, 2 * i, False)
        buf_a[...] = scores(2 * i + 2)
        consume(buf_b, 2 * i + 1, False)
        return 0

    lax.fori_loop(0, n_past // 2, two_tiles, 0)

    @pl.when(n_past % 2 == 1)
    def _():
        buf_b[...] = scores(n_past)
        consume(buf_a, n_past - 1, False)
        consume(buf_b, n_past, True)

    @pl.when(n_past % 2 == 0)
    def _():
        consume(buf_a, n_past, True)

    o_s = acc_scr[0:HEAD_DIM, :] / acc_scr[HEAD_DIM:HEAD_DIM + 1, :]

    gt = gt_ref[0, 0]
    outs = []
    for hh in range(HEADS_PER_KV):
        cs = slice(hh * TQ, (hh + 1) * TQ)
        o = (gt[3 * hh:3 * hh + 1, :] * o_c[:, cs] + gt[3 * hh + 1:3 * hh + 2, :] * o_s[:, cs]
             + gt[3 * hh + 2:3 * hh + 3, :] * o_w[:, cs])
        o = o * lax.rsqrt(jnp.mean(o * o, axis=0, keepdims=True) + RMS_EPS) * onw_ref[0, hh]
        outs.append(o)
    o_ref[0] = jnp.concatenate(outs, axis=0).T


def _nsa(q, kc, vct, ks, vst, kw, vwt, gates_t, attn_out_norm_w):
    b, _, s, aw = q.shape
    nc = kc.shape[2]
    ns = s // SEL_BLOCK
    n_top = min(N_SELECT, ns)
    ncols = HEADS_PER_KV * TQ
    nw = WINDOW + TQ
    tl = np.arange(ncols)[None, :] & (TQ - 1)
    cdiff = jnp.asarray((np.arange(nc)[:, None] * CMP_STRIDE + (CMP_BLOCK - 1) - tl).astype(np.int32))
    wdiff = jnp.asarray((tl - np.arange(nw)[:, None]).astype(np.int32))
    ci = np.arange(nc)[None, :] * CMP_STRIDE
    bj = np.arange(ns)[:, None]
    ovl = ((ci < (bj + 1) * SEL_BLOCK) & (ci + CMP_BLOCK > bj * SEL_BLOCK) & (np.arange(nc)[None, :] < nc - 1))
    ovl = jnp.asarray(ovl.astype(np.float32)).astype(BF16)
    assert ns <= 128
    onehot = (np.arange(s)[:, None] // SEL_BLOCK == np.arange(128)[None, :])
    onehot = jnp.asarray(onehot.astype(np.float32)).astype(BF16)
    onw = jnp.broadcast_to(attn_out_norm_w.reshape(KV_HEADS, HEADS_PER_KV, HEAD_DIM, 1),
                           (KV_HEADS, HEADS_PER_KV, HEAD_DIM, TQ))
    gt = gates_t.reshape(b, KV_HEADS, HEADS_PER_KV * 3, s)
    per_bg = lambda bi, g, i: (bi, g, 0, 0)
    fixed = lambda bi, g, i: (0, 0)
    return pl.pallas_call(
        functools.partial(_nsa_kernel, n_top=n_top),
        grid=(b, KV_HEADS, s // TQ),
        in_specs=[pl.BlockSpec((1, HEADS_PER_KV, TQ, aw), lambda bi, g, i: (bi, g, i, 0)),
                  pl.BlockSpec((1, 1, nc, aw), per_bg),
                  pl.BlockSpec((1, 1, HEAD_DIM, nc), per_bg),
                  pl.BlockSpec((1, 1, s, aw), per_bg),
                  pl.BlockSpec((1, 1, HEAD_DIM, s), per_bg),
                  pl.BlockSpec((1, 1, s, aw), per_bg),
                  pl.BlockSpec((1, 1, HEAD_DIM, s), per_bg),
                  pl.BlockSpec((1, 1, HEADS_PER_KV * 3, TQ), lambda bi, g, i: (bi, g, 0, i)),
                  pl.BlockSpec((nc, ncols), fixed),
                  pl.BlockSpec((nw, ncols), fixed),
                  pl.BlockSpec((ns, nc), fixed),
                  pl.BlockSpec((s, 128), fixed),
                  pl.BlockSpec((1, HEADS_PER_KV, HEAD_DIM, TQ), lambda bi, g, i: (g, 0, 0, 0))],
        out_specs=pl.BlockSpec((1, TQ, HEADS_PER_KV * HEAD_DIM), lambda bi, g, i: (bi, i, g)),
        out_shape=jax.ShapeDtypeStruct((b, s, NSA_WIDTH), F32),
        scratch_shapes=[pltpu.VMEM((TK, ncols), F32), pltpu.VMEM((TK, ncols), F32),
                        pltpu.VMEM((1, ncols), F32), pltpu.VMEM((HEAD_DIM + 16, ncols), F32)],
        compiler_params=_cparams("parallel", "parallel", "arbitrary"),
        name="nsa",
    )(q, kc, vct, ks, vst, kw, vwt, gt, cdiff, wdiff, ovl, onehot, onw)


def _hgrn_kernel(q_ref, k_ref, lf_ref, v_ref, g_ref, onw_ref, o_ref, state_scr, *, n_chunks):
    c = HGRN_CHUNK

    @pl.when(pl.program_id(1) == 0)
    def _():
        state_scr[...] = jnp.zeros_like(state_scr)

    ri = lax.broadcasted_iota(I32, (c, c), 0)
    ci = lax.broadcasted_iota(I32, (c, c), 1)
    tril = (ri >= ci).astype(F32)
    rsub = ri // HGRN_SUB
    rin = ri & (HGRN_SUB - 1)

    def head_chunk(r0, hd, state_t):
        cols = slice(hd * HGRN_DIM, (hd + 1) * HGRN_DIM)
        q = q_ref[0, pl.ds(r0, c), cols]
        k = k_ref[0, pl.ds(r0, c), cols]
        lf = lf_ref[0, pl.ds(r0, c), cols]
        v = v_ref[0, pl.ds(r0, c), cols]
        cum = _dot(tril, lf, precision=HIGHEST)
        o = _dot_nt((q * jnp.exp(cum)).astype(BF16), state_t.astype(BF16))
        scores = jnp.zeros((c, c), F32)
        for i in range(1, c // HGRN_SUB):
            ref_row = cum[i * HGRN_SUB - 1:i * HGRN_SUB, :]
            qs = q * jnp.exp(jnp.minimum(cum - ref_row, 0.0))
            kd = k * jnp.exp(jnp.minimum(ref_row - cum, 0.0))
            blk = _dot_nt(qs.astype(BF16), kd.astype(BF16))
            scores = jnp.where((rsub == i) & (ci < i * HGRN_SUB), blk, scores)
        for d in range(HGRN_SUB):
            if d == 0:
                w = jnp.sum(q * k, axis=-1, keepdims=True)
            else:
                ksh = pltpu.roll(k, d, 0)
                csh = pltpu.roll(cum, d, 0)
                w = jnp.sum(q * ksh * jnp.exp(jnp.minimum(cum - csh, 0.0)), axis=-1, keepdims=True)
            scores = jnp.where((ri - ci == d) & (rin >= d), w, scores)
        o = o + _dot(scores.astype(BF16), v.astype(BF16))
        last = cum[c - 1:c, :]
        kd = (k * jnp.exp(last - cum)).astype(BF16)
        state_t = state_t * jnp.exp(last) + _dot(v.T.astype(BF16), kd)
        o = o * g_ref[0, pl.ds(r0, c), cols]
        o = o * lax.rsqrt(jnp.mean(o * o, axis=-1, keepdims=True) + RMS_EPS) * onw_ref[:, cols]
        o_ref[0, pl.ds(r0, c), cols] = o
        return state_t

    def chunk(ck, states):
        r0 = pl.multiple_of(ck * c, c)
        return tuple(head_chunk(r0, hd, states[hd]) for hd in range(HGRN_HEADS))

    states = lax.fori_loop(0, n_chunks, chunk, tuple(state_scr[hd] for hd in range(HGRN_HEADS)))
    for hd in range(HGRN_HEADS):
        state_scr[hd] = states[hd]


def _hgrn(hq, hk, hlf, hv, hg, rec_out_norm_w, rows):
    b, s, _ = hq.shape
    blk = pl.BlockSpec((1, rows, HGRN_WIDTH), lambda bi, i: (bi, i, 0))
    return pl.pallas_call(
        functools.partial(_hgrn_kernel, n_chunks=rows // HGRN_CHUNK),
        grid=(b, s // rows),
        in_specs=[blk, blk, blk, blk, blk,
                  pl.BlockSpec((1, HGRN_WIDTH), lambda bi, i: (0, 0))],
        out_specs=blk,
        out_shape=jax.ShapeDtypeStruct((b, s, HGRN_WIDTH), F32),
        scratch_shapes=[pltpu.VMEM((HGRN_HEADS, HGRN_DIM, HGRN_DIM), F32)],
        compiler_params=_cparams("parallel", "arbitrary"),
        name="hgrn",
    )(hq, hk, hlf, hv, hg, rec_out_norm_w.reshape(1, HGRN_WIDTH))


def _outproj_kernel(x_ref, a_ref, r_ref, wa_ref, wr_ref, gt_ref, sc_ref, sh_ref, n2_ref, x1_ref, h2_ref, h2p_ref):
    mixed = _dot(a_ref[0].astype(BF16), wa_ref[...]) + _dot(r_ref[0].astype(BF16), wr_ref[...])
    x1 = x_ref[0] + gt_ref[0] * mixed
    x1_ref[0] = x1
    ms = jnp.mean(x1 * x1, axis=-1, keepdims=True)
    h2 = x1 * lax.rsqrt(ms + RMS_EPS) * n2_ref[...] * (1.0 + sc_ref[0]) + sh_ref[0]
    h2_ref[0] = h2
    h2p_ref[0] = _pack_bf16_pair(h2[:, :D_MODEL // 2], h2[:, D_MODEL // 2:])


def _outproj(x, attn, rec, w_out, gt1, sc2, sh2, norm2_w, tm):
    b, s, d = x.shape
    row = lambda bi, i: (bi, i, 0)
    per_b = lambda bi, i: (bi, 0, 0)
    fixed2 = lambda bi, i: (0, 0)
    w = w_out.astype(BF16)
    return pl.pallas_call(
        _outproj_kernel,
        grid=(b, s // tm),
        in_specs=[pl.BlockSpec((1, tm, d), row),
                  pl.BlockSpec((1, tm, NSA_WIDTH), row),
                  pl.BlockSpec((1, tm, HGRN_WIDTH), row),
                  pl.BlockSpec((NSA_WIDTH, d), fixed2),
                  pl.BlockSpec((HGRN_WIDTH, d), fixed2),
                  pl.BlockSpec((1, 1, d), per_b),
                  pl.BlockSpec((1, 1, d), per_b),
                  pl.BlockSpec((1, 1, d), per_b),
                  pl.BlockSpec((1, d), fixed2)],
        out_specs=(pl.BlockSpec((1, tm, d), row), pl.BlockSpec((1, tm, d), row), pl.BlockSpec((1, tm, d // 2), row)),
        out_shape=(jax.ShapeDtypeStruct((b, s, d), F32), jax.ShapeDtypeStruct((b, s, d), F32),
                   jax.ShapeDtypeStruct((b, s, d // 2), jnp.uint32)),
        compiler_params=_cparams("parallel", "parallel"),
        name="outproj",
    )(x, attn, rec, w[:NSA_WIDTH], w[NSA_WIDTH:], gt1, sc2, sh2, norm2_w)


def _mixer(x, c, ada_w, ada_b, norm1_w, norm2_w, w_in, q_norm_w, k_norm_w, cmp_pos, cmp_w1, cmp_b1, cmp_w2,
           attn_out_norm_w, hgrn_lb_param, rec_out_norm_w, w_out):
    b, s, d = x.shape
    mod = _mod(c, ada_w, ada_b)
    sh1, sc1, gt1, sh2, sc2, gt2 = [m.reshape(b, 1, d) for m in jnp.split(mod, 6, axis=-1)]
    o = NSA_WIDTH + 6 * KV_WIDTH
    w_cat = jnp.concatenate([w_in[:, :o], w_in[:, o:o + NSA_HEADS * 3],
                             jnp.zeros((d, GATE_PAD - NSA_HEADS * 3), w_in.dtype),
                             w_in[:, o + NSA_HEADS * 3:]], axis=1).astype(BF16)
    tm = min(256, s)
    (q, kc_raw, vc_raw, ks, vst, kw, vwt, gates_t, hq, hk, hlf, hv, hg) = _inproj(
        x, sc1, sh1, norm1_w.reshape(1, d), w_cat, q_norm_w.reshape(1, HEAD_DIM), k_norm_w, hgrn_lb_param, tm)
    kc, vct = _compress(kc_raw, vc_raw, cmp_pos, cmp_w1, cmp_b1, cmp_w2, k_norm_w)
    attn = _nsa(q, kc, vct, ks, vst, kw, vwt, gates_t, attn_out_norm_w)
    rec = _hgrn(hq, hk, hlf, hv, hg, rec_out_norm_w, min(512, s))
    x1, h2, h2p = _outproj(x, attn, rec, w_out, gt1, sc2, sh2, norm2_w.reshape(1, d), tm)
    return x1, h2, h2p, gt2


def _router_kernel(h_ref, rwt_ref, bias_ref, tri_ref, ones_ref, idx_ref, w_ref, rank_ref, cnt_ref, carry_scr, *, tr):
    @pl.when(pl.program_id(0) == 0)
    def _():
        carry_scr[...] = jnp.zeros_like(carry_scr)

    h = h_ref[...]
    h_hi = h.astype(BF16)
    h_lo = (h - h_hi.astype(F32)).astype(BF16)
    logits = _dot_nt(rwt_ref[0], h_hi) + _dot_nt(rwt_ref[1], h_hi) + _dot_nt(rwt_ref[0], h_lo)
    scores = _sigmoid(logits)
    biased = scores + bias_ref[...]
    neg = -jnp.inf

    gs = []
    for g in range(N_GROUPS):
        sub = biased[g * GROUP_SIZE:(g + 1) * GROUP_SIZE, :]
        m1 = jnp.max(sub, axis=0, keepdims=True)
        dup = jnp.sum((sub == m1).astype(F32), axis=0, keepdims=True)
        m2 = jnp.max(jnp.where(sub < m1, sub, neg), axis=0, keepdims=True)
        gs.append(m1 + jnp.where(dup >= 2.0, m1, m2))
    parts = []
    for g in range(N_GROUPS):
        beaten = jnp.zeros_like(gs[g])
        for g2 in range(N_GROUPS):
            if g2 != g:
                beats = (gs[g2] >= gs[g]) if g2 < g else (gs[g2] > gs[g])
                beaten = beaten + beats.astype(F32)
        sub = biased[g * GROUP_SIZE:(g + 1) * GROUP_SIZE, :]
        parts.append(jnp.where(beaten < float(TOPK_GROUPS), sub, neg))
    cand = jnp.concatenate(parts, axis=0)

    rowf = lax.broadcasted_iota(I32, (N_EXPERTS, tr), 0).astype(F32)
    idx_rows, w_rows, hits = [], [], []
    multi = jnp.zeros((N_EXPERTS, tr), F32)
    for _ in range(TOP_K):
        mx = jnp.max(cand, axis=0, keepdims=True)
        first = jnp.min(jnp.where(cand == mx, rowf, float(N_EXPERTS)), axis=0, keepdims=True)
        hit = rowf == first
        idx_rows.append(first)
        w_rows.append(jnp.sum(jnp.where(hit, scores, 0.0), axis=0, keepdims=True))
        cand = jnp.where(hit, neg, cand)
        multi = jnp.where(hit, 1.0, multi)
    w = jnp.concatenate(w_rows, axis=0)
    w_ref[...] = w / jnp.sum(w, axis=0, keepdims=True) * ROUTED_SCALE
    idx = jnp.concatenate(idx_rows, axis=0)
    idx_ref[...] = idx.astype(I32)

    carry = carry_scr[...]
    mb = multi.astype(BF16)
    before = _dot(mb, tri_ref[...]) + jnp.concatenate([carry] * (tr // 128), axis=1)
    rank_rows = [jnp.sum(jnp.where(rowf == idx_rows[k], before, 0.0), axis=0, keepdims=True) for k in range(TOP_K)]
    rank_ref[...] = jnp.concatenate(rank_rows, axis=0).astype(I32)
    carry = carry + _dot(mb, ones_ref[...])
    carry_scr[...] = carry
    cnt_ref[...] = carry


def _router(h2, router_w, router_bias, tr):
    t, d = h2.shape
    tri = jnp.asarray(np.triu(np.ones((tr, tr), np.float32), 1)).astype(BF16)
    ones = jnp.ones((tr, 128), BF16)
    tok = pl.BlockSpec((TOP_K, tr), lambda i: (0, i))
    fixed = lambda i: (0, 0)
    rwt = router_w.T
    rwt_hi = rwt.astype(BF16)
    rwt_split = jnp.stack([rwt_hi, (rwt - rwt_hi.astype(F32)).astype(BF16)])
    return pl.pallas_call(
        functools.partial(_router_kernel, tr=tr),
        grid=(t // tr,),
        in_specs=[pl.BlockSpec((tr, d), lambda i: (i, 0)),
                  pl.BlockSpec((2, N_EXPERTS, d), lambda i: (0, 0, 0)),
                  pl.BlockSpec((N_EXPERTS, 1), fixed),
                  pl.BlockSpec((tr, tr), fixed),
                  pl.BlockSpec((tr, 128), fixed)],
        out_specs=(tok, tok, tok, pl.BlockSpec((N_EXPERTS, 128), fixed)),
        out_shape=(jax.ShapeDtypeStruct((TOP_K, t), I32), jax.ShapeDtypeStruct((TOP_K, t), F32),
                   jax.ShapeDtypeStruct((TOP_K, t), I32), jax.ShapeDtypeStruct((N_EXPERTS, 128), F32)),
        scratch_shapes=[pltpu.VMEM((N_EXPERTS, 128), F32)],
        compiler_params=_cparams("arbitrary"),
        name="router",
    )(h2, rwt_split, router_bias.reshape(N_EXPERTS, 1), tri, ones)


def _pack_bf16_pair(a, b):
    ua = lax.bitcast_convert_type(a.astype(BF16).astype(F32), jnp.uint32)
    ub = lax.bitcast_convert_type(b.astype(BF16).astype(F32), jnp.uint32)
    return ua | (ub >> 16)


def _unpack_bf16_pair(w):
    a = lax.bitcast_convert_type(w & jnp.uint32(0xFFFF0000), F32)
    b = lax.bitcast_convert_type(w << 16, F32)
    return a, b


def _slot_kernel(ps_ref, idx_ref, rank_ref, slot_ref):
    idx = idx_ref[...]

    def body(e, acc):
        return jnp.where(idx == e, ps_ref[e], acc)

    slot_ref[...] = lax.fori_loop(0, N_EXPERTS, body, jnp.zeros_like(idx)) + rank_ref[...]


def _slots(pad_start, idx, rank, tt):
    t = idx.shape[1]
    tok = pl.BlockSpec((TOP_K, tt), lambda i, ps: (0, i))
    return pl.pallas_call(
        _slot_kernel,
        grid_spec=pltpu.PrefetchScalarGridSpec(num_scalar_prefetch=1, grid=(t // tt,),
                                               in_specs=[tok, tok], out_specs=tok),
        out_shape=jax.ShapeDtypeStruct((TOP_K, t), I32),
        compiler_params=_cparams("parallel"),
        name="slots",
    )(pad_start, idx, rank)


SC_CORES = 2
SC_SUBCORES = 16
SC_CHUNK = 64


def _sc_mesh():
    return plsc.VectorSubcoreMesh(core_axis_name="c", subcore_axis_name="s")


def _sc_dispatch(h2p, slot_chunks, n_rows):
    t, dw = h2p.shape
    per = slot_chunks.shape[0] // (SC_CORES * SC_SUBCORES)

    def body(h_hbm, slot_hbm, xs_hbm, idx_v, rows_v, sem):
        wid = lax.axis_index("s") * SC_CORES + lax.axis_index("c")

        @pl.loop(0, per)
        def _(c):
            ch = wid * per + c
            pltpu.sync_copy(slot_hbm.at[ch], idx_v)
            pltpu.sync_copy(h_hbm.at[pl.ds(ch * SC_CHUNK, SC_CHUNK)], rows_v)
            copies = [pltpu.async_copy(rows_v, xs_hbm.at[idx_v.at[k]], sem) for k in range(TOP_K)]
            for cp in copies:
                cp.wait()

    return pl.kernel(
        body, out_type=jax.ShapeDtypeStruct((n_rows, dw), h2p.dtype), mesh=_sc_mesh(),
        scratch_types=[pltpu.VMEM((TOP_K, SC_CHUNK), I32), pltpu.VMEM((SC_CHUNK, dw), h2p.dtype),
                       pltpu.SemaphoreType.DMA],
    )(h2p, slot_chunks)


def _sc_gather(ys, slot_chunks, t):
    dw = ys.shape[1]
    per = slot_chunks.shape[0] // (SC_CORES * SC_SUBCORES)

    def body(ys_hbm, slot_hbm, yg_hbm, idx_v, rows_v, sem):
        wid = lax.axis_index("s") * SC_CORES + lax.axis_index("c")

        @pl.loop(0, per)
        def _(c):
            ch = wid * per + c
            pltpu.sync_copy(slot_hbm.at[ch], idx_v)
            for k in range(TOP_K):
                pltpu.async_copy(ys_hbm.at[idx_v.at[k]], rows_v, sem).wait()
                pltpu.sync_copy(rows_v, yg_hbm.at[k, pl.ds(ch * SC_CHUNK, SC_CHUNK)])

    return pl.kernel(
        body, out_type=jax.ShapeDtypeStruct((TOP_K, t, dw), ys.dtype), mesh=_sc_mesh(),
        scratch_types=[pltpu.VMEM((TOP_K, SC_CHUNK), I32), pltpu.VMEM((SC_CHUNK, dw), ys.dtype),
                       pltpu.SemaphoreType.DMA],
    )(ys, slot_chunks)


def _experts_kernel(be_ref, nu_ref, bv_ref, xs_ref, wg_ref, wu_ref, wd_ref, ys_ref):
    i = pl.program_id(0)
    half = D_MODEL // 2

    @pl.when(i < nu_ref[0])
    def _():
        live = lax.broadcasted_iota(I32, xs_ref.shape, 0) < bv_ref[i]
        xa, xb = _unpack_bf16_pair(jnp.where(live, xs_ref[...], jnp.uint32(0)))
        xa, xb = xa.astype(BF16), xb.astype(BF16)
        g = _dot(xa, wg_ref[0, :half].astype(BF16)) + _dot(xb, wg_ref[0, half:].astype(BF16))
        u = _dot(xa, wu_ref[0, :half].astype(BF16)) + _dot(xb, wu_ref[0, half:].astype(BF16))
        act = (g * _sigmoid(g) * u).astype(BF16)
        y = _dot(act, wd_ref[0].astype(BF16))
        ys_ref[...] = _pack_bf16_pair(y[:, :half], y[:, half:])

    @pl.when(i >= nu_ref[0])
    def _():
        ys_ref[...] = jnp.zeros_like(ys_ref)


def _experts(xs, blk_e, n_used, blk_valid, w_gate, w_up, w_down):
    n_rows, dw = xs.shape
    d = w_gate.shape[1]
    nblk = n_rows // EXPERT_BLOCK
    row_map = lambda i, be, nu, bv: (jnp.minimum(i, nu[0] - 1), 0)
    w_map = lambda i, be, nu, bv: (be[i], 0, 0)
    return pl.pallas_call(
        _experts_kernel,
        grid_spec=pltpu.PrefetchScalarGridSpec(
            num_scalar_prefetch=3,
            grid=(nblk,),
            in_specs=[pl.BlockSpec((EXPERT_BLOCK, dw), row_map),
                      pl.BlockSpec((1, d, EXPERT_FF), w_map),
                      pl.BlockSpec((1, d, EXPERT_FF), w_map),
                      pl.BlockSpec((1, EXPERT_FF, d), w_map)],
            out_specs=pl.BlockSpec((EXPERT_BLOCK, dw), lambda i, be, nu, bv: (i, 0))),
        out_shape=jax.ShapeDtypeStruct((n_rows, dw), xs.dtype),
        compiler_params=_cparams("arbitrary"),
        name="experts",
    )(blk_e, n_used, blk_valid, xs, w_gate, w_up, w_down)


def _combine_kernel(x1_ref, h_ref, w_ref, gt_ref, sg_ref, su_ref, sd_ref, yg_ref, o_ref):
    tc = x1_ref.shape[0]
    half = D_MODEL // 2
    hb = h_ref[...].astype(BF16)
    g = _dot(hb, sg_ref[...])
    u = _dot(hb, su_ref[...])
    ffn = _dot((g * _sigmoid(g) * u).astype(BF16), sd_ref[...])

    w = w_ref[...]
    ra = jnp.zeros((tc, half), F32)
    rb = jnp.zeros((tc, half), F32)
    for k in range(TOP_K):
        ya, yb = _unpack_bf16_pair(yg_ref[k])
        ra = ra + w[:, k:k + 1] * ya
        rb = rb + w[:, k:k + 1] * yb
    ffn = ffn + jnp.concatenate([ra, rb], axis=1)
    o_ref[...] = x1_ref[...] + gt_ref[0] * ffn


def _combine(x1, h2, w_tok, gt2, yg, sg, su, sd, seq, tc):
    t, d = x1.shape
    row = lambda i: (i, 0)
    fixed = lambda i: (0, 0)
    return pl.pallas_call(
        _combine_kernel,
        grid=(t // tc,),
        in_specs=[pl.BlockSpec((tc, d), row),
                  pl.BlockSpec((tc, d), row),
                  pl.BlockSpec((tc, TOP_K), row),
                  pl.BlockSpec((1, 1, d), lambda i: ((i * tc) // seq, 0, 0)),
                  pl.BlockSpec((d, SHARED_FF), fixed),
                  pl.BlockSpec((d, SHARED_FF), fixed),
                  pl.BlockSpec((SHARED_FF, d), fixed),
                  pl.BlockSpec((TOP_K, tc, d // 2), lambda i: (0, i, 0))],
        out_specs=pl.BlockSpec((tc, d), row),
        out_shape=jax.ShapeDtypeStruct((t, d), F32),
        compiler_params=_cparams("parallel"),
        name="combine",
    )(x1, h2, w_tok, gt2, sg.astype(BF16), su.astype(BF16), sd.astype(BF16), yg)


def _moe_parts(x1, h2, h2p, gt2, router_w, router_bias, w_gate, w_up, w_down, sg, su, sd):
    b, s, d = x1.shape
    t = b * s
    h2 = h2.reshape(t, d)
    idx, w, rank, cnt = _router(h2, router_w, router_bias, min(256, t))
    counts = cnt[:, 0].astype(I32)
    padded = (counts + EXPERT_BLOCK - 1) // EXPERT_BLOCK * EXPERT_BLOCK
    pad_end = jnp.cumsum(padded)
    pad_start = pad_end - padded
    n_rows = t * TOP_K + N_EXPERTS * EXPERT_BLOCK
    nblk = n_rows // EXPERT_BLOCK
    n_used = (pad_end[-1:] // EXPERT_BLOCK).astype(I32)
    blk_e = jnp.searchsorted(pad_end, jnp.arange(nblk, dtype=I32) * EXPERT_BLOCK, side='right')
    blk_e = jnp.minimum(blk_e, N_EXPERTS - 1).astype(I32)
    blk_e = jnp.where(jnp.arange(nblk) < n_used[0], blk_e, blk_e[jnp.maximum(n_used[0] - 1, 0)])
    blk_start = jnp.arange(nblk, dtype=I32) * EXPERT_BLOCK
    blk_valid = jnp.clip(counts[blk_e] - (blk_start - pad_start[blk_e]), 0, EXPERT_BLOCK).astype(I32)
    slot = _slots(pad_start.astype(I32), idx, rank, min(2048, t))
    slot_chunks = slot.reshape(TOP_K, t // SC_CHUNK, SC_CHUNK).transpose(1, 0, 2)
    xs = _sc_dispatch(h2p.reshape(t, d // 2), slot_chunks, n_rows)
    ys = _experts(xs, blk_e, n_used, blk_valid, w_gate, w_up, w_down)
    yg = _sc_gather(ys, slot_chunks, t)
    out = _combine(x1.reshape(t, d), h2, w.T, gt2, yg, sg, su, sd, s, min(256, t))
    return out.reshape(b, s, d), dict(idx=idx, w=w, rank=rank, cnt=cnt)


def kernel(x, c, ada_w, ada_b, norm1_w, norm2_w, w_in, q_norm_w, k_norm_w, cmp_pos, cmp_w1, cmp_b1, cmp_w2, attn_out_norm_w, hgrn_lb_param, rec_out_norm_w, w_out, router_w, router_bias, exp_w_gate, exp_w_up, exp_w_down, shared_w_gate, shared_w_up, shared_w_down):
    assert ada_w.shape[0] == 1, "one layer"
    assert x.shape[0] <= 8 and x.shape[1] % TK == 0 and x.shape[1] >= WINDOW + TQ
    l = 0
    x1, h2, h2p, gt2 = _mixer(x, c, ada_w[l], ada_b[l], norm1_w[l], norm2_w[l], w_in[l], q_norm_w[l], k_norm_w[l],
                         cmp_pos[l], cmp_w1[l], cmp_b1[l], cmp_w2[l], attn_out_norm_w[l], hgrn_lb_param,
                         rec_out_norm_w[l], w_out[l])
    out, _ = _moe_parts(x1, h2, h2p, gt2, router_w[l], router_bias[l], exp_w_gate[l], exp_w_up[l], exp_w_down[l],
                        shared_w_gate[l], shared_w_up[l], shared_w_down[l])
    return out
```

```python
import functools

import numpy as np
import jax
import jax.numpy as jnp
from jax import lax
from jax.experimental import pallas as pl
from jax.experimental.pallas import tpu as pltpu
from jax.experimental.pallas import tpu_sc as plsc

F32 = jnp.float32
BF16 = jnp.bfloat16
I32 = jnp.int32

D_MODEL = 1024
NSA_HEADS = 8
HEAD_DIM = 64
NSA_WIDTH = NSA_HEADS * HEAD_DIM
KV_HEADS = 2
HEADS_PER_KV = NSA_HEADS // KV_HEADS
KV_WIDTH = KV_HEADS * HEAD_DIM
CMP_BLOCK = 32
CMP_STRIDE = 16
CMP_HIDDEN = 256
SEL_BLOCK = 64
N_SELECT = 16
WINDOW = 512
HGRN_HEADS = 4
HGRN_DIM = 128
HGRN_WIDTH = HGRN_HEADS * HGRN_DIM
HGRN_CHUNK = 64
HGRN_SUB = 16
N_EXPERTS = 256
TOP_K = 8
N_GROUPS = 8
GROUP_SIZE = N_EXPERTS // N_GROUPS
TOPK_GROUPS = 4
EXPERT_FF = 256
SHARED_FF = 256
ROUTED_SCALE = 2.5
RMS_EPS = 1e-6
BIG = 1e9
LOG2E = 1.4426950408889634
GATE_PAD = 128
PROJ_COLS = NSA_WIDTH + 6 * KV_WIDTH + GATE_PAD + 4 * HGRN_WIDTH

VMEM_LIMIT = 56 * 1024 * 1024

TQ = 128
TK = 512
EXPERT_BLOCK = 512
HIGHEST = lax.Precision.HIGHEST


def _cparams(*sem):
    return pltpu.CompilerParams(dimension_semantics=sem, vmem_limit_bytes=VMEM_LIMIT)


def _sigmoid(x):
    return 1.0 / (1.0 + jnp.exp(-x))


def _dot_nt(a, b):
    return lax.dot_general(a, b, (((1,), (1,)), ((), ())), preferred_element_type=F32)


def _dot(a, b, **kw):
    return jnp.dot(a, b, preferred_element_type=F32, **kw)


def _split_dot(a_bf16_exact, x):
    hi = x.astype(BF16)
    lo = (x - hi.astype(F32)).astype(BF16)
    return _dot(a_bf16_exact, hi) + _dot(a_bf16_exact, lo)


def _mod_kernel(c_ref, w_ref, b_ref, o_ref):
    c = c_ref[...]
    cond = c * _sigmoid(c)
    o_ref[...] = _dot(cond, w_ref[...], precision=HIGHEST) + b_ref[...]


def _mod(c, ada_w, ada_b):
    b, d = c.shape
    rows = 8
    c_pad = jnp.zeros((rows, d), F32).at[:b].set(c)
    n = ada_w.shape[1]
    out = pl.pallas_call(
        _mod_kernel,
        grid=(n // d,),
        in_specs=[pl.BlockSpec((rows, d), lambda j: (0, 0)),
                  pl.BlockSpec((d, d), lambda j: (0, j)),
                  pl.BlockSpec((1, d), lambda j: (0, j))],
        out_specs=pl.BlockSpec((rows, d), lambda j: (0, j)),
        out_shape=jax.ShapeDtypeStruct((rows, n), F32),
        compiler_params=_cparams("parallel"),
        name="mod",
    )(c_pad, ada_w, ada_b.reshape(1, n))
    return out[:b]


def _head_rms(t, w):
    return t * lax.rsqrt(jnp.mean(t * t, axis=-1, keepdims=True) + RMS_EPS) * w


def _pos_digits(pos):
    lane = lax.broadcasted_iota(I32, pos.shape, 1)
    d0 = (lane == 0) | (lane == 3) | (lane == 6)
    d1 = (lane == 1) | (lane == 4) | (lane == 7)
    d2 = (lane == 2) | (lane == 5) | (lane == 8)
    dig = jnp.where(d0, pos >> 12, jnp.where(d1, (pos >> 6) & 63, jnp.where(d2, pos & 63, 0)))
    return dig.astype(F32)


def _inproj_kernel(x_ref, sc_ref, sh_ref, n1_ref, w_ref, qnw_ref, knw_ref, lbp_ref, qaug_ref,
                   q_ref, kcr_ref, vcr_ref, ks_ref, vst_ref, kw_ref, vwt_ref, gt_ref,
                   hq_ref, hk_ref, hlf_ref, hv_ref, hg_ref):
    x = x_ref[0]
    ms = jnp.mean(x * x, axis=-1, keepdims=True)
    h = x * lax.rsqrt(ms + RMS_EPS) * n1_ref[...] * (1.0 + sc_ref[0]) + sh_ref[0]
    p = _dot(h.astype(BF16), w_ref[...])
    tm = x.shape[0]

    qnw = qnw_ref[...]
    for hd in range(NSA_HEADS):
        t = p[:, hd * HEAD_DIM:(hd + 1) * HEAD_DIM]
        qn = _head_rms(t, qnw) * (HEAD_DIM ** -0.5 * LOG2E)
        qa = jnp.broadcast_to(qaug_ref[hd:hd + 1, :], (tm, HEAD_DIM))
        q_ref[0, hd] = jnp.concatenate([qn, qa], axis=1).astype(BF16)
    kaug = _pos_digits(pl.program_id(1) * tm + lax.broadcasted_iota(I32, (tm, HEAD_DIM), 0))

    o = NSA_WIDTH
    kcr_ref[0] = p[:, o:o + KV_WIDTH]
    vcr_ref[0] = p[:, o + KV_WIDTH:o + 2 * KV_WIDTH]
    ks = p[:, o + 2 * KV_WIDTH:o + 3 * KV_WIDTH]
    vs = p[:, o + 3 * KV_WIDTH:o + 4 * KV_WIDTH]
    kw = p[:, o + 4 * KV_WIDTH:o + 5 * KV_WIDTH]
    vw = p[:, o + 5 * KV_WIDTH:o + 6 * KV_WIDTH]
    for g in range(KV_HEADS):
        sl = slice(g * HEAD_DIM, (g + 1) * HEAD_DIM)
        ks_ref[0, g] = jnp.concatenate([_head_rms(ks[:, sl], knw_ref[1:2, :]), kaug], axis=1).astype(BF16)
        kw_ref[0, g] = jnp.concatenate([_head_rms(kw[:, sl], knw_ref[2:3, :]), kaug], axis=1).astype(BF16)
    vst = vs.T.astype(BF16)
    vwt = vw.T.astype(BF16)
    for g in range(KV_HEADS):
        vst_ref[0, g] = vst[g * HEAD_DIM:(g + 1) * HEAD_DIM, :]
        vwt_ref[0, g] = vwt[g * HEAD_DIM:(g + 1) * HEAD_DIM, :]

    o = NSA_WIDTH + 6 * KV_WIDTH
    gates = _sigmoid(p[:, o:o + GATE_PAD])
    gt_ref[0] = gates.T[:NSA_HEADS * 3, :]

    o = o + GATE_PAD
    hq = p[:, o:o + HGRN_WIDTH]
    hf = p[:, o + HGRN_WIDTH:o + 2 * HGRN_WIDTH]
    hi = p[:, o + 2 * HGRN_WIDTH:o + 3 * HGRN_WIDTH]
    hg = p[:, o + 3 * HGRN_WIDTH:o + 4 * HGRN_WIDTH]
    lbp = lbp_ref[...]
    e = jnp.exp(lbp - jnp.max(lbp, axis=0, keepdims=True))
    lb = e[0:1, :] / jnp.sum(e, axis=0, keepdims=True)
    f = lb + (1.0 - lb) * _sigmoid(hf)
    hq_ref[0] = hq * _sigmoid(hq) * (HGRN_DIM ** -0.5)
    hk_ref[0] = 1.0 - f
    hlf_ref[0] = jnp.log(f)
    hv_ref[0] = hi
    hg_ref[0] = _sigmoid(hg)


def _inproj(x, sc1, sh1, norm1_w, w_cat, q_norm_w, k_norm_w, lb_param, tm):
    b, s, d = x.shape
    row = lambda bi, i: (bi, i, 0)
    per_b = lambda bi, i: (bi, 0, 0)
    fixed2 = lambda bi, i: (0, 0)
    aw = 2 * HEAD_DIM
    rest = np.array([2.0 ** (-8.0 * (i + 1) / NSA_HEADS) for i in range(NSA_HEADS)], np.float64) * LOG2E
    qaug = np.zeros((NSA_HEADS, HEAD_DIM), np.float32)
    for i in range(3):
        term = rest.astype(np.float32).astype(BF16).astype(np.float64)
        rest = rest - term
        for dgt, wgt in enumerate((4096.0, 64.0, 1.0)):
            qaug[:, 3 * i + dgt] = term * wgt
    assert np.all(qaug == qaug.astype(BF16).astype(np.float32))
    out_shape = (
        jax.ShapeDtypeStruct((b, NSA_HEADS, s, aw), BF16),
        jax.ShapeDtypeStruct((b, s, KV_WIDTH), F32),
        jax.ShapeDtypeStruct((b, s, KV_WIDTH), F32),
        jax.ShapeDtypeStruct((b, KV_HEADS, s, aw), BF16),
        jax.ShapeDtypeStruct((b, KV_HEADS, HEAD_DIM, s), BF16),
        jax.ShapeDtypeStruct((b, KV_HEADS, s, aw), BF16),
        jax.ShapeDtypeStruct((b, KV_HEADS, HEAD_DIM, s), BF16),
        jax.ShapeDtypeStruct((b, NSA_HEADS * 3, s), F32),
    ) + tuple(jax.ShapeDtypeStruct((b, s, HGRN_WIDTH), F32) for _ in range(5))
    hm = lambda n, w: pl.BlockSpec((1, n, tm, w), lambda bi, i: (bi, 0, i, 0))
    hmt = lambda n, w: pl.BlockSpec((1, n, w, tm), lambda bi, i: (bi, 0, 0, i))
    out_specs = (
        hm(NSA_HEADS, aw),
        pl.BlockSpec((1, tm, KV_WIDTH), row),
        pl.BlockSpec((1, tm, KV_WIDTH), row),
        hm(KV_HEADS, aw), hmt(KV_HEADS, HEAD_DIM),
        hm(KV_HEADS, aw), hmt(KV_HEADS, HEAD_DIM),
        pl.BlockSpec((1, NSA_HEADS * 3, tm), lambda bi, i: (bi, 0, i)),
    ) + tuple(pl.BlockSpec((1, tm, HGRN_WIDTH), row) for _ in range(5))
    return pl.pallas_call(
        _inproj_kernel,
        grid=(b, s // tm),
        in_specs=[pl.BlockSpec((1, tm, d), row),
                  pl.BlockSpec((1, 1, d), per_b),
                  pl.BlockSpec((1, 1, d), per_b),
                  pl.BlockSpec((1, d), fixed2),
                  pl.BlockSpec((d, PROJ_COLS), fixed2),
                  pl.BlockSpec((1, HEAD_DIM), fixed2),
                  pl.BlockSpec((3, HEAD_DIM), fixed2),
                  pl.BlockSpec(lb_param.shape, fixed2),
                  pl.BlockSpec((NSA_HEADS, HEAD_DIM), fixed2)],
        out_specs=out_specs,
        out_shape=out_shape,
        compiler_params=_cparams("parallel", "parallel"),
        name="inproj",
    )(x, sc1, sh1, norm1_w, w_cat, q_norm_w, k_norm_w, lb_param, jnp.asarray(qaug))


def _gelu_tanh(x):
    return 0.5 * x * (1.0 + jnp.tanh(0.7978845608028654 * (x + 0.044715 * x * x * x)))


def _compress_kernel(kch_ref, vch_ref, pos_ref, wa_ref, wb_ref, b1_ref, w2_ref, knw_ref,
                     kc_ref, vct_ref):
    n = kch_ref.shape[1]
    outs = []
    for br, ch_ref in enumerate((kch_ref, vch_ref)):
        ch = ch_ref[0]
        a = _dot((ch + pos_ref[br, 0:1, :]).astype(BF16), wa_ref[br])
        bm = _dot((ch + pos_ref[br, 1:2, :]).astype(BF16), wb_ref[br])
        pre = a + pltpu.roll(bm, n - 1, 0) + b1_ref[br]
        hid = _gelu_tanh(pre).astype(BF16)
        outs.append([_dot(hid[:, g * CMP_HIDDEN:(g + 1) * CMP_HIDDEN], w2_ref[br]) for g in range(KV_HEADS)])
    end_digits = _pos_digits(lax.broadcasted_iota(I32, (n, HEAD_DIM), 0) * CMP_STRIDE + (CMP_BLOCK - 1))
    for g in range(KV_HEADS):
        kc_ref[0, g] = jnp.concatenate([_head_rms(outs[0][g], knw_ref[0:1, :]), end_digits], axis=1).astype(BF16)
    vct = jnp.concatenate(outs[1], axis=1).T.astype(BF16)
    for g in range(KV_HEADS):
        vct_ref[0, g] = vct[g * HEAD_DIM:(g + 1) * HEAD_DIM, :]


def _compress(kc_raw, vc_raw, cmp_pos, cmp_w1, cmp_b1, cmp_w2, k_norm_w):
    b, s, _ = kc_raw.shape
    n = s // CMP_STRIDE
    half = CMP_STRIDE
    cw = CMP_STRIDE * KV_WIDTH
    kch = kc_raw.reshape(b, n, cw)
    vch = vc_raw.reshape(b, n, cw)
    pos = cmp_pos.reshape(2, 2, half, 1, HEAD_DIM)
    pos = jnp.broadcast_to(pos, (2, 2, half, KV_HEADS, HEAD_DIM)).reshape(2, 2, cw)
    w1 = cmp_w1.reshape(2, 2, half, HEAD_DIM, CMP_HIDDEN)
    eye = jnp.eye(KV_HEADS, dtype=F32)
    wfull = jnp.einsum('rhjdn,gk->rhjgdkn', w1, eye).reshape(2, 2, cw, KV_HEADS * CMP_HIDDEN).astype(BF16)
    b1 = jnp.tile(cmp_b1.reshape(2, 1, CMP_HIDDEN), (1, 1, KV_HEADS))
    fix = lambda r: (lambda bi: (0,) * r)
    return pl.pallas_call(
        _compress_kernel,
        grid=(b,),
        in_specs=[pl.BlockSpec((1, n, cw), lambda bi: (bi, 0, 0)),
                  pl.BlockSpec((1, n, cw), lambda bi: (bi, 0, 0)),
                  pl.BlockSpec((2, 2, cw), fix(3)),
                  pl.BlockSpec((2, cw, KV_HEADS * CMP_HIDDEN), fix(3)),
                  pl.BlockSpec((2, cw, KV_HEADS * CMP_HIDDEN), fix(3)),
                  pl.BlockSpec((2, 1, KV_HEADS * CMP_HIDDEN), fix(3)),
                  pl.BlockSpec((2, CMP_HIDDEN, HEAD_DIM), fix(3)),
                  pl.BlockSpec((3, HEAD_DIM), fix(2))],
        out_specs=(pl.BlockSpec((1, KV_HEADS, n, 2 * HEAD_DIM), lambda bi: (bi, 0, 0, 0)),
                   pl.BlockSpec((1, KV_HEADS, HEAD_DIM, n), lambda bi: (bi, 0, 0, 0))),
        out_shape=(jax.ShapeDtypeStruct((b, KV_HEADS, n, 2 * HEAD_DIM), BF16),
                   jax.ShapeDtypeStruct((b, KV_HEADS, HEAD_DIM, n), BF16)),
        compiler_params=_cparams("parallel"),
        name="compress",
    )(kch, vch, pos, wfull[:, 0], wfull[:, 1], b1, cmp_w2.astype(BF16), k_norm_w)


def _nsa_kernel(q_ref, kc_ref, vct_ref, ks_ref, vst_ref, kw_ref, vwt_ref, gt_ref, cdiff_ref, wdiff_ref,
                ovl_ref, oh_ref, onw_ref, o_ref, buf_a, buf_b, m_scr, acc_scr, *, n_top):
    q0 = pl.program_id(2) * TQ
    ncols = HEADS_PER_KV * TQ
    q = q_ref[0].reshape(ncols, 2 * HEAD_DIM)
    ns = ovl_ref.shape[0]

    s = jnp.where(cdiff_ref[...] <= q0, _dot_nt(kc_ref[0, 0], q), -jnp.inf)
    m = jnp.max(s, axis=0, keepdims=True)
    m = jnp.where(m == -jnp.inf, 0.0, m)
    e = jnp.exp2(s - m)
    p = e / jnp.maximum(jnp.sum(e, axis=0, keepdims=True), 1e-30)
    o_c = _dot(vct_ref[0, 0], p.astype(BF16))

    nw = WINDOW + TQ
    start = pl.multiple_of(jnp.maximum(q0 - WINDOW, 0), TQ)
    dist = wdiff_ref[...] + (q0 - start)
    sw = jnp.where((dist >= 0) & (dist < WINDOW), _dot_nt(kw_ref[0, 0, pl.ds(start, nw), :], q), -jnp.inf)
    ew = jnp.exp2(sw - jnp.max(sw, axis=0, keepdims=True))
    o_w = _dot(vwt_ref[0, 0, :, pl.ds(start, nw)], ew.astype(BF16)) / jnp.sum(ew, axis=0, keepdims=True)

    psum = p[:, 0:TQ]
    for hh in range(1, HEADS_PER_KV):
        psum = psum + p[:, hh * TQ:(hh + 1) * TQ]
    imp = _split_dot(ovl_ref[...], psum)
    blk = lax.broadcasted_iota(I32, (ns, TQ), 0)
    tq = q0 + lax.broadcasted_iota(I32, (ns, TQ), 1)
    cur = tq >> 6
    forced = (blk == 0) | (blk == cur) | (blk == cur - 1)
    rank = jnp.where(forced, BIG, jnp.where(blk * SEL_BLOCK <= tq, imp, -BIG))

    blkf = blk.astype(F32)

    bias = jnp.full((ns, TQ), -1e30, F32)
    for _ in range(n_top):
        mx = jnp.max(rank, axis=0, keepdims=True)
        first = jnp.min(jnp.where(rank == mx, blkf, float(ns)), axis=0, keepdims=True)
        hit = blkf == first
        rank = jnp.where(hit, -jnp.inf, rank)
        bias = jnp.where(hit, 0.0, bias)

    if ns < 128:
        bias = jnp.concatenate([bias, jnp.zeros((128 - ns, TQ), F32)], axis=0)
    bias_t = bias.T.astype(BF16)
    qq = jnp.concatenate([q, jnp.concatenate([bias_t] * HEADS_PER_KV, axis=0)], axis=1)
    ones_rows = jnp.ones((16, TK), BF16)

    def scores(j):
        k0 = pl.multiple_of(j * TK, TK)
        kk = jnp.concatenate([ks_ref[0, 0, pl.ds(k0, TK), :], oh_ref[pl.ds(k0, TK), :]], axis=1)
        return _dot_nt(kk, qq)

    def consume(buf, j, causal, part):
        sc = buf[...]
        if causal:
            sc = jnp.where(wdiff_ref[0:TK, :] + (q0 - j * TK) >= 0, sc, -1e30)
        k0 = pl.multiple_of(j * TK, TK)
        m_run = m_scr[part]
        m_new = jnp.maximum(m_run, jnp.max(sc, axis=0, keepdims=True))
        ex = jnp.exp2(sc - m_new).astype(BF16)
        va = jnp.concatenate([vst_ref[0, 0, :, pl.ds(k0, TK)], ones_rows], axis=0)
        acc_scr[part] = jnp.exp2(m_run - m_new) * acc_scr[part] + _dot(va, ex)
        m_scr[part] = m_new

    n_past = q0 // TK
    m_scr[...] = jnp.full(m_scr.shape, -1e30, F32)
    acc_scr[...] = jnp.zeros(acc_scr.shape, F32)
    buf_a[...] = scores(0)

    def tiles(first, count):
        for u in range(0, count, 2):
            buf_b[...] = scores(first + u + 1)
            consume(buf_a, first + u, False, 0)
            buf_a[...] = scores(first + u + 2)
            consume(buf_b, first + u + 1, False, 1)
        return 0

    lax.fori_loop(0, n_past // 4, lambda i, _: tiles(4 * i, 4), 0)
    lax.fori_loop(0, (n_past // 2) % 2, lambda i, _: tiles((n_past // 4) * 4, 2), 0)

    @pl.when(n_past % 2 == 1)
    def _():
        buf_b[...] = scores(n_past)
        consume(buf_a, n_past - 1, False, 0)
        consume(buf_b, n_past, True, 1)

    @pl.when(n_past % 2 == 0)
    def _():
        consume(buf_a, n_past, True, 0)

    m_all = jnp.maximum(m_scr[0], m_scr[1])
    acc_s = jnp.exp2(m_scr[0] - m_all) * acc_scr[0] + jnp.exp2(m_scr[1] - m_all) * acc_scr[1]
    o_s = acc_s[0:HEAD_DIM, :] / acc_s[HEAD_DIM:HEAD_DIM + 1, :]

    gt = gt_ref[0, 0]
    outs = []
    for hh in range(HEADS_PER_KV):
        cs = slice(hh * TQ, (hh + 1) * TQ)
        o = (gt[3 * hh:3 * hh + 1, :] * o_c[:, cs] + gt[3 * hh + 1:3 * hh + 2, :] * o_s[:, cs]
             + gt[3 * hh + 2:3 * hh + 3, :] * o_w[:, cs])
        o = o * lax.rsqrt(jnp.mean(o * o, axis=0, keepdims=True) + RMS_EPS) * onw_ref[0, hh]
        outs.append(o)
    o_ref[0] = jnp.concatenate(outs, axis=0).T


def _nsa(q, kc, vct, ks, vst, kw, vwt, gates_t, attn_out_norm_w):
    b, _, s, aw = q.shape
    nc = kc.shape[2]
    ns = s // SEL_BLOCK
    n_top = min(N_SELECT, ns)
    ncols = HEADS_PER_KV * TQ
    nw = WINDOW + TQ
    tl = np.arange(ncols)[None, :] & (TQ - 1)
    cdiff = jnp.asarray((np.arange(nc)[:, None] * CMP_STRIDE + (CMP_BLOCK - 1) - tl).astype(np.int32))
    wdiff = jnp.asarray((tl - np.arange(nw)[:, None]).astype(np.int32))
    ci = np.arange(nc)[None, :] * CMP_STRIDE
    bj = np.arange(ns)[:, None]
    ovl = ((ci < (bj + 1) * SEL_BLOCK) & (ci + CMP_BLOCK > bj * SEL_BLOCK) & (np.arange(nc)[None, :] < nc - 1))
    ovl = jnp.asarray(ovl.astype(np.float32)).astype(BF16)
    assert ns <= 128
    onehot = (np.arange(s)[:, None] // SEL_BLOCK == np.arange(128)[None, :])
    onehot = jnp.asarray(onehot.astype(np.float32)).astype(BF16)
    onw = jnp.broadcast_to(attn_out_norm_w.reshape(KV_HEADS, HEADS_PER_KV, HEAD_DIM, 1),
                           (KV_HEADS, HEADS_PER_KV, HEAD_DIM, TQ))
    gt = gates_t.reshape(b, KV_HEADS, HEADS_PER_KV * 3, s)
    per_bg = lambda bi, g, i: (bi, g, 0, 0)
    fixed = lambda bi, g, i: (0, 0)
    return pl.pallas_call(
        functools.partial(_nsa_kernel, n_top=n_top),
        grid=(b, KV_HEADS, s // TQ),
        in_specs=[pl.BlockSpec((1, HEADS_PER_KV, TQ, aw), lambda bi, g, i: (bi, g, i, 0)),
                  pl.BlockSpec((1, 1, nc, aw), per_bg),
                  pl.BlockSpec((1, 1, HEAD_DIM, nc), per_bg),
                  pl.BlockSpec((1, 1, s, aw), per_bg),
                  pl.BlockSpec((1, 1, HEAD_DIM, s), per_bg),
                  pl.BlockSpec((1, 1, s, aw), per_bg),
                  pl.BlockSpec((1, 1, HEAD_DIM, s), per_bg),
                  pl.BlockSpec((1, 1, HEADS_PER_KV * 3, TQ), lambda bi, g, i: (bi, g, 0, i)),
                  pl.BlockSpec((nc, ncols), fixed),
                  pl.BlockSpec((nw, ncols), fixed),
                  pl.BlockSpec((ns, nc), fixed),
                  pl.BlockSpec((s, 128), fixed),
                  pl.BlockSpec((1, HEADS_PER_KV, HEAD_DIM, TQ), lambda bi, g, i: (g, 0, 0, 0))],
        out_specs=pl.BlockSpec((1, TQ, HEADS_PER_KV * HEAD_DIM), lambda bi, g, i: (bi, i, g)),
        out_shape=jax.ShapeDtypeStruct((b, s, NSA_WIDTH), F32),
        scratch_shapes=[pltpu.VMEM((TK, ncols), F32), pltpu.VMEM((TK, ncols), F32),
                        pltpu.VMEM((2, 1, ncols), F32), pltpu.VMEM((2, HEAD_DIM + 16, ncols), F32)],
        compiler_params=_cparams("parallel", "parallel", "arbitrary"),
        name="nsa",
    )(q, kc, vct, ks, vst, kw, vwt, gt, cdiff, wdiff, ovl, onehot, onw)


def _hgrn_kernel(q_ref, k_ref, lf_ref, v_ref, g_ref, onw_ref, o_ref, state_scr, *, n_chunks):
    c = HGRN_CHUNK

    @pl.when(pl.program_id(1) == 0)
    def _():
        state_scr[...] = jnp.zeros_like(state_scr)

    ri = lax.broadcasted_iota(I32, (c, c), 0)
    ci = lax.broadcasted_iota(I32, (c, c), 1)
    tril = (ri >= ci).astype(F32)
    rsub = ri // HGRN_SUB
    rin = ri & (HGRN_SUB - 1)

    def head_chunk(r0, hd, state_t):
        cols = slice(hd * HGRN_DIM, (hd + 1) * HGRN_DIM)
        q = q_ref[0, pl.ds(r0, c), cols]
        k = k_ref[0, pl.ds(r0, c), cols]
        lf = lf_ref[0, pl.ds(r0, c), cols]
        v = v_ref[0, pl.ds(r0, c), cols]
        cum = _dot(tril, lf, precision=HIGHEST)
        o = _dot_nt((q * jnp.exp(cum)).astype(BF16), state_t.astype(BF16))
        scores = jnp.zeros((c, c), F32)
        for i in range(1, c // HGRN_SUB):
            ref_row = cum[i * HGRN_SUB - 1:i * HGRN_SUB, :]
            qs = q * jnp.exp(jnp.minimum(cum - ref_row, 0.0))
            kd = k * jnp.exp(jnp.minimum(ref_row - cum, 0.0))
            blk = _dot_nt(qs.astype(BF16), kd.astype(BF16))
            scores = jnp.where((rsub == i) & (ci < i * HGRN_SUB), blk, scores)
        for d in range(HGRN_SUB):
            if d == 0:
                w = jnp.sum(q * k, axis=-1, keepdims=True)
            else:
                ksh = pltpu.roll(k, d, 0)
                csh = pltpu.roll(cum, d, 0)
                w = jnp.sum(q * ksh * jnp.exp(jnp.minimum(cum - csh, 0.0)), axis=-1, keepdims=True)
            scores = jnp.where((ri - ci == d) & (rin >= d), w, scores)
        o = o + _dot(scores.astype(BF16), v.astype(BF16))
        last = cum[c - 1:c, :]
        kd = (k * jnp.exp(last - cum)).astype(BF16)
        state_t = state_t * jnp.exp(last) + _dot(v.T.astype(BF16), kd)
        o = o * g_ref[0, pl.ds(r0, c), cols]
        o = o * lax.rsqrt(jnp.mean(o * o, axis=-1, keepdims=True) + RMS_EPS) * onw_ref[:, cols]
        o_ref[0, pl.ds(r0, c), cols] = o
        return state_t

    def chunk(ck, states):
        r0 = pl.multiple_of(ck * c, c)
        return tuple(head_chunk(r0, hd, states[hd]) for hd in range(HGRN_HEADS))

    states = lax.fori_loop(0, n_chunks, chunk, tuple(state_scr[hd] for hd in range(HGRN_HEADS)))
    for hd in range(HGRN_HEADS):
        state_scr[hd] = states[hd]


def _hgrn(hq, hk, hlf, hv, hg, rec_out_norm_w, rows):
    b, s, _ = hq.shape
    blk = pl.BlockSpec((1, rows, HGRN_WIDTH), lambda bi, i: (bi, i, 0))
    return pl.pallas_call(
        functools.partial(_hgrn_kernel, n_chunks=rows // HGRN_CHUNK),
        grid=(b, s // rows),
        in_specs=[blk, blk, blk, blk, blk,
                  pl.BlockSpec((1, HGRN_WIDTH), lambda bi, i: (0, 0))],
        out_specs=blk,
        out_shape=jax.ShapeDtypeStruct((b, s, HGRN_WIDTH), F32),
        scratch_shapes=[pltpu.VMEM((HGRN_HEADS, HGRN_DIM, HGRN_DIM), F32)],
        compiler_params=_cparams("parallel", "arbitrary"),
        name="hgrn",
    )(hq, hk, hlf, hv, hg, rec_out_norm_w.reshape(1, HGRN_WIDTH))


def _outproj_kernel(x_ref, a_ref, r_ref, wa_ref, wr_ref, gt_ref, sc_ref, sh_ref, n2_ref, x1_ref, h2_ref, h2p_ref):
    mixed = _dot(a_ref[0].astype(BF16), wa_ref[...]) + _dot(r_ref[0].astype(BF16), wr_ref[...])
    x1 = x_ref[0] + gt_ref[0] * mixed
    x1_ref[0] = x1
    ms = jnp.mean(x1 * x1, axis=-1, keepdims=True)
    h2 = x1 * lax.rsqrt(ms + RMS_EPS) * n2_ref[...] * (1.0 + sc_ref[0]) + sh_ref[0]
    h2_ref[0] = h2
    h2p_ref[0] = _pack_bf16_pair(h2[:, :D_MODEL // 2], h2[:, D_MODEL // 2:])


def _outproj(x, attn, rec, w_out, gt1, sc2, sh2, norm2_w, tm):
    b, s, d = x.shape
    row = lambda bi, i: (bi, i, 0)
    per_b = lambda bi, i: (bi, 0, 0)
    fixed2 = lambda bi, i: (0, 0)
    w = w_out.astype(BF16)
    return pl.pallas_call(
        _outproj_kernel,
        grid=(b, s // tm),
        in_specs=[pl.BlockSpec((1, tm, d), row),
                  pl.BlockSpec((1, tm, NSA_WIDTH), row),
                  pl.BlockSpec((1, tm, HGRN_WIDTH), row),
                  pl.BlockSpec((NSA_WIDTH, d), fixed2),
                  pl.BlockSpec((HGRN_WIDTH, d), fixed2),
                  pl.BlockSpec((1, 1, d), per_b),
                  pl.BlockSpec((1, 1, d), per_b),
                  pl.BlockSpec((1, 1, d), per_b),
                  pl.BlockSpec((1, d), fixed2)],
        out_specs=(pl.BlockSpec((1, tm, d), row), pl.BlockSpec((1, tm, d), row), pl.BlockSpec((1, tm, d // 2), row)),
        out_shape=(jax.ShapeDtypeStruct((b, s, d), F32), jax.ShapeDtypeStruct((b, s, d), F32),
                   jax.ShapeDtypeStruct((b, s, d // 2), jnp.uint32)),
        compiler_params=_cparams("parallel", "parallel"),
        name="outproj",
    )(x, attn, rec, w[:NSA_WIDTH], w[NSA_WIDTH:], gt1, sc2, sh2, norm2_w)


def _mixer(x, c, ada_w, ada_b, norm1_w, norm2_w, w_in, q_norm_w, k_norm_w, cmp_pos, cmp_w1, cmp_b1, cmp_w2,
           attn_out_norm_w, hgrn_lb_param, rec_out_norm_w, w_out):
    b, s, d = x.shape
    mod = _mod(c, ada_w, ada_b)
    sh1, sc1, gt1, sh2, sc2, gt2 = [m.reshape(b, 1, d) for m in jnp.split(mod, 6, axis=-1)]
    o = NSA_WIDTH + 6 * KV_WIDTH
    w_cat = jnp.concatenate([w_in[:, :o], w_in[:, o:o + NSA_HEADS * 3],
                             jnp.zeros((d, GATE_PAD - NSA_HEADS * 3), w_in.dtype),
                             w_in[:, o + NSA_HEADS * 3:]], axis=1).astype(BF16)
    tm = min(256, s)
    (q, kc_raw, vc_raw, ks, vst, kw, vwt, gates_t, hq, hk, hlf, hv, hg) = _inproj(
        x, sc1, sh1, norm1_w.reshape(1, d), w_cat, q_norm_w.reshape(1, HEAD_DIM), k_norm_w, hgrn_lb_param, tm)
    kc, vct = _compress(kc_raw, vc_raw, cmp_pos, cmp_w1, cmp_b1, cmp_w2, k_norm_w)
    attn = _nsa(q, kc, vct, ks, vst, kw, vwt, gates_t, attn_out_norm_w)
    rec = _hgrn(hq, hk, hlf, hv, hg, rec_out_norm_w, min(512, s))
    x1, h2, h2p = _outproj(x, attn, rec, w_out, gt1, sc2, sh2, norm2_w.reshape(1, d), tm)
    return x1, h2, h2p, gt2


def _router_kernel(h_ref, rwt_ref, bias_ref, tri_ref, ones_ref, idx_ref, w_ref, rank_ref, cnt_ref, carry_scr, *, tr):
    @pl.when(pl.program_id(0) == 0)
    def _():
        carry_scr[...] = jnp.zeros_like(carry_scr)

    h = h_ref[...]
    h_hi = h.astype(BF16)
    h_lo = (h - h_hi.astype(F32)).astype(BF16)
    logits = _dot_nt(rwt_ref[0], h_hi) + _dot_nt(rwt_ref[1], h_hi) + _dot_nt(rwt_ref[0], h_lo)
    scores = _sigmoid(logits)
    biased = scores + bias_ref[...]
    neg = -jnp.inf

    gs = []
    for g in range(N_GROUPS):
        sub = biased[g * GROUP_SIZE:(g + 1) * GROUP_SIZE, :]
        m1 = jnp.max(sub, axis=0, keepdims=True)
        dup = jnp.sum((sub == m1).astype(F32), axis=0, keepdims=True)
        m2 = jnp.max(jnp.where(sub < m1, sub, neg), axis=0, keepdims=True)
        gs.append(m1 + jnp.where(dup >= 2.0, m1, m2))
    parts = []
    for g in range(N_GROUPS):
        beaten = jnp.zeros_like(gs[g])
        for g2 in range(N_GROUPS):
            if g2 != g:
                beats = (gs[g2] >= gs[g]) if g2 < g else (gs[g2] > gs[g])
                beaten = beaten + beats.astype(F32)
        sub = biased[g * GROUP_SIZE:(g + 1) * GROUP_SIZE, :]
        parts.append(jnp.where(beaten < float(TOPK_GROUPS), sub, neg))
    cand = jnp.concatenate(parts, axis=0)

    rowf = lax.broadcasted_iota(I32, (N_EXPERTS, tr), 0).astype(F32)
    idx_rows, w_rows, hits = [], [], []
    multi = jnp.zeros((N_EXPERTS, tr), F32)
    for _ in range(TOP_K):
        mx = jnp.max(cand, axis=0, keepdims=True)
        first = jnp.min(jnp.where(cand == mx, rowf, float(N_EXPERTS)), axis=0, keepdims=True)
        hit = rowf == first
        idx_rows.append(first)
        w_rows.append(jnp.sum(jnp.where(hit, scores, 0.0), axis=0, keepdims=True))
        cand = jnp.where(hit, neg, cand)
        multi = jnp.where(hit, 1.0, multi)
    w = jnp.concatenate(w_rows, axis=0)
    w_ref[...] = w / jnp.sum(w, axis=0, keepdims=True) * ROUTED_SCALE
    idx = jnp.concatenate(idx_rows, axis=0)
    idx_ref[...] = idx.astype(I32)

    carry = carry_scr[...]
    mb = multi.astype(BF16)
    before = _dot(mb, tri_ref[...]) + jnp.concatenate([carry] * (tr // 128), axis=1)
    rank_rows = [jnp.sum(jnp.where(rowf == idx_rows[k], before, 0.0), axis=0, keepdims=True) for k in range(TOP_K)]
    rank_ref[...] = jnp.concatenate(rank_rows, axis=0).astype(I32)
    carry = carry + _dot(mb, ones_ref[...])
    carry_scr[...] = carry
    cnt_ref[...] = carry


def _router(h2, router_w, router_bias, tr):
    t, d = h2.shape
    tri = jnp.asarray(np.triu(np.ones((tr, tr), np.float32), 1)).astype(BF16)
    ones = jnp.ones((tr, 128), BF16)
    tok = pl.BlockSpec((TOP_K, tr), lambda i: (0, i))
    fixed = lambda i: (0, 0)
    rwt = router_w.T
    rwt_hi = rwt.astype(BF16)
    rwt_split = jnp.stack([rwt_hi, (rwt - rwt_hi.astype(F32)).astype(BF16)])
    return pl.pallas_call(
        functools.partial(_router_kernel, tr=tr),
        grid=(t // tr,),
        in_specs=[pl.BlockSpec((tr, d), lambda i: (i, 0)),
                  pl.BlockSpec((2, N_EXPERTS, d), lambda i: (0, 0, 0)),
                  pl.BlockSpec((N_EXPERTS, 1), fixed),
                  pl.BlockSpec((tr, tr), fixed),
                  pl.BlockSpec((tr, 128), fixed)],
        out_specs=(tok, tok, tok, pl.BlockSpec((N_EXPERTS, 128), fixed)),
        out_shape=(jax.ShapeDtypeStruct((TOP_K, t), I32), jax.ShapeDtypeStruct((TOP_K, t), F32),
                   jax.ShapeDtypeStruct((TOP_K, t), I32), jax.ShapeDtypeStruct((N_EXPERTS, 128), F32)),
        scratch_shapes=[pltpu.VMEM((N_EXPERTS, 128), F32)],
        compiler_params=_cparams("arbitrary"),
        name="router",
    )(h2, rwt_split, router_bias.reshape(N_EXPERTS, 1), tri, ones)


def _pack_bf16_pair(a, b):
    ua = lax.bitcast_convert_type(a.astype(BF16).astype(F32), jnp.uint32)
    ub = lax.bitcast_convert_type(b.astype(BF16).astype(F32), jnp.uint32)
    return ua | (ub >> 16)


def _unpack_bf16_pair(w):
    a = lax.bitcast_convert_type(w & jnp.uint32(0xFFFF0000), F32)
    b = lax.bitcast_convert_type(w << 16, F32)
    return a, b


def _slot_kernel(ps_ref, idx_ref, rank_ref, slot_ref):
    idx = idx_ref[...]

    def body(e, acc):
        return jnp.where(idx == e, ps_ref[e], acc)

    slot_ref[...] = lax.fori_loop(0, N_EXPERTS, body, jnp.zeros_like(idx)) + rank_ref[...]


def _slots(pad_start, idx, rank, tt):
    t = idx.shape[1]
    tok = pl.BlockSpec((TOP_K, tt), lambda i, ps: (0, i))
    return pl.pallas_call(
        _slot_kernel,
        grid_spec=pltpu.PrefetchScalarGridSpec(num_scalar_prefetch=1, grid=(t // tt,),
                                               in_specs=[tok, tok], out_specs=tok),
        out_shape=jax.ShapeDtypeStruct((TOP_K, t), I32),
        compiler_params=_cparams("parallel"),
        name="slots",
    )(pad_start, idx, rank)


SC_CORES = 2
SC_SUBCORES = 16
SC_CHUNK = 64


def _sc_mesh():
    return plsc.VectorSubcoreMesh(core_axis_name="c", subcore_axis_name="s")


def _sc_dispatch(h2p, slot_chunks, n_rows):
    t, dw = h2p.shape
    per = slot_chunks.shape[0] // (SC_CORES * SC_SUBCORES)

    def body(h_hbm, slot_hbm, xs_hbm, idx_v, rows_v, sem):
        wid = lax.axis_index("s") * SC_CORES + lax.axis_index("c")

        @pl.loop(0, per)
        def _(c):
            ch = wid * per + c
            pltpu.sync_copy(slot_hbm.at[ch], idx_v)
            pltpu.sync_copy(h_hbm.at[pl.ds(ch * SC_CHUNK, SC_CHUNK)], rows_v)
            copies = [pltpu.async_copy(rows_v, xs_hbm.at[idx_v.at[k]], sem) for k in range(TOP_K)]
            for cp in copies:
                cp.wait()

    return pl.kernel(
        body, out_type=jax.ShapeDtypeStruct((n_rows, dw), h2p.dtype), mesh=_sc_mesh(),
        scratch_types=[pltpu.VMEM((TOP_K, SC_CHUNK), I32), pltpu.VMEM((SC_CHUNK, dw), h2p.dtype),
                       pltpu.SemaphoreType.DMA],
    )(h2p, slot_chunks)


def _sc_gather(ys, slot_chunks, t):
    dw = ys.shape[1]
    per = slot_chunks.shape[0] // (SC_CORES * SC_SUBCORES)

    def body(ys_hbm, slot_hbm, yg_hbm, idx_v, rows_v, sem):
        wid = lax.axis_index("s") * SC_CORES + lax.axis_index("c")

        @pl.loop(0, per)
        def _(c):
            ch = wid * per + c
            pltpu.sync_copy(slot_hbm.at[ch], idx_v)
            for k in range(TOP_K):
                pltpu.async_copy(ys_hbm.at[idx_v.at[k]], rows_v, sem).wait()
                pltpu.sync_copy(rows_v, yg_hbm.at[k, pl.ds(ch * SC_CHUNK, SC_CHUNK)])

    return pl.kernel(
        body, out_type=jax.ShapeDtypeStruct((TOP_K, t, dw), ys.dtype), mesh=_sc_mesh(),
        scratch_types=[pltpu.VMEM((TOP_K, SC_CHUNK), I32), pltpu.VMEM((SC_CHUNK, dw), ys.dtype),
                       pltpu.SemaphoreType.DMA],
    )(ys, slot_chunks)


def _experts_kernel(be_ref, nu_ref, bv_ref, xs_ref, wg_ref, wu_ref, wd_ref, ys_ref):
    i = pl.program_id(0)
    half = D_MODEL // 2

    @pl.when(i < nu_ref[0])
    def _():
        live = lax.broadcasted_iota(I32, xs_ref.shape, 0) < bv_ref[i]
        xa, xb = _unpack_bf16_pair(jnp.where(live, xs_ref[...], jnp.uint32(0)))
        xa, xb = xa.astype(BF16), xb.astype(BF16)
        g = _dot(xa, wg_ref[0, :half].astype(BF16)) + _dot(xb, wg_ref[0, half:].astype(BF16))
        u = _dot(xa, wu_ref[0, :half].astype(BF16)) + _dot(xb, wu_ref[0, half:].astype(BF16))
        act = (g * _sigmoid(g) * u).astype(BF16)
        y = _dot(act, wd_ref[0].astype(BF16))
        ys_ref[...] = _pack_bf16_pair(y[:, :half], y[:, half:])

    @pl.when(i >= nu_ref[0])
    def _():
        ys_ref[...] = jnp.zeros_like(ys_ref)


def _experts(xs, blk_e, n_used, blk_valid, w_gate, w_up, w_down):
    n_rows, dw = xs.shape
    d = w_gate.shape[1]
    nblk = n_rows // EXPERT_BLOCK
    row_map = lambda i, be, nu, bv: (jnp.minimum(i, nu[0] - 1), 0)
    w_map = lambda i, be, nu, bv: (be[i], 0, 0)
    return pl.pallas_call(
        _experts_kernel,
        grid_spec=pltpu.PrefetchScalarGridSpec(
            num_scalar_prefetch=3,
            grid=(nblk,),
            in_specs=[pl.BlockSpec((EXPERT_BLOCK, dw), row_map),
                      pl.BlockSpec((1, d, EXPERT_FF), w_map),
                      pl.BlockSpec((1, d, EXPERT_FF), w_map),
                      pl.BlockSpec((1, EXPERT_FF, d), w_map)],
            out_specs=pl.BlockSpec((EXPERT_BLOCK, dw), lambda i, be, nu, bv: (i, 0))),
        out_shape=jax.ShapeDtypeStruct((n_rows, dw), xs.dtype),
        compiler_params=_cparams("arbitrary"),
        name="experts",
    )(blk_e, n_used, blk_valid, xs, w_gate, w_up, w_down)


def _combine_kernel(x1_ref, h_ref, w_ref, gt_ref, sg_ref, su_ref, sd_ref, yg_ref, o_ref):
    tc = x1_ref.shape[0]
    half = D_MODEL // 2
    hb = h_ref[...].astype(BF16)
    g = _dot(hb, sg_ref[...])
    u = _dot(hb, su_ref[...])
    ffn = _dot((g * _sigmoid(g) * u).astype(BF16), sd_ref[...])

    w = w_ref[...]
    ra = jnp.zeros((tc, half), F32)
    rb = jnp.zeros((tc, half), F32)
    for k in range(TOP_K):
        ya, yb = _unpack_bf16_pair(yg_ref[k])
        ra = ra + w[:, k:k + 1] * ya
        rb = rb + w[:, k:k + 1] * yb
    ffn = ffn + jnp.concatenate([ra, rb], axis=1)
    o_ref[...] = x1_ref[...] + gt_ref[0] * ffn


def _combine(x1, h2, w_tok, gt2, yg, sg, su, sd, seq, tc):
    t, d = x1.shape
    row = lambda i: (i, 0)
    fixed = lambda i: (0, 0)
    return pl.pallas_call(
        _combine_kernel,
        grid=(t // tc,),
        in_specs=[pl.BlockSpec((tc, d), row),
                  pl.BlockSpec((tc, d), row),
                  pl.BlockSpec((tc, TOP_K), row),
                  pl.BlockSpec((1, 1, d), lambda i: ((i * tc) // seq, 0, 0)),
                  pl.BlockSpec((d, SHARED_FF), fixed),
                  pl.BlockSpec((d, SHARED_FF), fixed),
                  pl.BlockSpec((SHARED_FF, d), fixed),
                  pl.BlockSpec((TOP_K, tc, d // 2), lambda i: (0, i, 0))],
        out_specs=pl.BlockSpec((tc, d), row),
        out_shape=jax.ShapeDtypeStruct((t, d), F32),
        compiler_params=_cparams("parallel"),
        name="combine",
    )(x1, h2, w_tok, gt2, sg.astype(BF16), su.astype(BF16), sd.astype(BF16), yg)


def _moe_parts(x1, h2, h2p, gt2, router_w, router_bias, w_gate, w_up, w_down, sg, su, sd):
    b, s, d = x1.shape
    t = b * s
    h2 = h2.reshape(t, d)
    idx, w, rank, cnt = _router(h2, router_w, router_bias, min(256, t))
    counts = cnt[:, 0].astype(I32)
    padded = (counts + EXPERT_BLOCK - 1) // EXPERT_BLOCK * EXPERT_BLOCK
    pad_end = jnp.cumsum(padded)
    pad_start = pad_end - padded
    n_rows = t * TOP_K + N_EXPERTS * EXPERT_BLOCK
    nblk = n_rows // EXPERT_BLOCK
    n_used = (pad_end[-1:] // EXPERT_BLOCK).astype(I32)
    blk_e = jnp.searchsorted(pad_end, jnp.arange(nblk, dtype=I32) * EXPERT_BLOCK, side='right')
    blk_e = jnp.minimum(blk_e, N_EXPERTS - 1).astype(I32)
    blk_e = jnp.where(jnp.arange(nblk) < n_used[0], blk_e, blk_e[jnp.maximum(n_used[0] - 1, 0)])
    blk_start = jnp.arange(nblk, dtype=I32) * EXPERT_BLOCK
    blk_valid = jnp.clip(counts[blk_e] - (blk_start - pad_start[blk_e]), 0, EXPERT_BLOCK).astype(I32)
    slot = _slots(pad_start.astype(I32), idx, rank, min(2048, t))
    slot_chunks = slot.reshape(TOP_K, t // SC_CHUNK, SC_CHUNK).transpose(1, 0, 2)
    xs = _sc_dispatch(h2p.reshape(t, d // 2), slot_chunks, n_rows)
    ys = _experts(xs, blk_e, n_used, blk_valid, w_gate, w_up, w_down)
    yg = _sc_gather(ys, slot_chunks, t)
    out = _combine(x1.reshape(t, d), h2, w.T, gt2, yg, sg, su, sd, s, min(256, t))
    return out.reshape(b, s, d), dict(idx=idx, w=w, rank=rank, cnt=cnt)


def kernel(x, c, ada_w, ada_b, norm1_w, norm2_w, w_in, q_norm_w, k_norm_w, cmp_pos, cmp_w1, cmp_b1, cmp_w2, attn_out_norm_w, hgrn_lb_param, rec_out_norm_w, w_out, router_w, router_bias, exp_w_gate, exp_w_up, exp_w_down, shared_w_gate, shared_w_up, shared_w_down):
    assert ada_w.shape[0] == 1, "one layer"
    assert x.shape[0] <= 8 and x.shape[1] % TK == 0 and x.shape[1] >= WINDOW + TQ
    l = 0
    x1, h2, h2p, gt2 = _mixer(x, c, ada_w[l], ada_b[l], norm1_w[l], norm2_w[l], w_in[l], q_norm_w[l], k_norm_w[l],
                         cmp_pos[l], cmp_w1[l], cmp_b1[l], cmp_w2[l], attn_out_norm_w[l], hgrn_lb_param,
                         rec_out_norm_w[l], w_out[l])
    out, _ = _moe_parts(x1, h2, h2p, gt2, router_w[l], router_bias[l], exp_w_gate[l], exp_w_up[l], exp_w_down[l],
                        shared_w_gate[l], shared_w_up[l], shared_w_down[l])
    return out
```

```python
import functools

import numpy as np
import jax
import jax.numpy as jnp
from jax import lax
from jax.experimental import pallas as pl
from jax.experimental.pallas import tpu as pltpu
from jax.experimental.pallas import tpu_sc as plsc

F32 = jnp.float32
BF16 = jnp.bfloat16
I32 = jnp.int32

D_MODEL = 1024
NSA_HEADS = 8
HEAD_DIM = 64
NSA_WIDTH = NSA_HEADS * HEAD_DIM
KV_HEADS = 2
HEADS_PER_KV = NSA_HEADS // KV_HEADS
KV_WIDTH = KV_HEADS * HEAD_DIM
CMP_BLOCK = 32
CMP_STRIDE = 16
CMP_HIDDEN = 256
SEL_BLOCK = 64
N_SELECT = 16
WINDOW = 512
HGRN_HEADS = 4
HGRN_DIM = 128
HGRN_WIDTH = HGRN_HEADS * HGRN_DIM
HGRN_CHUNK = 64
HGRN_SUB = 16
N_EXPERTS = 256
TOP_K = 8
N_GROUPS = 8
GROUP_SIZE = N_EXPERTS // N_GROUPS
TOPK_GROUPS = 4
EXPERT_FF = 256
SHARED_FF = 256
ROUTED_SCALE = 2.5
RMS_EPS = 1e-6
BIG = 1e9
LOG2E = 1.4426950408889634
GATE_PAD = 128
PROJ_COLS = NSA_WIDTH + 6 * KV_WIDTH + GATE_PAD + 4 * HGRN_WIDTH

VMEM_LIMIT = 56 * 1024 * 1024

TQ = 128
TK = 512
EXPERT_BLOCK = 512
HIGHEST = lax.Precision.HIGHEST


def _cparams(*sem):
    return pltpu.CompilerParams(dimension_semantics=sem, vmem_limit_bytes=VMEM_LIMIT)


def _sigmoid(x):
    return 1.0 / (1.0 + jnp.exp(-x))


def _dot_nt(a, b):
    return lax.dot_general(a, b, (((1,), (1,)), ((), ())), preferred_element_type=F32)


def _dot(a, b, **kw):
    return jnp.dot(a, b, preferred_element_type=F32, **kw)


def _split_dot(a_bf16_exact, x):
    hi = x.astype(BF16)
    lo = (x - hi.astype(F32)).astype(BF16)
    return _dot(a_bf16_exact, hi) + _dot(a_bf16_exact, lo)


def _mod_kernel(c_ref, w_ref, b_ref, o_ref):
    c = c_ref[...]
    cond = c * _sigmoid(c)
    o_ref[...] = _dot(cond, w_ref[...], precision=HIGHEST) + b_ref[...]


def _mod(c, ada_w, ada_b):
    b, d = c.shape
    rows = 8
    c_pad = jnp.zeros((rows, d), F32).at[:b].set(c)
    n = ada_w.shape[1]
    out = pl.pallas_call(
        _mod_kernel,
        grid=(n // d,),
        in_specs=[pl.BlockSpec((rows, d), lambda j: (0, 0)),
                  pl.BlockSpec((d, d), lambda j: (0, j)),
                  pl.BlockSpec((1, d), lambda j: (0, j))],
        out_specs=pl.BlockSpec((rows, d), lambda j: (0, j)),
        out_shape=jax.ShapeDtypeStruct((rows, n), F32),
        compiler_params=_cparams("parallel"),
        name="mod",
    )(c_pad, ada_w, ada_b.reshape(1, n))
    return out[:b]


def _head_rms(t, w):
    return t * lax.rsqrt(jnp.mean(t * t, axis=-1, keepdims=True) + RMS_EPS) * w


def _pos_digits(pos):
    lane = lax.broadcasted_iota(I32, pos.shape, 1)
    d0 = (lane == 0) | (lane == 3) | (lane == 6)
    d1 = (lane == 1) | (lane == 4) | (lane == 7)
    d2 = (lane == 2) | (lane == 5) | (lane == 8)
    dig = jnp.where(d0, pos >> 12, jnp.where(d1, (pos >> 6) & 63, jnp.where(d2, pos & 63, 0)))
    return dig.astype(F32)


def _inproj_kernel(x_ref, sc_ref, sh_ref, n1_ref, w_ref, qnw_ref, knw_ref, lbp_ref, qaug_ref,
                   q_ref, kcr_ref, vcr_ref, ks_ref, vst_ref, kw_ref, vwt_ref, gt_ref,
                   hq_ref, hk_ref, hlf_ref, hv_ref, hg_ref):
    x = x_ref[0]
    ms = jnp.mean(x * x, axis=-1, keepdims=True)
    h = x * lax.rsqrt(ms + RMS_EPS) * n1_ref[...] * (1.0 + sc_ref[0]) + sh_ref[0]
    p = _dot(h.astype(BF16), w_ref[...])
    tm = x.shape[0]

    qnw = qnw_ref[...]
    for hd in range(NSA_HEADS):
        t = p[:, hd * HEAD_DIM:(hd + 1) * HEAD_DIM]
        qn = _head_rms(t, qnw) * (HEAD_DIM ** -0.5 * LOG2E)
        qa = jnp.broadcast_to(qaug_ref[hd:hd + 1, :], (tm, HEAD_DIM))
        q_ref[0, hd] = jnp.concatenate([qn, qa], axis=1).astype(BF16)
    kaug = _pos_digits(pl.program_id(1) * tm + lax.broadcasted_iota(I32, (tm, HEAD_DIM), 0))

    o = NSA_WIDTH
    kcr_ref[0] = p[:, o:o + KV_WIDTH]
    vcr_ref[0] = p[:, o + KV_WIDTH:o + 2 * KV_WIDTH]
    ks = p[:, o + 2 * KV_WIDTH:o + 3 * KV_WIDTH]
    vs = p[:, o + 3 * KV_WIDTH:o + 4 * KV_WIDTH]
    kw = p[:, o + 4 * KV_WIDTH:o + 5 * KV_WIDTH]
    vw = p[:, o + 5 * KV_WIDTH:o + 6 * KV_WIDTH]
    for g in range(KV_HEADS):
        sl = slice(g * HEAD_DIM, (g + 1) * HEAD_DIM)
        ks_ref[0, g] = jnp.concatenate([_head_rms(ks[:, sl], knw_ref[1:2, :]), kaug], axis=1).astype(BF16)
        kw_ref[0, g] = jnp.concatenate([_head_rms(kw[:, sl], knw_ref[2:3, :]), kaug], axis=1).astype(BF16)
    vst = vs.T.astype(BF16)
    vwt = vw.T.astype(BF16)
    for g in range(KV_HEADS):
        vst_ref[0, g] = vst[g * HEAD_DIM:(g + 1) * HEAD_DIM, :]
        vwt_ref[0, g] = vwt[g * HEAD_DIM:(g + 1) * HEAD_DIM, :]

    o = NSA_WIDTH + 6 * KV_WIDTH
    gates = _sigmoid(p[:, o:o + GATE_PAD])
    gt_ref[0] = gates.T[:NSA_HEADS * 3, :]

    o = o + GATE_PAD
    hq = p[:, o:o + HGRN_WIDTH]
    hf = p[:, o + HGRN_WIDTH:o + 2 * HGRN_WIDTH]
    hi = p[:, o + 2 * HGRN_WIDTH:o + 3 * HGRN_WIDTH]
    hg = p[:, o + 3 * HGRN_WIDTH:o + 4 * HGRN_WIDTH]
    lbp = lbp_ref[...]
    e = jnp.exp(lbp - jnp.max(lbp, axis=0, keepdims=True))
    lb = e[0:1, :] / jnp.sum(e, axis=0, keepdims=True)
    f = lb + (1.0 - lb) * _sigmoid(hf)
    hq_ref[0] = hq * _sigmoid(hq) * (HGRN_DIM ** -0.5)
    hk_ref[0] = 1.0 - f
    hlf_ref[0] = jnp.log(f)
    hv_ref[0] = hi
    hg_ref[0] = _sigmoid(hg)


def _inproj(x, sc1, sh1, norm1_w, w_cat, q_norm_w, k_norm_w, lb_param, tm):
    b, s, d = x.shape
    row = lambda bi, i: (bi, i, 0)
    per_b = lambda bi, i: (bi, 0, 0)
    fixed2 = lambda bi, i: (0, 0)
    aw = 2 * HEAD_DIM
    rest = np.array([2.0 ** (-8.0 * (i + 1) / NSA_HEADS) for i in range(NSA_HEADS)], np.float64) * LOG2E
    qaug = np.zeros((NSA_HEADS, HEAD_DIM), np.float32)
    for i in range(3):
        term = rest.astype(np.float32).astype(BF16).astype(np.float64)
        rest = rest - term
        for dgt, wgt in enumerate((4096.0, 64.0, 1.0)):
            qaug[:, 3 * i + dgt] = term * wgt
    assert np.all(qaug == qaug.astype(BF16).astype(np.float32))
    out_shape = (
        jax.ShapeDtypeStruct((b, NSA_HEADS, s, aw), BF16),
        jax.ShapeDtypeStruct((b, s, KV_WIDTH), F32),
        jax.ShapeDtypeStruct((b, s, KV_WIDTH), F32),
        jax.ShapeDtypeStruct((b, KV_HEADS, s, aw), BF16),
        jax.ShapeDtypeStruct((b, KV_HEADS, HEAD_DIM, s), BF16),
        jax.ShapeDtypeStruct((b, KV_HEADS, s, aw), BF16),
        jax.ShapeDtypeStruct((b, KV_HEADS, HEAD_DIM, s), BF16),
        jax.ShapeDtypeStruct((b, NSA_HEADS * 3, s), F32),
    ) + tuple(jax.ShapeDtypeStruct((b, s, HGRN_WIDTH), F32) for _ in range(5))
    hm = lambda n, w: pl.BlockSpec((1, n, tm, w), lambda bi, i: (bi, 0, i, 0))
    hmt = lambda n, w: pl.BlockSpec((1, n, w, tm), lambda bi, i: (bi, 0, 0, i))
    out_specs = (
        hm(NSA_HEADS, aw),
        pl.BlockSpec((1, tm, KV_WIDTH), row),
        pl.BlockSpec((1, tm, KV_WIDTH), row),
        hm(KV_HEADS, aw), hmt(KV_HEADS, HEAD_DIM),
        hm(KV_HEADS, aw), hmt(KV_HEADS, HEAD_DIM),
        pl.BlockSpec((1, NSA_HEADS * 3, tm), lambda bi, i: (bi, 0, i)),
    ) + tuple(pl.BlockSpec((1, tm, HGRN_WIDTH), row) for _ in range(5))
    return pl.pallas_call(
        _inproj_kernel,
        grid=(b, s // tm),
        in_specs=[pl.BlockSpec((1, tm, d), row),
                  pl.BlockSpec((1, 1, d), per_b),
                  pl.BlockSpec((1, 1, d), per_b),
                  pl.BlockSpec((1, d), fixed2),
                  pl.BlockSpec((d, PROJ_COLS), fixed2),
                  pl.BlockSpec((1, HEAD_DIM), fixed2),
                  pl.BlockSpec((3, HEAD_DIM), fixed2),
                  pl.BlockSpec(lb_param.shape, fixed2),
                  pl.BlockSpec((NSA_HEADS, HEAD_DIM), fixed2)],
        out_specs=out_specs,
        out_shape=out_shape,
        compiler_params=_cparams("parallel", "parallel"),
        name="inproj",
    )(x, sc1, sh1, norm1_w, w_cat, q_norm_w, k_norm_w, lb_param, jnp.asarray(qaug))


def _gelu_tanh(x):
    return 0.5 * x * (1.0 + jnp.tanh(0.7978845608028654 * (x + 0.044715 * x * x * x)))


def _compress_kernel(kch_ref, vch_ref, pos_ref, wa_ref, wb_ref, b1_ref, w2_ref, knw_ref,
                     kc_ref, vct_ref):
    n = kch_ref.shape[1]
    outs = []
    for br, ch_ref in enumerate((kch_ref, vch_ref)):
        ch = ch_ref[0]
        a = _dot((ch + pos_ref[br, 0:1, :]).astype(BF16), wa_ref[br])
        bm = _dot((ch + pos_ref[br, 1:2, :]).astype(BF16), wb_ref[br])
        pre = a + pltpu.roll(bm, n - 1, 0) + b1_ref[br]
        hid = _gelu_tanh(pre).astype(BF16)
        outs.append([_dot(hid[:, g * CMP_HIDDEN:(g + 1) * CMP_HIDDEN], w2_ref[br]) for g in range(KV_HEADS)])
    end_digits = _pos_digits(lax.broadcasted_iota(I32, (n, HEAD_DIM), 0) * CMP_STRIDE + (CMP_BLOCK - 1))
    for g in range(KV_HEADS):
        kc_ref[0, g] = jnp.concatenate([_head_rms(outs[0][g], knw_ref[0:1, :]), end_digits], axis=1).astype(BF16)
    vct = jnp.concatenate(outs[1], axis=1).T.astype(BF16)
    for g in range(KV_HEADS):
        vct_ref[0, g] = vct[g * HEAD_DIM:(g + 1) * HEAD_DIM, :]


def _compress(kc_raw, vc_raw, cmp_pos, cmp_w1, cmp_b1, cmp_w2, k_norm_w):
    b, s, _ = kc_raw.shape
    n = s // CMP_STRIDE
    half = CMP_STRIDE
    cw = CMP_STRIDE * KV_WIDTH
    kch = kc_raw.reshape(b, n, cw)
    vch = vc_raw.reshape(b, n, cw)
    pos = cmp_pos.reshape(2, 2, half, 1, HEAD_DIM)
    pos = jnp.broadcast_to(pos, (2, 2, half, KV_HEADS, HEAD_DIM)).reshape(2, 2, cw)
    w1 = cmp_w1.reshape(2, 2, half, HEAD_DIM, CMP_HIDDEN)
    eye = jnp.eye(KV_HEADS, dtype=F32)
    wfull = jnp.einsum('rhjdn,gk->rhjgdkn', w1, eye).reshape(2, 2, cw, KV_HEADS * CMP_HIDDEN).astype(BF16)
    b1 = jnp.tile(cmp_b1.reshape(2, 1, CMP_HIDDEN), (1, 1, KV_HEADS))
    fix = lambda r: (lambda bi: (0,) * r)
    return pl.pallas_call(
        _compress_kernel,
        grid=(b,),
        in_specs=[pl.BlockSpec((1, n, cw), lambda bi: (bi, 0, 0)),
                  pl.BlockSpec((1, n, cw), lambda bi: (bi, 0, 0)),
                  pl.BlockSpec((2, 2, cw), fix(3)),
                  pl.BlockSpec((2, cw, KV_HEADS * CMP_HIDDEN), fix(3)),
                  pl.BlockSpec((2, cw, KV_HEADS * CMP_HIDDEN), fix(3)),
                  pl.BlockSpec((2, 1, KV_HEADS * CMP_HIDDEN), fix(3)),
                  pl.BlockSpec((2, CMP_HIDDEN, HEAD_DIM), fix(3)),
                  pl.BlockSpec((3, HEAD_DIM), fix(2))],
        out_specs=(pl.BlockSpec((1, KV_HEADS, n, 2 * HEAD_DIM), lambda bi: (bi, 0, 0, 0)),
                   pl.BlockSpec((1, KV_HEADS, HEAD_DIM, n), lambda bi: (bi, 0, 0, 0))),
        out_shape=(jax.ShapeDtypeStruct((b, KV_HEADS, n, 2 * HEAD_DIM), BF16),
                   jax.ShapeDtypeStruct((b, KV_HEADS, HEAD_DIM, n), BF16)),
        compiler_params=_cparams("parallel"),
        name="compress",
    )(kch, vch, pos, wfull[:, 0], wfull[:, 1], b1, cmp_w2.astype(BF16), k_norm_w)


def _nsa_kernel(q_ref, kc_ref, vct_ref, ks_ref, vst_ref, kw_ref, vwt_ref, gt_ref, cdiff_ref, wdiff_ref,
                ovl_ref, oh_ref, onw_ref, o_ref, buf_a, buf_b, m_scr, acc_scr, *, n_top):
    q0 = pl.program_id(2) * TQ
    ncols = HEADS_PER_KV * TQ
    q = q_ref[0].reshape(ncols, 2 * HEAD_DIM)
    ns = ovl_ref.shape[0]

    s = jnp.where(cdiff_ref[...] <= q0, _dot_nt(kc_ref[0, 0], q), -jnp.inf)
    m = jnp.max(s, axis=0, keepdims=True)
    m = jnp.where(m == -jnp.inf, 0.0, m)
    e = jnp.exp2(s - m)
    p = e / jnp.maximum(jnp.sum(e, axis=0, keepdims=True), 1e-30)
    o_c = _dot(vct_ref[0, 0], p.astype(BF16))

    psum = p[:, 0:TQ]
    for hh in range(1, HEADS_PER_KV):
        psum = psum + p[:, hh * TQ:(hh + 1) * TQ]
    imp = _split_dot(ovl_ref[...], psum)
    blk = lax.broadcasted_iota(I32, (ns, TQ), 0)
    tq = q0 + lax.broadcasted_iota(I32, (ns, TQ), 1)
    cur = tq >> 6
    forced = (blk == 0) | (blk == cur) | (blk == cur - 1)
    rank = jnp.where(forced, BIG, jnp.where(blk * SEL_BLOCK <= tq, imp, -BIG))

    blkf = blk.astype(F32)

    bias = jnp.full((ns, TQ), -1e30, F32)
    for _ in range(n_top):
        mx = jnp.max(rank, axis=0, keepdims=True)
        first = jnp.min(jnp.where(rank == mx, blkf, float(ns)), axis=0, keepdims=True)
        hit = blkf == first
        rank = jnp.where(hit, -jnp.inf, rank)
        bias = jnp.where(hit, 0.0, bias)

    if ns < 128:
        bias = jnp.concatenate([bias, jnp.zeros((128 - ns, TQ), F32)], axis=0)
    bias_t = bias.T.astype(BF16)
    qq = jnp.concatenate([q, jnp.concatenate([bias_t] * HEADS_PER_KV, axis=0)], axis=1)
    ones_rows = jnp.ones((16, TK), BF16)

    def scores(j):
        k0 = pl.multiple_of(j * TK, TK)
        kk = jnp.concatenate([ks_ref[0, 0, pl.ds(k0, TK), :], oh_ref[pl.ds(k0, TK), :]], axis=1)
        return _dot_nt(kk, qq)

    def consume(buf, j, causal, part):
        sc = buf[...]
        if causal:
            sc = jnp.where(wdiff_ref[0:TK, :] + (q0 - j * TK) >= 0, sc, -1e30)
        k0 = pl.multiple_of(j * TK, TK)
        m_run = m_scr[part]
        m_new = jnp.maximum(m_run, jnp.max(sc, axis=0, keepdims=True))
        ex = jnp.exp2(sc - m_new).astype(BF16)
        va = jnp.concatenate([vst_ref[0, 0, :, pl.ds(k0, TK)], ones_rows], axis=0)
        acc_scr[part] = jnp.exp2(m_run - m_new) * acc_scr[part] + _dot(va, ex)
        m_scr[part] = m_new

    n_past = q0 // TK
    m_scr[...] = jnp.full(m_scr.shape, -1e30, F32)
    acc_scr[...] = jnp.zeros(acc_scr.shape, F32)
    buf_a[...] = scores(0)

    nw = WINDOW + TQ
    start = pl.multiple_of(jnp.maximum(q0 - WINDOW, 0), TQ)
    dist = wdiff_ref[...] + (q0 - start)
    sw = jnp.where((dist >= 0) & (dist < WINDOW), _dot_nt(kw_ref[0, 0, pl.ds(start, nw), :], q), -jnp.inf)
    ew = jnp.exp2(sw - jnp.max(sw, axis=0, keepdims=True))
    o_w = _dot(vwt_ref[0, 0, :, pl.ds(start, nw)], ew.astype(BF16)) / jnp.sum(ew, axis=0, keepdims=True)

    def tiles(first, count):
        for u in range(0, count, 2):
            buf_b[...] = scores(first + u + 1)
            consume(buf_a, first + u, False, 0)
            buf_a[...] = scores(first + u + 2)
            consume(buf_b, first + u + 1, False, 1)
        return 0

    lax.fori_loop(0, n_past // 4, lambda i, _: tiles(4 * i, 4), 0)
    lax.fori_loop(0, (n_past // 2) % 2, lambda i, _: tiles((n_past // 4) * 4, 2), 0)

    @pl.when(n_past % 2 == 1)
    def _():
        buf_b[...] = scores(n_past)
        consume(buf_a, n_past - 1, False, 0)
        consume(buf_b, n_past, True, 1)

    @pl.when(n_past % 2 == 0)
    def _():
        consume(buf_a, n_past, True, 0)

    m_all = jnp.maximum(m_scr[0], m_scr[1])
    acc_s = jnp.exp2(m_scr[0] - m_all) * acc_scr[0] + jnp.exp2(m_scr[1] - m_all) * acc_scr[1]
    o_s = acc_s[0:HEAD_DIM, :] / acc_s[HEAD_DIM:HEAD_DIM + 1, :]

    gt = gt_ref[0, 0]
    outs = []
    for hh in range(HEADS_PER_KV):
        cs = slice(hh * TQ, (hh + 1) * TQ)
        o = (gt[3 * hh:3 * hh + 1, :] * o_c[:, cs] + gt[3 * hh + 1:3 * hh + 2, :] * o_s[:, cs]
             + gt[3 * hh + 2:3 * hh + 3, :] * o_w[:, cs])
        o = o * lax.rsqrt(jnp.mean(o * o, axis=0, keepdims=True) + RMS_EPS) * onw_ref[0, hh]
        outs.append(o)
    o_ref[0] = jnp.concatenate(outs, axis=0).T


def _nsa(q, kc, vct, ks, vst, kw, vwt, gates_t, attn_out_norm_w):
    b, _, s, aw = q.shape
    nc = kc.shape[2]
    ns = s // SEL_BLOCK
    n_top = min(N_SELECT, ns)
    ncols = HEADS_PER_KV * TQ
    nw = WINDOW + TQ
    tl = np.arange(ncols)[None, :] & (TQ - 1)
    cdiff = jnp.asarray((np.arange(nc)[:, None] * CMP_STRIDE + (CMP_BLOCK - 1) - tl).astype(np.int32))
    wdiff = jnp.asarray((tl - np.arange(nw)[:, None]).astype(np.int32))
    ci = np.arange(nc)[None, :] * CMP_STRIDE
    bj = np.arange(ns)[:, None]
    ovl = ((ci < (bj + 1) * SEL_BLOCK) & (ci + CMP_BLOCK > bj * SEL_BLOCK) & (np.arange(nc)[None, :] < nc - 1))
    ovl = jnp.asarray(ovl.astype(np.float32)).astype(BF16)
    assert ns <= 128
    onehot = (np.arange(s)[:, None] // SEL_BLOCK == np.arange(128)[None, :])
    onehot = jnp.asarray(onehot.astype(np.float32)).astype(BF16)
    onw = jnp.broadcast_to(attn_out_norm_w.reshape(KV_HEADS, HEADS_PER_KV, HEAD_DIM, 1),
                           (KV_HEADS, HEADS_PER_KV, HEAD_DIM, TQ))
    gt = gates_t.reshape(b, KV_HEADS, HEADS_PER_KV * 3, s)
    per_bg = lambda bi, g, i: (bi, g, 0, 0)
    fixed = lambda bi, g, i: (0, 0)
    return pl.pallas_call(
        functools.partial(_nsa_kernel, n_top=n_top),
        grid=(b, KV_HEADS, s // TQ),
        in_specs=[pl.BlockSpec((1, HEADS_PER_KV, TQ, aw), lambda bi, g, i: (bi, g, i, 0)),
                  pl.BlockSpec((1, 1, nc, aw), per_bg),
                  pl.BlockSpec((1, 1, HEAD_DIM, nc), per_bg),
                  pl.BlockSpec((1, 1, s, aw), per_bg),
                  pl.BlockSpec((1, 1, HEAD_DIM, s), per_bg),
                  pl.BlockSpec((1, 1, s, aw), per_bg),
                  pl.BlockSpec((1, 1, HEAD_DIM, s), per_bg),
                  pl.BlockSpec((1, 1, HEADS_PER_KV * 3, TQ), lambda bi, g, i: (bi, g, 0, i)),
                  pl.BlockSpec((nc, ncols), fixed),
                  pl.BlockSpec((nw, ncols), fixed),
                  pl.BlockSpec((ns, nc), fixed),
                  pl.BlockSpec((s, 128), fixed),
                  pl.BlockSpec((1, HEADS_PER_KV, HEAD_DIM, TQ), lambda bi, g, i: (g, 0, 0, 0))],
        out_specs=pl.BlockSpec((1, TQ, HEADS_PER_KV * HEAD_DIM), lambda bi, g, i: (bi, i, g)),
        out_shape=jax.ShapeDtypeStruct((b, s, NSA_WIDTH), F32),
        scratch_shapes=[pltpu.VMEM((TK, ncols), F32), pltpu.VMEM((TK, ncols), F32),
                        pltpu.VMEM((2, 1, ncols), F32), pltpu.VMEM((2, HEAD_DIM + 16, ncols), F32)],
        compiler_params=_cparams("parallel", "parallel", "arbitrary"),
        name="nsa",
    )(q, kc, vct, ks, vst, kw, vwt, gt, cdiff, wdiff, ovl, onehot, onw)


def _hgrn_kernel(q_ref, k_ref, lf_ref, v_ref, g_ref, onw_ref, o_ref, state_scr, *, n_chunks):
    c = HGRN_CHUNK

    @pl.when(pl.program_id(1) == 0)
    def _():
        state_scr[...] = jnp.zeros_like(state_scr)

    ri = lax.broadcasted_iota(I32, (c, c), 0)
    ci = lax.broadcasted_iota(I32, (c, c), 1)
    tril = (ri >= ci).astype(F32)
    rsub = ri // HGRN_SUB
    rin = ri & (HGRN_SUB - 1)

    def head_chunk(r0, hd, state_t):
        cols = slice(hd * HGRN_DIM, (hd + 1) * HGRN_DIM)
        q = q_ref[0, pl.ds(r0, c), cols]
        k = k_ref[0, pl.ds(r0, c), cols]
        lf = lf_ref[0, pl.ds(r0, c), cols]
        v = v_ref[0, pl.ds(r0, c), cols]
        cum = _dot(tril, lf, precision=HIGHEST)
        o = _dot_nt((q * jnp.exp(cum)).astype(BF16), state_t.astype(BF16))
        scores = jnp.zeros((c, c), F32)
        for i in range(1, c // HGRN_SUB):
            ref_row = cum[i * HGRN_SUB - 1:i * HGRN_SUB, :]
            qs = q * jnp.exp(jnp.minimum(cum - ref_row, 0.0))
            kd = k * jnp.exp(jnp.minimum(ref_row - cum, 0.0))
            blk = _dot_nt(qs.astype(BF16), kd.astype(BF16))
            scores = jnp.where((rsub == i) & (ci < i * HGRN_SUB), blk, scores)
        for d in range(HGRN_SUB):
            if d == 0:
                w = jnp.sum(q * k, axis=-1, keepdims=True)
            else:
                ksh = pltpu.roll(k, d, 0)
                csh = pltpu.roll(cum, d, 0)
                w = jnp.sum(q * ksh * jnp.exp(jnp.minimum(cum - csh, 0.0)), axis=-1, keepdims=True)
            scores = jnp.where((ri - ci == d) & (rin >= d), w, scores)
        o = o + _dot(scores.astype(BF16), v.astype(BF16))
        last = cum[c - 1:c, :]
        kd = (k * jnp.exp(last - cum)).astype(BF16)
        state_t = state_t * jnp.exp(last) + _dot(v.T.astype(BF16), kd)
        o = o * g_ref[0, pl.ds(r0, c), cols]
        o = o * lax.rsqrt(jnp.mean(o * o, axis=-1, keepdims=True) + RMS_EPS) * onw_ref[:, cols]
        o_ref[0, pl.ds(r0, c), cols] = o
        return state_t

    def chunk(ck, states):
        r0 = pl.multiple_of(ck * c, c)
        return tuple(head_chunk(r0, hd, states[hd]) for hd in range(HGRN_HEADS))

    states = lax.fori_loop(0, n_chunks, chunk, tuple(state_scr[hd] for hd in range(HGRN_HEADS)))
    for hd in range(HGRN_HEADS):
        state_scr[hd] = states[hd]


def _hgrn(hq, hk, hlf, hv, hg, rec_out_norm_w, rows):
    b, s, _ = hq.shape
    blk = pl.BlockSpec((1, rows, HGRN_WIDTH), lambda bi, i: (bi, i, 0))
    return pl.pallas_call(
        functools.partial(_hgrn_kernel, n_chunks=rows // HGRN_CHUNK),
        grid=(b, s // rows),
        in_specs=[blk, blk, blk, blk, blk,
                  pl.BlockSpec((1, HGRN_WIDTH), lambda bi, i: (0, 0))],
        out_specs=blk,
        out_shape=jax.ShapeDtypeStruct((b, s, HGRN_WIDTH), F32),
        scratch_shapes=[pltpu.VMEM((HGRN_HEADS, HGRN_DIM, HGRN_DIM), F32)],
        compiler_params=_cparams("parallel", "arbitrary"),
        name="hgrn",
    )(hq, hk, hlf, hv, hg, rec_out_norm_w.reshape(1, HGRN_WIDTH))


def _outproj_kernel(x_ref, a_ref, r_ref, wa_ref, wr_ref, gt_ref, sc_ref, sh_ref, n2_ref, x1_ref, h2_ref, h2p_ref):
    mixed = _dot(a_ref[0].astype(BF16), wa_ref[...]) + _dot(r_ref[0].astype(BF16), wr_ref[...])
    x1 = x_ref[0] + gt_ref[0] * mixed
    x1_ref[0] = x1
    ms = jnp.mean(x1 * x1, axis=-1, keepdims=True)
    h2 = x1 * lax.rsqrt(ms + RMS_EPS) * n2_ref[...] * (1.0 + sc_ref[0]) + sh_ref[0]
    h2_ref[0] = h2
    h2p_ref[0] = _pack_bf16_pair(h2[:, :D_MODEL // 2], h2[:, D_MODEL // 2:])


def _outproj(x, attn, rec, w_out, gt1, sc2, sh2, norm2_w, tm):
    b, s, d = x.shape
    row = lambda bi, i: (bi, i, 0)
    per_b = lambda bi, i: (bi, 0, 0)
    fixed2 = lambda bi, i: (0, 0)
    w = w_out.astype(BF16)
    return pl.pallas_call(
        _outproj_kernel,
        grid=(b, s // tm),
        in_specs=[pl.BlockSpec((1, tm, d), row),
                  pl.BlockSpec((1, tm, NSA_WIDTH), row),
                  pl.BlockSpec((1, tm, HGRN_WIDTH), row),
                  pl.BlockSpec((NSA_WIDTH, d), fixed2),
                  pl.BlockSpec((HGRN_WIDTH, d), fixed2),
                  pl.BlockSpec((1, 1, d), per_b),
                  pl.BlockSpec((1, 1, d), per_b),
                  pl.BlockSpec((1, 1, d), per_b),
                  pl.BlockSpec((1, d), fixed2)],
        out_specs=(pl.BlockSpec((1, tm, d), row), pl.BlockSpec((1, tm, d), row), pl.BlockSpec((1, tm, d // 2), row)),
        out_shape=(jax.ShapeDtypeStruct((b, s, d), F32), jax.ShapeDtypeStruct((b, s, d), F32),
                   jax.ShapeDtypeStruct((b, s, d // 2), jnp.uint32)),
        compiler_params=_cparams("parallel", "parallel"),
        name="outproj",
    )(x, attn, rec, w[:NSA_WIDTH], w[NSA_WIDTH:], gt1, sc2, sh2, norm2_w)


def _mixer(x, c, ada_w, ada_b, norm1_w, norm2_w, w_in, q_norm_w, k_norm_w, cmp_pos, cmp_w1, cmp_b1, cmp_w2,
           attn_out_norm_w, hgrn_lb_param, rec_out_norm_w, w_out):
    b, s, d = x.shape
    mod = _mod(c, ada_w, ada_b)
    sh1, sc1, gt1, sh2, sc2, gt2 = [m.reshape(b, 1, d) for m in jnp.split(mod, 6, axis=-1)]
    o = NSA_WIDTH + 6 * KV_WIDTH
    w_cat = jnp.concatenate([w_in[:, :o], w_in[:, o:o + NSA_HEADS * 3],
                             jnp.zeros((d, GATE_PAD - NSA_HEADS * 3), w_in.dtype),
                             w_in[:, o + NSA_HEADS * 3:]], axis=1).astype(BF16)
    tm = min(256, s)
    (q, kc_raw, vc_raw, ks, vst, kw, vwt, gates_t, hq, hk, hlf, hv, hg) = _inproj(
        x, sc1, sh1, norm1_w.reshape(1, d), w_cat, q_norm_w.reshape(1, HEAD_DIM), k_norm_w, hgrn_lb_param, tm)
    kc, vct = _compress(kc_raw, vc_raw, cmp_pos, cmp_w1, cmp_b1, cmp_w2, k_norm_w)
    attn = _nsa(q, kc, vct, ks, vst, kw, vwt, gates_t, attn_out_norm_w)
    rec = _hgrn(hq, hk, hlf, hv, hg, rec_out_norm_w, min(512, s))
    x1, h2, h2p = _outproj(x, attn, rec, w_out, gt1, sc2, sh2, norm2_w.reshape(1, d), tm)
    return x1, h2, h2p, gt2


def _router_kernel(h_ref, rwt_ref, bias_ref, tri_ref, ones_ref, idx_ref, w_ref, rank_ref, cnt_ref, carry_scr, *, tr):
    @pl.when(pl.program_id(0) == 0)
    def _():
        carry_scr[...] = jnp.zeros_like(carry_scr)

    h = h_ref[...]
    h_hi = h.astype(BF16)
    h_lo = (h - h_hi.astype(F32)).astype(BF16)
    logits = _dot_nt(rwt_ref[0], h_hi) + _dot_nt(rwt_ref[1], h_hi) + _dot_nt(rwt_ref[0], h_lo)
    scores = _sigmoid(logits)
    biased = scores + bias_ref[...]
    neg = -jnp.inf

    gs = []
    for g in range(N_GROUPS):
        sub = biased[g * GROUP_SIZE:(g + 1) * GROUP_SIZE, :]
        m1 = jnp.max(sub, axis=0, keepdims=True)
        dup = jnp.sum((sub == m1).astype(F32), axis=0, keepdims=True)
        m2 = jnp.max(jnp.where(sub < m1, sub, neg), axis=0, keepdims=True)
        gs.append(m1 + jnp.where(dup >= 2.0, m1, m2))
    parts = []
    for g in range(N_GROUPS):
        beaten = jnp.zeros_like(gs[g])
        for g2 in range(N_GROUPS):
            if g2 != g:
                beats = (gs[g2] >= gs[g]) if g2 < g else (gs[g2] > gs[g])
                beaten = beaten + beats.astype(F32)
        sub = biased[g * GROUP_SIZE:(g + 1) * GROUP_SIZE, :]
        parts.append(jnp.where(beaten < float(TOPK_GROUPS), sub, neg))
    cand = jnp.concatenate(parts, axis=0)

    rowf = lax.broadcasted_iota(I32, (N_EXPERTS, tr), 0).astype(F32)
    idx_rows, w_rows, hits = [], [], []
    multi = jnp.zeros((N_EXPERTS, tr), F32)
    for _ in range(TOP_K):
        mx = jnp.max(cand, axis=0, keepdims=True)
        first = jnp.min(jnp.where(cand == mx, rowf, float(N_EXPERTS)), axis=0, keepdims=True)
        hit = rowf == first
        idx_rows.append(first)
        w_rows.append(jnp.sum(jnp.where(hit, scores, 0.0), axis=0, keepdims=True))
        cand = jnp.where(hit, neg, cand)
        multi = jnp.where(hit, 1.0, multi)
    w = jnp.concatenate(w_rows, axis=0)
    w_ref[...] = w / jnp.sum(w, axis=0, keepdims=True) * ROUTED_SCALE
    idx = jnp.concatenate(idx_rows, axis=0)
    idx_ref[...] = idx.astype(I32)

    carry = carry_scr[...]
    mb = multi.astype(BF16)
    before = _dot(mb, tri_ref[...]) + jnp.concatenate([carry] * (tr // 128), axis=1)
    rank_rows = [jnp.sum(jnp.where(rowf == idx_rows[k], before, 0.0), axis=0, keepdims=True) for k in range(TOP_K)]
    rank_ref[...] = jnp.concatenate(rank_rows, axis=0).astype(I32)
    carry = carry + _dot(mb, ones_ref[...])
    carry_scr[...] = carry
    cnt_ref[...] = carry


def _router(h2, router_w, router_bias, tr):
    t, d = h2.shape
    tri = jnp.asarray(np.triu(np.ones((tr, tr), np.float32), 1)).astype(BF16)
    ones = jnp.ones((tr, 128), BF16)
    tok = pl.BlockSpec((TOP_K, tr), lambda i: (0, i))
    fixed = lambda i: (0, 0)
    rwt = router_w.T
    rwt_hi = rwt.astype(BF16)
    rwt_split = jnp.stack([rwt_hi, (rwt - rwt_hi.astype(F32)).astype(BF16)])
    return pl.pallas_call(
        functools.partial(_router_kernel, tr=tr),
        grid=(t // tr,),
        in_specs=[pl.BlockSpec((tr, d), lambda i: (i, 0)),
                  pl.BlockSpec((2, N_EXPERTS, d), lambda i: (0, 0, 0)),
                  pl.BlockSpec((N_EXPERTS, 1), fixed),
                  pl.BlockSpec((tr, tr), fixed),
                  pl.BlockSpec((tr, 128), fixed)],
        out_specs=(tok, tok, tok, pl.BlockSpec((N_EXPERTS, 128), fixed)),
        out_shape=(jax.ShapeDtypeStruct((TOP_K, t), I32), jax.ShapeDtypeStruct((TOP_K, t), F32),
                   jax.ShapeDtypeStruct((TOP_K, t), I32), jax.ShapeDtypeStruct((N_EXPERTS, 128), F32)),
        scratch_shapes=[pltpu.VMEM((N_EXPERTS, 128), F32)],
        compiler_params=_cparams("arbitrary"),
        name="router",
    )(h2, rwt_split, router_bias.reshape(N_EXPERTS, 1), tri, ones)


def _pack_bf16_pair(a, b):
    ua = lax.bitcast_convert_type(a.astype(BF16).astype(F32), jnp.uint32)
    ub = lax.bitcast_convert_type(b.astype(BF16).astype(F32), jnp.uint32)
    return ua | (ub >> 16)


def _unpack_bf16_pair(w):
    a = lax.bitcast_convert_type(w & jnp.uint32(0xFFFF0000), F32)
    b = lax.bitcast_convert_type(w << 16, F32)
    return a, b


def _slot_kernel(ps_ref, idx_ref, rank_ref, slot_ref):
    idx = idx_ref[...]

    def body(e, acc):
        return jnp.where(idx == e, ps_ref[e], acc)

    slot_ref[...] = lax.fori_loop(0, N_EXPERTS, body, jnp.zeros_like(idx)) + rank_ref[...]


def _slots(pad_start, idx, rank, tt):
    t = idx.shape[1]
    tok = pl.BlockSpec((TOP_K, tt), lambda i, ps: (0, i))
    return pl.pallas_call(
        _slot_kernel,
        grid_spec=pltpu.PrefetchScalarGridSpec(num_scalar_prefetch=1, grid=(t // tt,),
                                               in_specs=[tok, tok], out_specs=tok),
        out_shape=jax.ShapeDtypeStruct((TOP_K, t), I32),
        compiler_params=_cparams("parallel"),
        name="slots",
    )(pad_start, idx, rank)


SC_CORES = 2
SC_SUBCORES = 16
SC_CHUNK = 64


def _sc_mesh():
    return plsc.VectorSubcoreMesh(core_axis_name="c", subcore_axis_name="s")


def _sc_dispatch(h2p, slot_chunks, n_rows):
    t, dw = h2p.shape
    per = slot_chunks.shape[0] // (SC_CORES * SC_SUBCORES)

    def body(h_hbm, slot_hbm, xs_hbm, idx_v, rows_v, sem):
        wid = lax.axis_index("s") * SC_CORES + lax.axis_index("c")

        @pl.loop(0, per)
        def _(c):
            ch = wid * per + c
            pltpu.sync_copy(slot_hbm.at[ch], idx_v)
            pltpu.sync_copy(h_hbm.at[pl.ds(ch * SC_CHUNK, SC_CHUNK)], rows_v)
            copies = [pltpu.async_copy(rows_v, xs_hbm.at[idx_v.at[k]], sem) for k in range(TOP_K)]
            for cp in copies:
                cp.wait()

    return pl.kernel(
        body, out_type=jax.ShapeDtypeStruct((n_rows, dw), h2p.dtype), mesh=_sc_mesh(),
        scratch_types=[pltpu.VMEM((TOP_K, SC_CHUNK), I32), pltpu.VMEM((SC_CHUNK, dw), h2p.dtype),
                       pltpu.SemaphoreType.DMA],
    )(h2p, slot_chunks)


def _sc_gather(ys, slot_chunks, t):
    dw = ys.shape[1]
    per = slot_chunks.shape[0] // (SC_CORES * SC_SUBCORES)

    def body(ys_hbm, slot_hbm, yg_hbm, idx_v, rows_v, gsem, wsem):
        wid = lax.axis_index("s") * SC_CORES + lax.axis_index("c")

        @pl.loop(0, per)
        def _(c):
            ch = wid * per + c
            pltpu.sync_copy(slot_hbm.at[ch], idx_v)
            gathers = [None] * TOP_K
            writes = [None] * TOP_K
            gathers[0] = pltpu.async_copy(ys_hbm.at[idx_v.at[0]], rows_v.at[0], gsem)
            for k in range(TOP_K):
                gathers[k].wait()
                if k + 1 < TOP_K:
                    if k >= 1:
                        writes[k - 1].wait()
                    gathers[k + 1] = pltpu.async_copy(ys_hbm.at[idx_v.at[k + 1]], rows_v.at[(k + 1) % 2], gsem)
                writes[k] = pltpu.async_copy(rows_v.at[k % 2], yg_hbm.at[k, pl.ds(ch * SC_CHUNK, SC_CHUNK)], wsem)
            writes[TOP_K - 2].wait()
            writes[TOP_K - 1].wait()

    return pl.kernel(
        body, out_type=jax.ShapeDtypeStruct((TOP_K, t, dw), ys.dtype), mesh=_sc_mesh(),
        scratch_types=[pltpu.VMEM((TOP_K, SC_CHUNK), I32), pltpu.VMEM((2, SC_CHUNK, dw), ys.dtype),
                       pltpu.SemaphoreType.DMA, pltpu.SemaphoreType.DMA],
    )(ys, slot_chunks)


def _experts_kernel(be_ref, nu_ref, bv_ref, xs_ref, wg_ref, wu_ref, wd_ref, ys_ref):
    i = pl.program_id(0)
    half = D_MODEL // 2

    @pl.when(i < nu_ref[0])
    def _():
        live = lax.broadcasted_iota(I32, xs_ref.shape, 0) < bv_ref[i]
        xa, xb = _unpack_bf16_pair(jnp.where(live, xs_ref[...], jnp.uint32(0)))
        xa, xb = xa.astype(BF16), xb.astype(BF16)
        g = _dot(xa, wg_ref[0, :half].astype(BF16)) + _dot(xb, wg_ref[0, half:].astype(BF16))
        u = _dot(xa, wu_ref[0, :half].astype(BF16)) + _dot(xb, wu_ref[0, half:].astype(BF16))
        act = (g * _sigmoid(g) * u).astype(BF16)
        y = _dot(act, wd_ref[0].astype(BF16))
        ys_ref[...] = _pack_bf16_pair(y[:, :half], y[:, half:])

    @pl.when(i >= nu_ref[0])
    def _():
        ys_ref[...] = jnp.zeros_like(ys_ref)


def _experts(xs, blk_e, n_used, blk_valid, w_gate, w_up, w_down):
    n_rows, dw = xs.shape
    d = w_gate.shape[1]
    nblk = n_rows // EXPERT_BLOCK
    row_map = lambda i, be, nu, bv: (jnp.minimum(i, nu[0] - 1), 0)
    w_map = lambda i, be, nu, bv: (be[i], 0, 0)
    return pl.pallas_call(
        _experts_kernel,
        grid_spec=pltpu.PrefetchScalarGridSpec(
            num_scalar_prefetch=3,
            grid=(nblk,),
            in_specs=[pl.BlockSpec((EXPERT_BLOCK, dw), row_map),
                      pl.BlockSpec((1, d, EXPERT_FF), w_map),
                      pl.BlockSpec((1, d, EXPERT_FF), w_map),
                      pl.BlockSpec((1, EXPERT_FF, d), w_map)],
            out_specs=pl.BlockSpec((EXPERT_BLOCK, dw), lambda i, be, nu, bv: (i, 0))),
        out_shape=jax.ShapeDtypeStruct((n_rows, dw), xs.dtype),
        compiler_params=_cparams("arbitrary"),
        name="experts",
    )(blk_e, n_used, blk_valid, xs, w_gate, w_up, w_down)


def _combine_kernel(x1_ref, h_ref, w_ref, gt_ref, sg_ref, su_ref, sd_ref, yg_ref, o_ref):
    tc = x1_ref.shape[0]
    half = D_MODEL // 2
    hb = h_ref[...].astype(BF16)
    g = _dot(hb, sg_ref[...])
    u = _dot(hb, su_ref[...])
    ffn = _dot((g * _sigmoid(g) * u).astype(BF16), sd_ref[...])

    w = w_ref[...]
    ra = jnp.zeros((tc, half), F32)
    rb = jnp.zeros((tc, half), F32)
    for k in range(TOP_K):
        ya, yb = _unpack_bf16_pair(yg_ref[k])
        ra = ra + w[:, k:k + 1] * ya
        rb = rb + w[:, k:k + 1] * yb
    ffn = ffn + jnp.concatenate([ra, rb], axis=1)
    o_ref[...] = x1_ref[...] + gt_ref[0] * ffn


def _combine(x1, h2, w_tok, gt2, yg, sg, su, sd, seq, tc):
    t, d = x1.shape
    row = lambda i: (i, 0)
    fixed = lambda i: (0, 0)
    return pl.pallas_call(
        _combine_kernel,
        grid=(t // tc,),
        in_specs=[pl.BlockSpec((tc, d), row),
                  pl.BlockSpec((tc, d), row),
                  pl.BlockSpec((tc, TOP_K), row),
                  pl.BlockSpec((1, 1, d), lambda i: ((i * tc) // seq, 0, 0)),
                  pl.BlockSpec((d, SHARED_FF), fixed),
                  pl.BlockSpec((d, SHARED_FF), fixed),
                  pl.BlockSpec((SHARED_FF, d), fixed),
                  pl.BlockSpec((TOP_K, tc, d // 2), lambda i: (0, i, 0))],
        out_specs=pl.BlockSpec((tc, d), row),
        out_shape=jax.ShapeDtypeStruct((t, d), F32),
        compiler_params=_cparams("parallel"),
        name="combine",
    )(x1, h2, w_tok, gt2, sg.astype(BF16), su.astype(BF16), sd.astype(BF16), yg)


def _moe_parts(x1, h2, h2p, gt2, router_w, router_bias, w_gate, w_up, w_down, sg, su, sd):
    b, s, d = x1.shape
    t = b * s
    h2 = h2.reshape(t, d)
    idx, w, rank, cnt = _router(h2, router_w, router_bias, min(256, t))
    counts = cnt[:, 0].astype(I32)
    padded = (counts + EXPERT_BLOCK - 1) // EXPERT_BLOCK * EXPERT_BLOCK
    pad_end = jnp.cumsum(padded)
    pad_start = pad_end - padded
    n_rows = t * TOP_K + N_EXPERTS * EXPERT_BLOCK
    nblk = n_rows // EXPERT_BLOCK
    n_used = (pad_end[-1:] // EXPERT_BLOCK).astype(I32)
    blk_start = jnp.arange(nblk, dtype=I32) * EXPERT_BLOCK
    owns = (pad_start[None, :] <= blk_start[:, None]) & (blk_start[:, None] < pad_end[None, :])
    e_ids = jnp.arange(N_EXPERTS, dtype=I32)[None, :]
    last_e = jnp.max(jnp.where(counts > 0, e_ids[0], 0))
    blk_e = jnp.where(blk_start < pad_end[-1], jnp.sum(jnp.where(owns, e_ids, 0), axis=1), last_e).astype(I32)
    rows_left = jnp.sum(jnp.where(owns, (pad_start + counts)[None, :] - blk_start[:, None], 0), axis=1)
    blk_valid = jnp.clip(rows_left, 0, EXPERT_BLOCK).astype(I32)
    slot = _slots(pad_start.astype(I32), idx, rank, min(2048, t))
    slot_chunks = slot.reshape(TOP_K, t // SC_CHUNK, SC_CHUNK).transpose(1, 0, 2)
    xs = _sc_dispatch(h2p.reshape(t, d // 2), slot_chunks, n_rows)
    ys = _experts(xs, blk_e, n_used, blk_valid, w_gate, w_up, w_down)
    yg = _sc_gather(ys, slot_chunks, t)
    out = _combine(x1.reshape(t, d), h2, w.T, gt2, yg, sg, su, sd, s, min(256, t))
    return out.reshape(b, s, d), dict(idx=idx, w=w, rank=rank, cnt=cnt)


def kernel(x, c, ada_w, ada_b, norm1_w, norm2_w, w_in, q_norm_w, k_norm_w, cmp_pos, cmp_w1, cmp_b1, cmp_w2, attn_out_norm_w, hgrn_lb_param, rec_out_norm_w, w_out, router_w, router_bias, exp_w_gate, exp_w_up, exp_w_down, shared_w_gate, shared_w_up, shared_w_down):
    assert ada_w.shape[0] == 1, "one layer"
    assert x.shape[0] <= 8 and x.shape[1] % TK == 0 and x.shape[1] >= WINDOW + TQ
    l = 0
    x1, h2, h2p, gt2 = _mixer(x, c, ada_w[l], ada_b[l], norm1_w[l], norm2_w[l], w_in[l], q_norm_w[l], k_norm_w[l],
                         cmp_pos[l], cmp_w1[l], cmp_b1[l], cmp_w2[l], attn_out_norm_w[l], hgrn_lb_param,
                         rec_out_norm_w[l], w_out[l])
    out, _ = _moe_parts(x1, h2, h2p, gt2, router_w[l], router_bias[l], exp_w_gate[l], exp_w_up[l], exp_w_down[l],
                        shared_w_gate[l], shared_w_up[l], shared_w_down[l])
    return out
```

```python
import functools

import numpy as np
import jax
import jax.numpy as jnp
from jax import lax
from jax.experimental import pallas as pl
from jax.experimental.pallas import tpu as pltpu
from jax.experimental.pallas import tpu_sc as plsc

F32 = jnp.float32
BF16 = jnp.bfloat16
I32 = jnp.int32

D_MODEL = 1024
NSA_HEADS = 8
HEAD_DIM = 64
NSA_WIDTH = NSA_HEADS * HEAD_DIM
KV_HEADS = 2
HEADS_PER_KV = NSA_HEADS // KV_HEADS
KV_WIDTH = KV_HEADS * HEAD_DIM
CMP_BLOCK = 32
CMP_STRIDE = 16
CMP_HIDDEN = 256
SEL_BLOCK = 64
N_SELECT = 16
WINDOW = 512
HGRN_HEADS = 4
HGRN_DIM = 128
HGRN_WIDTH = HGRN_HEADS * HGRN_DIM
HGRN_CHUNK = 64
HGRN_SUB = 16
N_EXPERTS = 256
TOP_K = 8
N_GROUPS = 8
GROUP_SIZE = N_EXPERTS // N_GROUPS
TOPK_GROUPS = 4
EXPERT_FF = 256
SHARED_FF = 256
ROUTED_SCALE = 2.5
RMS_EPS = 1e-6
BIG = 1e9
LOG2E = 1.4426950408889634
GATE_PAD = 128
PROJ_COLS = NSA_WIDTH + 6 * KV_WIDTH + GATE_PAD + 4 * HGRN_WIDTH

VMEM_LIMIT = 56 * 1024 * 1024

TQ = 256
TK = 512
EXPERT_BLOCK = 512
HIGHEST = lax.Precision.HIGHEST


def _cparams(*sem):
    return pltpu.CompilerParams(dimension_semantics=sem, vmem_limit_bytes=VMEM_LIMIT)


def _sigmoid(x):
    return 1.0 / (1.0 + jnp.exp(-x))


def _dot_nt(a, b):
    return lax.dot_general(a, b, (((1,), (1,)), ((), ())), preferred_element_type=F32)


def _dot(a, b, **kw):
    return jnp.dot(a, b, preferred_element_type=F32, **kw)


def _split_dot(a_bf16_exact, x):
    hi = x.astype(BF16)
    lo = (x - hi.astype(F32)).astype(BF16)
    return _dot(a_bf16_exact, hi) + _dot(a_bf16_exact, lo)


def _mod_kernel(c_ref, w_ref, b_ref, o_ref):
    c = c_ref[...]
    cond = c * _sigmoid(c)
    o_ref[...] = _dot(cond, w_ref[...], precision=HIGHEST) + b_ref[...]


def _mod(c, ada_w, ada_b):
    b, d = c.shape
    rows = 8
    c_pad = jnp.zeros((rows, d), F32).at[:b].set(c)
    n = ada_w.shape[1]
    out = pl.pallas_call(
        _mod_kernel,
        grid=(n // d,),
        in_specs=[pl.BlockSpec((rows, d), lambda j: (0, 0)),
                  pl.BlockSpec((d, d), lambda j: (0, j)),
                  pl.BlockSpec((1, d), lambda j: (0, j))],
        out_specs=pl.BlockSpec((rows, d), lambda j: (0, j)),
        out_shape=jax.ShapeDtypeStruct((rows, n), F32),
        compiler_params=_cparams("parallel"),
        name="mod",
    )(c_pad, ada_w, ada_b.reshape(1, n))
    return out[:b]


def _head_rms(t, w):
    return t * lax.rsqrt(jnp.mean(t * t, axis=-1, keepdims=True) + RMS_EPS) * w


def _pos_digits(pos):
    lane = lax.broadcasted_iota(I32, pos.shape, 1)
    d0 = (lane == 0) | (lane == 3) | (lane == 6)
    d1 = (lane == 1) | (lane == 4) | (lane == 7)
    d2 = (lane == 2) | (lane == 5) | (lane == 8)
    dig = jnp.where(d0, pos >> 12, jnp.where(d1, (pos >> 6) & 63, jnp.where(d2, pos & 63, 0)))
    return dig.astype(F32)


def _inproj_kernel(x_ref, sc_ref, sh_ref, n1_ref, w_ref, qnw_ref, knw_ref, lbp_ref, qaug_ref,
                   q_ref, kcr_ref, vcr_ref, ks_ref, vst_ref, kw_ref, vwt_ref, gt_ref,
                   hq_ref, hk_ref, hlf_ref, hv_ref, hg_ref):
    x = x_ref[0]
    ms = jnp.mean(x * x, axis=-1, keepdims=True)
    h = x * lax.rsqrt(ms + RMS_EPS) * n1_ref[...] * (1.0 + sc_ref[0]) + sh_ref[0]
    p = _dot(h.astype(BF16), w_ref[...])
    tm = x.shape[0]

    qnw = qnw_ref[...]
    for hd in range(NSA_HEADS):
        t = p[:, hd * HEAD_DIM:(hd + 1) * HEAD_DIM]
        qn = _head_rms(t, qnw) * (HEAD_DIM ** -0.5 * LOG2E)
        qa = jnp.broadcast_to(qaug_ref[hd:hd + 1, :], (tm, HEAD_DIM))
        q_ref[0, hd] = jnp.concatenate([qn, qa], axis=1).astype(BF16)
    kaug = _pos_digits(pl.program_id(1) * tm + lax.broadcasted_iota(I32, (tm, HEAD_DIM), 0))

    o = NSA_WIDTH
    kcr_ref[0] = p[:, o:o + KV_WIDTH]
    vcr_ref[0] = p[:, o + KV_WIDTH:o + 2 * KV_WIDTH]
    ks = p[:, o + 2 * KV_WIDTH:o + 3 * KV_WIDTH]
    vs = p[:, o + 3 * KV_WIDTH:o + 4 * KV_WIDTH]
    kw = p[:, o + 4 * KV_WIDTH:o + 5 * KV_WIDTH]
    vw = p[:, o + 5 * KV_WIDTH:o + 6 * KV_WIDTH]
    for g in range(KV_HEADS):
        sl = slice(g * HEAD_DIM, (g + 1) * HEAD_DIM)
        ks_ref[0, g] = jnp.concatenate([_head_rms(ks[:, sl], knw_ref[1:2, :]), kaug], axis=1).astype(BF16)
        kw_ref[0, g] = jnp.concatenate([_head_rms(kw[:, sl], knw_ref[2:3, :]), kaug], axis=1).astype(BF16)
    vst = vs.T.astype(BF16)
    vwt = vw.T.astype(BF16)
    for g in range(KV_HEADS):
        vst_ref[0, g] = vst[g * HEAD_DIM:(g + 1) * HEAD_DIM, :]
        vwt_ref[0, g] = vwt[g * HEAD_DIM:(g + 1) * HEAD_DIM, :]

    o = NSA_WIDTH + 6 * KV_WIDTH
    gates = _sigmoid(p[:, o:o + GATE_PAD])
    gt_ref[0] = gates.T[:NSA_HEADS * 3, :]

    o = o + GATE_PAD
    hq = p[:, o:o + HGRN_WIDTH]
    hf = p[:, o + HGRN_WIDTH:o + 2 * HGRN_WIDTH]
    hi = p[:, o + 2 * HGRN_WIDTH:o + 3 * HGRN_WIDTH]
    hg = p[:, o + 3 * HGRN_WIDTH:o + 4 * HGRN_WIDTH]
    lbp = lbp_ref[...]
    e = jnp.exp(lbp - jnp.max(lbp, axis=0, keepdims=True))
    lb = e[0:1, :] / jnp.sum(e, axis=0, keepdims=True)
    f = lb + (1.0 - lb) * _sigmoid(hf)
    hq_ref[0] = hq * _sigmoid(hq) * (HGRN_DIM ** -0.5)
    hk_ref[0] = 1.0 - f
    hlf_ref[0] = jnp.log(f)
    hv_ref[0] = hi
    hg_ref[0] = _sigmoid(hg)


def _inproj(x, sc1, sh1, norm1_w, w_cat, q_norm_w, k_norm_w, lb_param, tm):
    b, s, d = x.shape
    row = lambda bi, i: (bi, i, 0)
    per_b = lambda bi, i: (bi, 0, 0)
    fixed2 = lambda bi, i: (0, 0)
    aw = 2 * HEAD_DIM
    rest = np.array([2.0 ** (-8.0 * (i + 1) / NSA_HEADS) for i in range(NSA_HEADS)], np.float64) * LOG2E
    qaug = np.zeros((NSA_HEADS, HEAD_DIM), np.float32)
    for i in range(3):
        term = rest.astype(np.float32).astype(BF16).astype(np.float64)
        rest = rest - term
        for dgt, wgt in enumerate((4096.0, 64.0, 1.0)):
            qaug[:, 3 * i + dgt] = term * wgt
    assert np.all(qaug == qaug.astype(BF16).astype(np.float32))
    out_shape = (
        jax.ShapeDtypeStruct((b, NSA_HEADS, s, aw), BF16),
        jax.ShapeDtypeStruct((b, s, KV_WIDTH), F32),
        jax.ShapeDtypeStruct((b, s, KV_WIDTH), F32),
        jax.ShapeDtypeStruct((b, KV_HEADS, s, aw), BF16),
        jax.ShapeDtypeStruct((b, KV_HEADS, HEAD_DIM, s), BF16),
        jax.ShapeDtypeStruct((b, KV_HEADS, s, aw), BF16),
        jax.ShapeDtypeStruct((b, KV_HEADS, HEAD_DIM, s), BF16),
        jax.ShapeDtypeStruct((b, NSA_HEADS * 3, s), F32),
    ) + tuple(jax.ShapeDtypeStruct((b, s, HGRN_WIDTH), F32) for _ in range(5))
    hm = lambda n, w: pl.BlockSpec((1, n, tm, w), lambda bi, i: (bi, 0, i, 0))
    hmt = lambda n, w: pl.BlockSpec((1, n, w, tm), lambda bi, i: (bi, 0, 0, i))
    out_specs = (
        hm(NSA_HEADS, aw),
        pl.BlockSpec((1, tm, KV_WIDTH), row),
        pl.BlockSpec((1, tm, KV_WIDTH), row),
        hm(KV_HEADS, aw), hmt(KV_HEADS, HEAD_DIM),
        hm(KV_HEADS, aw), hmt(KV_HEADS, HEAD_DIM),
        pl.BlockSpec((1, NSA_HEADS * 3, tm), lambda bi, i: (bi, 0, i)),
    ) + tuple(pl.BlockSpec((1, tm, HGRN_WIDTH), row) for _ in range(5))
    return pl.pallas_call(
        _inproj_kernel,
        grid=(b, s // tm),
        in_specs=[pl.BlockSpec((1, tm, d), row),
                  pl.BlockSpec((1, 1, d), per_b),
                  pl.BlockSpec((1, 1, d), per_b),
                  pl.BlockSpec((1, d), fixed2),
                  pl.BlockSpec((d, PROJ_COLS), fixed2),
                  pl.BlockSpec((1, HEAD_DIM), fixed2),
                  pl.BlockSpec((3, HEAD_DIM), fixed2),
                  pl.BlockSpec(lb_param.shape, fixed2),
                  pl.BlockSpec((NSA_HEADS, HEAD_DIM), fixed2)],
        out_specs=out_specs,
        out_shape=out_shape,
        compiler_params=_cparams("parallel", "parallel"),
        name="inproj",
    )(x, sc1, sh1, norm1_w, w_cat, q_norm_w, k_norm_w, lb_param, jnp.asarray(qaug))


def _gelu_tanh(x):
    return 0.5 * x * (1.0 + jnp.tanh(0.7978845608028654 * (x + 0.044715 * x * x * x)))


def _compress_kernel(kch_ref, vch_ref, pos_ref, wa_ref, wb_ref, b1_ref, w2_ref, knw_ref,
                     kc_ref, vct_ref):
    n = kch_ref.shape[1]
    outs = []
    for br, ch_ref in enumerate((kch_ref, vch_ref)):
        ch = ch_ref[0]
        a = _dot((ch + pos_ref[br, 0:1, :]).astype(BF16), wa_ref[br])
        bm = _dot((ch + pos_ref[br, 1:2, :]).astype(BF16), wb_ref[br])
        pre = a + pltpu.roll(bm, n - 1, 0) + b1_ref[br]
        hid = _gelu_tanh(pre).astype(BF16)
        outs.append([_dot(hid[:, g * CMP_HIDDEN:(g + 1) * CMP_HIDDEN], w2_ref[br]) for g in range(KV_HEADS)])
    end_digits = _pos_digits(lax.broadcasted_iota(I32, (n, HEAD_DIM), 0) * CMP_STRIDE + (CMP_BLOCK - 1))
    for g in range(KV_HEADS):
        kc_ref[0, g] = jnp.concatenate([_head_rms(outs[0][g], knw_ref[0:1, :]), end_digits], axis=1).astype(BF16)
    vct = jnp.concatenate(outs[1], axis=1).T.astype(BF16)
    for g in range(KV_HEADS):
        vct_ref[0, g] = vct[g * HEAD_DIM:(g + 1) * HEAD_DIM, :]


def _compress(kc_raw, vc_raw, cmp_pos, cmp_w1, cmp_b1, cmp_w2, k_norm_w):
    b, s, _ = kc_raw.shape
    n = s // CMP_STRIDE
    half = CMP_STRIDE
    cw = CMP_STRIDE * KV_WIDTH
    kch = kc_raw.reshape(b, n, cw)
    vch = vc_raw.reshape(b, n, cw)
    pos = cmp_pos.reshape(2, 2, half, 1, HEAD_DIM)
    pos = jnp.broadcast_to(pos, (2, 2, half, KV_HEADS, HEAD_DIM)).reshape(2, 2, cw)
    w1 = cmp_w1.reshape(2, 2, half, HEAD_DIM, CMP_HIDDEN)
    eye = jnp.eye(KV_HEADS, dtype=F32)
    wfull = jnp.einsum('rhjdn,gk->rhjgdkn', w1, eye).reshape(2, 2, cw, KV_HEADS * CMP_HIDDEN).astype(BF16)
    b1 = jnp.tile(cmp_b1.reshape(2, 1, CMP_HIDDEN), (1, 1, KV_HEADS))
    fix = lambda r: (lambda bi: (0,) * r)
    return pl.pallas_call(
        _compress_kernel,
        grid=(b,),
        in_specs=[pl.BlockSpec((1, n, cw), lambda bi: (bi, 0, 0)),
                  pl.BlockSpec((1, n, cw), lambda bi: (bi, 0, 0)),
                  pl.BlockSpec((2, 2, cw), fix(3)),
                  pl.BlockSpec((2, cw, KV_HEADS * CMP_HIDDEN), fix(3)),
                  pl.BlockSpec((2, cw, KV_HEADS * CMP_HIDDEN), fix(3)),
                  pl.BlockSpec((2, 1, KV_HEADS * CMP_HIDDEN), fix(3)),
                  pl.BlockSpec((2, CMP_HIDDEN, HEAD_DIM), fix(3)),
                  pl.BlockSpec((3, HEAD_DIM), fix(2))],
        out_specs=(pl.BlockSpec((1, KV_HEADS, n, 2 * HEAD_DIM), lambda bi: (bi, 0, 0, 0)),
                   pl.BlockSpec((1, KV_HEADS, HEAD_DIM, n), lambda bi: (bi, 0, 0, 0))),
        out_shape=(jax.ShapeDtypeStruct((b, KV_HEADS, n, 2 * HEAD_DIM), BF16),
                   jax.ShapeDtypeStruct((b, KV_HEADS, HEAD_DIM, n), BF16)),
        compiler_params=_cparams("parallel"),
        name="compress",
    )(kch, vch, pos, wfull[:, 0], wfull[:, 1], b1, cmp_w2.astype(BF16), k_norm_w)


def _nsa_kernel(q_ref, kc_ref, vct_ref, ks_ref, vst_ref, kw_ref, vwt_ref, gt_ref, cdiff_ref, wdiff_ref,
                ovl_ref, oh_ref, onw_ref, o_ref, buf_a, buf_b, m_scr, acc_scr, *, n_top):
    q0 = pl.program_id(2) * TQ
    ncols = HEADS_PER_KV * TQ
    q = q_ref[0].reshape(ncols, 2 * HEAD_DIM)
    ns = ovl_ref.shape[0]

    s = jnp.where(cdiff_ref[...] <= q0, _dot_nt(kc_ref[0, 0], q), -jnp.inf)
    m = jnp.max(s, axis=0, keepdims=True)
    m = jnp.where(m == -jnp.inf, 0.0, m)
    e = jnp.exp2(s - m)
    p = e / jnp.maximum(jnp.sum(e, axis=0, keepdims=True), 1e-30)
    o_c = _dot(vct_ref[0, 0], p.astype(BF16))

    psum = p[:, 0:TQ]
    for hh in range(1, HEADS_PER_KV):
        psum = psum + p[:, hh * TQ:(hh + 1) * TQ]
    imp = _split_dot(ovl_ref[...], psum)
    blk = lax.broadcasted_iota(I32, (ns, TQ), 0)
    tq = q0 + lax.broadcasted_iota(I32, (ns, TQ), 1)
    cur = tq >> 6
    forced = (blk == 0) | (blk == cur) | (blk == cur - 1)
    rank = jnp.where(forced, BIG, jnp.where(blk * SEL_BLOCK <= tq, imp, -BIG))

    blkf = blk.astype(F32)

    bias = jnp.full((ns, TQ), -1e30, F32)
    for _ in range(n_top):
        mx = jnp.max(rank, axis=0, keepdims=True)
        first = jnp.min(jnp.where(rank == mx, blkf, float(ns)), axis=0, keepdims=True)
        hit = blkf == first
        rank = jnp.where(hit, -jnp.inf, rank)
        bias = jnp.where(hit, 0.0, bias)

    if ns < 128:
        bias = jnp.concatenate([bias, jnp.zeros((128 - ns, TQ), F32)], axis=0)
    bias_t = bias.T.astype(BF16)
    qq = jnp.concatenate([q, jnp.concatenate([bias_t] * HEADS_PER_KV, axis=0)], axis=1)
    ones_rows = jnp.ones((16, TK), BF16)

    def scores(j):
        k0 = pl.multiple_of(j * TK, TK)
        kk = jnp.concatenate([ks_ref[0, 0, pl.ds(k0, TK), :], oh_ref[pl.ds(k0, TK), :]], axis=1)
        return _dot_nt(kk, qq)

    def consume(buf, j, causal, part):
        sc = buf[...]
        if causal:
            sc = jnp.where(wdiff_ref[0:TK, :] + (q0 - j * TK) >= 0, sc, -1e30)
        k0 = pl.multiple_of(j * TK, TK)
        m_run = m_scr[part]
        m_new = jnp.maximum(m_run, jnp.max(sc, axis=0, keepdims=True))
        ex = jnp.exp2(sc - m_new).astype(BF16)
        va = jnp.concatenate([vst_ref[0, 0, :, pl.ds(k0, TK)], ones_rows], axis=0)
        acc_scr[part] = jnp.exp2(m_run - m_new) * acc_scr[part] + _dot(va, ex)
        m_scr[part] = m_new

    n_past = q0 // TK
    m_scr[...] = jnp.full(m_scr.shape, -1e30, F32)
    acc_scr[...] = jnp.zeros(acc_scr.shape, F32)
    buf_a[...] = scores(0)

    nw = WINDOW + TQ
    start = pl.multiple_of(jnp.maximum(q0 - WINDOW, 0), TQ)
    dist = wdiff_ref[...] + (q0 - start)
    sw = jnp.where((dist >= 0) & (dist < WINDOW), _dot_nt(kw_ref[0, 0, pl.ds(start, nw), :], q), -jnp.inf)
    ew = jnp.exp2(sw - jnp.max(sw, axis=0, keepdims=True))
    o_w = _dot(vwt_ref[0, 0, :, pl.ds(start, nw)], ew.astype(BF16)) / jnp.sum(ew, axis=0, keepdims=True)

    def tiles(first, count):
        for u in range(0, count, 2):
            buf_b[...] = scores(first + u + 1)
            consume(buf_a, first + u, False, 0)
            buf_a[...] = scores(first + u + 2)
            consume(buf_b, first + u + 1, False, 1)
        return 0

    lax.fori_loop(0, n_past // 4, lambda i, _: tiles(4 * i, 4), 0)
    lax.fori_loop(0, (n_past // 2) % 2, lambda i, _: tiles((n_past // 4) * 4, 2), 0)

    @pl.when(n_past % 2 == 1)
    def _():
        buf_b[...] = scores(n_past)
        consume(buf_a, n_past - 1, False, 0)
        consume(buf_b, n_past, True, 1)

    @pl.when(n_past % 2 == 0)
    def _():
        consume(buf_a, n_past, True, 0)

    m_all = jnp.maximum(m_scr[0], m_scr[1])
    acc_s = jnp.exp2(m_scr[0] - m_all) * acc_scr[0] + jnp.exp2(m_scr[1] - m_all) * acc_scr[1]
    o_s = acc_s[0:HEAD_DIM, :] / acc_s[HEAD_DIM:HEAD_DIM + 1, :]

    gt = gt_ref[0, 0]
    outs = []
    for hh in range(HEADS_PER_KV):
        cs = slice(hh * TQ, (hh + 1) * TQ)
        o = (gt[3 * hh:3 * hh + 1, :] * o_c[:, cs] + gt[3 * hh + 1:3 * hh + 2, :] * o_s[:, cs]
             + gt[3 * hh + 2:3 * hh + 3, :] * o_w[:, cs])
        o = o * lax.rsqrt(jnp.mean(o * o, axis=0, keepdims=True) + RMS_EPS) * onw_ref[0, hh]
        outs.append(o)
    o_ref[0] = jnp.concatenate(outs, axis=0).T


def _nsa(q, kc, vct, ks, vst, kw, vwt, gates_t, attn_out_norm_w):
    b, _, s, aw = q.shape
    nc = kc.shape[2]
    ns = s // SEL_BLOCK
    n_top = min(N_SELECT, ns)
    ncols = HEADS_PER_KV * TQ
    nw = WINDOW + TQ
    tl = np.arange(ncols)[None, :] & (TQ - 1)
    cdiff = jnp.asarray((np.arange(nc)[:, None] * CMP_STRIDE + (CMP_BLOCK - 1) - tl).astype(np.int32))
    wdiff = jnp.asarray((tl - np.arange(nw)[:, None]).astype(np.int32))
    ci = np.arange(nc)[None, :] * CMP_STRIDE
    bj = np.arange(ns)[:, None]
    ovl = ((ci < (bj + 1) * SEL_BLOCK) & (ci + CMP_BLOCK > bj * SEL_BLOCK) & (np.arange(nc)[None, :] < nc - 1))
    ovl = jnp.asarray(ovl.astype(np.float32)).astype(BF16)
    assert ns <= 128
    onehot = (np.arange(s)[:, None] // SEL_BLOCK == np.arange(128)[None, :])
    onehot = jnp.asarray(onehot.astype(np.float32)).astype(BF16)
    onw = jnp.broadcast_to(attn_out_norm_w.reshape(KV_HEADS, HEADS_PER_KV, HEAD_DIM, 1),
                           (KV_HEADS, HEADS_PER_KV, HEAD_DIM, TQ))
    gt = gates_t.reshape(b, KV_HEADS, HEADS_PER_KV * 3, s)
    per_bg = lambda bi, g, i: (bi, g, 0, 0)
    fixed = lambda bi, g, i: (0, 0)
    return pl.pallas_call(
        functools.partial(_nsa_kernel, n_top=n_top),
        grid=(b, KV_HEADS, s // TQ),
        in_specs=[pl.BlockSpec((1, HEADS_PER_KV, TQ, aw), lambda bi, g, i: (bi, g, i, 0)),
                  pl.BlockSpec((1, 1, nc, aw), per_bg),
                  pl.BlockSpec((1, 1, HEAD_DIM, nc), per_bg),
                  pl.BlockSpec((1, 1, s, aw), per_bg),
                  pl.BlockSpec((1, 1, HEAD_DIM, s), per_bg),
                  pl.BlockSpec((1, 1, s, aw), per_bg),
                  pl.BlockSpec((1, 1, HEAD_DIM, s), per_bg),
                  pl.BlockSpec((1, 1, HEADS_PER_KV * 3, TQ), lambda bi, g, i: (bi, g, 0, i)),
                  pl.BlockSpec((nc, ncols), fixed),
                  pl.BlockSpec((nw, ncols), fixed),
                  pl.BlockSpec((ns, nc), fixed),
                  pl.BlockSpec((s, 128), fixed),
                  pl.BlockSpec((1, HEADS_PER_KV, HEAD_DIM, TQ), lambda bi, g, i: (g, 0, 0, 0))],
        out_specs=pl.BlockSpec((1, TQ, HEADS_PER_KV * HEAD_DIM), lambda bi, g, i: (bi, i, g)),
        out_shape=jax.ShapeDtypeStruct((b, s, NSA_WIDTH), F32),
        scratch_shapes=[pltpu.VMEM((TK, ncols), F32), pltpu.VMEM((TK, ncols), F32),
                        pltpu.VMEM((2, 1, ncols), F32), pltpu.VMEM((2, HEAD_DIM + 16, ncols), F32)],
        compiler_params=_cparams("parallel", "parallel", "arbitrary"),
        name="nsa",
    )(q, kc, vct, ks, vst, kw, vwt, gt, cdiff, wdiff, ovl, onehot, onw)


def _hgrn_kernel(q_ref, k_ref, lf_ref, v_ref, g_ref, onw_ref, o_ref, state_scr, *, n_chunks):
    c = HGRN_CHUNK

    @pl.when(pl.program_id(1) == 0)
    def _():
        state_scr[...] = jnp.zeros_like(state_scr)

    ri = lax.broadcasted_iota(I32, (c, c), 0)
    ci = lax.broadcasted_iota(I32, (c, c), 1)
    tril = (ri >= ci).astype(F32)
    rsub = ri // HGRN_SUB
    rin = ri & (HGRN_SUB - 1)

    def head_chunk(r0, hd, state_t):
        cols = slice(hd * HGRN_DIM, (hd + 1) * HGRN_DIM)
        q = q_ref[0, pl.ds(r0, c), cols]
        k = k_ref[0, pl.ds(r0, c), cols]
        lf = lf_ref[0, pl.ds(r0, c), cols]
        v = v_ref[0, pl.ds(r0, c), cols]
        cum = _dot(tril, lf, precision=HIGHEST)
        o = _dot_nt((q * jnp.exp(cum)).astype(BF16), state_t.astype(BF16))
        scores = jnp.zeros((c, c), F32)
        for i in range(1, c // HGRN_SUB):
            ref_row = cum[i * HGRN_SUB - 1:i * HGRN_SUB, :]
            qs = q * jnp.exp(jnp.minimum(cum - ref_row, 0.0))
            kd = k * jnp.exp(jnp.minimum(ref_row - cum, 0.0))
            blk = _dot_nt(qs.astype(BF16), kd.astype(BF16))
            scores = jnp.where((rsub == i) & (ci < i * HGRN_SUB), blk, scores)
        for d in range(HGRN_SUB):
            if d == 0:
                w = jnp.sum(q * k, axis=-1, keepdims=True)
            else:
                ksh = pltpu.roll(k, d, 0)
                csh = pltpu.roll(cum, d, 0)
                w = jnp.sum(q * ksh * jnp.exp(jnp.minimum(cum - csh, 0.0)), axis=-1, keepdims=True)
            scores = jnp.where((ri - ci == d) & (rin >= d), w, scores)
        o = o + _dot(scores.astype(BF16), v.astype(BF16))
        last = cum[c - 1:c, :]
        kd = (k * jnp.exp(last - cum)).astype(BF16)
        state_t = state_t * jnp.exp(last) + _dot(v.T.astype(BF16), kd)
        o = o * g_ref[0, pl.ds(r0, c), cols]
        o = o * lax.rsqrt(jnp.mean(o * o, axis=-1, keepdims=True) + RMS_EPS) * onw_ref[:, cols]
        o_ref[0, pl.ds(r0, c), cols] = o
        return state_t

    def chunk(ck, states):
        r0 = pl.multiple_of(ck * c, c)
        return tuple(head_chunk(r0, hd, states[hd]) for hd in range(HGRN_HEADS))

    states = lax.fori_loop(0, n_chunks, chunk, tuple(state_scr[hd] for hd in range(HGRN_HEADS)))
    for hd in range(HGRN_HEADS):
        state_scr[hd] = states[hd]


def _hgrn(hq, hk, hlf, hv, hg, rec_out_norm_w, rows):
    b, s, _ = hq.shape
    blk = pl.BlockSpec((1, rows, HGRN_WIDTH), lambda bi, i: (bi, i, 0))
    return pl.pallas_call(
        functools.partial(_hgrn_kernel, n_chunks=rows // HGRN_CHUNK),
        grid=(b, s // rows),
        in_specs=[blk, blk, blk, blk, blk,
                  pl.BlockSpec((1, HGRN_WIDTH), lambda bi, i: (0, 0))],
        out_specs=blk,
        out_shape=jax.ShapeDtypeStruct((b, s, HGRN_WIDTH), F32),
        scratch_shapes=[pltpu.VMEM((HGRN_HEADS, HGRN_DIM, HGRN_DIM), F32)],
        compiler_params=_cparams("parallel", "arbitrary"),
        name="hgrn",
    )(hq, hk, hlf, hv, hg, rec_out_norm_w.reshape(1, HGRN_WIDTH))


def _outproj_kernel(x_ref, a_ref, r_ref, wa_ref, wr_ref, gt_ref, sc_ref, sh_ref, n2_ref, x1_ref, h2_ref, h2p_ref):
    mixed = _dot(a_ref[0].astype(BF16), wa_ref[...]) + _dot(r_ref[0].astype(BF16), wr_ref[...])
    x1 = x_ref[0] + gt_ref[0] * mixed
    x1_ref[0] = x1
    ms = jnp.mean(x1 * x1, axis=-1, keepdims=True)
    h2 = x1 * lax.rsqrt(ms + RMS_EPS) * n2_ref[...] * (1.0 + sc_ref[0]) + sh_ref[0]
    h2_ref[0] = h2
    h2p_ref[0] = _pack_bf16_pair(h2[:, :D_MODEL // 2], h2[:, D_MODEL // 2:])


def _outproj(x, attn, rec, w_out, gt1, sc2, sh2, norm2_w, tm):
    b, s, d = x.shape
    row = lambda bi, i: (bi, i, 0)
    per_b = lambda bi, i: (bi, 0, 0)
    fixed2 = lambda bi, i: (0, 0)
    w = w_out.astype(BF16)
    return pl.pallas_call(
        _outproj_kernel,
        grid=(b, s // tm),
        in_specs=[pl.BlockSpec((1, tm, d), row),
                  pl.BlockSpec((1, tm, NSA_WIDTH), row),
                  pl.BlockSpec((1, tm, HGRN_WIDTH), row),
                  pl.BlockSpec((NSA_WIDTH, d), fixed2),
                  pl.BlockSpec((HGRN_WIDTH, d), fixed2),
                  pl.BlockSpec((1, 1, d), per_b),
                  pl.BlockSpec((1, 1, d), per_b),
                  pl.BlockSpec((1, 1, d), per_b),
                  pl.BlockSpec((1, d), fixed2)],
        out_specs=(pl.BlockSpec((1, tm, d), row), pl.BlockSpec((1, tm, d), row), pl.BlockSpec((1, tm, d // 2), row)),
        out_shape=(jax.ShapeDtypeStruct((b, s, d), F32), jax.ShapeDtypeStruct((b, s, d), F32),
                   jax.ShapeDtypeStruct((b, s, d // 2), jnp.uint32)),
        compiler_params=_cparams("parallel", "parallel"),
        name="outproj",
    )(x, attn, rec, w[:NSA_WIDTH], w[NSA_WIDTH:], gt1, sc2, sh2, norm2_w)


def _mixer(x, c, ada_w, ada_b, norm1_w, norm2_w, w_in, q_norm_w, k_norm_w, cmp_pos, cmp_w1, cmp_b1, cmp_w2,
           attn_out_norm_w, hgrn_lb_param, rec_out_norm_w, w_out):
    b, s, d = x.shape
    mod = _mod(c, ada_w, ada_b)
    sh1, sc1, gt1, sh2, sc2, gt2 = [m.reshape(b, 1, d) for m in jnp.split(mod, 6, axis=-1)]
    o = NSA_WIDTH + 6 * KV_WIDTH
    w_cat = jnp.concatenate([w_in[:, :o], w_in[:, o:o + NSA_HEADS * 3],
                             jnp.zeros((d, GATE_PAD - NSA_HEADS * 3), w_in.dtype),
                             w_in[:, o + NSA_HEADS * 3:]], axis=1).astype(BF16)
    tm = min(256, s)
    (q, kc_raw, vc_raw, ks, vst, kw, vwt, gates_t, hq, hk, hlf, hv, hg) = _inproj(
        x, sc1, sh1, norm1_w.reshape(1, d), w_cat, q_norm_w.reshape(1, HEAD_DIM), k_norm_w, hgrn_lb_param, tm)
    kc, vct = _compress(kc_raw, vc_raw, cmp_pos, cmp_w1, cmp_b1, cmp_w2, k_norm_w)
    attn = _nsa(q, kc, vct, ks, vst, kw, vwt, gates_t, attn_out_norm_w)
    rec = _hgrn(hq, hk, hlf, hv, hg, rec_out_norm_w, min(512, s))
    x1, h2, h2p = _outproj(x, attn, rec, w_out, gt1, sc2, sh2, norm2_w.reshape(1, d), tm)
    return x1, h2, h2p, gt2


def _router_kernel(h_ref, rwt_ref, bias_ref, tri_ref, ones_ref, idx_ref, w_ref, rank_ref, cnt_ref, carry_scr, *, tr):
    @pl.when(pl.program_id(0) == 0)
    def _():
        carry_scr[...] = jnp.zeros_like(carry_scr)

    h = h_ref[...]
    h_hi = h.astype(BF16)
    h_lo = (h - h_hi.astype(F32)).astype(BF16)
    logits = _dot_nt(rwt_ref[0], h_hi) + _dot_nt(rwt_ref[1], h_hi) + _dot_nt(rwt_ref[0], h_lo)
    scores = _sigmoid(logits)
    biased = scores + bias_ref[...]
    neg = -jnp.inf

    gs = []
    for g in range(N_GROUPS):
        sub = biased[g * GROUP_SIZE:(g + 1) * GROUP_SIZE, :]
        m1 = jnp.max(sub, axis=0, keepdims=True)
        dup = jnp.sum((sub == m1).astype(F32), axis=0, keepdims=True)
        m2 = jnp.max(jnp.where(sub < m1, sub, neg), axis=0, keepdims=True)
        gs.append(m1 + jnp.where(dup >= 2.0, m1, m2))
    parts = []
    for g in range(N_GROUPS):
        beaten = jnp.zeros_like(gs[g])
        for g2 in range(N_GROUPS):
            if g2 != g:
                beats = (gs[g2] >= gs[g]) if g2 < g else (gs[g2] > gs[g])
                beaten = beaten + beats.astype(F32)
        sub = biased[g * GROUP_SIZE:(g + 1) * GROUP_SIZE, :]
        parts.append(jnp.where(beaten < float(TOPK_GROUPS), sub, neg))
    cand = jnp.concatenate(parts, axis=0)

    rowf = lax.broadcasted_iota(I32, (N_EXPERTS, tr), 0).astype(F32)
    idx_rows, w_rows, hits = [], [], []
    multi = jnp.zeros((N_EXPERTS, tr), F32)
    for _ in range(TOP_K):
        mx = jnp.max(cand, axis=0, keepdims=True)
        first = jnp.min(jnp.where(cand == mx, rowf, float(N_EXPERTS)), axis=0, keepdims=True)
        hit = rowf == first
        idx_rows.append(first)
        w_rows.append(jnp.sum(jnp.where(hit, scores, 0.0), axis=0, keepdims=True))
        cand = jnp.where(hit, neg, cand)
        multi = jnp.where(hit, 1.0, multi)
    w = jnp.concatenate(w_rows, axis=0)
    w_ref[...] = w / jnp.sum(w, axis=0, keepdims=True) * ROUTED_SCALE
    idx = jnp.concatenate(idx_rows, axis=0)
    idx_ref[...] = idx.astype(I32)

    carry = carry_scr[...]
    mb = multi.astype(BF16)
    before = _dot(mb, tri_ref[...]) + jnp.concatenate([carry] * (tr // 128), axis=1)
    rank_rows = [jnp.sum(jnp.where(rowf == idx_rows[k], before, 0.0), axis=0, keepdims=True) for k in range(TOP_K)]
    rank_ref[...] = jnp.concatenate(rank_rows, axis=0).astype(I32)
    carry = carry + _dot(mb, ones_ref[...])
    carry_scr[...] = carry
    cnt_ref[...] = carry


def _router(h2, router_w, router_bias, tr):
    t, d = h2.shape
    tri = jnp.asarray(np.triu(np.ones((tr, tr), np.float32), 1)).astype(BF16)
    ones = jnp.ones((tr, 128), BF16)
    tok = pl.BlockSpec((TOP_K, tr), lambda i: (0, i))
    fixed = lambda i: (0, 0)
    rwt = router_w.T
    rwt_hi = rwt.astype(BF16)
    rwt_split = jnp.stack([rwt_hi, (rwt - rwt_hi.astype(F32)).astype(BF16)])
    return pl.pallas_call(
        functools.partial(_router_kernel, tr=tr),
        grid=(t // tr,),
        in_specs=[pl.BlockSpec((tr, d), lambda i: (i, 0)),
                  pl.BlockSpec((2, N_EXPERTS, d), lambda i: (0, 0, 0)),
                  pl.BlockSpec((N_EXPERTS, 1), fixed),
                  pl.BlockSpec((tr, tr), fixed),
                  pl.BlockSpec((tr, 128), fixed)],
        out_specs=(tok, tok, tok, pl.BlockSpec((N_EXPERTS, 128), fixed)),
        out_shape=(jax.ShapeDtypeStruct((TOP_K, t), I32), jax.ShapeDtypeStruct((TOP_K, t), F32),
                   jax.ShapeDtypeStruct((TOP_K, t), I32), jax.ShapeDtypeStruct((N_EXPERTS, 128), F32)),
        scratch_shapes=[pltpu.VMEM((N_EXPERTS, 128), F32)],
        compiler_params=_cparams("arbitrary"),
        name="router",
    )(h2, rwt_split, router_bias.reshape(N_EXPERTS, 1), tri, ones)


def _pack_bf16_pair(a, b):
    ua = lax.bitcast_convert_type(a.astype(BF16).astype(F32), jnp.uint32)
    ub = lax.bitcast_convert_type(b.astype(BF16).astype(F32), jnp.uint32)
    return ua | (ub >> 16)


def _unpack_bf16_pair(w):
    a = lax.bitcast_convert_type(w & jnp.uint32(0xFFFF0000), F32)
    b = lax.bitcast_convert_type(w << 16, F32)
    return a, b


def _slot_kernel(ps_ref, idx_ref, rank_ref, slot_ref):
    idx = idx_ref[...]

    def body(e, acc):
        return jnp.where(idx == e, ps_ref[e], acc)

    slot_ref[...] = lax.fori_loop(0, N_EXPERTS, body, jnp.zeros_like(idx)) + rank_ref[...]


def _slots(pad_start, idx, rank, tt):
    t = idx.shape[1]
    tok = pl.BlockSpec((TOP_K, tt), lambda i, ps: (0, i))
    return pl.pallas_call(
        _slot_kernel,
        grid_spec=pltpu.PrefetchScalarGridSpec(num_scalar_prefetch=1, grid=(t // tt,),
                                               in_specs=[tok, tok], out_specs=tok),
        out_shape=jax.ShapeDtypeStruct((TOP_K, t), I32),
        compiler_params=_cparams("parallel"),
        name="slots",
    )(pad_start, idx, rank)


SC_CORES = 2
SC_SUBCORES = 16
SC_CHUNK = 64


def _sc_mesh():
    return plsc.VectorSubcoreMesh(core_axis_name="c", subcore_axis_name="s")


def _sc_dispatch(h2p, slot_chunks, n_rows):
    t, dw = h2p.shape
    per = slot_chunks.shape[0] // (SC_CORES * SC_SUBCORES)

    def body(h_hbm, slot_hbm, xs_hbm, idx_v, rows_v, sem):
        wid = lax.axis_index("s") * SC_CORES + lax.axis_index("c")

        @pl.loop(0, per)
        def _(c):
            ch = wid * per + c
            pltpu.sync_copy(slot_hbm.at[ch], idx_v)
            pltpu.sync_copy(h_hbm.at[pl.ds(ch * SC_CHUNK, SC_CHUNK)], rows_v)
            copies = [pltpu.async_copy(rows_v, xs_hbm.at[idx_v.at[k]], sem) for k in range(TOP_K)]
            for cp in copies:
                cp.wait()

    return pl.kernel(
        body, out_type=jax.ShapeDtypeStruct((n_rows, dw), h2p.dtype), mesh=_sc_mesh(),
        scratch_types=[pltpu.VMEM((TOP_K, SC_CHUNK), I32), pltpu.VMEM((SC_CHUNK, dw), h2p.dtype),
                       pltpu.SemaphoreType.DMA],
    )(h2p, slot_chunks)


def _sc_gather(ys, slot_chunks, t):
    dw = ys.shape[1]
    per = slot_chunks.shape[0] // (SC_CORES * SC_SUBCORES)

    def body(ys_hbm, slot_hbm, yg_hbm, idx_v, rows_v, gsem, wsem):
        wid = lax.axis_index("s") * SC_CORES + lax.axis_index("c")

        @pl.loop(0, per)
        def _(c):
            ch = wid * per + c
            pltpu.sync_copy(slot_hbm.at[ch], idx_v)
            gathers = [None] * TOP_K
            writes = [None] * TOP_K
            gathers[0] = pltpu.async_copy(ys_hbm.at[idx_v.at[0]], rows_v.at[0], gsem)
            for k in range(TOP_K):
                gathers[k].wait()
                if k + 1 < TOP_K:
                    if k >= 1:
                        writes[k - 1].wait()
                    gathers[k + 1] = pltpu.async_copy(ys_hbm.at[idx_v.at[k + 1]], rows_v.at[(k + 1) % 2], gsem)
                writes[k] = pltpu.async_copy(rows_v.at[k % 2], yg_hbm.at[k, pl.ds(ch * SC_CHUNK, SC_CHUNK)], wsem)
            writes[TOP_K - 2].wait()
            writes[TOP_K - 1].wait()

    return pl.kernel(
        body, out_type=jax.ShapeDtypeStruct((TOP_K, t, dw), ys.dtype), mesh=_sc_mesh(),
        scratch_types=[pltpu.VMEM((TOP_K, SC_CHUNK), I32), pltpu.VMEM((2, SC_CHUNK, dw), ys.dtype),
                       pltpu.SemaphoreType.DMA, pltpu.SemaphoreType.DMA],
    )(ys, slot_chunks)


def _experts_kernel(be_ref, nu_ref, bv_ref, xs_ref, wg_ref, wu_ref, wd_ref, ys_ref):
    i = pl.program_id(0)
    half = D_MODEL // 2

    @pl.when(i < nu_ref[0])
    def _():
        live = lax.broadcasted_iota(I32, xs_ref.shape, 0) < bv_ref[i]
        xa, xb = _unpack_bf16_pair(jnp.where(live, xs_ref[...], jnp.uint32(0)))
        xa, xb = xa.astype(BF16), xb.astype(BF16)
        g = _dot(xa, wg_ref[0, :half].astype(BF16)) + _dot(xb, wg_ref[0, half:].astype(BF16))
        u = _dot(xa, wu_ref[0, :half].astype(BF16)) + _dot(xb, wu_ref[0, half:].astype(BF16))
        act = (g * _sigmoid(g) * u).astype(BF16)
        y = _dot(act, wd_ref[0].astype(BF16))
        ys_ref[...] = _pack_bf16_pair(y[:, :half], y[:, half:])

    @pl.when(i >= nu_ref[0])
    def _():
        ys_ref[...] = jnp.zeros_like(ys_ref)


def _experts(xs, blk_e, n_used, blk_valid, w_gate, w_up, w_down):
    n_rows, dw = xs.shape
    d = w_gate.shape[1]
    nblk = n_rows // EXPERT_BLOCK
    row_map = lambda i, be, nu, bv: (jnp.minimum(i, nu[0] - 1), 0)
    w_map = lambda i, be, nu, bv: (be[i], 0, 0)
    return pl.pallas_call(
        _experts_kernel,
        grid_spec=pltpu.PrefetchScalarGridSpec(
            num_scalar_prefetch=3,
            grid=(nblk,),
            in_specs=[pl.BlockSpec((EXPERT_BLOCK, dw), row_map),
                      pl.BlockSpec((1, d, EXPERT_FF), w_map),
                      pl.BlockSpec((1, d, EXPERT_FF), w_map),
                      pl.BlockSpec((1, EXPERT_FF, d), w_map)],
            out_specs=pl.BlockSpec((EXPERT_BLOCK, dw), lambda i, be, nu, bv: (i, 0))),
        out_shape=jax.ShapeDtypeStruct((n_rows, dw), xs.dtype),
        compiler_params=_cparams("arbitrary"),
        name="experts",
    )(blk_e, n_used, blk_valid, xs, w_gate, w_up, w_down)


def _combine_kernel(x1_ref, h_ref, w_ref, gt_ref, sg_ref, su_ref, sd_ref, yg_ref, o_ref):
    tc = x1_ref.shape[0]
    half = D_MODEL // 2
    hb = h_ref[...].astype(BF16)
    g = _dot(hb, sg_ref[...])
    u = _dot(hb, su_ref[...])
    ffn = _dot((g * _sigmoid(g) * u).astype(BF16), sd_ref[...])

    w = w_ref[...]
    ra = jnp.zeros((tc, half), F32)
    rb = jnp.zeros((tc, half), F32)
    for k in range(TOP_K):
        ya, yb = _unpack_bf16_pair(yg_ref[k])
        ra = ra + w[:, k:k + 1] * ya
        rb = rb + w[:, k:k + 1] * yb
    ffn = ffn + jnp.concatenate([ra, rb], axis=1)
    o_ref[...] = x1_ref[...] + gt_ref[0] * ffn


def _combine(x1, h2, w_tok, gt2, yg, sg, su, sd, seq, tc):
    t, d = x1.shape
    row = lambda i: (i, 0)
    fixed = lambda i: (0, 0)
    return pl.pallas_call(
        _combine_kernel,
        grid=(t // tc,),
        in_specs=[pl.BlockSpec((tc, d), row),
                  pl.BlockSpec((tc, d), row),
                  pl.BlockSpec((tc, TOP_K), row),
                  pl.BlockSpec((1, 1, d), lambda i: ((i * tc) // seq, 0, 0)),
                  pl.BlockSpec((d, SHARED_FF), fixed),
                  pl.BlockSpec((d, SHARED_FF), fixed),
                  pl.BlockSpec((SHARED_FF, d), fixed),
                  pl.BlockSpec((TOP_K, tc, d // 2), lambda i: (0, i, 0))],
        out_specs=pl.BlockSpec((tc, d), row),
        out_shape=jax.ShapeDtypeStruct((t, d), F32),
        compiler_params=_cparams("parallel"),
        name="combine",
    )(x1, h2, w_tok, gt2, sg.astype(BF16), su.astype(BF16), sd.astype(BF16), yg)


def _moe_parts(x1, h2, h2p, gt2, router_w, router_bias, w_gate, w_up, w_down, sg, su, sd):
    b, s, d = x1.shape
    t = b * s
    h2 = h2.reshape(t, d)
    idx, w, rank, cnt = _router(h2, router_w, router_bias, min(256, t))
    counts = cnt[:, 0].astype(I32)
    padded = (counts + EXPERT_BLOCK - 1) // EXPERT_BLOCK * EXPERT_BLOCK
    pad_end = jnp.cumsum(padded)
    pad_start = pad_end - padded
    n_rows = t * TOP_K + N_EXPERTS * EXPERT_BLOCK
    nblk = n_rows // EXPERT_BLOCK
    n_used = (pad_end[-1:] // EXPERT_BLOCK).astype(I32)
    blk_start = jnp.arange(nblk, dtype=I32) * EXPERT_BLOCK
    owns = (pad_start[None, :] <= blk_start[:, None]) & (blk_start[:, None] < pad_end[None, :])
    e_ids = jnp.arange(N_EXPERTS, dtype=I32)[None, :]
    last_e = jnp.max(jnp.where(counts > 0, e_ids[0], 0))
    blk_e = jnp.where(blk_start < pad_end[-1], jnp.sum(jnp.where(owns, e_ids, 0), axis=1), last_e).astype(I32)
    rows_left = jnp.sum(jnp.where(owns, (pad_start + counts)[None, :] - blk_start[:, None], 0), axis=1)
    blk_valid = jnp.clip(rows_left, 0, EXPERT_BLOCK).astype(I32)
    slot = _slots(pad_start.astype(I32), idx, rank, min(2048, t))
    slot_chunks = slot.reshape(TOP_K, t // SC_CHUNK, SC_CHUNK).transpose(1, 0, 2)
    xs = _sc_dispatch(h2p.reshape(t, d // 2), slot_chunks, n_rows)
    ys = _experts(xs, blk_e, n_used, blk_valid, w_gate, w_up, w_down)
    yg = _sc_gather(ys, slot_chunks, t)
    out = _combine(x1.reshape(t, d), h2, w.T, gt2, yg, sg, su, sd, s, min(256, t))
    return out.reshape(b, s, d), dict(idx=idx, w=w, rank=rank, cnt=cnt)


def kernel(x, c, ada_w, ada_b, norm1_w, norm2_w, w_in, q_norm_w, k_norm_w, cmp_pos, cmp_w1, cmp_b1, cmp_w2, attn_out_norm_w, hgrn_lb_param, rec_out_norm_w, w_out, router_w, router_bias, exp_w_gate, exp_w_up, exp_w_down, shared_w_gate, shared_w_up, shared_w_down):
    assert ada_w.shape[0] == 1, "one layer"
    assert x.shape[0] <= 8 and x.shape[1] % TK == 0 and x.shape[1] >= WINDOW + TQ
    l = 0
    x1, h2, h2p, gt2 = _mixer(x, c, ada_w[l], ada_b[l], norm1_w[l], norm2_w[l], w_in[l], q_norm_w[l], k_norm_w[l],
                         cmp_pos[l], cmp_w1[l], cmp_b1[l], cmp_w2[l], attn_out_norm_w[l], hgrn_lb_param,
                         rec_out_norm_w[l], w_out[l])
    out, _ = _moe_parts(x1, h2, h2p, gt2, router_w[l], router_bias[l], exp_w_gate[l], exp_w_up[l], exp_w_down[l],
                        shared_w_gate[l], shared_w_up[l], shared_w_down[l])
    return out
```

```python
import functools

import numpy as np
import jax
import jax.numpy as jnp
from jax import lax
from jax.experimental import pallas as pl
from jax.experimental.pallas import tpu as pltpu
from jax.experimental.pallas import tpu_sc as plsc

F32 = jnp.float32
BF16 = jnp.bfloat16
I32 = jnp.int32

D_MODEL = 1024
NSA_HEADS = 8
HEAD_DIM = 64
NSA_WIDTH = NSA_HEADS * HEAD_DIM
KV_HEADS = 2
HEADS_PER_KV = NSA_HEADS // KV_HEADS
KV_WIDTH = KV_HEADS * HEAD_DIM
CMP_BLOCK = 32
CMP_STRIDE = 16
CMP_HIDDEN = 256
SEL_BLOCK = 64
N_SELECT = 16
WINDOW = 512
HGRN_HEADS = 4
HGRN_DIM = 128
HGRN_WIDTH = HGRN_HEADS * HGRN_DIM
HGRN_CHUNK = 64
HGRN_SUB = 16
N_EXPERTS = 256
TOP_K = 8
N_GROUPS = 8
GROUP_SIZE = N_EXPERTS // N_GROUPS
TOPK_GROUPS = 4
EXPERT_FF = 256
SHARED_FF = 256
ROUTED_SCALE = 2.5
RMS_EPS = 1e-6
BIG = 1e9
LOG2E = 1.4426950408889634
GATE_PAD = 128
PROJ_COLS = NSA_WIDTH + 6 * KV_WIDTH + GATE_PAD + 4 * HGRN_WIDTH

VMEM_LIMIT = 56 * 1024 * 1024

TQ = 256
TK = 512
EXPERT_BLOCK = 384
HIGHEST = lax.Precision.HIGHEST


def _cparams(*sem):
    return pltpu.CompilerParams(dimension_semantics=sem, vmem_limit_bytes=VMEM_LIMIT)


def _sigmoid(x):
    return 1.0 / (1.0 + jnp.exp(-x))


def _dot_nt(a, b):
    return lax.dot_general(a, b, (((1,), (1,)), ((), ())), preferred_element_type=F32)


def _dot(a, b, **kw):
    return jnp.dot(a, b, preferred_element_type=F32, **kw)


def _split_dot(a_bf16_exact, x):
    hi = x.astype(BF16)
    lo = (x - hi.astype(F32)).astype(BF16)
    return _dot(a_bf16_exact, hi) + _dot(a_bf16_exact, lo)


def _mod_kernel(c_ref, w_ref, b_ref, o_ref):
    c = c_ref[...]
    cond = c * _sigmoid(c)
    o_ref[...] = _dot(cond, w_ref[...], precision=HIGHEST) + b_ref[...]


def _mod(c, ada_w, ada_b):
    b, d = c.shape
    rows = 8
    c_pad = jnp.zeros((rows, d), F32).at[:b].set(c)
    n = ada_w.shape[1]
    out = pl.pallas_call(
        _mod_kernel,
        grid=(n // d,),
        in_specs=[pl.BlockSpec((rows, d), lambda j: (0, 0)),
                  pl.BlockSpec((d, d), lambda j: (0, j)),
                  pl.BlockSpec((1, d), lambda j: (0, j))],
        out_specs=pl.BlockSpec((rows, d), lambda j: (0, j)),
        out_shape=jax.ShapeDtypeStruct((rows, n), F32),
        compiler_params=_cparams("parallel"),
        name="mod",
    )(c_pad, ada_w, ada_b.reshape(1, n))
    return out[:b]


def _head_rms(t, w):
    return t * lax.rsqrt(jnp.mean(t * t, axis=-1, keepdims=True) + RMS_EPS) * w


def _pos_digits(pos):
    lane = lax.broadcasted_iota(I32, pos.shape, 1)
    d0 = (lane == 0) | (lane == 3) | (lane == 6)
    d1 = (lane == 1) | (lane == 4) | (lane == 7)
    d2 = (lane == 2) | (lane == 5) | (lane == 8)
    dig = jnp.where(d0, pos >> 12, jnp.where(d1, (pos >> 6) & 63, jnp.where(d2, pos & 63, 0)))
    return dig.astype(F32)


def _inproj_kernel(x_ref, sc_ref, sh_ref, n1_ref, w_ref, qnw_ref, knw_ref, lbp_ref, qaug_ref,
                   q_ref, kcr_ref, vcr_ref, ks_ref, vst_ref, kw_ref, vwt_ref, gt_ref,
                   hq_ref, hk_ref, hlf_ref, hv_ref, hg_ref):
    x = x_ref[0]
    ms = jnp.mean(x * x, axis=-1, keepdims=True)
    h = x * lax.rsqrt(ms + RMS_EPS) * n1_ref[...] * (1.0 + sc_ref[0]) + sh_ref[0]
    p = _dot(h.astype(BF16), w_ref[...])
    tm = x.shape[0]

    qnw = qnw_ref[...]
    for hd in range(NSA_HEADS):
        t = p[:, hd * HEAD_DIM:(hd + 1) * HEAD_DIM]
        qn = _head_rms(t, qnw) * (HEAD_DIM ** -0.5 * LOG2E)
        qa = jnp.broadcast_to(qaug_ref[hd:hd + 1, :], (tm, HEAD_DIM))
        q_ref[0, hd] = jnp.concatenate([qn, qa], axis=1).astype(BF16)
    kaug = _pos_digits(pl.program_id(1) * tm + lax.broadcasted_iota(I32, (tm, HEAD_DIM), 0))

    o = NSA_WIDTH
    kcr_ref[0] = p[:, o:o + KV_WIDTH]
    vcr_ref[0] = p[:, o + KV_WIDTH:o + 2 * KV_WIDTH]
    ks = p[:, o + 2 * KV_WIDTH:o + 3 * KV_WIDTH]
    vs = p[:, o + 3 * KV_WIDTH:o + 4 * KV_WIDTH]
    kw = p[:, o + 4 * KV_WIDTH:o + 5 * KV_WIDTH]
    vw = p[:, o + 5 * KV_WIDTH:o + 6 * KV_WIDTH]
    for g in range(KV_HEADS):
        sl = slice(g * HEAD_DIM, (g + 1) * HEAD_DIM)
        ks_ref[0, g] = jnp.concatenate([_head_rms(ks[:, sl], knw_ref[1:2, :]), kaug], axis=1).astype(BF16)
        kw_ref[0, g] = jnp.concatenate([_head_rms(kw[:, sl], knw_ref[2:3, :]), kaug], axis=1).astype(BF16)
    vst = vs.T.astype(BF16)
    vwt = vw.T.astype(BF16)
    for g in range(KV_HEADS):
        vst_ref[0, g] = vst[g * HEAD_DIM:(g + 1) * HEAD_DIM, :]
        vwt_ref[0, g] = vwt[g * HEAD_DIM:(g + 1) * HEAD_DIM, :]

    o = NSA_WIDTH + 6 * KV_WIDTH
    gates = _sigmoid(p[:, o:o + GATE_PAD])
    gt_ref[0] = gates.T[:NSA_HEADS * 3, :]

    o = o + GATE_PAD
    hq = p[:, o:o + HGRN_WIDTH]
    hf = p[:, o + HGRN_WIDTH:o + 2 * HGRN_WIDTH]
    hi = p[:, o + 2 * HGRN_WIDTH:o + 3 * HGRN_WIDTH]
    hg = p[:, o + 3 * HGRN_WIDTH:o + 4 * HGRN_WIDTH]
    lbp = lbp_ref[...]
    e = jnp.exp(lbp - jnp.max(lbp, axis=0, keepdims=True))
    lb = e[0:1, :] / jnp.sum(e, axis=0, keepdims=True)
    f = lb + (1.0 - lb) * _sigmoid(hf)
    hq_ref[0] = hq * _sigmoid(hq) * (HGRN_DIM ** -0.5)
    hk_ref[0] = 1.0 - f
    hlf_ref[0] = jnp.log(f)
    hv_ref[0] = hi
    hg_ref[0] = _sigmoid(hg)


def _inproj(x, sc1, sh1, norm1_w, w_cat, q_norm_w, k_norm_w, lb_param, tm):
    b, s, d = x.shape
    row = lambda bi, i: (bi, i, 0)
    per_b = lambda bi, i: (bi, 0, 0)
    fixed2 = lambda bi, i: (0, 0)
    aw = 2 * HEAD_DIM
    rest = np.array([2.0 ** (-8.0 * (i + 1) / NSA_HEADS) for i in range(NSA_HEADS)], np.float64) * LOG2E
    qaug = np.zeros((NSA_HEADS, HEAD_DIM), np.float32)
    for i in range(3):
        term = rest.astype(np.float32).astype(BF16).astype(np.float64)
        rest = rest - term
        for dgt, wgt in enumerate((4096.0, 64.0, 1.0)):
            qaug[:, 3 * i + dgt] = term * wgt
    assert np.all(qaug == qaug.astype(BF16).astype(np.float32))
    out_shape = (
        jax.ShapeDtypeStruct((b, NSA_HEADS, s, aw), BF16),
        jax.ShapeDtypeStruct((b, s, KV_WIDTH), F32),
        jax.ShapeDtypeStruct((b, s, KV_WIDTH), F32),
        jax.ShapeDtypeStruct((b, KV_HEADS, s, aw), BF16),
        jax.ShapeDtypeStruct((b, KV_HEADS, HEAD_DIM, s), BF16),
        jax.ShapeDtypeStruct((b, KV_HEADS, s, aw), BF16),
        jax.ShapeDtypeStruct((b, KV_HEADS, HEAD_DIM, s), BF16),
        jax.ShapeDtypeStruct((b, NSA_HEADS * 3, s), F32),
    ) + tuple(jax.ShapeDtypeStruct((b, s, HGRN_WIDTH), F32) for _ in range(5))
    hm = lambda n, w: pl.BlockSpec((1, n, tm, w), lambda bi, i: (bi, 0, i, 0))
    hmt = lambda n, w: pl.BlockSpec((1, n, w, tm), lambda bi, i: (bi, 0, 0, i))
    out_specs = (
        hm(NSA_HEADS, aw),
        pl.BlockSpec((1, tm, KV_WIDTH), row),
        pl.BlockSpec((1, tm, KV_WIDTH), row),
        hm(KV_HEADS, aw), hmt(KV_HEADS, HEAD_DIM),
        hm(KV_HEADS, aw), hmt(KV_HEADS, HEAD_DIM),
        pl.BlockSpec((1, NSA_HEADS * 3, tm), lambda bi, i: (bi, 0, i)),
    ) + tuple(pl.BlockSpec((1, tm, HGRN_WIDTH), row) for _ in range(5))
    return pl.pallas_call(
        _inproj_kernel,
        grid=(b, s // tm),
        in_specs=[pl.BlockSpec((1, tm, d), row),
                  pl.BlockSpec((1, 1, d), per_b),
                  pl.BlockSpec((1, 1, d), per_b),
                  pl.BlockSpec((1, d), fixed2),
                  pl.BlockSpec((d, PROJ_COLS), fixed2),
                  pl.BlockSpec((1, HEAD_DIM), fixed2),
                  pl.BlockSpec((3, HEAD_DIM), fixed2),
                  pl.BlockSpec(lb_param.shape, fixed2),
                  pl.BlockSpec((NSA_HEADS, HEAD_DIM), fixed2)],
        out_specs=out_specs,
        out_shape=out_shape,
        compiler_params=_cparams("parallel", "parallel"),
        name="inproj",
    )(x, sc1, sh1, norm1_w, w_cat, q_norm_w, k_norm_w, lb_param, jnp.asarray(qaug))


def _gelu_tanh(x):
    return 0.5 * x * (1.0 + jnp.tanh(0.7978845608028654 * (x + 0.044715 * x * x * x)))


def _compress_kernel(kch_ref, vch_ref, pos_ref, wa_ref, wb_ref, b1_ref, w2_ref, knw_ref,
                     kc_ref, vct_ref):
    n = kch_ref.shape[1]
    outs = []
    for br, ch_ref in enumerate((kch_ref, vch_ref)):
        ch = ch_ref[0]
        a = _dot((ch + pos_ref[br, 0:1, :]).astype(BF16), wa_ref[br])
        bm = _dot((ch + pos_ref[br, 1:2, :]).astype(BF16), wb_ref[br])
        pre = a + pltpu.roll(bm, n - 1, 0) + b1_ref[br]
        hid = _gelu_tanh(pre).astype(BF16)
        outs.append([_dot(hid[:, g * CMP_HIDDEN:(g + 1) * CMP_HIDDEN], w2_ref[br]) for g in range(KV_HEADS)])
    end_digits = _pos_digits(lax.broadcasted_iota(I32, (n, HEAD_DIM), 0) * CMP_STRIDE + (CMP_BLOCK - 1))
    for g in range(KV_HEADS):
        kc_ref[0, g] = jnp.concatenate([_head_rms(outs[0][g], knw_ref[0:1, :]), end_digits], axis=1).astype(BF16)
    vct = jnp.concatenate(outs[1], axis=1).T.astype(BF16)
    for g in range(KV_HEADS):
        vct_ref[0, g] = vct[g * HEAD_DIM:(g + 1) * HEAD_DIM, :]


def _compress(kc_raw, vc_raw, cmp_pos, cmp_w1, cmp_b1, cmp_w2, k_norm_w):
    b, s, _ = kc_raw.shape
    n = s // CMP_STRIDE
    half = CMP_STRIDE
    cw = CMP_STRIDE * KV_WIDTH
    kch = kc_raw.reshape(b, n, cw)
    vch = vc_raw.reshape(b, n, cw)
    pos = cmp_pos.reshape(2, 2, half, 1, HEAD_DIM)
    pos = jnp.broadcast_to(pos, (2, 2, half, KV_HEADS, HEAD_DIM)).reshape(2, 2, cw)
    w1 = cmp_w1.reshape(2, 2, half, HEAD_DIM, CMP_HIDDEN)
    eye = jnp.eye(KV_HEADS, dtype=F32)
    wfull = jnp.einsum('rhjdn,gk->rhjgdkn', w1, eye).reshape(2, 2, cw, KV_HEADS * CMP_HIDDEN).astype(BF16)
    b1 = jnp.tile(cmp_b1.reshape(2, 1, CMP_HIDDEN), (1, 1, KV_HEADS))
    fix = lambda r: (lambda bi: (0,) * r)
    return pl.pallas_call(
        _compress_kernel,
        grid=(b,),
        in_specs=[pl.BlockSpec((1, n, cw), lambda bi: (bi, 0, 0)),
                  pl.BlockSpec((1, n, cw), lambda bi: (bi, 0, 0)),
                  pl.BlockSpec((2, 2, cw), fix(3)),
                  pl.BlockSpec((2, cw, KV_HEADS * CMP_HIDDEN), fix(3)),
                  pl.BlockSpec((2, cw, KV_HEADS * CMP_HIDDEN), fix(3)),
                  pl.BlockSpec((2, 1, KV_HEADS * CMP_HIDDEN), fix(3)),
                  pl.BlockSpec((2, CMP_HIDDEN, HEAD_DIM), fix(3)),
                  pl.BlockSpec((3, HEAD_DIM), fix(2))],
        out_specs=(pl.BlockSpec((1, KV_HEADS, n, 2 * HEAD_DIM), lambda bi: (bi, 0, 0, 0)),
                   pl.BlockSpec((1, KV_HEADS, HEAD_DIM, n), lambda bi: (bi, 0, 0, 0))),
        out_shape=(jax.ShapeDtypeStruct((b, KV_HEADS, n, 2 * HEAD_DIM), BF16),
                   jax.ShapeDtypeStruct((b, KV_HEADS, HEAD_DIM, n), BF16)),
        compiler_params=_cparams("parallel"),
        name="compress",
    )(kch, vch, pos, wfull[:, 0], wfull[:, 1], b1, cmp_w2.astype(BF16), k_norm_w)


def _nsa_kernel(q_ref, kc_ref, vct_ref, ks_ref, vst_ref, kw_ref, vwt_ref, gt_ref, cdiff_ref, wdiff_ref,
                ovl_ref, oh_ref, onw_ref, wmask_ref, o_ref, buf_a, buf_b, m_scr, acc_scr, *, n_top):
    q0 = pl.program_id(2) * TQ
    ncols = HEADS_PER_KV * TQ
    q = q_ref[0].reshape(ncols, 2 * HEAD_DIM)
    ns = ovl_ref.shape[0]

    s = jnp.where(cdiff_ref[...] <= q0, _dot_nt(kc_ref[0, 0], q), -jnp.inf)
    m = jnp.max(s, axis=0, keepdims=True)
    m = jnp.where(m == -jnp.inf, 0.0, m)
    e = jnp.exp2(s - m)
    p = e / jnp.maximum(jnp.sum(e, axis=0, keepdims=True), 1e-30)
    o_c = _dot(vct_ref[0, 0], p.astype(BF16))

    psum = p[:, 0:TQ]
    for hh in range(1, HEADS_PER_KV):
        psum = psum + p[:, hh * TQ:(hh + 1) * TQ]
    imp = _split_dot(ovl_ref[...], psum)
    blk = lax.broadcasted_iota(I32, (ns, TQ), 0)
    tq = q0 + lax.broadcasted_iota(I32, (ns, TQ), 1)
    cur = tq >> 6
    forced = (blk == 0) | (blk == cur) | (blk == cur - 1)
    rank = jnp.where(forced, BIG, jnp.where(blk * SEL_BLOCK <= tq, imp, -BIG))

    blkf = blk.astype(F32)

    bias = jnp.full((ns, TQ), -1e30, F32)
    for _ in range(n_top):
        mx = jnp.max(rank, axis=0, keepdims=True)
        first = jnp.min(jnp.where(rank == mx, blkf, float(ns)), axis=0, keepdims=True)
        hit = blkf == first
        rank = jnp.where(hit, -jnp.inf, rank)
        bias = jnp.where(hit, 0.0, bias)

    if ns < 128:
        bias = jnp.concatenate([bias, jnp.zeros((128 - ns, TQ), F32)], axis=0)
    bias_t = bias.T.astype(BF16)
    qq = jnp.concatenate([q, jnp.concatenate([bias_t] * HEADS_PER_KV, axis=0)], axis=1)
    ones_rows = jnp.ones((16, TK), BF16)

    def scores(j):
        k0 = pl.multiple_of(j * TK, TK)
        kk = jnp.concatenate([ks_ref[0, 0, pl.ds(k0, TK), :], oh_ref[pl.ds(k0, TK), :]], axis=1)
        return _dot_nt(kk, qq)

    def consume(buf, j, causal, part):
        sc = buf[...]
        if causal:
            sc = jnp.where(wdiff_ref[0:TK, :] + (q0 - j * TK) >= 0, sc, -1e30)
        k0 = pl.multiple_of(j * TK, TK)
        m_run = m_scr[part]
        m_new = jnp.maximum(m_run, jnp.max(sc, axis=0, keepdims=True))
        ex = jnp.exp2(sc - m_new).astype(BF16)
        va = jnp.concatenate([vst_ref[0, 0, :, pl.ds(k0, TK)], ones_rows], axis=0)
        acc_scr[part] = jnp.exp2(m_run - m_new) * acc_scr[part] + _dot(va, ex)
        m_scr[part] = m_new

    n_past = q0 // TK
    m_scr[...] = jnp.full(m_scr.shape, -1e30, F32)
    acc_scr[...] = jnp.zeros(acc_scr.shape, F32)
    buf_a[...] = scores(0)

    nw = WINDOW + TQ
    start = pl.multiple_of(jnp.maximum(q0 - WINDOW, 0), TQ)
    sw = _dot_nt(kw_ref[0, 0, pl.ds(start, nw), :], q) + wmask_ref[0]
    ew = jnp.exp2(sw - jnp.max(sw, axis=0, keepdims=True))
    vw_aug = jnp.concatenate([vwt_ref[0, 0, :, pl.ds(start, nw)], jnp.ones((16, nw), BF16)], axis=0)
    acc_w = _dot(vw_aug, ew.astype(BF16))
    o_w = acc_w[0:HEAD_DIM, :] / acc_w[HEAD_DIM:HEAD_DIM + 1, :]

    def tiles(first, count):
        for u in range(0, count, 2):
            buf_b[...] = scores(first + u + 1)
            consume(buf_a, first + u, False, 0)
            buf_a[...] = scores(first + u + 2)
            consume(buf_b, first + u + 1, False, 1)
        return 0

    lax.fori_loop(0, n_past // 4, lambda i, _: tiles(4 * i, 4), 0)
    lax.fori_loop(0, (n_past // 2) % 2, lambda i, _: tiles((n_past // 4) * 4, 2), 0)

    @pl.when(n_past % 2 == 1)
    def _():
        buf_b[...] = scores(n_past)
        consume(buf_a, n_past - 1, False, 0)
        consume(buf_b, n_past, True, 1)

    @pl.when(n_past % 2 == 0)
    def _():
        consume(buf_a, n_past, True, 0)

    m_all = jnp.maximum(m_scr[0], m_scr[1])
    acc_s = jnp.exp2(m_scr[0] - m_all) * acc_scr[0] + jnp.exp2(m_scr[1] - m_all) * acc_scr[1]
    o_s = acc_s[0:HEAD_DIM, :] / acc_s[HEAD_DIM:HEAD_DIM + 1, :]

    gt = gt_ref[0, 0]
    outs = []
    for hh in range(HEADS_PER_KV):
        cs = slice(hh * TQ, (hh + 1) * TQ)
        o = (gt[3 * hh:3 * hh + 1, :] * o_c[:, cs] + gt[3 * hh + 1:3 * hh + 2, :] * o_s[:, cs]
             + gt[3 * hh + 2:3 * hh + 3, :] * o_w[:, cs])
        o = o * lax.rsqrt(jnp.mean(o * o, axis=0, keepdims=True) + RMS_EPS) * onw_ref[0, hh]
        outs.append(o)
    o_ref[0] = jnp.concatenate(outs, axis=0).T


def _nsa(q, kc, vct, ks, vst, kw, vwt, gates_t, attn_out_norm_w):
    b, _, s, aw = q.shape
    nc = kc.shape[2]
    ns = s // SEL_BLOCK
    n_top = min(N_SELECT, ns)
    ncols = HEADS_PER_KV * TQ
    nw = WINDOW + TQ
    tl = np.arange(ncols)[None, :] & (TQ - 1)
    cdiff = jnp.asarray((np.arange(nc)[:, None] * CMP_STRIDE + (CMP_BLOCK - 1) - tl).astype(np.int32))
    wdiff_np = (tl - np.arange(nw)[:, None]).astype(np.int32)
    wdiff = jnp.asarray(wdiff_np)
    n_off = WINDOW // TQ + 1
    dist_np = wdiff_np[None] + (np.arange(n_off) * TQ)[:, None, None]
    wmask = jnp.asarray(np.where((dist_np >= 0) & (dist_np < WINDOW), 0.0, -np.inf).astype(np.float32))
    ci = np.arange(nc)[None, :] * CMP_STRIDE
    bj = np.arange(ns)[:, None]
    ovl = ((ci < (bj + 1) * SEL_BLOCK) & (ci + CMP_BLOCK > bj * SEL_BLOCK) & (np.arange(nc)[None, :] < nc - 1))
    ovl = jnp.asarray(ovl.astype(np.float32)).astype(BF16)
    assert ns <= 128
    onehot = (np.arange(s)[:, None] // SEL_BLOCK == np.arange(128)[None, :])
    onehot = jnp.asarray(onehot.astype(np.float32)).astype(BF16)
    onw = jnp.broadcast_to(attn_out_norm_w.reshape(KV_HEADS, HEADS_PER_KV, HEAD_DIM, 1),
                           (KV_HEADS, HEADS_PER_KV, HEAD_DIM, TQ))
    gt = gates_t.reshape(b, KV_HEADS, HEADS_PER_KV * 3, s)
    per_bg = lambda bi, g, i: (bi, g, 0, 0)
    fixed = lambda bi, g, i: (0, 0)
    return pl.pallas_call(
        functools.partial(_nsa_kernel, n_top=n_top),
        grid=(b, KV_HEADS, s // TQ),
        in_specs=[pl.BlockSpec((1, HEADS_PER_KV, TQ, aw), lambda bi, g, i: (bi, g, i, 0)),
                  pl.BlockSpec((1, 1, nc, aw), per_bg),
                  pl.BlockSpec((1, 1, HEAD_DIM, nc), per_bg),
                  pl.BlockSpec((1, 1, s, aw), per_bg),
                  pl.BlockSpec((1, 1, HEAD_DIM, s), per_bg),
                  pl.BlockSpec((1, 1, s, aw), per_bg),
                  pl.BlockSpec((1, 1, HEAD_DIM, s), per_bg),
                  pl.BlockSpec((1, 1, HEADS_PER_KV * 3, TQ), lambda bi, g, i: (bi, g, 0, i)),
                  pl.BlockSpec((nc, ncols), fixed, pipeline_mode=pl.Buffered(1)),
                  pl.BlockSpec((nw, ncols), fixed, pipeline_mode=pl.Buffered(1)),
                  pl.BlockSpec((ns, nc), fixed, pipeline_mode=pl.Buffered(1)),
                  pl.BlockSpec((s, 128), fixed, pipeline_mode=pl.Buffered(1)),
                  pl.BlockSpec((1, HEADS_PER_KV, HEAD_DIM, TQ), lambda bi, g, i: (g, 0, 0, 0)),
                  pl.BlockSpec((1, nw, ncols), lambda bi, g, i: (jnp.minimum(i, n_off - 1), 0, 0))],
        out_specs=pl.BlockSpec((1, TQ, HEADS_PER_KV * HEAD_DIM), lambda bi, g, i: (bi, i, g)),
        out_shape=jax.ShapeDtypeStruct((b, s, NSA_WIDTH), F32),
        scratch_shapes=[pltpu.VMEM((TK, ncols), F32), pltpu.VMEM((TK, ncols), F32),
                        pltpu.VMEM((2, 1, ncols), F32), pltpu.VMEM((2, HEAD_DIM + 16, ncols), F32)],
        compiler_params=_cparams("parallel", "parallel", "arbitrary"),
        name="nsa",
    )(q, kc, vct, ks, vst, kw, vwt, gt, cdiff, wdiff, ovl, onehot, onw, wmask)


def _hgrn_kernel(q_ref, k_ref, lf_ref, v_ref, g_ref, onw_ref, o_ref, state_scr, *, n_chunks):
    c = HGRN_CHUNK

    @pl.when(pl.program_id(1) == 0)
    def _():
        state_scr[...] = jnp.zeros_like(state_scr)

    ri = lax.broadcasted_iota(I32, (c, c), 0)
    ci = lax.broadcasted_iota(I32, (c, c), 1)
    tril = (ri >= ci).astype(F32)
    rsub = ri // HGRN_SUB
    rin = ri & (HGRN_SUB - 1)

    def head_chunk(r0, hd, state_t):
        cols = slice(hd * HGRN_DIM, (hd + 1) * HGRN_DIM)
        q = q_ref[0, pl.ds(r0, c), cols]
        k = k_ref[0, pl.ds(r0, c), cols]
        lf = lf_ref[0, pl.ds(r0, c), cols]
        v = v_ref[0, pl.ds(r0, c), cols]
        cum = _dot(tril, lf, precision=HIGHEST)
        o = _dot_nt((q * jnp.exp(cum)).astype(BF16), state_t.astype(BF16))
        scores = jnp.zeros((c, c), F32)
        for i in range(1, c // HGRN_SUB):
            ref_row = cum[i * HGRN_SUB - 1:i * HGRN_SUB, :]
            qs = q * jnp.exp(jnp.minimum(cum - ref_row, 0.0))
            kd = k * jnp.exp(jnp.minimum(ref_row - cum, 0.0))
            blk = _dot_nt(qs.astype(BF16), kd.astype(BF16))
            scores = jnp.where((rsub == i) & (ci < i * HGRN_SUB), blk, scores)
        for d in range(HGRN_SUB):
            if d == 0:
                w = jnp.sum(q * k, axis=-1, keepdims=True)
            else:
                ksh = pltpu.roll(k, d, 0)
                csh = pltpu.roll(cum, d, 0)
                w = jnp.sum(q * ksh * jnp.exp(jnp.minimum(cum - csh, 0.0)), axis=-1, keepdims=True)
            scores = jnp.where((ri - ci == d) & (rin >= d), w, scores)
        o = o + _dot(scores.astype(BF16), v.astype(BF16))
        last = cum[c - 1:c, :]
        kd = (k * jnp.exp(last - cum)).astype(BF16)
        state_t = state_t * jnp.exp(last) + _dot(v.T.astype(BF16), kd)
        o = o * g_ref[0, pl.ds(r0, c), cols]
        o = o * lax.rsqrt(jnp.mean(o * o, axis=-1, keepdims=True) + RMS_EPS) * onw_ref[:, cols]
        o_ref[0, pl.ds(r0, c), cols] = o
        return state_t

    def chunk(ck, states):
        r0 = pl.multiple_of(ck * c, c)
        return tuple(head_chunk(r0, hd, states[hd]) for hd in range(HGRN_HEADS))

    states = lax.fori_loop(0, n_chunks, chunk, tuple(state_scr[hd] for hd in range(HGRN_HEADS)))
    for hd in range(HGRN_HEADS):
        state_scr[hd] = states[hd]


def _hgrn(hq, hk, hlf, hv, hg, rec_out_norm_w, rows):
    b, s, _ = hq.shape
    blk = pl.BlockSpec((1, rows, HGRN_WIDTH), lambda bi, i: (bi, i, 0))
    return pl.pallas_call(
        functools.partial(_hgrn_kernel, n_chunks=rows // HGRN_CHUNK),
        grid=(b, s // rows),
        in_specs=[blk, blk, blk, blk, blk,
                  pl.BlockSpec((1, HGRN_WIDTH), lambda bi, i: (0, 0))],
        out_specs=blk,
        out_shape=jax.ShapeDtypeStruct((b, s, HGRN_WIDTH), F32),
        scratch_shapes=[pltpu.VMEM((HGRN_HEADS, HGRN_DIM, HGRN_DIM), F32)],
        compiler_params=_cparams("parallel", "arbitrary"),
        name="hgrn",
    )(hq, hk, hlf, hv, hg, rec_out_norm_w.reshape(1, HGRN_WIDTH))


def _outproj_kernel(x_ref, a_ref, r_ref, wa_ref, wr_ref, gt_ref, sc_ref, sh_ref, n2_ref, x1_ref, h2_ref, h2p_ref):
    mixed = _dot(a_ref[0].astype(BF16), wa_ref[...]) + _dot(r_ref[0].astype(BF16), wr_ref[...])
    x1 = x_ref[0] + gt_ref[0] * mixed
    x1_ref[0] = x1
    ms = jnp.mean(x1 * x1, axis=-1, keepdims=True)
    h2 = x1 * lax.rsqrt(ms + RMS_EPS) * n2_ref[...] * (1.0 + sc_ref[0]) + sh_ref[0]
    h2_ref[0] = h2
    h2p_ref[0] = _pack_bf16_pair(h2[:, :D_MODEL // 2], h2[:, D_MODEL // 2:])


def _outproj(x, attn, rec, w_out, gt1, sc2, sh2, norm2_w, tm):
    b, s, d = x.shape
    row = lambda bi, i: (bi, i, 0)
    per_b = lambda bi, i: (bi, 0, 0)
    fixed2 = lambda bi, i: (0, 0)
    w = w_out.astype(BF16)
    return pl.pallas_call(
        _outproj_kernel,
        grid=(b, s // tm),
        in_specs=[pl.BlockSpec((1, tm, d), row),
                  pl.BlockSpec((1, tm, NSA_WIDTH), row),
                  pl.BlockSpec((1, tm, HGRN_WIDTH), row),
                  pl.BlockSpec((NSA_WIDTH, d), fixed2),
                  pl.BlockSpec((HGRN_WIDTH, d), fixed2),
                  pl.BlockSpec((1, 1, d), per_b),
                  pl.BlockSpec((1, 1, d), per_b),
                  pl.BlockSpec((1, 1, d), per_b),
                  pl.BlockSpec((1, d), fixed2)],
        out_specs=(pl.BlockSpec((1, tm, d), row), pl.BlockSpec((1, tm, d), row), pl.BlockSpec((1, tm, d // 2), row)),
        out_shape=(jax.ShapeDtypeStruct((b, s, d), F32), jax.ShapeDtypeStruct((b, s, d), F32),
                   jax.ShapeDtypeStruct((b, s, d // 2), jnp.uint32)),
        compiler_params=_cparams("parallel", "parallel"),
        name="outproj",
    )(x, attn, rec, w[:NSA_WIDTH], w[NSA_WIDTH:], gt1, sc2, sh2, norm2_w)


def _mixer(x, c, ada_w, ada_b, norm1_w, norm2_w, w_in, q_norm_w, k_norm_w, cmp_pos, cmp_w1, cmp_b1, cmp_w2,
           attn_out_norm_w, hgrn_lb_param, rec_out_norm_w, w_out):
    b, s, d = x.shape
    mod = _mod(c, ada_w, ada_b)
    sh1, sc1, gt1, sh2, sc2, gt2 = [m.reshape(b, 1, d) for m in jnp.split(mod, 6, axis=-1)]
    o = NSA_WIDTH + 6 * KV_WIDTH
    w_cat = jnp.concatenate([w_in[:, :o], w_in[:, o:o + NSA_HEADS * 3],
                             jnp.zeros((d, GATE_PAD - NSA_HEADS * 3), w_in.dtype),
                             w_in[:, o + NSA_HEADS * 3:]], axis=1).astype(BF16)
    tm = min(256, s)
    (q, kc_raw, vc_raw, ks, vst, kw, vwt, gates_t, hq, hk, hlf, hv, hg) = _inproj(
        x, sc1, sh1, norm1_w.reshape(1, d), w_cat, q_norm_w.reshape(1, HEAD_DIM), k_norm_w, hgrn_lb_param, tm)
    kc, vct = _compress(kc_raw, vc_raw, cmp_pos, cmp_w1, cmp_b1, cmp_w2, k_norm_w)
    attn = _nsa(q, kc, vct, ks, vst, kw, vwt, gates_t, attn_out_norm_w)
    rec = _hgrn(hq, hk, hlf, hv, hg, rec_out_norm_w, min(512, s))
    x1, h2, h2p = _outproj(x, attn, rec, w_out, gt1, sc2, sh2, norm2_w.reshape(1, d), tm)
    return x1, h2, h2p, gt2


def _router_kernel(h_ref, rwt_ref, bias_ref, tri_ref, ones_ref, idx_ref, w_ref, rank_ref, cnt_ref, carry_scr, *, tr):
    @pl.when(pl.program_id(0) == 0)
    def _():
        carry_scr[...] = jnp.zeros_like(carry_scr)

    h = h_ref[...]
    h_hi = h.astype(BF16)
    h_lo = (h - h_hi.astype(F32)).astype(BF16)
    logits = _dot_nt(rwt_ref[0], h_hi) + _dot_nt(rwt_ref[1], h_hi) + _dot_nt(rwt_ref[0], h_lo)
    scores = _sigmoid(logits)
    biased = scores + bias_ref[...]
    neg = -jnp.inf

    gs = []
    for g in range(N_GROUPS):
        sub = biased[g * GROUP_SIZE:(g + 1) * GROUP_SIZE, :]
        m1 = jnp.max(sub, axis=0, keepdims=True)
        dup = jnp.sum((sub == m1).astype(F32), axis=0, keepdims=True)
        m2 = jnp.max(jnp.where(sub < m1, sub, neg), axis=0, keepdims=True)
        gs.append(m1 + jnp.where(dup >= 2.0, m1, m2))
    parts = []
    for g in range(N_GROUPS):
        beaten = jnp.zeros_like(gs[g])
        for g2 in range(N_GROUPS):
            if g2 != g:
                beats = (gs[g2] >= gs[g]) if g2 < g else (gs[g2] > gs[g])
                beaten = beaten + beats.astype(F32)
        sub = biased[g * GROUP_SIZE:(g + 1) * GROUP_SIZE, :]
        parts.append(jnp.where(beaten < float(TOPK_GROUPS), sub, neg))
    cand = jnp.concatenate(parts, axis=0)

    rowf = lax.broadcasted_iota(I32, (N_EXPERTS, tr), 0).astype(F32)
    idx_rows, w_rows, hits = [], [], []
    multi = jnp.zeros((N_EXPERTS, tr), F32)
    for _ in range(TOP_K):
        mx = jnp.max(cand, axis=0, keepdims=True)
        first = jnp.min(jnp.where(cand == mx, rowf, float(N_EXPERTS)), axis=0, keepdims=True)
        hit = rowf == first
        idx_rows.append(first)
        w_rows.append(jnp.sum(jnp.where(hit, scores, 0.0), axis=0, keepdims=True))
        cand = jnp.where(hit, neg, cand)
        multi = jnp.where(hit, 1.0, multi)
    w = jnp.concatenate(w_rows, axis=0)
    w_ref[...] = w / jnp.sum(w, axis=0, keepdims=True) * ROUTED_SCALE
    idx = jnp.concatenate(idx_rows, axis=0)
    idx_ref[...] = idx.astype(I32)

    carry = carry_scr[...]
    mb = multi.astype(BF16)
    before = _dot(mb, tri_ref[...]) + jnp.concatenate([carry] * (tr // 128), axis=1)
    rank_rows = [jnp.sum(jnp.where(rowf == idx_rows[k], before, 0.0), axis=0, keepdims=True) for k in range(TOP_K)]
    rank_ref[...] = jnp.concatenate(rank_rows, axis=0).astype(I32)
    carry = carry + _dot(mb, ones_ref[...])
    carry_scr[...] = carry
    cnt_ref[...] = carry


def _router(h2, router_w, router_bias, tr):
    t, d = h2.shape
    tri = jnp.asarray(np.triu(np.ones((tr, tr), np.float32), 1)).astype(BF16)
    ones = jnp.ones((tr, 128), BF16)
    tok = pl.BlockSpec((TOP_K, tr), lambda i: (0, i))
    fixed = lambda i: (0, 0)
    rwt = router_w.T
    rwt_hi = rwt.astype(BF16)
    rwt_split = jnp.stack([rwt_hi, (rwt - rwt_hi.astype(F32)).astype(BF16)])
    return pl.pallas_call(
        functools.partial(_router_kernel, tr=tr),
        grid=(t // tr,),
        in_specs=[pl.BlockSpec((tr, d), lambda i: (i, 0)),
                  pl.BlockSpec((2, N_EXPERTS, d), lambda i: (0, 0, 0)),
                  pl.BlockSpec((N_EXPERTS, 1), fixed),
                  pl.BlockSpec((tr, tr), fixed),
                  pl.BlockSpec((tr, 128), fixed)],
        out_specs=(tok, tok, tok, pl.BlockSpec((N_EXPERTS, 128), fixed)),
        out_shape=(jax.ShapeDtypeStruct((TOP_K, t), I32), jax.ShapeDtypeStruct((TOP_K, t), F32),
                   jax.ShapeDtypeStruct((TOP_K, t), I32), jax.ShapeDtypeStruct((N_EXPERTS, 128), F32)),
        scratch_shapes=[pltpu.VMEM((N_EXPERTS, 128), F32)],
        compiler_params=_cparams("arbitrary"),
        name="router",
    )(h2, rwt_split, router_bias.reshape(N_EXPERTS, 1), tri, ones)


def _pack_bf16_pair(a, b):
    ua = lax.bitcast_convert_type(a.astype(BF16).astype(F32), jnp.uint32)
    ub = lax.bitcast_convert_type(b.astype(BF16).astype(F32), jnp.uint32)
    return ua | (ub >> 16)


def _unpack_bf16_pair(w):
    a = lax.bitcast_convert_type(w & jnp.uint32(0xFFFF0000), F32)
    b = lax.bitcast_convert_type(w << 16, F32)
    return a, b


def _slot_kernel(ps_ref, idx_ref, rank_ref, slot_ref):
    idx = idx_ref[...]

    def body(e, acc):
        return jnp.where(idx == e, ps_ref[e], acc)

    slot_ref[...] = lax.fori_loop(0, N_EXPERTS, body, jnp.zeros_like(idx)) + rank_ref[...]


def _slots(pad_start, idx, rank, tt):
    t = idx.shape[1]
    tok = pl.BlockSpec((TOP_K, tt), lambda i, ps: (0, i))
    return pl.pallas_call(
        _slot_kernel,
        grid_spec=pltpu.PrefetchScalarGridSpec(num_scalar_prefetch=1, grid=(t // tt,),
                                               in_specs=[tok, tok], out_specs=tok),
        out_shape=jax.ShapeDtypeStruct((TOP_K, t), I32),
        compiler_params=_cparams("parallel"),
        name="slots",
    )(pad_start, idx, rank)


SC_CORES = 2
SC_SUBCORES = 16
SC_CHUNK = 64


def _sc_mesh():
    return plsc.VectorSubcoreMesh(core_axis_name="c", subcore_axis_name="s")


def _sc_dispatch(h2p, slot_chunks, n_rows):
    t, dw = h2p.shape
    per = slot_chunks.shape[0] // (SC_CORES * SC_SUBCORES)

    def body(h_hbm, slot_hbm, xs_hbm, idx_v, rows_v, sem):
        wid = lax.axis_index("s") * SC_CORES + lax.axis_index("c")

        @pl.loop(0, per)
        def _(c):
            ch = wid * per + c
            pltpu.sync_copy(slot_hbm.at[ch], idx_v)
            pltpu.sync_copy(h_hbm.at[pl.ds(ch * SC_CHUNK, SC_CHUNK)], rows_v)
            copies = [pltpu.async_copy(rows_v, xs_hbm.at[idx_v.at[k]], sem) for k in range(TOP_K)]
            for cp in copies:
                cp.wait()

    return pl.kernel(
        body, out_type=jax.ShapeDtypeStruct((n_rows, dw), h2p.dtype), mesh=_sc_mesh(),
        scratch_types=[pltpu.VMEM((TOP_K, SC_CHUNK), I32), pltpu.VMEM((SC_CHUNK, dw), h2p.dtype),
                       pltpu.SemaphoreType.DMA],
    )(h2p, slot_chunks)


def _sc_gather(ys, slot_chunks, t):
    dw = ys.shape[1]
    per = slot_chunks.shape[0] // (SC_CORES * SC_SUBCORES)

    def body(ys_hbm, slot_hbm, yg_hbm, idx_v, rows_v, gsem, wsem):
        wid = lax.axis_index("s") * SC_CORES + lax.axis_index("c")

        @pl.loop(0, per)
        def _(c):
            ch = wid * per + c
            pltpu.sync_copy(slot_hbm.at[ch], idx_v)
            gathers = [None] * TOP_K
            writes = [None] * TOP_K
            gathers[0] = pltpu.async_copy(ys_hbm.at[idx_v.at[0]], rows_v.at[0], gsem)
            for k in range(TOP_K):
                gathers[k].wait()
                if k + 1 < TOP_K:
                    if k >= 1:
                        writes[k - 1].wait()
                    gathers[k + 1] = pltpu.async_copy(ys_hbm.at[idx_v.at[k + 1]], rows_v.at[(k + 1) % 2], gsem)
                writes[k] = pltpu.async_copy(rows_v.at[k % 2], yg_hbm.at[k, pl.ds(ch * SC_CHUNK, SC_CHUNK)], wsem)
            writes[TOP_K - 2].wait()
            writes[TOP_K - 1].wait()

    return pl.kernel(
        body, out_type=jax.ShapeDtypeStruct((TOP_K, t, dw), ys.dtype), mesh=_sc_mesh(),
        scratch_types=[pltpu.VMEM((TOP_K, SC_CHUNK), I32), pltpu.VMEM((2, SC_CHUNK, dw), ys.dtype),
                       pltpu.SemaphoreType.DMA, pltpu.SemaphoreType.DMA],
    )(ys, slot_chunks)


def _experts_kernel(be_ref, nu_ref, bv_ref, xs_ref, wg_ref, wu_ref, wd_ref, ys_ref):
    i = pl.program_id(0)
    half = D_MODEL // 2

    @pl.when(i < nu_ref[0])
    def _():
        live = lax.broadcasted_iota(I32, xs_ref.shape, 0) < bv_ref[i]
        xa, xb = _unpack_bf16_pair(jnp.where(live, xs_ref[...], jnp.uint32(0)))
        xa, xb = xa.astype(BF16), xb.astype(BF16)
        g = _dot(xa, wg_ref[0, :half].astype(BF16)) + _dot(xb, wg_ref[0, half:].astype(BF16))
        u = _dot(xa, wu_ref[0, :half].astype(BF16)) + _dot(xb, wu_ref[0, half:].astype(BF16))
        act = (g * _sigmoid(g) * u).astype(BF16)
        y = _dot(act, wd_ref[0].astype(BF16))
        ys_ref[...] = _pack_bf16_pair(y[:, :half], y[:, half:])

    @pl.when(i >= nu_ref[0])
    def _():
        ys_ref[...] = jnp.zeros_like(ys_ref)


def _experts(xs, blk_e, n_used, blk_valid, w_gate, w_up, w_down):
    n_rows, dw = xs.shape
    d = w_gate.shape[1]
    nblk = n_rows // EXPERT_BLOCK
    row_map = lambda i, be, nu, bv: (jnp.minimum(i, nu[0] - 1), 0)
    w_map = lambda i, be, nu, bv: (be[i], 0, 0)
    return pl.pallas_call(
        _experts_kernel,
        grid_spec=pltpu.PrefetchScalarGridSpec(
            num_scalar_prefetch=3,
            grid=(nblk,),
            in_specs=[pl.BlockSpec((EXPERT_BLOCK, dw), row_map),
                      pl.BlockSpec((1, d, EXPERT_FF), w_map),
                      pl.BlockSpec((1, d, EXPERT_FF), w_map),
                      pl.BlockSpec((1, EXPERT_FF, d), w_map)],
            out_specs=pl.BlockSpec((EXPERT_BLOCK, dw), lambda i, be, nu, bv: (i, 0))),
        out_shape=jax.ShapeDtypeStruct((n_rows, dw), xs.dtype),
        compiler_params=_cparams("arbitrary"),
        name="experts",
    )(blk_e, n_used, blk_valid, xs, w_gate, w_up, w_down)


def _combine_kernel(x1_ref, h_ref, w_ref, gt_ref, sg_ref, su_ref, sd_ref, yg_ref, o_ref):
    tc = x1_ref.shape[0]
    half = D_MODEL // 2
    hb = h_ref[...].astype(BF16)
    g = _dot(hb, sg_ref[...])
    u = _dot(hb, su_ref[...])
    ffn = _dot((g * _sigmoid(g) * u).astype(BF16), sd_ref[...])

    w = w_ref[...]
    ra = jnp.zeros((tc, half), F32)
    rb = jnp.zeros((tc, half), F32)
    for k in range(TOP_K):
        ya, yb = _unpack_bf16_pair(yg_ref[k])
        ra = ra + w[:, k:k + 1] * ya
        rb = rb + w[:, k:k + 1] * yb
    ffn = ffn + jnp.concatenate([ra, rb], axis=1)
    o_ref[...] = x1_ref[...] + gt_ref[0] * ffn


def _combine(x1, h2, w_tok, gt2, yg, sg, su, sd, seq, tc):
    t, d = x1.shape
    row = lambda i: (i, 0)
    fixed = lambda i: (0, 0)
    return pl.pallas_call(
        _combine_kernel,
        grid=(t // tc,),
        in_specs=[pl.BlockSpec((tc, d), row),
                  pl.BlockSpec((tc, d), row),
                  pl.BlockSpec((tc, TOP_K), row),
                  pl.BlockSpec((1, 1, d), lambda i: ((i * tc) // seq, 0, 0)),
                  pl.BlockSpec((d, SHARED_FF), fixed),
                  pl.BlockSpec((d, SHARED_FF), fixed),
                  pl.BlockSpec((SHARED_FF, d), fixed),
                  pl.BlockSpec((TOP_K, tc, d // 2), lambda i: (0, i, 0))],
        out_specs=pl.BlockSpec((tc, d), row),
        out_shape=jax.ShapeDtypeStruct((t, d), F32),
        compiler_params=_cparams("parallel"),
        name="combine",
    )(x1, h2, w_tok, gt2, sg.astype(BF16), su.astype(BF16), sd.astype(BF16), yg)


def _moe_parts(x1, h2, h2p, gt2, router_w, router_bias, w_gate, w_up, w_down, sg, su, sd):
    b, s, d = x1.shape
    t = b * s
    h2 = h2.reshape(t, d)
    idx, w, rank, cnt = _router(h2, router_w, router_bias, min(256, t))
    counts = cnt[:, 0].astype(I32)
    padded = (counts + EXPERT_BLOCK - 1) // EXPERT_BLOCK * EXPERT_BLOCK
    pad_end = jnp.cumsum(padded)
    pad_start = pad_end - padded
    n_rows = t * TOP_K + N_EXPERTS * EXPERT_BLOCK
    nblk = n_rows // EXPERT_BLOCK
    n_used = (pad_end[-1:] // EXPERT_BLOCK).astype(I32)
    blk_start = jnp.arange(nblk, dtype=I32) * EXPERT_BLOCK
    owns = (pad_start[None, :] <= blk_start[:, None]) & (blk_start[:, None] < pad_end[None, :])
    e_ids = jnp.arange(N_EXPERTS, dtype=I32)[None, :]
    last_e = jnp.max(jnp.where(counts > 0, e_ids[0], 0))
    blk_e = jnp.where(blk_start < pad_end[-1], jnp.sum(jnp.where(owns, e_ids, 0), axis=1), last_e).astype(I32)
    rows_left = jnp.sum(jnp.where(owns, (pad_start + counts)[None, :] - blk_start[:, None], 0), axis=1)
    blk_valid = jnp.clip(rows_left, 0, EXPERT_BLOCK).astype(I32)
    slot = _slots(pad_start.astype(I32), idx, rank, min(2048, t))
    slot_chunks = slot.reshape(TOP_K, t // SC_CHUNK, SC_CHUNK).transpose(1, 0, 2)
    xs = _sc_dispatch(h2p.reshape(t, d // 2), slot_chunks, n_rows)
    ys = _experts(xs, blk_e, n_used, blk_valid, w_gate, w_up, w_down)
    yg = _sc_gather(ys, slot_chunks, t)
    out = _combine(x1.reshape(t, d), h2, w.T, gt2, yg, sg, su, sd, s, min(256, t))
    return out.reshape(b, s, d), dict(idx=idx, w=w, rank=rank, cnt=cnt)


def kernel(x, c, ada_w, ada_b, norm1_w, norm2_w, w_in, q_norm_w, k_norm_w, cmp_pos, cmp_w1, cmp_b1, cmp_w2, attn_out_norm_w, hgrn_lb_param, rec_out_norm_w, w_out, router_w, router_bias, exp_w_gate, exp_w_up, exp_w_down, shared_w_gate, shared_w_up, shared_w_down):
    assert ada_w.shape[0] == 1, "one layer"
    assert x.shape[0] <= 8 and x.shape[1] % TK == 0 and x.shape[1] >= WINDOW + TQ
    l = 0
    x1, h2, h2p, gt2 = _mixer(x, c, ada_w[l], ada_b[l], norm1_w[l], norm2_w[l], w_in[l], q_norm_w[l], k_norm_w[l],
                         cmp_pos[l], cmp_w1[l], cmp_b1[l], cmp_w2[l], attn_out_norm_w[l], hgrn_lb_param,
                         rec_out_norm_w[l], w_out[l])
    out, _ = _moe_parts(x1, h2, h2p, gt2, router_w[l], router_bias[l], exp_w_gate[l], exp_w_up[l], exp_w_down[l],
                        shared_w_gate[l], shared_w_up[l], shared_w_down[l])
    return out
```

```python
import functools

import numpy as np
import jax
import jax.numpy as jnp
from jax import lax
from jax.experimental import pallas as pl
from jax.experimental.pallas import tpu as pltpu
from jax.experimental.pallas import tpu_sc as plsc

F32 = jnp.float32
BF16 = jnp.bfloat16
I32 = jnp.int32

D_MODEL = 1024
NSA_HEADS = 8
HEAD_DIM = 64
NSA_WIDTH = NSA_HEADS * HEAD_DIM
KV_HEADS = 2
HEADS_PER_KV = NSA_HEADS // KV_HEADS
KV_WIDTH = KV_HEADS * HEAD_DIM
CMP_BLOCK = 32
CMP_STRIDE = 16
CMP_HIDDEN = 256
SEL_BLOCK = 64
N_SELECT = 16
WINDOW = 512
HGRN_HEADS = 4
HGRN_DIM = 128
HGRN_WIDTH = HGRN_HEADS * HGRN_DIM
HGRN_CHUNK = 64
HGRN_SUB = 16
HGRN_LEVELS = ()
HGRN_LEAF = 16
N_EXPERTS = 256
TOP_K = 8
N_GROUPS = 8
GROUP_SIZE = N_EXPERTS // N_GROUPS
TOPK_GROUPS = 4
EXPERT_FF = 256
SHARED_FF = 256
ROUTED_SCALE = 2.5
RMS_EPS = 1e-6
BIG = 1e9
LOG2E = 1.4426950408889634
GATE_PAD = 128
PROJ_COLS = NSA_WIDTH + 6 * KV_WIDTH + GATE_PAD + 4 * HGRN_WIDTH

VMEM_LIMIT = 56 * 1024 * 1024

TQ = 256
TK = 512
EXPERT_BLOCK = 512
HIGHEST = lax.Precision.HIGHEST


def _cparams(*sem):
    return pltpu.CompilerParams(dimension_semantics=sem, vmem_limit_bytes=VMEM_LIMIT)


def _sigmoid(x):
    return 1.0 / (1.0 + jnp.exp(-x))


def _dot_nt(a, b):
    return lax.dot_general(a, b, (((1,), (1,)), ((), ())), preferred_element_type=F32)


def _dot(a, b, **kw):
    return jnp.dot(a, b, preferred_element_type=F32, **kw)


def _split_dot(a_bf16_exact, x):
    hi = x.astype(BF16)
    lo = (x - hi.astype(F32)).astype(BF16)
    return _dot(a_bf16_exact, hi) + _dot(a_bf16_exact, lo)


def _mod_kernel(c_ref, w_ref, b_ref, o_ref):
    c = c_ref[...]
    cond = c * _sigmoid(c)
    o_ref[...] = _dot(cond, w_ref[...], precision=HIGHEST) + b_ref[...]


def _mod(c, ada_w, ada_b):
    b, d = c.shape
    rows = 8
    c_pad = jnp.zeros((rows, d), F32).at[:b].set(c)
    n = ada_w.shape[1]
    out = pl.pallas_call(
        _mod_kernel,
        grid=(n // d,),
        in_specs=[pl.BlockSpec((rows, d), lambda j: (0, 0)),
                  pl.BlockSpec((d, d), lambda j: (0, j)),
                  pl.BlockSpec((1, d), lambda j: (0, j))],
        out_specs=pl.BlockSpec((rows, d), lambda j: (0, j)),
        out_shape=jax.ShapeDtypeStruct((rows, n), F32),
        compiler_params=_cparams("parallel"),
        name="mod",
    )(c_pad, ada_w, ada_b.reshape(1, n))
    return out[:b]


def _head_rms(t, w):
    return t * lax.rsqrt(jnp.mean(t * t, axis=-1, keepdims=True) + RMS_EPS) * w


def _pos_digits(pos):
    lane = lax.broadcasted_iota(I32, pos.shape, 1)
    d0 = (lane == 0) | (lane == 3) | (lane == 6)
    d1 = (lane == 1) | (lane == 4) | (lane == 7)
    d2 = (lane == 2) | (lane == 5) | (lane == 8)
    dig = jnp.where(d0, pos >> 12, jnp.where(d1, (pos >> 6) & 63, jnp.where(d2, pos & 63, 0)))
    return dig.astype(F32)


def _inproj_kernel(x_ref, sc_ref, sh_ref, n1_ref, w_ref, qnw_ref, knw_ref, lbp_ref, qaug_ref,
                   q_ref, kcr_ref, vcr_ref, ks_ref, vst_ref, kw_ref, vwt_ref, gt_ref,
                   hq_ref, hk_ref, hlf_ref, hv_ref, hg_ref):
    x = x_ref[0]
    ms = jnp.mean(x * x, axis=-1, keepdims=True)
    h = x * lax.rsqrt(ms + RMS_EPS) * n1_ref[...] * (1.0 + sc_ref[0]) + sh_ref[0]
    p = _dot(h.astype(BF16), w_ref[...])
    tm = x.shape[0]

    qnw = qnw_ref[...]
    for hd in range(NSA_HEADS):
        t = p[:, hd * HEAD_DIM:(hd + 1) * HEAD_DIM]
        qn = _head_rms(t, qnw) * (HEAD_DIM ** -0.5 * LOG2E)
        qa = jnp.broadcast_to(qaug_ref[hd:hd + 1, :], (tm, HEAD_DIM))
        q_ref[0, hd] = jnp.concatenate([qn, qa], axis=1).astype(BF16)
    kaug = _pos_digits(pl.program_id(1) * tm + lax.broadcasted_iota(I32, (tm, HEAD_DIM), 0))

    o = NSA_WIDTH
    kcr_ref[0] = p[:, o:o + KV_WIDTH]
    vcr_ref[0] = p[:, o + KV_WIDTH:o + 2 * KV_WIDTH]
    ks = p[:, o + 2 * KV_WIDTH:o + 3 * KV_WIDTH]
    vs = p[:, o + 3 * KV_WIDTH:o + 4 * KV_WIDTH]
    kw = p[:, o + 4 * KV_WIDTH:o + 5 * KV_WIDTH]
    vw = p[:, o + 5 * KV_WIDTH:o + 6 * KV_WIDTH]
    for g in range(KV_HEADS):
        sl = slice(g * HEAD_DIM, (g + 1) * HEAD_DIM)
        ks_ref[0, g] = jnp.concatenate([_head_rms(ks[:, sl], knw_ref[1:2, :]), kaug], axis=1).astype(BF16)
        kw_ref[0, g] = jnp.concatenate([_head_rms(kw[:, sl], knw_ref[2:3, :]), kaug], axis=1).astype(BF16)
    vst = vs.T.astype(BF16)
    vwt = vw.T.astype(BF16)
    for g in range(KV_HEADS):
        vst_ref[0, g] = vst[g * HEAD_DIM:(g + 1) * HEAD_DIM, :]
        vwt_ref[0, g] = vwt[g * HEAD_DIM:(g + 1) * HEAD_DIM, :]

    o = NSA_WIDTH + 6 * KV_WIDTH
    gates = _sigmoid(p[:, o:o + GATE_PAD])
    gt_ref[0] = gates.T[:NSA_HEADS * 3, :]

    o = o + GATE_PAD
    hq = p[:, o:o + HGRN_WIDTH]
    hf = p[:, o + HGRN_WIDTH:o + 2 * HGRN_WIDTH]
    hi = p[:, o + 2 * HGRN_WIDTH:o + 3 * HGRN_WIDTH]
    hg = p[:, o + 3 * HGRN_WIDTH:o + 4 * HGRN_WIDTH]
    lbp = lbp_ref[...]
    e = jnp.exp(lbp - jnp.max(lbp, axis=0, keepdims=True))
    lb = e[0:1, :] / jnp.sum(e, axis=0, keepdims=True)
    f = lb + (1.0 - lb) * _sigmoid(hf)
    hq_ref[0] = hq * _sigmoid(hq) * (HGRN_DIM ** -0.5)
    hk_ref[0] = 1.0 - f
    hlf_ref[0] = jnp.log(f)
    hv_ref[0] = hi
    hg_ref[0] = _sigmoid(hg)


def _inproj(x, sc1, sh1, norm1_w, w_cat, q_norm_w, k_norm_w, lb_param, tm):
    b, s, d = x.shape
    row = lambda bi, i: (bi, i, 0)
    per_b = lambda bi, i: (bi, 0, 0)
    fixed2 = lambda bi, i: (0, 0)
    aw = 2 * HEAD_DIM
    rest = np.array([2.0 ** (-8.0 * (i + 1) / NSA_HEADS) for i in range(NSA_HEADS)], np.float64) * LOG2E
    qaug = np.zeros((NSA_HEADS, HEAD_DIM), np.float32)
    for i in range(3):
        term = rest.astype(np.float32).astype(BF16).astype(np.float64)
        rest = rest - term
        for dgt, wgt in enumerate((4096.0, 64.0, 1.0)):
            qaug[:, 3 * i + dgt] = term * wgt
    assert np.all(qaug == qaug.astype(BF16).astype(np.float32))
    out_shape = (
        jax.ShapeDtypeStruct((b, NSA_HEADS, s, aw), BF16),
        jax.ShapeDtypeStruct((b, s, KV_WIDTH), F32),
        jax.ShapeDtypeStruct((b, s, KV_WIDTH), F32),
        jax.ShapeDtypeStruct((b, KV_HEADS, s, aw), BF16),
        jax.ShapeDtypeStruct((b, KV_HEADS, HEAD_DIM, s), BF16),
        jax.ShapeDtypeStruct((b, KV_HEADS, s, aw), BF16),
        jax.ShapeDtypeStruct((b, KV_HEADS, HEAD_DIM, s), BF16),
        jax.ShapeDtypeStruct((b, NSA_HEADS * 3, s), F32),
    ) + tuple(jax.ShapeDtypeStruct((b, s, HGRN_WIDTH), F32) for _ in range(5))
    hm = lambda n, w: pl.BlockSpec((1, n, tm, w), lambda bi, i: (bi, 0, i, 0))
    hmt = lambda n, w: pl.BlockSpec((1, n, w, tm), lambda bi, i: (bi, 0, 0, i))
    out_specs = (
        hm(NSA_HEADS, aw),
        pl.BlockSpec((1, tm, KV_WIDTH), row),
        pl.BlockSpec((1, tm, KV_WIDTH), row),
        hm(KV_HEADS, aw), hmt(KV_HEADS, HEAD_DIM),
        hm(KV_HEADS, aw), hmt(KV_HEADS, HEAD_DIM),
        pl.BlockSpec((1, NSA_HEADS * 3, tm), lambda bi, i: (bi, 0, i)),
    ) + tuple(pl.BlockSpec((1, tm, HGRN_WIDTH), row) for _ in range(5))
    return pl.pallas_call(
        _inproj_kernel,
        grid=(b, s // tm),
        in_specs=[pl.BlockSpec((1, tm, d), row),
                  pl.BlockSpec((1, 1, d), per_b),
                  pl.BlockSpec((1, 1, d), per_b),
                  pl.BlockSpec((1, d), fixed2),
                  pl.BlockSpec((d, PROJ_COLS), fixed2),
                  pl.BlockSpec((1, HEAD_DIM), fixed2),
                  pl.BlockSpec((3, HEAD_DIM), fixed2),
                  pl.BlockSpec(lb_param.shape, fixed2),
                  pl.BlockSpec((NSA_HEADS, HEAD_DIM), fixed2)],
        out_specs=out_specs,
        out_shape=out_shape,
        compiler_params=_cparams("parallel", "parallel"),
        name="inproj",
    )(x, sc1, sh1, norm1_w, w_cat, q_norm_w, k_norm_w, lb_param, jnp.asarray(qaug))


def _gelu_tanh(x):
    return 0.5 * x * (1.0 + jnp.tanh(0.7978845608028654 * (x + 0.044715 * x * x * x)))


def _compress_kernel(kch_ref, vch_ref, pos_ref, wa_ref, wb_ref, b1_ref, w2_ref, knw_ref,
                     kc_ref, vct_ref):
    n = kch_ref.shape[1]
    outs = []
    for br, ch_ref in enumerate((kch_ref, vch_ref)):
        ch = ch_ref[0]
        a = _dot((ch + pos_ref[br, 0:1, :]).astype(BF16), wa_ref[br])
        bm = _dot((ch + pos_ref[br, 1:2, :]).astype(BF16), wb_ref[br])
        pre = a + pltpu.roll(bm, n - 1, 0) + b1_ref[br]
        hid = _gelu_tanh(pre).astype(BF16)
        outs.append([_dot(hid[:, g * CMP_HIDDEN:(g + 1) * CMP_HIDDEN], w2_ref[br]) for g in range(KV_HEADS)])
    end_digits = _pos_digits(lax.broadcasted_iota(I32, (n, HEAD_DIM), 0) * CMP_STRIDE + (CMP_BLOCK - 1))
    for g in range(KV_HEADS):
        kc_ref[0, g] = jnp.concatenate([_head_rms(outs[0][g], knw_ref[0:1, :]), end_digits], axis=1).astype(BF16)
    vct = jnp.concatenate(outs[1], axis=1).T.astype(BF16)
    for g in range(KV_HEADS):
        vct_ref[0, g] = vct[g * HEAD_DIM:(g + 1) * HEAD_DIM, :]


def _compress(kc_raw, vc_raw, cmp_pos, cmp_w1, cmp_b1, cmp_w2, k_norm_w):
    b, s, _ = kc_raw.shape
    n = s // CMP_STRIDE
    half = CMP_STRIDE
    cw = CMP_STRIDE * KV_WIDTH
    kch = kc_raw.reshape(b, n, cw)
    vch = vc_raw.reshape(b, n, cw)
    pos = cmp_pos.reshape(2, 2, half, 1, HEAD_DIM)
    pos = jnp.broadcast_to(pos, (2, 2, half, KV_HEADS, HEAD_DIM)).reshape(2, 2, cw)
    w1 = cmp_w1.reshape(2, 2, half, HEAD_DIM, CMP_HIDDEN)
    eye = jnp.eye(KV_HEADS, dtype=F32)
    wfull = jnp.einsum('rhjdn,gk->rhjgdkn', w1, eye).reshape(2, 2, cw, KV_HEADS * CMP_HIDDEN).astype(BF16)
    b1 = jnp.tile(cmp_b1.reshape(2, 1, CMP_HIDDEN), (1, 1, KV_HEADS))
    fix = lambda r: (lambda bi: (0,) * r)
    return pl.pallas_call(
        _compress_kernel,
        grid=(b,),
        in_specs=[pl.BlockSpec((1, n, cw), lambda bi: (bi, 0, 0)),
                  pl.BlockSpec((1, n, cw), lambda bi: (bi, 0, 0)),
                  pl.BlockSpec((2, 2, cw), fix(3)),
                  pl.BlockSpec((2, cw, KV_HEADS * CMP_HIDDEN), fix(3)),
                  pl.BlockSpec((2, cw, KV_HEADS * CMP_HIDDEN), fix(3)),
                  pl.BlockSpec((2, 1, KV_HEADS * CMP_HIDDEN), fix(3)),
                  pl.BlockSpec((2, CMP_HIDDEN, HEAD_DIM), fix(3)),
                  pl.BlockSpec((3, HEAD_DIM), fix(2))],
        out_specs=(pl.BlockSpec((1, KV_HEADS, n, 2 * HEAD_DIM), lambda bi: (bi, 0, 0, 0)),
                   pl.BlockSpec((1, KV_HEADS, HEAD_DIM, n), lambda bi: (bi, 0, 0, 0))),
        out_shape=(jax.ShapeDtypeStruct((b, KV_HEADS, n, 2 * HEAD_DIM), BF16),
                   jax.ShapeDtypeStruct((b, KV_HEADS, HEAD_DIM, n), BF16)),
        compiler_params=_cparams("parallel"),
        name="compress",
    )(kch, vch, pos, wfull[:, 0], wfull[:, 1], b1, cmp_w2.astype(BF16), k_norm_w)


def _nsa_kernel(q_ref, kc_ref, vct_ref, ks_ref, vst_ref, kw_ref, vwt_ref, gt_ref, cdiff_ref, wdiff_ref,
                ovl_ref, oh_ref, onw_ref, wmask_ref, o_ref, buf_a, buf_b, m_scr, acc_scr, *, n_top):
    q0 = pl.program_id(2) * TQ
    ncols = HEADS_PER_KV * TQ
    q = q_ref[0].reshape(ncols, 2 * HEAD_DIM)
    ns = ovl_ref.shape[0]

    s = jnp.where(cdiff_ref[...] <= q0, _dot_nt(kc_ref[0, 0], q), -jnp.inf)
    m = jnp.max(s, axis=0, keepdims=True)
    m = jnp.where(m == -jnp.inf, 0.0, m)
    e = jnp.exp2(s - m)
    p = e / jnp.maximum(jnp.sum(e, axis=0, keepdims=True), 1e-30)
    o_c = _dot(vct_ref[0, 0], p.astype(BF16))

    psum = p[:, 0:TQ]
    for hh in range(1, HEADS_PER_KV):
        psum = psum + p[:, hh * TQ:(hh + 1) * TQ]
    imp = _split_dot(ovl_ref[...], psum)
    blk = lax.broadcasted_iota(I32, (ns, TQ), 0)
    tq = q0 + lax.broadcasted_iota(I32, (ns, TQ), 1)
    cur = tq >> 6
    forced = (blk == 0) | (blk == cur) | (blk == cur - 1)
    rank = jnp.where(forced, BIG, jnp.where(blk * SEL_BLOCK <= tq, imp, -BIG))

    blkf = blk.astype(F32)

    bias = jnp.full((ns, TQ), -1e30, F32)
    for _ in range(n_top):
        mx = jnp.max(rank, axis=0, keepdims=True)
        first = jnp.min(jnp.where(rank == mx, blkf, float(ns)), axis=0, keepdims=True)
        hit = blkf == first
        rank = jnp.where(hit, -jnp.inf, rank)
        bias = jnp.where(hit, 0.0, bias)

    if ns < 128:
        bias = jnp.concatenate([bias, jnp.zeros((128 - ns, TQ), F32)], axis=0)
    bias_t = bias.T.astype(BF16)
    qq = jnp.concatenate([q, jnp.concatenate([bias_t] * HEADS_PER_KV, axis=0)], axis=1)
    ones_rows = jnp.ones((16, TK), BF16)

    def scores(j):
        k0 = pl.multiple_of(j * TK, TK)
        kk = jnp.concatenate([ks_ref[0, 0, pl.ds(k0, TK), :], oh_ref[pl.ds(k0, TK), :]], axis=1)
        return _dot_nt(kk, qq)

    def consume(buf, j, causal, part):
        sc = buf[...]
        if causal:
            sc = jnp.where(wdiff_ref[0:TK, :] + (q0 - j * TK) >= 0, sc, -1e30)
        k0 = pl.multiple_of(j * TK, TK)
        m_run = m_scr[part]
        m_new = jnp.maximum(m_run, jnp.max(sc, axis=0, keepdims=True))
        ex = jnp.exp2(sc - m_new).astype(BF16)
        va = jnp.concatenate([vst_ref[0, 0, :, pl.ds(k0, TK)], ones_rows], axis=0)
        acc_scr[part] = jnp.exp2(m_run - m_new) * acc_scr[part] + _dot(va, ex)
        m_scr[part] = m_new

    n_past = q0 // TK
    m_scr[...] = jnp.full(m_scr.shape, -1e30, F32)
    acc_scr[...] = jnp.zeros(acc_scr.shape, F32)
    buf_a[...] = scores(0)

    nw = WINDOW + TQ
    start = pl.multiple_of(jnp.maximum(q0 - WINDOW, 0), TQ)
    sw = _dot_nt(kw_ref[0, 0, pl.ds(start, nw), :], q) + wmask_ref[0]
    ew = jnp.exp2(sw - jnp.max(sw, axis=0, keepdims=True))
    vw_aug = jnp.concatenate([vwt_ref[0, 0, :, pl.ds(start, nw)], jnp.ones((16, nw), BF16)], axis=0)
    acc_w = _dot(vw_aug, ew.astype(BF16))
    o_w = acc_w[0:HEAD_DIM, :] / acc_w[HEAD_DIM:HEAD_DIM + 1, :]

    def tiles(first, count):
        for u in range(0, count, 2):
            buf_b[...] = scores(first + u + 1)
            consume(buf_a, first + u, False, 0)
            buf_a[...] = scores(first + u + 2)
            consume(buf_b, first + u + 1, False, 1)
        return 0

    lax.fori_loop(0, n_past // 4, lambda i, _: tiles(4 * i, 4), 0)
    lax.fori_loop(0, (n_past // 2) % 2, lambda i, _: tiles((n_past // 4) * 4, 2), 0)

    @pl.when(n_past % 2 == 1)
    def _():
        buf_b[...] = scores(n_past)
        consume(buf_a, n_past - 1, False, 0)
        consume(buf_b, n_past, True, 1)

    @pl.when(n_past % 2 == 0)
    def _():
        consume(buf_a, n_past, True, 0)

    m_all = jnp.maximum(m_scr[0], m_scr[1])
    acc_s = jnp.exp2(m_scr[0] - m_all) * acc_scr[0] + jnp.exp2(m_scr[1] - m_all) * acc_scr[1]
    o_s = acc_s[0:HEAD_DIM, :] / acc_s[HEAD_DIM:HEAD_DIM + 1, :]

    gt = gt_ref[0, 0]
    outs = []
    for hh in range(HEADS_PER_KV):
        cs = slice(hh * TQ, (hh + 1) * TQ)
        o = (gt[3 * hh:3 * hh + 1, :] * o_c[:, cs] + gt[3 * hh + 1:3 * hh + 2, :] * o_s[:, cs]
             + gt[3 * hh + 2:3 * hh + 3, :] * o_w[:, cs])
        o = o * lax.rsqrt(jnp.mean(o * o, axis=0, keepdims=True) + RMS_EPS) * onw_ref[0, hh]
        outs.append(o)
    o_ref[0] = jnp.concatenate(outs, axis=0).T


def _nsa(q, kc, vct, ks, vst, kw, vwt, gates_t, attn_out_norm_w):
    b, _, s, aw = q.shape
    nc = kc.shape[2]
    ns = s // SEL_BLOCK
    n_top = min(N_SELECT, ns)
    ncols = HEADS_PER_KV * TQ
    nw = WINDOW + TQ
    tl = np.arange(ncols)[None, :] & (TQ - 1)
    cdiff = jnp.asarray((np.arange(nc)[:, None] * CMP_STRIDE + (CMP_BLOCK - 1) - tl).astype(np.int32))
    wdiff_np = (tl - np.arange(nw)[:, None]).astype(np.int32)
    wdiff = jnp.asarray(wdiff_np)
    n_off = WINDOW // TQ + 1
    dist_np = wdiff_np[None] + (np.arange(n_off) * TQ)[:, None, None]
    wmask = jnp.asarray(np.where((dist_np >= 0) & (dist_np < WINDOW), 0.0, -np.inf).astype(np.float32))
    ci = np.arange(nc)[None, :] * CMP_STRIDE
    bj = np.arange(ns)[:, None]
    ovl = ((ci < (bj + 1) * SEL_BLOCK) & (ci + CMP_BLOCK > bj * SEL_BLOCK) & (np.arange(nc)[None, :] < nc - 1))
    ovl = jnp.asarray(ovl.astype(np.float32)).astype(BF16)
    assert ns <= 128
    onehot = (np.arange(s)[:, None] // SEL_BLOCK == np.arange(128)[None, :])
    onehot = jnp.asarray(onehot.astype(np.float32)).astype(BF16)
    onw = jnp.broadcast_to(attn_out_norm_w.reshape(KV_HEADS, HEADS_PER_KV, HEAD_DIM, 1),
                           (KV_HEADS, HEADS_PER_KV, HEAD_DIM, TQ))
    gt = gates_t.reshape(b, KV_HEADS, HEADS_PER_KV * 3, s)
    per_bg = lambda bi, g, i: (bi, g, 0, 0)
    fixed = lambda bi, g, i: (0, 0)
    return pl.pallas_call(
        functools.partial(_nsa_kernel, n_top=n_top),
        grid=(b, KV_HEADS, s // TQ),
        in_specs=[pl.BlockSpec((1, HEADS_PER_KV, TQ, aw), lambda bi, g, i: (bi, g, i, 0)),
                  pl.BlockSpec((1, 1, nc, aw), per_bg),
                  pl.BlockSpec((1, 1, HEAD_DIM, nc), per_bg),
                  pl.BlockSpec((1, 1, s, aw), per_bg),
                  pl.BlockSpec((1, 1, HEAD_DIM, s), per_bg),
                  pl.BlockSpec((1, 1, s, aw), per_bg),
                  pl.BlockSpec((1, 1, HEAD_DIM, s), per_bg),
                  pl.BlockSpec((1, 1, HEADS_PER_KV * 3, TQ), lambda bi, g, i: (bi, g, 0, i)),
                  pl.BlockSpec((nc, ncols), fixed, pipeline_mode=pl.Buffered(1)),
                  pl.BlockSpec((nw, ncols), fixed, pipeline_mode=pl.Buffered(1)),
                  pl.BlockSpec((ns, nc), fixed, pipeline_mode=pl.Buffered(1)),
                  pl.BlockSpec((s, 128), fixed, pipeline_mode=pl.Buffered(1)),
                  pl.BlockSpec((1, HEADS_PER_KV, HEAD_DIM, TQ), lambda bi, g, i: (g, 0, 0, 0)),
                  pl.BlockSpec((1, nw, ncols), lambda bi, g, i: (jnp.minimum(i, n_off - 1), 0, 0))],
        out_specs=pl.BlockSpec((1, TQ, HEADS_PER_KV * HEAD_DIM), lambda bi, g, i: (bi, i, g)),
        out_shape=jax.ShapeDtypeStruct((b, s, NSA_WIDTH), F32),
        scratch_shapes=[pltpu.VMEM((TK, ncols), F32), pltpu.VMEM((TK, ncols), F32),
                        pltpu.VMEM((2, 1, ncols), F32), pltpu.VMEM((2, HEAD_DIM + 16, ncols), F32)],
        compiler_params=_cparams("parallel", "parallel", "arbitrary"),
        name="nsa",
    )(q, kc, vct, ks, vst, kw, vwt, gt, cdiff, wdiff, ovl, onehot, onw, wmask)


def _hgrn_cum_matrix():
    c = HGRN_CHUNK
    t = np.arange(c)
    mats = [(t[None, :] <= t[:, None])]
    for half in HGRN_LEVELS:
        ref = (t & ~(2 * half - 1)) + half - 1
        mats.append(t[None, :] <= ref[:, None])
    return np.concatenate(mats, axis=0).astype(np.float32)


def _hgrn_kernel(q_ref, k_ref, lf_ref, v_ref, g_ref, onw_ref, cm_ref, o_ref, state_scr, *, n_chunks):
    c = HGRN_CHUNK

    @pl.when(pl.program_id(1) == 0)
    def _():
        state_scr[...] = jnp.zeros_like(state_scr)

    ri = lax.broadcasted_iota(I32, (c, c), 0)
    ci = lax.broadcasted_iota(I32, (c, c), 1)
    rsub = ri // HGRN_SUB
    level_masks = [((ri & ~(2 * h - 1)) == (ci & ~(2 * h - 1))) & ((ri & h) != 0) & ((ci & h) == 0)
                   for h in HGRN_LEVELS]
    diag = ri == ci

    def head_chunk(r0, hd, state_t):
        cols = slice(hd * HGRN_DIM, (hd + 1) * HGRN_DIM)
        q = q_ref[0, pl.ds(r0, c), cols]
        k = k_ref[0, pl.ds(r0, c), cols]
        lf = lf_ref[0, pl.ds(r0, c), cols]
        v = v_ref[0, pl.ds(r0, c), cols]
        cm = cm_ref[...]
        l1 = lf.astype(BF16)
        rest = lf - l1.astype(F32)
        l2 = rest.astype(BF16)
        l3 = (rest - l2.astype(F32)).astype(BF16)
        cums = _dot(cm, l1) + _dot(cm, l2) + _dot(cm, l3)
        cum = cums[0:c]
        o = _dot_nt((q * jnp.exp(cum)).astype(BF16), state_t.astype(BF16))
        scores = jnp.where(diag, jnp.sum(q * k, axis=-1, keepdims=True), 0.0)

        def factored(ref, mask, acc):
            qs = q * jnp.exp(jnp.minimum(cum - ref, 0.0))
            kd = k * jnp.exp(jnp.minimum(ref - cum, 0.0))
            return jnp.where(mask, _dot_nt(qs.astype(BF16), kd.astype(BF16)), acc)

        for i in range(1, c // HGRN_SUB):
            scores = factored(cum[i * HGRN_SUB - 1:i * HGRN_SUB, :], (rsub == i) & (ci < i * HGRN_SUB), scores)
        for lv in range(len(HGRN_LEVELS)):
            scores = factored(cums[(lv + 1) * c:(lv + 2) * c], level_masks[lv], scores)
        for d in range(1, HGRN_LEAF):
            ksh = pltpu.roll(k, d, 0)
            csh = pltpu.roll(cum, d, 0)
            w = jnp.sum(q * ksh * jnp.exp(cum - csh), axis=-1, keepdims=True)
            scores = jnp.where((ri - ci == d) & ((ri & (HGRN_LEAF - 1)) >= d), w, scores)
        o = o + _dot(scores.astype(BF16), v.astype(BF16))
        last = cum[c - 1:c, :]
        kd = (k * jnp.exp(last - cum)).astype(BF16)
        state_t = state_t * jnp.exp(last) + _dot(v.T.astype(BF16), kd)
        o = o * g_ref[0, pl.ds(r0, c), cols]
        o = o * lax.rsqrt(jnp.mean(o * o, axis=-1, keepdims=True) + RMS_EPS) * onw_ref[:, cols]
        o_ref[0, pl.ds(r0, c), cols] = o
        return state_t

    def chunk(ck, states):
        r0 = pl.multiple_of(ck * c, c)
        return tuple(head_chunk(r0, hd, states[hd]) for hd in range(HGRN_HEADS))

    states = lax.fori_loop(0, n_chunks, chunk, tuple(state_scr[hd] for hd in range(HGRN_HEADS)))
    for hd in range(HGRN_HEADS):
        state_scr[hd] = states[hd]


def _hgrn(hq, hk, hlf, hv, hg, rec_out_norm_w, rows):
    b, s, _ = hq.shape
    cm = jnp.asarray(_hgrn_cum_matrix()).astype(BF16)
    blk = pl.BlockSpec((1, rows, HGRN_WIDTH), lambda bi, i: (bi, i, 0))
    return pl.pallas_call(
        functools.partial(_hgrn_kernel, n_chunks=rows // HGRN_CHUNK),
        grid=(b, s // rows),
        in_specs=[blk, blk, blk, blk, blk,
                  pl.BlockSpec((1, HGRN_WIDTH), lambda bi, i: (0, 0)),
                  pl.BlockSpec(cm.shape, lambda bi, i: (0, 0))],
        out_specs=blk,
        out_shape=jax.ShapeDtypeStruct((b, s, HGRN_WIDTH), F32),
        scratch_shapes=[pltpu.VMEM((HGRN_HEADS, HGRN_DIM, HGRN_DIM), F32)],
        compiler_params=_cparams("parallel", "arbitrary"),
        name="hgrn",
    )(hq, hk, hlf, hv, hg, rec_out_norm_w.reshape(1, HGRN_WIDTH), cm)


def _outproj_kernel(x_ref, a_ref, r_ref, wa_ref, wr_ref, gt_ref, sc_ref, sh_ref, n2_ref, x1_ref, h2_ref, h2p_ref):
    mixed = _dot(a_ref[0].astype(BF16), wa_ref[...]) + _dot(r_ref[0].astype(BF16), wr_ref[...])
    x1 = x_ref[0] + gt_ref[0] * mixed
    x1_ref[0] = x1
    ms = jnp.mean(x1 * x1, axis=-1, keepdims=True)
    h2 = x1 * lax.rsqrt(ms + RMS_EPS) * n2_ref[...] * (1.0 + sc_ref[0]) + sh_ref[0]
    h2_ref[0] = h2
    h2p_ref[0] = _pack_bf16_pair(h2[:, :D_MODEL // 2], h2[:, D_MODEL // 2:])


def _outproj(x, attn, rec, w_out, gt1, sc2, sh2, norm2_w, tm):
    b, s, d = x.shape
    row = lambda bi, i: (bi, i, 0)
    per_b = lambda bi, i: (bi, 0, 0)
    fixed2 = lambda bi, i: (0, 0)
    w = w_out.astype(BF16)
    return pl.pallas_call(
        _outproj_kernel,
        grid=(b, s // tm),
        in_specs=[pl.BlockSpec((1, tm, d), row),
                  pl.BlockSpec((1, tm, NSA_WIDTH), row),
                  pl.BlockSpec((1, tm, HGRN_WIDTH), row),
                  pl.BlockSpec((NSA_WIDTH, d), fixed2),
                  pl.BlockSpec((HGRN_WIDTH, d), fixed2),
                  pl.BlockSpec((1, 1, d), per_b),
                  pl.BlockSpec((1, 1, d), per_b),
                  pl.BlockSpec((1, 1, d), per_b),
                  pl.BlockSpec((1, d), fixed2)],
        out_specs=(pl.BlockSpec((1, tm, d), row), pl.BlockSpec((1, tm, d), row), pl.BlockSpec((1, tm, d // 2), row)),
        out_shape=(jax.ShapeDtypeStruct((b, s, d), F32), jax.ShapeDtypeStruct((b, s, d), F32),
                   jax.ShapeDtypeStruct((b, s, d // 2), jnp.uint32)),
        compiler_params=_cparams("parallel", "parallel"),
        name="outproj",
    )(x, attn, rec, w[:NSA_WIDTH], w[NSA_WIDTH:], gt1, sc2, sh2, norm2_w)


def _mixer(x, c, ada_w, ada_b, norm1_w, norm2_w, w_in, q_norm_w, k_norm_w, cmp_pos, cmp_w1, cmp_b1, cmp_w2,
           attn_out_norm_w, hgrn_lb_param, rec_out_norm_w, w_out):
    b, s, d = x.shape
    mod = _mod(c, ada_w, ada_b)
    sh1, sc1, gt1, sh2, sc2, gt2 = [m.reshape(b, 1, d) for m in jnp.split(mod, 6, axis=-1)]
    o = NSA_WIDTH + 6 * KV_WIDTH
    w_cat = jnp.concatenate([w_in[:, :o], w_in[:, o:o + NSA_HEADS * 3],
                             jnp.zeros((d, GATE_PAD - NSA_HEADS * 3), w_in.dtype),
                             w_in[:, o + NSA_HEADS * 3:]], axis=1).astype(BF16)
    tm = min(256, s)
    (q, kc_raw, vc_raw, ks, vst, kw, vwt, gates_t, hq, hk, hlf, hv, hg) = _inproj(
        x, sc1, sh1, norm1_w.reshape(1, d), w_cat, q_norm_w.reshape(1, HEAD_DIM), k_norm_w, hgrn_lb_param, tm)
    kc, vct = _compress(kc_raw, vc_raw, cmp_pos, cmp_w1, cmp_b1, cmp_w2, k_norm_w)
    attn = _nsa(q, kc, vct, ks, vst, kw, vwt, gates_t, attn_out_norm_w)
    rec = _hgrn(hq, hk, hlf, hv, hg, rec_out_norm_w, min(512, s))
    x1, h2, h2p = _outproj(x, attn, rec, w_out, gt1, sc2, sh2, norm2_w.reshape(1, d), tm)
    return x1, h2, h2p, gt2


def _router_kernel(h_ref, rwt_ref, bias_ref, tri_ref, ones_ref, idx_ref, w_ref, rank_ref, cnt_ref, carry_scr, *, tr):
    @pl.when(pl.program_id(0) == 0)
    def _():
        carry_scr[...] = jnp.zeros_like(carry_scr)

    h = h_ref[...]
    h_hi = h.astype(BF16)
    h_lo = (h - h_hi.astype(F32)).astype(BF16)
    logits = _dot_nt(rwt_ref[0], h_hi) + _dot_nt(rwt_ref[1], h_hi) + _dot_nt(rwt_ref[0], h_lo)
    scores = _sigmoid(logits)
    biased = scores + bias_ref[...]
    neg = -jnp.inf

    gs = []
    for g in range(N_GROUPS):
        sub = biased[g * GROUP_SIZE:(g + 1) * GROUP_SIZE, :]
        m1 = jnp.max(sub, axis=0, keepdims=True)
        dup = jnp.sum((sub == m1).astype(F32), axis=0, keepdims=True)
        m2 = jnp.max(jnp.where(sub < m1, sub, neg), axis=0, keepdims=True)
        gs.append(m1 + jnp.where(dup >= 2.0, m1, m2))
    parts = []
    for g in range(N_GROUPS):
        beaten = jnp.zeros_like(gs[g])
        for g2 in range(N_GROUPS):
            if g2 != g:
                beats = (gs[g2] >= gs[g]) if g2 < g else (gs[g2] > gs[g])
                beaten = beaten + beats.astype(F32)
        sub = biased[g * GROUP_SIZE:(g + 1) * GROUP_SIZE, :]
        parts.append(jnp.where(beaten < float(TOPK_GROUPS), sub, neg))
    cand = jnp.concatenate(parts, axis=0)

    rowf = lax.broadcasted_iota(I32, (N_EXPERTS, tr), 0).astype(F32)
    idx_rows, w_rows, hits = [], [], []
    multi = jnp.zeros((N_EXPERTS, tr), F32)
    for _ in range(TOP_K):
        mx = jnp.max(cand, axis=0, keepdims=True)
        first = jnp.min(jnp.where(cand == mx, rowf, float(N_EXPERTS)), axis=0, keepdims=True)
        hit = rowf == first
        idx_rows.append(first)
        w_rows.append(jnp.sum(jnp.where(hit, scores, 0.0), axis=0, keepdims=True))
        cand = jnp.where(hit, neg, cand)
        multi = jnp.where(hit, 1.0, multi)
    w = jnp.concatenate(w_rows, axis=0)
    w_ref[...] = w / jnp.sum(w, axis=0, keepdims=True) * ROUTED_SCALE
    idx = jnp.concatenate(idx_rows, axis=0)
    idx_ref[...] = idx.astype(I32)

    carry = carry_scr[...]
    mb = multi.astype(BF16)
    before = _dot(mb, tri_ref[...]) + jnp.concatenate([carry] * (tr // 128), axis=1)
    rank_rows = [jnp.sum(jnp.where(rowf == idx_rows[k], before, 0.0), axis=0, keepdims=True) for k in range(TOP_K)]
    rank_ref[...] = jnp.concatenate(rank_rows, axis=0).astype(I32)
    carry = carry + _dot(mb, ones_ref[...])
    carry_scr[...] = carry
    cnt_ref[...] = carry


def _router(h2, router_w, router_bias, tr):
    t, d = h2.shape
    tri = jnp.asarray(np.triu(np.ones((tr, tr), np.float32), 1)).astype(BF16)
    ones = jnp.ones((tr, 128), BF16)
    tok = pl.BlockSpec((TOP_K, tr), lambda i: (0, i))
    fixed = lambda i: (0, 0)
    rwt = router_w.T
    rwt_hi = rwt.astype(BF16)
    rwt_split = jnp.stack([rwt_hi, (rwt - rwt_hi.astype(F32)).astype(BF16)])
    return pl.pallas_call(
        functools.partial(_router_kernel, tr=tr),
        grid=(t // tr,),
        in_specs=[pl.BlockSpec((tr, d), lambda i: (i, 0)),
                  pl.BlockSpec((2, N_EXPERTS, d), lambda i: (0, 0, 0)),
                  pl.BlockSpec((N_EXPERTS, 1), fixed),
                  pl.BlockSpec((tr, tr), fixed),
                  pl.BlockSpec((tr, 128), fixed)],
        out_specs=(tok, tok, tok, pl.BlockSpec((N_EXPERTS, 128), fixed)),
        out_shape=(jax.ShapeDtypeStruct((TOP_K, t), I32), jax.ShapeDtypeStruct((TOP_K, t), F32),
                   jax.ShapeDtypeStruct((TOP_K, t), I32), jax.ShapeDtypeStruct((N_EXPERTS, 128), F32)),
        scratch_shapes=[pltpu.VMEM((N_EXPERTS, 128), F32)],
        compiler_params=_cparams("arbitrary"),
        name="router",
    )(h2, rwt_split, router_bias.reshape(N_EXPERTS, 1), tri, ones)


def _pack_bf16_pair(a, b):
    ua = lax.bitcast_convert_type(a.astype(BF16).astype(F32), jnp.uint32)
    ub = lax.bitcast_convert_type(b.astype(BF16).astype(F32), jnp.uint32)
    return ua | (ub >> 16)


def _unpack_bf16_pair(w):
    a = lax.bitcast_convert_type(w & jnp.uint32(0xFFFF0000), F32)
    b = lax.bitcast_convert_type(w << 16, F32)
    return a, b


def _slot_kernel(ps_ref, idx_ref, rank_ref, slot_ref):
    idx = idx_ref[...]

    def body(e, acc):
        return jnp.where(idx == e, ps_ref[e], acc)

    slot_ref[...] = lax.fori_loop(0, N_EXPERTS, body, jnp.zeros_like(idx)) + rank_ref[...]


def _slots(pad_start, idx, rank, tt):
    t = idx.shape[1]
    tok = pl.BlockSpec((TOP_K, tt), lambda i, ps: (0, i))
    return pl.pallas_call(
        _slot_kernel,
        grid_spec=pltpu.PrefetchScalarGridSpec(num_scalar_prefetch=1, grid=(t // tt,),
                                               in_specs=[tok, tok], out_specs=tok),
        out_shape=jax.ShapeDtypeStruct((TOP_K, t), I32),
        compiler_params=_cparams("parallel"),
        name="slots",
    )(pad_start, idx, rank)


SC_CORES = 2
SC_SUBCORES = 16
SC_CHUNK = 64


def _sc_mesh():
    return plsc.VectorSubcoreMesh(core_axis_name="c", subcore_axis_name="s")


def _sc_dispatch(h2p, slot_chunks, n_rows):
    t, dw = h2p.shape
    per = slot_chunks.shape[0] // (SC_CORES * SC_SUBCORES)

    def body(h_hbm, slot_hbm, xs_hbm, idx_v, rows_v, sem):
        wid = lax.axis_index("s") * SC_CORES + lax.axis_index("c")

        @pl.loop(0, per)
        def _(c):
            ch = wid * per + c
            pltpu.sync_copy(slot_hbm.at[ch], idx_v)
            pltpu.sync_copy(h_hbm.at[pl.ds(ch * SC_CHUNK, SC_CHUNK)], rows_v)
            copies = [pltpu.async_copy(rows_v, xs_hbm.at[idx_v.at[k]], sem) for k in range(TOP_K)]
            for cp in copies:
                cp.wait()

    return pl.kernel(
        body, out_type=jax.ShapeDtypeStruct((n_rows, dw), h2p.dtype), mesh=_sc_mesh(),
        scratch_types=[pltpu.VMEM((TOP_K, SC_CHUNK), I32), pltpu.VMEM((SC_CHUNK, dw), h2p.dtype),
                       pltpu.SemaphoreType.DMA],
    )(h2p, slot_chunks)


def _sc_gather(ys, slot_chunks, t):
    dw = ys.shape[1]
    per = slot_chunks.shape[0] // (SC_CORES * SC_SUBCORES)

    def body(ys_hbm, slot_hbm, yg_hbm, idx_v, rows_v, gsem, wsem):
        wid = lax.axis_index("s") * SC_CORES + lax.axis_index("c")

        @pl.loop(0, per)
        def _(c):
            ch = wid * per + c
            pltpu.sync_copy(slot_hbm.at[ch], idx_v)
            gathers = [None] * TOP_K
            writes = [None] * TOP_K
            gathers[0] = pltpu.async_copy(ys_hbm.at[idx_v.at[0]], rows_v.at[0], gsem)
            for k in range(TOP_K):
                gathers[k].wait()
                if k + 1 < TOP_K:
                    if k >= 1:
                        writes[k - 1].wait()
                    gathers[k + 1] = pltpu.async_copy(ys_hbm.at[idx_v.at[k + 1]], rows_v.at[(k + 1) % 2], gsem)
                writes[k] = pltpu.async_copy(rows_v.at[k % 2], yg_hbm.at[k, pl.ds(ch * SC_CHUNK, SC_CHUNK)], wsem)
            writes[TOP_K - 2].wait()
            writes[TOP_K - 1].wait()

    return pl.kernel(
        body, out_type=jax.ShapeDtypeStruct((TOP_K, t, dw), ys.dtype), mesh=_sc_mesh(),
        scratch_types=[pltpu.VMEM((TOP_K, SC_CHUNK), I32), pltpu.VMEM((2, SC_CHUNK, dw), ys.dtype),
                       pltpu.SemaphoreType.DMA, pltpu.SemaphoreType.DMA],
    )(ys, slot_chunks)


def _experts_kernel(be_ref, nu_ref, bv_ref, xs_ref, wg_ref, wu_ref, wd_ref, ys_ref):
    i = pl.program_id(0)
    half = D_MODEL // 2

    @pl.when(i < nu_ref[0])
    def _():
        live = lax.broadcasted_iota(I32, xs_ref.shape, 0) < bv_ref[i]
        xa, xb = _unpack_bf16_pair(jnp.where(live, xs_ref[...], jnp.uint32(0)))
        xa, xb = xa.astype(BF16), xb.astype(BF16)
        g = _dot(xa, wg_ref[0, :half].astype(BF16)) + _dot(xb, wg_ref[0, half:].astype(BF16))
        u = _dot(xa, wu_ref[0, :half].astype(BF16)) + _dot(xb, wu_ref[0, half:].astype(BF16))
        act = (g * _sigmoid(g) * u).astype(BF16)
        y = _dot(act, wd_ref[0].astype(BF16))
        ys_ref[...] = _pack_bf16_pair(y[:, :half], y[:, half:])

    @pl.when(i >= nu_ref[0])
    def _():
        ys_ref[...] = jnp.zeros_like(ys_ref)


def _experts(xs, blk_e, n_used, blk_valid, w_gate, w_up, w_down):
    n_rows, dw = xs.shape
    d = w_gate.shape[1]
    nblk = n_rows // EXPERT_BLOCK
    row_map = lambda i, be, nu, bv: (jnp.minimum(i, nu[0] - 1), 0)
    w_map = lambda i, be, nu, bv: (be[i], 0, 0)
    return pl.pallas_call(
        _experts_kernel,
        grid_spec=pltpu.PrefetchScalarGridSpec(
            num_scalar_prefetch=3,
            grid=(nblk,),
            in_specs=[pl.BlockSpec((EXPERT_BLOCK, dw), row_map),
                      pl.BlockSpec((1, d, EXPERT_FF), w_map),
                      pl.BlockSpec((1, d, EXPERT_FF), w_map),
                      pl.BlockSpec((1, EXPERT_FF, d), w_map)],
            out_specs=pl.BlockSpec((EXPERT_BLOCK, dw), lambda i, be, nu, bv: (i, 0))),
        out_shape=jax.ShapeDtypeStruct((n_rows, dw), xs.dtype),
        compiler_params=_cparams("arbitrary"),
        name="experts",
    )(blk_e, n_used, blk_valid, xs, w_gate, w_up, w_down)


def _combine_kernel(x1_ref, h_ref, w_ref, gt_ref, sg_ref, su_ref, sd_ref, yg_ref, o_ref):
    tc = x1_ref.shape[0]
    half = D_MODEL // 2
    hb = h_ref[...].astype(BF16)
    g = _dot(hb, sg_ref[...])
    u = _dot(hb, su_ref[...])
    ffn = _dot((g * _sigmoid(g) * u).astype(BF16), sd_ref[...])

    w = w_ref[...]
    ra = jnp.zeros((tc, half), F32)
    rb = jnp.zeros((tc, half), F32)
    for k in range(TOP_K):
        ya, yb = _unpack_bf16_pair(yg_ref[k])
        ra = ra + w[:, k:k + 1] * ya
        rb = rb + w[:, k:k + 1] * yb
    ffn = ffn + jnp.concatenate([ra, rb], axis=1)
    o_ref[...] = x1_ref[...] + gt_ref[0] * ffn


def _combine(x1, h2, w_tok, gt2, yg, sg, su, sd, seq, tc):
    t, d = x1.shape
    row = lambda i: (i, 0)
    fixed = lambda i: (0, 0)
    return pl.pallas_call(
        _combine_kernel,
        grid=(t // tc,),
        in_specs=[pl.BlockSpec((tc, d), row),
                  pl.BlockSpec((tc, d), row),
                  pl.BlockSpec((tc, TOP_K), row),
                  pl.BlockSpec((1, 1, d), lambda i: ((i * tc) // seq, 0, 0)),
                  pl.BlockSpec((d, SHARED_FF), fixed),
                  pl.BlockSpec((d, SHARED_FF), fixed),
                  pl.BlockSpec((SHARED_FF, d), fixed),
                  pl.BlockSpec((TOP_K, tc, d // 2), lambda i: (0, i, 0))],
        out_specs=pl.BlockSpec((tc, d), row),
        out_shape=jax.ShapeDtypeStruct((t, d), F32),
        compiler_params=_cparams("parallel"),
        name="combine",
    )(x1, h2, w_tok, gt2, sg.astype(BF16), su.astype(BF16), sd.astype(BF16), yg)


def _moe_parts(x1, h2, h2p, gt2, router_w, router_bias, w_gate, w_up, w_down, sg, su, sd):
    b, s, d = x1.shape
    t = b * s
    h2 = h2.reshape(t, d)
    idx, w, rank, cnt = _router(h2, router_w, router_bias, min(256, t))
    counts = cnt[:, 0].astype(I32)
    padded = (counts + EXPERT_BLOCK - 1) // EXPERT_BLOCK * EXPERT_BLOCK
    pad_end = jnp.cumsum(padded)
    pad_start = pad_end - padded
    n_rows = t * TOP_K + N_EXPERTS * EXPERT_BLOCK
    nblk = n_rows // EXPERT_BLOCK
    n_used = (pad_end[-1:] // EXPERT_BLOCK).astype(I32)
    blk_start = jnp.arange(nblk, dtype=I32) * EXPERT_BLOCK
    owns = (pad_start[None, :] <= blk_start[:, None]) & (blk_start[:, None] < pad_end[None, :])
    e_ids = jnp.arange(N_EXPERTS, dtype=I32)[None, :]
    last_e = jnp.max(jnp.where(counts > 0, e_ids[0], 0))
    blk_e = jnp.where(blk_start < pad_end[-1], jnp.sum(jnp.where(owns, e_ids, 0), axis=1), last_e).astype(I32)
    rows_left = jnp.sum(jnp.where(owns, (pad_start + counts)[None, :] - blk_start[:, None], 0), axis=1)
    blk_valid = jnp.clip(rows_left, 0, EXPERT_BLOCK).astype(I32)
    slot = _slots(pad_start.astype(I32), idx, rank, min(2048, t))
    slot_chunks = slot.reshape(TOP_K, t // SC_CHUNK, SC_CHUNK).transpose(1, 0, 2)
    xs = _sc_dispatch(h2p.reshape(t, d // 2), slot_chunks, n_rows)
    ys = _experts(xs, blk_e, n_used, blk_valid, w_gate, w_up, w_down)
    yg = _sc_gather(ys, slot_chunks, t)
    out = _combine(x1.reshape(t, d), h2, w.T, gt2, yg, sg, su, sd, s, min(256, t))
    return out.reshape(b, s, d), dict(idx=idx, w=w, rank=rank, cnt=cnt)


def kernel(x, c, ada_w, ada_b, norm1_w, norm2_w, w_in, q_norm_w, k_norm_w, cmp_pos, cmp_w1, cmp_b1, cmp_w2, attn_out_norm_w, hgrn_lb_param, rec_out_norm_w, w_out, router_w, router_bias, exp_w_gate, exp_w_up, exp_w_down, shared_w_gate, shared_w_up, shared_w_down):
    assert ada_w.shape[0] == 1, "one layer"
    assert x.shape[0] <= 8 and x.shape[1] % TK == 0 and x.shape[1] >= WINDOW + TQ
    l = 0
    x1, h2, h2p, gt2 = _mixer(x, c, ada_w[l], ada_b[l], norm1_w[l], norm2_w[l], w_in[l], q_norm_w[l], k_norm_w[l],
                         cmp_pos[l], cmp_w1[l], cmp_b1[l], cmp_w2[l], attn_out_norm_w[l], hgrn_lb_param,
                         rec_out_norm_w[l], w_out[l])
    out, _ = _moe_parts(x1, h2, h2p, gt2, router_w[l], router_bias[l], exp_w_gate[l], exp_w_up[l], exp_w_down[l],
                        shared_w_gate[l], shared_w_up[l], shared_w_down[l])
    return out
```

```python
import functools

import numpy as np
import jax
import jax.numpy as jnp
from jax import lax
from jax.experimental import pallas as pl
from jax.experimental.pallas import tpu as pltpu
from jax.experimental.pallas import tpu_sc as plsc

F32 = jnp.float32
BF16 = jnp.bfloat16
I32 = jnp.int32

D_MODEL = 1024
NSA_HEADS = 8
HEAD_DIM = 64
NSA_WIDTH = NSA_HEADS * HEAD_DIM
KV_HEADS = 2
HEADS_PER_KV = NSA_HEADS // KV_HEADS
KV_WIDTH = KV_HEADS * HEAD_DIM
CMP_BLOCK = 32
CMP_STRIDE = 16
CMP_HIDDEN = 256
SEL_BLOCK = 64
N_SELECT = 16
WINDOW = 512
HGRN_HEADS = 4
HGRN_DIM = 128
HGRN_WIDTH = HGRN_HEADS * HGRN_DIM
HGRN_CHUNK = 64
HGRN_SUB = 16
HGRN_LEVELS = ()
HGRN_LEAF = 16
N_EXPERTS = 256
TOP_K = 8
N_GROUPS = 8
GROUP_SIZE = N_EXPERTS // N_GROUPS
TOPK_GROUPS = 4
EXPERT_FF = 256
SHARED_FF = 256
ROUTED_SCALE = 2.5
RMS_EPS = 1e-6
BIG = 1e9
LOG2E = 1.4426950408889634
GATE_PAD = 128
PROJ_COLS = NSA_WIDTH + 6 * KV_WIDTH + GATE_PAD + 4 * HGRN_WIDTH

VMEM_LIMIT = 56 * 1024 * 1024

TQ = 256
TK = 512
EXPERT_BLOCK = 512
EXPERT_TAIL = 128
HIGHEST = lax.Precision.HIGHEST


def _cparams(*sem):
    return pltpu.CompilerParams(dimension_semantics=sem, vmem_limit_bytes=VMEM_LIMIT)


def _sigmoid(x):
    return 1.0 / (1.0 + jnp.exp(-x))


def _dot_nt(a, b):
    return lax.dot_general(a, b, (((1,), (1,)), ((), ())), preferred_element_type=F32)


def _dot(a, b, **kw):
    return jnp.dot(a, b, preferred_element_type=F32, **kw)


def _split_dot(a_bf16_exact, x):
    hi = x.astype(BF16)
    lo = (x - hi.astype(F32)).astype(BF16)
    return _dot(a_bf16_exact, hi) + _dot(a_bf16_exact, lo)


def _mod_kernel(c_ref, w_ref, b_ref, o_ref):
    c = c_ref[...]
    cond = c * _sigmoid(c)
    o_ref[...] = _dot(cond, w_ref[...], precision=HIGHEST) + b_ref[...]


def _mod(c, ada_w, ada_b):
    b, d = c.shape
    rows = 8
    c_pad = jnp.zeros((rows, d), F32).at[:b].set(c)
    n = ada_w.shape[1]
    out = pl.pallas_call(
        _mod_kernel,
        grid=(n // d,),
        in_specs=[pl.BlockSpec((rows, d), lambda j: (0, 0)),
                  pl.BlockSpec((d, d), lambda j: (0, j)),
                  pl.BlockSpec((1, d), lambda j: (0, j))],
        out_specs=pl.BlockSpec((rows, d), lambda j: (0, j)),
        out_shape=jax.ShapeDtypeStruct((rows, n), F32),
        compiler_params=_cparams("parallel"),
        name="mod",
    )(c_pad, ada_w, ada_b.reshape(1, n))
    return out[:b]


def _head_rms(t, w):
    return t * lax.rsqrt(jnp.mean(t * t, axis=-1, keepdims=True) + RMS_EPS) * w


def _pos_digits(pos):
    lane = lax.broadcasted_iota(I32, pos.shape, 1)
    d0 = (lane == 0) | (lane == 3) | (lane == 6)
    d1 = (lane == 1) | (lane == 4) | (lane == 7)
    d2 = (lane == 2) | (lane == 5) | (lane == 8)
    dig = jnp.where(d0, pos >> 12, jnp.where(d1, (pos >> 6) & 63, jnp.where(d2, pos & 63, 0)))
    return dig.astype(F32)


def _inproj_kernel(x_ref, sc_ref, sh_ref, n1_ref, w_ref, qnw_ref, knw_ref, lbp_ref, qaug_ref,
                   q_ref, kcr_ref, vcr_ref, ks_ref, vst_ref, kw_ref, vwt_ref, gt_ref,
                   hq_ref, hk_ref, hlf_ref, hv_ref, hg_ref):
    x = x_ref[0]
    ms = jnp.mean(x * x, axis=-1, keepdims=True)
    h = x * lax.rsqrt(ms + RMS_EPS) * n1_ref[...] * (1.0 + sc_ref[0]) + sh_ref[0]
    p = _dot(h.astype(BF16), w_ref[...])
    tm = x.shape[0]

    qnw = qnw_ref[...]
    for hd in range(NSA_HEADS):
        t = p[:, hd * HEAD_DIM:(hd + 1) * HEAD_DIM]
        qn = _head_rms(t, qnw) * (HEAD_DIM ** -0.5 * LOG2E)
        qa = jnp.broadcast_to(qaug_ref[hd:hd + 1, :], (tm, HEAD_DIM))
        q_ref[0, hd] = jnp.concatenate([qn, qa], axis=1).astype(BF16)
    kaug = _pos_digits(pl.program_id(1) * tm + lax.broadcasted_iota(I32, (tm, HEAD_DIM), 0))

    o = NSA_WIDTH
    kcr_ref[0] = p[:, o:o + KV_WIDTH]
    vcr_ref[0] = p[:, o + KV_WIDTH:o + 2 * KV_WIDTH]
    ks = p[:, o + 2 * KV_WIDTH:o + 3 * KV_WIDTH]
    vs = p[:, o + 3 * KV_WIDTH:o + 4 * KV_WIDTH]
    kw = p[:, o + 4 * KV_WIDTH:o + 5 * KV_WIDTH]
    vw = p[:, o + 5 * KV_WIDTH:o + 6 * KV_WIDTH]
    for g in range(KV_HEADS):
        sl = slice(g * HEAD_DIM, (g + 1) * HEAD_DIM)
        ks_ref[0, g] = jnp.concatenate([_head_rms(ks[:, sl], knw_ref[1:2, :]), kaug], axis=1).astype(BF16)
        kw_ref[0, g] = jnp.concatenate([_head_rms(kw[:, sl], knw_ref[2:3, :]), kaug], axis=1).astype(BF16)
    vst = vs.T.astype(BF16)
    vwt = vw.T.astype(BF16)
    for g in range(KV_HEADS):
        vst_ref[0, g] = vst[g * HEAD_DIM:(g + 1) * HEAD_DIM, :]
        vwt_ref[0, g] = vwt[g * HEAD_DIM:(g + 1) * HEAD_DIM, :]

    o = NSA_WIDTH + 6 * KV_WIDTH
    gates = _sigmoid(p[:, o:o + GATE_PAD])
    gt_ref[0] = gates.T[:NSA_HEADS * 3, :]

    o = o + GATE_PAD
    hq = p[:, o:o + HGRN_WIDTH]
    hf = p[:, o + HGRN_WIDTH:o + 2 * HGRN_WIDTH]
    hi = p[:, o + 2 * HGRN_WIDTH:o + 3 * HGRN_WIDTH]
    hg = p[:, o + 3 * HGRN_WIDTH:o + 4 * HGRN_WIDTH]
    lbp = lbp_ref[...]
    e = jnp.exp(lbp - jnp.max(lbp, axis=0, keepdims=True))
    lb = e[0:1, :] / jnp.sum(e, axis=0, keepdims=True)
    f = lb + (1.0 - lb) * _sigmoid(hf)
    hq_ref[0] = hq * _sigmoid(hq) * (HGRN_DIM ** -0.5)
    hk_ref[0] = 1.0 - f
    hlf_ref[0] = jnp.log(f)
    hv_ref[0] = hi
    hg_ref[0] = _sigmoid(hg)


def _inproj(x, sc1, sh1, norm1_w, w_cat, q_norm_w, k_norm_w, lb_param, tm):
    b, s, d = x.shape
    row = lambda bi, i: (bi, i, 0)
    per_b = lambda bi, i: (bi, 0, 0)
    fixed2 = lambda bi, i: (0, 0)
    aw = 2 * HEAD_DIM
    rest = np.array([2.0 ** (-8.0 * (i + 1) / NSA_HEADS) for i in range(NSA_HEADS)], np.float64) * LOG2E
    qaug = np.zeros((NSA_HEADS, HEAD_DIM), np.float32)
    for i in range(3):
        term = rest.astype(np.float32).astype(BF16).astype(np.float64)
        rest = rest - term
        for dgt, wgt in enumerate((4096.0, 64.0, 1.0)):
            qaug[:, 3 * i + dgt] = term * wgt
    assert np.all(qaug == qaug.astype(BF16).astype(np.float32))
    out_shape = (
        jax.ShapeDtypeStruct((b, NSA_HEADS, s, aw), BF16),
        jax.ShapeDtypeStruct((b, s, KV_WIDTH), F32),
        jax.ShapeDtypeStruct((b, s, KV_WIDTH), F32),
        jax.ShapeDtypeStruct((b, KV_HEADS, s, aw), BF16),
        jax.ShapeDtypeStruct((b, KV_HEADS, HEAD_DIM, s), BF16),
        jax.ShapeDtypeStruct((b, KV_HEADS, s, aw), BF16),
        jax.ShapeDtypeStruct((b, KV_HEADS, HEAD_DIM, s), BF16),
        jax.ShapeDtypeStruct((b, NSA_HEADS * 3, s), F32),
    ) + tuple(jax.ShapeDtypeStruct((b, s, HGRN_WIDTH), F32) for _ in range(5))
    hm = lambda n, w: pl.BlockSpec((1, n, tm, w), lambda bi, i: (bi, 0, i, 0))
    hmt = lambda n, w: pl.BlockSpec((1, n, w, tm), lambda bi, i: (bi, 0, 0, i))
    out_specs = (
        hm(NSA_HEADS, aw),
        pl.BlockSpec((1, tm, KV_WIDTH), row),
        pl.BlockSpec((1, tm, KV_WIDTH), row),
        hm(KV_HEADS, aw), hmt(KV_HEADS, HEAD_DIM),
        hm(KV_HEADS, aw), hmt(KV_HEADS, HEAD_DIM),
        pl.BlockSpec((1, NSA_HEADS * 3, tm), lambda bi, i: (bi, 0, i)),
    ) + tuple(pl.BlockSpec((1, tm, HGRN_WIDTH), row) for _ in range(5))
    return pl.pallas_call(
        _inproj_kernel,
        grid=(b, s // tm),
        in_specs=[pl.BlockSpec((1, tm, d), row),
                  pl.BlockSpec((1, 1, d), per_b),
                  pl.BlockSpec((1, 1, d), per_b),
                  pl.BlockSpec((1, d), fixed2),
                  pl.BlockSpec((d, PROJ_COLS), fixed2),
                  pl.BlockSpec((1, HEAD_DIM), fixed2),
                  pl.BlockSpec((3, HEAD_DIM), fixed2),
                  pl.BlockSpec(lb_param.shape, fixed2),
                  pl.BlockSpec((NSA_HEADS, HEAD_DIM), fixed2)],
        out_specs=out_specs,
        out_shape=out_shape,
        compiler_params=_cparams("parallel", "parallel"),
        name="inproj",
    )(x, sc1, sh1, norm1_w, w_cat, q_norm_w, k_norm_w, lb_param, jnp.asarray(qaug))


def _gelu_tanh(x):
    return 0.5 * x * (1.0 + jnp.tanh(0.7978845608028654 * (x + 0.044715 * x * x * x)))


def _compress_kernel(kch_ref, vch_ref, pos_ref, wa_ref, wb_ref, b1_ref, w2_ref, knw_ref,
                     kc_ref, vct_ref):
    n = kch_ref.shape[1]
    outs = []
    for br, ch_ref in enumerate((kch_ref, vch_ref)):
        ch = ch_ref[0]
        a = _dot((ch + pos_ref[br, 0:1, :]).astype(BF16), wa_ref[br])
        bm = _dot((ch + pos_ref[br, 1:2, :]).astype(BF16), wb_ref[br])
        pre = a + pltpu.roll(bm, n - 1, 0) + b1_ref[br]
        hid = _gelu_tanh(pre).astype(BF16)
        outs.append([_dot(hid[:, g * CMP_HIDDEN:(g + 1) * CMP_HIDDEN], w2_ref[br]) for g in range(KV_HEADS)])
    end_digits = _pos_digits(lax.broadcasted_iota(I32, (n, HEAD_DIM), 0) * CMP_STRIDE + (CMP_BLOCK - 1))
    for g in range(KV_HEADS):
        kc_ref[0, g] = jnp.concatenate([_head_rms(outs[0][g], knw_ref[0:1, :]), end_digits], axis=1).astype(BF16)
    vct = jnp.concatenate(outs[1], axis=1).T.astype(BF16)
    for g in range(KV_HEADS):
        vct_ref[0, g] = vct[g * HEAD_DIM:(g + 1) * HEAD_DIM, :]


def _compress(kc_raw, vc_raw, cmp_pos, cmp_w1, cmp_b1, cmp_w2, k_norm_w):
    b, s, _ = kc_raw.shape
    n = s // CMP_STRIDE
    half = CMP_STRIDE
    cw = CMP_STRIDE * KV_WIDTH
    kch = kc_raw.reshape(b, n, cw)
    vch = vc_raw.reshape(b, n, cw)
    pos = cmp_pos.reshape(2, 2, half, 1, HEAD_DIM)
    pos = jnp.broadcast_to(pos, (2, 2, half, KV_HEADS, HEAD_DIM)).reshape(2, 2, cw)
    w1 = cmp_w1.reshape(2, 2, half, HEAD_DIM, CMP_HIDDEN)
    eye = jnp.eye(KV_HEADS, dtype=F32)
    wfull = jnp.einsum('rhjdn,gk->rhjgdkn', w1, eye).reshape(2, 2, cw, KV_HEADS * CMP_HIDDEN).astype(BF16)
    b1 = jnp.tile(cmp_b1.reshape(2, 1, CMP_HIDDEN), (1, 1, KV_HEADS))
    fix = lambda r: (lambda bi: (0,) * r)
    return pl.pallas_call(
        _compress_kernel,
        grid=(b,),
        in_specs=[pl.BlockSpec((1, n, cw), lambda bi: (bi, 0, 0)),
                  pl.BlockSpec((1, n, cw), lambda bi: (bi, 0, 0)),
                  pl.BlockSpec((2, 2, cw), fix(3)),
                  pl.BlockSpec((2, cw, KV_HEADS * CMP_HIDDEN), fix(3)),
                  pl.BlockSpec((2, cw, KV_HEADS * CMP_HIDDEN), fix(3)),
                  pl.BlockSpec((2, 1, KV_HEADS * CMP_HIDDEN), fix(3)),
                  pl.BlockSpec((2, CMP_HIDDEN, HEAD_DIM), fix(3)),
                  pl.BlockSpec((3, HEAD_DIM), fix(2))],
        out_specs=(pl.BlockSpec((1, KV_HEADS, n, 2 * HEAD_DIM), lambda bi: (bi, 0, 0, 0)),
                   pl.BlockSpec((1, KV_HEADS, HEAD_DIM, n), lambda bi: (bi, 0, 0, 0))),
        out_shape=(jax.ShapeDtypeStruct((b, KV_HEADS, n, 2 * HEAD_DIM), BF16),
                   jax.ShapeDtypeStruct((b, KV_HEADS, HEAD_DIM, n), BF16)),
        compiler_params=_cparams("parallel"),
        name="compress",
    )(kch, vch, pos, wfull[:, 0], wfull[:, 1], b1, cmp_w2.astype(BF16), k_norm_w)


def _nsa_kernel(q_ref, kc_ref, vct_ref, ks_ref, vst_ref, kw_ref, vwt_ref, gt_ref, cdiff_ref, wdiff_ref,
                ovl_ref, oh_ref, onw_ref, wmask_ref, o_ref, buf_a, buf_b, m_scr, acc_scr, *, n_top):
    q0 = pl.program_id(2) * TQ
    ncols = HEADS_PER_KV * TQ
    q = q_ref[0].reshape(ncols, 2 * HEAD_DIM)
    ns = ovl_ref.shape[0]

    s = jnp.where(cdiff_ref[...] <= q0, _dot_nt(kc_ref[0, 0], q), -jnp.inf)
    m = jnp.max(s, axis=0, keepdims=True)
    m = jnp.where(m == -jnp.inf, 0.0, m)
    e = jnp.exp2(s - m)
    p = e / jnp.maximum(jnp.sum(e, axis=0, keepdims=True), 1e-30)
    o_c = _dot(vct_ref[0, 0], p.astype(BF16))

    psum = p[:, 0:TQ]
    for hh in range(1, HEADS_PER_KV):
        psum = psum + p[:, hh * TQ:(hh + 1) * TQ]
    imp = _split_dot(ovl_ref[...], psum)
    blk = lax.broadcasted_iota(I32, (ns, TQ), 0)
    tq = q0 + lax.broadcasted_iota(I32, (ns, TQ), 1)
    cur = tq >> 6
    forced = (blk == 0) | (blk == cur) | (blk == cur - 1)
    rank = jnp.where(forced, BIG, jnp.where(blk * SEL_BLOCK <= tq, imp, -BIG))

    blkf = blk.astype(F32)

    bias = jnp.full((ns, TQ), -1e30, F32)
    for _ in range(n_top):
        mx = jnp.max(rank, axis=0, keepdims=True)
        first = jnp.min(jnp.where(rank == mx, blkf, float(ns)), axis=0, keepdims=True)
        hit = blkf == first
        rank = jnp.where(hit, -jnp.inf, rank)
        bias = jnp.where(hit, 0.0, bias)

    if ns < 128:
        bias = jnp.concatenate([bias, jnp.zeros((128 - ns, TQ), F32)], axis=0)
    bias_t = bias.T.astype(BF16)
    qq = jnp.concatenate([q, jnp.concatenate([bias_t] * HEADS_PER_KV, axis=0)], axis=1)
    ones_rows = jnp.ones((16, TK), BF16)

    def scores(j):
        k0 = pl.multiple_of(j * TK, TK)
        kk = jnp.concatenate([ks_ref[0, 0, pl.ds(k0, TK), :], oh_ref[pl.ds(k0, TK), :]], axis=1)
        return _dot_nt(kk, qq)

    def consume(buf, j, causal, part):
        sc = buf[...]
        if causal:
            sc = jnp.where(wdiff_ref[0:TK, :] + (q0 - j * TK) >= 0, sc, -1e30)
        k0 = pl.multiple_of(j * TK, TK)
        m_run = m_scr[part]
        m_new = jnp.maximum(m_run, jnp.max(sc, axis=0, keepdims=True))
        ex = jnp.exp2(sc - m_new).astype(BF16)
        va = jnp.concatenate([vst_ref[0, 0, :, pl.ds(k0, TK)], ones_rows], axis=0)
        acc_scr[part] = jnp.exp2(m_run - m_new) * acc_scr[part] + _dot(va, ex)
        m_scr[part] = m_new

    n_past = q0 // TK
    m_scr[...] = jnp.full(m_scr.shape, -1e30, F32)
    acc_scr[...] = jnp.zeros(acc_scr.shape, F32)
    buf_a[...] = scores(0)

    nw = WINDOW + TQ
    start = pl.multiple_of(jnp.maximum(q0 - WINDOW, 0), TQ)
    sw = _dot_nt(kw_ref[0, 0, pl.ds(start, nw), :], q) + wmask_ref[0]
    ew = jnp.exp2(sw - jnp.max(sw, axis=0, keepdims=True))
    vw_aug = jnp.concatenate([vwt_ref[0, 0, :, pl.ds(start, nw)], jnp.ones((16, nw), BF16)], axis=0)
    acc_w = _dot(vw_aug, ew.astype(BF16))
    o_w = acc_w[0:HEAD_DIM, :] / acc_w[HEAD_DIM:HEAD_DIM + 1, :]

    def tiles(first, count):
        for u in range(0, count, 2):
            buf_b[...] = scores(first + u + 1)
            consume(buf_a, first + u, False, 0)
            buf_a[...] = scores(first + u + 2)
            consume(buf_b, first + u + 1, False, 1)
        return 0

    lax.fori_loop(0, n_past // 4, lambda i, _: tiles(4 * i, 4), 0)
    lax.fori_loop(0, (n_past // 2) % 2, lambda i, _: tiles((n_past // 4) * 4, 2), 0)

    @pl.when(n_past % 2 == 1)
    def _():
        buf_b[...] = scores(n_past)
        consume(buf_a, n_past - 1, False, 0)
        consume(buf_b, n_past, True, 1)

    @pl.when(n_past % 2 == 0)
    def _():
        consume(buf_a, n_past, True, 0)

    m_all = jnp.maximum(m_scr[0], m_scr[1])
    acc_s = jnp.exp2(m_scr[0] - m_all) * acc_scr[0] + jnp.exp2(m_scr[1] - m_all) * acc_scr[1]
    o_s = acc_s[0:HEAD_DIM, :] / acc_s[HEAD_DIM:HEAD_DIM + 1, :]

    gt = gt_ref[0, 0]
    outs = []
    for hh in range(HEADS_PER_KV):
        cs = slice(hh * TQ, (hh + 1) * TQ)
        o = (gt[3 * hh:3 * hh + 1, :] * o_c[:, cs] + gt[3 * hh + 1:3 * hh + 2, :] * o_s[:, cs]
             + gt[3 * hh + 2:3 * hh + 3, :] * o_w[:, cs])
        o = o * lax.rsqrt(jnp.mean(o * o, axis=0, keepdims=True) + RMS_EPS) * onw_ref[0, hh]
        outs.append(o)
    o_ref[0] = jnp.concatenate(outs, axis=0).T


def _nsa(q, kc, vct, ks, vst, kw, vwt, gates_t, attn_out_norm_w):
    b, _, s, aw = q.shape
    nc = kc.shape[2]
    ns = s // SEL_BLOCK
    n_top = min(N_SELECT, ns)
    ncols = HEADS_PER_KV * TQ
    nw = WINDOW + TQ
    tl = np.arange(ncols)[None, :] & (TQ - 1)
    cdiff = jnp.asarray((np.arange(nc)[:, None] * CMP_STRIDE + (CMP_BLOCK - 1) - tl).astype(np.int32))
    wdiff_np = (tl - np.arange(nw)[:, None]).astype(np.int32)
    wdiff = jnp.asarray(wdiff_np)
    n_off = WINDOW // TQ + 1
    dist_np = wdiff_np[None] + (np.arange(n_off) * TQ)[:, None, None]
    wmask = jnp.asarray(np.where((dist_np >= 0) & (dist_np < WINDOW), 0.0, -np.inf).astype(np.float32))
    ci = np.arange(nc)[None, :] * CMP_STRIDE
    bj = np.arange(ns)[:, None]
    ovl = ((ci < (bj + 1) * SEL_BLOCK) & (ci + CMP_BLOCK > bj * SEL_BLOCK) & (np.arange(nc)[None, :] < nc - 1))
    ovl = jnp.asarray(ovl.astype(np.float32)).astype(BF16)
    assert ns <= 128
    onehot = (np.arange(s)[:, None] // SEL_BLOCK == np.arange(128)[None, :])
    onehot = jnp.asarray(onehot.astype(np.float32)).astype(BF16)
    onw = jnp.broadcast_to(attn_out_norm_w.reshape(KV_HEADS, HEADS_PER_KV, HEAD_DIM, 1),
                           (KV_HEADS, HEADS_PER_KV, HEAD_DIM, TQ))
    gt = gates_t.reshape(b, KV_HEADS, HEADS_PER_KV * 3, s)
    per_bg = lambda bi, g, i: (bi, g, 0, 0)
    fixed = lambda bi, g, i: (0, 0)
    return pl.pallas_call(
        functools.partial(_nsa_kernel, n_top=n_top),
        grid=(b, KV_HEADS, s // TQ),
        in_specs=[pl.BlockSpec((1, HEADS_PER_KV, TQ, aw), lambda bi, g, i: (bi, g, i, 0)),
                  pl.BlockSpec((1, 1, nc, aw), per_bg),
                  pl.BlockSpec((1, 1, HEAD_DIM, nc), per_bg),
                  pl.BlockSpec((1, 1, s, aw), per_bg),
                  pl.BlockSpec((1, 1, HEAD_DIM, s), per_bg),
                  pl.BlockSpec((1, 1, s, aw), per_bg),
                  pl.BlockSpec((1, 1, HEAD_DIM, s), per_bg),
                  pl.BlockSpec((1, 1, HEADS_PER_KV * 3, TQ), lambda bi, g, i: (bi, g, 0, i)),
                  pl.BlockSpec((nc, ncols), fixed, pipeline_mode=pl.Buffered(1)),
                  pl.BlockSpec((nw, ncols), fixed, pipeline_mode=pl.Buffered(1)),
                  pl.BlockSpec((ns, nc), fixed, pipeline_mode=pl.Buffered(1)),
                  pl.BlockSpec((s, 128), fixed, pipeline_mode=pl.Buffered(1)),
                  pl.BlockSpec((1, HEADS_PER_KV, HEAD_DIM, TQ), lambda bi, g, i: (g, 0, 0, 0)),
                  pl.BlockSpec((1, nw, ncols), lambda bi, g, i: (jnp.minimum(i, n_off - 1), 0, 0))],
        out_specs=pl.BlockSpec((1, TQ, HEADS_PER_KV * HEAD_DIM), lambda bi, g, i: (bi, i, g)),
        out_shape=jax.ShapeDtypeStruct((b, s, NSA_WIDTH), F32),
        scratch_shapes=[pltpu.VMEM((TK, ncols), F32), pltpu.VMEM((TK, ncols), F32),
                        pltpu.VMEM((2, 1, ncols), F32), pltpu.VMEM((2, HEAD_DIM + 16, ncols), F32)],
        compiler_params=_cparams("parallel", "parallel", "arbitrary"),
        name="nsa",
    )(q, kc, vct, ks, vst, kw, vwt, gt, cdiff, wdiff, ovl, onehot, onw, wmask)


def _hgrn_cum_matrix():
    c = HGRN_CHUNK
    t = np.arange(c)
    mats = [(t[None, :] <= t[:, None])]
    for half in HGRN_LEVELS:
        ref = (t & ~(2 * half - 1)) + half - 1
        mats.append(t[None, :] <= ref[:, None])
    return np.concatenate(mats, axis=0).astype(np.float32)


def _hgrn_kernel(q_ref, k_ref, lf_ref, v_ref, g_ref, onw_ref, cm_ref, o_ref, state_scr, *, n_chunks):
    c = HGRN_CHUNK

    @pl.when(pl.program_id(1) == 0)
    def _():
        state_scr[...] = jnp.zeros_like(state_scr)

    ri = lax.broadcasted_iota(I32, (c, c), 0)
    ci = lax.broadcasted_iota(I32, (c, c), 1)
    rsub = ri // HGRN_SUB
    level_masks = [((ri & ~(2 * h - 1)) == (ci & ~(2 * h - 1))) & ((ri & h) != 0) & ((ci & h) == 0)
                   for h in HGRN_LEVELS]
    diag = ri == ci

    def head_chunk(r0, hd, state_t):
        cols = slice(hd * HGRN_DIM, (hd + 1) * HGRN_DIM)
        q = q_ref[0, pl.ds(r0, c), cols]
        k = k_ref[0, pl.ds(r0, c), cols]
        lf = lf_ref[0, pl.ds(r0, c), cols] * LOG2E
        v = v_ref[0, pl.ds(r0, c), cols]
        cm = cm_ref[...]
        l1 = lf.astype(BF16)
        rest = lf - l1.astype(F32)
        l2 = rest.astype(BF16)
        l3 = (rest - l2.astype(F32)).astype(BF16)
        cums = _dot(cm, l1) + _dot(cm, l2) + _dot(cm, l3)
        cum = cums[0:c]
        o = _dot_nt((q * jnp.exp2(cum)).astype(BF16), state_t.astype(BF16))
        scores = jnp.where(diag, jnp.sum(q * k, axis=-1, keepdims=True), 0.0)

        def factored(ref, mask, acc):
            qs = q * jnp.exp2(jnp.minimum(cum - ref, 0.0))
            kd = k * jnp.exp2(jnp.minimum(ref - cum, 0.0))
            return jnp.where(mask, _dot_nt(qs.astype(BF16), kd.astype(BF16)), acc)

        for i in range(1, c // HGRN_SUB):
            scores = factored(cum[i * HGRN_SUB - 1:i * HGRN_SUB, :], (rsub == i) & (ci < i * HGRN_SUB), scores)
        for lv in range(len(HGRN_LEVELS)):
            scores = factored(cums[(lv + 1) * c:(lv + 2) * c], level_masks[lv], scores)
        for d in range(1, HGRN_LEAF):
            ksh = pltpu.roll(k, d, 0)
            csh = pltpu.roll(cum, d, 0)
            w = jnp.sum(q * ksh * jnp.exp2(cum - csh), axis=-1, keepdims=True)
            scores = jnp.where((ri - ci == d) & ((ri & (HGRN_LEAF - 1)) >= d), w, scores)
        o = o + _dot(scores.astype(BF16), v.astype(BF16))
        last = cum[c - 1:c, :]
        kd = (k * jnp.exp2(last - cum)).astype(BF16)
        state_t = state_t * jnp.exp2(last) + _dot(v.T.astype(BF16), kd)
        o = o * g_ref[0, pl.ds(r0, c), cols]
        o = o * lax.rsqrt(jnp.mean(o * o, axis=-1, keepdims=True) + RMS_EPS) * onw_ref[:, cols]
        o_ref[0, pl.ds(r0, c), cols] = o
        return state_t

    def chunk(ck, states):
        r0 = pl.multiple_of(ck * c, c)
        return tuple(head_chunk(r0, hd, states[hd]) for hd in range(HGRN_HEADS))

    states = lax.fori_loop(0, n_chunks, chunk, tuple(state_scr[hd] for hd in range(HGRN_HEADS)))
    for hd in range(HGRN_HEADS):
        state_scr[hd] = states[hd]


def _hgrn(hq, hk, hlf, hv, hg, rec_out_norm_w, rows):
    b, s, _ = hq.shape
    cm = jnp.asarray(_hgrn_cum_matrix()).astype(BF16)
    blk = pl.BlockSpec((1, rows, HGRN_WIDTH), lambda bi, i: (bi, i, 0))
    return pl.pallas_call(
        functools.partial(_hgrn_kernel, n_chunks=rows // HGRN_CHUNK),
        grid=(b, s // rows),
        in_specs=[blk, blk, blk, blk, blk,
                  pl.BlockSpec((1, HGRN_WIDTH), lambda bi, i: (0, 0)),
                  pl.BlockSpec(cm.shape, lambda bi, i: (0, 0))],
        out_specs=blk,
        out_shape=jax.ShapeDtypeStruct((b, s, HGRN_WIDTH), F32),
        scratch_shapes=[pltpu.VMEM((HGRN_HEADS, HGRN_DIM, HGRN_DIM), F32)],
        compiler_params=_cparams("parallel", "arbitrary"),
        name="hgrn",
    )(hq, hk, hlf, hv, hg, rec_out_norm_w.reshape(1, HGRN_WIDTH), cm)


def _outproj_kernel(x_ref, a_ref, r_ref, wa_ref, wr_ref, gt_ref, sc_ref, sh_ref, n2_ref, x1_ref, h2_ref, h2p_ref):
    mixed = _dot(a_ref[0].astype(BF16), wa_ref[...]) + _dot(r_ref[0].astype(BF16), wr_ref[...])
    x1 = x_ref[0] + gt_ref[0] * mixed
    x1_ref[0] = x1
    ms = jnp.mean(x1 * x1, axis=-1, keepdims=True)
    h2 = x1 * lax.rsqrt(ms + RMS_EPS) * n2_ref[...] * (1.0 + sc_ref[0]) + sh_ref[0]
    h2_ref[0] = h2
    h2p_ref[0] = _pack_bf16_pair(h2[:, :D_MODEL // 2], h2[:, D_MODEL // 2:])


def _outproj(x, attn, rec, w_out, gt1, sc2, sh2, norm2_w, tm):
    b, s, d = x.shape
    row = lambda bi, i: (bi, i, 0)
    per_b = lambda bi, i: (bi, 0, 0)
    fixed2 = lambda bi, i: (0, 0)
    w = w_out.astype(BF16)
    return pl.pallas_call(
        _outproj_kernel,
        grid=(b, s // tm),
        in_specs=[pl.BlockSpec((1, tm, d), row),
                  pl.BlockSpec((1, tm, NSA_WIDTH), row),
                  pl.BlockSpec((1, tm, HGRN_WIDTH), row),
                  pl.BlockSpec((NSA_WIDTH, d), fixed2),
                  pl.BlockSpec((HGRN_WIDTH, d), fixed2),
                  pl.BlockSpec((1, 1, d), per_b),
                  pl.BlockSpec((1, 1, d), per_b),
                  pl.BlockSpec((1, 1, d), per_b),
                  pl.BlockSpec((1, d), fixed2)],
        out_specs=(pl.BlockSpec((1, tm, d), row), pl.BlockSpec((1, tm, d), row), pl.BlockSpec((1, tm, d // 2), row)),
        out_shape=(jax.ShapeDtypeStruct((b, s, d), F32), jax.ShapeDtypeStruct((b, s, d), F32),
                   jax.ShapeDtypeStruct((b, s, d // 2), jnp.uint32)),
        compiler_params=_cparams("parallel", "parallel"),
        name="outproj",
    )(x, attn, rec, w[:NSA_WIDTH], w[NSA_WIDTH:], gt1, sc2, sh2, norm2_w)


def _mixer(x, c, ada_w, ada_b, norm1_w, norm2_w, w_in, q_norm_w, k_norm_w, cmp_pos, cmp_w1, cmp_b1, cmp_w2,
           attn_out_norm_w, hgrn_lb_param, rec_out_norm_w, w_out):
    b, s, d = x.shape
    mod = _mod(c, ada_w, ada_b)
    sh1, sc1, gt1, sh2, sc2, gt2 = [m.reshape(b, 1, d) for m in jnp.split(mod, 6, axis=-1)]
    o = NSA_WIDTH + 6 * KV_WIDTH
    w_cat = jnp.concatenate([w_in[:, :o], w_in[:, o:o + NSA_HEADS * 3],
                             jnp.zeros((d, GATE_PAD - NSA_HEADS * 3), w_in.dtype),
                             w_in[:, o + NSA_HEADS * 3:]], axis=1).astype(BF16)
    tm = min(256, s)
    (q, kc_raw, vc_raw, ks, vst, kw, vwt, gates_t, hq, hk, hlf, hv, hg) = _inproj(
        x, sc1, sh1, norm1_w.reshape(1, d), w_cat, q_norm_w.reshape(1, HEAD_DIM), k_norm_w, hgrn_lb_param, tm)
    kc, vct = _compress(kc_raw, vc_raw, cmp_pos, cmp_w1, cmp_b1, cmp_w2, k_norm_w)
    attn = _nsa(q, kc, vct, ks, vst, kw, vwt, gates_t, attn_out_norm_w)
    rec = _hgrn(hq, hk, hlf, hv, hg, rec_out_norm_w, min(512, s))
    x1, h2, h2p = _outproj(x, attn, rec, w_out, gt1, sc2, sh2, norm2_w.reshape(1, d), tm)
    return x1, h2, h2p, gt2


def _router_kernel(h_ref, rwt_ref, bias_ref, tri_ref, ones_ref, idx_ref, w_ref, rank_ref, cnt_ref, carry_scr, *, tr):
    @pl.when(pl.program_id(0) == 0)
    def _():
        carry_scr[...] = jnp.zeros_like(carry_scr)

    h = h_ref[...]
    h_hi = h.astype(BF16)
    h_lo = (h - h_hi.astype(F32)).astype(BF16)
    logits = _dot_nt(rwt_ref[0], h_hi) + _dot_nt(rwt_ref[1], h_hi) + _dot_nt(rwt_ref[0], h_lo)
    scores = _sigmoid(logits)
    biased = scores + bias_ref[...]
    neg = -jnp.inf

    gs = []
    for g in range(N_GROUPS):
        sub = biased[g * GROUP_SIZE:(g + 1) * GROUP_SIZE, :]
        m1 = jnp.max(sub, axis=0, keepdims=True)
        dup = jnp.sum((sub == m1).astype(F32), axis=0, keepdims=True)
        m2 = jnp.max(jnp.where(sub < m1, sub, neg), axis=0, keepdims=True)
        gs.append(m1 + jnp.where(dup >= 2.0, m1, m2))
    parts = []
    for g in range(N_GROUPS):
        beaten = jnp.zeros_like(gs[g])
        for g2 in range(N_GROUPS):
            if g2 != g:
                beats = (gs[g2] >= gs[g]) if g2 < g else (gs[g2] > gs[g])
                beaten = beaten + beats.astype(F32)
        sub = biased[g * GROUP_SIZE:(g + 1) * GROUP_SIZE, :]
        parts.append(jnp.where(beaten < float(TOPK_GROUPS), sub, neg))
    cand = jnp.concatenate(parts, axis=0)

    rowf = lax.broadcasted_iota(I32, (N_EXPERTS, tr), 0).astype(F32)
    idx_rows, w_rows, hits = [], [], []
    multi = jnp.zeros((N_EXPERTS, tr), F32)
    for _ in range(TOP_K):
        mx = jnp.max(cand, axis=0, keepdims=True)
        first = jnp.min(jnp.where(cand == mx, rowf, float(N_EXPERTS)), axis=0, keepdims=True)
        hit = rowf == first
        idx_rows.append(first)
        w_rows.append(jnp.sum(jnp.where(hit, scores, 0.0), axis=0, keepdims=True))
        cand = jnp.where(hit, neg, cand)
        multi = jnp.where(hit, 1.0, multi)
    w = jnp.concatenate(w_rows, axis=0)
    w_ref[...] = w / jnp.sum(w, axis=0, keepdims=True) * ROUTED_SCALE
    idx = jnp.concatenate(idx_rows, axis=0)
    idx_ref[...] = idx.astype(I32)

    carry = carry_scr[...]
    mb = multi.astype(BF16)
    before = _dot(mb, tri_ref[...]) + jnp.concatenate([carry] * (tr // 128), axis=1)
    rank_rows = [jnp.sum(jnp.where(rowf == idx_rows[k], before, 0.0), axis=0, keepdims=True) for k in range(TOP_K)]
    rank_ref[...] = jnp.concatenate(rank_rows, axis=0).astype(I32)
    carry = carry + _dot(mb, ones_ref[...])
    carry_scr[...] = carry
    cnt_ref[...] = carry


def _router(h2, router_w, router_bias, tr):
    t, d = h2.shape
    tri = jnp.asarray(np.triu(np.ones((tr, tr), np.float32), 1)).astype(BF16)
    ones = jnp.ones((tr, 128), BF16)
    tok = pl.BlockSpec((TOP_K, tr), lambda i: (0, i))
    fixed = lambda i: (0, 0)
    rwt = router_w.T
    rwt_hi = rwt.astype(BF16)
    rwt_split = jnp.stack([rwt_hi, (rwt - rwt_hi.astype(F32)).astype(BF16)])
    return pl.pallas_call(
        functools.partial(_router_kernel, tr=tr),
        grid=(t // tr,),
        in_specs=[pl.BlockSpec((tr, d), lambda i: (i, 0)),
                  pl.BlockSpec((2, N_EXPERTS, d), lambda i: (0, 0, 0)),
                  pl.BlockSpec((N_EXPERTS, 1), fixed),
                  pl.BlockSpec((tr, tr), fixed),
                  pl.BlockSpec((tr, 128), fixed)],
        out_specs=(tok, tok, tok, pl.BlockSpec((N_EXPERTS, 128), fixed)),
        out_shape=(jax.ShapeDtypeStruct((TOP_K, t), I32), jax.ShapeDtypeStruct((TOP_K, t), F32),
                   jax.ShapeDtypeStruct((TOP_K, t), I32), jax.ShapeDtypeStruct((N_EXPERTS, 128), F32)),
        scratch_shapes=[pltpu.VMEM((N_EXPERTS, 128), F32)],
        compiler_params=_cparams("arbitrary"),
        name="router",
    )(h2, rwt_split, router_bias.reshape(N_EXPERTS, 1), tri, ones)


def _pack_bf16_pair(a, b):
    ua = lax.bitcast_convert_type(a.astype(BF16).astype(F32), jnp.uint32)
    ub = lax.bitcast_convert_type(b.astype(BF16).astype(F32), jnp.uint32)
    return ua | (ub >> 16)


def _unpack_bf16_pair(w):
    a = lax.bitcast_convert_type(w & jnp.uint32(0xFFFF0000), F32)
    b = lax.bitcast_convert_type(w << 16, F32)
    return a, b


def _slot_kernel(ps_ref, idx_ref, rank_ref, slot_ref):
    idx = idx_ref[...]

    def body(e, acc):
        return jnp.where(idx == e, ps_ref[e], acc)

    slot_ref[...] = lax.fori_loop(0, N_EXPERTS, body, jnp.zeros_like(idx)) + rank_ref[...]


def _slots(pad_start, idx, rank, tt):
    t = idx.shape[1]
    tok = pl.BlockSpec((TOP_K, tt), lambda i, ps: (0, i))
    return pl.pallas_call(
        _slot_kernel,
        grid_spec=pltpu.PrefetchScalarGridSpec(num_scalar_prefetch=1, grid=(t // tt,),
                                               in_specs=[tok, tok], out_specs=tok),
        out_shape=jax.ShapeDtypeStruct((TOP_K, t), I32),
        compiler_params=_cparams("parallel"),
        name="slots",
    )(pad_start, idx, rank)


SC_CORES = 2
SC_SUBCORES = 16
SC_CHUNK = 64


def _sc_mesh():
    return plsc.VectorSubcoreMesh(core_axis_name="c", subcore_axis_name="s")


def _sc_dispatch(h2p, slot_chunks, n_rows):
    t, dw = h2p.shape
    per = slot_chunks.shape[0] // (SC_CORES * SC_SUBCORES)

    def body(h_hbm, slot_hbm, xs_hbm, idx_v, rows_v, sem):
        wid = lax.axis_index("s") * SC_CORES + lax.axis_index("c")

        @pl.loop(0, per)
        def _(c):
            ch = wid * per + c
            pltpu.sync_copy(slot_hbm.at[ch], idx_v)
            pltpu.sync_copy(h_hbm.at[pl.ds(ch * SC_CHUNK, SC_CHUNK)], rows_v)
            copies = [pltpu.async_copy(rows_v, xs_hbm.at[idx_v.at[k]], sem) for k in range(TOP_K)]
            for cp in copies:
                cp.wait()

    return pl.kernel(
        body, out_type=jax.ShapeDtypeStruct((n_rows, dw), h2p.dtype), mesh=_sc_mesh(),
        scratch_types=[pltpu.VMEM((TOP_K, SC_CHUNK), I32), pltpu.VMEM((SC_CHUNK, dw), h2p.dtype),
                       pltpu.SemaphoreType.DMA],
    )(h2p, slot_chunks)


def _sc_gather(ys, slot_chunks, t):
    dw = ys.shape[1]
    per = slot_chunks.shape[0] // (SC_CORES * SC_SUBCORES)

    def body(ys_hbm, slot_hbm, yg_hbm, idx_v, rows_v, gsem, wsem):
        wid = lax.axis_index("s") * SC_CORES + lax.axis_index("c")

        @pl.loop(0, per)
        def _(c):
            ch = wid * per + c
            pltpu.sync_copy(slot_hbm.at[ch], idx_v)
            gathers = [None] * TOP_K
            writes = [None] * TOP_K
            gathers[0] = pltpu.async_copy(ys_hbm.at[idx_v.at[0]], rows_v.at[0], gsem)
            for k in range(TOP_K):
                gathers[k].wait()
                if k + 1 < TOP_K:
                    if k >= 1:
                        writes[k - 1].wait()
                    gathers[k + 1] = pltpu.async_copy(ys_hbm.at[idx_v.at[k + 1]], rows_v.at[(k + 1) % 2], gsem)
                writes[k] = pltpu.async_copy(rows_v.at[k % 2], yg_hbm.at[k, pl.ds(ch * SC_CHUNK, SC_CHUNK)], wsem)
            writes[TOP_K - 2].wait()
            writes[TOP_K - 1].wait()

    return pl.kernel(
        body, out_type=jax.ShapeDtypeStruct((TOP_K, t, dw), ys.dtype), mesh=_sc_mesh(),
        scratch_types=[pltpu.VMEM((TOP_K, SC_CHUNK), I32), pltpu.VMEM((2, SC_CHUNK, dw), ys.dtype),
                       pltpu.SemaphoreType.DMA, pltpu.SemaphoreType.DMA],
    )(ys, slot_chunks)


def _experts_kernel(be_ref, nu_ref, bv_ref, xs_ref, wg_ref, wu_ref, wd_ref, ys_ref):
    i = pl.program_id(0)
    half = D_MODEL // 2

    def ffn(rows):
        live = lax.broadcasted_iota(I32, (rows, xs_ref.shape[1]), 0) < bv_ref[i]
        xa, xb = _unpack_bf16_pair(jnp.where(live, xs_ref[0:rows, :], jnp.uint32(0)))
        xa, xb = xa.astype(BF16), xb.astype(BF16)
        g = _dot(xa, wg_ref[0, :half].astype(BF16)) + _dot(xb, wg_ref[0, half:].astype(BF16))
        u = _dot(xa, wu_ref[0, :half].astype(BF16)) + _dot(xb, wu_ref[0, half:].astype(BF16))
        act = (g * _sigmoid(g) * u).astype(BF16)
        y = _dot(act, wd_ref[0].astype(BF16))
        ys_ref[0:rows, :] = _pack_bf16_pair(y[:, :half], y[:, half:])

    used = i < nu_ref[0]
    short = bv_ref[i] <= EXPERT_TAIL

    @pl.when(used & jnp.logical_not(short))
    def _():
        ffn(EXPERT_BLOCK)

    @pl.when(used & short)
    def _():
        ffn(EXPERT_TAIL)
        ys_ref[EXPERT_TAIL:, :] = jnp.zeros((EXPERT_BLOCK - EXPERT_TAIL, ys_ref.shape[1]), ys_ref.dtype)

    @pl.when(jnp.logical_not(used))
    def _():
        ys_ref[...] = jnp.zeros_like(ys_ref)


def _experts(xs, blk_e, n_used, blk_valid, w_gate, w_up, w_down):
    n_rows, dw = xs.shape
    d = w_gate.shape[1]
    nblk = n_rows // EXPERT_BLOCK
    row_map = lambda i, be, nu, bv: (jnp.minimum(i, nu[0] - 1), 0)
    w_map = lambda i, be, nu, bv: (be[i], 0, 0)
    return pl.pallas_call(
        _experts_kernel,
        grid_spec=pltpu.PrefetchScalarGridSpec(
            num_scalar_prefetch=3,
            grid=(nblk,),
            in_specs=[pl.BlockSpec((EXPERT_BLOCK, dw), row_map),
                      pl.BlockSpec((1, d, EXPERT_FF), w_map),
                      pl.BlockSpec((1, d, EXPERT_FF), w_map),
                      pl.BlockSpec((1, EXPERT_FF, d), w_map)],
            out_specs=pl.BlockSpec((EXPERT_BLOCK, dw), lambda i, be, nu, bv: (i, 0))),
        out_shape=jax.ShapeDtypeStruct((n_rows, dw), xs.dtype),
        compiler_params=_cparams("arbitrary"),
        name="experts",
    )(blk_e, n_used, blk_valid, xs, w_gate, w_up, w_down)


def _combine_kernel(x1_ref, h_ref, w_ref, gt_ref, sg_ref, su_ref, sd_ref, yg_ref, o_ref):
    tc = x1_ref.shape[0]
    half = D_MODEL // 2
    hb = h_ref[...].astype(BF16)
    g = _dot(hb, sg_ref[...])
    u = _dot(hb, su_ref[...])
    ffn = _dot((g * _sigmoid(g) * u).astype(BF16), sd_ref[...])

    w = w_ref[...]
    ra = jnp.zeros((tc, half), F32)
    rb = jnp.zeros((tc, half), F32)
    for k in range(TOP_K):
        ya, yb = _unpack_bf16_pair(yg_ref[k])
        ra = ra + w[:, k:k + 1] * ya
        rb = rb + w[:, k:k + 1] * yb
    ffn = ffn + jnp.concatenate([ra, rb], axis=1)
    o_ref[...] = x1_ref[...] + gt_ref[0] * ffn


def _combine(x1, h2, w_tok, gt2, yg, sg, su, sd, seq, tc):
    t, d = x1.shape
    row = lambda i: (i, 0)
    fixed = lambda i: (0, 0)
    return pl.pallas_call(
        _combine_kernel,
        grid=(t // tc,),
        in_specs=[pl.BlockSpec((tc, d), row),
                  pl.BlockSpec((tc, d), row),
                  pl.BlockSpec((tc, TOP_K), row),
                  pl.BlockSpec((1, 1, d), lambda i: ((i * tc) // seq, 0, 0)),
                  pl.BlockSpec((d, SHARED_FF), fixed),
                  pl.BlockSpec((d, SHARED_FF), fixed),
                  pl.BlockSpec((SHARED_FF, d), fixed),
                  pl.BlockSpec((TOP_K, tc, d // 2), lambda i: (0, i, 0))],
        out_specs=pl.BlockSpec((tc, d), row),
        out_shape=jax.ShapeDtypeStruct((t, d), F32),
        compiler_params=_cparams("parallel"),
        name="combine",
    )(x1, h2, w_tok, gt2, sg.astype(BF16), su.astype(BF16), sd.astype(BF16), yg)


def _moe_parts(x1, h2, h2p, gt2, router_w, router_bias, w_gate, w_up, w_down, sg, su, sd):
    b, s, d = x1.shape
    t = b * s
    h2 = h2.reshape(t, d)
    idx, w, rank, cnt = _router(h2, router_w, router_bias, min(256, t))
    counts = cnt[:, 0].astype(I32)
    padded = (counts + EXPERT_BLOCK - 1) // EXPERT_BLOCK * EXPERT_BLOCK
    pad_end = jnp.cumsum(padded)
    pad_start = pad_end - padded
    n_rows = t * TOP_K + N_EXPERTS * EXPERT_BLOCK
    nblk = n_rows // EXPERT_BLOCK
    n_used = (pad_end[-1:] // EXPERT_BLOCK).astype(I32)
    blk_start = jnp.arange(nblk, dtype=I32) * EXPERT_BLOCK
    owns = (pad_start[None, :] <= blk_start[:, None]) & (blk_start[:, None] < pad_end[None, :])
    e_ids = jnp.arange(N_EXPERTS, dtype=I32)[None, :]
    last_e = jnp.max(jnp.where(counts > 0, e_ids[0], 0))
    blk_e = jnp.where(blk_start < pad_end[-1], jnp.sum(jnp.where(owns, e_ids, 0), axis=1), last_e).astype(I32)
    rows_left = jnp.sum(jnp.where(owns, (pad_start + counts)[None, :] - blk_start[:, None], 0), axis=1)
    blk_valid = jnp.clip(rows_left, 0, EXPERT_BLOCK).astype(I32)
    slot = _slots(pad_start.astype(I32), idx, rank, min(2048, t))
    slot_chunks = slot.reshape(TOP_K, t // SC_CHUNK, SC_CHUNK).transpose(1, 0, 2)
    xs = _sc_dispatch(h2p.reshape(t, d // 2), slot_chunks, n_rows)
    ys = _experts(xs, blk_e, n_used, blk_valid, w_gate, w_up, w_down)
    yg = _sc_gather(ys, slot_chunks, t)
    out = _combine(x1.reshape(t, d), h2, w.T, gt2, yg, sg, su, sd, s, min(256, t))
    return out.reshape(b, s, d), dict(idx=idx, w=w, rank=rank, cnt=cnt)


def kernel(x, c, ada_w, ada_b, norm1_w, norm2_w, w_in, q_norm_w, k_norm_w, cmp_pos, cmp_w1, cmp_b1, cmp_w2, attn_out_norm_w, hgrn_lb_param, rec_out_norm_w, w_out, router_w, router_bias, exp_w_gate, exp_w_up, exp_w_down, shared_w_gate, shared_w_up, shared_w_down):
    assert ada_w.shape[0] == 1, "one layer"
    assert x.shape[0] <= 8 and x.shape[1] % TK == 0 and x.shape[1] >= WINDOW + TQ
    l = 0
    x1, h2, h2p, gt2 = _mixer(x, c, ada_w[l], ada_b[l], norm1_w[l], norm2_w[l], w_in[l], q_norm_w[l], k_norm_w[l],
                         cmp_pos[l], cmp_w1[l], cmp_b1[l], cmp_w2[l], attn_out_norm_w[l], hgrn_lb_param,
                         rec_out_norm_w[l], w_out[l])
    out, _ = _moe_parts(x1, h2, h2p, gt2, router_w[l], router_bias[l], exp_w_gate[l], exp_w_up[l], exp_w_down[l],
                        shared_w_gate[l], shared_w_up[l], shared_w_down[l])
    return out
```

```python
import functools

import numpy as np
import jax
import jax.numpy as jnp
from jax import lax
from jax.experimental import pallas as pl
from jax.experimental.pallas import tpu as pltpu
from jax.experimental.pallas import tpu_sc as plsc

F32 = jnp.float32
BF16 = jnp.bfloat16
I32 = jnp.int32

D_MODEL = 1024
NSA_HEADS = 8
HEAD_DIM = 64
NSA_WIDTH = NSA_HEADS * HEAD_DIM
KV_HEADS = 2
HEADS_PER_KV = NSA_HEADS // KV_HEADS
KV_WIDTH = KV_HEADS * HEAD_DIM
CMP_BLOCK = 32
CMP_STRIDE = 16
CMP_HIDDEN = 256
SEL_BLOCK = 64
N_SELECT = 16
WINDOW = 512
HGRN_HEADS = 4
HGRN_DIM = 128
HGRN_WIDTH = HGRN_HEADS * HGRN_DIM
HGRN_CHUNK = 64
HGRN_SUB = 16
HGRN_LEVELS = ()
HGRN_LEAF = 16
N_EXPERTS = 256
TOP_K = 8
N_GROUPS = 8
GROUP_SIZE = N_EXPERTS // N_GROUPS
TOPK_GROUPS = 4
EXPERT_FF = 256
SHARED_FF = 256
ROUTED_SCALE = 2.5
RMS_EPS = 1e-6
BIG = 1e9
LOG2E = 1.4426950408889634
GATE_PAD = 128
PROJ_COLS = NSA_WIDTH + 6 * KV_WIDTH + GATE_PAD + 4 * HGRN_WIDTH

VMEM_LIMIT = 56 * 1024 * 1024

TQ = 256
TK = 512
EXPERT_BLOCK = 512
EXPERT_TAIL = 128
HIGHEST = lax.Precision.HIGHEST


def _cparams(*sem):
    return pltpu.CompilerParams(dimension_semantics=sem, vmem_limit_bytes=VMEM_LIMIT)


def _sigmoid(x):
    return 1.0 / (1.0 + jnp.exp(-x))


def _dot_nt(a, b):
    return lax.dot_general(a, b, (((1,), (1,)), ((), ())), preferred_element_type=F32)


def _dot(a, b, **kw):
    return jnp.dot(a, b, preferred_element_type=F32, **kw)


def _split_dot(a_bf16_exact, x):
    hi = x.astype(BF16)
    lo = (x - hi.astype(F32)).astype(BF16)
    return _dot(a_bf16_exact, hi) + _dot(a_bf16_exact, lo)


def _mod_kernel(c_ref, w_ref, b_ref, o_ref):
    c = c_ref[...]
    cond = c * _sigmoid(c)
    o_ref[...] = _dot(cond, w_ref[...], precision=HIGHEST) + b_ref[...]


def _mod(c, ada_w, ada_b):
    b, d = c.shape
    rows = 8
    c_pad = jnp.zeros((rows, d), F32).at[:b].set(c)
    n = ada_w.shape[1]
    out = pl.pallas_call(
        _mod_kernel,
        grid=(n // d,),
        in_specs=[pl.BlockSpec((rows, d), lambda j: (0, 0)),
                  pl.BlockSpec((d, d), lambda j: (0, j)),
                  pl.BlockSpec((1, d), lambda j: (0, j))],
        out_specs=pl.BlockSpec((rows, d), lambda j: (0, j)),
        out_shape=jax.ShapeDtypeStruct((rows, n), F32),
        compiler_params=_cparams("parallel"),
        name="mod",
    )(c_pad, ada_w, ada_b.reshape(1, n))
    return out[:b]


def _head_rms(t, w):
    return t * lax.rsqrt(jnp.mean(t * t, axis=-1, keepdims=True) + RMS_EPS) * w


def _pos_digits(pos):
    lane = lax.broadcasted_iota(I32, pos.shape, 1)
    d0 = (lane == 0) | (lane == 3) | (lane == 6)
    d1 = (lane == 1) | (lane == 4) | (lane == 7)
    d2 = (lane == 2) | (lane == 5) | (lane == 8)
    dig = jnp.where(d0, pos >> 12, jnp.where(d1, (pos >> 6) & 63, jnp.where(d2, pos & 63, 0)))
    return dig.astype(F32)


def _inproj_kernel(x_ref, sc_ref, sh_ref, n1_ref, w_ref, qnw_ref, knw_ref, lbp_ref, qaug_ref,
                   q_ref, kcr_ref, vcr_ref, ks_ref, vst_ref, kw_ref, vwt_ref, gt_ref,
                   hq_ref, hk_ref, hlf_ref, hv_ref, hg_ref):
    x = x_ref[0]
    ms = jnp.mean(x * x, axis=-1, keepdims=True)
    h = x * lax.rsqrt(ms + RMS_EPS) * n1_ref[...] * (1.0 + sc_ref[0]) + sh_ref[0]
    p = _dot(h.astype(BF16), w_ref[...])
    tm = x.shape[0]

    qnw = qnw_ref[...]
    for hd in range(NSA_HEADS):
        t = p[:, hd * HEAD_DIM:(hd + 1) * HEAD_DIM]
        qn = _head_rms(t, qnw) * (HEAD_DIM ** -0.5 * LOG2E)
        qa = jnp.broadcast_to(qaug_ref[hd:hd + 1, :], (tm, HEAD_DIM))
        q_ref[0, hd] = jnp.concatenate([qn, qa], axis=1).astype(BF16)
    kaug = _pos_digits(pl.program_id(1) * tm + lax.broadcasted_iota(I32, (tm, HEAD_DIM), 0))

    o = NSA_WIDTH
    kcr_ref[0] = p[:, o:o + KV_WIDTH]
    vcr_ref[0] = p[:, o + KV_WIDTH:o + 2 * KV_WIDTH]
    ks = p[:, o + 2 * KV_WIDTH:o + 3 * KV_WIDTH]
    vs = p[:, o + 3 * KV_WIDTH:o + 4 * KV_WIDTH]
    kw = p[:, o + 4 * KV_WIDTH:o + 5 * KV_WIDTH]
    vw = p[:, o + 5 * KV_WIDTH:o + 6 * KV_WIDTH]
    for g in range(KV_HEADS):
        sl = slice(g * HEAD_DIM, (g + 1) * HEAD_DIM)
        ks_ref[0, g] = jnp.concatenate([_head_rms(ks[:, sl], knw_ref[1:2, :]), kaug], axis=1).astype(BF16)
        kw_ref[0, g] = jnp.concatenate([_head_rms(kw[:, sl], knw_ref[2:3, :]), kaug], axis=1).astype(BF16)
    vst = vs.T.astype(BF16)
    vwt = vw.T.astype(BF16)
    for g in range(KV_HEADS):
        vst_ref[0, g] = vst[g * HEAD_DIM:(g + 1) * HEAD_DIM, :]
        vwt_ref[0, g] = vwt[g * HEAD_DIM:(g + 1) * HEAD_DIM, :]

    o = NSA_WIDTH + 6 * KV_WIDTH
    gates = _sigmoid(p[:, o:o + GATE_PAD])
    gt_ref[0] = gates.T[:NSA_HEADS * 3, :]

    o = o + GATE_PAD
    hq = p[:, o:o + HGRN_WIDTH]
    hf = p[:, o + HGRN_WIDTH:o + 2 * HGRN_WIDTH]
    hi = p[:, o + 2 * HGRN_WIDTH:o + 3 * HGRN_WIDTH]
    hg = p[:, o + 3 * HGRN_WIDTH:o + 4 * HGRN_WIDTH]
    lbp = lbp_ref[...]
    e = jnp.exp(lbp - jnp.max(lbp, axis=0, keepdims=True))
    lb = e[0:1, :] / jnp.sum(e, axis=0, keepdims=True)
    f = lb + (1.0 - lb) * _sigmoid(hf)
    hq_ref[0] = hq * _sigmoid(hq) * (HGRN_DIM ** -0.5)
    hk_ref[0] = 1.0 - f
    hlf_ref[0] = jnp.log(f)
    hv_ref[0] = hi
    hg_ref[0] = _sigmoid(hg)


def _inproj(x, sc1, sh1, norm1_w, w_cat, q_norm_w, k_norm_w, lb_param, tm):
    b, s, d = x.shape
    row = lambda bi, i: (bi, i, 0)
    per_b = lambda bi, i: (bi, 0, 0)
    fixed2 = lambda bi, i: (0, 0)
    aw = 2 * HEAD_DIM
    rest = np.array([2.0 ** (-8.0 * (i + 1) / NSA_HEADS) for i in range(NSA_HEADS)], np.float64) * LOG2E
    qaug = np.zeros((NSA_HEADS, HEAD_DIM), np.float32)
    for i in range(3):
        term = rest.astype(np.float32).astype(BF16).astype(np.float64)
        rest = rest - term
        for dgt, wgt in enumerate((4096.0, 64.0, 1.0)):
            qaug[:, 3 * i + dgt] = term * wgt
    assert np.all(qaug == qaug.astype(BF16).astype(np.float32))
    out_shape = (
        jax.ShapeDtypeStruct((b, NSA_HEADS, s, aw), BF16),
        jax.ShapeDtypeStruct((b, s, KV_WIDTH), F32),
        jax.ShapeDtypeStruct((b, s, KV_WIDTH), F32),
        jax.ShapeDtypeStruct((b, KV_HEADS, s, aw), BF16),
        jax.ShapeDtypeStruct((b, KV_HEADS, HEAD_DIM, s), BF16),
        jax.ShapeDtypeStruct((b, KV_HEADS, s, aw), BF16),
        jax.ShapeDtypeStruct((b, KV_HEADS, HEAD_DIM, s), BF16),
        jax.ShapeDtypeStruct((b, NSA_HEADS * 3, s), F32),
    ) + tuple(jax.ShapeDtypeStruct((b, s, HGRN_WIDTH), F32) for _ in range(5))
    hm = lambda n, w: pl.BlockSpec((1, n, tm, w), lambda bi, i: (bi, 0, i, 0))
    hmt = lambda n, w: pl.BlockSpec((1, n, w, tm), lambda bi, i: (bi, 0, 0, i))
    out_specs = (
        hm(NSA_HEADS, aw),
        pl.BlockSpec((1, tm, KV_WIDTH), row),
        pl.BlockSpec((1, tm, KV_WIDTH), row),
        hm(KV_HEADS, aw), hmt(KV_HEADS, HEAD_DIM),
        hm(KV_HEADS, aw), hmt(KV_HEADS, HEAD_DIM),
        pl.BlockSpec((1, NSA_HEADS * 3, tm), lambda bi, i: (bi, 0, i)),
    ) + tuple(pl.BlockSpec((1, tm, HGRN_WIDTH), row) for _ in range(5))
    return pl.pallas_call(
        _inproj_kernel,
        grid=(b, s // tm),
        in_specs=[pl.BlockSpec((1, tm, d), row),
                  pl.BlockSpec((1, 1, d), per_b),
                  pl.BlockSpec((1, 1, d), per_b),
                  pl.BlockSpec((1, d), fixed2),
                  pl.BlockSpec((d, PROJ_COLS), fixed2),
                  pl.BlockSpec((1, HEAD_DIM), fixed2),
                  pl.BlockSpec((3, HEAD_DIM), fixed2),
                  pl.BlockSpec(lb_param.shape, fixed2),
                  pl.BlockSpec((NSA_HEADS, HEAD_DIM), fixed2)],
        out_specs=out_specs,
        out_shape=out_shape,
        compiler_params=_cparams("parallel", "parallel"),
        name="inproj",
    )(x, sc1, sh1, norm1_w, w_cat, q_norm_w, k_norm_w, lb_param, jnp.asarray(qaug))


def _gelu_tanh(x):
    return 0.5 * x * (1.0 + jnp.tanh(0.7978845608028654 * (x + 0.044715 * x * x * x)))


def _compress_kernel(kch_ref, vch_ref, pos_ref, wa_ref, wb_ref, b1_ref, w2_ref, knw_ref,
                     kc_ref, vct_ref):
    n = kch_ref.shape[1]
    outs = []
    for br, ch_ref in enumerate((kch_ref, vch_ref)):
        ch = ch_ref[0]
        a = _dot((ch + pos_ref[br, 0:1, :]).astype(BF16), wa_ref[br])
        bm = _dot((ch + pos_ref[br, 1:2, :]).astype(BF16), wb_ref[br])
        pre = a + pltpu.roll(bm, n - 1, 0) + b1_ref[br]
        hid = _gelu_tanh(pre).astype(BF16)
        outs.append([_dot(hid[:, g * CMP_HIDDEN:(g + 1) * CMP_HIDDEN], w2_ref[br]) for g in range(KV_HEADS)])
    end_digits = _pos_digits(lax.broadcasted_iota(I32, (n, HEAD_DIM), 0) * CMP_STRIDE + (CMP_BLOCK - 1))
    for g in range(KV_HEADS):
        kc_ref[0, g] = jnp.concatenate([_head_rms(outs[0][g], knw_ref[0:1, :]), end_digits], axis=1).astype(BF16)
    vct = jnp.concatenate(outs[1], axis=1).T.astype(BF16)
    for g in range(KV_HEADS):
        vct_ref[0, g] = vct[g * HEAD_DIM:(g + 1) * HEAD_DIM, :]


def _compress(kc_raw, vc_raw, cmp_pos, cmp_w1, cmp_b1, cmp_w2, k_norm_w):
    b, s, _ = kc_raw.shape
    n = s // CMP_STRIDE
    half = CMP_STRIDE
    cw = CMP_STRIDE * KV_WIDTH
    kch = kc_raw.reshape(b, n, cw)
    vch = vc_raw.reshape(b, n, cw)
    pos = cmp_pos.reshape(2, 2, half, 1, HEAD_DIM)
    pos = jnp.broadcast_to(pos, (2, 2, half, KV_HEADS, HEAD_DIM)).reshape(2, 2, cw)
    w1 = cmp_w1.reshape(2, 2, half, HEAD_DIM, CMP_HIDDEN)
    eye = jnp.eye(KV_HEADS, dtype=F32)
    wfull = jnp.einsum('rhjdn,gk->rhjgdkn', w1, eye).reshape(2, 2, cw, KV_HEADS * CMP_HIDDEN).astype(BF16)
    b1 = jnp.tile(cmp_b1.reshape(2, 1, CMP_HIDDEN), (1, 1, KV_HEADS))
    fix = lambda r: (lambda bi: (0,) * r)
    return pl.pallas_call(
        _compress_kernel,
        grid=(b,),
        in_specs=[pl.BlockSpec((1, n, cw), lambda bi: (bi, 0, 0)),
                  pl.BlockSpec((1, n, cw), lambda bi: (bi, 0, 0)),
                  pl.BlockSpec((2, 2, cw), fix(3)),
                  pl.BlockSpec((2, cw, KV_HEADS * CMP_HIDDEN), fix(3)),
                  pl.BlockSpec((2, cw, KV_HEADS * CMP_HIDDEN), fix(3)),
                  pl.BlockSpec((2, 1, KV_HEADS * CMP_HIDDEN), fix(3)),
                  pl.BlockSpec((2, CMP_HIDDEN, HEAD_DIM), fix(3)),
                  pl.BlockSpec((3, HEAD_DIM), fix(2))],
        out_specs=(pl.BlockSpec((1, KV_HEADS, n, 2 * HEAD_DIM), lambda bi: (bi, 0, 0, 0)),
                   pl.BlockSpec((1, KV_HEADS, HEAD_DIM, n), lambda bi: (bi, 0, 0, 0))),
        out_shape=(jax.ShapeDtypeStruct((b, KV_HEADS, n, 2 * HEAD_DIM), BF16),
                   jax.ShapeDtypeStruct((b, KV_HEADS, HEAD_DIM, n), BF16)),
        compiler_params=_cparams("parallel"),
        name="compress",
    )(kch, vch, pos, wfull[:, 0], wfull[:, 1], b1, cmp_w2.astype(BF16), k_norm_w)


def _nsa_kernel(q_ref, kc_ref, vct_ref, ks_ref, vst_ref, kw_ref, vwt_ref, gt_ref, cdiff_ref, wdiff_ref,
                ovl_ref, oh_ref, onw_ref, wmask_ref, o_ref, buf_a, buf_b, m_scr, acc_scr, *, n_top):
    q0 = pl.program_id(2) * TQ
    ncols = HEADS_PER_KV * TQ
    q = q_ref[0].reshape(ncols, 2 * HEAD_DIM)
    ns = ovl_ref.shape[0]

    s = jnp.where(cdiff_ref[...] <= q0, _dot_nt(kc_ref[0, 0], q), -jnp.inf)
    m = jnp.max(s, axis=0, keepdims=True)
    m = jnp.where(m == -jnp.inf, 0.0, m)
    e = jnp.exp2(s - m)
    p = e / jnp.maximum(jnp.sum(e, axis=0, keepdims=True), 1e-30)
    o_c = _dot(vct_ref[0, 0], p.astype(BF16))

    psum = p[:, 0:TQ]
    for hh in range(1, HEADS_PER_KV):
        psum = psum + p[:, hh * TQ:(hh + 1) * TQ]
    imp = _split_dot(ovl_ref[...], psum)
    blk = lax.broadcasted_iota(I32, (ns, TQ), 0)
    tq = q0 + lax.broadcasted_iota(I32, (ns, TQ), 1)
    cur = tq >> 6
    forced = (blk == 0) | (blk == cur) | (blk == cur - 1)
    rank = jnp.where(forced, BIG, jnp.where(blk * SEL_BLOCK <= tq, imp, -BIG))

    blkf = blk.astype(F32)

    bias = jnp.full((ns, TQ), -1e30, F32)
    for _ in range(n_top):
        mx = jnp.max(rank, axis=0, keepdims=True)
        first = jnp.min(jnp.where(rank == mx, blkf, float(ns)), axis=0, keepdims=True)
        hit = blkf == first
        rank = jnp.where(hit, -jnp.inf, rank)
        bias = jnp.where(hit, 0.0, bias)

    if ns < 128:
        bias = jnp.concatenate([bias, jnp.zeros((128 - ns, TQ), F32)], axis=0)
    bias_t = bias.T.astype(BF16)
    qq = jnp.concatenate([q, jnp.concatenate([bias_t] * HEADS_PER_KV, axis=0)], axis=1)
    ones_rows = jnp.ones((16, TK), BF16)

    def scores(j):
        k0 = pl.multiple_of(j * TK, TK)
        kk = jnp.concatenate([ks_ref[0, 0, pl.ds(k0, TK), :], oh_ref[pl.ds(k0, TK), :]], axis=1)
        return _dot_nt(kk, qq)

    def consume(buf, j, causal, part):
        sc = buf[...]
        if causal:
            sc = jnp.where(wdiff_ref[0:TK, :] + (q0 - j * TK) >= 0, sc, -1e30)
        k0 = pl.multiple_of(j * TK, TK)
        m_run = m_scr[part]
        m_new = jnp.maximum(m_run, jnp.max(sc, axis=0, keepdims=True))
        ex = jnp.exp2(sc - m_new).astype(BF16)
        va = jnp.concatenate([vst_ref[0, 0, :, pl.ds(k0, TK)], ones_rows], axis=0)
        acc_scr[part] = jnp.exp2(m_run - m_new) * acc_scr[part] + _dot(va, ex)
        m_scr[part] = m_new

    n_past = q0 // TK
    m_scr[...] = jnp.full(m_scr.shape, -1e30, F32)
    acc_scr[...] = jnp.zeros(acc_scr.shape, F32)
    buf_a[...] = scores(0)

    nw = WINDOW + TQ
    start = pl.multiple_of(jnp.maximum(q0 - WINDOW, 0), TQ)
    sw = _dot_nt(kw_ref[0, 0, pl.ds(start, nw), :], q) + wmask_ref[0]
    ew = jnp.exp2(sw - jnp.max(sw, axis=0, keepdims=True))
    vw_aug = jnp.concatenate([vwt_ref[0, 0, :, pl.ds(start, nw)], jnp.ones((16, nw), BF16)], axis=0)
    acc_w = _dot(vw_aug, ew.astype(BF16))
    o_w = acc_w[0:HEAD_DIM, :] / acc_w[HEAD_DIM:HEAD_DIM + 1, :]

    def tiles(first, count):
        for u in range(0, count, 2):
            buf_b[...] = scores(first + u + 1)
            consume(buf_a, first + u, False, 0)
            buf_a[...] = scores(first + u + 2)
            consume(buf_b, first + u + 1, False, 1)
        return 0

    lax.fori_loop(0, n_past // 4, lambda i, _: tiles(4 * i, 4), 0)
    lax.fori_loop(0, (n_past // 2) % 2, lambda i, _: tiles((n_past // 4) * 4, 2), 0)

    @pl.when(n_past % 2 == 1)
    def _():
        buf_b[...] = scores(n_past)
        consume(buf_a, n_past - 1, False, 0)
        consume(buf_b, n_past, True, 1)

    @pl.when(n_past % 2 == 0)
    def _():
        consume(buf_a, n_past, True, 0)

    m_all = jnp.maximum(m_scr[0], m_scr[1])
    acc_s = jnp.exp2(m_scr[0] - m_all) * acc_scr[0] + jnp.exp2(m_scr[1] - m_all) * acc_scr[1]
    o_s = acc_s[0:HEAD_DIM, :] / acc_s[HEAD_DIM:HEAD_DIM + 1, :]

    gt = gt_ref[0, 0]
    outs = []
    for hh in range(HEADS_PER_KV):
        cs = slice(hh * TQ, (hh + 1) * TQ)
        o = (gt[3 * hh:3 * hh + 1, :] * o_c[:, cs] + gt[3 * hh + 1:3 * hh + 2, :] * o_s[:, cs]
             + gt[3 * hh + 2:3 * hh + 3, :] * o_w[:, cs])
        o = o * lax.rsqrt(jnp.mean(o * o, axis=0, keepdims=True) + RMS_EPS) * onw_ref[0, hh]
        outs.append(o)
    o_ref[0] = jnp.concatenate(outs, axis=0).T


def _nsa(q, kc, vct, ks, vst, kw, vwt, gates_t, attn_out_norm_w):
    b, _, s, aw = q.shape
    nc = kc.shape[2]
    ns = s // SEL_BLOCK
    n_top = min(N_SELECT, ns)
    ncols = HEADS_PER_KV * TQ
    nw = WINDOW + TQ
    tl = np.arange(ncols)[None, :] & (TQ - 1)
    cdiff = jnp.asarray((np.arange(nc)[:, None] * CMP_STRIDE + (CMP_BLOCK - 1) - tl).astype(np.int32))
    wdiff_np = (tl - np.arange(nw)[:, None]).astype(np.int32)
    wdiff = jnp.asarray(wdiff_np)
    n_off = WINDOW // TQ + 1
    dist_np = wdiff_np[None] + (np.arange(n_off) * TQ)[:, None, None]
    wmask = jnp.asarray(np.where((dist_np >= 0) & (dist_np < WINDOW), 0.0, -np.inf).astype(np.float32))
    ci = np.arange(nc)[None, :] * CMP_STRIDE
    bj = np.arange(ns)[:, None]
    ovl = ((ci < (bj + 1) * SEL_BLOCK) & (ci + CMP_BLOCK > bj * SEL_BLOCK) & (np.arange(nc)[None, :] < nc - 1))
    ovl = jnp.asarray(ovl.astype(np.float32)).astype(BF16)
    assert ns <= 128
    onehot = (np.arange(s)[:, None] // SEL_BLOCK == np.arange(128)[None, :])
    onehot = jnp.asarray(onehot.astype(np.float32)).astype(BF16)
    onw = jnp.broadcast_to(attn_out_norm_w.reshape(KV_HEADS, HEADS_PER_KV, HEAD_DIM, 1),
                           (KV_HEADS, HEADS_PER_KV, HEAD_DIM, TQ))
    gt = gates_t.reshape(b, KV_HEADS, HEADS_PER_KV * 3, s)
    per_bg = lambda bi, g, i: (bi, g, 0, 0)
    fixed = lambda bi, g, i: (0, 0)
    return pl.pallas_call(
        functools.partial(_nsa_kernel, n_top=n_top),
        grid=(b, KV_HEADS, s // TQ),
        in_specs=[pl.BlockSpec((1, HEADS_PER_KV, TQ, aw), lambda bi, g, i: (bi, g, i, 0)),
                  pl.BlockSpec((1, 1, nc, aw), per_bg),
                  pl.BlockSpec((1, 1, HEAD_DIM, nc), per_bg),
                  pl.BlockSpec((1, 1, s, aw), per_bg),
                  pl.BlockSpec((1, 1, HEAD_DIM, s), per_bg),
                  pl.BlockSpec((1, 1, s, aw), per_bg),
                  pl.BlockSpec((1, 1, HEAD_DIM, s), per_bg),
                  pl.BlockSpec((1, 1, HEADS_PER_KV * 3, TQ), lambda bi, g, i: (bi, g, 0, i)),
                  pl.BlockSpec((nc, ncols), fixed, pipeline_mode=pl.Buffered(1)),
                  pl.BlockSpec((nw, ncols), fixed, pipeline_mode=pl.Buffered(1)),
                  pl.BlockSpec((ns, nc), fixed, pipeline_mode=pl.Buffered(1)),
                  pl.BlockSpec((s, 128), fixed, pipeline_mode=pl.Buffered(1)),
                  pl.BlockSpec((1, HEADS_PER_KV, HEAD_DIM, TQ), lambda bi, g, i: (g, 0, 0, 0)),
                  pl.BlockSpec((1, nw, ncols), lambda bi, g, i: (jnp.minimum(i, n_off - 1), 0, 0))],
        out_specs=pl.BlockSpec((1, TQ, HEADS_PER_KV * HEAD_DIM), lambda bi, g, i: (bi, i, g)),
        out_shape=jax.ShapeDtypeStruct((b, s, NSA_WIDTH), F32),
        scratch_shapes=[pltpu.VMEM((TK, ncols), F32), pltpu.VMEM((TK, ncols), F32),
                        pltpu.VMEM((2, 1, ncols), F32), pltpu.VMEM((2, HEAD_DIM + 16, ncols), F32)],
        compiler_params=_cparams("parallel", "parallel", "arbitrary"),
        name="nsa",
    )(q, kc, vct, ks, vst, kw, vwt, gt, cdiff, wdiff, ovl, onehot, onw, wmask)


def _hgrn_cum_matrix():
    c = HGRN_CHUNK
    t = np.arange(c)
    mats = [(t[None, :] <= t[:, None])]
    for half in HGRN_LEVELS:
        ref = (t & ~(2 * half - 1)) + half - 1
        mats.append(t[None, :] <= ref[:, None])
    return np.concatenate(mats, axis=0).astype(np.float32)


def _hgrn_kernel(q_ref, k_ref, lf_ref, v_ref, g_ref, onw_ref, cm_ref, o_ref, state_scr, *, n_chunks):
    c = HGRN_CHUNK

    @pl.when(pl.program_id(1) == 0)
    def _():
        state_scr[...] = jnp.zeros_like(state_scr)

    ri = lax.broadcasted_iota(I32, (c, c), 0)
    ci = lax.broadcasted_iota(I32, (c, c), 1)
    rsub = ri // HGRN_SUB
    level_masks = [((ri & ~(2 * h - 1)) == (ci & ~(2 * h - 1))) & ((ri & h) != 0) & ((ci & h) == 0)
                   for h in HGRN_LEVELS]
    diag = ri == ci

    def head_chunk(r0, hd, state_t):
        cols = slice(hd * HGRN_DIM, (hd + 1) * HGRN_DIM)
        q = q_ref[0, pl.ds(r0, c), cols]
        k = k_ref[0, pl.ds(r0, c), cols]
        lf = lf_ref[0, pl.ds(r0, c), cols] * LOG2E
        v = v_ref[0, pl.ds(r0, c), cols]
        cm = cm_ref[...]
        l1 = lf.astype(BF16)
        rest = lf - l1.astype(F32)
        l2 = rest.astype(BF16)
        l3 = (rest - l2.astype(F32)).astype(BF16)
        cums = _dot(cm, l1) + _dot(cm, l2) + _dot(cm, l3)
        cum = cums[0:c]
        o = _dot_nt((q * jnp.exp2(cum)).astype(BF16), state_t.astype(BF16))
        scores = jnp.where(diag, jnp.sum(q * k, axis=-1, keepdims=True), 0.0)

        def factored(ref, mask, acc):
            qs = q * jnp.exp2(jnp.minimum(cum - ref, 0.0))
            kd = k * jnp.exp2(jnp.minimum(ref - cum, 0.0))
            return jnp.where(mask, _dot_nt(qs.astype(BF16), kd.astype(BF16)), acc)

        for i in range(1, c // HGRN_SUB):
            scores = factored(cum[i * HGRN_SUB - 1:i * HGRN_SUB, :], (rsub == i) & (ci < i * HGRN_SUB), scores)
        for lv in range(len(HGRN_LEVELS)):
            scores = factored(cums[(lv + 1) * c:(lv + 2) * c], level_masks[lv], scores)
        for d in range(1, HGRN_LEAF):
            ksh = pltpu.roll(k, d, 0)
            csh = pltpu.roll(cum, d, 0)
            w = jnp.sum(q * ksh * jnp.exp2(cum - csh), axis=-1, keepdims=True)
            scores = jnp.where((ri - ci == d) & ((ri & (HGRN_LEAF - 1)) >= d), w, scores)
        o = o + _dot(scores.astype(BF16), v.astype(BF16))
        last = cum[c - 1:c, :]
        kd = (k * jnp.exp2(last - cum)).astype(BF16)
        state_t = state_t * jnp.exp2(last) + _dot(v.T.astype(BF16), kd)
        o = o * g_ref[0, pl.ds(r0, c), cols]
        o = o * lax.rsqrt(jnp.mean(o * o, axis=-1, keepdims=True) + RMS_EPS) * onw_ref[:, cols]
        o_ref[0, pl.ds(r0, c), cols] = o
        return state_t

    def chunk(ck, states):
        r0 = pl.multiple_of(ck * c, c)
        return tuple(head_chunk(r0, hd, states[hd]) for hd in range(HGRN_HEADS))

    states = lax.fori_loop(0, n_chunks, chunk, tuple(state_scr[hd] for hd in range(HGRN_HEADS)))
    for hd in range(HGRN_HEADS):
        state_scr[hd] = states[hd]


def _hgrn(hq, hk, hlf, hv, hg, rec_out_norm_w, rows):
    b, s, _ = hq.shape
    cm = jnp.asarray(_hgrn_cum_matrix()).astype(BF16)
    blk = pl.BlockSpec((1, rows, HGRN_WIDTH), lambda bi, i: (bi, i, 0))
    return pl.pallas_call(
        functools.partial(_hgrn_kernel, n_chunks=rows // HGRN_CHUNK),
        grid=(b, s // rows),
        in_specs=[blk, blk, blk, blk, blk,
                  pl.BlockSpec((1, HGRN_WIDTH), lambda bi, i: (0, 0)),
                  pl.BlockSpec(cm.shape, lambda bi, i: (0, 0))],
        out_specs=blk,
        out_shape=jax.ShapeDtypeStruct((b, s, HGRN_WIDTH), F32),
        scratch_shapes=[pltpu.VMEM((HGRN_HEADS, HGRN_DIM, HGRN_DIM), F32)],
        compiler_params=_cparams("parallel", "arbitrary"),
        name="hgrn",
    )(hq, hk, hlf, hv, hg, rec_out_norm_w.reshape(1, HGRN_WIDTH), cm)


def _outproj_kernel(x_ref, a_ref, r_ref, wa_ref, wr_ref, gt_ref, sc_ref, sh_ref, n2_ref, x1_ref, h2_ref, h2p_ref):
    mixed = _dot(a_ref[0].astype(BF16), wa_ref[...]) + _dot(r_ref[0].astype(BF16), wr_ref[...])
    x1 = x_ref[0] + gt_ref[0] * mixed
    x1_ref[0] = x1
    ms = jnp.mean(x1 * x1, axis=-1, keepdims=True)
    h2 = x1 * lax.rsqrt(ms + RMS_EPS) * n2_ref[...] * (1.0 + sc_ref[0]) + sh_ref[0]
    h2_ref[0] = h2
    h2p_ref[0] = _pack_bf16_pair(h2[:, :D_MODEL // 2], h2[:, D_MODEL // 2:])


def _outproj(x, attn, rec, w_out, gt1, sc2, sh2, norm2_w, tm):
    b, s, d = x.shape
    row = lambda bi, i: (bi, i, 0)
    per_b = lambda bi, i: (bi, 0, 0)
    fixed2 = lambda bi, i: (0, 0)
    w = w_out.astype(BF16)
    return pl.pallas_call(
        _outproj_kernel,
        grid=(b, s // tm),
        in_specs=[pl.BlockSpec((1, tm, d), row),
                  pl.BlockSpec((1, tm, NSA_WIDTH), row),
                  pl.BlockSpec((1, tm, HGRN_WIDTH), row),
                  pl.BlockSpec((NSA_WIDTH, d), fixed2),
                  pl.BlockSpec((HGRN_WIDTH, d), fixed2),
                  pl.BlockSpec((1, 1, d), per_b),
                  pl.BlockSpec((1, 1, d), per_b),
                  pl.BlockSpec((1, 1, d), per_b),
                  pl.BlockSpec((1, d), fixed2)],
        out_specs=(pl.BlockSpec((1, tm, d), row), pl.BlockSpec((1, tm, d), row), pl.BlockSpec((1, tm, d // 2), row)),
        out_shape=(jax.ShapeDtypeStruct((b, s, d), F32), jax.ShapeDtypeStruct((b, s, d), F32),
                   jax.ShapeDtypeStruct((b, s, d // 2), jnp.uint32)),
        compiler_params=_cparams("parallel", "parallel"),
        name="outproj",
    )(x, attn, rec, w[:NSA_WIDTH], w[NSA_WIDTH:], gt1, sc2, sh2, norm2_w)


def _mixer(x, c, ada_w, ada_b, norm1_w, norm2_w, w_in, q_norm_w, k_norm_w, cmp_pos, cmp_w1, cmp_b1, cmp_w2,
           attn_out_norm_w, hgrn_lb_param, rec_out_norm_w, w_out):
    b, s, d = x.shape
    mod = _mod(c, ada_w, ada_b)
    sh1, sc1, gt1, sh2, sc2, gt2 = [m.reshape(b, 1, d) for m in jnp.split(mod, 6, axis=-1)]
    o = NSA_WIDTH + 6 * KV_WIDTH
    w_cat = jnp.concatenate([w_in[:, :o], w_in[:, o:o + NSA_HEADS * 3],
                             jnp.zeros((d, GATE_PAD - NSA_HEADS * 3), w_in.dtype),
                             w_in[:, o + NSA_HEADS * 3:]], axis=1).astype(BF16)
    tm = min(256, s)
    (q, kc_raw, vc_raw, ks, vst, kw, vwt, gates_t, hq, hk, hlf, hv, hg) = _inproj(
        x, sc1, sh1, norm1_w.reshape(1, d), w_cat, q_norm_w.reshape(1, HEAD_DIM), k_norm_w, hgrn_lb_param, tm)
    kc, vct = _compress(kc_raw, vc_raw, cmp_pos, cmp_w1, cmp_b1, cmp_w2, k_norm_w)
    attn = _nsa(q, kc, vct, ks, vst, kw, vwt, gates_t, attn_out_norm_w)
    rec = _hgrn(hq, hk, hlf, hv, hg, rec_out_norm_w, min(512, s))
    x1, h2, h2p = _outproj(x, attn, rec, w_out, gt1, sc2, sh2, norm2_w.reshape(1, d), tm)
    return x1, h2, h2p, gt2


def _router_kernel(h_ref, rwt_ref, bias_ref, tri_ref, ones_ref, idx_ref, w_ref, rank_ref, cnt_ref, carry_scr, *, tr):
    @pl.when(pl.program_id(0) == 0)
    def _():
        carry_scr[...] = jnp.zeros_like(carry_scr)

    h = h_ref[...]
    h_hi = h.astype(BF16)
    h_lo = (h - h_hi.astype(F32)).astype(BF16)
    logits = _dot_nt(rwt_ref[0], h_hi) + _dot_nt(rwt_ref[1], h_hi) + _dot_nt(rwt_ref[0], h_lo)
    scores = _sigmoid(logits)
    biased = scores + bias_ref[...]
    neg = -jnp.inf

    gs = []
    for g in range(N_GROUPS):
        sub = biased[g * GROUP_SIZE:(g + 1) * GROUP_SIZE, :]
        m1 = jnp.max(sub, axis=0, keepdims=True)
        dup = jnp.sum((sub == m1).astype(F32), axis=0, keepdims=True)
        m2 = jnp.max(jnp.where(sub < m1, sub, neg), axis=0, keepdims=True)
        gs.append(m1 + jnp.where(dup >= 2.0, m1, m2))
    parts = []
    for g in range(N_GROUPS):
        beaten = jnp.zeros_like(gs[g])
        for g2 in range(N_GROUPS):
            if g2 != g:
                beats = (gs[g2] >= gs[g]) if g2 < g else (gs[g2] > gs[g])
                beaten = beaten + beats.astype(F32)
        sub = biased[g * GROUP_SIZE:(g + 1) * GROUP_SIZE, :]
        parts.append(jnp.where(beaten < float(TOPK_GROUPS), sub, neg))
    cand = jnp.concatenate(parts, axis=0)

    rowf = lax.broadcasted_iota(I32, (N_EXPERTS, tr), 0).astype(F32)
    idx_rows, w_rows, hits = [], [], []
    multi = jnp.zeros((N_EXPERTS, tr), F32)
    for _ in range(TOP_K):
        mx = jnp.max(cand, axis=0, keepdims=True)
        first = jnp.min(jnp.where(cand == mx, rowf, float(N_EXPERTS)), axis=0, keepdims=True)
        hit = rowf == first
        idx_rows.append(first)
        w_rows.append(jnp.sum(jnp.where(hit, scores, 0.0), axis=0, keepdims=True))
        cand = jnp.where(hit, neg, cand)
        multi = jnp.where(hit, 1.0, multi)
    w = jnp.concatenate(w_rows, axis=0)
    w_ref[...] = w / jnp.sum(w, axis=0, keepdims=True) * ROUTED_SCALE
    idx = jnp.concatenate(idx_rows, axis=0)
    idx_ref[...] = idx.astype(I32)

    carry = carry_scr[...]
    mb = multi.astype(BF16)
    before = _dot(mb, tri_ref[...]) + jnp.concatenate([carry] * (tr // 128), axis=1)
    rank_rows = [jnp.sum(jnp.where(rowf == idx_rows[k], before, 0.0), axis=0, keepdims=True) for k in range(TOP_K)]
    rank_ref[...] = jnp.concatenate(rank_rows, axis=0).astype(I32)
    carry = carry + _dot(mb, ones_ref[...])
    carry_scr[...] = carry
    cnt_ref[...] = carry


def _router(h2, router_w, router_bias, tr):
    t, d = h2.shape
    tri = jnp.asarray(np.triu(np.ones((tr, tr), np.float32), 1)).astype(BF16)
    ones = jnp.ones((tr, 128), BF16)
    tok = pl.BlockSpec((TOP_K, tr), lambda i: (0, i))
    fixed = lambda i: (0, 0)
    rwt = router_w.T
    rwt_hi = rwt.astype(BF16)
    rwt_split = jnp.stack([rwt_hi, (rwt - rwt_hi.astype(F32)).astype(BF16)])
    return pl.pallas_call(
        functools.partial(_router_kernel, tr=tr),
        grid=(t // tr,),
        in_specs=[pl.BlockSpec((tr, d), lambda i: (i, 0)),
                  pl.BlockSpec((2, N_EXPERTS, d), lambda i: (0, 0, 0)),
                  pl.BlockSpec((N_EXPERTS, 1), fixed),
                  pl.BlockSpec((tr, tr), fixed),
                  pl.BlockSpec((tr, 128), fixed)],
        out_specs=(tok, tok, tok, pl.BlockSpec((N_EXPERTS, 128), fixed)),
        out_shape=(jax.ShapeDtypeStruct((TOP_K, t), I32), jax.ShapeDtypeStruct((TOP_K, t), F32),
                   jax.ShapeDtypeStruct((TOP_K, t), I32), jax.ShapeDtypeStruct((N_EXPERTS, 128), F32)),
        scratch_shapes=[pltpu.VMEM((N_EXPERTS, 128), F32)],
        compiler_params=_cparams("arbitrary"),
        name="router",
    )(h2, rwt_split, router_bias.reshape(N_EXPERTS, 1), tri, ones)


def _pack_bf16_pair(a, b):
    ua = lax.bitcast_convert_type(a.astype(BF16).astype(F32), jnp.uint32)
    ub = lax.bitcast_convert_type(b.astype(BF16).astype(F32), jnp.uint32)
    return ua | (ub >> 16)


def _unpack_bf16_pair(w):
    a = lax.bitcast_convert_type(w & jnp.uint32(0xFFFF0000), F32)
    b = lax.bitcast_convert_type(w << 16, F32)
    return a, b


def _slot_kernel(ps_ref, idx_ref, rank_ref, slot_ref):
    idx = idx_ref[...]

    def body(e, acc):
        return jnp.where(idx == e, ps_ref[e], acc)

    slot_ref[...] = lax.fori_loop(0, N_EXPERTS, body, jnp.zeros_like(idx)) + rank_ref[...]


def _slots(pad_start, idx, rank, tt):
    t = idx.shape[1]
    tok = pl.BlockSpec((TOP_K, tt), lambda i, ps: (0, i))
    return pl.pallas_call(
        _slot_kernel,
        grid_spec=pltpu.PrefetchScalarGridSpec(num_scalar_prefetch=1, grid=(t // tt,),
                                               in_specs=[tok, tok], out_specs=tok),
        out_shape=jax.ShapeDtypeStruct((TOP_K, t), I32),
        compiler_params=_cparams("parallel"),
        name="slots",
    )(pad_start, idx, rank)


SC_CORES = 2
SC_SUBCORES = 16
SC_CHUNK = 64


def _sc_mesh():
    return plsc.VectorSubcoreMesh(core_axis_name="c", subcore_axis_name="s")


def _sc_dispatch(h2p, slot_chunks, n_rows):
    t, dw = h2p.shape
    per = slot_chunks.shape[0] // (SC_CORES * SC_SUBCORES)

    def body(h_hbm, slot_hbm, xs_hbm, idx_v, rows_v, sem):
        wid = lax.axis_index("s") * SC_CORES + lax.axis_index("c")

        @pl.loop(0, per)
        def _(c):
            ch = wid * per + c
            pltpu.sync_copy(slot_hbm.at[ch], idx_v)
            pltpu.sync_copy(h_hbm.at[pl.ds(ch * SC_CHUNK, SC_CHUNK)], rows_v)
            copies = [pltpu.async_copy(rows_v, xs_hbm.at[idx_v.at[k]], sem) for k in range(TOP_K)]
            for cp in copies:
                cp.wait()

    return pl.kernel(
        body, out_type=jax.ShapeDtypeStruct((n_rows, dw), h2p.dtype), mesh=_sc_mesh(),
        scratch_types=[pltpu.VMEM((TOP_K, SC_CHUNK), I32), pltpu.VMEM((SC_CHUNK, dw), h2p.dtype),
                       pltpu.SemaphoreType.DMA],
    )(h2p, slot_chunks)


def _sc_gather(ys, slot_chunks, t):
    dw = ys.shape[1]
    per = slot_chunks.shape[0] // (SC_CORES * SC_SUBCORES)

    def body(ys_hbm, slot_hbm, yg_hbm, idx_v, rows_v, gsem, wsem):
        wid = lax.axis_index("s") * SC_CORES + lax.axis_index("c")

        @pl.loop(0, per)
        def _(c):
            ch = wid * per + c
            pltpu.sync_copy(slot_hbm.at[ch], idx_v)
            gathers = [None] * TOP_K
            writes = [None] * TOP_K
            gathers[0] = pltpu.async_copy(ys_hbm.at[idx_v.at[0]], rows_v.at[0], gsem)
            for k in range(TOP_K):
                gathers[k].wait()
                if k + 1 < TOP_K:
                    if k >= 1:
                        writes[k - 1].wait()
                    gathers[k + 1] = pltpu.async_copy(ys_hbm.at[idx_v.at[k + 1]], rows_v.at[(k + 1) % 2], gsem)
                writes[k] = pltpu.async_copy(rows_v.at[k % 2], yg_hbm.at[k, pl.ds(ch * SC_CHUNK, SC_CHUNK)], wsem)
            writes[TOP_K - 2].wait()
            writes[TOP_K - 1].wait()

    return pl.kernel(
        body, out_type=jax.ShapeDtypeStruct((TOP_K, t, dw), ys.dtype), mesh=_sc_mesh(),
        scratch_types=[pltpu.VMEM((TOP_K, SC_CHUNK), I32), pltpu.VMEM((2, SC_CHUNK, dw), ys.dtype),
                       pltpu.SemaphoreType.DMA, pltpu.SemaphoreType.DMA],
    )(ys, slot_chunks)


def _experts_kernel(be_ref, nu_ref, bv_ref, xs_ref, wg_ref, wu_ref, wd_ref, ys_ref, wgb, wub, wdb):
    i = pl.program_id(0)
    half = D_MODEL // 2

    @pl.when((i == 0) | (be_ref[i] != be_ref[jnp.maximum(i - 1, 0)]))
    def _():
        wgb[...] = wg_ref[0].astype(BF16)
        wub[...] = wu_ref[0].astype(BF16)
        wdb[...] = wd_ref[0].astype(BF16)

    def ffn(rows):
        live = lax.broadcasted_iota(I32, (rows, xs_ref.shape[1]), 0) < bv_ref[i]
        xa, xb = _unpack_bf16_pair(jnp.where(live, xs_ref[0:rows, :], jnp.uint32(0)))
        xa, xb = xa.astype(BF16), xb.astype(BF16)
        g = _dot(xa, wgb[:half, :]) + _dot(xb, wgb[half:, :])
        u = _dot(xa, wub[:half, :]) + _dot(xb, wub[half:, :])
        act = (g * _sigmoid(g) * u).astype(BF16)
        y = _dot(act, wdb[...])
        ys_ref[0:rows, :] = _pack_bf16_pair(y[:, :half], y[:, half:])

    used = i < nu_ref[0]
    short = bv_ref[i] <= EXPERT_TAIL

    @pl.when(used & jnp.logical_not(short))
    def _():
        ffn(EXPERT_BLOCK)

    @pl.when(used & short)
    def _():
        ffn(EXPERT_TAIL)
        ys_ref[EXPERT_TAIL:, :] = jnp.zeros((EXPERT_BLOCK - EXPERT_TAIL, ys_ref.shape[1]), ys_ref.dtype)

    @pl.when(jnp.logical_not(used))
    def _():
        ys_ref[...] = jnp.zeros_like(ys_ref)


def _experts(xs, blk_e, n_used, blk_valid, w_gate, w_up, w_down):
    n_rows, dw = xs.shape
    d = w_gate.shape[1]
    nblk = n_rows // EXPERT_BLOCK
    row_map = lambda i, be, nu, bv: (jnp.minimum(i, nu[0] - 1), 0)
    w_map = lambda i, be, nu, bv: (be[i], 0, 0)
    return pl.pallas_call(
        _experts_kernel,
        grid_spec=pltpu.PrefetchScalarGridSpec(
            num_scalar_prefetch=3,
            grid=(nblk,),
            in_specs=[pl.BlockSpec((EXPERT_BLOCK, dw), row_map),
                      pl.BlockSpec((1, d, EXPERT_FF), w_map),
                      pl.BlockSpec((1, d, EXPERT_FF), w_map),
                      pl.BlockSpec((1, EXPERT_FF, d), w_map)],
            out_specs=pl.BlockSpec((EXPERT_BLOCK, dw), lambda i, be, nu, bv: (i, 0)),
            scratch_shapes=[pltpu.VMEM((d, EXPERT_FF), BF16), pltpu.VMEM((d, EXPERT_FF), BF16),
                            pltpu.VMEM((EXPERT_FF, d), BF16)]),
        out_shape=jax.ShapeDtypeStruct((n_rows, dw), xs.dtype),
        compiler_params=_cparams("arbitrary"),
        name="experts",
    )(blk_e, n_used, blk_valid, xs, w_gate, w_up, w_down)


def _combine_kernel(x1_ref, h_ref, w_ref, gt_ref, sg_ref, su_ref, sd_ref, yg_ref, o_ref):
    tc = x1_ref.shape[0]
    half = D_MODEL // 2
    hb = h_ref[...].astype(BF16)
    g = _dot(hb, sg_ref[...])
    u = _dot(hb, su_ref[...])
    ffn = _dot((g * _sigmoid(g) * u).astype(BF16), sd_ref[...])

    w = w_ref[...]
    ra = jnp.zeros((tc, half), F32)
    rb = jnp.zeros((tc, half), F32)
    for k in range(TOP_K):
        ya, yb = _unpack_bf16_pair(yg_ref[k])
        ra = ra + w[:, k:k + 1] * ya
        rb = rb + w[:, k:k + 1] * yb
    ffn = ffn + jnp.concatenate([ra, rb], axis=1)
    o_ref[...] = x1_ref[...] + gt_ref[0] * ffn


def _combine(x1, h2, w_tok, gt2, yg, sg, su, sd, seq, tc):
    t, d = x1.shape
    row = lambda i: (i, 0)
    fixed = lambda i: (0, 0)
    return pl.pallas_call(
        _combine_kernel,
        grid=(t // tc,),
        in_specs=[pl.BlockSpec((tc, d), row),
                  pl.BlockSpec((tc, d), row),
                  pl.BlockSpec((tc, TOP_K), row),
                  pl.BlockSpec((1, 1, d), lambda i: ((i * tc) // seq, 0, 0)),
                  pl.BlockSpec((d, SHARED_FF), fixed),
                  pl.BlockSpec((d, SHARED_FF), fixed),
                  pl.BlockSpec((SHARED_FF, d), fixed),
                  pl.BlockSpec((TOP_K, tc, d // 2), lambda i: (0, i, 0))],
        out_specs=pl.BlockSpec((tc, d), row),
        out_shape=jax.ShapeDtypeStruct((t, d), F32),
        compiler_params=_cparams("parallel"),
        name="combine",
    )(x1, h2, w_tok, gt2, sg.astype(BF16), su.astype(BF16), sd.astype(BF16), yg)


def _moe_parts(x1, h2, h2p, gt2, router_w, router_bias, w_gate, w_up, w_down, sg, su, sd):
    b, s, d = x1.shape
    t = b * s
    h2 = h2.reshape(t, d)
    idx, w, rank, cnt = _router(h2, router_w, router_bias, min(256, t))
    counts = cnt[:, 0].astype(I32)
    padded = (counts + EXPERT_BLOCK - 1) // EXPERT_BLOCK * EXPERT_BLOCK
    pad_end = jnp.cumsum(padded)
    pad_start = pad_end - padded
    n_rows = t * TOP_K + N_EXPERTS * EXPERT_BLOCK
    nblk = n_rows // EXPERT_BLOCK
    n_used = (pad_end[-1:] // EXPERT_BLOCK).astype(I32)
    blk_start = jnp.arange(nblk, dtype=I32) * EXPERT_BLOCK
    owns = (pad_start[None, :] <= blk_start[:, None]) & (blk_start[:, None] < pad_end[None, :])
    e_ids = jnp.arange(N_EXPERTS, dtype=I32)[None, :]
    last_e = jnp.max(jnp.where(counts > 0, e_ids[0], 0))
    blk_e = jnp.where(blk_start < pad_end[-1], jnp.sum(jnp.where(owns, e_ids, 0), axis=1), last_e).astype(I32)
    rows_left = jnp.sum(jnp.where(owns, (pad_start + counts)[None, :] - blk_start[:, None], 0), axis=1)
    blk_valid = jnp.clip(rows_left, 0, EXPERT_BLOCK).astype(I32)
    slot = _slots(pad_start.astype(I32), idx, rank, min(2048, t))
    slot_chunks = slot.reshape(TOP_K, t // SC_CHUNK, SC_CHUNK).transpose(1, 0, 2)
    xs = _sc_dispatch(h2p.reshape(t, d // 2), slot_chunks, n_rows)
    ys = _experts(xs, blk_e, n_used, blk_valid, w_gate, w_up, w_down)
    yg = _sc_gather(ys, slot_chunks, t)
    out = _combine(x1.reshape(t, d), h2, w.T, gt2, yg, sg, su, sd, s, min(256, t))
    return out.reshape(b, s, d), dict(idx=idx, w=w, rank=rank, cnt=cnt)


def kernel(x, c, ada_w, ada_b, norm1_w, norm2_w, w_in, q_norm_w, k_norm_w, cmp_pos, cmp_w1, cmp_b1, cmp_w2, attn_out_norm_w, hgrn_lb_param, rec_out_norm_w, w_out, router_w, router_bias, exp_w_gate, exp_w_up, exp_w_down, shared_w_gate, shared_w_up, shared_w_down):
    assert ada_w.shape[0] == 1, "one layer"
    assert x.shape[0] <= 8 and x.shape[1] % TK == 0 and x.shape[1] >= WINDOW + TQ
    l = 0
    x1, h2, h2p, gt2 = _mixer(x, c, ada_w[l], ada_b[l], norm1_w[l], norm2_w[l], w_in[l], q_norm_w[l], k_norm_w[l],
                         cmp_pos[l], cmp_w1[l], cmp_b1[l], cmp_w2[l], attn_out_norm_w[l], hgrn_lb_param,
                         rec_out_norm_w[l], w_out[l])
    out, _ = _moe_parts(x1, h2, h2p, gt2, router_w[l], router_bias[l], exp_w_gate[l], exp_w_up[l], exp_w_down[l],
                        shared_w_gate[l], shared_w_up[l], shared_w_down[l])
    return out
```

```python
import functools

import numpy as np
import jax
import jax.numpy as jnp
from jax import lax
from jax.experimental import pallas as pl
from jax.experimental.pallas import tpu as pltpu
from jax.experimental.pallas import tpu_sc as plsc

F32 = jnp.float32
BF16 = jnp.bfloat16
I32 = jnp.int32

D_MODEL = 1024
NSA_HEADS = 8
HEAD_DIM = 64
NSA_WIDTH = NSA_HEADS * HEAD_DIM
KV_HEADS = 2
HEADS_PER_KV = NSA_HEADS // KV_HEADS
KV_WIDTH = KV_HEADS * HEAD_DIM
CMP_BLOCK = 32
CMP_STRIDE = 16
CMP_HIDDEN = 256
SEL_BLOCK = 64
N_SELECT = 16
WINDOW = 512
HGRN_HEADS = 4
HGRN_DIM = 128
HGRN_WIDTH = HGRN_HEADS * HGRN_DIM
HGRN_CHUNK = 64
HGRN_SUB = 16
HGRN_LEVELS = ()
HGRN_LEAF = 16
N_EXPERTS = 256
TOP_K = 8
N_GROUPS = 8
GROUP_SIZE = N_EXPERTS // N_GROUPS
TOPK_GROUPS = 4
EXPERT_FF = 256
SHARED_FF = 256
ROUTED_SCALE = 2.5
RMS_EPS = 1e-6
BIG = 1e9
LOG2E = 1.4426950408889634
GATE_PAD = 128
PROJ_COLS = NSA_WIDTH + 6 * KV_WIDTH + GATE_PAD + 4 * HGRN_WIDTH

VMEM_LIMIT = 56 * 1024 * 1024

TQ = 256
TK = 512
EXPERT_BLOCK = 256
HIGHEST = lax.Precision.HIGHEST


def _cparams(*sem):
    return pltpu.CompilerParams(dimension_semantics=sem, vmem_limit_bytes=VMEM_LIMIT)


def _sigmoid(x):
    return 1.0 / (1.0 + jnp.exp(-x))


def _dot_nt(a, b):
    return lax.dot_general(a, b, (((1,), (1,)), ((), ())), preferred_element_type=F32)


def _dot(a, b, **kw):
    return jnp.dot(a, b, preferred_element_type=F32, **kw)


def _split_dot(a_bf16_exact, x):
    hi = x.astype(BF16)
    lo = (x - hi.astype(F32)).astype(BF16)
    return _dot(a_bf16_exact, hi) + _dot(a_bf16_exact, lo)


def _mod_kernel(c_ref, w_ref, b_ref, o_ref):
    c = c_ref[...]
    cond = c * _sigmoid(c)
    o_ref[...] = _dot(cond, w_ref[...], precision=HIGHEST) + b_ref[...]


def _mod(c, ada_w, ada_b):
    b, d = c.shape
    rows = 8
    c_pad = jnp.zeros((rows, d), F32).at[:b].set(c)
    n = ada_w.shape[1]
    out = pl.pallas_call(
        _mod_kernel,
        grid=(n // d,),
        in_specs=[pl.BlockSpec((rows, d), lambda j: (0, 0)),
                  pl.BlockSpec((d, d), lambda j: (0, j)),
                  pl.BlockSpec((1, d), lambda j: (0, j))],
        out_specs=pl.BlockSpec((rows, d), lambda j: (0, j)),
        out_shape=jax.ShapeDtypeStruct((rows, n), F32),
        compiler_params=_cparams("parallel"),
        name="mod",
    )(c_pad, ada_w, ada_b.reshape(1, n))
    return out[:b]


def _head_rms(t, w):
    return t * lax.rsqrt(jnp.mean(t * t, axis=-1, keepdims=True) + RMS_EPS) * w


def _pos_digits(pos):
    lane = lax.broadcasted_iota(I32, pos.shape, 1)
    d0 = (lane == 0) | (lane == 3) | (lane == 6)
    d1 = (lane == 1) | (lane == 4) | (lane == 7)
    d2 = (lane == 2) | (lane == 5) | (lane == 8)
    dig = jnp.where(d0, pos >> 12, jnp.where(d1, (pos >> 6) & 63, jnp.where(d2, pos & 63, 0)))
    return dig.astype(F32)


def _inproj_kernel(x_ref, sc_ref, sh_ref, n1_ref, w_ref, qnw_ref, knw_ref, lbp_ref, qaug_ref,
                   q_ref, kcr_ref, vcr_ref, ks_ref, vst_ref, kw_ref, vwt_ref, gt_ref,
                   hq_ref, hk_ref, hlf_ref, hv_ref, hg_ref):
    x = x_ref[0]
    ms = jnp.mean(x * x, axis=-1, keepdims=True)
    h = x * lax.rsqrt(ms + RMS_EPS) * n1_ref[...] * (1.0 + sc_ref[0]) + sh_ref[0]
    p = _dot(h.astype(BF16), w_ref[...])
    tm = x.shape[0]

    qnw = qnw_ref[...]
    for hd in range(NSA_HEADS):
        t = p[:, hd * HEAD_DIM:(hd + 1) * HEAD_DIM]
        qn = _head_rms(t, qnw) * (HEAD_DIM ** -0.5 * LOG2E)
        qa = jnp.broadcast_to(qaug_ref[hd:hd + 1, :], (tm, HEAD_DIM))
        q_ref[0, hd] = jnp.concatenate([qn, qa], axis=1).astype(BF16)
    kaug = _pos_digits(pl.program_id(1) * tm + lax.broadcasted_iota(I32, (tm, HEAD_DIM), 0))

    o = NSA_WIDTH
    kcr_ref[0] = p[:, o:o + KV_WIDTH]
    vcr_ref[0] = p[:, o + KV_WIDTH:o + 2 * KV_WIDTH]
    ks = p[:, o + 2 * KV_WIDTH:o + 3 * KV_WIDTH]
    vs = p[:, o + 3 * KV_WIDTH:o + 4 * KV_WIDTH]
    kw = p[:, o + 4 * KV_WIDTH:o + 5 * KV_WIDTH]
    vw = p[:, o + 5 * KV_WIDTH:o + 6 * KV_WIDTH]
    for g in range(KV_HEADS):
        sl = slice(g * HEAD_DIM, (g + 1) * HEAD_DIM)
        ks_ref[0, g] = jnp.concatenate([_head_rms(ks[:, sl], knw_ref[1:2, :]), kaug], axis=1).astype(BF16)
        kw_ref[0, g] = jnp.concatenate([_head_rms(kw[:, sl], knw_ref[2:3, :]), kaug], axis=1).astype(BF16)
    vst = vs.T.astype(BF16)
    vwt = vw.T.astype(BF16)
    for g in range(KV_HEADS):
        vst_ref[0, g] = vst[g * HEAD_DIM:(g + 1) * HEAD_DIM, :]
        vwt_ref[0, g] = vwt[g * HEAD_DIM:(g + 1) * HEAD_DIM, :]

    o = NSA_WIDTH + 6 * KV_WIDTH
    gates = _sigmoid(p[:, o:o + GATE_PAD])
    gt_ref[0] = gates.T[:NSA_HEADS * 3, :]

    o = o + GATE_PAD
    hq = p[:, o:o + HGRN_WIDTH]
    hf = p[:, o + HGRN_WIDTH:o + 2 * HGRN_WIDTH]
    hi = p[:, o + 2 * HGRN_WIDTH:o + 3 * HGRN_WIDTH]
    hg = p[:, o + 3 * HGRN_WIDTH:o + 4 * HGRN_WIDTH]
    lbp = lbp_ref[...]
    e = jnp.exp(lbp - jnp.max(lbp, axis=0, keepdims=True))
    lb = e[0:1, :] / jnp.sum(e, axis=0, keepdims=True)
    f = lb + (1.0 - lb) * _sigmoid(hf)
    hq_ref[0] = hq * _sigmoid(hq) * (HGRN_DIM ** -0.5)
    hk_ref[0] = 1.0 - f
    hlf_ref[0] = jnp.log(f)
    hv_ref[0] = hi
    hg_ref[0] = _sigmoid(hg)


def _inproj(x, sc1, sh1, norm1_w, w_cat, q_norm_w, k_norm_w, lb_param, tm):
    b, s, d = x.shape
    row = lambda bi, i: (bi, i, 0)
    per_b = lambda bi, i: (bi, 0, 0)
    fixed2 = lambda bi, i: (0, 0)
    aw = 2 * HEAD_DIM
    rest = np.array([2.0 ** (-8.0 * (i + 1) / NSA_HEADS) for i in range(NSA_HEADS)], np.float64) * LOG2E
    qaug = np.zeros((NSA_HEADS, HEAD_DIM), np.float32)
    for i in range(3):
        term = rest.astype(np.float32).astype(BF16).astype(np.float64)
        rest = rest - term
        for dgt, wgt in enumerate((4096.0, 64.0, 1.0)):
            qaug[:, 3 * i + dgt] = term * wgt
    assert np.all(qaug == qaug.astype(BF16).astype(np.float32))
    out_shape = (
        jax.ShapeDtypeStruct((b, NSA_HEADS, s, aw), BF16),
        jax.ShapeDtypeStruct((b, s, KV_WIDTH), F32),
        jax.ShapeDtypeStruct((b, s, KV_WIDTH), F32),
        jax.ShapeDtypeStruct((b, KV_HEADS, s, aw), BF16),
        jax.ShapeDtypeStruct((b, KV_HEADS, HEAD_DIM, s), BF16),
        jax.ShapeDtypeStruct((b, KV_HEADS, s, aw), BF16),
        jax.ShapeDtypeStruct((b, KV_HEADS, HEAD_DIM, s), BF16),
        jax.ShapeDtypeStruct((b, NSA_HEADS * 3, s), F32),
    ) + tuple(jax.ShapeDtypeStruct((b, s, HGRN_WIDTH), F32) for _ in range(5))
    hm = lambda n, w: pl.BlockSpec((1, n, tm, w), lambda bi, i: (bi, 0, i, 0))
    hmt = lambda n, w: pl.BlockSpec((1, n, w, tm), lambda bi, i: (bi, 0, 0, i))
    out_specs = (
        hm(NSA_HEADS, aw),
        pl.BlockSpec((1, tm, KV_WIDTH), row),
        pl.BlockSpec((1, tm, KV_WIDTH), row),
        hm(KV_HEADS, aw), hmt(KV_HEADS, HEAD_DIM),
        hm(KV_HEADS, aw), hmt(KV_HEADS, HEAD_DIM),
        pl.BlockSpec((1, NSA_HEADS * 3, tm), lambda bi, i: (bi, 0, i)),
    ) + tuple(pl.BlockSpec((1, tm, HGRN_WIDTH), row) for _ in range(5))
    return pl.pallas_call(
        _inproj_kernel,
        grid=(b, s // tm),
        in_specs=[pl.BlockSpec((1, tm, d), row),
                  pl.BlockSpec((1, 1, d), per_b),
                  pl.BlockSpec((1, 1, d), per_b),
                  pl.BlockSpec((1, d), fixed2),
                  pl.BlockSpec((d, PROJ_COLS), fixed2),
                  pl.BlockSpec((1, HEAD_DIM), fixed2),
                  pl.BlockSpec((3, HEAD_DIM), fixed2),
                  pl.BlockSpec(lb_param.shape, fixed2),
                  pl.BlockSpec((NSA_HEADS, HEAD_DIM), fixed2)],
        out_specs=out_specs,
        out_shape=out_shape,
        compiler_params=_cparams("parallel", "parallel"),
        name="inproj",
    )(x, sc1, sh1, norm1_w, w_cat, q_norm_w, k_norm_w, lb_param, jnp.asarray(qaug))


def _gelu_tanh(x):
    return 0.5 * x * (1.0 + jnp.tanh(0.7978845608028654 * (x + 0.044715 * x * x * x)))


def _compress_kernel(kch_ref, vch_ref, pos_ref, wa_ref, wb_ref, b1_ref, w2_ref, knw_ref,
                     kc_ref, vct_ref):
    n = kch_ref.shape[1]
    outs = []
    for br, ch_ref in enumerate((kch_ref, vch_ref)):
        ch = ch_ref[0]
        a = _dot((ch + pos_ref[br, 0:1, :]).astype(BF16), wa_ref[br])
        bm = _dot((ch + pos_ref[br, 1:2, :]).astype(BF16), wb_ref[br])
        pre = a + pltpu.roll(bm, n - 1, 0) + b1_ref[br]
        hid = _gelu_tanh(pre).astype(BF16)
        outs.append([_dot(hid[:, g * CMP_HIDDEN:(g + 1) * CMP_HIDDEN], w2_ref[br]) for g in range(KV_HEADS)])
    end_digits = _pos_digits(lax.broadcasted_iota(I32, (n, HEAD_DIM), 0) * CMP_STRIDE + (CMP_BLOCK - 1))
    for g in range(KV_HEADS):
        kc_ref[0, g] = jnp.concatenate([_head_rms(outs[0][g], knw_ref[0:1, :]), end_digits], axis=1).astype(BF16)
    vct = jnp.concatenate(outs[1], axis=1).T.astype(BF16)
    for g in range(KV_HEADS):
        vct_ref[0, g] = vct[g * HEAD_DIM:(g + 1) * HEAD_DIM, :]


def _compress(kc_raw, vc_raw, cmp_pos, cmp_w1, cmp_b1, cmp_w2, k_norm_w):
    b, s, _ = kc_raw.shape
    n = s // CMP_STRIDE
    half = CMP_STRIDE
    cw = CMP_STRIDE * KV_WIDTH
    kch = kc_raw.reshape(b, n, cw)
    vch = vc_raw.reshape(b, n, cw)
    pos = cmp_pos.reshape(2, 2, half, 1, HEAD_DIM)
    pos = jnp.broadcast_to(pos, (2, 2, half, KV_HEADS, HEAD_DIM)).reshape(2, 2, cw)
    w1 = cmp_w1.reshape(2, 2, half, HEAD_DIM, CMP_HIDDEN)
    eye = jnp.eye(KV_HEADS, dtype=F32)
    wfull = jnp.einsum('rhjdn,gk->rhjgdkn', w1, eye).reshape(2, 2, cw, KV_HEADS * CMP_HIDDEN).astype(BF16)
    b1 = jnp.tile(cmp_b1.reshape(2, 1, CMP_HIDDEN), (1, 1, KV_HEADS))
    fix = lambda r: (lambda bi: (0,) * r)
    return pl.pallas_call(
        _compress_kernel,
        grid=(b,),
        in_specs=[pl.BlockSpec((1, n, cw), lambda bi: (bi, 0, 0)),
                  pl.BlockSpec((1, n, cw), lambda bi: (bi, 0, 0)),
                  pl.BlockSpec((2, 2, cw), fix(3)),
                  pl.BlockSpec((2, cw, KV_HEADS * CMP_HIDDEN), fix(3)),
                  pl.BlockSpec((2, cw, KV_HEADS * CMP_HIDDEN), fix(3)),
                  pl.BlockSpec((2, 1, KV_HEADS * CMP_HIDDEN), fix(3)),
                  pl.BlockSpec((2, CMP_HIDDEN, HEAD_DIM), fix(3)),
                  pl.BlockSpec((3, HEAD_DIM), fix(2))],
        out_specs=(pl.BlockSpec((1, KV_HEADS, n, 2 * HEAD_DIM), lambda bi: (bi, 0, 0, 0)),
                   pl.BlockSpec((1, KV_HEADS, HEAD_DIM, n), lambda bi: (bi, 0, 0, 0))),
        out_shape=(jax.ShapeDtypeStruct((b, KV_HEADS, n, 2 * HEAD_DIM), BF16),
                   jax.ShapeDtypeStruct((b, KV_HEADS, HEAD_DIM, n), BF16)),
        compiler_params=_cparams("parallel"),
        name="compress",
    )(kch, vch, pos, wfull[:, 0], wfull[:, 1], b1, cmp_w2.astype(BF16), k_norm_w)


def _nsa_kernel(q_ref, kc_ref, vct_ref, ks_ref, vst_ref, kw_ref, vwt_ref, gt_ref, cdiff_ref, wdiff_ref,
                ovl_ref, oh_ref, onw_ref, wmask_ref, o_ref, buf_a, buf_b, m_scr, acc_scr, *, n_top):
    q0 = pl.program_id(2) * TQ
    ncols = HEADS_PER_KV * TQ
    q = q_ref[0].reshape(ncols, 2 * HEAD_DIM)
    ns = ovl_ref.shape[0]

    s = jnp.where(cdiff_ref[...] <= q0, _dot_nt(kc_ref[0, 0], q), -jnp.inf)
    m = jnp.max(s, axis=0, keepdims=True)
    m = jnp.where(m == -jnp.inf, 0.0, m)
    e = jnp.exp2(s - m)
    p = e / jnp.maximum(jnp.sum(e, axis=0, keepdims=True), 1e-30)
    o_c = _dot(vct_ref[0, 0], p.astype(BF16))

    psum = p[:, 0:TQ]
    for hh in range(1, HEADS_PER_KV):
        psum = psum + p[:, hh * TQ:(hh + 1) * TQ]
    imp = _split_dot(ovl_ref[...], psum)
    blk = lax.broadcasted_iota(I32, (ns, TQ), 0)
    tq = q0 + lax.broadcasted_iota(I32, (ns, TQ), 1)
    cur = tq >> 6
    forced = (blk == 0) | (blk == cur) | (blk == cur - 1)
    rank = jnp.where(forced, BIG, jnp.where(blk * SEL_BLOCK <= tq, imp, -BIG))

    blkf = blk.astype(F32)

    bias = jnp.full((ns, TQ), -1e30, F32)
    for _ in range(n_top):
        mx = jnp.max(rank, axis=0, keepdims=True)
        first = jnp.min(jnp.where(rank == mx, blkf, float(ns)), axis=0, keepdims=True)
        hit = blkf == first
        rank = jnp.where(hit, -jnp.inf, rank)
        bias = jnp.where(hit, 0.0, bias)

    if ns < 128:
        bias = jnp.concatenate([bias, jnp.zeros((128 - ns, TQ), F32)], axis=0)
    bias_t = bias.T.astype(BF16)
    qq = jnp.concatenate([q, jnp.concatenate([bias_t] * HEADS_PER_KV, axis=0)], axis=1)
    ones_rows = jnp.ones((16, TK), BF16)

    def scores(j):
        k0 = pl.multiple_of(j * TK, TK)
        kk = jnp.concatenate([ks_ref[0, 0, pl.ds(k0, TK), :], oh_ref[pl.ds(k0, TK), :]], axis=1)
        return _dot_nt(kk, qq)

    def consume(buf, j, causal, part):
        sc = buf[...]
        if causal:
            sc = jnp.where(wdiff_ref[0:TK, :] + (q0 - j * TK) >= 0, sc, -1e30)
        k0 = pl.multiple_of(j * TK, TK)
        m_run = m_scr[part]
        m_new = jnp.maximum(m_run, jnp.max(sc, axis=0, keepdims=True))
        ex = jnp.exp2(sc - m_new).astype(BF16)
        va = jnp.concatenate([vst_ref[0, 0, :, pl.ds(k0, TK)], ones_rows], axis=0)
        acc_scr[part] = jnp.exp2(m_run - m_new) * acc_scr[part] + _dot(va, ex)
        m_scr[part] = m_new

    n_past = q0 // TK
    m_scr[...] = jnp.full(m_scr.shape, -1e30, F32)
    acc_scr[...] = jnp.zeros(acc_scr.shape, F32)
    buf_a[...] = scores(0)

    nw = WINDOW + TQ
    start = pl.multiple_of(jnp.maximum(q0 - WINDOW, 0), TQ)
    sw = _dot_nt(kw_ref[0, 0, pl.ds(start, nw), :], q) + wmask_ref[0]
    ew = jnp.exp2(sw - jnp.max(sw, axis=0, keepdims=True))
    vw_aug = jnp.concatenate([vwt_ref[0, 0, :, pl.ds(start, nw)], jnp.ones((16, nw), BF16)], axis=0)
    acc_w = _dot(vw_aug, ew.astype(BF16))
    o_w = acc_w[0:HEAD_DIM, :] / acc_w[HEAD_DIM:HEAD_DIM + 1, :]

    def tiles(first, count):
        for u in range(0, count, 2):
            buf_b[...] = scores(first + u + 1)
            consume(buf_a, first + u, False, 0)
            buf_a[...] = scores(first + u + 2)
            consume(buf_b, first + u + 1, False, 1)
        return 0

    lax.fori_loop(0, n_past // 4, lambda i, _: tiles(4 * i, 4), 0)
    lax.fori_loop(0, (n_past // 2) % 2, lambda i, _: tiles((n_past // 4) * 4, 2), 0)

    @pl.when(n_past % 2 == 1)
    def _():
        buf_b[...] = scores(n_past)
        consume(buf_a, n_past - 1, False, 0)
        consume(buf_b, n_past, True, 1)

    @pl.when(n_past % 2 == 0)
    def _():
        consume(buf_a, n_past, True, 0)

    m_all = jnp.maximum(m_scr[0], m_scr[1])
    acc_s = jnp.exp2(m_scr[0] - m_all) * acc_scr[0] + jnp.exp2(m_scr[1] - m_all) * acc_scr[1]
    o_s = acc_s[0:HEAD_DIM, :] / acc_s[HEAD_DIM:HEAD_DIM + 1, :]

    gt = gt_ref[0, 0]
    outs = []
    for hh in range(HEADS_PER_KV):
        cs = slice(hh * TQ, (hh + 1) * TQ)
        o = (gt[3 * hh:3 * hh + 1, :] * o_c[:, cs] + gt[3 * hh + 1:3 * hh + 2, :] * o_s[:, cs]
             + gt[3 * hh + 2:3 * hh + 3, :] * o_w[:, cs])
        o = o * lax.rsqrt(jnp.mean(o * o, axis=0, keepdims=True) + RMS_EPS) * onw_ref[0, hh]
        outs.append(o)
    o_ref[0] = jnp.concatenate(outs, axis=0).T


def _nsa(q, kc, vct, ks, vst, kw, vwt, gates_t, attn_out_norm_w):
    b, _, s, aw = q.shape
    nc = kc.shape[2]
    ns = s // SEL_BLOCK
    n_top = min(N_SELECT, ns)
    ncols = HEADS_PER_KV * TQ
    nw = WINDOW + TQ
    tl = np.arange(ncols)[None, :] & (TQ - 1)
    cdiff = jnp.asarray((np.arange(nc)[:, None] * CMP_STRIDE + (CMP_BLOCK - 1) - tl).astype(np.int32))
    wdiff_np = (tl - np.arange(nw)[:, None]).astype(np.int32)
    wdiff = jnp.asarray(wdiff_np)
    n_off = WINDOW // TQ + 1
    dist_np = wdiff_np[None] + (np.arange(n_off) * TQ)[:, None, None]
    wmask = jnp.asarray(np.where((dist_np >= 0) & (dist_np < WINDOW), 0.0, -np.inf).astype(np.float32))
    ci = np.arange(nc)[None, :] * CMP_STRIDE
    bj = np.arange(ns)[:, None]
    ovl = ((ci < (bj + 1) * SEL_BLOCK) & (ci + CMP_BLOCK > bj * SEL_BLOCK) & (np.arange(nc)[None, :] < nc - 1))
    ovl = jnp.asarray(ovl.astype(np.float32)).astype(BF16)
    assert ns <= 128
    onehot = (np.arange(s)[:, None] // SEL_BLOCK == np.arange(128)[None, :])
    onehot = jnp.asarray(onehot.astype(np.float32)).astype(BF16)
    onw = jnp.broadcast_to(attn_out_norm_w.reshape(KV_HEADS, HEADS_PER_KV, HEAD_DIM, 1),
                           (KV_HEADS, HEADS_PER_KV, HEAD_DIM, TQ))
    gt = gates_t.reshape(b, KV_HEADS, HEADS_PER_KV * 3, s)
    per_bg = lambda bi, g, i: (bi, g, 0, 0)
    fixed = lambda bi, g, i: (0, 0)
    return pl.pallas_call(
        functools.partial(_nsa_kernel, n_top=n_top),
        grid=(b, KV_HEADS, s // TQ),
        in_specs=[pl.BlockSpec((1, HEADS_PER_KV, TQ, aw), lambda bi, g, i: (bi, g, i, 0)),
                  pl.BlockSpec((1, 1, nc, aw), per_bg),
                  pl.BlockSpec((1, 1, HEAD_DIM, nc), per_bg),
                  pl.BlockSpec((1, 1, s, aw), per_bg),
                  pl.BlockSpec((1, 1, HEAD_DIM, s), per_bg),
                  pl.BlockSpec((1, 1, s, aw), per_bg),
                  pl.BlockSpec((1, 1, HEAD_DIM, s), per_bg),
                  pl.BlockSpec((1, 1, HEADS_PER_KV * 3, TQ), lambda bi, g, i: (bi, g, 0, i)),
                  pl.BlockSpec((nc, ncols), fixed, pipeline_mode=pl.Buffered(1)),
                  pl.BlockSpec((nw, ncols), fixed, pipeline_mode=pl.Buffered(1)),
                  pl.BlockSpec((ns, nc), fixed, pipeline_mode=pl.Buffered(1)),
                  pl.BlockSpec((s, 128), fixed, pipeline_mode=pl.Buffered(1)),
                  pl.BlockSpec((1, HEADS_PER_KV, HEAD_DIM, TQ), lambda bi, g, i: (g, 0, 0, 0)),
                  pl.BlockSpec((1, nw, ncols), lambda bi, g, i: (jnp.minimum(i, n_off - 1), 0, 0))],
        out_specs=pl.BlockSpec((1, TQ, HEADS_PER_KV * HEAD_DIM), lambda bi, g, i: (bi, i, g)),
        out_shape=jax.ShapeDtypeStruct((b, s, NSA_WIDTH), F32),
        scratch_shapes=[pltpu.VMEM((TK, ncols), F32), pltpu.VMEM((TK, ncols), F32),
                        pltpu.VMEM((2, 1, ncols), F32), pltpu.VMEM((2, HEAD_DIM + 16, ncols), F32)],
        compiler_params=_cparams("parallel", "parallel", "arbitrary"),
        name="nsa",
    )(q, kc, vct, ks, vst, kw, vwt, gt, cdiff, wdiff, ovl, onehot, onw, wmask)


def _hgrn_cum_matrix():
    c = HGRN_CHUNK
    t = np.arange(c)
    mats = [(t[None, :] <= t[:, None])]
    for half in HGRN_LEVELS:
        ref = (t & ~(2 * half - 1)) + half - 1
        mats.append(t[None, :] <= ref[:, None])
    return np.concatenate(mats, axis=0).astype(np.float32)


def _hgrn_kernel(q_ref, k_ref, lf_ref, v_ref, g_ref, onw_ref, cm_ref, o_ref, state_scr, *, n_chunks):
    c = HGRN_CHUNK

    @pl.when(pl.program_id(1) == 0)
    def _():
        state_scr[...] = jnp.zeros_like(state_scr)

    ri = lax.broadcasted_iota(I32, (c, c), 0)
    ci = lax.broadcasted_iota(I32, (c, c), 1)
    rsub = ri // HGRN_SUB
    level_masks = [((ri & ~(2 * h - 1)) == (ci & ~(2 * h - 1))) & ((ri & h) != 0) & ((ci & h) == 0)
                   for h in HGRN_LEVELS]
    diag = ri == ci

    def head_chunk(r0, hd, state_t):
        cols = slice(hd * HGRN_DIM, (hd + 1) * HGRN_DIM)
        q = q_ref[0, pl.ds(r0, c), cols]
        k = k_ref[0, pl.ds(r0, c), cols]
        lf = lf_ref[0, pl.ds(r0, c), cols] * LOG2E
        v = v_ref[0, pl.ds(r0, c), cols]
        cm = cm_ref[...]
        l1 = lf.astype(BF16)
        rest = lf - l1.astype(F32)
        l2 = rest.astype(BF16)
        l3 = (rest - l2.astype(F32)).astype(BF16)
        cums = _dot(cm, l1) + _dot(cm, l2) + _dot(cm, l3)
        cum = cums[0:c]
        o = _dot_nt((q * jnp.exp2(cum)).astype(BF16), state_t.astype(BF16))
        scores = jnp.where(diag, jnp.sum(q * k, axis=-1, keepdims=True), 0.0)

        def factored(ref, mask, acc):
            qs = q * jnp.exp2(jnp.minimum(cum - ref, 0.0))
            kd = k * jnp.exp2(jnp.minimum(ref - cum, 0.0))
            return jnp.where(mask, _dot_nt(qs.astype(BF16), kd.astype(BF16)), acc)

        for i in range(1, c // HGRN_SUB):
            scores = factored(cum[i * HGRN_SUB - 1:i * HGRN_SUB, :], (rsub == i) & (ci < i * HGRN_SUB), scores)
        for lv in range(len(HGRN_LEVELS)):
            scores = factored(cums[(lv + 1) * c:(lv + 2) * c], level_masks[lv], scores)
        for d in range(1, HGRN_LEAF):
            ksh = pltpu.roll(k, d, 0)
            csh = pltpu.roll(cum, d, 0)
            w = jnp.sum(q * ksh * jnp.exp2(cum - csh), axis=-1, keepdims=True)
            scores = jnp.where((ri - ci == d) & ((ri & (HGRN_LEAF - 1)) >= d), w, scores)
        o = o + _dot(scores.astype(BF16), v.astype(BF16))
        last = cum[c - 1:c, :]
        kd = (k * jnp.exp2(last - cum)).astype(BF16)
        state_t = state_t * jnp.exp2(last) + _dot(v.T.astype(BF16), kd)
        o = o * g_ref[0, pl.ds(r0, c), cols]
        o = o * lax.rsqrt(jnp.mean(o * o, axis=-1, keepdims=True) + RMS_EPS) * onw_ref[:, cols]
        o_ref[0, pl.ds(r0, c), cols] = o
        return state_t

    def chunk(ck, states):
        r0 = pl.multiple_of(ck * c, c)
        return tuple(head_chunk(r0, hd, states[hd]) for hd in range(HGRN_HEADS))

    states = lax.fori_loop(0, n_chunks, chunk, tuple(state_scr[hd] for hd in range(HGRN_HEADS)))
    for hd in range(HGRN_HEADS):
        state_scr[hd] = states[hd]


def _hgrn(hq, hk, hlf, hv, hg, rec_out_norm_w, rows):
    b, s, _ = hq.shape
    cm = jnp.asarray(_hgrn_cum_matrix()).astype(BF16)
    blk = pl.BlockSpec((1, rows, HGRN_WIDTH), lambda bi, i: (bi, i, 0))
    return pl.pallas_call(
        functools.partial(_hgrn_kernel, n_chunks=rows // HGRN_CHUNK),
        grid=(b, s // rows),
        in_specs=[blk, blk, blk, blk, blk,
                  pl.BlockSpec((1, HGRN_WIDTH), lambda bi, i: (0, 0)),
                  pl.BlockSpec(cm.shape, lambda bi, i: (0, 0))],
        out_specs=blk,
        out_shape=jax.ShapeDtypeStruct((b, s, HGRN_WIDTH), F32),
        scratch_shapes=[pltpu.VMEM((HGRN_HEADS, HGRN_DIM, HGRN_DIM), F32)],
        compiler_params=_cparams("parallel", "arbitrary"),
        name="hgrn",
    )(hq, hk, hlf, hv, hg, rec_out_norm_w.reshape(1, HGRN_WIDTH), cm)


def _outproj_kernel(x_ref, a_ref, r_ref, wa_ref, wr_ref, gt_ref, sc_ref, sh_ref, n2_ref, x1_ref, h2_ref, h2p_ref):
    mixed = _dot(a_ref[0].astype(BF16), wa_ref[...]) + _dot(r_ref[0].astype(BF16), wr_ref[...])
    x1 = x_ref[0] + gt_ref[0] * mixed
    x1_ref[0] = x1
    ms = jnp.mean(x1 * x1, axis=-1, keepdims=True)
    h2 = x1 * lax.rsqrt(ms + RMS_EPS) * n2_ref[...] * (1.0 + sc_ref[0]) + sh_ref[0]
    h2_ref[0] = h2
    h2p_ref[0] = _pack_bf16_pair(h2[:, :D_MODEL // 2], h2[:, D_MODEL // 2:])


def _outproj(x, attn, rec, w_out, gt1, sc2, sh2, norm2_w, tm):
    b, s, d = x.shape
    row = lambda bi, i: (bi, i, 0)
    per_b = lambda bi, i: (bi, 0, 0)
    fixed2 = lambda bi, i: (0, 0)
    w = w_out.astype(BF16)
    return pl.pallas_call(
        _outproj_kernel,
        grid=(b, s // tm),
        in_specs=[pl.BlockSpec((1, tm, d), row),
                  pl.BlockSpec((1, tm, NSA_WIDTH), row),
                  pl.BlockSpec((1, tm, HGRN_WIDTH), row),
                  pl.BlockSpec((NSA_WIDTH, d), fixed2),
                  pl.BlockSpec((HGRN_WIDTH, d), fixed2),
                  pl.BlockSpec((1, 1, d), per_b),
                  pl.BlockSpec((1, 1, d), per_b),
                  pl.BlockSpec((1, 1, d), per_b),
                  pl.BlockSpec((1, d), fixed2)],
        out_specs=(pl.BlockSpec((1, tm, d), row), pl.BlockSpec((1, tm, d), row), pl.BlockSpec((1, tm, d // 2), row)),
        out_shape=(jax.ShapeDtypeStruct((b, s, d), F32), jax.ShapeDtypeStruct((b, s, d), F32),
                   jax.ShapeDtypeStruct((b, s, d // 2), jnp.uint32)),
        compiler_params=_cparams("parallel", "parallel"),
        name="outproj",
    )(x, attn, rec, w[:NSA_WIDTH], w[NSA_WIDTH:], gt1, sc2, sh2, norm2_w)


def _mixer(x, c, ada_w, ada_b, norm1_w, norm2_w, w_in, q_norm_w, k_norm_w, cmp_pos, cmp_w1, cmp_b1, cmp_w2,
           attn_out_norm_w, hgrn_lb_param, rec_out_norm_w, w_out):
    b, s, d = x.shape
    mod = _mod(c, ada_w, ada_b)
    sh1, sc1, gt1, sh2, sc2, gt2 = [m.reshape(b, 1, d) for m in jnp.split(mod, 6, axis=-1)]
    o = NSA_WIDTH + 6 * KV_WIDTH
    w_cat = jnp.concatenate([w_in[:, :o], w_in[:, o:o + NSA_HEADS * 3],
                             jnp.zeros((d, GATE_PAD - NSA_HEADS * 3), w_in.dtype),
                             w_in[:, o + NSA_HEADS * 3:]], axis=1).astype(BF16)
    tm = min(256, s)
    (q, kc_raw, vc_raw, ks, vst, kw, vwt, gates_t, hq, hk, hlf, hv, hg) = _inproj(
        x, sc1, sh1, norm1_w.reshape(1, d), w_cat, q_norm_w.reshape(1, HEAD_DIM), k_norm_w, hgrn_lb_param, tm)
    kc, vct = _compress(kc_raw, vc_raw, cmp_pos, cmp_w1, cmp_b1, cmp_w2, k_norm_w)
    attn = _nsa(q, kc, vct, ks, vst, kw, vwt, gates_t, attn_out_norm_w)
    rec = _hgrn(hq, hk, hlf, hv, hg, rec_out_norm_w, min(512, s))
    x1, h2, h2p = _outproj(x, attn, rec, w_out, gt1, sc2, sh2, norm2_w.reshape(1, d), tm)
    return x1, h2, h2p, gt2


def _router_kernel(h_ref, rwt_ref, bias_ref, tri_ref, ones_ref, idx_ref, w_ref, rank_ref, cnt_ref, carry_scr, *, tr):
    @pl.when(pl.program_id(0) == 0)
    def _():
        carry_scr[...] = jnp.zeros_like(carry_scr)

    h = h_ref[...]
    h_hi = h.astype(BF16)
    h_lo = (h - h_hi.astype(F32)).astype(BF16)
    logits = _dot_nt(rwt_ref[0], h_hi) + _dot_nt(rwt_ref[1], h_hi) + _dot_nt(rwt_ref[0], h_lo)
    scores = _sigmoid(logits)
    biased = scores + bias_ref[...]
    neg = -jnp.inf

    gs = []
    for g in range(N_GROUPS):
        sub = biased[g * GROUP_SIZE:(g + 1) * GROUP_SIZE, :]
        m1 = jnp.max(sub, axis=0, keepdims=True)
        dup = jnp.sum((sub == m1).astype(F32), axis=0, keepdims=True)
        m2 = jnp.max(jnp.where(sub < m1, sub, neg), axis=0, keepdims=True)
        gs.append(m1 + jnp.where(dup >= 2.0, m1, m2))
    parts = []
    for g in range(N_GROUPS):
        beaten = jnp.zeros_like(gs[g])
        for g2 in range(N_GROUPS):
            if g2 != g:
                beats = (gs[g2] >= gs[g]) if g2 < g else (gs[g2] > gs[g])
                beaten = beaten + beats.astype(F32)
        sub = biased[g * GROUP_SIZE:(g + 1) * GROUP_SIZE, :]
        parts.append(jnp.where(beaten < float(TOPK_GROUPS), sub, neg))
    cand = jnp.concatenate(parts, axis=0)

    rowf = lax.broadcasted_iota(I32, (N_EXPERTS, tr), 0).astype(F32)
    idx_rows, w_rows, hits = [], [], []
    multi = jnp.zeros((N_EXPERTS, tr), F32)
    for _ in range(TOP_K):
        mx = jnp.max(cand, axis=0, keepdims=True)
        first = jnp.min(jnp.where(cand == mx, rowf, float(N_EXPERTS)), axis=0, keepdims=True)
        hit = rowf == first
        idx_rows.append(first)
        w_rows.append(jnp.sum(jnp.where(hit, scores, 0.0), axis=0, keepdims=True))
        cand = jnp.where(hit, neg, cand)
        multi = jnp.where(hit, 1.0, multi)
    w = jnp.concatenate(w_rows, axis=0)
    w_ref[...] = w / jnp.sum(w, axis=0, keepdims=True) * ROUTED_SCALE
    idx = jnp.concatenate(idx_rows, axis=0)
    idx_ref[...] = idx.astype(I32)

    carry = carry_scr[...]
    mb = multi.astype(BF16)
    before = _dot(mb, tri_ref[...]) + jnp.concatenate([carry] * (tr // 128), axis=1)
    rank_rows = [jnp.sum(jnp.where(rowf == idx_rows[k], before, 0.0), axis=0, keepdims=True) for k in range(TOP_K)]
    rank_ref[...] = jnp.concatenate(rank_rows, axis=0).astype(I32)
    carry = carry + _dot(mb, ones_ref[...])
    carry_scr[...] = carry
    cnt_ref[...] = carry


def _router(h2, router_w, router_bias, tr):
    t, d = h2.shape
    tri = jnp.asarray(np.triu(np.ones((tr, tr), np.float32), 1)).astype(BF16)
    ones = jnp.ones((tr, 128), BF16)
    tok = pl.BlockSpec((TOP_K, tr), lambda i: (0, i))
    fixed = lambda i: (0, 0)
    rwt = router_w.T
    rwt_hi = rwt.astype(BF16)
    rwt_split = jnp.stack([rwt_hi, (rwt - rwt_hi.astype(F32)).astype(BF16)])
    return pl.pallas_call(
        functools.partial(_router_kernel, tr=tr),
        grid=(t // tr,),
        in_specs=[pl.BlockSpec((tr, d), lambda i: (i, 0)),
                  pl.BlockSpec((2, N_EXPERTS, d), lambda i: (0, 0, 0)),
                  pl.BlockSpec((N_EXPERTS, 1), fixed),
                  pl.BlockSpec((tr, tr), fixed),
                  pl.BlockSpec((tr, 128), fixed)],
        out_specs=(tok, tok, tok, pl.BlockSpec((N_EXPERTS, 128), fixed)),
        out_shape=(jax.ShapeDtypeStruct((TOP_K, t), I32), jax.ShapeDtypeStruct((TOP_K, t), F32),
                   jax.ShapeDtypeStruct((TOP_K, t), I32), jax.ShapeDtypeStruct((N_EXPERTS, 128), F32)),
        scratch_shapes=[pltpu.VMEM((N_EXPERTS, 128), F32)],
        compiler_params=_cparams("arbitrary"),
        name="router",
    )(h2, rwt_split, router_bias.reshape(N_EXPERTS, 1), tri, ones)


def _pack_bf16_pair(a, b):
    ua = lax.bitcast_convert_type(a.astype(BF16).astype(F32), jnp.uint32)
    ub = lax.bitcast_convert_type(b.astype(BF16).astype(F32), jnp.uint32)
    return ua | (ub >> 16)


def _unpack_bf16_pair(w):
    a = lax.bitcast_convert_type(w & jnp.uint32(0xFFFF0000), F32)
    b = lax.bitcast_convert_type(w << 16, F32)
    return a, b


def _slot_kernel(ps_ref, idx_ref, rank_ref, slot_ref):
    idx = idx_ref[...]

    def body(e, acc):
        return jnp.where(idx == e, ps_ref[e], acc)

    slot_ref[...] = lax.fori_loop(0, N_EXPERTS, body, jnp.zeros_like(idx)) + rank_ref[...]


def _slots(pad_start, idx, rank, tt):
    t = idx.shape[1]
    tok = pl.BlockSpec((TOP_K, tt), lambda i, ps: (0, i))
    return pl.pallas_call(
        _slot_kernel,
        grid_spec=pltpu.PrefetchScalarGridSpec(num_scalar_prefetch=1, grid=(t // tt,),
                                               in_specs=[tok, tok], out_specs=tok),
        out_shape=jax.ShapeDtypeStruct((TOP_K, t), I32),
        compiler_params=_cparams("parallel"),
        name="slots",
    )(pad_start, idx, rank)


SC_CORES = 2
SC_SUBCORES = 16
SC_CHUNK = 64


def _sc_mesh():
    return plsc.VectorSubcoreMesh(core_axis_name="c", subcore_axis_name="s")


def _sc_dispatch(h2p, slot_chunks, n_rows):
    t, dw = h2p.shape
    per = slot_chunks.shape[0] // (SC_CORES * SC_SUBCORES)

    def body(h_hbm, slot_hbm, xs_hbm, idx_v, rows_v, sem):
        wid = lax.axis_index("s") * SC_CORES + lax.axis_index("c")

        @pl.loop(0, per)
        def _(c):
            ch = wid * per + c
            pltpu.sync_copy(slot_hbm.at[ch], idx_v)
            pltpu.sync_copy(h_hbm.at[pl.ds(ch * SC_CHUNK, SC_CHUNK)], rows_v)
            copies = [pltpu.async_copy(rows_v, xs_hbm.at[idx_v.at[k]], sem) for k in range(TOP_K)]
            for cp in copies:
                cp.wait()

    return pl.kernel(
        body, out_type=jax.ShapeDtypeStruct((n_rows, dw), h2p.dtype), mesh=_sc_mesh(),
        scratch_types=[pltpu.VMEM((TOP_K, SC_CHUNK), I32), pltpu.VMEM((SC_CHUNK, dw), h2p.dtype),
                       pltpu.SemaphoreType.DMA],
    )(h2p, slot_chunks)


def _sc_gather(ys, slot_chunks, t):
    dw = ys.shape[1]
    per = slot_chunks.shape[0] // (SC_CORES * SC_SUBCORES)

    def body(ys_hbm, slot_hbm, yg_hbm, idx_v, rows_v, gsem, wsem):
        wid = lax.axis_index("s") * SC_CORES + lax.axis_index("c")

        @pl.loop(0, per)
        def _(c):
            ch = wid * per + c
            pltpu.sync_copy(slot_hbm.at[ch], idx_v)
            gathers = [None] * TOP_K
            writes = [None] * TOP_K
            gathers[0] = pltpu.async_copy(ys_hbm.at[idx_v.at[0]], rows_v.at[0], gsem)
            for k in range(TOP_K):
                gathers[k].wait()
                if k + 1 < TOP_K:
                    if k >= 1:
                        writes[k - 1].wait()
                    gathers[k + 1] = pltpu.async_copy(ys_hbm.at[idx_v.at[k + 1]], rows_v.at[(k + 1) % 2], gsem)
                writes[k] = pltpu.async_copy(rows_v.at[k % 2], yg_hbm.at[k, pl.ds(ch * SC_CHUNK, SC_CHUNK)], wsem)
            writes[TOP_K - 2].wait()
            writes[TOP_K - 1].wait()

    return pl.kernel(
        body, out_type=jax.ShapeDtypeStruct((TOP_K, t, dw), ys.dtype), mesh=_sc_mesh(),
        scratch_types=[pltpu.VMEM((TOP_K, SC_CHUNK), I32), pltpu.VMEM((2, SC_CHUNK, dw), ys.dtype),
                       pltpu.SemaphoreType.DMA, pltpu.SemaphoreType.DMA],
    )(ys, slot_chunks)


def _experts_kernel(ps_ref, cnt_ref, xs_ref, wg_ref, wu_ref, wd_ref, ys_ref,
                    xbuf, ybuf, wgb, wub, wdb, isem, osem, seq):
    e = pl.program_id(0)
    half = D_MODEL // 2
    n_experts = pl.num_programs(0)
    rb = EXPERT_BLOCK

    def chunks(ex):
        return (cnt_ref[ex] + rb - 1) // rb

    def fetch(row, slot):
        return pltpu.make_async_copy(xs_ref.at[pl.ds(row, rb)], xbuf.at[slot], isem.at[slot])

    def store(row, slot):
        return pltpu.make_async_copy(ybuf.at[slot], ys_ref.at[pl.ds(row, rb)], osem.at[slot])

    n = chunks(e)
    base = ps_ref[e]

    @pl.when(e == 0)
    def _():
        seq[0] = 0

        @pl.when(n > 0)
        def _():
            fetch(pl.multiple_of(base, rb), 0).start()

    @pl.when(n > 0)
    def _():
        wgb[...] = wg_ref[0].astype(BF16)
        wub[...] = wu_ref[0].astype(BF16)
        wdb[...] = wd_ref[0].astype(BF16)

        def chunk(j, _):
            g = seq[0]
            slot = g & 1
            row = pl.multiple_of(base + j * rb, rb)
            fetch(row, slot).wait()

            @pl.when(j + 1 < n)
            def _():
                fetch(pl.multiple_of(row + rb, rb), 1 - slot).start()

            live = lax.broadcasted_iota(I32, (rb, xbuf.shape[2]), 0) < cnt_ref[e] - j * rb
            xa, xb = _unpack_bf16_pair(jnp.where(live, xbuf[slot], jnp.uint32(0)))
            xa, xb = xa.astype(BF16), xb.astype(BF16)
            gate = _dot(xa, wgb[:half, :]) + _dot(xb, wgb[half:, :])
            up = _dot(xa, wub[:half, :]) + _dot(xb, wub[half:, :])
            y = _dot((gate * _sigmoid(gate) * up).astype(BF16), wdb[...])

            @pl.when(g >= 2)
            def _():
                store(0, slot).wait()

            ybuf[slot] = _pack_bf16_pair(y[:, :half], y[:, half:])
            store(row, slot).start()
            seq[0] = g + 1
            return 0

        lax.fori_loop(0, n, chunk, 0)

    nxt = jnp.minimum(e + 1, n_experts - 1)

    @pl.when((e + 1 < n_experts) & (chunks(nxt) > 0))
    def _():
        fetch(pl.multiple_of(ps_ref[nxt], rb), seq[0] & 1).start()

    @pl.when(e == n_experts - 1)
    def _():
        total = seq[0]

        @pl.when(total >= 1)
        def _():
            store(0, (total - 1) & 1).wait()

        @pl.when(total >= 2)
        def _():
            store(0, total & 1).wait()


def _experts(xs, pad_start, counts, w_gate, w_up, w_down):
    n_rows, dw = xs.shape
    n_exp, d, ff = w_gate.shape
    w_map = lambda e, ps, cn: (e, 0, 0)
    return pl.pallas_call(
        _experts_kernel,
        grid_spec=pltpu.PrefetchScalarGridSpec(
            num_scalar_prefetch=2,
            grid=(n_exp,),
            in_specs=[pl.BlockSpec(memory_space=pl.ANY),
                      pl.BlockSpec((1, d, ff), w_map),
                      pl.BlockSpec((1, d, ff), w_map),
                      pl.BlockSpec((1, ff, d), w_map)],
            out_specs=pl.BlockSpec(memory_space=pl.ANY),
            scratch_shapes=[pltpu.VMEM((2, EXPERT_BLOCK, dw), xs.dtype), pltpu.VMEM((2, EXPERT_BLOCK, dw), xs.dtype),
                            pltpu.VMEM((d, ff), BF16), pltpu.VMEM((d, ff), BF16), pltpu.VMEM((ff, d), BF16),
                            pltpu.SemaphoreType.DMA((2,)), pltpu.SemaphoreType.DMA((2,)),
                            pltpu.SMEM((1,), I32)]),
        out_shape=jax.ShapeDtypeStruct((n_rows, dw), xs.dtype),
        compiler_params=pltpu.CompilerParams(dimension_semantics=("arbitrary",), vmem_limit_bytes=VMEM_LIMIT,
                                             has_side_effects=True),
        name="experts",
    )(pad_start, counts, xs, w_gate, w_up, w_down)


def _combine_kernel(x1_ref, h_ref, w_ref, gt_ref, sg_ref, su_ref, sd_ref, yg_ref, o_ref):
    tc = x1_ref.shape[0]
    half = D_MODEL // 2
    hb = h_ref[...].astype(BF16)
    g = _dot(hb, sg_ref[...])
    u = _dot(hb, su_ref[...])
    ffn = _dot((g * _sigmoid(g) * u).astype(BF16), sd_ref[...])

    w = w_ref[...]
    ra = jnp.zeros((tc, half), F32)
    rb = jnp.zeros((tc, half), F32)
    for k in range(TOP_K):
        ya, yb = _unpack_bf16_pair(yg_ref[k])
        ra = ra + w[:, k:k + 1] * ya
        rb = rb + w[:, k:k + 1] * yb
    ffn = ffn + jnp.concatenate([ra, rb], axis=1)
    o_ref[...] = x1_ref[...] + gt_ref[0] * ffn


def _combine(x1, h2, w_tok, gt2, yg, sg, su, sd, seq, tc):
    t, d = x1.shape
    row = lambda i: (i, 0)
    fixed = lambda i: (0, 0)
    return pl.pallas_call(
        _combine_kernel,
        grid=(t // tc,),
        in_specs=[pl.BlockSpec((tc, d), row),
                  pl.BlockSpec((tc, d), row),
                  pl.BlockSpec((tc, TOP_K), row),
                  pl.BlockSpec((1, 1, d), lambda i: ((i * tc) // seq, 0, 0)),
                  pl.BlockSpec((d, SHARED_FF), fixed),
                  pl.BlockSpec((d, SHARED_FF), fixed),
                  pl.BlockSpec((SHARED_FF, d), fixed),
                  pl.BlockSpec((TOP_K, tc, d // 2), lambda i: (0, i, 0))],
        out_specs=pl.BlockSpec((tc, d), row),
        out_shape=jax.ShapeDtypeStruct((t, d), F32),
        compiler_params=_cparams("parallel"),
        name="combine",
    )(x1, h2, w_tok, gt2, sg.astype(BF16), su.astype(BF16), sd.astype(BF16), yg)


def _moe_parts(x1, h2, h2p, gt2, router_w, router_bias, w_gate, w_up, w_down, sg, su, sd):
    b, s, d = x1.shape
    t = b * s
    h2 = h2.reshape(t, d)
    idx, w, rank, cnt = _router(h2, router_w, router_bias, min(256, t))
    counts = cnt[:, 0].astype(I32)
    padded = (counts + EXPERT_BLOCK - 1) // EXPERT_BLOCK * EXPERT_BLOCK
    pad_end = jnp.cumsum(padded)
    pad_start = pad_end - padded
    pad_start = pad_start.astype(I32)
    n_rows = t * TOP_K + N_EXPERTS * EXPERT_BLOCK
    slot = _slots(pad_start, idx, rank, min(2048, t))
    slot_chunks = slot.reshape(TOP_K, t // SC_CHUNK, SC_CHUNK).transpose(1, 0, 2)
    xs = _sc_dispatch(h2p.reshape(t, d // 2), slot_chunks, n_rows)
    ys = _experts(xs, pad_start, counts, w_gate, w_up, w_down)
    yg = _sc_gather(ys, slot_chunks, t)
    out = _combine(x1.reshape(t, d), h2, w.T, gt2, yg, sg, su, sd, s, min(256, t))
    return out.reshape(b, s, d), dict(idx=idx, w=w, rank=rank, cnt=cnt)


def kernel(x, c, ada_w, ada_b, norm1_w, norm2_w, w_in, q_norm_w, k_norm_w, cmp_pos, cmp_w1, cmp_b1, cmp_w2, attn_out_norm_w, hgrn_lb_param, rec_out_norm_w, w_out, router_w, router_bias, exp_w_gate, exp_w_up, exp_w_down, shared_w_gate, shared_w_up, shared_w_down):
    assert ada_w.shape[0] == 1, "one layer"
    assert x.shape[0] <= 8 and x.shape[1] % TK == 0 and x.shape[1] >= WINDOW + TQ
    l = 0
    x1, h2, h2p, gt2 = _mixer(x, c, ada_w[l], ada_b[l], norm1_w[l], norm2_w[l], w_in[l], q_norm_w[l], k_norm_w[l],
                         cmp_pos[l], cmp_w1[l], cmp_b1[l], cmp_w2[l], attn_out_norm_w[l], hgrn_lb_param,
                         rec_out_norm_w[l], w_out[l])
    out, _ = _moe_parts(x1, h2, h2p, gt2, router_w[l], router_bias[l], exp_w_gate[l], exp_w_up[l], exp_w_down[l],
                        shared_w_gate[l], shared_w_up[l], shared_w_down[l])
    return out
```

```python
import functools

import numpy as np
import jax
import jax.numpy as jnp
from jax import lax
from jax.experimental import pallas as pl
from jax.experimental.pallas import tpu as pltpu
from jax.experimental.pallas import tpu_sc as plsc

F32 = jnp.float32
BF16 = jnp.bfloat16
I32 = jnp.int32

D_MODEL = 1024
NSA_HEADS = 8
HEAD_DIM = 64
NSA_WIDTH = NSA_HEADS * HEAD_DIM
KV_HEADS = 2
HEADS_PER_KV = NSA_HEADS // KV_HEADS
KV_WIDTH = KV_HEADS * HEAD_DIM
CMP_BLOCK = 32
CMP_STRIDE = 16
CMP_HIDDEN = 256
SEL_BLOCK = 64
N_SELECT = 16
WINDOW = 512
HGRN_HEADS = 4
HGRN_DIM = 128
HGRN_WIDTH = HGRN_HEADS * HGRN_DIM
HGRN_CHUNK = 64
HGRN_SUB = 16
HGRN_LEVELS = ()
HGRN_LEAF = 16
N_EXPERTS = 256
TOP_K = 8
N_GROUPS = 8
GROUP_SIZE = N_EXPERTS // N_GROUPS
TOPK_GROUPS = 4
EXPERT_FF = 256
SHARED_FF = 256
ROUTED_SCALE = 2.5
RMS_EPS = 1e-6
BIG = 1e9
LOG2E = 1.4426950408889634
GATE_PAD = 128
PROJ_COLS = NSA_WIDTH + 6 * KV_WIDTH + GATE_PAD + 4 * HGRN_WIDTH

VMEM_LIMIT = 56 * 1024 * 1024

TQ = 256
TK = 512
EXPERT_BLOCK = 512
EXPERT_TAIL = 128
EXPERT_RING = 3
HIGHEST = lax.Precision.HIGHEST


def _cparams(*sem):
    return pltpu.CompilerParams(dimension_semantics=sem, vmem_limit_bytes=VMEM_LIMIT)


def _sigmoid(x):
    return 1.0 / (1.0 + jnp.exp(-x))


def _dot_nt(a, b):
    return lax.dot_general(a, b, (((1,), (1,)), ((), ())), preferred_element_type=F32)


def _dot(a, b, **kw):
    return jnp.dot(a, b, preferred_element_type=F32, **kw)


def _split_dot(a_bf16_exact, x):
    hi = x.astype(BF16)
    lo = (x - hi.astype(F32)).astype(BF16)
    return _dot(a_bf16_exact, hi) + _dot(a_bf16_exact, lo)


def _mod_kernel(c_ref, w_ref, b_ref, o_ref):
    c = c_ref[...]
    cond = c * _sigmoid(c)
    o_ref[...] = _dot(cond, w_ref[...], precision=HIGHEST) + b_ref[...]


def _mod(c, ada_w, ada_b):
    b, d = c.shape
    rows = 8
    c_pad = jnp.zeros((rows, d), F32).at[:b].set(c)
    n = ada_w.shape[1]
    out = pl.pallas_call(
        _mod_kernel,
        grid=(n // d,),
        in_specs=[pl.BlockSpec((rows, d), lambda j: (0, 0)),
                  pl.BlockSpec((d, d), lambda j: (0, j)),
                  pl.BlockSpec((1, d), lambda j: (0, j))],
        out_specs=pl.BlockSpec((rows, d), lambda j: (0, j)),
        out_shape=jax.ShapeDtypeStruct((rows, n), F32),
        compiler_params=_cparams("parallel"),
        name="mod",
    )(c_pad, ada_w, ada_b.reshape(1, n))
    return out[:b]


def _head_rms(t, w):
    return t * lax.rsqrt(jnp.mean(t * t, axis=-1, keepdims=True) + RMS_EPS) * w


def _pos_digits(pos):
    lane = lax.broadcasted_iota(I32, pos.shape, 1)
    d0 = (lane == 0) | (lane == 3) | (lane == 6)
    d1 = (lane == 1) | (lane == 4) | (lane == 7)
    d2 = (lane == 2) | (lane == 5) | (lane == 8)
    dig = jnp.where(d0, pos >> 12, jnp.where(d1, (pos >> 6) & 63, jnp.where(d2, pos & 63, 0)))
    return dig.astype(F32)


def _inproj_kernel(x_ref, sc_ref, sh_ref, n1_ref, w_ref, qnw_ref, knw_ref, lbp_ref, qaug_ref,
                   q_ref, kcr_ref, vcr_ref, ks_ref, vst_ref, kw_ref, vwt_ref, gt_ref,
                   hq_ref, hk_ref, hlf_ref, hv_ref, hg_ref):
    x = x_ref[0]
    ms = jnp.mean(x * x, axis=-1, keepdims=True)
    h = x * lax.rsqrt(ms + RMS_EPS) * n1_ref[...] * (1.0 + sc_ref[0]) + sh_ref[0]
    p = _dot(h.astype(BF16), w_ref[...])
    tm = x.shape[0]

    qnw = qnw_ref[...]
    for hd in range(NSA_HEADS):
        t = p[:, hd * HEAD_DIM:(hd + 1) * HEAD_DIM]
        qn = _head_rms(t, qnw) * (HEAD_DIM ** -0.5 * LOG2E)
        qa = jnp.broadcast_to(qaug_ref[hd:hd + 1, :], (tm, HEAD_DIM))
        q_ref[0, hd] = jnp.concatenate([qn, qa], axis=1).astype(BF16)
    kaug = _pos_digits(pl.program_id(1) * tm + lax.broadcasted_iota(I32, (tm, HEAD_DIM), 0))

    o = NSA_WIDTH
    kcr_ref[0] = p[:, o:o + KV_WIDTH]
    vcr_ref[0] = p[:, o + KV_WIDTH:o + 2 * KV_WIDTH]
    ks = p[:, o + 2 * KV_WIDTH:o + 3 * KV_WIDTH]
    vs = p[:, o + 3 * KV_WIDTH:o + 4 * KV_WIDTH]
    kw = p[:, o + 4 * KV_WIDTH:o + 5 * KV_WIDTH]
    vw = p[:, o + 5 * KV_WIDTH:o + 6 * KV_WIDTH]
    for g in range(KV_HEADS):
        sl = slice(g * HEAD_DIM, (g + 1) * HEAD_DIM)
        ks_ref[0, g] = jnp.concatenate([_head_rms(ks[:, sl], knw_ref[1:2, :]), kaug], axis=1).astype(BF16)
        kw_ref[0, g] = jnp.concatenate([_head_rms(kw[:, sl], knw_ref[2:3, :]), kaug], axis=1).astype(BF16)
    vst = vs.T.astype(BF16)
    vwt = vw.T.astype(BF16)
    for g in range(KV_HEADS):
        vst_ref[0, g] = vst[g * HEAD_DIM:(g + 1) * HEAD_DIM, :]
        vwt_ref[0, g] = vwt[g * HEAD_DIM:(g + 1) * HEAD_DIM, :]

    o = NSA_WIDTH + 6 * KV_WIDTH
    gates = _sigmoid(p[:, o:o + GATE_PAD])
    gt_ref[0] = gates.T[:NSA_HEADS * 3, :]

    o = o + GATE_PAD
    hq = p[:, o:o + HGRN_WIDTH]
    hf = p[:, o + HGRN_WIDTH:o + 2 * HGRN_WIDTH]
    hi = p[:, o + 2 * HGRN_WIDTH:o + 3 * HGRN_WIDTH]
    hg = p[:, o + 3 * HGRN_WIDTH:o + 4 * HGRN_WIDTH]
    lbp = lbp_ref[...]
    e = jnp.exp(lbp - jnp.max(lbp, axis=0, keepdims=True))
    lb = e[0:1, :] / jnp.sum(e, axis=0, keepdims=True)
    f = lb + (1.0 - lb) * _sigmoid(hf)
    hq_ref[0] = hq * _sigmoid(hq) * (HGRN_DIM ** -0.5)
    hk_ref[0] = 1.0 - f
    hlf_ref[0] = jnp.log(f)
    hv_ref[0] = hi
    hg_ref[0] = _sigmoid(hg)


def _inproj(x, sc1, sh1, norm1_w, w_cat, q_norm_w, k_norm_w, lb_param, tm):
    b, s, d = x.shape
    row = lambda bi, i: (bi, i, 0)
    per_b = lambda bi, i: (bi, 0, 0)
    fixed2 = lambda bi, i: (0, 0)
    aw = 2 * HEAD_DIM
    rest = np.array([2.0 ** (-8.0 * (i + 1) / NSA_HEADS) for i in range(NSA_HEADS)], np.float64) * LOG2E
    qaug = np.zeros((NSA_HEADS, HEAD_DIM), np.float32)
    for i in range(3):
        term = rest.astype(np.float32).astype(BF16).astype(np.float64)
        rest = rest - term
        for dgt, wgt in enumerate((4096.0, 64.0, 1.0)):
            qaug[:, 3 * i + dgt] = term * wgt
    assert np.all(qaug == qaug.astype(BF16).astype(np.float32))
    out_shape = (
        jax.ShapeDtypeStruct((b, NSA_HEADS, s, aw), BF16),
        jax.ShapeDtypeStruct((b, s, KV_WIDTH), F32),
        jax.ShapeDtypeStruct((b, s, KV_WIDTH), F32),
        jax.ShapeDtypeStruct((b, KV_HEADS, s, aw), BF16),
        jax.ShapeDtypeStruct((b, KV_HEADS, HEAD_DIM, s), BF16),
        jax.ShapeDtypeStruct((b, KV_HEADS, s, aw), BF16),
        jax.ShapeDtypeStruct((b, KV_HEADS, HEAD_DIM, s), BF16),
        jax.ShapeDtypeStruct((b, NSA_HEADS * 3, s), F32),
    ) + tuple(jax.ShapeDtypeStruct((b, s, HGRN_WIDTH), F32) for _ in range(5))
    hm = lambda n, w: pl.BlockSpec((1, n, tm, w), lambda bi, i: (bi, 0, i, 0))
    hmt = lambda n, w: pl.BlockSpec((1, n, w, tm), lambda bi, i: (bi, 0, 0, i))
    out_specs = (
        hm(NSA_HEADS, aw),
        pl.BlockSpec((1, tm, KV_WIDTH), row),
        pl.BlockSpec((1, tm, KV_WIDTH), row),
        hm(KV_HEADS, aw), hmt(KV_HEADS, HEAD_DIM),
        hm(KV_HEADS, aw), hmt(KV_HEADS, HEAD_DIM),
        pl.BlockSpec((1, NSA_HEADS * 3, tm), lambda bi, i: (bi, 0, i)),
    ) + tuple(pl.BlockSpec((1, tm, HGRN_WIDTH), row) for _ in range(5))
    return pl.pallas_call(
        _inproj_kernel,
        grid=(b, s // tm),
        in_specs=[pl.BlockSpec((1, tm, d), row),
                  pl.BlockSpec((1, 1, d), per_b),
                  pl.BlockSpec((1, 1, d), per_b),
                  pl.BlockSpec((1, d), fixed2),
                  pl.BlockSpec((d, PROJ_COLS), fixed2),
                  pl.BlockSpec((1, HEAD_DIM), fixed2),
                  pl.BlockSpec((3, HEAD_DIM), fixed2),
                  pl.BlockSpec(lb_param.shape, fixed2),
                  pl.BlockSpec((NSA_HEADS, HEAD_DIM), fixed2)],
        out_specs=out_specs,
        out_shape=out_shape,
        compiler_params=_cparams("parallel", "parallel"),
        name="inproj",
    )(x, sc1, sh1, norm1_w, w_cat, q_norm_w, k_norm_w, lb_param, jnp.asarray(qaug))


def _gelu_tanh(x):
    return 0.5 * x * (1.0 + jnp.tanh(0.7978845608028654 * (x + 0.044715 * x * x * x)))


def _compress_kernel(kch_ref, vch_ref, pos_ref, wa_ref, wb_ref, b1_ref, w2_ref, knw_ref,
                     kc_ref, vct_ref):
    n = kch_ref.shape[1]
    outs = []
    for br, ch_ref in enumerate((kch_ref, vch_ref)):
        ch = ch_ref[0]
        a = _dot((ch + pos_ref[br, 0:1, :]).astype(BF16), wa_ref[br])
        bm = _dot((ch + pos_ref[br, 1:2, :]).astype(BF16), wb_ref[br])
        pre = a + pltpu.roll(bm, n - 1, 0) + b1_ref[br]
        hid = _gelu_tanh(pre).astype(BF16)
        outs.append([_dot(hid[:, g * CMP_HIDDEN:(g + 1) * CMP_HIDDEN], w2_ref[br]) for g in range(KV_HEADS)])
    end_digits = _pos_digits(lax.broadcasted_iota(I32, (n, HEAD_DIM), 0) * CMP_STRIDE + (CMP_BLOCK - 1))
    for g in range(KV_HEADS):
        kc_ref[0, g] = jnp.concatenate([_head_rms(outs[0][g], knw_ref[0:1, :]), end_digits], axis=1).astype(BF16)
    vct = jnp.concatenate(outs[1], axis=1).T.astype(BF16)
    for g in range(KV_HEADS):
        vct_ref[0, g] = vct[g * HEAD_DIM:(g + 1) * HEAD_DIM, :]


def _compress(kc_raw, vc_raw, cmp_pos, cmp_w1, cmp_b1, cmp_w2, k_norm_w):
    b, s, _ = kc_raw.shape
    n = s // CMP_STRIDE
    half = CMP_STRIDE
    cw = CMP_STRIDE * KV_WIDTH
    kch = kc_raw.reshape(b, n, cw)
    vch = vc_raw.reshape(b, n, cw)
    pos = cmp_pos.reshape(2, 2, half, 1, HEAD_DIM)
    pos = jnp.broadcast_to(pos, (2, 2, half, KV_HEADS, HEAD_DIM)).reshape(2, 2, cw)
    w1 = cmp_w1.reshape(2, 2, half, HEAD_DIM, CMP_HIDDEN)
    eye = jnp.eye(KV_HEADS, dtype=F32)
    wfull = jnp.einsum('rhjdn,gk->rhjgdkn', w1, eye).reshape(2, 2, cw, KV_HEADS * CMP_HIDDEN).astype(BF16)
    b1 = jnp.tile(cmp_b1.reshape(2, 1, CMP_HIDDEN), (1, 1, KV_HEADS))
    fix = lambda r: (lambda bi: (0,) * r)
    return pl.pallas_call(
        _compress_kernel,
        grid=(b,),
        in_specs=[pl.BlockSpec((1, n, cw), lambda bi: (bi, 0, 0)),
                  pl.BlockSpec((1, n, cw), lambda bi: (bi, 0, 0)),
                  pl.BlockSpec((2, 2, cw), fix(3)),
                  pl.BlockSpec((2, cw, KV_HEADS * CMP_HIDDEN), fix(3)),
                  pl.BlockSpec((2, cw, KV_HEADS * CMP_HIDDEN), fix(3)),
                  pl.BlockSpec((2, 1, KV_HEADS * CMP_HIDDEN), fix(3)),
                  pl.BlockSpec((2, CMP_HIDDEN, HEAD_DIM), fix(3)),
                  pl.BlockSpec((3, HEAD_DIM), fix(2))],
        out_specs=(pl.BlockSpec((1, KV_HEADS, n, 2 * HEAD_DIM), lambda bi: (bi, 0, 0, 0)),
                   pl.BlockSpec((1, KV_HEADS, HEAD_DIM, n), lambda bi: (bi, 0, 0, 0))),
        out_shape=(jax.ShapeDtypeStruct((b, KV_HEADS, n, 2 * HEAD_DIM), BF16),
                   jax.ShapeDtypeStruct((b, KV_HEADS, HEAD_DIM, n), BF16)),
        compiler_params=_cparams("parallel"),
        name="compress",
    )(kch, vch, pos, wfull[:, 0], wfull[:, 1], b1, cmp_w2.astype(BF16), k_norm_w)


def _nsa_kernel(q_ref, kc_ref, vct_ref, ks_ref, vst_ref, kw_ref, vwt_ref, gt_ref, cdiff_ref, wdiff_ref,
                ovl_ref, oh_ref, onw_ref, wmask_ref, o_ref, buf_a, buf_b, m_scr, acc_scr, *, n_top):
    q0 = pl.program_id(2) * TQ
    ncols = HEADS_PER_KV * TQ
    q = q_ref[0].reshape(ncols, 2 * HEAD_DIM)
    ns = ovl_ref.shape[0]

    s = jnp.where(cdiff_ref[...] <= q0, _dot_nt(kc_ref[0, 0], q), -jnp.inf)
    m = jnp.max(s, axis=0, keepdims=True)
    m = jnp.where(m == -jnp.inf, 0.0, m)
    e = jnp.exp2(s - m)
    p = e / jnp.maximum(jnp.sum(e, axis=0, keepdims=True), 1e-30)
    o_c = _dot(vct_ref[0, 0], p.astype(BF16))

    psum = p[:, 0:TQ]
    for hh in range(1, HEADS_PER_KV):
        psum = psum + p[:, hh * TQ:(hh + 1) * TQ]
    imp = _split_dot(ovl_ref[...], psum)
    blk = lax.broadcasted_iota(I32, (ns, TQ), 0)
    tq = q0 + lax.broadcasted_iota(I32, (ns, TQ), 1)
    cur = tq >> 6
    forced = (blk == 0) | (blk == cur) | (blk == cur - 1)
    rank = jnp.where(forced, BIG, jnp.where(blk * SEL_BLOCK <= tq, imp, -BIG))

    blkf = blk.astype(F32)

    bias = jnp.full((ns, TQ), -1e30, F32)
    for _ in range(n_top):
        mx = jnp.max(rank, axis=0, keepdims=True)
        first = jnp.min(jnp.where(rank == mx, blkf, float(ns)), axis=0, keepdims=True)
        hit = blkf == first
        rank = jnp.where(hit, -jnp.inf, rank)
        bias = jnp.where(hit, 0.0, bias)

    if ns < 128:
        bias = jnp.concatenate([bias, jnp.zeros((128 - ns, TQ), F32)], axis=0)
    bias_t = bias.T.astype(BF16)
    qq = jnp.concatenate([q, jnp.concatenate([bias_t] * HEADS_PER_KV, axis=0)], axis=1)
    ones_rows = jnp.ones((16, TK), BF16)

    def scores(j):
        k0 = pl.multiple_of(j * TK, TK)
        kk = jnp.concatenate([ks_ref[0, 0, pl.ds(k0, TK), :], oh_ref[pl.ds(k0, TK), :]], axis=1)
        return _dot_nt(kk, qq)

    def consume(buf, j, causal, part):
        sc = buf[...]
        if causal:
            sc = jnp.where(wdiff_ref[0:TK, :] + (q0 - j * TK) >= 0, sc, -1e30)
        k0 = pl.multiple_of(j * TK, TK)
        m_run = m_scr[part]
        m_new = jnp.maximum(m_run, jnp.max(sc, axis=0, keepdims=True))
        ex = jnp.exp2(sc - m_new).astype(BF16)
        va = jnp.concatenate([vst_ref[0, 0, :, pl.ds(k0, TK)], ones_rows], axis=0)
        acc_scr[part] = jnp.exp2(m_run - m_new) * acc_scr[part] + _dot(va, ex)
        m_scr[part] = m_new

    n_past = q0 // TK
    m_scr[...] = jnp.full(m_scr.shape, -1e30, F32)
    acc_scr[...] = jnp.zeros(acc_scr.shape, F32)
    buf_a[...] = scores(0)

    nw = WINDOW + TQ
    start = pl.multiple_of(jnp.maximum(q0 - WINDOW, 0), TQ)
    sw = _dot_nt(kw_ref[0, 0, pl.ds(start, nw), :], q) + wmask_ref[0]
    ew = jnp.exp2(sw - jnp.max(sw, axis=0, keepdims=True))
    vw_aug = jnp.concatenate([vwt_ref[0, 0, :, pl.ds(start, nw)], jnp.ones((16, nw), BF16)], axis=0)
    acc_w = _dot(vw_aug, ew.astype(BF16))
    o_w = acc_w[0:HEAD_DIM, :] / acc_w[HEAD_DIM:HEAD_DIM + 1, :]

    def tiles(first, count):
        for u in range(0, count, 2):
            buf_b[...] = scores(first + u + 1)
            consume(buf_a, first + u, False, 0)
            buf_a[...] = scores(first + u + 2)
            consume(buf_b, first + u + 1, False, 1)
        return 0

    lax.fori_loop(0, n_past // 4, lambda i, _: tiles(4 * i, 4), 0)
    lax.fori_loop(0, (n_past // 2) % 2, lambda i, _: tiles((n_past // 4) * 4, 2), 0)

    @pl.when(n_past % 2 == 1)
    def _():
        buf_b[...] = scores(n_past)
        consume(buf_a, n_past - 1, False, 0)
        consume(buf_b, n_past, True, 1)

    @pl.when(n_past % 2 == 0)
    def _():
        consume(buf_a, n_past, True, 0)

    m_all = jnp.maximum(m_scr[0], m_scr[1])
    acc_s = jnp.exp2(m_scr[0] - m_all) * acc_scr[0] + jnp.exp2(m_scr[1] - m_all) * acc_scr[1]
    o_s = acc_s[0:HEAD_DIM, :] / acc_s[HEAD_DIM:HEAD_DIM + 1, :]

    gt = gt_ref[0, 0]
    outs = []
    for hh in range(HEADS_PER_KV):
        cs = slice(hh * TQ, (hh + 1) * TQ)
        o = (gt[3 * hh:3 * hh + 1, :] * o_c[:, cs] + gt[3 * hh + 1:3 * hh + 2, :] * o_s[:, cs]
             + gt[3 * hh + 2:3 * hh + 3, :] * o_w[:, cs])
        o = o * lax.rsqrt(jnp.mean(o * o, axis=0, keepdims=True) + RMS_EPS) * onw_ref[0, hh]
        outs.append(o)
    o_ref[0] = jnp.concatenate(outs, axis=0).T


def _nsa(q, kc, vct, ks, vst, kw, vwt, gates_t, attn_out_norm_w):
    b, _, s, aw = q.shape
    nc = kc.shape[2]
    ns = s // SEL_BLOCK
    n_top = min(N_SELECT, ns)
    ncols = HEADS_PER_KV * TQ
    nw = WINDOW + TQ
    tl = np.arange(ncols)[None, :] & (TQ - 1)
    cdiff = jnp.asarray((np.arange(nc)[:, None] * CMP_STRIDE + (CMP_BLOCK - 1) - tl).astype(np.int32))
    wdiff_np = (tl - np.arange(nw)[:, None]).astype(np.int32)
    wdiff = jnp.asarray(wdiff_np)
    n_off = WINDOW // TQ + 1
    dist_np = wdiff_np[None] + (np.arange(n_off) * TQ)[:, None, None]
    wmask = jnp.asarray(np.where((dist_np >= 0) & (dist_np < WINDOW), 0.0, -np.inf).astype(np.float32))
    ci = np.arange(nc)[None, :] * CMP_STRIDE
    bj = np.arange(ns)[:, None]
    ovl = ((ci < (bj + 1) * SEL_BLOCK) & (ci + CMP_BLOCK > bj * SEL_BLOCK) & (np.arange(nc)[None, :] < nc - 1))
    ovl = jnp.asarray(ovl.astype(np.float32)).astype(BF16)
    assert ns <= 128
    onehot = (np.arange(s)[:, None] // SEL_BLOCK == np.arange(128)[None, :])
    onehot = jnp.asarray(onehot.astype(np.float32)).astype(BF16)
    onw = jnp.broadcast_to(attn_out_norm_w.reshape(KV_HEADS, HEADS_PER_KV, HEAD_DIM, 1),
                           (KV_HEADS, HEADS_PER_KV, HEAD_DIM, TQ))
    gt = gates_t.reshape(b, KV_HEADS, HEADS_PER_KV * 3, s)
    per_bg = lambda bi, g, i: (bi, g, 0, 0)
    fixed = lambda bi, g, i: (0, 0)
    return pl.pallas_call(
        functools.partial(_nsa_kernel, n_top=n_top),
        grid=(b, KV_HEADS, s // TQ),
        in_specs=[pl.BlockSpec((1, HEADS_PER_KV, TQ, aw), lambda bi, g, i: (bi, g, i, 0)),
                  pl.BlockSpec((1, 1, nc, aw), per_bg),
                  pl.BlockSpec((1, 1, HEAD_DIM, nc), per_bg),
                  pl.BlockSpec((1, 1, s, aw), per_bg),
                  pl.BlockSpec((1, 1, HEAD_DIM, s), per_bg),
                  pl.BlockSpec((1, 1, s, aw), per_bg),
                  pl.BlockSpec((1, 1, HEAD_DIM, s), per_bg),
                  pl.BlockSpec((1, 1, HEADS_PER_KV * 3, TQ), lambda bi, g, i: (bi, g, 0, i)),
                  pl.BlockSpec((nc, ncols), fixed, pipeline_mode=pl.Buffered(1)),
                  pl.BlockSpec((nw, ncols), fixed, pipeline_mode=pl.Buffered(1)),
                  pl.BlockSpec((ns, nc), fixed, pipeline_mode=pl.Buffered(1)),
                  pl.BlockSpec((s, 128), fixed, pipeline_mode=pl.Buffered(1)),
                  pl.BlockSpec((1, HEADS_PER_KV, HEAD_DIM, TQ), lambda bi, g, i: (g, 0, 0, 0)),
                  pl.BlockSpec((1, nw, ncols), lambda bi, g, i: (jnp.minimum(i, n_off - 1), 0, 0))],
        out_specs=pl.BlockSpec((1, TQ, HEADS_PER_KV * HEAD_DIM), lambda bi, g, i: (bi, i, g)),
        out_shape=jax.ShapeDtypeStruct((b, s, NSA_WIDTH), F32),
        scratch_shapes=[pltpu.VMEM((TK, ncols), F32), pltpu.VMEM((TK, ncols), F32),
                        pltpu.VMEM((2, 1, ncols), F32), pltpu.VMEM((2, HEAD_DIM + 16, ncols), F32)],
        compiler_params=_cparams("parallel", "parallel", "arbitrary"),
        name="nsa",
    )(q, kc, vct, ks, vst, kw, vwt, gt, cdiff, wdiff, ovl, onehot, onw, wmask)


def _hgrn_cum_matrix():
    c = HGRN_CHUNK
    t = np.arange(c)
    mats = [(t[None, :] <= t[:, None])]
    for half in HGRN_LEVELS:
        ref = (t & ~(2 * half - 1)) + half - 1
        mats.append(t[None, :] <= ref[:, None])
    return np.concatenate(mats, axis=0).astype(np.float32)


def _hgrn_kernel(q_ref, k_ref, lf_ref, v_ref, g_ref, onw_ref, cm_ref, o_ref, state_scr, *, n_chunks):
    c = HGRN_CHUNK

    @pl.when(pl.program_id(1) == 0)
    def _():
        state_scr[...] = jnp.zeros_like(state_scr)

    ri = lax.broadcasted_iota(I32, (c, c), 0)
    ci = lax.broadcasted_iota(I32, (c, c), 1)
    rsub = ri // HGRN_SUB
    level_masks = [((ri & ~(2 * h - 1)) == (ci & ~(2 * h - 1))) & ((ri & h) != 0) & ((ci & h) == 0)
                   for h in HGRN_LEVELS]
    diag = ri == ci

    def head_chunk(r0, hd, state_t):
        cols = slice(hd * HGRN_DIM, (hd + 1) * HGRN_DIM)
        q = q_ref[0, pl.ds(r0, c), cols]
        k = k_ref[0, pl.ds(r0, c), cols]
        lf = lf_ref[0, pl.ds(r0, c), cols] * LOG2E
        v = v_ref[0, pl.ds(r0, c), cols]
        cm = cm_ref[...]
        l1 = lf.astype(BF16)
        rest = lf - l1.astype(F32)
        l2 = rest.astype(BF16)
        l3 = (rest - l2.astype(F32)).astype(BF16)
        cums = _dot(cm, l1) + _dot(cm, l2) + _dot(cm, l3)
        cum = cums[0:c]
        o = _dot_nt((q * jnp.exp2(cum)).astype(BF16), state_t.astype(BF16))
        scores = jnp.where(diag, jnp.sum(q * k, axis=-1, keepdims=True), 0.0)

        def factored(ref, mask, acc):
            qs = q * jnp.exp2(jnp.minimum(cum - ref, 0.0))
            kd = k * jnp.exp2(jnp.minimum(ref - cum, 0.0))
            return jnp.where(mask, _dot_nt(qs.astype(BF16), kd.astype(BF16)), acc)

        for i in range(1, c // HGRN_SUB):
            scores = factored(cum[i * HGRN_SUB - 1:i * HGRN_SUB, :], (rsub == i) & (ci < i * HGRN_SUB), scores)
        for lv in range(len(HGRN_LEVELS)):
            scores = factored(cums[(lv + 1) * c:(lv + 2) * c], level_masks[lv], scores)
        for d in range(1, HGRN_LEAF):
            ksh = pltpu.roll(k, d, 0)
            csh = pltpu.roll(cum, d, 0)
            w = jnp.sum(q * ksh * jnp.exp2(cum - csh), axis=-1, keepdims=True)
            scores = jnp.where((ri - ci == d) & ((ri & (HGRN_LEAF - 1)) >= d), w, scores)
        o = o + _dot(scores.astype(BF16), v.astype(BF16))
        last = cum[c - 1:c, :]
        kd = (k * jnp.exp2(last - cum)).astype(BF16)
        state_t = state_t * jnp.exp2(last) + _dot(v.T.astype(BF16), kd)
        o = o * g_ref[0, pl.ds(r0, c), cols]
        o = o * lax.rsqrt(jnp.mean(o * o, axis=-1, keepdims=True) + RMS_EPS) * onw_ref[:, cols]
        o_ref[0, pl.ds(r0, c), cols] = o
        return state_t

    def chunk(ck, states):
        r0 = pl.multiple_of(ck * c, c)
        return tuple(head_chunk(r0, hd, states[hd]) for hd in range(HGRN_HEADS))

    states = lax.fori_loop(0, n_chunks, chunk, tuple(state_scr[hd] for hd in range(HGRN_HEADS)))
    for hd in range(HGRN_HEADS):
        state_scr[hd] = states[hd]


def _hgrn(hq, hk, hlf, hv, hg, rec_out_norm_w, rows):
    b, s, _ = hq.shape
    cm = jnp.asarray(_hgrn_cum_matrix()).astype(BF16)
    blk = pl.BlockSpec((1, rows, HGRN_WIDTH), lambda bi, i: (bi, i, 0))
    return pl.pallas_call(
        functools.partial(_hgrn_kernel, n_chunks=rows // HGRN_CHUNK),
        grid=(b, s // rows),
        in_specs=[blk, blk, blk, blk, blk,
                  pl.BlockSpec((1, HGRN_WIDTH), lambda bi, i: (0, 0)),
                  pl.BlockSpec(cm.shape, lambda bi, i: (0, 0))],
        out_specs=blk,
        out_shape=jax.ShapeDtypeStruct((b, s, HGRN_WIDTH), F32),
        scratch_shapes=[pltpu.VMEM((HGRN_HEADS, HGRN_DIM, HGRN_DIM), F32)],
        compiler_params=_cparams("parallel", "arbitrary"),
        name="hgrn",
    )(hq, hk, hlf, hv, hg, rec_out_norm_w.reshape(1, HGRN_WIDTH), cm)


def _outproj_kernel(x_ref, a_ref, r_ref, wa_ref, wr_ref, gt_ref, sc_ref, sh_ref, n2_ref, x1_ref, h2_ref, h2p_ref):
    mixed = _dot(a_ref[0].astype(BF16), wa_ref[...]) + _dot(r_ref[0].astype(BF16), wr_ref[...])
    x1 = x_ref[0] + gt_ref[0] * mixed
    x1_ref[0] = x1
    ms = jnp.mean(x1 * x1, axis=-1, keepdims=True)
    h2 = x1 * lax.rsqrt(ms + RMS_EPS) * n2_ref[...] * (1.0 + sc_ref[0]) + sh_ref[0]
    h2_ref[0] = h2
    h2p_ref[0] = _pack_bf16_pair(h2[:, :D_MODEL // 2], h2[:, D_MODEL // 2:])


def _outproj(x, attn, rec, w_out, gt1, sc2, sh2, norm2_w, tm):
    b, s, d = x.shape
    row = lambda bi, i: (bi, i, 0)
    per_b = lambda bi, i: (bi, 0, 0)
    fixed2 = lambda bi, i: (0, 0)
    w = w_out.astype(BF16)
    return pl.pallas_call(
        _outproj_kernel,
        grid=(b, s // tm),
        in_specs=[pl.BlockSpec((1, tm, d), row),
                  pl.BlockSpec((1, tm, NSA_WIDTH), row),
                  pl.BlockSpec((1, tm, HGRN_WIDTH), row),
                  pl.BlockSpec((NSA_WIDTH, d), fixed2),
                  pl.BlockSpec((HGRN_WIDTH, d), fixed2),
                  pl.BlockSpec((1, 1, d), per_b),
                  pl.BlockSpec((1, 1, d), per_b),
                  pl.BlockSpec((1, 1, d), per_b),
                  pl.BlockSpec((1, d), fixed2)],
        out_specs=(pl.BlockSpec((1, tm, d), row), pl.BlockSpec((1, tm, d), row), pl.BlockSpec((1, tm, d // 2), row)),
        out_shape=(jax.ShapeDtypeStruct((b, s, d), F32), jax.ShapeDtypeStruct((b, s, d), F32),
                   jax.ShapeDtypeStruct((b, s, d // 2), jnp.uint32)),
        compiler_params=_cparams("parallel", "parallel"),
        name="outproj",
    )(x, attn, rec, w[:NSA_WIDTH], w[NSA_WIDTH:], gt1, sc2, sh2, norm2_w)


def _mixer(x, c, ada_w, ada_b, norm1_w, norm2_w, w_in, q_norm_w, k_norm_w, cmp_pos, cmp_w1, cmp_b1, cmp_w2,
           attn_out_norm_w, hgrn_lb_param, rec_out_norm_w, w_out):
    b, s, d = x.shape
    mod = _mod(c, ada_w, ada_b)
    sh1, sc1, gt1, sh2, sc2, gt2 = [m.reshape(b, 1, d) for m in jnp.split(mod, 6, axis=-1)]
    o = NSA_WIDTH + 6 * KV_WIDTH
    w_cat = jnp.concatenate([w_in[:, :o], w_in[:, o:o + NSA_HEADS * 3],
                             jnp.zeros((d, GATE_PAD - NSA_HEADS * 3), w_in.dtype),
                             w_in[:, o + NSA_HEADS * 3:]], axis=1).astype(BF16)
    tm = min(256, s)
    (q, kc_raw, vc_raw, ks, vst, kw, vwt, gates_t, hq, hk, hlf, hv, hg) = _inproj(
        x, sc1, sh1, norm1_w.reshape(1, d), w_cat, q_norm_w.reshape(1, HEAD_DIM), k_norm_w, hgrn_lb_param, tm)
    kc, vct = _compress(kc_raw, vc_raw, cmp_pos, cmp_w1, cmp_b1, cmp_w2, k_norm_w)
    attn = _nsa(q, kc, vct, ks, vst, kw, vwt, gates_t, attn_out_norm_w)
    rec = _hgrn(hq, hk, hlf, hv, hg, rec_out_norm_w, min(512, s))
    x1, h2, h2p = _outproj(x, attn, rec, w_out, gt1, sc2, sh2, norm2_w.reshape(1, d), tm)
    return x1, h2, h2p, gt2


def _router_kernel(h_ref, rwt_ref, bias_ref, tri_ref, ones_ref, idx_ref, w_ref, rank_ref, cnt_ref, carry_scr, *, tr):
    @pl.when(pl.program_id(0) == 0)
    def _():
        carry_scr[...] = jnp.zeros_like(carry_scr)

    h = h_ref[...]
    h_hi = h.astype(BF16)
    h_lo = (h - h_hi.astype(F32)).astype(BF16)
    logits = _dot_nt(rwt_ref[0], h_hi) + _dot_nt(rwt_ref[1], h_hi) + _dot_nt(rwt_ref[0], h_lo)
    scores = _sigmoid(logits)
    biased = scores + bias_ref[...]
    neg = -jnp.inf

    gs = []
    for g in range(N_GROUPS):
        sub = biased[g * GROUP_SIZE:(g + 1) * GROUP_SIZE, :]
        m1 = jnp.max(sub, axis=0, keepdims=True)
        dup = jnp.sum((sub == m1).astype(F32), axis=0, keepdims=True)
        m2 = jnp.max(jnp.where(sub < m1, sub, neg), axis=0, keepdims=True)
        gs.append(m1 + jnp.where(dup >= 2.0, m1, m2))
    parts = []
    for g in range(N_GROUPS):
        beaten = jnp.zeros_like(gs[g])
        for g2 in range(N_GROUPS):
            if g2 != g:
                beats = (gs[g2] >= gs[g]) if g2 < g else (gs[g2] > gs[g])
                beaten = beaten + beats.astype(F32)
        sub = biased[g * GROUP_SIZE:(g + 1) * GROUP_SIZE, :]
        parts.append(jnp.where(beaten < float(TOPK_GROUPS), sub, neg))
    cand = jnp.concatenate(parts, axis=0)

    rowf = lax.broadcasted_iota(I32, (N_EXPERTS, tr), 0).astype(F32)
    idx_rows, w_rows, hits = [], [], []
    multi = jnp.zeros((N_EXPERTS, tr), F32)
    for _ in range(TOP_K):
        mx = jnp.max(cand, axis=0, keepdims=True)
        first = jnp.min(jnp.where(cand == mx, rowf, float(N_EXPERTS)), axis=0, keepdims=True)
        hit = rowf == first
        idx_rows.append(first)
        w_rows.append(jnp.sum(jnp.where(hit, scores, 0.0), axis=0, keepdims=True))
        cand = jnp.where(hit, neg, cand)
        multi = jnp.where(hit, 1.0, multi)
    w = jnp.concatenate(w_rows, axis=0)
    w_ref[...] = w / jnp.sum(w, axis=0, keepdims=True) * ROUTED_SCALE
    idx = jnp.concatenate(idx_rows, axis=0)
    idx_ref[...] = idx.astype(I32)

    carry = carry_scr[...]
    mb = multi.astype(BF16)
    before = _dot(mb, tri_ref[...]) + jnp.concatenate([carry] * (tr // 128), axis=1)
    rank_rows = [jnp.sum(jnp.where(rowf == idx_rows[k], before, 0.0), axis=0, keepdims=True) for k in range(TOP_K)]
    rank_ref[...] = jnp.concatenate(rank_rows, axis=0).astype(I32)
    carry = carry + _dot(mb, ones_ref[...])
    carry_scr[...] = carry
    cnt_ref[...] = carry


def _router(h2, router_w, router_bias, tr):
    t, d = h2.shape
    tri = jnp.asarray(np.triu(np.ones((tr, tr), np.float32), 1)).astype(BF16)
    ones = jnp.ones((tr, 128), BF16)
    tok = pl.BlockSpec((TOP_K, tr), lambda i: (0, i))
    fixed = lambda i: (0, 0)
    rwt = router_w.T
    rwt_hi = rwt.astype(BF16)
    rwt_split = jnp.stack([rwt_hi, (rwt - rwt_hi.astype(F32)).astype(BF16)])
    return pl.pallas_call(
        functools.partial(_router_kernel, tr=tr),
        grid=(t // tr,),
        in_specs=[pl.BlockSpec((tr, d), lambda i: (i, 0)),
                  pl.BlockSpec((2, N_EXPERTS, d), lambda i: (0, 0, 0)),
                  pl.BlockSpec((N_EXPERTS, 1), fixed),
                  pl.BlockSpec((tr, tr), fixed),
                  pl.BlockSpec((tr, 128), fixed)],
        out_specs=(tok, tok, tok, pl.BlockSpec((N_EXPERTS, 128), fixed)),
        out_shape=(jax.ShapeDtypeStruct((TOP_K, t), I32), jax.ShapeDtypeStruct((TOP_K, t), F32),
                   jax.ShapeDtypeStruct((TOP_K, t), I32), jax.ShapeDtypeStruct((N_EXPERTS, 128), F32)),
        scratch_shapes=[pltpu.VMEM((N_EXPERTS, 128), F32)],
        compiler_params=_cparams("arbitrary"),
        name="router",
    )(h2, rwt_split, router_bias.reshape(N_EXPERTS, 1), tri, ones)


def _pack_bf16_pair(a, b):
    ua = lax.bitcast_convert_type(a.astype(BF16).astype(F32), jnp.uint32)
    ub = lax.bitcast_convert_type(b.astype(BF16).astype(F32), jnp.uint32)
    return ua | (ub >> 16)


def _unpack_bf16_pair(w):
    a = lax.bitcast_convert_type(w & jnp.uint32(0xFFFF0000), F32)
    b = lax.bitcast_convert_type(w << 16, F32)
    return a, b


def _slot_kernel(ps_ref, idx_ref, rank_ref, slot_ref):
    idx = idx_ref[...]

    def body(e, acc):
        return jnp.where(idx == e, ps_ref[e], acc)

    slot_ref[...] = lax.fori_loop(0, N_EXPERTS, body, jnp.zeros_like(idx)) + rank_ref[...]


def _slots(pad_start, idx, rank, tt):
    t = idx.shape[1]
    tok = pl.BlockSpec((TOP_K, tt), lambda i, ps: (0, i))
    return pl.pallas_call(
        _slot_kernel,
        grid_spec=pltpu.PrefetchScalarGridSpec(num_scalar_prefetch=1, grid=(t // tt,),
                                               in_specs=[tok, tok], out_specs=tok),
        out_shape=jax.ShapeDtypeStruct((TOP_K, t), I32),
        compiler_params=_cparams("parallel"),
        name="slots",
    )(pad_start, idx, rank)


SC_CORES = 2
SC_SUBCORES = 16
SC_CHUNK = 64


def _sc_mesh():
    return plsc.VectorSubcoreMesh(core_axis_name="c", subcore_axis_name="s")


def _sc_dispatch(h2p, slot_chunks, n_rows):
    t, dw = h2p.shape
    per = slot_chunks.shape[0] // (SC_CORES * SC_SUBCORES)

    def body(h_hbm, slot_hbm, xs_hbm, idx_v, rows_v, sem):
        wid = lax.axis_index("s") * SC_CORES + lax.axis_index("c")

        @pl.loop(0, per)
        def _(c):
            ch = wid * per + c
            pltpu.sync_copy(slot_hbm.at[ch], idx_v)
            pltpu.sync_copy(h_hbm.at[pl.ds(ch * SC_CHUNK, SC_CHUNK)], rows_v)
            copies = [pltpu.async_copy(rows_v, xs_hbm.at[idx_v.at[k]], sem) for k in range(TOP_K)]
            for cp in copies:
                cp.wait()

    return pl.kernel(
        body, out_type=jax.ShapeDtypeStruct((n_rows, dw), h2p.dtype), mesh=_sc_mesh(),
        scratch_types=[pltpu.VMEM((TOP_K, SC_CHUNK), I32), pltpu.VMEM((SC_CHUNK, dw), h2p.dtype),
                       pltpu.SemaphoreType.DMA],
    )(h2p, slot_chunks)


def _sc_gather(ys, slot_chunks, t):
    dw = ys.shape[1]
    per = slot_chunks.shape[0] // (SC_CORES * SC_SUBCORES)

    def body(ys_hbm, slot_hbm, yg_hbm, idx_v, rows_v, gsem, wsem):
        wid = lax.axis_index("s") * SC_CORES + lax.axis_index("c")

        @pl.loop(0, per)
        def _(c):
            ch = wid * per + c
            pltpu.sync_copy(slot_hbm.at[ch], idx_v)
            gathers = [None] * TOP_K
            writes = [None] * TOP_K
            gathers[0] = pltpu.async_copy(ys_hbm.at[idx_v.at[0]], rows_v.at[0], gsem)
            for k in range(TOP_K):
                gathers[k].wait()
                if k + 1 < TOP_K:
                    if k >= 1:
                        writes[k - 1].wait()
                    gathers[k + 1] = pltpu.async_copy(ys_hbm.at[idx_v.at[k + 1]], rows_v.at[(k + 1) % 2], gsem)
                writes[k] = pltpu.async_copy(rows_v.at[k % 2], yg_hbm.at[k, pl.ds(ch * SC_CHUNK, SC_CHUNK)], wsem)
            writes[TOP_K - 2].wait()
            writes[TOP_K - 1].wait()

    return pl.kernel(
        body, out_type=jax.ShapeDtypeStruct((TOP_K, t, dw), ys.dtype), mesh=_sc_mesh(),
        scratch_types=[pltpu.VMEM((TOP_K, SC_CHUNK), I32), pltpu.VMEM((2, SC_CHUNK, dw), ys.dtype),
                       pltpu.SemaphoreType.DMA, pltpu.SemaphoreType.DMA],
    )(ys, slot_chunks)


def _experts_kernel(be_ref, nu_ref, bv_ref, xs_hbm, wg_ref, wu_ref, wd_ref, ys_ref, xring, rsem):
    i = pl.program_id(0)
    half = D_MODEL // 2
    n_used = nu_ref[0]

    def fetch(blk):
        slot = blk % EXPERT_RING
        return pltpu.make_async_copy(xs_hbm.at[pl.ds(pl.multiple_of(blk * EXPERT_BLOCK, EXPERT_BLOCK), EXPERT_BLOCK)],
                                     xring.at[slot], rsem.at[slot])

    @pl.when(i == 0)
    def _():
        for first in range(EXPERT_RING - 1):
            @pl.when(first < n_used)
            def _():
                fetch(jnp.int32(first)).start()

    @pl.when(i + (EXPERT_RING - 1) < n_used)
    def _():
        fetch(i + (EXPERT_RING - 1)).start()

    @pl.when(i < n_used)
    def _():
        fetch(i).wait()

    xs_ref = xring.at[i % EXPERT_RING]

    def ffn(rows):
        live = lax.broadcasted_iota(I32, (rows, xs_ref.shape[1]), 0) < bv_ref[i]
        xa, xb = _unpack_bf16_pair(jnp.where(live, xs_ref[0:rows, :], jnp.uint32(0)))
        xa, xb = xa.astype(BF16), xb.astype(BF16)
        g = _dot(xa, wg_ref[0, :half].astype(BF16)) + _dot(xb, wg_ref[0, half:].astype(BF16))
        u = _dot(xa, wu_ref[0, :half].astype(BF16)) + _dot(xb, wu_ref[0, half:].astype(BF16))
        act = (g * _sigmoid(g) * u).astype(BF16)
        y = _dot(act, wd_ref[0].astype(BF16))
        ys_ref[0:rows, :] = _pack_bf16_pair(y[:, :half], y[:, half:])

    used = i < n_used
    short = bv_ref[i] <= EXPERT_TAIL

    @pl.when(used & jnp.logical_not(short))
    def _():
        ffn(EXPERT_BLOCK)

    @pl.when(used & short)
    def _():
        ffn(EXPERT_TAIL)
        ys_ref[EXPERT_TAIL:, :] = jnp.zeros((EXPERT_BLOCK - EXPERT_TAIL, ys_ref.shape[1]), ys_ref.dtype)

    @pl.when(jnp.logical_not(used))
    def _():
        ys_ref[...] = jnp.zeros_like(ys_ref)


def _experts(xs, blk_e, n_used, blk_valid, w_gate, w_up, w_down):
    n_rows, dw = xs.shape
    d = w_gate.shape[1]
    nblk = n_rows // EXPERT_BLOCK
    w_map = lambda i, be, nu, bv: (be[i], 0, 0)
    return pl.pallas_call(
        _experts_kernel,
        grid_spec=pltpu.PrefetchScalarGridSpec(
            num_scalar_prefetch=3,
            grid=(nblk,),
            in_specs=[pl.BlockSpec(memory_space=pl.ANY),
                      pl.BlockSpec((1, d, EXPERT_FF), w_map),
                      pl.BlockSpec((1, d, EXPERT_FF), w_map),
                      pl.BlockSpec((1, EXPERT_FF, d), w_map)],
            out_specs=pl.BlockSpec((EXPERT_BLOCK, dw), lambda i, be, nu, bv: (i, 0)),
            scratch_shapes=[pltpu.VMEM((EXPERT_RING, EXPERT_BLOCK, dw), xs.dtype),
                            pltpu.SemaphoreType.DMA((EXPERT_RING,))]),
        out_shape=jax.ShapeDtypeStruct((n_rows, dw), xs.dtype),
        compiler_params=pltpu.CompilerParams(dimension_semantics=("arbitrary",), vmem_limit_bytes=VMEM_LIMIT,
                                             has_side_effects=True),
        name="experts",
    )(blk_e, n_used, blk_valid, xs, w_gate, w_up, w_down)


def _combine_kernel(x1_ref, h_ref, w_ref, gt_ref, sg_ref, su_ref, sd_ref, yg_ref, o_ref):
    tc = x1_ref.shape[0]
    half = D_MODEL // 2
    hb = h_ref[...].astype(BF16)
    g = _dot(hb, sg_ref[...])
    u = _dot(hb, su_ref[...])
    ffn = _dot((g * _sigmoid(g) * u).astype(BF16), sd_ref[...])

    w = w_ref[...]
    ra = jnp.zeros((tc, half), F32)
    rb = jnp.zeros((tc, half), F32)
    for k in range(TOP_K):
        ya, yb = _unpack_bf16_pair(yg_ref[k])
        ra = ra + w[:, k:k + 1] * ya
        rb = rb + w[:, k:k + 1] * yb
    ffn = ffn + jnp.concatenate([ra, rb], axis=1)
    o_ref[...] = x1_ref[...] + gt_ref[0] * ffn


def _combine(x1, h2, w_tok, gt2, yg, sg, su, sd, seq, tc):
    t, d = x1.shape
    row = lambda i: (i, 0)
    fixed = lambda i: (0, 0)
    return pl.pallas_call(
        _combine_kernel,
        grid=(t // tc,),
        in_specs=[pl.BlockSpec((tc, d), row),
                  pl.BlockSpec((tc, d), row),
                  pl.BlockSpec((tc, TOP_K), row),
                  pl.BlockSpec((1, 1, d), lambda i: ((i * tc) // seq, 0, 0)),
                  pl.BlockSpec((d, SHARED_FF), fixed),
                  pl.BlockSpec((d, SHARED_FF), fixed),
                  pl.BlockSpec((SHARED_FF, d), fixed),
                  pl.BlockSpec((TOP_K, tc, d // 2), lambda i: (0, i, 0))],
        out_specs=pl.BlockSpec((tc, d), row),
        out_shape=jax.ShapeDtypeStruct((t, d), F32),
        compiler_params=_cparams("parallel"),
        name="combine",
    )(x1, h2, w_tok, gt2, sg.astype(BF16), su.astype(BF16), sd.astype(BF16), yg)


def _moe_parts(x1, h2, h2p, gt2, router_w, router_bias, w_gate, w_up, w_down, sg, su, sd):
    b, s, d = x1.shape
    t = b * s
    h2 = h2.reshape(t, d)
    idx, w, rank, cnt = _router(h2, router_w, router_bias, min(256, t))
    counts = cnt[:, 0].astype(I32)
    padded = (counts + EXPERT_BLOCK - 1) // EXPERT_BLOCK * EXPERT_BLOCK
    pad_end = jnp.cumsum(padded)
    pad_start = pad_end - padded
    n_rows = t * TOP_K + N_EXPERTS * EXPERT_BLOCK
    nblk = n_rows // EXPERT_BLOCK
    n_used = (pad_end[-1:] // EXPERT_BLOCK).astype(I32)
    blk_start = jnp.arange(nblk, dtype=I32) * EXPERT_BLOCK
    owns = (pad_start[None, :] <= blk_start[:, None]) & (blk_start[:, None] < pad_end[None, :])
    e_ids = jnp.arange(N_EXPERTS, dtype=I32)[None, :]
    last_e = jnp.max(jnp.where(counts > 0, e_ids[0], 0))
    blk_e = jnp.where(blk_start < pad_end[-1], jnp.sum(jnp.where(owns, e_ids, 0), axis=1), last_e).astype(I32)
    rows_left = jnp.sum(jnp.where(owns, (pad_start + counts)[None, :] - blk_start[:, None], 0), axis=1)
    blk_valid = jnp.clip(rows_left, 0, EXPERT_BLOCK).astype(I32)
    slot = _slots(pad_start.astype(I32), idx, rank, min(2048, t))
    slot_chunks = slot.reshape(TOP_K, t // SC_CHUNK, SC_CHUNK).transpose(1, 0, 2)
    xs = _sc_dispatch(h2p.reshape(t, d // 2), slot_chunks, n_rows)
    ys = _experts(xs, blk_e, n_used, blk_valid, w_gate, w_up, w_down)
    yg = _sc_gather(ys, slot_chunks, t)
    out = _combine(x1.reshape(t, d), h2, w.T, gt2, yg, sg, su, sd, s, min(256, t))
    return out.reshape(b, s, d), dict(idx=idx, w=w, rank=rank, cnt=cnt)


def kernel(x, c, ada_w, ada_b, norm1_w, norm2_w, w_in, q_norm_w, k_norm_w, cmp_pos, cmp_w1, cmp_b1, cmp_w2, attn_out_norm_w, hgrn_lb_param, rec_out_norm_w, w_out, router_w, router_bias, exp_w_gate, exp_w_up, exp_w_down, shared_w_gate, shared_w_up, shared_w_down):
    assert ada_w.shape[0] == 1, "one layer"
    assert x.shape[0] <= 8 and x.shape[1] % TK == 0 and x.shape[1] >= WINDOW + TQ
    l = 0
    x1, h2, h2p, gt2 = _mixer(x, c, ada_w[l], ada_b[l], norm1_w[l], norm2_w[l], w_in[l], q_norm_w[l], k_norm_w[l],
                         cmp_pos[l], cmp_w1[l], cmp_b1[l], cmp_w2[l], attn_out_norm_w[l], hgrn_lb_param,
                         rec_out_norm_w[l], w_out[l])
    out, _ = _moe_parts(x1, h2, h2p, gt2, router_w[l], router_bias[l], exp_w_gate[l], exp_w_up[l], exp_w_down[l],
                        shared_w_gate[l], shared_w_up[l], shared_w_down[l])
    return out
```

```python
import functools

import numpy as np
import jax
import jax.numpy as jnp
from jax import lax
from jax.experimental import pallas as pl
from jax.experimental.pallas import tpu as pltpu
from jax.experimental.pallas import tpu_sc as plsc

F32 = jnp.float32
BF16 = jnp.bfloat16
I32 = jnp.int32

D_MODEL = 1024
NSA_HEADS = 8
HEAD_DIM = 64
NSA_WIDTH = NSA_HEADS * HEAD_DIM
KV_HEADS = 2
HEADS_PER_KV = NSA_HEADS // KV_HEADS
KV_WIDTH = KV_HEADS * HEAD_DIM
CMP_BLOCK = 32
CMP_STRIDE = 16
CMP_HIDDEN = 256
SEL_BLOCK = 64
N_SELECT = 16
WINDOW = 512
HGRN_HEADS = 4
HGRN_DIM = 128
HGRN_WIDTH = HGRN_HEADS * HGRN_DIM
HGRN_CHUNK = 64
HGRN_SUB = 16
HGRN_LEVELS = ()
HGRN_LEAF = 16
N_EXPERTS = 256
TOP_K = 8
N_GROUPS = 8
GROUP_SIZE = N_EXPERTS // N_GROUPS
TOPK_GROUPS = 4
EXPERT_FF = 256
SHARED_FF = 256
ROUTED_SCALE = 2.5
RMS_EPS = 1e-6
BIG = 1e9
LOG2E = 1.4426950408889634
GATE_PAD = 128
PROJ_COLS = NSA_WIDTH + 6 * KV_WIDTH + GATE_PAD + 4 * HGRN_WIDTH

VMEM_LIMIT = 56 * 1024 * 1024

TQ = 256
TK = 512
EXPERT_BLOCK = 512
EXPERT_TAIL = 128
EXPERT_RING = 3
HIGHEST = lax.Precision.HIGHEST


def _cparams(*sem):
    return pltpu.CompilerParams(dimension_semantics=sem, vmem_limit_bytes=VMEM_LIMIT)


def _sigmoid(x):
    return 1.0 / (1.0 + jnp.exp(-x))


def _dot_nt(a, b):
    return lax.dot_general(a, b, (((1,), (1,)), ((), ())), preferred_element_type=F32)


def _dot(a, b, **kw):
    return jnp.dot(a, b, preferred_element_type=F32, **kw)


def _split_dot(a_bf16_exact, x):
    hi = x.astype(BF16)
    lo = (x - hi.astype(F32)).astype(BF16)
    return _dot(a_bf16_exact, hi) + _dot(a_bf16_exact, lo)


def _mod_kernel(c_ref, w_ref, b_ref, o_ref):
    c = c_ref[...]
    cond = c * _sigmoid(c)
    o_ref[...] = _dot(cond, w_ref[...], precision=HIGHEST) + b_ref[...]


def _mod(c, ada_w, ada_b):
    b, d = c.shape
    rows = 8
    c_pad = jnp.zeros((rows, d), F32).at[:b].set(c)
    n = ada_w.shape[1]
    out = pl.pallas_call(
        _mod_kernel,
        grid=(n // d,),
        in_specs=[pl.BlockSpec((rows, d), lambda j: (0, 0)),
                  pl.BlockSpec((d, d), lambda j: (0, j)),
                  pl.BlockSpec((1, d), lambda j: (0, j))],
        out_specs=pl.BlockSpec((rows, d), lambda j: (0, j)),
        out_shape=jax.ShapeDtypeStruct((rows, n), F32),
        compiler_params=_cparams("parallel"),
        name="mod",
    )(c_pad, ada_w, ada_b.reshape(1, n))
    return out[:b]


def _head_rms(t, w):
    return t * lax.rsqrt(jnp.mean(t * t, axis=-1, keepdims=True) + RMS_EPS) * w


def _pos_digits(pos):
    lane = lax.broadcasted_iota(I32, pos.shape, 1)
    d0 = (lane == 0) | (lane == 3) | (lane == 6)
    d1 = (lane == 1) | (lane == 4) | (lane == 7)
    d2 = (lane == 2) | (lane == 5) | (lane == 8)
    dig = jnp.where(d0, pos >> 12, jnp.where(d1, (pos >> 6) & 63, jnp.where(d2, pos & 63, 0)))
    return dig.astype(F32)


def _inproj_kernel(x_ref, sc_ref, sh_ref, n1_ref, w_ref, qnw_ref, knw_ref, lbp_ref, qaug_ref,
                   q_ref, kcr_ref, vcr_ref, ks_ref, vst_ref, kw_ref, vwt_ref, gt_ref,
                   hq_ref, hk_ref, hlf_ref, hv_ref, hg_ref):
    x = x_ref[0]
    ms = jnp.mean(x * x, axis=-1, keepdims=True)
    h = x * lax.rsqrt(ms + RMS_EPS) * n1_ref[...] * (1.0 + sc_ref[0]) + sh_ref[0]
    p = _dot(h.astype(BF16), w_ref[...])
    tm = x.shape[0]

    qnw = qnw_ref[...]
    for hd in range(NSA_HEADS):
        t = p[:, hd * HEAD_DIM:(hd + 1) * HEAD_DIM]
        qn = _head_rms(t, qnw) * (HEAD_DIM ** -0.5 * LOG2E)
        qa = jnp.broadcast_to(qaug_ref[hd:hd + 1, :], (tm, HEAD_DIM))
        q_ref[0, hd] = jnp.concatenate([qn, qa], axis=1).astype(BF16)
    kaug = _pos_digits(pl.program_id(1) * tm + lax.broadcasted_iota(I32, (tm, HEAD_DIM), 0))

    o = NSA_WIDTH
    kcr_ref[0] = p[:, o:o + KV_WIDTH]
    vcr_ref[0] = p[:, o + KV_WIDTH:o + 2 * KV_WIDTH]
    ks = p[:, o + 2 * KV_WIDTH:o + 3 * KV_WIDTH]
    vs = p[:, o + 3 * KV_WIDTH:o + 4 * KV_WIDTH]
    kw = p[:, o + 4 * KV_WIDTH:o + 5 * KV_WIDTH]
    vw = p[:, o + 5 * KV_WIDTH:o + 6 * KV_WIDTH]
    for g in range(KV_HEADS):
        sl = slice(g * HEAD_DIM, (g + 1) * HEAD_DIM)
        ks_ref[0, g] = jnp.concatenate([_head_rms(ks[:, sl], knw_ref[1:2, :]), kaug], axis=1).astype(BF16)
        kw_ref[0, g] = jnp.concatenate([_head_rms(kw[:, sl], knw_ref[2:3, :]), kaug], axis=1).astype(BF16)
    vst = vs.T.astype(BF16)
    vwt = vw.T.astype(BF16)
    for g in range(KV_HEADS):
        vst_ref[0, g] = vst[g * HEAD_DIM:(g + 1) * HEAD_DIM, :]
        vwt_ref[0, g] = vwt[g * HEAD_DIM:(g + 1) * HEAD_DIM, :]

    o = NSA_WIDTH + 6 * KV_WIDTH
    gates = _sigmoid(p[:, o:o + GATE_PAD])
    gt_ref[0] = gates.T[:NSA_HEADS * 3, :]

    o = o + GATE_PAD
    hq = p[:, o:o + HGRN_WIDTH]
    hf = p[:, o + HGRN_WIDTH:o + 2 * HGRN_WIDTH]
    hi = p[:, o + 2 * HGRN_WIDTH:o + 3 * HGRN_WIDTH]
    hg = p[:, o + 3 * HGRN_WIDTH:o + 4 * HGRN_WIDTH]
    lbp = lbp_ref[...]
    e = jnp.exp(lbp - jnp.max(lbp, axis=0, keepdims=True))
    lb = e[0:1, :] / jnp.sum(e, axis=0, keepdims=True)
    f = lb + (1.0 - lb) * _sigmoid(hf)
    hq_ref[0] = hq * _sigmoid(hq) * (HGRN_DIM ** -0.5)
    hk_ref[0] = 1.0 - f
    hlf_ref[0] = jnp.log(f)
    hv_ref[0] = hi
    hg_ref[0] = _sigmoid(hg)


def _inproj(x, sc1, sh1, norm1_w, w_cat, q_norm_w, k_norm_w, lb_param, tm):
    b, s, d = x.shape
    row = lambda bi, i: (bi, i, 0)
    per_b = lambda bi, i: (bi, 0, 0)
    fixed2 = lambda bi, i: (0, 0)
    aw = 2 * HEAD_DIM
    rest = np.array([2.0 ** (-8.0 * (i + 1) / NSA_HEADS) for i in range(NSA_HEADS)], np.float64) * LOG2E
    qaug = np.zeros((NSA_HEADS, HEAD_DIM), np.float32)
    for i in range(3):
        term = rest.astype(np.float32).astype(BF16).astype(np.float64)
        rest = rest - term
        for dgt, wgt in enumerate((4096.0, 64.0, 1.0)):
            qaug[:, 3 * i + dgt] = term * wgt
    assert np.all(qaug == qaug.astype(BF16).astype(np.float32))
    out_shape = (
        jax.ShapeDtypeStruct((b, NSA_HEADS, s, aw), BF16),
        jax.ShapeDtypeStruct((b, s, KV_WIDTH), F32),
        jax.ShapeDtypeStruct((b, s, KV_WIDTH), F32),
        jax.ShapeDtypeStruct((b, KV_HEADS, s, aw), BF16),
        jax.ShapeDtypeStruct((b, KV_HEADS, HEAD_DIM, s), BF16),
        jax.ShapeDtypeStruct((b, KV_HEADS, s, aw), BF16),
        jax.ShapeDtypeStruct((b, KV_HEADS, HEAD_DIM, s), BF16),
        jax.ShapeDtypeStruct((b, NSA_HEADS * 3, s), F32),
    ) + tuple(jax.ShapeDtypeStruct((b, s, HGRN_WIDTH), F32) for _ in range(5))
    hm = lambda n, w: pl.BlockSpec((1, n, tm, w), lambda bi, i: (bi, 0, i, 0))
    hmt = lambda n, w: pl.BlockSpec((1, n, w, tm), lambda bi, i: (bi, 0, 0, i))
    out_specs = (
        hm(NSA_HEADS, aw),
        pl.BlockSpec((1, tm, KV_WIDTH), row),
        pl.BlockSpec((1, tm, KV_WIDTH), row),
        hm(KV_HEADS, aw), hmt(KV_HEADS, HEAD_DIM),
        hm(KV_HEADS, aw), hmt(KV_HEADS, HEAD_DIM),
        pl.BlockSpec((1, NSA_HEADS * 3, tm), lambda bi, i: (bi, 0, i)),
    ) + tuple(pl.BlockSpec((1, tm, HGRN_WIDTH), row) for _ in range(5))
    return pl.pallas_call(
        _inproj_kernel,
        grid=(b, s // tm),
        in_specs=[pl.BlockSpec((1, tm, d), row),
                  pl.BlockSpec((1, 1, d), per_b),
                  pl.BlockSpec((1, 1, d), per_b),
                  pl.BlockSpec((1, d), fixed2),
                  pl.BlockSpec((d, PROJ_COLS), fixed2),
                  pl.BlockSpec((1, HEAD_DIM), fixed2),
                  pl.BlockSpec((3, HEAD_DIM), fixed2),
                  pl.BlockSpec(lb_param.shape, fixed2),
                  pl.BlockSpec((NSA_HEADS, HEAD_DIM), fixed2)],
        out_specs=out_specs,
        out_shape=out_shape,
        compiler_params=_cparams("parallel", "parallel"),
        name="inproj",
    )(x, sc1, sh1, norm1_w, w_cat, q_norm_w, k_norm_w, lb_param, jnp.asarray(qaug))


def _gelu_tanh(x):
    return 0.5 * x * (1.0 + jnp.tanh(0.7978845608028654 * (x + 0.044715 * x * x * x)))


def _compress_kernel(kch_ref, vch_ref, pos_ref, wa_ref, wb_ref, b1_ref, w2_ref, knw_ref,
                     kc_ref, vct_ref):
    n = kch_ref.shape[1]
    outs = []
    for br, ch_ref in enumerate((kch_ref, vch_ref)):
        ch = ch_ref[0]
        a = _dot((ch + pos_ref[br, 0:1, :]).astype(BF16), wa_ref[br])
        bm = _dot((ch + pos_ref[br, 1:2, :]).astype(BF16), wb_ref[br])
        pre = a + pltpu.roll(bm, n - 1, 0) + b1_ref[br]
        hid = _gelu_tanh(pre).astype(BF16)
        outs.append([_dot(hid[:, g * CMP_HIDDEN:(g + 1) * CMP_HIDDEN], w2_ref[br]) for g in range(KV_HEADS)])
    end_digits = _pos_digits(lax.broadcasted_iota(I32, (n, HEAD_DIM), 0) * CMP_STRIDE + (CMP_BLOCK - 1))
    for g in range(KV_HEADS):
        kc_ref[0, g] = jnp.concatenate([_head_rms(outs[0][g], knw_ref[0:1, :]), end_digits], axis=1).astype(BF16)
    vct = jnp.concatenate(outs[1], axis=1).T.astype(BF16)
    for g in range(KV_HEADS):
        vct_ref[0, g] = vct[g * HEAD_DIM:(g + 1) * HEAD_DIM, :]


def _compress(kc_raw, vc_raw, cmp_pos, cmp_w1, cmp_b1, cmp_w2, k_norm_w):
    b, s, _ = kc_raw.shape
    n = s // CMP_STRIDE
    half = CMP_STRIDE
    cw = CMP_STRIDE * KV_WIDTH
    kch = kc_raw.reshape(b, n, cw)
    vch = vc_raw.reshape(b, n, cw)
    pos = cmp_pos.reshape(2, 2, half, 1, HEAD_DIM)
    pos = jnp.broadcast_to(pos, (2, 2, half, KV_HEADS, HEAD_DIM)).reshape(2, 2, cw)
    w1 = cmp_w1.reshape(2, 2, half, HEAD_DIM, CMP_HIDDEN)
    eye = jnp.eye(KV_HEADS, dtype=F32)
    wfull = jnp.einsum('rhjdn,gk->rhjgdkn', w1, eye).reshape(2, 2, cw, KV_HEADS * CMP_HIDDEN).astype(BF16)
    b1 = jnp.tile(cmp_b1.reshape(2, 1, CMP_HIDDEN), (1, 1, KV_HEADS))
    fix = lambda r: (lambda bi: (0,) * r)
    return pl.pallas_call(
        _compress_kernel,
        grid=(b,),
        in_specs=[pl.BlockSpec((1, n, cw), lambda bi: (bi, 0, 0)),
                  pl.BlockSpec((1, n, cw), lambda bi: (bi, 0, 0)),
                  pl.BlockSpec((2, 2, cw), fix(3)),
                  pl.BlockSpec((2, cw, KV_HEADS * CMP_HIDDEN), fix(3)),
                  pl.BlockSpec((2, cw, KV_HEADS * CMP_HIDDEN), fix(3)),
                  pl.BlockSpec((2, 1, KV_HEADS * CMP_HIDDEN), fix(3)),
                  pl.BlockSpec((2, CMP_HIDDEN, HEAD_DIM), fix(3)),
                  pl.BlockSpec((3, HEAD_DIM), fix(2))],
        out_specs=(pl.BlockSpec((1, KV_HEADS, n, 2 * HEAD_DIM), lambda bi: (bi, 0, 0, 0)),
                   pl.BlockSpec((1, KV_HEADS, HEAD_DIM, n), lambda bi: (bi, 0, 0, 0))),
        out_shape=(jax.ShapeDtypeStruct((b, KV_HEADS, n, 2 * HEAD_DIM), BF16),
                   jax.ShapeDtypeStruct((b, KV_HEADS, HEAD_DIM, n), BF16)),
        compiler_params=_cparams("parallel"),
        name="compress",
    )(kch, vch, pos, wfull[:, 0], wfull[:, 1], b1, cmp_w2.astype(BF16), k_norm_w)


def _nsa_kernel(q_ref, kc_ref, vct_ref, ks_ref, vst_ref, kw_ref, vwt_ref, gt_ref, cdiff_ref, wdiff_ref,
                ovl_ref, oh_ref, onw_ref, wmask_ref, o_ref, buf_a, buf_b, m_scr, acc_scr, *, n_top):
    q0 = pl.program_id(2) * TQ
    ncols = HEADS_PER_KV * TQ
    q = q_ref[0].reshape(ncols, 2 * HEAD_DIM)
    ns = ovl_ref.shape[0]

    s = jnp.where(cdiff_ref[...] <= q0, _dot_nt(kc_ref[0, 0], q), -jnp.inf)
    m = jnp.max(s, axis=0, keepdims=True)
    m = jnp.where(m == -jnp.inf, 0.0, m)
    e = jnp.exp2(s - m)
    p = e / jnp.maximum(jnp.sum(e, axis=0, keepdims=True), 1e-30)
    o_c = _dot(vct_ref[0, 0], p.astype(BF16))

    psum = p[:, 0:TQ]
    for hh in range(1, HEADS_PER_KV):
        psum = psum + p[:, hh * TQ:(hh + 1) * TQ]
    imp = _split_dot(ovl_ref[...], psum)
    blk = lax.broadcasted_iota(I32, (ns, TQ), 0)
    tq = q0 + lax.broadcasted_iota(I32, (ns, TQ), 1)
    cur = tq >> 6
    forced = (blk == 0) | (blk == cur) | (blk == cur - 1)
    rank = jnp.where(forced, BIG, jnp.where(blk * SEL_BLOCK <= tq, imp, -BIG))

    blkf = blk.astype(F32)

    bias = jnp.full((ns, TQ), -1e30, F32)
    for _ in range(n_top):
        mx = jnp.max(rank, axis=0, keepdims=True)
        first = jnp.min(jnp.where(rank == mx, blkf, float(ns)), axis=0, keepdims=True)
        hit = blkf == first
        rank = jnp.where(hit, -jnp.inf, rank)
        bias = jnp.where(hit, 0.0, bias)

    if ns < 128:
        bias = jnp.concatenate([bias, jnp.zeros((128 - ns, TQ), F32)], axis=0)
    bias_t = bias.T.astype(BF16)
    qq = jnp.concatenate([q, jnp.concatenate([bias_t] * HEADS_PER_KV, axis=0)], axis=1)
    ones_rows = jnp.ones((16, TK), BF16)

    def scores(j):
        k0 = pl.multiple_of(j * TK, TK)
        kk = jnp.concatenate([ks_ref[0, 0, pl.ds(k0, TK), :], oh_ref[pl.ds(k0, TK), :]], axis=1)
        return _dot_nt(kk, qq)

    def consume(buf, j, causal, part):
        sc = buf[...]
        if causal:
            sc = jnp.where(wdiff_ref[0:TK, :] + (q0 - j * TK) >= 0, sc, -1e30)
        k0 = pl.multiple_of(j * TK, TK)
        m_run = m_scr[part]
        m_new = jnp.maximum(m_run, jnp.max(sc, axis=0, keepdims=True))
        ex = jnp.exp2(sc - m_new).astype(BF16)
        va = jnp.concatenate([vst_ref[0, 0, :, pl.ds(k0, TK)], ones_rows], axis=0)
        acc_scr[part] = jnp.exp2(m_run - m_new) * acc_scr[part] + _dot(va, ex)
        m_scr[part] = m_new

    n_past = q0 // TK
    m_scr[...] = jnp.full(m_scr.shape, -1e30, F32)
    acc_scr[...] = jnp.zeros(acc_scr.shape, F32)
    buf_a[...] = scores(0)

    nw = WINDOW + TQ
    start = pl.multiple_of(jnp.maximum(q0 - WINDOW, 0), TQ)
    sw = _dot_nt(kw_ref[0, 0, pl.ds(start, nw), :], q) + wmask_ref[0]
    ew = jnp.exp2(sw - jnp.max(sw, axis=0, keepdims=True))
    vw_aug = jnp.concatenate([vwt_ref[0, 0, :, pl.ds(start, nw)], jnp.ones((16, nw), BF16)], axis=0)
    acc_w = _dot(vw_aug, ew.astype(BF16))
    o_w = acc_w[0:HEAD_DIM, :] / acc_w[HEAD_DIM:HEAD_DIM + 1, :]

    def tiles(first, count):
        for u in range(0, count, 2):
            buf_b[...] = scores(first + u + 1)
            consume(buf_a, first + u, False, 0)
            buf_a[...] = scores(first + u + 2)
            consume(buf_b, first + u + 1, False, 1)
        return 0

    lax.fori_loop(0, n_past // 4, lambda i, _: tiles(4 * i, 4), 0)
    lax.fori_loop(0, (n_past // 2) % 2, lambda i, _: tiles((n_past // 4) * 4, 2), 0)

    @pl.when(n_past % 2 == 1)
    def _():
        buf_b[...] = scores(n_past)
        consume(buf_a, n_past - 1, False, 0)
        consume(buf_b, n_past, True, 1)

    @pl.when(n_past % 2 == 0)
    def _():
        consume(buf_a, n_past, True, 0)

    m_all = jnp.maximum(m_scr[0], m_scr[1])
    acc_s = jnp.exp2(m_scr[0] - m_all) * acc_scr[0] + jnp.exp2(m_scr[1] - m_all) * acc_scr[1]
    o_s = acc_s[0:HEAD_DIM, :] / acc_s[HEAD_DIM:HEAD_DIM + 1, :]

    gt = gt_ref[0, 0]
    outs = []
    for hh in range(HEADS_PER_KV):
        cs = slice(hh * TQ, (hh + 1) * TQ)
        o = (gt[3 * hh:3 * hh + 1, :] * o_c[:, cs] + gt[3 * hh + 1:3 * hh + 2, :] * o_s[:, cs]
             + gt[3 * hh + 2:3 * hh + 3, :] * o_w[:, cs])
        o = o * lax.rsqrt(jnp.mean(o * o, axis=0, keepdims=True) + RMS_EPS) * onw_ref[0, hh]
        outs.append(o)
    o_ref[0] = jnp.concatenate(outs, axis=0).T


def _nsa(q, kc, vct, ks, vst, kw, vwt, gates_t, attn_out_norm_w):
    b, _, s, aw = q.shape
    nc = kc.shape[2]
    ns = s // SEL_BLOCK
    n_top = min(N_SELECT, ns)
    ncols = HEADS_PER_KV * TQ
    nw = WINDOW + TQ
    tl = np.arange(ncols)[None, :] & (TQ - 1)
    cdiff = jnp.asarray((np.arange(nc)[:, None] * CMP_STRIDE + (CMP_BLOCK - 1) - tl).astype(np.int32))
    wdiff_np = (tl - np.arange(nw)[:, None]).astype(np.int32)
    wdiff = jnp.asarray(wdiff_np)
    n_off = WINDOW // TQ + 1
    dist_np = wdiff_np[None] + (np.arange(n_off) * TQ)[:, None, None]
    wmask = jnp.asarray(np.where((dist_np >= 0) & (dist_np < WINDOW), 0.0, -np.inf).astype(np.float32))
    ci = np.arange(nc)[None, :] * CMP_STRIDE
    bj = np.arange(ns)[:, None]
    ovl = ((ci < (bj + 1) * SEL_BLOCK) & (ci + CMP_BLOCK > bj * SEL_BLOCK) & (np.arange(nc)[None, :] < nc - 1))
    ovl = jnp.asarray(ovl.astype(np.float32)).astype(BF16)
    assert ns <= 128
    onehot = (np.arange(s)[:, None] // SEL_BLOCK == np.arange(128)[None, :])
    onehot = jnp.asarray(onehot.astype(np.float32)).astype(BF16)
    onw = jnp.broadcast_to(attn_out_norm_w.reshape(KV_HEADS, HEADS_PER_KV, HEAD_DIM, 1),
                           (KV_HEADS, HEADS_PER_KV, HEAD_DIM, TQ))
    gt = gates_t.reshape(b, KV_HEADS, HEADS_PER_KV * 3, s)
    per_bg = lambda bi, g, i: (bi, g, 0, 0)
    fixed = lambda bi, g, i: (0, 0)
    return pl.pallas_call(
        functools.partial(_nsa_kernel, n_top=n_top),
        grid=(b, KV_HEADS, s // TQ),
        in_specs=[pl.BlockSpec((1, HEADS_PER_KV, TQ, aw), lambda bi, g, i: (bi, g, i, 0)),
                  pl.BlockSpec((1, 1, nc, aw), per_bg),
                  pl.BlockSpec((1, 1, HEAD_DIM, nc), per_bg),
                  pl.BlockSpec((1, 1, s, aw), per_bg),
                  pl.BlockSpec((1, 1, HEAD_DIM, s), per_bg),
                  pl.BlockSpec((1, 1, s, aw), per_bg),
                  pl.BlockSpec((1, 1, HEAD_DIM, s), per_bg),
                  pl.BlockSpec((1, 1, HEADS_PER_KV * 3, TQ), lambda bi, g, i: (bi, g, 0, i)),
                  pl.BlockSpec((nc, ncols), fixed, pipeline_mode=pl.Buffered(1)),
                  pl.BlockSpec((nw, ncols), fixed, pipeline_mode=pl.Buffered(1)),
                  pl.BlockSpec((ns, nc), fixed, pipeline_mode=pl.Buffered(1)),
                  pl.BlockSpec((s, 128), fixed, pipeline_mode=pl.Buffered(1)),
                  pl.BlockSpec((1, HEADS_PER_KV, HEAD_DIM, TQ), lambda bi, g, i: (g, 0, 0, 0)),
                  pl.BlockSpec((1, nw, ncols), lambda bi, g, i: (jnp.minimum(i, n_off - 1), 0, 0))],
        out_specs=pl.BlockSpec((1, TQ, HEADS_PER_KV * HEAD_DIM), lambda bi, g, i: (bi, i, g)),
        out_shape=jax.ShapeDtypeStruct((b, s, NSA_WIDTH), F32),
        scratch_shapes=[pltpu.VMEM((TK, ncols), F32), pltpu.VMEM((TK, ncols), F32),
                        pltpu.VMEM((2, 1, ncols), F32), pltpu.VMEM((2, HEAD_DIM + 16, ncols), F32)],
        compiler_params=_cparams("parallel", "parallel", "arbitrary"),
        name="nsa",
    )(q, kc, vct, ks, vst, kw, vwt, gt, cdiff, wdiff, ovl, onehot, onw, wmask)


def _hgrn_cum_matrix():
    c = HGRN_CHUNK
    t = np.arange(c)
    mats = [(t[None, :] <= t[:, None])]
    for half in HGRN_LEVELS:
        ref = (t & ~(2 * half - 1)) + half - 1
        mats.append(t[None, :] <= ref[:, None])
    return np.concatenate(mats, axis=0).astype(np.float32)


def _hgrn_kernel(q_ref, k_ref, lf_ref, v_ref, g_ref, onw_ref, cm_ref, o_ref, state_scr, *, n_chunks):
    c = HGRN_CHUNK

    @pl.when(pl.program_id(1) == 0)
    def _():
        state_scr[...] = jnp.zeros_like(state_scr)

    ri = lax.broadcasted_iota(I32, (c, c), 0)
    ci = lax.broadcasted_iota(I32, (c, c), 1)
    rsub = ri // HGRN_SUB
    level_masks = [((ri & ~(2 * h - 1)) == (ci & ~(2 * h - 1))) & ((ri & h) != 0) & ((ci & h) == 0)
                   for h in HGRN_LEVELS]
    diag = ri == ci

    def head_chunk(r0, hd, state_t):
        cols = slice(hd * HGRN_DIM, (hd + 1) * HGRN_DIM)
        q = q_ref[0, pl.ds(r0, c), cols]
        k = k_ref[0, pl.ds(r0, c), cols]
        lf = lf_ref[0, pl.ds(r0, c), cols] * LOG2E
        v = v_ref[0, pl.ds(r0, c), cols]
        cm = cm_ref[...]
        l1 = lf.astype(BF16)
        rest = lf - l1.astype(F32)
        l2 = rest.astype(BF16)
        l3 = (rest - l2.astype(F32)).astype(BF16)
        cums = _dot(cm, l1) + _dot(cm, l2) + _dot(cm, l3)
        cum = cums[0:c]
        o = _dot_nt((q * jnp.exp2(cum)).astype(BF16), state_t.astype(BF16))
        scores = jnp.where(diag, jnp.sum(q * k, axis=-1, keepdims=True), 0.0)

        def factored(ref, mask, acc):
            qs = q * jnp.exp2(jnp.minimum(cum - ref, 0.0))
            kd = k * jnp.exp2(jnp.minimum(ref - cum, 0.0))
            return jnp.where(mask, _dot_nt(qs.astype(BF16), kd.astype(BF16)), acc)

        for i in range(1, c // HGRN_SUB):
            scores = factored(cum[i * HGRN_SUB - 1:i * HGRN_SUB, :], (rsub == i) & (ci < i * HGRN_SUB), scores)
        for lv in range(len(HGRN_LEVELS)):
            scores = factored(cums[(lv + 1) * c:(lv + 2) * c], level_masks[lv], scores)
        for d in range(1, HGRN_LEAF):
            ksh = pltpu.roll(k, d, 0)
            csh = pltpu.roll(cum, d, 0)
            w = jnp.sum(q * ksh * jnp.exp2(cum - csh), axis=-1, keepdims=True)
            scores = jnp.where((ri - ci == d) & ((ri & (HGRN_LEAF - 1)) >= d), w, scores)
        o = o + _dot(scores.astype(BF16), v.astype(BF16))
        last = cum[c - 1:c, :]
        kd = (k * jnp.exp2(last - cum)).astype(BF16)
        state_t = state_t * jnp.exp2(last) + _dot(v.T.astype(BF16), kd)
        o = o * g_ref[0, pl.ds(r0, c), cols]
        o = o * lax.rsqrt(jnp.mean(o * o, axis=-1, keepdims=True) + RMS_EPS) * onw_ref[:, cols]
        o_ref[0, pl.ds(r0, c), cols] = o
        return state_t

    def chunk(ck, states):
        r0 = pl.multiple_of(ck * c, c)
        return tuple(head_chunk(r0, hd, states[hd]) for hd in range(HGRN_HEADS))

    states = lax.fori_loop(0, n_chunks, chunk, tuple(state_scr[hd] for hd in range(HGRN_HEADS)))
    for hd in range(HGRN_HEADS):
        state_scr[hd] = states[hd]


def _hgrn(hq, hk, hlf, hv, hg, rec_out_norm_w, rows):
    b, s, _ = hq.shape
    cm = jnp.asarray(_hgrn_cum_matrix()).astype(BF16)
    blk = pl.BlockSpec((1, rows, HGRN_WIDTH), lambda bi, i: (bi, i, 0))
    return pl.pallas_call(
        functools.partial(_hgrn_kernel, n_chunks=rows // HGRN_CHUNK),
        grid=(b, s // rows),
        in_specs=[blk, blk, blk, blk, blk,
                  pl.BlockSpec((1, HGRN_WIDTH), lambda bi, i: (0, 0)),
                  pl.BlockSpec(cm.shape, lambda bi, i: (0, 0))],
        out_specs=blk,
        out_shape=jax.ShapeDtypeStruct((b, s, HGRN_WIDTH), F32),
        scratch_shapes=[pltpu.VMEM((HGRN_HEADS, HGRN_DIM, HGRN_DIM), F32)],
        compiler_params=_cparams("parallel", "arbitrary"),
        name="hgrn",
    )(hq, hk, hlf, hv, hg, rec_out_norm_w.reshape(1, HGRN_WIDTH), cm)


def _outproj_kernel(x_ref, a_ref, r_ref, wa_ref, wr_ref, gt_ref, sc_ref, sh_ref, n2_ref, x1_ref, h2_ref, h2p_ref):
    mixed = _dot(a_ref[0].astype(BF16), wa_ref[...]) + _dot(r_ref[0].astype(BF16), wr_ref[...])
    x1 = x_ref[0] + gt_ref[0] * mixed
    x1_ref[0] = x1
    ms = jnp.mean(x1 * x1, axis=-1, keepdims=True)
    h2 = x1 * lax.rsqrt(ms + RMS_EPS) * n2_ref[...] * (1.0 + sc_ref[0]) + sh_ref[0]
    h2_ref[0] = h2
    h2p_ref[0] = _pack_bf16_pair(h2[:, :D_MODEL // 2], h2[:, D_MODEL // 2:])


def _outproj(x, attn, rec, w_out, gt1, sc2, sh2, norm2_w, tm):
    b, s, d = x.shape
    row = lambda bi, i: (bi, i, 0)
    per_b = lambda bi, i: (bi, 0, 0)
    fixed2 = lambda bi, i: (0, 0)
    w = w_out.astype(BF16)
    return pl.pallas_call(
        _outproj_kernel,
        grid=(b, s // tm),
        in_specs=[pl.BlockSpec((1, tm, d), row),
                  pl.BlockSpec((1, tm, NSA_WIDTH), row),
                  pl.BlockSpec((1, tm, HGRN_WIDTH), row),
                  pl.BlockSpec((NSA_WIDTH, d), fixed2),
                  pl.BlockSpec((HGRN_WIDTH, d), fixed2),
                  pl.BlockSpec((1, 1, d), per_b),
                  pl.BlockSpec((1, 1, d), per_b),
                  pl.BlockSpec((1, 1, d), per_b),
                  pl.BlockSpec((1, d), fixed2)],
        out_specs=(pl.BlockSpec((1, tm, d), row), pl.BlockSpec((1, tm, d), row), pl.BlockSpec((1, tm, d // 2), row)),
        out_shape=(jax.ShapeDtypeStruct((b, s, d), F32), jax.ShapeDtypeStruct((b, s, d), F32),
                   jax.ShapeDtypeStruct((b, s, d // 2), jnp.uint32)),
        compiler_params=_cparams("parallel", "parallel"),
        name="outproj",
    )(x, attn, rec, w[:NSA_WIDTH], w[NSA_WIDTH:], gt1, sc2, sh2, norm2_w)


def _mixer(x, c, ada_w, ada_b, norm1_w, norm2_w, w_in, q_norm_w, k_norm_w, cmp_pos, cmp_w1, cmp_b1, cmp_w2,
           attn_out_norm_w, hgrn_lb_param, rec_out_norm_w, w_out):
    b, s, d = x.shape
    mod = _mod(c, ada_w, ada_b)
    sh1, sc1, gt1, sh2, sc2, gt2 = [m.reshape(b, 1, d) for m in jnp.split(mod, 6, axis=-1)]
    o = NSA_WIDTH + 6 * KV_WIDTH
    w_cat = jnp.concatenate([w_in[:, :o], w_in[:, o:o + NSA_HEADS * 3],
                             jnp.zeros((d, GATE_PAD - NSA_HEADS * 3), w_in.dtype),
                             w_in[:, o + NSA_HEADS * 3:]], axis=1).astype(BF16)
    tm = min(256, s)
    (q, kc_raw, vc_raw, ks, vst, kw, vwt, gates_t, hq, hk, hlf, hv, hg) = _inproj(
        x, sc1, sh1, norm1_w.reshape(1, d), w_cat, q_norm_w.reshape(1, HEAD_DIM), k_norm_w, hgrn_lb_param, tm)
    kc, vct = _compress(kc_raw, vc_raw, cmp_pos, cmp_w1, cmp_b1, cmp_w2, k_norm_w)
    attn = _nsa(q, kc, vct, ks, vst, kw, vwt, gates_t, attn_out_norm_w)
    rec = _hgrn(hq, hk, hlf, hv, hg, rec_out_norm_w, min(512, s))
    x1, h2, h2p = _outproj(x, attn, rec, w_out, gt1, sc2, sh2, norm2_w.reshape(1, d), tm)
    return x1, h2, h2p, gt2


def _router_kernel(h_ref, rwt_ref, bias_ref, tri_ref, ones_ref, idx_ref, w_ref, rank_ref, cnt_ref, carry_scr, *, tr):
    @pl.when(pl.program_id(0) == 0)
    def _():
        carry_scr[...] = jnp.zeros_like(carry_scr)

    h = h_ref[...]
    h_hi = h.astype(BF16)
    h_lo = (h - h_hi.astype(F32)).astype(BF16)
    logits = _dot_nt(rwt_ref[0], h_hi) + _dot_nt(rwt_ref[1], h_hi) + _dot_nt(rwt_ref[0], h_lo)
    scores = _sigmoid(logits)
    biased = scores + bias_ref[...]
    neg = -jnp.inf

    gs = []
    for g in range(N_GROUPS):
        sub = biased[g * GROUP_SIZE:(g + 1) * GROUP_SIZE, :]
        m1 = jnp.max(sub, axis=0, keepdims=True)
        dup = jnp.sum((sub == m1).astype(F32), axis=0, keepdims=True)
        m2 = jnp.max(jnp.where(sub < m1, sub, neg), axis=0, keepdims=True)
        gs.append(m1 + jnp.where(dup >= 2.0, m1, m2))
    parts = []
    for g in range(N_GROUPS):
        beaten = jnp.zeros_like(gs[g])
        for g2 in range(N_GROUPS):
            if g2 != g:
                beats = (gs[g2] >= gs[g]) if g2 < g else (gs[g2] > gs[g])
                beaten = beaten + beats.astype(F32)
        sub = biased[g * GROUP_SIZE:(g + 1) * GROUP_SIZE, :]
        parts.append(jnp.where(beaten < float(TOPK_GROUPS), sub, neg))
    cand = jnp.concatenate(parts, axis=0)

    rowf = lax.broadcasted_iota(I32, (N_EXPERTS, tr), 0).astype(F32)
    idx_rows, w_rows, hits = [], [], []
    multi = jnp.zeros((N_EXPERTS, tr), F32)
    for _ in range(TOP_K):
        mx = jnp.max(cand, axis=0, keepdims=True)
        first = jnp.min(jnp.where(cand == mx, rowf, float(N_EXPERTS)), axis=0, keepdims=True)
        hit = rowf == first
        idx_rows.append(first)
        w_rows.append(jnp.sum(jnp.where(hit, scores, 0.0), axis=0, keepdims=True))
        cand = jnp.where(hit, neg, cand)
        multi = jnp.where(hit, 1.0, multi)
    w = jnp.concatenate(w_rows, axis=0)
    w_ref[...] = w / jnp.sum(w, axis=0, keepdims=True) * ROUTED_SCALE
    idx = jnp.concatenate(idx_rows, axis=0)
    idx_ref[...] = idx.astype(I32)

    carry = carry_scr[...]
    mb = multi.astype(BF16)
    before = _dot(mb, tri_ref[...]) + jnp.concatenate([carry] * (tr // 128), axis=1)
    rank_rows = [jnp.sum(jnp.where(rowf == idx_rows[k], before, 0.0), axis=0, keepdims=True) for k in range(TOP_K)]
    rank_ref[...] = jnp.concatenate(rank_rows, axis=0).astype(I32)
    carry = carry + _dot(mb, ones_ref[...])
    carry_scr[...] = carry
    cnt_ref[...] = carry


def _router(h2, router_w, router_bias, tr):
    t, d = h2.shape
    tri = jnp.asarray(np.triu(np.ones((tr, tr), np.float32), 1)).astype(BF16)
    ones = jnp.ones((tr, 128), BF16)
    tok = pl.BlockSpec((TOP_K, tr), lambda i: (0, i))
    fixed = lambda i: (0, 0)
    rwt = router_w.T
    rwt_hi = rwt.astype(BF16)
    rwt_split = jnp.stack([rwt_hi, (rwt - rwt_hi.astype(F32)).astype(BF16)])
    return pl.pallas_call(
        functools.partial(_router_kernel, tr=tr),
        grid=(t // tr,),
        in_specs=[pl.BlockSpec((tr, d), lambda i: (i, 0)),
                  pl.BlockSpec((2, N_EXPERTS, d), lambda i: (0, 0, 0)),
                  pl.BlockSpec((N_EXPERTS, 1), fixed),
                  pl.BlockSpec((tr, tr), fixed),
                  pl.BlockSpec((tr, 128), fixed)],
        out_specs=(tok, tok, tok, pl.BlockSpec((N_EXPERTS, 128), fixed)),
        out_shape=(jax.ShapeDtypeStruct((TOP_K, t), I32), jax.ShapeDtypeStruct((TOP_K, t), F32),
                   jax.ShapeDtypeStruct((TOP_K, t), I32), jax.ShapeDtypeStruct((N_EXPERTS, 128), F32)),
        scratch_shapes=[pltpu.VMEM((N_EXPERTS, 128), F32)],
        compiler_params=_cparams("arbitrary"),
        name="router",
    )(h2, rwt_split, router_bias.reshape(N_EXPERTS, 1), tri, ones)


def _pack_bf16_pair(a, b):
    ua = lax.bitcast_convert_type(a.astype(BF16).astype(F32), jnp.uint32)
    ub = lax.bitcast_convert_type(b.astype(BF16).astype(F32), jnp.uint32)
    return ua | (ub >> 16)


def _unpack_bf16_pair(w):
    a = lax.bitcast_convert_type(w & jnp.uint32(0xFFFF0000), F32)
    b = lax.bitcast_convert_type(w << 16, F32)
    return a, b


def _slot_kernel(ps_ref, idx_ref, rank_ref, slot_ref):
    idx = idx_ref[...]

    def body(e, acc):
        return jnp.where(idx == e, ps_ref[e], acc)

    slot_ref[...] = lax.fori_loop(0, N_EXPERTS, body, jnp.zeros_like(idx)) + rank_ref[...]


def _slots(pad_start, idx, rank, tt):
    t = idx.shape[1]
    tok = pl.BlockSpec((TOP_K, tt), lambda i, ps: (0, i))
    return pl.pallas_call(
        _slot_kernel,
        grid_spec=pltpu.PrefetchScalarGridSpec(num_scalar_prefetch=1, grid=(t // tt,),
                                               in_specs=[tok, tok], out_specs=tok),
        out_shape=jax.ShapeDtypeStruct((TOP_K, t), I32),
        compiler_params=_cparams("parallel"),
        name="slots",
    )(pad_start, idx, rank)


SC_CORES = 2
SC_SUBCORES = 16
SC_CHUNK = 64


def _sc_mesh():
    return plsc.VectorSubcoreMesh(core_axis_name="c", subcore_axis_name="s")


def _sc_dispatch(h2p, slot_chunks, n_rows):
    t, dw = h2p.shape
    per = slot_chunks.shape[0] // (SC_CORES * SC_SUBCORES)

    def body(h_hbm, slot_hbm, xs_hbm, idx_v, rows_v, sem):
        wid = lax.axis_index("s") * SC_CORES + lax.axis_index("c")

        @pl.loop(0, per)
        def _(c):
            ch = wid * per + c
            pltpu.sync_copy(slot_hbm.at[ch], idx_v)
            pltpu.sync_copy(h_hbm.at[pl.ds(ch * SC_CHUNK, SC_CHUNK)], rows_v)
            copies = [pltpu.async_copy(rows_v, xs_hbm.at[idx_v.at[k]], sem) for k in range(TOP_K)]
            for cp in copies:
                cp.wait()

    return pl.kernel(
        body, out_type=jax.ShapeDtypeStruct((n_rows, dw), h2p.dtype), mesh=_sc_mesh(),
        scratch_types=[pltpu.VMEM((TOP_K, SC_CHUNK), I32), pltpu.VMEM((SC_CHUNK, dw), h2p.dtype),
                       pltpu.SemaphoreType.DMA],
    )(h2p, slot_chunks)


def _sc_gather(ys, slot_chunks, t):
    dw = ys.shape[1]
    per = slot_chunks.shape[0] // (SC_CORES * SC_SUBCORES)

    def body(ys_hbm, slot_hbm, yg_hbm, idx_v, rows_v, gsem, wsem):
        wid = lax.axis_index("s") * SC_CORES + lax.axis_index("c")

        @pl.loop(0, per)
        def _(c):
            ch = wid * per + c
            pltpu.sync_copy(slot_hbm.at[ch], idx_v)
            gathers = [None] * TOP_K
            writes = [None] * TOP_K
            gathers[0] = pltpu.async_copy(ys_hbm.at[idx_v.at[0]], rows_v.at[0], gsem)
            for k in range(TOP_K):
                gathers[k].wait()
                if k + 1 < TOP_K:
                    if k >= 1:
                        writes[k - 1].wait()
                    gathers[k + 1] = pltpu.async_copy(ys_hbm.at[idx_v.at[k + 1]], rows_v.at[(k + 1) % 2], gsem)
                writes[k] = pltpu.async_copy(rows_v.at[k % 2], yg_hbm.at[k, pl.ds(ch * SC_CHUNK, SC_CHUNK)], wsem)
            writes[TOP_K - 2].wait()
            writes[TOP_K - 1].wait()

    return pl.kernel(
        body, out_type=jax.ShapeDtypeStruct((TOP_K, t, dw), ys.dtype), mesh=_sc_mesh(),
        scratch_types=[pltpu.VMEM((TOP_K, SC_CHUNK), I32), pltpu.VMEM((2, SC_CHUNK, dw), ys.dtype),
                       pltpu.SemaphoreType.DMA, pltpu.SemaphoreType.DMA],
    )(ys, slot_chunks)


def _experts_kernel(be_ref, nu_ref, bv_ref, run_ref, xs_hbm, wg_hbm, wu_hbm, wd_hbm, ys_ref,
                    xring, rsem, gring, uring, dring, wsem):
    i = pl.program_id(0)
    half = D_MODEL // 2
    n_used = nu_ref[0]
    n_steps = pl.num_programs(0)

    def weight_copies(blk):
        ex = be_ref[blk]
        slot = run_ref[blk] % EXPERT_RING
        return [pltpu.make_async_copy(src.at[ex], ring.at[slot], wsem.at[a, slot])
                for a, (src, ring) in enumerate(((wg_hbm, gring), (wu_hbm, uring), (wd_hbm, dring)))]

    def starts_run(blk):
        return run_ref[blk] != run_ref[jnp.maximum(blk - 1, 0)]

    @pl.when(i == 0)
    def _():
        for cp in weight_copies(jnp.int32(0)):
            cp.start()

        @pl.when((n_steps > 1) & starts_run(jnp.int32(1)))
        def _():
            for cp in weight_copies(jnp.int32(1)):
                cp.start()

    ahead = jnp.minimum(i + (EXPERT_RING - 1), n_steps - 1)

    @pl.when((i + (EXPERT_RING - 1) < n_steps) & starts_run(ahead))
    def _():
        for cp in weight_copies(ahead):
            cp.start()

    @pl.when((i == 0) | starts_run(i))
    def _():
        for cp in weight_copies(i):
            cp.wait()

    wslot = run_ref[i] % EXPERT_RING
    wg_ref, wu_ref, wd_ref = gring.at[wslot], uring.at[wslot], dring.at[wslot]

    def fetch(blk):
        slot = blk % EXPERT_RING
        return pltpu.make_async_copy(xs_hbm.at[pl.ds(pl.multiple_of(blk * EXPERT_BLOCK, EXPERT_BLOCK), EXPERT_BLOCK)],
                                     xring.at[slot], rsem.at[slot])

    @pl.when(i == 0)
    def _():
        for first in range(EXPERT_RING - 1):
            @pl.when(first < n_used)
            def _():
                fetch(jnp.int32(first)).start()

    @pl.when(i + (EXPERT_RING - 1) < n_used)
    def _():
        fetch(i + (EXPERT_RING - 1)).start()

    @pl.when(i < n_used)
    def _():
        fetch(i).wait()

    xs_ref = xring.at[i % EXPERT_RING]

    def ffn(rows):
        live = lax.broadcasted_iota(I32, (rows, xs_ref.shape[1]), 0) < bv_ref[i]
        xa, xb = _unpack_bf16_pair(jnp.where(live, xs_ref[0:rows, :], jnp.uint32(0)))
        xa, xb = xa.astype(BF16), xb.astype(BF16)
        g = _dot(xa, wg_ref[:half, :].astype(BF16)) + _dot(xb, wg_ref[half:, :].astype(BF16))
        u = _dot(xa, wu_ref[:half, :].astype(BF16)) + _dot(xb, wu_ref[half:, :].astype(BF16))
        act = (g * _sigmoid(g) * u).astype(BF16)
        y = _dot(act, wd_ref[...].astype(BF16))
        ys_ref[0:rows, :] = _pack_bf16_pair(y[:, :half], y[:, half:])

    used = i < n_used
    short = bv_ref[i] <= EXPERT_TAIL

    @pl.when(used & jnp.logical_not(short))
    def _():
        ffn(EXPERT_BLOCK)

    @pl.when(used & short)
    def _():
        ffn(EXPERT_TAIL)
        ys_ref[EXPERT_TAIL:, :] = jnp.zeros((EXPERT_BLOCK - EXPERT_TAIL, ys_ref.shape[1]), ys_ref.dtype)

    @pl.when(jnp.logical_not(used))
    def _():
        ys_ref[...] = jnp.zeros_like(ys_ref)


def _experts(xs, blk_e, n_used, blk_valid, w_gate, w_up, w_down):
    n_rows, dw = xs.shape
    d = w_gate.shape[1]
    nblk = n_rows // EXPERT_BLOCK
    blk_run = jnp.cumsum(jnp.concatenate([jnp.zeros((1,), I32), (blk_e[1:] != blk_e[:-1]).astype(I32)])).astype(I32)
    hbm = pl.BlockSpec(memory_space=pl.ANY)
    return pl.pallas_call(
        _experts_kernel,
        grid_spec=pltpu.PrefetchScalarGridSpec(
            num_scalar_prefetch=4,
            grid=(nblk,),
            in_specs=[hbm, hbm, hbm, hbm],
            out_specs=pl.BlockSpec((EXPERT_BLOCK, dw), lambda i, be, nu, bv, rn: (i, 0)),
            scratch_shapes=[pltpu.VMEM((EXPERT_RING, EXPERT_BLOCK, dw), xs.dtype),
                            pltpu.SemaphoreType.DMA((EXPERT_RING,)),
                            pltpu.VMEM((EXPERT_RING, d, EXPERT_FF), w_gate.dtype),
                            pltpu.VMEM((EXPERT_RING, d, EXPERT_FF), w_up.dtype),
                            pltpu.VMEM((EXPERT_RING, EXPERT_FF, d), w_down.dtype),
                            pltpu.SemaphoreType.DMA((3, EXPERT_RING))]),
        out_shape=jax.ShapeDtypeStruct((n_rows, dw), xs.dtype),
        compiler_params=pltpu.CompilerParams(dimension_semantics=("arbitrary",), vmem_limit_bytes=VMEM_LIMIT,
                                             has_side_effects=True),
        name="experts",
    )(blk_e, n_used, blk_valid, blk_run, xs, w_gate, w_up, w_down)


def _combine_kernel(x1_ref, h_ref, w_ref, gt_ref, sg_ref, su_ref, sd_ref, yg_ref, o_ref):
    tc = x1_ref.shape[0]
    half = D_MODEL // 2
    hb = h_ref[...].astype(BF16)
    g = _dot(hb, sg_ref[...])
    u = _dot(hb, su_ref[...])
    ffn = _dot((g * _sigmoid(g) * u).astype(BF16), sd_ref[...])

    w = w_ref[...]
    ra = jnp.zeros((tc, half), F32)
    rb = jnp.zeros((tc, half), F32)
    for k in range(TOP_K):
        ya, yb = _unpack_bf16_pair(yg_ref[k])
        ra = ra + w[:, k:k + 1] * ya
        rb = rb + w[:, k:k + 1] * yb
    ffn = ffn + jnp.concatenate([ra, rb], axis=1)
    o_ref[...] = x1_ref[...] + gt_ref[0] * ffn


def _combine(x1, h2, w_tok, gt2, yg, sg, su, sd, seq, tc):
    t, d = x1.shape
    row = lambda i: (i, 0)
    fixed = lambda i: (0, 0)
    return pl.pallas_call(
        _combine_kernel,
        grid=(t // tc,),
        in_specs=[pl.BlockSpec((tc, d), row),
                  pl.BlockSpec((tc, d), row),
                  pl.BlockSpec((tc, TOP_K), row),
                  pl.BlockSpec((1, 1, d), lambda i: ((i * tc) // seq, 0, 0)),
                  pl.BlockSpec((d, SHARED_FF), fixed),
                  pl.BlockSpec((d, SHARED_FF), fixed),
                  pl.BlockSpec((SHARED_FF, d), fixed),
                  pl.BlockSpec((TOP_K, tc, d // 2), lambda i: (0, i, 0))],
        out_specs=pl.BlockSpec((tc, d), row),
        out_shape=jax.ShapeDtypeStruct((t, d), F32),
        compiler_params=_cparams("parallel"),
        name="combine",
    )(x1, h2, w_tok, gt2, sg.astype(BF16), su.astype(BF16), sd.astype(BF16), yg)


def _moe_parts(x1, h2, h2p, gt2, router_w, router_bias, w_gate, w_up, w_down, sg, su, sd):
    b, s, d = x1.shape
    t = b * s
    h2 = h2.reshape(t, d)
    idx, w, rank, cnt = _router(h2, router_w, router_bias, min(256, t))
    counts = cnt[:, 0].astype(I32)
    padded = (counts + EXPERT_BLOCK - 1) // EXPERT_BLOCK * EXPERT_BLOCK
    pad_end = jnp.cumsum(padded)
    pad_start = pad_end - padded
    n_rows = t * TOP_K + N_EXPERTS * EXPERT_BLOCK
    nblk = n_rows // EXPERT_BLOCK
    n_used = (pad_end[-1:] // EXPERT_BLOCK).astype(I32)
    blk_start = jnp.arange(nblk, dtype=I32) * EXPERT_BLOCK
    owns = (pad_start[None, :] <= blk_start[:, None]) & (blk_start[:, None] < pad_end[None, :])
    e_ids = jnp.arange(N_EXPERTS, dtype=I32)[None, :]
    last_e = jnp.max(jnp.where(counts > 0, e_ids[0], 0))
    blk_e = jnp.where(blk_start < pad_end[-1], jnp.sum(jnp.where(owns, e_ids, 0), axis=1), last_e).astype(I32)
    rows_left = jnp.sum(jnp.where(owns, (pad_start + counts)[None, :] - blk_start[:, None], 0), axis=1)
    blk_valid = jnp.clip(rows_left, 0, EXPERT_BLOCK).astype(I32)
    slot = _slots(pad_start.astype(I32), idx, rank, min(2048, t))
    slot_chunks = slot.reshape(TOP_K, t // SC_CHUNK, SC_CHUNK).transpose(1, 0, 2)
    xs = _sc_dispatch(h2p.reshape(t, d // 2), slot_chunks, n_rows)
    ys = _experts(xs, blk_e, n_used, blk_valid, w_gate, w_up, w_down)
    yg = _sc_gather(ys, slot_chunks, t)
    out = _combine(x1.reshape(t, d), h2, w.T, gt2, yg, sg, su, sd, s, min(256, t))
    return out.reshape(b, s, d), dict(idx=idx, w=w, rank=rank, cnt=cnt)


def kernel(x, c, ada_w, ada_b, norm1_w, norm2_w, w_in, q_norm_w, k_norm_w, cmp_pos, cmp_w1, cmp_b1, cmp_w2, attn_out_norm_w, hgrn_lb_param, rec_out_norm_w, w_out, router_w, router_bias, exp_w_gate, exp_w_up, exp_w_down, shared_w_gate, shared_w_up, shared_w_down):
    assert ada_w.shape[0] == 1, "one layer"
    assert x.shape[0] <= 8 and x.shape[1] % TK == 0 and x.shape[1] >= WINDOW + TQ
    l = 0
    x1, h2, h2p, gt2 = _mixer(x, c, ada_w[l], ada_b[l], norm1_w[l], norm2_w[l], w_in[l], q_norm_w[l], k_norm_w[l],
                         cmp_pos[l], cmp_w1[l], cmp_b1[l], cmp_w2[l], attn_out_norm_w[l], hgrn_lb_param,
                         rec_out_norm_w[l], w_out[l])
    out, _ = _moe_parts(x1, h2, h2p, gt2, router_w[l], router_bias[l], exp_w_gate[l], exp_w_up[l], exp_w_down[l],
                        shared_w_gate[l], shared_w_up[l], shared_w_down[l])
    return out
```

```python
import functools

import numpy as np
import jax
import jax.numpy as jnp
from jax import lax
from jax.experimental import pallas as pl
from jax.experimental.pallas import tpu as pltpu
from jax.experimental.pallas import tpu_sc as plsc

F32 = jnp.float32
BF16 = jnp.bfloat16
I32 = jnp.int32

D_MODEL = 1024
NSA_HEADS = 8
HEAD_DIM = 64
NSA_WIDTH = NSA_HEADS * HEAD_DIM
KV_HEADS = 2
HEADS_PER_KV = NSA_HEADS // KV_HEADS
KV_WIDTH = KV_HEADS * HEAD_DIM
CMP_BLOCK = 32
CMP_STRIDE = 16
CMP_HIDDEN = 256
SEL_BLOCK = 64
N_SELECT = 16
WINDOW = 512
HGRN_HEADS = 4
HGRN_DIM = 128
HGRN_WIDTH = HGRN_HEADS * HGRN_DIM
HGRN_CHUNK = 64
HGRN_SUB = 16
HGRN_LEVELS = ()
HGRN_LEAF = 16
N_EXPERTS = 256
TOP_K = 8
N_GROUPS = 8
GROUP_SIZE = N_EXPERTS // N_GROUPS
TOPK_GROUPS = 4
EXPERT_FF = 256
SHARED_FF = 256
ROUTED_SCALE = 2.5
RMS_EPS = 1e-6
BIG = 1e9
LOG2E = 1.4426950408889634
GATE_PAD = 128
PROJ_COLS = NSA_WIDTH + 6 * KV_WIDTH + GATE_PAD + 4 * HGRN_WIDTH

VMEM_LIMIT = 56 * 1024 * 1024

TQ = 256
TK = 512
EXPERT_BLOCK = 512
EXPERT_TAIL = 128
EXPERT_RING = 3
HIGHEST = lax.Precision.HIGHEST


def _cparams(*sem):
    return pltpu.CompilerParams(dimension_semantics=sem, vmem_limit_bytes=VMEM_LIMIT)


def _sigmoid(x):
    return 1.0 / (1.0 + jnp.exp(-x))


def _dot_nt(a, b):
    return lax.dot_general(a, b, (((1,), (1,)), ((), ())), preferred_element_type=F32)


def _dot(a, b, **kw):
    return jnp.dot(a, b, preferred_element_type=F32, **kw)


def _split_dot(a_bf16_exact, x):
    hi = x.astype(BF16)
    lo = (x - hi.astype(F32)).astype(BF16)
    return _dot(a_bf16_exact, hi) + _dot(a_bf16_exact, lo)


def _mod_kernel(c_ref, w_ref, b_ref, o_ref):
    c = c_ref[...]
    cond = c * _sigmoid(c)
    o_ref[...] = _dot(cond, w_ref[...], precision=HIGHEST) + b_ref[...]


def _mod(c, ada_w, ada_b):
    b, d = c.shape
    rows = 8
    c_pad = jnp.zeros((rows, d), F32).at[:b].set(c)
    n = ada_w.shape[1]
    out = pl.pallas_call(
        _mod_kernel,
        grid=(n // d,),
        in_specs=[pl.BlockSpec((rows, d), lambda j: (0, 0)),
                  pl.BlockSpec((d, d), lambda j: (0, j)),
                  pl.BlockSpec((1, d), lambda j: (0, j))],
        out_specs=pl.BlockSpec((rows, d), lambda j: (0, j)),
        out_shape=jax.ShapeDtypeStruct((rows, n), F32),
        compiler_params=_cparams("parallel"),
        name="mod",
    )(c_pad, ada_w, ada_b.reshape(1, n))
    return out[:b]


def _head_rms(t, w):
    return t * lax.rsqrt(jnp.mean(t * t, axis=-1, keepdims=True) + RMS_EPS) * w


def _pos_digits(pos):
    lane = lax.broadcasted_iota(I32, pos.shape, 1)
    d0 = (lane == 0) | (lane == 3) | (lane == 6)
    d1 = (lane == 1) | (lane == 4) | (lane == 7)
    d2 = (lane == 2) | (lane == 5) | (lane == 8)
    dig = jnp.where(d0, pos >> 12, jnp.where(d1, (pos >> 6) & 63, jnp.where(d2, pos & 63, 0)))
    return dig.astype(F32)


def _inproj_kernel(x_ref, sc_ref, sh_ref, n1_ref, w_ref, qnw_ref, knw_ref, lbp_ref, qaug_ref,
                   q_ref, kcr_ref, vcr_ref, ks_ref, vst_ref, kw_ref, vwt_ref, gt_ref,
                   hq_ref, hk_ref, hlf_ref, hv_ref, hg_ref):
    x = x_ref[0]
    ms = jnp.mean(x * x, axis=-1, keepdims=True)
    h = x * lax.rsqrt(ms + RMS_EPS) * n1_ref[...] * (1.0 + sc_ref[0]) + sh_ref[0]
    p = _dot(h.astype(BF16), w_ref[...])
    tm = x.shape[0]

    qnw = qnw_ref[...]
    for hd in range(NSA_HEADS):
        t = p[:, hd * HEAD_DIM:(hd + 1) * HEAD_DIM]
        qn = _head_rms(t, qnw) * (HEAD_DIM ** -0.5 * LOG2E)
        qa = jnp.broadcast_to(qaug_ref[hd:hd + 1, :], (tm, HEAD_DIM))
        q_ref[0, hd] = jnp.concatenate([qn, qa], axis=1).astype(BF16)
    kaug = _pos_digits(pl.program_id(1) * tm + lax.broadcasted_iota(I32, (tm, HEAD_DIM), 0))

    o = NSA_WIDTH
    kcr_ref[0] = p[:, o:o + KV_WIDTH]
    vcr_ref[0] = p[:, o + KV_WIDTH:o + 2 * KV_WIDTH]
    ks = p[:, o + 2 * KV_WIDTH:o + 3 * KV_WIDTH]
    vs = p[:, o + 3 * KV_WIDTH:o + 4 * KV_WIDTH]
    kw = p[:, o + 4 * KV_WIDTH:o + 5 * KV_WIDTH]
    vw = p[:, o + 5 * KV_WIDTH:o + 6 * KV_WIDTH]
    for g in range(KV_HEADS):
        sl = slice(g * HEAD_DIM, (g + 1) * HEAD_DIM)
        ks_ref[0, g] = jnp.concatenate([_head_rms(ks[:, sl], knw_ref[1:2, :]), kaug], axis=1).astype(BF16)
        kw_ref[0, g] = jnp.concatenate([_head_rms(kw[:, sl], knw_ref[2:3, :]), kaug], axis=1).astype(BF16)
    vst = vs.T.astype(BF16)
    vwt = vw.T.astype(BF16)
    for g in range(KV_HEADS):
        vst_ref[0, g] = vst[g * HEAD_DIM:(g + 1) * HEAD_DIM, :]
        vwt_ref[0, g] = vwt[g * HEAD_DIM:(g + 1) * HEAD_DIM, :]

    o = NSA_WIDTH + 6 * KV_WIDTH
    gates = _sigmoid(p[:, o:o + GATE_PAD])
    gt_ref[0] = gates.T[:NSA_HEADS * 3, :]

    o = o + GATE_PAD
    hq = p[:, o:o + HGRN_WIDTH]
    hf = p[:, o + HGRN_WIDTH:o + 2 * HGRN_WIDTH]
    hi = p[:, o + 2 * HGRN_WIDTH:o + 3 * HGRN_WIDTH]
    hg = p[:, o + 3 * HGRN_WIDTH:o + 4 * HGRN_WIDTH]
    lbp = lbp_ref[...]
    e = jnp.exp(lbp - jnp.max(lbp, axis=0, keepdims=True))
    lb = e[0:1, :] / jnp.sum(e, axis=0, keepdims=True)
    f = lb + (1.0 - lb) * _sigmoid(hf)
    hq_ref[0] = hq * _sigmoid(hq) * (HGRN_DIM ** -0.5)
    hk_ref[0] = 1.0 - f
    hlf_ref[0] = jnp.log(f)
    hv_ref[0] = hi
    hg_ref[0] = _sigmoid(hg)


def _inproj(x, sc1, sh1, norm1_w, w_cat, q_norm_w, k_norm_w, lb_param, tm):
    b, s, d = x.shape
    row = lambda bi, i: (bi, i, 0)
    per_b = lambda bi, i: (bi, 0, 0)
    fixed2 = lambda bi, i: (0, 0)
    aw = 2 * HEAD_DIM
    rest = np.array([2.0 ** (-8.0 * (i + 1) / NSA_HEADS) for i in range(NSA_HEADS)], np.float64) * LOG2E
    qaug = np.zeros((NSA_HEADS, HEAD_DIM), np.float32)
    for i in range(3):
        term = rest.astype(np.float32).astype(BF16).astype(np.float64)
        rest = rest - term
        for dgt, wgt in enumerate((4096.0, 64.0, 1.0)):
            qaug[:, 3 * i + dgt] = term * wgt
    assert np.all(qaug == qaug.astype(BF16).astype(np.float32))
    out_shape = (
        jax.ShapeDtypeStruct((b, NSA_HEADS, s, aw), BF16),
        jax.ShapeDtypeStruct((b, s, KV_WIDTH), F32),
        jax.ShapeDtypeStruct((b, s, KV_WIDTH), F32),
        jax.ShapeDtypeStruct((b, KV_HEADS, s, aw), BF16),
        jax.ShapeDtypeStruct((b, KV_HEADS, HEAD_DIM, s), BF16),
        jax.ShapeDtypeStruct((b, KV_HEADS, s, aw), BF16),
        jax.ShapeDtypeStruct((b, KV_HEADS, HEAD_DIM, s), BF16),
        jax.ShapeDtypeStruct((b, NSA_HEADS * 3, s), F32),
    ) + tuple(jax.ShapeDtypeStruct((b, s, HGRN_WIDTH), F32) for _ in range(5))
    hm = lambda n, w: pl.BlockSpec((1, n, tm, w), lambda bi, i: (bi, 0, i, 0))
    hmt = lambda n, w: pl.BlockSpec((1, n, w, tm), lambda bi, i: (bi, 0, 0, i))
    out_specs = (
        hm(NSA_HEADS, aw),
        pl.BlockSpec((1, tm, KV_WIDTH), row),
        pl.BlockSpec((1, tm, KV_WIDTH), row),
        hm(KV_HEADS, aw), hmt(KV_HEADS, HEAD_DIM),
        hm(KV_HEADS, aw), hmt(KV_HEADS, HEAD_DIM),
        pl.BlockSpec((1, NSA_HEADS * 3, tm), lambda bi, i: (bi, 0, i)),
    ) + tuple(pl.BlockSpec((1, tm, HGRN_WIDTH), row) for _ in range(5))
    return pl.pallas_call(
        _inproj_kernel,
        grid=(b, s // tm),
        in_specs=[pl.BlockSpec((1, tm, d), row),
                  pl.BlockSpec((1, 1, d), per_b),
                  pl.BlockSpec((1, 1, d), per_b),
                  pl.BlockSpec((1, d), fixed2),
                  pl.BlockSpec((d, PROJ_COLS), fixed2),
                  pl.BlockSpec((1, HEAD_DIM), fixed2),
                  pl.BlockSpec((3, HEAD_DIM), fixed2),
                  pl.BlockSpec(lb_param.shape, fixed2),
                  pl.BlockSpec((NSA_HEADS, HEAD_DIM), fixed2)],
        out_specs=out_specs,
        out_shape=out_shape,
        compiler_params=_cparams("parallel", "parallel"),
        name="inproj",
    )(x, sc1, sh1, norm1_w, w_cat, q_norm_w, k_norm_w, lb_param, jnp.asarray(qaug))


def _gelu_tanh(x):
    return 0.5 * x * (1.0 + jnp.tanh(0.7978845608028654 * (x + 0.044715 * x * x * x)))


def _compress_kernel(kch_ref, vch_ref, pos_ref, wa_ref, wb_ref, b1_ref, w2_ref, knw_ref,
                     kc_ref, vct_ref):
    n = kch_ref.shape[1]
    outs = []
    for br, ch_ref in enumerate((kch_ref, vch_ref)):
        ch = ch_ref[0]
        a = _dot((ch + pos_ref[br, 0:1, :]).astype(BF16), wa_ref[br])
        bm = _dot((ch + pos_ref[br, 1:2, :]).astype(BF16), wb_ref[br])
        pre = a + pltpu.roll(bm, n - 1, 0) + b1_ref[br]
        hid = _gelu_tanh(pre).astype(BF16)
        outs.append([_dot(hid[:, g * CMP_HIDDEN:(g + 1) * CMP_HIDDEN], w2_ref[br]) for g in range(KV_HEADS)])
    end_digits = _pos_digits(lax.broadcasted_iota(I32, (n, HEAD_DIM), 0) * CMP_STRIDE + (CMP_BLOCK - 1))
    for g in range(KV_HEADS):
        kc_ref[0, g] = jnp.concatenate([_head_rms(outs[0][g], knw_ref[0:1, :]), end_digits], axis=1).astype(BF16)
    vct = jnp.concatenate(outs[1], axis=1).T.astype(BF16)
    for g in range(KV_HEADS):
        vct_ref[0, g] = vct[g * HEAD_DIM:(g + 1) * HEAD_DIM, :]


def _compress(kc_raw, vc_raw, cmp_pos, cmp_w1, cmp_b1, cmp_w2, k_norm_w):
    b, s, _ = kc_raw.shape
    n = s // CMP_STRIDE
    half = CMP_STRIDE
    cw = CMP_STRIDE * KV_WIDTH
    kch = kc_raw.reshape(b, n, cw)
    vch = vc_raw.reshape(b, n, cw)
    pos = cmp_pos.reshape(2, 2, half, 1, HEAD_DIM)
    pos = jnp.broadcast_to(pos, (2, 2, half, KV_HEADS, HEAD_DIM)).reshape(2, 2, cw)
    w1 = cmp_w1.reshape(2, 2, half, HEAD_DIM, CMP_HIDDEN)
    eye = jnp.eye(KV_HEADS, dtype=F32)
    wfull = jnp.einsum('rhjdn,gk->rhjgdkn', w1, eye).reshape(2, 2, cw, KV_HEADS * CMP_HIDDEN).astype(BF16)
    b1 = jnp.tile(cmp_b1.reshape(2, 1, CMP_HIDDEN), (1, 1, KV_HEADS))
    fix = lambda r: (lambda bi: (0,) * r)
    return pl.pallas_call(
        _compress_kernel,
        grid=(b,),
        in_specs=[pl.BlockSpec((1, n, cw), lambda bi: (bi, 0, 0)),
                  pl.BlockSpec((1, n, cw), lambda bi: (bi, 0, 0)),
                  pl.BlockSpec((2, 2, cw), fix(3)),
                  pl.BlockSpec((2, cw, KV_HEADS * CMP_HIDDEN), fix(3)),
                  pl.BlockSpec((2, cw, KV_HEADS * CMP_HIDDEN), fix(3)),
                  pl.BlockSpec((2, 1, KV_HEADS * CMP_HIDDEN), fix(3)),
                  pl.BlockSpec((2, CMP_HIDDEN, HEAD_DIM), fix(3)),
                  pl.BlockSpec((3, HEAD_DIM), fix(2))],
        out_specs=(pl.BlockSpec((1, KV_HEADS, n, 2 * HEAD_DIM), lambda bi: (bi, 0, 0, 0)),
                   pl.BlockSpec((1, KV_HEADS, HEAD_DIM, n), lambda bi: (bi, 0, 0, 0))),
        out_shape=(jax.ShapeDtypeStruct((b, KV_HEADS, n, 2 * HEAD_DIM), BF16),
                   jax.ShapeDtypeStruct((b, KV_HEADS, HEAD_DIM, n), BF16)),
        compiler_params=_cparams("parallel"),
        name="compress",
    )(kch, vch, pos, wfull[:, 0], wfull[:, 1], b1, cmp_w2.astype(BF16), k_norm_w)


def _nsa_kernel(q_ref, kc_ref, vct_ref, ks_ref, vst_ref, kw_ref, vwt_ref, gt_ref, cdiff_ref, wdiff_ref,
                ovl_ref, oh_ref, onw_ref, wmask_ref, o_ref, buf_a, buf_b, m_scr, acc_scr, lst, cnt, *, n_top):
    q0 = pl.program_id(2) * TQ
    ncols = HEADS_PER_KV * TQ
    q = q_ref[0].reshape(ncols, 2 * HEAD_DIM)
    ns = ovl_ref.shape[0]

    s = jnp.where(cdiff_ref[...] <= q0, _dot_nt(kc_ref[0, 0], q), -jnp.inf)
    m = jnp.max(s, axis=0, keepdims=True)
    m = jnp.where(m == -jnp.inf, 0.0, m)
    e = jnp.exp2(s - m)
    p = e / jnp.maximum(jnp.sum(e, axis=0, keepdims=True), 1e-30)
    o_c = _dot(vct_ref[0, 0], p.astype(BF16))

    psum = p[:, 0:TQ]
    for hh in range(1, HEADS_PER_KV):
        psum = psum + p[:, hh * TQ:(hh + 1) * TQ]
    imp = _split_dot(ovl_ref[...], psum)
    blk = lax.broadcasted_iota(I32, (ns, TQ), 0)
    tq = q0 + lax.broadcasted_iota(I32, (ns, TQ), 1)
    cur = tq >> 6
    forced = (blk == 0) | (blk == cur) | (blk == cur - 1)
    rank = jnp.where(forced, BIG, jnp.where(blk * SEL_BLOCK <= tq, imp, -BIG))

    blkf = blk.astype(F32)

    bias = jnp.full((ns, TQ), -1e30, F32)
    for _ in range(n_top):
        mx = jnp.max(rank, axis=0, keepdims=True)
        first = jnp.min(jnp.where(rank == mx, blkf, float(ns)), axis=0, keepdims=True)
        hit = blkf == first
        rank = jnp.where(hit, -jnp.inf, rank)
        bias = jnp.where(hit, 0.0, bias)

    if ns < 128:
        bias = jnp.concatenate([bias, jnp.zeros((128 - ns, TQ), F32)], axis=0)
    bias_t = bias.T.astype(BF16)
    qq = jnp.concatenate([q, jnp.concatenate([bias_t] * HEADS_PER_KV, axis=0)], axis=1)
    ones_rows = jnp.ones((16, TK), BF16)

    def scores(j):
        k0 = pl.multiple_of(j * TK, TK)
        kk = jnp.concatenate([ks_ref[0, 0, pl.ds(k0, TK), :], oh_ref[pl.ds(k0, TK), :]], axis=1)
        return _dot_nt(kk, qq)

    def consume(buf, j, causal, part):
        sc = buf[...]
        if causal:
            sc = jnp.where(wdiff_ref[0:TK, :] + (q0 - j * TK) >= 0, sc, -1e30)
        k0 = pl.multiple_of(j * TK, TK)
        m_run = m_scr[part]
        m_new = jnp.maximum(m_run, jnp.max(sc, axis=0, keepdims=True))
        ex = jnp.exp2(sc - m_new).astype(BF16)
        va = jnp.concatenate([vst_ref[0, 0, :, pl.ds(k0, TK)], ones_rows], axis=0)
        acc_scr[part] = jnp.exp2(m_run - m_new) * acc_scr[part] + _dot(va, ex)
        m_scr[part] = m_new

    n_past = q0 // TK
    blocks_per_tile = TK // SEL_BLOCK
    cnt[0] = 0
    for j in range(ks_ref.shape[2] // TK):
        wanted = jnp.max(bias[j * blocks_per_tile:(j + 1) * blocks_per_tile, :]) == 0.0

        @pl.when(wanted & (j < n_past))
        def _():
            lst[cnt[0]] = j
            cnt[0] = cnt[0] + 1

    n_sel = cnt[0]
    lst[n_sel] = n_past

    m_scr[...] = jnp.full(m_scr.shape, -1e30, F32)
    acc_scr[...] = jnp.zeros(acc_scr.shape, F32)
    buf_a[...] = scores(lst[0])

    nw = WINDOW + TQ
    start = pl.multiple_of(jnp.maximum(q0 - WINDOW, 0), TQ)
    sw = _dot_nt(kw_ref[0, 0, pl.ds(start, nw), :], q) + wmask_ref[0]
    ew = jnp.exp2(sw - jnp.max(sw, axis=0, keepdims=True))
    vw_aug = jnp.concatenate([vwt_ref[0, 0, :, pl.ds(start, nw)], jnp.ones((16, nw), BF16)], axis=0)
    acc_w = _dot(vw_aug, ew.astype(BF16))
    o_w = acc_w[0:HEAD_DIM, :] / acc_w[HEAD_DIM:HEAD_DIM + 1, :]

    def tiles(first, count):
        for u in range(0, count, 2):
            buf_b[...] = scores(lst[first + u + 1])
            consume(buf_a, lst[first + u], False, 0)
            buf_a[...] = scores(lst[first + u + 2])
            consume(buf_b, lst[first + u + 1], False, 1)
        return 0

    lax.fori_loop(0, n_sel // 4, lambda i, _: tiles(4 * i, 4), 0)
    lax.fori_loop(0, (n_sel // 2) % 2, lambda i, _: tiles((n_sel // 4) * 4, 2), 0)

    @pl.when(n_sel % 2 == 1)
    def _():
        buf_b[...] = scores(n_past)
        consume(buf_a, lst[n_sel - 1], False, 0)
        consume(buf_b, n_past, True, 1)

    @pl.when(n_sel % 2 == 0)
    def _():
        consume(buf_a, n_past, True, 0)

    m_all = jnp.maximum(m_scr[0], m_scr[1])
    acc_s = jnp.exp2(m_scr[0] - m_all) * acc_scr[0] + jnp.exp2(m_scr[1] - m_all) * acc_scr[1]
    o_s = acc_s[0:HEAD_DIM, :] / acc_s[HEAD_DIM:HEAD_DIM + 1, :]

    gt = gt_ref[0, 0]
    outs = []
    for hh in range(HEADS_PER_KV):
        cs = slice(hh * TQ, (hh + 1) * TQ)
        o = (gt[3 * hh:3 * hh + 1, :] * o_c[:, cs] + gt[3 * hh + 1:3 * hh + 2, :] * o_s[:, cs]
             + gt[3 * hh + 2:3 * hh + 3, :] * o_w[:, cs])
        o = o * lax.rsqrt(jnp.mean(o * o, axis=0, keepdims=True) + RMS_EPS) * onw_ref[0, hh]
        outs.append(o)
    o_ref[0] = jnp.concatenate(outs, axis=0).T


def _nsa(q, kc, vct, ks, vst, kw, vwt, gates_t, attn_out_norm_w):
    b, _, s, aw = q.shape
    nc = kc.shape[2]
    ns = s // SEL_BLOCK
    n_top = min(N_SELECT, ns)
    ncols = HEADS_PER_KV * TQ
    nw = WINDOW + TQ
    tl = np.arange(ncols)[None, :] & (TQ - 1)
    cdiff = jnp.asarray((np.arange(nc)[:, None] * CMP_STRIDE + (CMP_BLOCK - 1) - tl).astype(np.int32))
    wdiff_np = (tl - np.arange(nw)[:, None]).astype(np.int32)
    wdiff = jnp.asarray(wdiff_np)
    n_off = WINDOW // TQ + 1
    dist_np = wdiff_np[None] + (np.arange(n_off) * TQ)[:, None, None]
    wmask = jnp.asarray(np.where((dist_np >= 0) & (dist_np < WINDOW), 0.0, -np.inf).astype(np.float32))
    ci = np.arange(nc)[None, :] * CMP_STRIDE
    bj = np.arange(ns)[:, None]
    ovl = ((ci < (bj + 1) * SEL_BLOCK) & (ci + CMP_BLOCK > bj * SEL_BLOCK) & (np.arange(nc)[None, :] < nc - 1))
    ovl = jnp.asarray(ovl.astype(np.float32)).astype(BF16)
    assert ns <= 128
    onehot = (np.arange(s)[:, None] // SEL_BLOCK == np.arange(128)[None, :])
    onehot = jnp.asarray(onehot.astype(np.float32)).astype(BF16)
    onw = jnp.broadcast_to(attn_out_norm_w.reshape(KV_HEADS, HEADS_PER_KV, HEAD_DIM, 1),
                           (KV_HEADS, HEADS_PER_KV, HEAD_DIM, TQ))
    gt = gates_t.reshape(b, KV_HEADS, HEADS_PER_KV * 3, s)
    per_bg = lambda bi, g, i: (bi, g, 0, 0)
    fixed = lambda bi, g, i: (0, 0)
    return pl.pallas_call(
        functools.partial(_nsa_kernel, n_top=n_top),
        grid=(b, KV_HEADS, s // TQ),
        in_specs=[pl.BlockSpec((1, HEADS_PER_KV, TQ, aw), lambda bi, g, i: (bi, g, i, 0)),
                  pl.BlockSpec((1, 1, nc, aw), per_bg),
                  pl.BlockSpec((1, 1, HEAD_DIM, nc), per_bg),
                  pl.BlockSpec((1, 1, s, aw), per_bg),
                  pl.BlockSpec((1, 1, HEAD_DIM, s), per_bg),
                  pl.BlockSpec((1, 1, s, aw), per_bg),
                  pl.BlockSpec((1, 1, HEAD_DIM, s), per_bg),
                  pl.BlockSpec((1, 1, HEADS_PER_KV * 3, TQ), lambda bi, g, i: (bi, g, 0, i)),
                  pl.BlockSpec((nc, ncols), fixed, pipeline_mode=pl.Buffered(1)),
                  pl.BlockSpec((nw, ncols), fixed, pipeline_mode=pl.Buffered(1)),
                  pl.BlockSpec((ns, nc), fixed, pipeline_mode=pl.Buffered(1)),
                  pl.BlockSpec((s, 128), fixed, pipeline_mode=pl.Buffered(1)),
                  pl.BlockSpec((1, HEADS_PER_KV, HEAD_DIM, TQ), lambda bi, g, i: (g, 0, 0, 0)),
                  pl.BlockSpec((1, nw, ncols), lambda bi, g, i: (jnp.minimum(i, n_off - 1), 0, 0))],
        out_specs=pl.BlockSpec((1, TQ, HEADS_PER_KV * HEAD_DIM), lambda bi, g, i: (bi, i, g)),
        out_shape=jax.ShapeDtypeStruct((b, s, NSA_WIDTH), F32),
        scratch_shapes=[pltpu.VMEM((TK, ncols), F32), pltpu.VMEM((TK, ncols), F32),
                        pltpu.VMEM((2, 1, ncols), F32), pltpu.VMEM((2, HEAD_DIM + 16, ncols), F32),
                        pltpu.SMEM((s // TK + 1,), I32), pltpu.SMEM((1,), I32)],
        compiler_params=_cparams("parallel", "parallel", "arbitrary"),
        name="nsa",
    )(q, kc, vct, ks, vst, kw, vwt, gt, cdiff, wdiff, ovl, onehot, onw, wmask)


def _hgrn_cum_matrix():
    c = HGRN_CHUNK
    t = np.arange(c)
    mats = [(t[None, :] <= t[:, None])]
    for half in HGRN_LEVELS:
        ref = (t & ~(2 * half - 1)) + half - 1
        mats.append(t[None, :] <= ref[:, None])
    return np.concatenate(mats, axis=0).astype(np.float32)


def _hgrn_kernel(q_ref, k_ref, lf_ref, v_ref, g_ref, onw_ref, cm_ref, o_ref, state_scr, *, n_chunks):
    c = HGRN_CHUNK

    @pl.when(pl.program_id(1) == 0)
    def _():
        state_scr[...] = jnp.zeros_like(state_scr)

    ri = lax.broadcasted_iota(I32, (c, c), 0)
    ci = lax.broadcasted_iota(I32, (c, c), 1)
    rsub = ri // HGRN_SUB
    level_masks = [((ri & ~(2 * h - 1)) == (ci & ~(2 * h - 1))) & ((ri & h) != 0) & ((ci & h) == 0)
                   for h in HGRN_LEVELS]
    diag = ri == ci

    def head_chunk(r0, hd, state_t):
        cols = slice(hd * HGRN_DIM, (hd + 1) * HGRN_DIM)
        q = q_ref[0, pl.ds(r0, c), cols]
        k = k_ref[0, pl.ds(r0, c), cols]
        lf = lf_ref[0, pl.ds(r0, c), cols] * LOG2E
        v = v_ref[0, pl.ds(r0, c), cols]
        cm = cm_ref[...]
        l1 = lf.astype(BF16)
        rest = lf - l1.astype(F32)
        l2 = rest.astype(BF16)
        l3 = (rest - l2.astype(F32)).astype(BF16)
        cums = _dot(cm, l1) + _dot(cm, l2) + _dot(cm, l3)
        cum = cums[0:c]
        o = _dot_nt((q * jnp.exp2(cum)).astype(BF16), state_t.astype(BF16))
        scores = jnp.where(diag, jnp.sum(q * k, axis=-1, keepdims=True), 0.0)

        def factored(ref, mask, acc):
            qs = q * jnp.exp2(jnp.minimum(cum - ref, 0.0))
            kd = k * jnp.exp2(jnp.minimum(ref - cum, 0.0))
            return jnp.where(mask, _dot_nt(qs.astype(BF16), kd.astype(BF16)), acc)

        for i in range(1, c // HGRN_SUB):
            scores = factored(cum[i * HGRN_SUB - 1:i * HGRN_SUB, :], (rsub == i) & (ci < i * HGRN_SUB), scores)
        for lv in range(len(HGRN_LEVELS)):
            scores = factored(cums[(lv + 1) * c:(lv + 2) * c], level_masks[lv], scores)
        for d in range(1, HGRN_LEAF):
            ksh = pltpu.roll(k, d, 0)
            csh = pltpu.roll(cum, d, 0)
            w = jnp.sum(q * ksh * jnp.exp2(cum - csh), axis=-1, keepdims=True)
            scores = jnp.where((ri - ci == d) & ((ri & (HGRN_LEAF - 1)) >= d), w, scores)
        o = o + _dot(scores.astype(BF16), v.astype(BF16))
        last = cum[c - 1:c, :]
        kd = (k * jnp.exp2(last - cum)).astype(BF16)
        state_t = state_t * jnp.exp2(last) + _dot(v.T.astype(BF16), kd)
        o = o * g_ref[0, pl.ds(r0, c), cols]
        o = o * lax.rsqrt(jnp.mean(o * o, axis=-1, keepdims=True) + RMS_EPS) * onw_ref[:, cols]
        o_ref[0, pl.ds(r0, c), cols] = o
        return state_t

    def chunk(ck, states):
        r0 = pl.multiple_of(ck * c, c)
        return tuple(head_chunk(r0, hd, states[hd]) for hd in range(HGRN_HEADS))

    states = lax.fori_loop(0, n_chunks, chunk, tuple(state_scr[hd] for hd in range(HGRN_HEADS)))
    for hd in range(HGRN_HEADS):
        state_scr[hd] = states[hd]


def _hgrn(hq, hk, hlf, hv, hg, rec_out_norm_w, rows):
    b, s, _ = hq.shape
    cm = jnp.asarray(_hgrn_cum_matrix()).astype(BF16)
    blk = pl.BlockSpec((1, rows, HGRN_WIDTH), lambda bi, i: (bi, i, 0))
    return pl.pallas_call(
        functools.partial(_hgrn_kernel, n_chunks=rows // HGRN_CHUNK),
        grid=(b, s // rows),
        in_specs=[blk, blk, blk, blk, blk,
                  pl.BlockSpec((1, HGRN_WIDTH), lambda bi, i: (0, 0)),
                  pl.BlockSpec(cm.shape, lambda bi, i: (0, 0))],
        out_specs=blk,
        out_shape=jax.ShapeDtypeStruct((b, s, HGRN_WIDTH), F32),
        scratch_shapes=[pltpu.VMEM((HGRN_HEADS, HGRN_DIM, HGRN_DIM), F32)],
        compiler_params=_cparams("parallel", "arbitrary"),
        name="hgrn",
    )(hq, hk, hlf, hv, hg, rec_out_norm_w.reshape(1, HGRN_WIDTH), cm)


def _outproj_kernel(x_ref, a_ref, r_ref, wa_ref, wr_ref, gt_ref, sc_ref, sh_ref, n2_ref, x1_ref, h2_ref, h2p_ref):
    mixed = _dot(a_ref[0].astype(BF16), wa_ref[...]) + _dot(r_ref[0].astype(BF16), wr_ref[...])
    x1 = x_ref[0] + gt_ref[0] * mixed
    x1_ref[0] = x1
    ms = jnp.mean(x1 * x1, axis=-1, keepdims=True)
    h2 = x1 * lax.rsqrt(ms + RMS_EPS) * n2_ref[...] * (1.0 + sc_ref[0]) + sh_ref[0]
    h2_ref[0] = h2
    h2p_ref[0] = _pack_bf16_pair(h2[:, :D_MODEL // 2], h2[:, D_MODEL // 2:])


def _outproj(x, attn, rec, w_out, gt1, sc2, sh2, norm2_w, tm):
    b, s, d = x.shape
    row = lambda bi, i: (bi, i, 0)
    per_b = lambda bi, i: (bi, 0, 0)
    fixed2 = lambda bi, i: (0, 0)
    w = w_out.astype(BF16)
    return pl.pallas_call(
        _outproj_kernel,
        grid=(b, s // tm),
        in_specs=[pl.BlockSpec((1, tm, d), row),
                  pl.BlockSpec((1, tm, NSA_WIDTH), row),
                  pl.BlockSpec((1, tm, HGRN_WIDTH), row),
                  pl.BlockSpec((NSA_WIDTH, d), fixed2),
                  pl.BlockSpec((HGRN_WIDTH, d), fixed2),
                  pl.BlockSpec((1, 1, d), per_b),
                  pl.BlockSpec((1, 1, d), per_b),
                  pl.BlockSpec((1, 1, d), per_b),
                  pl.BlockSpec((1, d), fixed2)],
        out_specs=(pl.BlockSpec((1, tm, d), row), pl.BlockSpec((1, tm, d), row), pl.BlockSpec((1, tm, d // 2), row)),
        out_shape=(jax.ShapeDtypeStruct((b, s, d), F32), jax.ShapeDtypeStruct((b, s, d), F32),
                   jax.ShapeDtypeStruct((b, s, d // 2), jnp.uint32)),
        compiler_params=_cparams("parallel", "parallel"),
        name="outproj",
    )(x, attn, rec, w[:NSA_WIDTH], w[NSA_WIDTH:], gt1, sc2, sh2, norm2_w)


def _mixer(x, c, ada_w, ada_b, norm1_w, norm2_w, w_in, q_norm_w, k_norm_w, cmp_pos, cmp_w1, cmp_b1, cmp_w2,
           attn_out_norm_w, hgrn_lb_param, rec_out_norm_w, w_out):
    b, s, d = x.shape
    mod = _mod(c, ada_w, ada_b)
    sh1, sc1, gt1, sh2, sc2, gt2 = [m.reshape(b, 1, d) for m in jnp.split(mod, 6, axis=-1)]
    o = NSA_WIDTH + 6 * KV_WIDTH
    w_cat = jnp.concatenate([w_in[:, :o], w_in[:, o:o + NSA_HEADS * 3],
                             jnp.zeros((d, GATE_PAD - NSA_HEADS * 3), w_in.dtype),
                             w_in[:, o + NSA_HEADS * 3:]], axis=1).astype(BF16)
    tm = min(256, s)
    (q, kc_raw, vc_raw, ks, vst, kw, vwt, gates_t, hq, hk, hlf, hv, hg) = _inproj(
        x, sc1, sh1, norm1_w.reshape(1, d), w_cat, q_norm_w.reshape(1, HEAD_DIM), k_norm_w, hgrn_lb_param, tm)
    kc, vct = _compress(kc_raw, vc_raw, cmp_pos, cmp_w1, cmp_b1, cmp_w2, k_norm_w)
    attn = _nsa(q, kc, vct, ks, vst, kw, vwt, gates_t, attn_out_norm_w)
    rec = _hgrn(hq, hk, hlf, hv, hg, rec_out_norm_w, min(512, s))
    x1, h2, h2p = _outproj(x, attn, rec, w_out, gt1, sc2, sh2, norm2_w.reshape(1, d), tm)
    return x1, h2, h2p, gt2


def _router_kernel(h_ref, rwt_ref, bias_ref, tri_ref, ones_ref, idx_ref, w_ref, rank_ref, cnt_ref, carry_scr, *, tr):
    @pl.when(pl.program_id(0) == 0)
    def _():
        carry_scr[...] = jnp.zeros_like(carry_scr)

    h = h_ref[...]
    h_hi = h.astype(BF16)
    h_lo = (h - h_hi.astype(F32)).astype(BF16)
    logits = _dot_nt(rwt_ref[0], h_hi) + _dot_nt(rwt_ref[1], h_hi) + _dot_nt(rwt_ref[0], h_lo)
    scores = _sigmoid(logits)
    biased = scores + bias_ref[...]
    neg = -jnp.inf

    gs = []
    for g in range(N_GROUPS):
        sub = biased[g * GROUP_SIZE:(g + 1) * GROUP_SIZE, :]
        m1 = jnp.max(sub, axis=0, keepdims=True)
        dup = jnp.sum((sub == m1).astype(F32), axis=0, keepdims=True)
        m2 = jnp.max(jnp.where(sub < m1, sub, neg), axis=0, keepdims=True)
        gs.append(m1 + jnp.where(dup >= 2.0, m1, m2))
    parts = []
    for g in range(N_GROUPS):
        beaten = jnp.zeros_like(gs[g])
        for g2 in range(N_GROUPS):
            if g2 != g:
                beats = (gs[g2] >= gs[g]) if g2 < g else (gs[g2] > gs[g])
                beaten = beaten + beats.astype(F32)
        sub = biased[g * GROUP_SIZE:(g + 1) * GROUP_SIZE, :]
        parts.append(jnp.where(beaten < float(TOPK_GROUPS), sub, neg))
    cand = jnp.concatenate(parts, axis=0)

    rowf = lax.broadcasted_iota(I32, (N_EXPERTS, tr), 0).astype(F32)
    idx_rows, w_rows, hits = [], [], []
    multi = jnp.zeros((N_EXPERTS, tr), F32)
    for _ in range(TOP_K):
        mx = jnp.max(cand, axis=0, keepdims=True)
        first = jnp.min(jnp.where(cand == mx, rowf, float(N_EXPERTS)), axis=0, keepdims=True)
        hit = rowf == first
        idx_rows.append(first)
        w_rows.append(jnp.sum(jnp.where(hit, scores, 0.0), axis=0, keepdims=True))
        cand = jnp.where(hit, neg, cand)
        multi = jnp.where(hit, 1.0, multi)
    w = jnp.concatenate(w_rows, axis=0)
    w_ref[...] = w / jnp.sum(w, axis=0, keepdims=True) * ROUTED_SCALE
    idx = jnp.concatenate(idx_rows, axis=0)
    idx_ref[...] = idx.astype(I32)

    carry = carry_scr[...]
    mb = multi.astype(BF16)
    before = _dot(mb, tri_ref[...]) + jnp.concatenate([carry] * (tr // 128), axis=1)
    rank_rows = [jnp.sum(jnp.where(rowf == idx_rows[k], before, 0.0), axis=0, keepdims=True) for k in range(TOP_K)]
    rank_ref[...] = jnp.concatenate(rank_rows, axis=0).astype(I32)
    carry = carry + _dot(mb, ones_ref[...])
    carry_scr[...] = carry
    cnt_ref[...] = carry


def _router(h2, router_w, router_bias, tr):
    t, d = h2.shape
    tri = jnp.asarray(np.triu(np.ones((tr, tr), np.float32), 1)).astype(BF16)
    ones = jnp.ones((tr, 128), BF16)
    tok = pl.BlockSpec((TOP_K, tr), lambda i: (0, i))
    fixed = lambda i: (0, 0)
    rwt = router_w.T
    rwt_hi = rwt.astype(BF16)
    rwt_split = jnp.stack([rwt_hi, (rwt - rwt_hi.astype(F32)).astype(BF16)])
    return pl.pallas_call(
        functools.partial(_router_kernel, tr=tr),
        grid=(t // tr,),
        in_specs=[pl.BlockSpec((tr, d), lambda i: (i, 0)),
                  pl.BlockSpec((2, N_EXPERTS, d), lambda i: (0, 0, 0)),
                  pl.BlockSpec((N_EXPERTS, 1), fixed),
                  pl.BlockSpec((tr, tr), fixed),
                  pl.BlockSpec((tr, 128), fixed)],
        out_specs=(tok, tok, tok, pl.BlockSpec((N_EXPERTS, 128), fixed)),
        out_shape=(jax.ShapeDtypeStruct((TOP_K, t), I32), jax.ShapeDtypeStruct((TOP_K, t), F32),
                   jax.ShapeDtypeStruct((TOP_K, t), I32), jax.ShapeDtypeStruct((N_EXPERTS, 128), F32)),
        scratch_shapes=[pltpu.VMEM((N_EXPERTS, 128), F32)],
        compiler_params=_cparams("arbitrary"),
        name="router",
    )(h2, rwt_split, router_bias.reshape(N_EXPERTS, 1), tri, ones)


def _pack_bf16_pair(a, b):
    ua = lax.bitcast_convert_type(a.astype(BF16).astype(F32), jnp.uint32)
    ub = lax.bitcast_convert_type(b.astype(BF16).astype(F32), jnp.uint32)
    return ua | (ub >> 16)


def _unpack_bf16_pair(w):
    a = lax.bitcast_convert_type(w & jnp.uint32(0xFFFF0000), F32)
    b = lax.bitcast_convert_type(w << 16, F32)
    return a, b


def _slot_kernel(ps_ref, idx_ref, rank_ref, slot_ref):
    idx = idx_ref[...]

    def body(e, acc):
        return jnp.where(idx == e, ps_ref[e], acc)

    slot_ref[...] = lax.fori_loop(0, N_EXPERTS, body, jnp.zeros_like(idx)) + rank_ref[...]


def _slots(pad_start, idx, rank, tt):
    t = idx.shape[1]
    tok = pl.BlockSpec((TOP_K, tt), lambda i, ps: (0, i))
    return pl.pallas_call(
        _slot_kernel,
        grid_spec=pltpu.PrefetchScalarGridSpec(num_scalar_prefetch=1, grid=(t // tt,),
                                               in_specs=[tok, tok], out_specs=tok),
        out_shape=jax.ShapeDtypeStruct((TOP_K, t), I32),
        compiler_params=_cparams("parallel"),
        name="slots",
    )(pad_start, idx, rank)


SC_CORES = 2
SC_SUBCORES = 16
SC_CHUNK = 64


def _sc_mesh():
    return plsc.VectorSubcoreMesh(core_axis_name="c", subcore_axis_name="s")


def _sc_dispatch(h2p, slot_chunks, n_rows):
    t, dw = h2p.shape
    per = slot_chunks.shape[0] // (SC_CORES * SC_SUBCORES)

    def body(h_hbm, slot_hbm, xs_hbm, idx_v, rows_v, sem):
        wid = lax.axis_index("s") * SC_CORES + lax.axis_index("c")

        @pl.loop(0, per)
        def _(c):
            ch = wid * per + c
            pltpu.sync_copy(slot_hbm.at[ch], idx_v)
            pltpu.sync_copy(h_hbm.at[pl.ds(ch * SC_CHUNK, SC_CHUNK)], rows_v)
            copies = [pltpu.async_copy(rows_v, xs_hbm.at[idx_v.at[k]], sem) for k in range(TOP_K)]
            for cp in copies:
                cp.wait()

    return pl.kernel(
        body, out_type=jax.ShapeDtypeStruct((n_rows, dw), h2p.dtype), mesh=_sc_mesh(),
        scratch_types=[pltpu.VMEM((TOP_K, SC_CHUNK), I32), pltpu.VMEM((SC_CHUNK, dw), h2p.dtype),
                       pltpu.SemaphoreType.DMA],
    )(h2p, slot_chunks)


def _sc_gather(ys, slot_chunks, t):
    dw = ys.shape[1]
    per = slot_chunks.shape[0] // (SC_CORES * SC_SUBCORES)

    def body(ys_hbm, slot_hbm, yg_hbm, idx_v, rows_v, gsem, wsem):
        wid = lax.axis_index("s") * SC_CORES + lax.axis_index("c")

        @pl.loop(0, per)
        def _(c):
            ch = wid * per + c
            pltpu.sync_copy(slot_hbm.at[ch], idx_v)
            gathers = [None] * TOP_K
            writes = [None] * TOP_K
            gathers[0] = pltpu.async_copy(ys_hbm.at[idx_v.at[0]], rows_v.at[0], gsem)
            for k in range(TOP_K):
                gathers[k].wait()
                if k + 1 < TOP_K:
                    if k >= 1:
                        writes[k - 1].wait()
                    gathers[k + 1] = pltpu.async_copy(ys_hbm.at[idx_v.at[k + 1]], rows_v.at[(k + 1) % 2], gsem)
                writes[k] = pltpu.async_copy(rows_v.at[k % 2], yg_hbm.at[k, pl.ds(ch * SC_CHUNK, SC_CHUNK)], wsem)
            writes[TOP_K - 2].wait()
            writes[TOP_K - 1].wait()

    return pl.kernel(
        body, out_type=jax.ShapeDtypeStruct((TOP_K, t, dw), ys.dtype), mesh=_sc_mesh(),
        scratch_types=[pltpu.VMEM((TOP_K, SC_CHUNK), I32), pltpu.VMEM((2, SC_CHUNK, dw), ys.dtype),
                       pltpu.SemaphoreType.DMA, pltpu.SemaphoreType.DMA],
    )(ys, slot_chunks)


def _experts_kernel(be_ref, nu_ref, bv_ref, run_ref, xs_hbm, wg_hbm, wu_hbm, wd_hbm, ys_ref,
                    xring, rsem, gring, uring, dring, wsem):
    i = pl.program_id(0)
    half = D_MODEL // 2
    n_used = nu_ref[0]
    n_steps = pl.num_programs(0)

    def weight_copies(blk):
        ex = be_ref[blk]
        slot = run_ref[blk] % EXPERT_RING
        return [pltpu.make_async_copy(src.at[ex], ring.at[slot], wsem.at[a, slot])
                for a, (src, ring) in enumerate(((wg_hbm, gring), (wu_hbm, uring), (wd_hbm, dring)))]

    def starts_run(blk):
        return run_ref[blk] != run_ref[jnp.maximum(blk - 1, 0)]

    @pl.when(i == 0)
    def _():
        for cp in weight_copies(jnp.int32(0)):
            cp.start()

        @pl.when((n_steps > 1) & starts_run(jnp.int32(1)))
        def _():
            for cp in weight_copies(jnp.int32(1)):
                cp.start()

    ahead = jnp.minimum(i + (EXPERT_RING - 1), n_steps - 1)

    @pl.when((i + (EXPERT_RING - 1) < n_steps) & starts_run(ahead))
    def _():
        for cp in weight_copies(ahead):
            cp.start()

    @pl.when((i == 0) | starts_run(i))
    def _():
        for cp in weight_copies(i):
            cp.wait()

    wslot = run_ref[i] % EXPERT_RING
    wg_ref, wu_ref, wd_ref = gring.at[wslot], uring.at[wslot], dring.at[wslot]

    def fetch(blk):
        slot = blk % EXPERT_RING
        return pltpu.make_async_copy(xs_hbm.at[pl.ds(pl.multiple_of(blk * EXPERT_BLOCK, EXPERT_BLOCK), EXPERT_BLOCK)],
                                     xring.at[slot], rsem.at[slot])

    @pl.when(i == 0)
    def _():
        for first in range(EXPERT_RING - 1):
            @pl.when(first < n_used)
            def _():
                fetch(jnp.int32(first)).start()

    @pl.when(i + (EXPERT_RING - 1) < n_used)
    def _():
        fetch(i + (EXPERT_RING - 1)).start()

    @pl.when(i < n_used)
    def _():
        fetch(i).wait()

    xs_ref = xring.at[i % EXPERT_RING]

    def ffn(rows):
        live = lax.broadcasted_iota(I32, (rows, xs_ref.shape[1]), 0) < bv_ref[i]
        xa, xb = _unpack_bf16_pair(jnp.where(live, xs_ref[0:rows, :], jnp.uint32(0)))
        xa, xb = xa.astype(BF16), xb.astype(BF16)
        g = _dot(xa, wg_ref[:half, :].astype(BF16)) + _dot(xb, wg_ref[half:, :].astype(BF16))
        u = _dot(xa, wu_ref[:half, :].astype(BF16)) + _dot(xb, wu_ref[half:, :].astype(BF16))
        act = (g * _sigmoid(g) * u).astype(BF16)
        y = _dot(act, wd_ref[...].astype(BF16))
        ys_ref[0:rows, :] = _pack_bf16_pair(y[:, :half], y[:, half:])

    used = i < n_used
    short = bv_ref[i] <= EXPERT_TAIL

    @pl.when(used & jnp.logical_not(short))
    def _():
        ffn(EXPERT_BLOCK)

    @pl.when(used & short)
    def _():
        ffn(EXPERT_TAIL)
        ys_ref[EXPERT_TAIL:, :] = jnp.zeros((EXPERT_BLOCK - EXPERT_TAIL, ys_ref.shape[1]), ys_ref.dtype)

    @pl.when(jnp.logical_not(used))
    def _():
        ys_ref[...] = jnp.zeros_like(ys_ref)


def _experts(xs, blk_e, n_used, blk_valid, w_gate, w_up, w_down):
    n_rows, dw = xs.shape
    d = w_gate.shape[1]
    nblk = n_rows // EXPERT_BLOCK
    blk_run = jnp.cumsum(jnp.concatenate([jnp.zeros((1,), I32), (blk_e[1:] != blk_e[:-1]).astype(I32)])).astype(I32)
    hbm = pl.BlockSpec(memory_space=pl.ANY)
    return pl.pallas_call(
        _experts_kernel,
        grid_spec=pltpu.PrefetchScalarGridSpec(
            num_scalar_prefetch=4,
            grid=(nblk,),
            in_specs=[hbm, hbm, hbm, hbm],
            out_specs=pl.BlockSpec((EXPERT_BLOCK, dw), lambda i, be, nu, bv, rn: (i, 0)),
            scratch_shapes=[pltpu.VMEM((EXPERT_RING, EXPERT_BLOCK, dw), xs.dtype),
                            pltpu.SemaphoreType.DMA((EXPERT_RING,)),
                            pltpu.VMEM((EXPERT_RING, d, EXPERT_FF), w_gate.dtype),
                            pltpu.VMEM((EXPERT_RING, d, EXPERT_FF), w_up.dtype),
                            pltpu.VMEM((EXPERT_RING, EXPERT_FF, d), w_down.dtype),
                            pltpu.SemaphoreType.DMA((3, EXPERT_RING))]),
        out_shape=jax.ShapeDtypeStruct((n_rows, dw), xs.dtype),
        compiler_params=pltpu.CompilerParams(dimension_semantics=("arbitrary",), vmem_limit_bytes=VMEM_LIMIT,
                                             has_side_effects=True),
        name="experts",
    )(blk_e, n_used, blk_valid, blk_run, xs, w_gate, w_up, w_down)


def _combine_kernel(x1_ref, h_ref, w_ref, gt_ref, sg_ref, su_ref, sd_ref, yg_ref, o_ref):
    tc = x1_ref.shape[0]
    half = D_MODEL // 2
    hb = h_ref[...].astype(BF16)
    g = _dot(hb, sg_ref[...])
    u = _dot(hb, su_ref[...])
    ffn = _dot((g * _sigmoid(g) * u).astype(BF16), sd_ref[...])

    w = w_ref[...]
    ra = jnp.zeros((tc, half), F32)
    rb = jnp.zeros((tc, half), F32)
    for k in range(TOP_K):
        ya, yb = _unpack_bf16_pair(yg_ref[k])
        ra = ra + w[:, k:k + 1] * ya
        rb = rb + w[:, k:k + 1] * yb
    ffn = ffn + jnp.concatenate([ra, rb], axis=1)
    o_ref[...] = x1_ref[...] + gt_ref[0] * ffn


def _combine(x1, h2, w_tok, gt2, yg, sg, su, sd, seq, tc):
    t, d = x1.shape
    row = lambda i: (i, 0)
    fixed = lambda i: (0, 0)
    return pl.pallas_call(
        _combine_kernel,
        grid=(t // tc,),
        in_specs=[pl.BlockSpec((tc, d), row),
                  pl.BlockSpec((tc, d), row),
                  pl.BlockSpec((tc, TOP_K), row),
                  pl.BlockSpec((1, 1, d), lambda i: ((i * tc) // seq, 0, 0)),
                  pl.BlockSpec((d, SHARED_FF), fixed),
                  pl.BlockSpec((d, SHARED_FF), fixed),
                  pl.BlockSpec((SHARED_FF, d), fixed),
                  pl.BlockSpec((TOP_K, tc, d // 2), lambda i: (0, i, 0))],
        out_specs=pl.BlockSpec((tc, d), row),
        out_shape=jax.ShapeDtypeStruct((t, d), F32),
        compiler_params=_cparams("parallel"),
        name="combine",
    )(x1, h2, w_tok, gt2, sg.astype(BF16), su.astype(BF16), sd.astype(BF16), yg)


def _moe_parts(x1, h2, h2p, gt2, router_w, router_bias, w_gate, w_up, w_down, sg, su, sd):
    b, s, d = x1.shape
    t = b * s
    h2 = h2.reshape(t, d)
    idx, w, rank, cnt = _router(h2, router_w, router_bias, min(256, t))
    counts = cnt[:, 0].astype(I32)
    padded = (counts + EXPERT_BLOCK - 1) // EXPERT_BLOCK * EXPERT_BLOCK
    pad_end = jnp.cumsum(padded)
    pad_start = pad_end - padded
    n_rows = t * TOP_K + N_EXPERTS * EXPERT_BLOCK
    nblk = n_rows // EXPERT_BLOCK
    n_used = (pad_end[-1:] // EXPERT_BLOCK).astype(I32)
    blk_start = jnp.arange(nblk, dtype=I32) * EXPERT_BLOCK
    owns = (pad_start[None, :] <= blk_start[:, None]) & (blk_start[:, None] < pad_end[None, :])
    e_ids = jnp.arange(N_EXPERTS, dtype=I32)[None, :]
    last_e = jnp.max(jnp.where(counts > 0, e_ids[0], 0))
    blk_e = jnp.where(blk_start < pad_end[-1], jnp.sum(jnp.where(owns, e_ids, 0), axis=1), last_e).astype(I32)
    rows_left = jnp.sum(jnp.where(owns, (pad_start + counts)[None, :] - blk_start[:, None], 0), axis=1)
    blk_valid = jnp.clip(rows_left, 0, EXPERT_BLOCK).astype(I32)
    slot = _slots(pad_start.astype(I32), idx, rank, min(2048, t))
    slot_chunks = slot.reshape(TOP_K, t // SC_CHUNK, SC_CHUNK).transpose(1, 0, 2)
    xs = _sc_dispatch(h2p.reshape(t, d // 2), slot_chunks, n_rows)
    ys = _experts(xs, blk_e, n_used, blk_valid, w_gate, w_up, w_down)
    yg = _sc_gather(ys, slot_chunks, t)
    out = _combine(x1.reshape(t, d), h2, w.T, gt2, yg, sg, su, sd, s, min(256, t))
    return out.reshape(b, s, d), dict(idx=idx, w=w, rank=rank, cnt=cnt)


def kernel(x, c, ada_w, ada_b, norm1_w, norm2_w, w_in, q_norm_w, k_norm_w, cmp_pos, cmp_w1, cmp_b1, cmp_w2, attn_out_norm_w, hgrn_lb_param, rec_out_norm_w, w_out, router_w, router_bias, exp_w_gate, exp_w_up, exp_w_down, shared_w_gate, shared_w_up, shared_w_down):
    assert ada_w.shape[0] == 1, "one layer"
    assert x.shape[0] <= 8 and x.shape[1] % TK == 0 and x.shape[1] >= WINDOW + TQ
    l = 0
    x1, h2, h2p, gt2 = _mixer(x, c, ada_w[l], ada_b[l], norm1_w[l], norm2_w[l], w_in[l], q_norm_w[l], k_norm_w[l],
                         cmp_pos[l], cmp_w1[l], cmp_b1[l], cmp_w2[l], attn_out_norm_w[l], hgrn_lb_param,
                         rec_out_norm_w[l], w_out[l])
    out, _ = _moe_parts(x1, h2, h2p, gt2, router_w[l], router_bias[l], exp_w_gate[l], exp_w_up[l], exp_w_down[l],
                        shared_w_gate[l], shared_w_up[l], shared_w_down[l])
    return out
```

```python
import functools

import numpy as np
import jax
import jax.numpy as jnp
from jax import lax
from jax.experimental import pallas as pl
from jax.experimental.pallas import tpu as pltpu
from jax.experimental.pallas import tpu_sc as plsc

F32 = jnp.float32
BF16 = jnp.bfloat16
I32 = jnp.int32

D_MODEL = 1024
NSA_HEADS = 8
HEAD_DIM = 64
NSA_WIDTH = NSA_HEADS * HEAD_DIM
KV_HEADS = 2
HEADS_PER_KV = NSA_HEADS // KV_HEADS
KV_WIDTH = KV_HEADS * HEAD_DIM
CMP_BLOCK = 32
CMP_STRIDE = 16
CMP_HIDDEN = 256
SEL_BLOCK = 64
N_SELECT = 16
WINDOW = 512
HGRN_HEADS = 4
HGRN_DIM = 128
HGRN_WIDTH = HGRN_HEADS * HGRN_DIM
HGRN_CHUNK = 64
HGRN_SUB = 16
HGRN_LEVELS = ()
HGRN_LEAF = 16
N_EXPERTS = 256
TOP_K = 8
N_GROUPS = 8
GROUP_SIZE = N_EXPERTS // N_GROUPS
TOPK_GROUPS = 4
EXPERT_FF = 256
SHARED_FF = 256
ROUTED_SCALE = 2.5
RMS_EPS = 1e-6
BIG = 1e9
LOG2E = 1.4426950408889634
GATE_PAD = 128
PROJ_COLS = NSA_WIDTH + 6 * KV_WIDTH + GATE_PAD + 4 * HGRN_WIDTH

VMEM_LIMIT = 56 * 1024 * 1024

TQ = 256
TK = 512
EXPERT_BLOCK = 512
EXPERT_TAIL = 128
EXPERT_RING = 3
HIGHEST = lax.Precision.HIGHEST


def _cparams(*sem):
    return pltpu.CompilerParams(dimension_semantics=sem, vmem_limit_bytes=VMEM_LIMIT)


def _sigmoid(x):
    return 1.0 / (1.0 + jnp.exp(-x))


def _dot_nt(a, b):
    return lax.dot_general(a, b, (((1,), (1,)), ((), ())), preferred_element_type=F32)


def _dot(a, b, **kw):
    return jnp.dot(a, b, preferred_element_type=F32, **kw)


def _split_dot(a_bf16_exact, x):
    hi = x.astype(BF16)
    lo = (x - hi.astype(F32)).astype(BF16)
    return _dot(a_bf16_exact, hi) + _dot(a_bf16_exact, lo)


def _mod_kernel(c_ref, w_ref, b_ref, o_ref):
    c = c_ref[...]
    cond = c * _sigmoid(c)
    o_ref[...] = _dot(cond, w_ref[...], precision=HIGHEST) + b_ref[...]


def _mod(c, ada_w, ada_b):
    b, d = c.shape
    rows = 8
    c_pad = jnp.zeros((rows, d), F32).at[:b].set(c)
    n = ada_w.shape[1]
    out = pl.pallas_call(
        _mod_kernel,
        grid=(n // d,),
        in_specs=[pl.BlockSpec((rows, d), lambda j: (0, 0)),
                  pl.BlockSpec((d, d), lambda j: (0, j)),
                  pl.BlockSpec((1, d), lambda j: (0, j))],
        out_specs=pl.BlockSpec((rows, d), lambda j: (0, j)),
        out_shape=jax.ShapeDtypeStruct((rows, n), F32),
        compiler_params=_cparams("parallel"),
        name="mod",
    )(c_pad, ada_w, ada_b.reshape(1, n))
    return out[:b]


def _head_rms(t, w):
    return t * lax.rsqrt(jnp.mean(t * t, axis=-1, keepdims=True) + RMS_EPS) * w


def _pos_digits(pos):
    lane = lax.broadcasted_iota(I32, pos.shape, 1)
    d0 = (lane == 0) | (lane == 3) | (lane == 6)
    d1 = (lane == 1) | (lane == 4) | (lane == 7)
    d2 = (lane == 2) | (lane == 5) | (lane == 8)
    dig = jnp.where(d0, pos >> 12, jnp.where(d1, (pos >> 6) & 63, jnp.where(d2, pos & 63, 0)))
    return dig.astype(F32)


def _inproj_kernel(x_ref, sc_ref, sh_ref, n1_ref, w_ref, qnw_ref, knw_ref, lbp_ref, qaug_ref,
                   q_ref, kcr_ref, vcr_ref, ks_ref, vst_ref, kw_ref, vwt_ref, gt_ref,
                   hq_ref, hk_ref, hlf_ref, hv_ref, hg_ref):
    x = x_ref[0]
    ms = jnp.mean(x * x, axis=-1, keepdims=True)
    h = x * lax.rsqrt(ms + RMS_EPS) * n1_ref[...] * (1.0 + sc_ref[0]) + sh_ref[0]
    p = _dot(h.astype(BF16), w_ref[...])
    tm = x.shape[0]

    qnw = qnw_ref[...]
    for hd in range(NSA_HEADS):
        t = p[:, hd * HEAD_DIM:(hd + 1) * HEAD_DIM]
        qn = _head_rms(t, qnw) * (HEAD_DIM ** -0.5 * LOG2E)
        qa = jnp.broadcast_to(qaug_ref[hd:hd + 1, :], (tm, HEAD_DIM))
        q_ref[0, hd] = jnp.concatenate([qn, qa], axis=1).astype(BF16)
    kaug = _pos_digits(pl.program_id(1) * tm + lax.broadcasted_iota(I32, (tm, HEAD_DIM), 0))

    o = NSA_WIDTH
    kcr_ref[0] = p[:, o:o + KV_WIDTH]
    vcr_ref[0] = p[:, o + KV_WIDTH:o + 2 * KV_WIDTH]
    ks = p[:, o + 2 * KV_WIDTH:o + 3 * KV_WIDTH]
    vs = p[:, o + 3 * KV_WIDTH:o + 4 * KV_WIDTH]
    kw = p[:, o + 4 * KV_WIDTH:o + 5 * KV_WIDTH]
    vw = p[:, o + 5 * KV_WIDTH:o + 6 * KV_WIDTH]
    for g in range(KV_HEADS):
        sl = slice(g * HEAD_DIM, (g + 1) * HEAD_DIM)
        ks_ref[0, g] = jnp.concatenate([_head_rms(ks[:, sl], knw_ref[1:2, :]), kaug], axis=1).astype(BF16)
        kw_ref[0, g] = jnp.concatenate([_head_rms(kw[:, sl], knw_ref[2:3, :]), kaug], axis=1).astype(BF16)
    vst = vs.T.astype(BF16)
    vwt = vw.T.astype(BF16)
    for g in range(KV_HEADS):
        vst_ref[0, g] = vst[g * HEAD_DIM:(g + 1) * HEAD_DIM, :]
        vwt_ref[0, g] = vwt[g * HEAD_DIM:(g + 1) * HEAD_DIM, :]

    o = NSA_WIDTH + 6 * KV_WIDTH
    gates = _sigmoid(p[:, o:o + GATE_PAD])
    gt_ref[0] = gates.T[:NSA_HEADS * 3, :]

    o = o + GATE_PAD
    hq = p[:, o:o + HGRN_WIDTH]
    hf = p[:, o + HGRN_WIDTH:o + 2 * HGRN_WIDTH]
    hi = p[:, o + 2 * HGRN_WIDTH:o + 3 * HGRN_WIDTH]
    hg = p[:, o + 3 * HGRN_WIDTH:o + 4 * HGRN_WIDTH]
    lbp = lbp_ref[...]
    e = jnp.exp(lbp - jnp.max(lbp, axis=0, keepdims=True))
    lb = e[0:1, :] / jnp.sum(e, axis=0, keepdims=True)
    f = lb + (1.0 - lb) * _sigmoid(hf)
    hq_ref[0] = hq * _sigmoid(hq) * (HGRN_DIM ** -0.5)
    hk_ref[0] = 1.0 - f
    hlf_ref[0] = jnp.log(f)
    hv_ref[0] = hi
    hg_ref[0] = _sigmoid(hg)


def _inproj(x, sc1, sh1, norm1_w, w_cat, q_norm_w, k_norm_w, lb_param, tm):
    b, s, d = x.shape
    row = lambda bi, i: (bi, i, 0)
    per_b = lambda bi, i: (bi, 0, 0)
    fixed2 = lambda bi, i: (0, 0)
    aw = 2 * HEAD_DIM
    rest = np.array([2.0 ** (-8.0 * (i + 1) / NSA_HEADS) for i in range(NSA_HEADS)], np.float64) * LOG2E
    qaug = np.zeros((NSA_HEADS, HEAD_DIM), np.float32)
    for i in range(3):
        term = rest.astype(np.float32).astype(BF16).astype(np.float64)
        rest = rest - term
        for dgt, wgt in enumerate((4096.0, 64.0, 1.0)):
            qaug[:, 3 * i + dgt] = term * wgt
    assert np.all(qaug == qaug.astype(BF16).astype(np.float32))
    out_shape = (
        jax.ShapeDtypeStruct((b, NSA_HEADS, s, aw), BF16),
        jax.ShapeDtypeStruct((b, s, KV_WIDTH), F32),
        jax.ShapeDtypeStruct((b, s, KV_WIDTH), F32),
        jax.ShapeDtypeStruct((b, KV_HEADS, s, aw), BF16),
        jax.ShapeDtypeStruct((b, KV_HEADS, HEAD_DIM, s), BF16),
        jax.ShapeDtypeStruct((b, KV_HEADS, s, aw), BF16),
        jax.ShapeDtypeStruct((b, KV_HEADS, HEAD_DIM, s), BF16),
        jax.ShapeDtypeStruct((b, NSA_HEADS * 3, s), F32),
    ) + tuple(jax.ShapeDtypeStruct((b, s, HGRN_WIDTH), F32) for _ in range(5))
    hm = lambda n, w: pl.BlockSpec((1, n, tm, w), lambda bi, i: (bi, 0, i, 0))
    hmt = lambda n, w: pl.BlockSpec((1, n, w, tm), lambda bi, i: (bi, 0, 0, i))
    out_specs = (
        hm(NSA_HEADS, aw),
        pl.BlockSpec((1, tm, KV_WIDTH), row),
        pl.BlockSpec((1, tm, KV_WIDTH), row),
        hm(KV_HEADS, aw), hmt(KV_HEADS, HEAD_DIM),
        hm(KV_HEADS, aw), hmt(KV_HEADS, HEAD_DIM),
        pl.BlockSpec((1, NSA_HEADS * 3, tm), lambda bi, i: (bi, 0, i)),
    ) + tuple(pl.BlockSpec((1, tm, HGRN_WIDTH), row) for _ in range(5))
    return pl.pallas_call(
        _inproj_kernel,
        grid=(b, s // tm),
        in_specs=[pl.BlockSpec((1, tm, d), row),
                  pl.BlockSpec((1, 1, d), per_b),
                  pl.BlockSpec((1, 1, d), per_b),
                  pl.BlockSpec((1, d), fixed2),
                  pl.BlockSpec((d, PROJ_COLS), fixed2),
                  pl.BlockSpec((1, HEAD_DIM), fixed2),
                  pl.BlockSpec((3, HEAD_DIM), fixed2),
                  pl.BlockSpec(lb_param.shape, fixed2),
                  pl.BlockSpec((NSA_HEADS, HEAD_DIM), fixed2)],
        out_specs=out_specs,
        out_shape=out_shape,
        compiler_params=_cparams("parallel", "parallel"),
        name="inproj",
    )(x, sc1, sh1, norm1_w, w_cat, q_norm_w, k_norm_w, lb_param, jnp.asarray(qaug))


def _gelu_tanh(x):
    return 0.5 * x * (1.0 + jnp.tanh(0.7978845608028654 * (x + 0.044715 * x * x * x)))


def _compress_kernel(kch_ref, vch_ref, pos_ref, wa_ref, wb_ref, b1_ref, w2_ref, knw_ref,
                     kc_ref, vct_ref):
    n = kch_ref.shape[1]
    outs = []
    for br, ch_ref in enumerate((kch_ref, vch_ref)):
        ch = ch_ref[0]
        a = _dot((ch + pos_ref[br, 0:1, :]).astype(BF16), wa_ref[br])
        bm = _dot((ch + pos_ref[br, 1:2, :]).astype(BF16), wb_ref[br])
        pre = a + pltpu.roll(bm, n - 1, 0) + b1_ref[br]
        hid = _gelu_tanh(pre).astype(BF16)
        outs.append([_dot(hid[:, g * CMP_HIDDEN:(g + 1) * CMP_HIDDEN], w2_ref[br]) for g in range(KV_HEADS)])
    end_digits = _pos_digits(lax.broadcasted_iota(I32, (n, HEAD_DIM), 0) * CMP_STRIDE + (CMP_BLOCK - 1))
    for g in range(KV_HEADS):
        kc_ref[0, g] = jnp.concatenate([_head_rms(outs[0][g], knw_ref[0:1, :]), end_digits], axis=1).astype(BF16)
    vct = jnp.concatenate(outs[1], axis=1).T.astype(BF16)
    for g in range(KV_HEADS):
        vct_ref[0, g] = vct[g * HEAD_DIM:(g + 1) * HEAD_DIM, :]


def _compress(kc_raw, vc_raw, cmp_pos, cmp_w1, cmp_b1, cmp_w2, k_norm_w):
    b, s, _ = kc_raw.shape
    n = s // CMP_STRIDE
    half = CMP_STRIDE
    cw = CMP_STRIDE * KV_WIDTH
    kch = kc_raw.reshape(b, n, cw)
    vch = vc_raw.reshape(b, n, cw)
    pos = cmp_pos.reshape(2, 2, half, 1, HEAD_DIM)
    pos = jnp.broadcast_to(pos, (2, 2, half, KV_HEADS, HEAD_DIM)).reshape(2, 2, cw)
    w1 = cmp_w1.reshape(2, 2, half, HEAD_DIM, CMP_HIDDEN)
    eye = jnp.eye(KV_HEADS, dtype=F32)
    wfull = jnp.einsum('rhjdn,gk->rhjgdkn', w1, eye).reshape(2, 2, cw, KV_HEADS * CMP_HIDDEN).astype(BF16)
    b1 = jnp.tile(cmp_b1.reshape(2, 1, CMP_HIDDEN), (1, 1, KV_HEADS))
    fix = lambda r: (lambda bi: (0,) * r)
    return pl.pallas_call(
        _compress_kernel,
        grid=(b,),
        in_specs=[pl.BlockSpec((1, n, cw), lambda bi: (bi, 0, 0)),
                  pl.BlockSpec((1, n, cw), lambda bi: (bi, 0, 0)),
                  pl.BlockSpec((2, 2, cw), fix(3)),
                  pl.BlockSpec((2, cw, KV_HEADS * CMP_HIDDEN), fix(3)),
                  pl.BlockSpec((2, cw, KV_HEADS * CMP_HIDDEN), fix(3)),
                  pl.BlockSpec((2, 1, KV_HEADS * CMP_HIDDEN), fix(3)),
                  pl.BlockSpec((2, CMP_HIDDEN, HEAD_DIM), fix(3)),
                  pl.BlockSpec((3, HEAD_DIM), fix(2))],
        out_specs=(pl.BlockSpec((1, KV_HEADS, n, 2 * HEAD_DIM), lambda bi: (bi, 0, 0, 0)),
                   pl.BlockSpec((1, KV_HEADS, HEAD_DIM, n), lambda bi: (bi, 0, 0, 0))),
        out_shape=(jax.ShapeDtypeStruct((b, KV_HEADS, n, 2 * HEAD_DIM), BF16),
                   jax.ShapeDtypeStruct((b, KV_HEADS, HEAD_DIM, n), BF16)),
        compiler_params=_cparams("parallel"),
        name="compress",
    )(kch, vch, pos, wfull[:, 0], wfull[:, 1], b1, cmp_w2.astype(BF16), k_norm_w)


def _nsa_kernel(q_ref, kc_ref, vct_ref, ks_ref, vst_ref, kw_ref, vwt_ref, gt_ref, cdiff_ref, wdiff_ref,
                ovl_ref, oh_ref, onw_ref, wmask_ref, o_ref, buf_a, buf_b, m_scr, acc_scr, lst, cnt, *, n_top):
    q0 = pl.program_id(2) * TQ
    ncols = HEADS_PER_KV * TQ
    q = q_ref[0].reshape(ncols, 2 * HEAD_DIM)
    ns = ovl_ref.shape[0]

    s = jnp.where(cdiff_ref[...] <= q0, _dot_nt(kc_ref[0, 0], q), -jnp.inf)
    m = jnp.max(s, axis=0, keepdims=True)
    m = jnp.where(m == -jnp.inf, 0.0, m)
    e = jnp.exp2(s - m)
    p = e / jnp.maximum(jnp.sum(e, axis=0, keepdims=True), 1e-30)
    o_c = _dot(vct_ref[0, 0], p.astype(BF16))

    psum = p[:, 0:TQ]
    for hh in range(1, HEADS_PER_KV):
        psum = psum + p[:, hh * TQ:(hh + 1) * TQ]
    imp = _split_dot(ovl_ref[...], psum)
    blk = lax.broadcasted_iota(I32, (ns, TQ), 0)
    tq = q0 + lax.broadcasted_iota(I32, (ns, TQ), 1)
    cur = tq >> 6
    forced = (blk == 0) | (blk == cur) | (blk == cur - 1)
    rank = jnp.where(forced, BIG, jnp.where(blk * SEL_BLOCK <= tq, imp, -BIG))

    blkf = blk.astype(F32)

    bias = jnp.full((ns, TQ), -1e30, F32)
    for _ in range(n_top):
        mx = jnp.max(rank, axis=0, keepdims=True)
        first = jnp.min(jnp.where(rank == mx, blkf, float(ns)), axis=0, keepdims=True)
        hit = blkf == first
        rank = jnp.where(hit, -jnp.inf, rank)
        bias = jnp.where(hit, 0.0, bias)

    bias = jnp.where(blk == 0, -1e30, bias)
    if ns < 128:
        bias = jnp.concatenate([bias, jnp.zeros((128 - ns, TQ), F32)], axis=0)
    bias_t = bias.T.astype(BF16)
    qq = jnp.concatenate([q, jnp.concatenate([bias_t] * HEADS_PER_KV, axis=0)], axis=1)
    ones_rows = jnp.ones((16, TK), BF16)

    def scores(j):
        k0 = pl.multiple_of(j * TK, TK)
        kk = jnp.concatenate([ks_ref[0, 0, pl.ds(k0, TK), :], oh_ref[pl.ds(k0, TK), :]], axis=1)
        return _dot_nt(kk, qq)

    def consume(buf, j, causal, part):
        sc = buf[...]
        if causal:
            sc = jnp.where(wdiff_ref[0:TK, :] + (q0 - j * TK) >= 0, sc, -1e30)
        k0 = pl.multiple_of(j * TK, TK)
        m_run = m_scr[part]
        m_new = jnp.maximum(m_run, jnp.max(sc, axis=0, keepdims=True))
        ex = jnp.exp2(sc - m_new).astype(BF16)
        va = jnp.concatenate([vst_ref[0, 0, :, pl.ds(k0, TK)], ones_rows], axis=0)
        acc_scr[part] = jnp.exp2(m_run - m_new) * acc_scr[part] + _dot(va, ex)
        m_scr[part] = m_new

    n_past = q0 // TK
    blocks_per_tile = TK // SEL_BLOCK
    cnt[0] = 0
    for j in range(ks_ref.shape[2] // TK):
        wanted = jnp.max(bias[j * blocks_per_tile:(j + 1) * blocks_per_tile, :]) == 0.0

        @pl.when(wanted & (j < n_past))
        def _():
            lst[cnt[0]] = j
            cnt[0] = cnt[0] + 1

    n_sel = cnt[0]
    lst[n_sel] = n_past

    buf_a[...] = scores(lst[0])

    s0 = jnp.where(wdiff_ref[0:SEL_BLOCK, :] + q0 >= 0, _dot_nt(ks_ref[0, 0, 0:SEL_BLOCK, :], q), -1e30)
    m0 = jnp.max(s0, axis=0, keepdims=True)
    v0 = jnp.concatenate([vst_ref[0, 0, :, 0:SEL_BLOCK], jnp.ones((16, SEL_BLOCK), BF16)], axis=0)
    m_scr[0] = m0
    acc_scr[0] = _dot(v0, jnp.exp2(s0 - m0).astype(BF16))
    m_scr[1] = jnp.full((1, ncols), -1e30, F32)
    acc_scr[1] = jnp.zeros((HEAD_DIM + 16, ncols), F32)

    nw = WINDOW + TQ
    start = pl.multiple_of(jnp.maximum(q0 - WINDOW, 0), TQ)
    sw = _dot_nt(kw_ref[0, 0, pl.ds(start, nw), :], q) + wmask_ref[0]
    ew = jnp.exp2(sw - jnp.max(sw, axis=0, keepdims=True))
    vw_aug = jnp.concatenate([vwt_ref[0, 0, :, pl.ds(start, nw)], jnp.ones((16, nw), BF16)], axis=0)
    acc_w = _dot(vw_aug, ew.astype(BF16))
    o_w = acc_w[0:HEAD_DIM, :] / acc_w[HEAD_DIM:HEAD_DIM + 1, :]

    def tiles(first, count):
        for u in range(0, count, 2):
            buf_b[...] = scores(lst[first + u + 1])
            consume(buf_a, lst[first + u], False, 0)
            buf_a[...] = scores(lst[first + u + 2])
            consume(buf_b, lst[first + u + 1], False, 1)
        return 0

    lax.fori_loop(0, n_sel // 4, lambda i, _: tiles(4 * i, 4), 0)
    lax.fori_loop(0, (n_sel // 2) % 2, lambda i, _: tiles((n_sel // 4) * 4, 2), 0)

    @pl.when(n_sel % 2 == 1)
    def _():
        buf_b[...] = scores(n_past)
        consume(buf_a, lst[n_sel - 1], False, 0)
        consume(buf_b, n_past, True, 1)

    @pl.when(n_sel % 2 == 0)
    def _():
        consume(buf_a, n_past, True, 0)

    m_all = jnp.maximum(m_scr[0], m_scr[1])
    acc_s = jnp.exp2(m_scr[0] - m_all) * acc_scr[0] + jnp.exp2(m_scr[1] - m_all) * acc_scr[1]
    o_s = acc_s[0:HEAD_DIM, :] / acc_s[HEAD_DIM:HEAD_DIM + 1, :]

    gt = gt_ref[0, 0]
    outs = []
    for hh in range(HEADS_PER_KV):
        cs = slice(hh * TQ, (hh + 1) * TQ)
        o = (gt[3 * hh:3 * hh + 1, :] * o_c[:, cs] + gt[3 * hh + 1:3 * hh + 2, :] * o_s[:, cs]
             + gt[3 * hh + 2:3 * hh + 3, :] * o_w[:, cs])
        o = o * lax.rsqrt(jnp.mean(o * o, axis=0, keepdims=True) + RMS_EPS) * onw_ref[0, hh]
        outs.append(o)
    o_ref[0] = jnp.concatenate(outs, axis=0).T


def _nsa(q, kc, vct, ks, vst, kw, vwt, gates_t, attn_out_norm_w):
    b, _, s, aw = q.shape
    nc = kc.shape[2]
    ns = s // SEL_BLOCK
    n_top = min(N_SELECT, ns)
    ncols = HEADS_PER_KV * TQ
    nw = WINDOW + TQ
    tl = np.arange(ncols)[None, :] & (TQ - 1)
    cdiff = jnp.asarray((np.arange(nc)[:, None] * CMP_STRIDE + (CMP_BLOCK - 1) - tl).astype(np.int32))
    wdiff_np = (tl - np.arange(nw)[:, None]).astype(np.int32)
    wdiff = jnp.asarray(wdiff_np)
    n_off = WINDOW // TQ + 1
    dist_np = wdiff_np[None] + (np.arange(n_off) * TQ)[:, None, None]
    wmask = jnp.asarray(np.where((dist_np >= 0) & (dist_np < WINDOW), 0.0, -np.inf).astype(np.float32))
    ci = np.arange(nc)[None, :] * CMP_STRIDE
    bj = np.arange(ns)[:, None]
    ovl = ((ci < (bj + 1) * SEL_BLOCK) & (ci + CMP_BLOCK > bj * SEL_BLOCK) & (np.arange(nc)[None, :] < nc - 1))
    ovl = jnp.asarray(ovl.astype(np.float32)).astype(BF16)
    assert ns <= 128
    onehot = (np.arange(s)[:, None] // SEL_BLOCK == np.arange(128)[None, :])
    onehot = jnp.asarray(onehot.astype(np.float32)).astype(BF16)
    onw = jnp.broadcast_to(attn_out_norm_w.reshape(KV_HEADS, HEADS_PER_KV, HEAD_DIM, 1),
                           (KV_HEADS, HEADS_PER_KV, HEAD_DIM, TQ))
    gt = gates_t.reshape(b, KV_HEADS, HEADS_PER_KV * 3, s)
    per_bg = lambda bi, g, i: (bi, g, 0, 0)
    fixed = lambda bi, g, i: (0, 0)
    return pl.pallas_call(
        functools.partial(_nsa_kernel, n_top=n_top),
        grid=(b, KV_HEADS, s // TQ),
        in_specs=[pl.BlockSpec((1, HEADS_PER_KV, TQ, aw), lambda bi, g, i: (bi, g, i, 0)),
                  pl.BlockSpec((1, 1, nc, aw), per_bg),
                  pl.BlockSpec((1, 1, HEAD_DIM, nc), per_bg),
                  pl.BlockSpec((1, 1, s, aw), per_bg),
                  pl.BlockSpec((1, 1, HEAD_DIM, s), per_bg),
                  pl.BlockSpec((1, 1, s, aw), per_bg),
                  pl.BlockSpec((1, 1, HEAD_DIM, s), per_bg),
                  pl.BlockSpec((1, 1, HEADS_PER_KV * 3, TQ), lambda bi, g, i: (bi, g, 0, i)),
                  pl.BlockSpec((nc, ncols), fixed, pipeline_mode=pl.Buffered(1)),
                  pl.BlockSpec((nw, ncols), fixed, pipeline_mode=pl.Buffered(1)),
                  pl.BlockSpec((ns, nc), fixed, pipeline_mode=pl.Buffered(1)),
                  pl.BlockSpec((s, 128), fixed, pipeline_mode=pl.Buffered(1)),
                  pl.BlockSpec((1, HEADS_PER_KV, HEAD_DIM, TQ), lambda bi, g, i: (g, 0, 0, 0)),
                  pl.BlockSpec((1, nw, ncols), lambda bi, g, i: (jnp.minimum(i, n_off - 1), 0, 0))],
        out_specs=pl.BlockSpec((1, TQ, HEADS_PER_KV * HEAD_DIM), lambda bi, g, i: (bi, i, g)),
        out_shape=jax.ShapeDtypeStruct((b, s, NSA_WIDTH), F32),
        scratch_shapes=[pltpu.VMEM((TK, ncols), F32), pltpu.VMEM((TK, ncols), F32),
                        pltpu.VMEM((2, 1, ncols), F32), pltpu.VMEM((2, HEAD_DIM + 16, ncols), F32),
                        pltpu.SMEM((s // TK + 1,), I32), pltpu.SMEM((1,), I32)],
        compiler_params=_cparams("parallel", "parallel", "arbitrary"),
        name="nsa",
    )(q, kc, vct, ks, vst, kw, vwt, gt, cdiff, wdiff, ovl, onehot, onw, wmask)


def _hgrn_cum_matrix():
    c = HGRN_CHUNK
    t = np.arange(c)
    mats = [(t[None, :] <= t[:, None])]
    for half in HGRN_LEVELS:
        ref = (t & ~(2 * half - 1)) + half - 1
        mats.append(t[None, :] <= ref[:, None])
    return np.concatenate(mats, axis=0).astype(np.float32)


def _hgrn_kernel(q_ref, k_ref, lf_ref, v_ref, g_ref, onw_ref, cm_ref, o_ref, state_scr, *, n_chunks):
    c = HGRN_CHUNK

    @pl.when(pl.program_id(1) == 0)
    def _():
        state_scr[...] = jnp.zeros_like(state_scr)

    ri = lax.broadcasted_iota(I32, (c, c), 0)
    ci = lax.broadcasted_iota(I32, (c, c), 1)
    rsub = ri // HGRN_SUB
    level_masks = [((ri & ~(2 * h - 1)) == (ci & ~(2 * h - 1))) & ((ri & h) != 0) & ((ci & h) == 0)
                   for h in HGRN_LEVELS]
    diag = ri == ci

    def head_chunk(r0, hd, state_t):
        cols = slice(hd * HGRN_DIM, (hd + 1) * HGRN_DIM)
        q = q_ref[0, pl.ds(r0, c), cols]
        k = k_ref[0, pl.ds(r0, c), cols]
        lf = lf_ref[0, pl.ds(r0, c), cols] * LOG2E
        v = v_ref[0, pl.ds(r0, c), cols]
        cm = cm_ref[...]
        l1 = lf.astype(BF16)
        rest = lf - l1.astype(F32)
        l2 = rest.astype(BF16)
        l3 = (rest - l2.astype(F32)).astype(BF16)
        cums = _dot(cm, l1) + _dot(cm, l2) + _dot(cm, l3)
        cum = cums[0:c]
        o = _dot_nt((q * jnp.exp2(cum)).astype(BF16), state_t.astype(BF16))
        scores = jnp.where(diag, jnp.sum(q * k, axis=-1, keepdims=True), 0.0)

        def factored(ref, mask, acc):
            qs = q * jnp.exp2(jnp.minimum(cum - ref, 0.0))
            kd = k * jnp.exp2(jnp.minimum(ref - cum, 0.0))
            return jnp.where(mask, _dot_nt(qs.astype(BF16), kd.astype(BF16)), acc)

        for i in range(1, c // HGRN_SUB):
            scores = factored(cum[i * HGRN_SUB - 1:i * HGRN_SUB, :], (rsub == i) & (ci < i * HGRN_SUB), scores)
        for lv in range(len(HGRN_LEVELS)):
            scores = factored(cums[(lv + 1) * c:(lv + 2) * c], level_masks[lv], scores)
        for d in range(1, HGRN_LEAF):
            ksh = pltpu.roll(k, d, 0)
            csh = pltpu.roll(cum, d, 0)
            w = jnp.sum(q * ksh * jnp.exp2(cum - csh), axis=-1, keepdims=True)
            scores = jnp.where((ri - ci == d) & ((ri & (HGRN_LEAF - 1)) >= d), w, scores)
        o = o + _dot(scores.astype(BF16), v.astype(BF16))
        last = cum[c - 1:c, :]
        kd = (k * jnp.exp2(last - cum)).astype(BF16)
        state_t = state_t * jnp.exp2(last) + _dot(v.T.astype(BF16), kd)
        o = o * g_ref[0, pl.ds(r0, c), cols]
        o = o * lax.rsqrt(jnp.mean(o * o, axis=-1, keepdims=True) + RMS_EPS) * onw_ref[:, cols]
        o_ref[0, pl.ds(r0, c), cols] = o
        return state_t

    def chunk(ck, states):
        r0 = pl.multiple_of(ck * c, c)
        return tuple(head_chunk(r0, hd, states[hd]) for hd in range(HGRN_HEADS))

    states = lax.fori_loop(0, n_chunks, chunk, tuple(state_scr[hd] for hd in range(HGRN_HEADS)))
    for hd in range(HGRN_HEADS):
        state_scr[hd] = states[hd]


def _hgrn(hq, hk, hlf, hv, hg, rec_out_norm_w, rows):
    b, s, _ = hq.shape
    cm = jnp.asarray(_hgrn_cum_matrix()).astype(BF16)
    blk = pl.BlockSpec((1, rows, HGRN_WIDTH), lambda bi, i: (bi, i, 0))
    return pl.pallas_call(
        functools.partial(_hgrn_kernel, n_chunks=rows // HGRN_CHUNK),
        grid=(b, s // rows),
        in_specs=[blk, blk, blk, blk, blk,
                  pl.BlockSpec((1, HGRN_WIDTH), lambda bi, i: (0, 0)),
                  pl.BlockSpec(cm.shape, lambda bi, i: (0, 0))],
        out_specs=blk,
        out_shape=jax.ShapeDtypeStruct((b, s, HGRN_WIDTH), F32),
        scratch_shapes=[pltpu.VMEM((HGRN_HEADS, HGRN_DIM, HGRN_DIM), F32)],
        compiler_params=_cparams("parallel", "arbitrary"),
        name="hgrn",
    )(hq, hk, hlf, hv, hg, rec_out_norm_w.reshape(1, HGRN_WIDTH), cm)


def _outproj_kernel(x_ref, a_ref, r_ref, wa_ref, wr_ref, gt_ref, sc_ref, sh_ref, n2_ref, x1_ref, h2_ref, h2p_ref):
    mixed = _dot(a_ref[0].astype(BF16), wa_ref[...]) + _dot(r_ref[0].astype(BF16), wr_ref[...])
    x1 = x_ref[0] + gt_ref[0] * mixed
    x1_ref[0] = x1
    ms = jnp.mean(x1 * x1, axis=-1, keepdims=True)
    h2 = x1 * lax.rsqrt(ms + RMS_EPS) * n2_ref[...] * (1.0 + sc_ref[0]) + sh_ref[0]
    h2_ref[0] = h2
    h2p_ref[0] = _pack_bf16_pair(h2[:, :D_MODEL // 2], h2[:, D_MODEL // 2:])


def _outproj(x, attn, rec, w_out, gt1, sc2, sh2, norm2_w, tm):
    b, s, d = x.shape
    row = lambda bi, i: (bi, i, 0)
    per_b = lambda bi, i: (bi, 0, 0)
    fixed2 = lambda bi, i: (0, 0)
    w = w_out.astype(BF16)
    return pl.pallas_call(
        _outproj_kernel,
        grid=(b, s // tm),
        in_specs=[pl.BlockSpec((1, tm, d), row),
                  pl.BlockSpec((1, tm, NSA_WIDTH), row),
                  pl.BlockSpec((1, tm, HGRN_WIDTH), row),
                  pl.BlockSpec((NSA_WIDTH, d), fixed2),
                  pl.BlockSpec((HGRN_WIDTH, d), fixed2),
                  pl.BlockSpec((1, 1, d), per_b),
                  pl.BlockSpec((1, 1, d), per_b),
                  pl.BlockSpec((1, 1, d), per_b),
                  pl.BlockSpec((1, d), fixed2)],
        out_specs=(pl.BlockSpec((1, tm, d), row), pl.BlockSpec((1, tm, d), row), pl.BlockSpec((1, tm, d // 2), row)),
        out_shape=(jax.ShapeDtypeStruct((b, s, d), F32), jax.ShapeDtypeStruct((b, s, d), F32),
                   jax.ShapeDtypeStruct((b, s, d // 2), jnp.uint32)),
        compiler_params=_cparams("parallel", "parallel"),
        name="outproj",
    )(x, attn, rec, w[:NSA_WIDTH], w[NSA_WIDTH:], gt1, sc2, sh2, norm2_w)


def _mixer(x, c, ada_w, ada_b, norm1_w, norm2_w, w_in, q_norm_w, k_norm_w, cmp_pos, cmp_w1, cmp_b1, cmp_w2,
           attn_out_norm_w, hgrn_lb_param, rec_out_norm_w, w_out):
    b, s, d = x.shape
    mod = _mod(c, ada_w, ada_b)
    sh1, sc1, gt1, sh2, sc2, gt2 = [m.reshape(b, 1, d) for m in jnp.split(mod, 6, axis=-1)]
    o = NSA_WIDTH + 6 * KV_WIDTH
    w_cat = jnp.concatenate([w_in[:, :o], w_in[:, o:o + NSA_HEADS * 3],
                             jnp.zeros((d, GATE_PAD - NSA_HEADS * 3), w_in.dtype),
                             w_in[:, o + NSA_HEADS * 3:]], axis=1).astype(BF16)
    tm = min(256, s)
    (q, kc_raw, vc_raw, ks, vst, kw, vwt, gates_t, hq, hk, hlf, hv, hg) = _inproj(
        x, sc1, sh1, norm1_w.reshape(1, d), w_cat, q_norm_w.reshape(1, HEAD_DIM), k_norm_w, hgrn_lb_param, tm)
    kc, vct = _compress(kc_raw, vc_raw, cmp_pos, cmp_w1, cmp_b1, cmp_w2, k_norm_w)
    attn = _nsa(q, kc, vct, ks, vst, kw, vwt, gates_t, attn_out_norm_w)
    rec = _hgrn(hq, hk, hlf, hv, hg, rec_out_norm_w, min(512, s))
    x1, h2, h2p = _outproj(x, attn, rec, w_out, gt1, sc2, sh2, norm2_w.reshape(1, d), tm)
    return x1, h2, h2p, gt2


def _router_kernel(h_ref, rwt_ref, bias_ref, tri_ref, ones_ref, idx_ref, w_ref, rank_ref, cnt_ref, carry_scr, *, tr):
    @pl.when(pl.program_id(0) == 0)
    def _():
        carry_scr[...] = jnp.zeros_like(carry_scr)

    h = h_ref[...]
    h_hi = h.astype(BF16)
    h_lo = (h - h_hi.astype(F32)).astype(BF16)
    logits = _dot_nt(rwt_ref[0], h_hi) + _dot_nt(rwt_ref[1], h_hi) + _dot_nt(rwt_ref[0], h_lo)
    scores = _sigmoid(logits)
    biased = scores + bias_ref[...]
    neg = -jnp.inf

    gs = []
    for g in range(N_GROUPS):
        sub = biased[g * GROUP_SIZE:(g + 1) * GROUP_SIZE, :]
        m1 = jnp.max(sub, axis=0, keepdims=True)
        dup = jnp.sum((sub == m1).astype(F32), axis=0, keepdims=True)
        m2 = jnp.max(jnp.where(sub < m1, sub, neg), axis=0, keepdims=True)
        gs.append(m1 + jnp.where(dup >= 2.0, m1, m2))
    parts = []
    for g in range(N_GROUPS):
        beaten = jnp.zeros_like(gs[g])
        for g2 in range(N_GROUPS):
            if g2 != g:
                beats = (gs[g2] >= gs[g]) if g2 < g else (gs[g2] > gs[g])
                beaten = beaten + beats.astype(F32)
        sub = biased[g * GROUP_SIZE:(g + 1) * GROUP_SIZE, :]
        parts.append(jnp.where(beaten < float(TOPK_GROUPS), sub, neg))
    cand = jnp.concatenate(parts, axis=0)

    rowf = lax.broadcasted_iota(I32, (N_EXPERTS, tr), 0).astype(F32)
    idx_rows, w_rows, hits = [], [], []
    multi = jnp.zeros((N_EXPERTS, tr), F32)
    for _ in range(TOP_K):
        mx = jnp.max(cand, axis=0, keepdims=True)
        first = jnp.min(jnp.where(cand == mx, rowf, float(N_EXPERTS)), axis=0, keepdims=True)
        hit = rowf == first
        idx_rows.append(first)
        w_rows.append(jnp.sum(jnp.where(hit, scores, 0.0), axis=0, keepdims=True))
        cand = jnp.where(hit, neg, cand)
        multi = jnp.where(hit, 1.0, multi)
    w = jnp.concatenate(w_rows, axis=0)
    w_ref[...] = w / jnp.sum(w, axis=0, keepdims=True) * ROUTED_SCALE
    idx = jnp.concatenate(idx_rows, axis=0)
    idx_ref[...] = idx.astype(I32)

    carry = carry_scr[...]
    mb = multi.astype(BF16)
    before = _dot(mb, tri_ref[...]) + jnp.concatenate([carry] * (tr // 128), axis=1)
    rank_rows = [jnp.sum(jnp.where(rowf == idx_rows[k], before, 0.0), axis=0, keepdims=True) for k in range(TOP_K)]
    rank_ref[...] = jnp.concatenate(rank_rows, axis=0).astype(I32)
    carry = carry + _dot(mb, ones_ref[...])
    carry_scr[...] = carry
    cnt_ref[...] = carry


def _router(h2, router_w, router_bias, tr):
    t, d = h2.shape
    tri = jnp.asarray(np.triu(np.ones((tr, tr), np.float32), 1)).astype(BF16)
    ones = jnp.ones((tr, 128), BF16)
    tok = pl.BlockSpec((TOP_K, tr), lambda i: (0, i))
    fixed = lambda i: (0, 0)
    rwt = router_w.T
    rwt_hi = rwt.astype(BF16)
    rwt_split = jnp.stack([rwt_hi, (rwt - rwt_hi.astype(F32)).astype(BF16)])
    return pl.pallas_call(
        functools.partial(_router_kernel, tr=tr),
        grid=(t // tr,),
        in_specs=[pl.BlockSpec((tr, d), lambda i: (i, 0)),
                  pl.BlockSpec((2, N_EXPERTS, d), lambda i: (0, 0, 0)),
                  pl.BlockSpec((N_EXPERTS, 1), fixed),
                  pl.BlockSpec((tr, tr), fixed),
                  pl.BlockSpec((tr, 128), fixed)],
        out_specs=(tok, tok, tok, pl.BlockSpec((N_EXPERTS, 128), fixed)),
        out_shape=(jax.ShapeDtypeStruct((TOP_K, t), I32), jax.ShapeDtypeStruct((TOP_K, t), F32),
                   jax.ShapeDtypeStruct((TOP_K, t), I32), jax.ShapeDtypeStruct((N_EXPERTS, 128), F32)),
        scratch_shapes=[pltpu.VMEM((N_EXPERTS, 128), F32)],
        compiler_params=_cparams("arbitrary"),
        name="router",
    )(h2, rwt_split, router_bias.reshape(N_EXPERTS, 1), tri, ones)


def _pack_bf16_pair(a, b):
    ua = lax.bitcast_convert_type(a.astype(BF16).astype(F32), jnp.uint32)
    ub = lax.bitcast_convert_type(b.astype(BF16).astype(F32), jnp.uint32)
    return ua | (ub >> 16)


def _unpack_bf16_pair(w):
    a = lax.bitcast_convert_type(w & jnp.uint32(0xFFFF0000), F32)
    b = lax.bitcast_convert_type(w << 16, F32)
    return a, b


def _slot_kernel(ps_ref, idx_ref, rank_ref, slot_ref):
    idx = idx_ref[...]

    def body(e, acc):
        return jnp.where(idx == e, ps_ref[e], acc)

    slot_ref[...] = lax.fori_loop(0, N_EXPERTS, body, jnp.zeros_like(idx)) + rank_ref[...]


def _slots(pad_start, idx, rank, tt):
    t = idx.shape[1]
    tok = pl.BlockSpec((TOP_K, tt), lambda i, ps: (0, i))
    return pl.pallas_call(
        _slot_kernel,
        grid_spec=pltpu.PrefetchScalarGridSpec(num_scalar_prefetch=1, grid=(t // tt,),
                                               in_specs=[tok, tok], out_specs=tok),
        out_shape=jax.ShapeDtypeStruct((TOP_K, t), I32),
        compiler_params=_cparams("parallel"),
        name="slots",
    )(pad_start, idx, rank)


SC_CORES = 2
SC_SUBCORES = 16
SC_CHUNK = 64


def _sc_mesh():
    return plsc.VectorSubcoreMesh(core_axis_name="c", subcore_axis_name="s")


def _sc_dispatch(h2p, slot_chunks, n_rows):
    t, dw = h2p.shape
    per = slot_chunks.shape[0] // (SC_CORES * SC_SUBCORES)

    def body(h_hbm, slot_hbm, xs_hbm, idx_v, rows_v, sem):
        wid = lax.axis_index("s") * SC_CORES + lax.axis_index("c")

        @pl.loop(0, per)
        def _(c):
            ch = wid * per + c
            pltpu.sync_copy(slot_hbm.at[ch], idx_v)
            pltpu.sync_copy(h_hbm.at[pl.ds(ch * SC_CHUNK, SC_CHUNK)], rows_v)
            copies = [pltpu.async_copy(rows_v, xs_hbm.at[idx_v.at[k]], sem) for k in range(TOP_K)]
            for cp in copies:
                cp.wait()

    return pl.kernel(
        body, out_type=jax.ShapeDtypeStruct((n_rows, dw), h2p.dtype), mesh=_sc_mesh(),
        scratch_types=[pltpu.VMEM((TOP_K, SC_CHUNK), I32), pltpu.VMEM((SC_CHUNK, dw), h2p.dtype),
                       pltpu.SemaphoreType.DMA],
    )(h2p, slot_chunks)


def _sc_gather(ys, slot_chunks, t):
    dw = ys.shape[1]
    per = slot_chunks.shape[0] // (SC_CORES * SC_SUBCORES)

    def body(ys_hbm, slot_hbm, yg_hbm, idx_v, rows_v, gsem, wsem):
        wid = lax.axis_index("s") * SC_CORES + lax.axis_index("c")

        @pl.loop(0, per)
        def _(c):
            ch = wid * per + c
            pltpu.sync_copy(slot_hbm.at[ch], idx_v)
            gathers = [None] * TOP_K
            writes = [None] * TOP_K
            gathers[0] = pltpu.async_copy(ys_hbm.at[idx_v.at[0]], rows_v.at[0], gsem)
            for k in range(TOP_K):
                gathers[k].wait()
                if k + 1 < TOP_K:
                    if k >= 1:
                        writes[k - 1].wait()
                    gathers[k + 1] = pltpu.async_copy(ys_hbm.at[idx_v.at[k + 1]], rows_v.at[(k + 1) % 2], gsem)
                writes[k] = pltpu.async_copy(rows_v.at[k % 2], yg_hbm.at[k, pl.ds(ch * SC_CHUNK, SC_CHUNK)], wsem)
            writes[TOP_K - 2].wait()
            writes[TOP_K - 1].wait()

    return pl.kernel(
        body, out_type=jax.ShapeDtypeStruct((TOP_K, t, dw), ys.dtype), mesh=_sc_mesh(),
        scratch_types=[pltpu.VMEM((TOP_K, SC_CHUNK), I32), pltpu.VMEM((2, SC_CHUNK, dw), ys.dtype),
                       pltpu.SemaphoreType.DMA, pltpu.SemaphoreType.DMA],
    )(ys, slot_chunks)


def _experts_kernel(be_ref, nu_ref, bv_ref, run_ref, xs_hbm, wg_hbm, wu_hbm, wd_hbm, ys_ref,
                    xring, rsem, gring, uring, dring, wsem):
    i = pl.program_id(0)
    half = D_MODEL // 2
    n_used = nu_ref[0]
    n_steps = pl.num_programs(0)

    def weight_copies(blk):
        ex = be_ref[blk]
        slot = run_ref[blk] % EXPERT_RING
        return [pltpu.make_async_copy(src.at[ex], ring.at[slot], wsem.at[a, slot])
                for a, (src, ring) in enumerate(((wg_hbm, gring), (wu_hbm, uring), (wd_hbm, dring)))]

    def starts_run(blk):
        return run_ref[blk] != run_ref[jnp.maximum(blk - 1, 0)]

    @pl.when(i == 0)
    def _():
        for cp in weight_copies(jnp.int32(0)):
            cp.start()

        @pl.when((n_steps > 1) & starts_run(jnp.int32(1)))
        def _():
            for cp in weight_copies(jnp.int32(1)):
                cp.start()

    ahead = jnp.minimum(i + (EXPERT_RING - 1), n_steps - 1)

    @pl.when((i + (EXPERT_RING - 1) < n_steps) & starts_run(ahead))
    def _():
        for cp in weight_copies(ahead):
            cp.start()

    @pl.when((i == 0) | starts_run(i))
    def _():
        for cp in weight_copies(i):
            cp.wait()

    wslot = run_ref[i] % EXPERT_RING
    wg_ref, wu_ref, wd_ref = gring.at[wslot], uring.at[wslot], dring.at[wslot]

    def fetch(blk):
        slot = blk % EXPERT_RING
        return pltpu.make_async_copy(xs_hbm.at[pl.ds(pl.multiple_of(blk * EXPERT_BLOCK, EXPERT_BLOCK), EXPERT_BLOCK)],
                                     xring.at[slot], rsem.at[slot])

    @pl.when(i == 0)
    def _():
        for first in range(EXPERT_RING - 1):
            @pl.when(first < n_used)
            def _():
                fetch(jnp.int32(first)).start()

    @pl.when(i + (EXPERT_RING - 1) < n_used)
    def _():
        fetch(i + (EXPERT_RING - 1)).start()

    @pl.when(i < n_used)
    def _():
        fetch(i).wait()

    xs_ref = xring.at[i % EXPERT_RING]

    def ffn(rows):
        live = lax.broadcasted_iota(I32, (rows, xs_ref.shape[1]), 0) < bv_ref[i]
        xa, xb = _unpack_bf16_pair(jnp.where(live, xs_ref[0:rows, :], jnp.uint32(0)))
        xa, xb = xa.astype(BF16), xb.astype(BF16)
        g = _dot(xa, wg_ref[:half, :].astype(BF16)) + _dot(xb, wg_ref[half:, :].astype(BF16))
        u = _dot(xa, wu_ref[:half, :].astype(BF16)) + _dot(xb, wu_ref[half:, :].astype(BF16))
        act = (g * _sigmoid(g) * u).astype(BF16)
        y = _dot(act, wd_ref[...].astype(BF16))
        ys_ref[0:rows, :] = _pack_bf16_pair(y[:, :half], y[:, half:])

    used = i < n_used
    short = bv_ref[i] <= EXPERT_TAIL

    @pl.when(used & jnp.logical_not(short))
    def _():
        ffn(EXPERT_BLOCK)

    @pl.when(used & short)
    def _():
        ffn(EXPERT_TAIL)
        ys_ref[EXPERT_TAIL:, :] = jnp.zeros((EXPERT_BLOCK - EXPERT_TAIL, ys_ref.shape[1]), ys_ref.dtype)

    @pl.when(jnp.logical_not(used))
    def _():
        ys_ref[...] = jnp.zeros_like(ys_ref)


def _experts(xs, blk_e, n_used, blk_valid, w_gate, w_up, w_down):
    n_rows, dw = xs.shape
    d = w_gate.shape[1]
    nblk = n_rows // EXPERT_BLOCK
    blk_run = jnp.cumsum(jnp.concatenate([jnp.zeros((1,), I32), (blk_e[1:] != blk_e[:-1]).astype(I32)])).astype(I32)
    hbm = pl.BlockSpec(memory_space=pl.ANY)
    return pl.pallas_call(
        _experts_kernel,
        grid_spec=pltpu.PrefetchScalarGridSpec(
            num_scalar_prefetch=4,
            grid=(nblk,),
            in_specs=[hbm, hbm, hbm, hbm],
            out_specs=pl.BlockSpec((EXPERT_BLOCK, dw), lambda i, be, nu, bv, rn: (i, 0)),
            scratch_shapes=[pltpu.VMEM((EXPERT_RING, EXPERT_BLOCK, dw), xs.dtype),
                            pltpu.SemaphoreType.DMA((EXPERT_RING,)),
                            pltpu.VMEM((EXPERT_RING, d, EXPERT_FF), w_gate.dtype),
                            pltpu.VMEM((EXPERT_RING, d, EXPERT_FF), w_up.dtype),
                            pltpu.VMEM((EXPERT_RING, EXPERT_FF, d), w_down.dtype),
                            pltpu.SemaphoreType.DMA((3, EXPERT_RING))]),
        out_shape=jax.ShapeDtypeStruct((n_rows, dw), xs.dtype),
        compiler_params=pltpu.CompilerParams(dimension_semantics=("arbitrary",), vmem_limit_bytes=VMEM_LIMIT,
                                             has_side_effects=True),
        name="experts",
    )(blk_e, n_used, blk_valid, blk_run, xs, w_gate, w_up, w_down)


def _combine_kernel(x1_ref, h_ref, w_ref, gt_ref, sg_ref, su_ref, sd_ref, yg_ref, o_ref):
    tc = x1_ref.shape[0]
    half = D_MODEL // 2
    hb = h_ref[...].astype(BF16)
    g = _dot(hb, sg_ref[...])
    u = _dot(hb, su_ref[...])
    ffn = _dot((g * _sigmoid(g) * u).astype(BF16), sd_ref[...])

    w = w_ref[...]
    ra = jnp.zeros((tc, half), F32)
    rb = jnp.zeros((tc, half), F32)
    for k in range(TOP_K):
        ya, yb = _unpack_bf16_pair(yg_ref[k])
        ra = ra + w[:, k:k + 1] * ya
        rb = rb + w[:, k:k + 1] * yb
    ffn = ffn + jnp.concatenate([ra, rb], axis=1)
    o_ref[...] = x1_ref[...] + gt_ref[0] * ffn


def _combine(x1, h2, w_tok, gt2, yg, sg, su, sd, seq, tc):
    t, d = x1.shape
    row = lambda i: (i, 0)
    fixed = lambda i: (0, 0)
    return pl.pallas_call(
        _combine_kernel,
        grid=(t // tc,),
        in_specs=[pl.BlockSpec((tc, d), row),
                  pl.BlockSpec((tc, d), row),
                  pl.BlockSpec((tc, TOP_K), row),
                  pl.BlockSpec((1, 1, d), lambda i: ((i * tc) // seq, 0, 0)),
                  pl.BlockSpec((d, SHARED_FF), fixed),
                  pl.BlockSpec((d, SHARED_FF), fixed),
                  pl.BlockSpec((SHARED_FF, d), fixed),
                  pl.BlockSpec((TOP_K, tc, d // 2), lambda i: (0, i, 0))],
        out_specs=pl.BlockSpec((tc, d), row),
        out_shape=jax.ShapeDtypeStruct((t, d), F32),
        compiler_params=_cparams("parallel"),
        name="combine",
    )(x1, h2, w_tok, gt2, sg.astype(BF16), su.astype(BF16), sd.astype(BF16), yg)


def _moe_parts(x1, h2, h2p, gt2, router_w, router_bias, w_gate, w_up, w_down, sg, su, sd):
    b, s, d = x1.shape
    t = b * s
    h2 = h2.reshape(t, d)
    idx, w, rank, cnt = _router(h2, router_w, router_bias, min(256, t))
    counts = cnt[:, 0].astype(I32)
    padded = (counts + EXPERT_BLOCK - 1) // EXPERT_BLOCK * EXPERT_BLOCK
    pad_end = jnp.cumsum(padded)
    pad_start = pad_end - padded
    n_rows = t * TOP_K + N_EXPERTS * EXPERT_BLOCK
    nblk = n_rows // EXPERT_BLOCK
    n_used = (pad_end[-1:] // EXPERT_BLOCK).astype(I32)
    blk_start = jnp.arange(nblk, dtype=I32) * EXPERT_BLOCK
    owns = (pad_start[None, :] <= blk_start[:, None]) & (blk_start[:, None] < pad_end[None, :])
    e_ids = jnp.arange(N_EXPERTS, dtype=I32)[None, :]
    last_e = jnp.max(jnp.where(counts > 0, e_ids[0], 0))
    blk_e = jnp.where(blk_start < pad_end[-1], jnp.sum(jnp.where(owns, e_ids, 0), axis=1), last_e).astype(I32)
    rows_left = jnp.sum(jnp.where(owns, (pad_start + counts)[None, :] - blk_start[:, None], 0), axis=1)
    blk_valid = jnp.clip(rows_left, 0, EXPERT_BLOCK).astype(I32)
    slot = _slots(pad_start.astype(I32), idx, rank, min(2048, t))
    slot_chunks = slot.reshape(TOP_K, t // SC_CHUNK, SC_CHUNK).transpose(1, 0, 2)
    xs = _sc_dispatch(h2p.reshape(t, d // 2), slot_chunks, n_rows)
    ys = _experts(xs, blk_e, n_used, blk_valid, w_gate, w_up, w_down)
    yg = _sc_gather(ys, slot_chunks, t)
    out = _combine(x1.reshape(t, d), h2, w.T, gt2, yg, sg, su, sd, s, min(256, t))
    return out.reshape(b, s, d), dict(idx=idx, w=w, rank=rank, cnt=cnt)


def kernel(x, c, ada_w, ada_b, norm1_w, norm2_w, w_in, q_norm_w, k_norm_w, cmp_pos, cmp_w1, cmp_b1, cmp_w2, attn_out_norm_w, hgrn_lb_param, rec_out_norm_w, w_out, router_w, router_bias, exp_w_gate, exp_w_up, exp_w_down, shared_w_gate, shared_w_up, shared_w_down):
    assert ada_w.shape[0] == 1, "one layer"
    assert x.shape[0] <= 8 and x.shape[1] % TK == 0 and x.shape[1] >= WINDOW + TQ
    l = 0
    x1, h2, h2p, gt2 = _mixer(x, c, ada_w[l], ada_b[l], norm1_w[l], norm2_w[l], w_in[l], q_norm_w[l], k_norm_w[l],
                         cmp_pos[l], cmp_w1[l], cmp_b1[l], cmp_w2[l], attn_out_norm_w[l], hgrn_lb_param,
                         rec_out_norm_w[l], w_out[l])
    out, _ = _moe_parts(x1, h2, h2p, gt2, router_w[l], router_bias[l], exp_w_gate[l], exp_w_up[l], exp_w_down[l],
                        shared_w_gate[l], shared_w_up[l], shared_w_down[l])
    return out
```

```python
import functools

import numpy as np
import jax
import jax.numpy as jnp
from jax import lax
from jax.experimental import pallas as pl
from jax.experimental.pallas import tpu as pltpu
from jax.experimental.pallas import tpu_sc as plsc

F32 = jnp.float32
BF16 = jnp.bfloat16
I32 = jnp.int32

D_MODEL = 1024
NSA_HEADS = 8
HEAD_DIM = 64
NSA_WIDTH = NSA_HEADS * HEAD_DIM
KV_HEADS = 2
HEADS_PER_KV = NSA_HEADS // KV_HEADS
KV_WIDTH = KV_HEADS * HEAD_DIM
CMP_BLOCK = 32
CMP_STRIDE = 16
CMP_HIDDEN = 256
SEL_BLOCK = 64
N_SELECT = 16
WINDOW = 512
HGRN_HEADS = 4
HGRN_DIM = 128
HGRN_WIDTH = HGRN_HEADS * HGRN_DIM
HGRN_CHUNK = 64
HGRN_SUB = 16
HGRN_LEVELS = ()
HGRN_LEAF = 16
N_EXPERTS = 256
TOP_K = 8
N_GROUPS = 8
GROUP_SIZE = N_EXPERTS // N_GROUPS
TOPK_GROUPS = 4
EXPERT_FF = 256
SHARED_FF = 256
ROUTED_SCALE = 2.5
RMS_EPS = 1e-6
BIG = 1e9
LOG2E = 1.4426950408889634
GATE_PAD = 128
PROJ_COLS = NSA_WIDTH + 6 * KV_WIDTH + GATE_PAD + 4 * HGRN_WIDTH

VMEM_LIMIT = 56 * 1024 * 1024

TQ = 256
TK = 512
EXPERT_BLOCK = 512
EXPERT_TAIL = 128
EXPERT_RING = 3
HIGHEST = lax.Precision.HIGHEST


def _cparams(*sem):
    return pltpu.CompilerParams(dimension_semantics=sem, vmem_limit_bytes=VMEM_LIMIT)


def _sigmoid(x):
    return 1.0 / (1.0 + jnp.exp(-x))


def _dot_nt(a, b):
    return lax.dot_general(a, b, (((1,), (1,)), ((), ())), preferred_element_type=F32)


def _dot(a, b, **kw):
    return jnp.dot(a, b, preferred_element_type=F32, **kw)


def _split_dot(a_bf16_exact, x):
    hi = x.astype(BF16)
    lo = (x - hi.astype(F32)).astype(BF16)
    return _dot(a_bf16_exact, hi) + _dot(a_bf16_exact, lo)


def _mod_kernel(c_ref, w_ref, b_ref, o_ref):
    c = c_ref[...]
    cond = c * _sigmoid(c)
    o_ref[...] = _dot(cond, w_ref[...], precision=HIGHEST) + b_ref[...]


def _mod(c, ada_w, ada_b):
    b, d = c.shape
    rows = 8
    c_pad = jnp.zeros((rows, d), F32).at[:b].set(c)
    n = ada_w.shape[1]
    out = pl.pallas_call(
        _mod_kernel,
        grid=(n // d,),
        in_specs=[pl.BlockSpec((rows, d), lambda j: (0, 0)),
                  pl.BlockSpec((d, d), lambda j: (0, j)),
                  pl.BlockSpec((1, d), lambda j: (0, j))],
        out_specs=pl.BlockSpec((rows, d), lambda j: (0, j)),
        out_shape=jax.ShapeDtypeStruct((rows, n), F32),
        compiler_params=_cparams("parallel"),
        name="mod",
    )(c_pad, ada_w, ada_b.reshape(1, n))
    return out[:b]


def _head_rms(t, w):
    return t * lax.rsqrt(jnp.mean(t * t, axis=-1, keepdims=True) + RMS_EPS) * w


def _pos_digits(pos):
    lane = lax.broadcasted_iota(I32, pos.shape, 1)
    d0 = (lane == 0) | (lane == 3) | (lane == 6)
    d1 = (lane == 1) | (lane == 4) | (lane == 7)
    d2 = (lane == 2) | (lane == 5) | (lane == 8)
    dig = jnp.where(d0, pos >> 12, jnp.where(d1, (pos >> 6) & 63, jnp.where(d2, pos & 63, 0)))
    return dig.astype(F32)


def _inproj_kernel(x_ref, sc_ref, sh_ref, n1_ref, w_ref, qnw_ref, knw_ref, lbp_ref, qaug_ref,
                   q_ref, kcr_ref, vcr_ref, ks_ref, vst_ref, kw_ref, vwt_ref, gt_ref,
                   hq_ref, hk_ref, hlf_ref, hv_ref, hg_ref):
    x = x_ref[0]
    ms = jnp.mean(x * x, axis=-1, keepdims=True)
    h = x * lax.rsqrt(ms + RMS_EPS) * n1_ref[...] * (1.0 + sc_ref[0]) + sh_ref[0]
    p = _dot(h.astype(BF16), w_ref[...])
    tm = x.shape[0]

    qnw = qnw_ref[...]
    for hd in range(NSA_HEADS):
        t = p[:, hd * HEAD_DIM:(hd + 1) * HEAD_DIM]
        qn = _head_rms(t, qnw) * (HEAD_DIM ** -0.5 * LOG2E)
        qa = jnp.broadcast_to(qaug_ref[hd:hd + 1, :], (tm, HEAD_DIM))
        q_ref[0, hd] = jnp.concatenate([qn, qa], axis=1).astype(BF16)
    kaug = _pos_digits(pl.program_id(1) * tm + lax.broadcasted_iota(I32, (tm, HEAD_DIM), 0))

    o = NSA_WIDTH
    kcr_ref[0] = p[:, o:o + KV_WIDTH]
    vcr_ref[0] = p[:, o + KV_WIDTH:o + 2 * KV_WIDTH]
    ks = p[:, o + 2 * KV_WIDTH:o + 3 * KV_WIDTH]
    vs = p[:, o + 3 * KV_WIDTH:o + 4 * KV_WIDTH]
    kw = p[:, o + 4 * KV_WIDTH:o + 5 * KV_WIDTH]
    vw = p[:, o + 5 * KV_WIDTH:o + 6 * KV_WIDTH]
    for g in range(KV_HEADS):
        sl = slice(g * HEAD_DIM, (g + 1) * HEAD_DIM)
        ks_ref[0, g] = jnp.concatenate([_head_rms(ks[:, sl], knw_ref[1:2, :]), kaug], axis=1).astype(BF16)
        kw_ref[0, g] = jnp.concatenate([_head_rms(kw[:, sl], knw_ref[2:3, :]), kaug], axis=1).astype(BF16)
    vst = vs.T.astype(BF16)
    vwt = vw.T.astype(BF16)
    for g in range(KV_HEADS):
        vst_ref[0, g] = vst[g * HEAD_DIM:(g + 1) * HEAD_DIM, :]
        vwt_ref[0, g] = vwt[g * HEAD_DIM:(g + 1) * HEAD_DIM, :]

    o = NSA_WIDTH + 6 * KV_WIDTH
    gates = _sigmoid(p[:, o:o + GATE_PAD])
    gt_ref[0] = gates.T[:NSA_HEADS * 3, :]

    o = o + GATE_PAD
    hq = p[:, o:o + HGRN_WIDTH]
    hf = p[:, o + HGRN_WIDTH:o + 2 * HGRN_WIDTH]
    hi = p[:, o + 2 * HGRN_WIDTH:o + 3 * HGRN_WIDTH]
    hg = p[:, o + 3 * HGRN_WIDTH:o + 4 * HGRN_WIDTH]
    lbp = lbp_ref[...]
    e = jnp.exp(lbp - jnp.max(lbp, axis=0, keepdims=True))
    lb = e[0:1, :] / jnp.sum(e, axis=0, keepdims=True)
    f = lb + (1.0 - lb) * _sigmoid(hf)
    hq_ref[0] = hq * _sigmoid(hq) * (HGRN_DIM ** -0.5)
    hk_ref[0] = 1.0 - f
    hlf_ref[0] = jnp.log(f)
    hv_ref[0] = hi
    hg_ref[0] = _sigmoid(hg)


def _inproj(x, sc1, sh1, norm1_w, w_cat, q_norm_w, k_norm_w, lb_param, tm):
    b, s, d = x.shape
    row = lambda bi, i: (bi, i, 0)
    per_b = lambda bi, i: (bi, 0, 0)
    fixed2 = lambda bi, i: (0, 0)
    aw = 2 * HEAD_DIM
    rest = np.array([2.0 ** (-8.0 * (i + 1) / NSA_HEADS) for i in range(NSA_HEADS)], np.float64) * LOG2E
    qaug = np.zeros((NSA_HEADS, HEAD_DIM), np.float32)
    for i in range(3):
        term = rest.astype(np.float32).astype(BF16).astype(np.float64)
        rest = rest - term
        for dgt, wgt in enumerate((4096.0, 64.0, 1.0)):
            qaug[:, 3 * i + dgt] = term * wgt
    assert np.all(qaug == qaug.astype(BF16).astype(np.float32))
    out_shape = (
        jax.ShapeDtypeStruct((b, NSA_HEADS, s, aw), BF16),
        jax.ShapeDtypeStruct((b, s, KV_WIDTH), F32),
        jax.ShapeDtypeStruct((b, s, KV_WIDTH), F32),
        jax.ShapeDtypeStruct((b, KV_HEADS, s, aw), BF16),
        jax.ShapeDtypeStruct((b, KV_HEADS, HEAD_DIM, s), BF16),
        jax.ShapeDtypeStruct((b, KV_HEADS, s, aw), BF16),
        jax.ShapeDtypeStruct((b, KV_HEADS, HEAD_DIM, s), BF16),
        jax.ShapeDtypeStruct((b, NSA_HEADS * 3, s), F32),
    ) + tuple(jax.ShapeDtypeStruct((b, s, HGRN_WIDTH), F32) for _ in range(5))
    hm = lambda n, w: pl.BlockSpec((1, n, tm, w), lambda bi, i: (bi, 0, i, 0))
    hmt = lambda n, w: pl.BlockSpec((1, n, w, tm), lambda bi, i: (bi, 0, 0, i))
    out_specs = (
        hm(NSA_HEADS, aw),
        pl.BlockSpec((1, tm, KV_WIDTH), row),
        pl.BlockSpec((1, tm, KV_WIDTH), row),
        hm(KV_HEADS, aw), hmt(KV_HEADS, HEAD_DIM),
        hm(KV_HEADS, aw), hmt(KV_HEADS, HEAD_DIM),
        pl.BlockSpec((1, NSA_HEADS * 3, tm), lambda bi, i: (bi, 0, i)),
    ) + tuple(pl.BlockSpec((1, tm, HGRN_WIDTH), row) for _ in range(5))
    return pl.pallas_call(
        _inproj_kernel,
        grid=(b, s // tm),
        in_specs=[pl.BlockSpec((1, tm, d), row),
                  pl.BlockSpec((1, 1, d), per_b),
                  pl.BlockSpec((1, 1, d), per_b),
                  pl.BlockSpec((1, d), fixed2),
                  pl.BlockSpec((d, PROJ_COLS), fixed2),
                  pl.BlockSpec((1, HEAD_DIM), fixed2),
                  pl.BlockSpec((3, HEAD_DIM), fixed2),
                  pl.BlockSpec(lb_param.shape, fixed2),
                  pl.BlockSpec((NSA_HEADS, HEAD_DIM), fixed2)],
        out_specs=out_specs,
        out_shape=out_shape,
        compiler_params=_cparams("parallel", "parallel"),
        name="inproj",
    )(x, sc1, sh1, norm1_w, w_cat, q_norm_w, k_norm_w, lb_param, jnp.asarray(qaug))


def _gelu_tanh(x):
    return 0.5 * x * (1.0 + jnp.tanh(0.7978845608028654 * (x + 0.044715 * x * x * x)))


def _compress_kernel(kch_ref, vch_ref, pos_ref, wa_ref, wb_ref, b1_ref, w2_ref, knw_ref,
                     kc_ref, vct_ref):
    n = kch_ref.shape[1]
    outs = []
    for br, ch_ref in enumerate((kch_ref, vch_ref)):
        ch = ch_ref[0]
        a = _dot((ch + pos_ref[br, 0:1, :]).astype(BF16), wa_ref[br])
        bm = _dot((ch + pos_ref[br, 1:2, :]).astype(BF16), wb_ref[br])
        pre = a + pltpu.roll(bm, n - 1, 0) + b1_ref[br]
        hid = _gelu_tanh(pre).astype(BF16)
        outs.append([_dot(hid[:, g * CMP_HIDDEN:(g + 1) * CMP_HIDDEN], w2_ref[br]) for g in range(KV_HEADS)])
    end_digits = _pos_digits(lax.broadcasted_iota(I32, (n, HEAD_DIM), 0) * CMP_STRIDE + (CMP_BLOCK - 1))
    for g in range(KV_HEADS):
        kc_ref[0, g] = jnp.concatenate([_head_rms(outs[0][g], knw_ref[0:1, :]), end_digits], axis=1).astype(BF16)
    vct = jnp.concatenate(outs[1], axis=1).T.astype(BF16)
    for g in range(KV_HEADS):
        vct_ref[0, g] = vct[g * HEAD_DIM:(g + 1) * HEAD_DIM, :]


def _compress(kc_raw, vc_raw, cmp_pos, cmp_w1, cmp_b1, cmp_w2, k_norm_w):
    b, s, _ = kc_raw.shape
    n = s // CMP_STRIDE
    half = CMP_STRIDE
    cw = CMP_STRIDE * KV_WIDTH
    kch = kc_raw.reshape(b, n, cw)
    vch = vc_raw.reshape(b, n, cw)
    pos = cmp_pos.reshape(2, 2, half, 1, HEAD_DIM)
    pos = jnp.broadcast_to(pos, (2, 2, half, KV_HEADS, HEAD_DIM)).reshape(2, 2, cw)
    w1 = cmp_w1.reshape(2, 2, half, HEAD_DIM, CMP_HIDDEN)
    eye = jnp.eye(KV_HEADS, dtype=F32)
    wfull = jnp.einsum('rhjdn,gk->rhjgdkn', w1, eye).reshape(2, 2, cw, KV_HEADS * CMP_HIDDEN).astype(BF16)
    b1 = jnp.tile(cmp_b1.reshape(2, 1, CMP_HIDDEN), (1, 1, KV_HEADS))
    fix = lambda r: (lambda bi: (0,) * r)
    return pl.pallas_call(
        _compress_kernel,
        grid=(b,),
        in_specs=[pl.BlockSpec((1, n, cw), lambda bi: (bi, 0, 0)),
                  pl.BlockSpec((1, n, cw), lambda bi: (bi, 0, 0)),
                  pl.BlockSpec((2, 2, cw), fix(3)),
                  pl.BlockSpec((2, cw, KV_HEADS * CMP_HIDDEN), fix(3)),
                  pl.BlockSpec((2, cw, KV_HEADS * CMP_HIDDEN), fix(3)),
                  pl.BlockSpec((2, 1, KV_HEADS * CMP_HIDDEN), fix(3)),
                  pl.BlockSpec((2, CMP_HIDDEN, HEAD_DIM), fix(3)),
                  pl.BlockSpec((3, HEAD_DIM), fix(2))],
        out_specs=(pl.BlockSpec((1, KV_HEADS, n, 2 * HEAD_DIM), lambda bi: (bi, 0, 0, 0)),
                   pl.BlockSpec((1, KV_HEADS, HEAD_DIM, n), lambda bi: (bi, 0, 0, 0))),
        out_shape=(jax.ShapeDtypeStruct((b, KV_HEADS, n, 2 * HEAD_DIM), BF16),
                   jax.ShapeDtypeStruct((b, KV_HEADS, HEAD_DIM, n), BF16)),
        compiler_params=_cparams("parallel"),
        name="compress",
    )(kch, vch, pos, wfull[:, 0], wfull[:, 1], b1, cmp_w2.astype(BF16), k_norm_w)


def _nsa_kernel(q_ref, kc_ref, vct_ref, ks_ref, vst_ref, kw_ref, vwt_ref, gt_ref, cdiff_ref, wdiff_ref,
                ovl_ref, oh_ref, onw_ref, wmask_ref, o_ref, buf_a, buf_b, m_scr, acc_scr, oc_scr, bias_scr,
                lst, cnt, *, n_top):
    q0 = pl.program_id(2) * TQ
    ncols = HEADS_PER_KV * TQ
    q = q_ref[0].reshape(ncols, 2 * HEAD_DIM)
    ns = ovl_ref.shape[0]

    def compress_and_select(nk, nb):
        s = jnp.where(cdiff_ref[0:nk, :] <= q0, _dot_nt(kc_ref[0, 0, 0:nk, :], q), -jnp.inf)
        m = jnp.max(s, axis=0, keepdims=True)
        m = jnp.where(m == -jnp.inf, 0.0, m)
        e = jnp.exp2(s - m)
        p = e / jnp.maximum(jnp.sum(e, axis=0, keepdims=True), 1e-30)
        oc_scr[...] = _dot(vct_ref[0, 0, :, 0:nk], p.astype(BF16))

        psum = p[:, 0:TQ]
        for hh in range(1, HEADS_PER_KV):
            psum = psum + p[:, hh * TQ:(hh + 1) * TQ]
        imp = _split_dot(ovl_ref[0:nb, 0:nk], psum)
        blk = lax.broadcasted_iota(I32, (nb, TQ), 0)
        tq = q0 + lax.broadcasted_iota(I32, (nb, TQ), 1)
        cur = tq >> 6
        forced = (blk == 0) | (blk == cur) | (blk == cur - 1)
        rank = jnp.where(forced, BIG, jnp.where(blk * SEL_BLOCK <= tq, imp, -BIG))
        blkf = blk.astype(F32)

        bias = jnp.full((nb, TQ), -1e30, F32)
        for _ in range(min(n_top, nb)):
            mx = jnp.max(rank, axis=0, keepdims=True)
            first = jnp.min(jnp.where(rank == mx, blkf, float(nb)), axis=0, keepdims=True)
            hit = blkf == first
            rank = jnp.where(hit, -jnp.inf, rank)
            bias = jnp.where(hit, 0.0, bias)
        bias_scr[...] = jnp.full((128, TQ), -1e30, F32)
        bias_scr[0:nb, :] = jnp.where(blk == 0, -1e30, bias)

    nc = kc_ref.shape[2]
    early = q0 + TQ <= ks_ref.shape[2] // 2

    @pl.when(early)
    def _():
        compress_and_select(nc // 2, ns // 2)

    @pl.when(jnp.logical_not(early))
    def _():
        compress_and_select(nc, ns)

    o_c = oc_scr[...]
    bias = bias_scr[...]

    bias_t = bias.T.astype(BF16)
    qq = jnp.concatenate([q, jnp.concatenate([bias_t] * HEADS_PER_KV, axis=0)], axis=1)
    ones_rows = jnp.ones((16, TK), BF16)

    def scores(j):
        k0 = pl.multiple_of(j * TK, TK)
        kk = jnp.concatenate([ks_ref[0, 0, pl.ds(k0, TK), :], oh_ref[pl.ds(k0, TK), :]], axis=1)
        return _dot_nt(kk, qq)

    def consume(buf, j, causal, part):
        sc = buf[...]
        if causal:
            sc = jnp.where(wdiff_ref[0:TK, :] + (q0 - j * TK) >= 0, sc, -1e30)
        k0 = pl.multiple_of(j * TK, TK)
        m_run = m_scr[part]
        m_new = jnp.maximum(m_run, jnp.max(sc, axis=0, keepdims=True))
        ex = jnp.exp2(sc - m_new).astype(BF16)
        va = jnp.concatenate([vst_ref[0, 0, :, pl.ds(k0, TK)], ones_rows], axis=0)
        acc_scr[part] = jnp.exp2(m_run - m_new) * acc_scr[part] + _dot(va, ex)
        m_scr[part] = m_new

    n_past = q0 // TK
    blocks_per_tile = TK // SEL_BLOCK
    cnt[0] = 0
    for j in range(ks_ref.shape[2] // TK):
        wanted = jnp.max(bias[j * blocks_per_tile:(j + 1) * blocks_per_tile, :]) == 0.0

        @pl.when(wanted & (j < n_past))
        def _():
            lst[cnt[0]] = j
            cnt[0] = cnt[0] + 1

    n_sel = cnt[0]
    lst[n_sel] = n_past

    buf_a[...] = scores(lst[0])

    s0 = jnp.where(wdiff_ref[0:SEL_BLOCK, :] + q0 >= 0, _dot_nt(ks_ref[0, 0, 0:SEL_BLOCK, :], q), -1e30)
    m0 = jnp.max(s0, axis=0, keepdims=True)
    v0 = jnp.concatenate([vst_ref[0, 0, :, 0:SEL_BLOCK], jnp.ones((16, SEL_BLOCK), BF16)], axis=0)
    m_scr[0] = m0
    acc_scr[0] = _dot(v0, jnp.exp2(s0 - m0).astype(BF16))
    m_scr[1] = jnp.full((1, ncols), -1e30, F32)
    acc_scr[1] = jnp.zeros((HEAD_DIM + 16, ncols), F32)

    nw = WINDOW + TQ
    start = pl.multiple_of(jnp.maximum(q0 - WINDOW, 0), TQ)
    sw = _dot_nt(kw_ref[0, 0, pl.ds(start, nw), :], q) + wmask_ref[0]
    ew = jnp.exp2(sw - jnp.max(sw, axis=0, keepdims=True))
    vw_aug = jnp.concatenate([vwt_ref[0, 0, :, pl.ds(start, nw)], jnp.ones((16, nw), BF16)], axis=0)
    acc_w = _dot(vw_aug, ew.astype(BF16))
    o_w = acc_w[0:HEAD_DIM, :] / acc_w[HEAD_DIM:HEAD_DIM + 1, :]

    def tiles(first, count):
        for u in range(0, count, 2):
            buf_b[...] = scores(lst[first + u + 1])
            consume(buf_a, lst[first + u], False, 0)
            buf_a[...] = scores(lst[first + u + 2])
            consume(buf_b, lst[first + u + 1], False, 1)
        return 0

    lax.fori_loop(0, n_sel // 4, lambda i, _: tiles(4 * i, 4), 0)
    lax.fori_loop(0, (n_sel // 2) % 2, lambda i, _: tiles((n_sel // 4) * 4, 2), 0)

    @pl.when(n_sel % 2 == 1)
    def _():
        buf_b[...] = scores(n_past)
        consume(buf_a, lst[n_sel - 1], False, 0)
        consume(buf_b, n_past, True, 1)

    @pl.when(n_sel % 2 == 0)
    def _():
        consume(buf_a, n_past, True, 0)

    m_all = jnp.maximum(m_scr[0], m_scr[1])
    acc_s = jnp.exp2(m_scr[0] - m_all) * acc_scr[0] + jnp.exp2(m_scr[1] - m_all) * acc_scr[1]
    o_s = acc_s[0:HEAD_DIM, :] / acc_s[HEAD_DIM:HEAD_DIM + 1, :]

    gt = gt_ref[0, 0]
    outs = []
    for hh in range(HEADS_PER_KV):
        cs = slice(hh * TQ, (hh + 1) * TQ)
        o = (gt[3 * hh:3 * hh + 1, :] * o_c[:, cs] + gt[3 * hh + 1:3 * hh + 2, :] * o_s[:, cs]
             + gt[3 * hh + 2:3 * hh + 3, :] * o_w[:, cs])
        o = o * lax.rsqrt(jnp.mean(o * o, axis=0, keepdims=True) + RMS_EPS) * onw_ref[0, hh]
        outs.append(o)
    o_ref[0] = jnp.concatenate(outs, axis=0).T


def _nsa(q, kc, vct, ks, vst, kw, vwt, gates_t, attn_out_norm_w):
    b, _, s, aw = q.shape
    nc = kc.shape[2]
    ns = s // SEL_BLOCK
    n_top = min(N_SELECT, ns)
    ncols = HEADS_PER_KV * TQ
    nw = WINDOW + TQ
    tl = np.arange(ncols)[None, :] & (TQ - 1)
    cdiff = jnp.asarray((np.arange(nc)[:, None] * CMP_STRIDE + (CMP_BLOCK - 1) - tl).astype(np.int32))
    wdiff_np = (tl - np.arange(nw)[:, None]).astype(np.int32)
    wdiff = jnp.asarray(wdiff_np)
    n_off = WINDOW // TQ + 1
    dist_np = wdiff_np[None] + (np.arange(n_off) * TQ)[:, None, None]
    wmask = jnp.asarray(np.where((dist_np >= 0) & (dist_np < WINDOW), 0.0, -np.inf).astype(np.float32))
    ci = np.arange(nc)[None, :] * CMP_STRIDE
    bj = np.arange(ns)[:, None]
    ovl = ((ci < (bj + 1) * SEL_BLOCK) & (ci + CMP_BLOCK > bj * SEL_BLOCK) & (np.arange(nc)[None, :] < nc - 1))
    ovl = jnp.asarray(ovl.astype(np.float32)).astype(BF16)
    assert ns <= 128
    onehot = (np.arange(s)[:, None] // SEL_BLOCK == np.arange(128)[None, :])
    onehot = jnp.asarray(onehot.astype(np.float32)).astype(BF16)
    onw = jnp.broadcast_to(attn_out_norm_w.reshape(KV_HEADS, HEADS_PER_KV, HEAD_DIM, 1),
                           (KV_HEADS, HEADS_PER_KV, HEAD_DIM, TQ))
    gt = gates_t.reshape(b, KV_HEADS, HEADS_PER_KV * 3, s)
    per_bg = lambda bi, g, i: (bi, g, 0, 0)
    fixed = lambda bi, g, i: (0, 0)
    return pl.pallas_call(
        functools.partial(_nsa_kernel, n_top=n_top),
        grid=(b, KV_HEADS, s // TQ),
        in_specs=[pl.BlockSpec((1, HEADS_PER_KV, TQ, aw), lambda bi, g, i: (bi, g, i, 0)),
                  pl.BlockSpec((1, 1, nc, aw), per_bg),
                  pl.BlockSpec((1, 1, HEAD_DIM, nc), per_bg),
                  pl.BlockSpec((1, 1, s, aw), per_bg),
                  pl.BlockSpec((1, 1, HEAD_DIM, s), per_bg),
                  pl.BlockSpec((1, 1, s, aw), per_bg),
                  pl.BlockSpec((1, 1, HEAD_DIM, s), per_bg),
                  pl.BlockSpec((1, 1, HEADS_PER_KV * 3, TQ), lambda bi, g, i: (bi, g, 0, i)),
                  pl.BlockSpec((nc, ncols), fixed, pipeline_mode=pl.Buffered(1)),
                  pl.BlockSpec((nw, ncols), fixed, pipeline_mode=pl.Buffered(1)),
                  pl.BlockSpec((ns, nc), fixed, pipeline_mode=pl.Buffered(1)),
                  pl.BlockSpec((s, 128), fixed, pipeline_mode=pl.Buffered(1)),
                  pl.BlockSpec((1, HEADS_PER_KV, HEAD_DIM, TQ), lambda bi, g, i: (g, 0, 0, 0)),
                  pl.BlockSpec((1, nw, ncols), lambda bi, g, i: (jnp.minimum(i, n_off - 1), 0, 0))],
        out_specs=pl.BlockSpec((1, TQ, HEADS_PER_KV * HEAD_DIM), lambda bi, g, i: (bi, i, g)),
        out_shape=jax.ShapeDtypeStruct((b, s, NSA_WIDTH), F32),
        scratch_shapes=[pltpu.VMEM((TK, ncols), F32), pltpu.VMEM((TK, ncols), F32),
                        pltpu.VMEM((2, 1, ncols), F32), pltpu.VMEM((2, HEAD_DIM + 16, ncols), F32),
                        pltpu.VMEM((HEAD_DIM, ncols), F32), pltpu.VMEM((128, TQ), F32),
                        pltpu.SMEM((s // TK + 1,), I32), pltpu.SMEM((1,), I32)],
        compiler_params=_cparams("parallel", "parallel", "arbitrary"),
        name="nsa",
    )(q, kc, vct, ks, vst, kw, vwt, gt, cdiff, wdiff, ovl, onehot, onw, wmask)


def _hgrn_cum_matrix():
    c = HGRN_CHUNK
    t = np.arange(c)
    mats = [(t[None, :] <= t[:, None])]
    for half in HGRN_LEVELS:
        ref = (t & ~(2 * half - 1)) + half - 1
        mats.append(t[None, :] <= ref[:, None])
    return np.concatenate(mats, axis=0).astype(np.float32)


def _hgrn_kernel(q_ref, k_ref, lf_ref, v_ref, g_ref, onw_ref, cm_ref, o_ref, state_scr, *, n_chunks):
    c = HGRN_CHUNK

    @pl.when(pl.program_id(1) == 0)
    def _():
        state_scr[...] = jnp.zeros_like(state_scr)

    ri = lax.broadcasted_iota(I32, (c, c), 0)
    ci = lax.broadcasted_iota(I32, (c, c), 1)
    rsub = ri // HGRN_SUB
    level_masks = [((ri & ~(2 * h - 1)) == (ci & ~(2 * h - 1))) & ((ri & h) != 0) & ((ci & h) == 0)
                   for h in HGRN_LEVELS]
    diag = ri == ci

    def head_chunk(r0, hd, state_t):
        cols = slice(hd * HGRN_DIM, (hd + 1) * HGRN_DIM)
        q = q_ref[0, pl.ds(r0, c), cols]
        k = k_ref[0, pl.ds(r0, c), cols]
        lf = lf_ref[0, pl.ds(r0, c), cols] * LOG2E
        v = v_ref[0, pl.ds(r0, c), cols]
        cm = cm_ref[...]
        l1 = lf.astype(BF16)
        rest = lf - l1.astype(F32)
        l2 = rest.astype(BF16)
        l3 = (rest - l2.astype(F32)).astype(BF16)
        cums = _dot(cm, l1) + _dot(cm, l2) + _dot(cm, l3)
        cum = cums[0:c]
        o = _dot_nt((q * jnp.exp2(cum)).astype(BF16), state_t.astype(BF16))
        scores = jnp.where(diag, jnp.sum(q * k, axis=-1, keepdims=True), 0.0)

        def factored(ref, mask, acc):
            qs = q * jnp.exp2(jnp.minimum(cum - ref, 0.0))
            kd = k * jnp.exp2(jnp.minimum(ref - cum, 0.0))
            return jnp.where(mask, _dot_nt(qs.astype(BF16), kd.astype(BF16)), acc)

        for i in range(1, c // HGRN_SUB):
            scores = factored(cum[i * HGRN_SUB - 1:i * HGRN_SUB, :], (rsub == i) & (ci < i * HGRN_SUB), scores)
        for lv in range(len(HGRN_LEVELS)):
            scores = factored(cums[(lv + 1) * c:(lv + 2) * c], level_masks[lv], scores)
        for d in range(1, HGRN_LEAF):
            ksh = pltpu.roll(k, d, 0)
            csh = pltpu.roll(cum, d, 0)
            w = jnp.sum(q * ksh * jnp.exp2(cum - csh), axis=-1, keepdims=True)
            scores = jnp.where((ri - ci == d) & ((ri & (HGRN_LEAF - 1)) >= d), w, scores)
        o = o + _dot(scores.astype(BF16), v.astype(BF16))
        last = cum[c - 1:c, :]
        kd = (k * jnp.exp2(last - cum)).astype(BF16)
        state_t = state_t * jnp.exp2(last) + _dot(v.T.astype(BF16), kd)
        o = o * g_ref[0, pl.ds(r0, c), cols]
        o = o * lax.rsqrt(jnp.mean(o * o, axis=-1, keepdims=True) + RMS_EPS) * onw_ref[:, cols]
        o_ref[0, pl.ds(r0, c), cols] = o
        return state_t

    def chunk(ck, states):
        r0 = pl.multiple_of(ck * c, c)
        return tuple(head_chunk(r0, hd, states[hd]) for hd in range(HGRN_HEADS))

    states = lax.fori_loop(0, n_chunks, chunk, tuple(state_scr[hd] for hd in range(HGRN_HEADS)))
    for hd in range(HGRN_HEADS):
        state_scr[hd] = states[hd]


def _hgrn(hq, hk, hlf, hv, hg, rec_out_norm_w, rows):
    b, s, _ = hq.shape
    cm = jnp.asarray(_hgrn_cum_matrix()).astype(BF16)
    blk = pl.BlockSpec((1, rows, HGRN_WIDTH), lambda bi, i: (bi, i, 0))
    return pl.pallas_call(
        functools.partial(_hgrn_kernel, n_chunks=rows // HGRN_CHUNK),
        grid=(b, s // rows),
        in_specs=[blk, blk, blk, blk, blk,
                  pl.BlockSpec((1, HGRN_WIDTH), lambda bi, i: (0, 0)),
                  pl.BlockSpec(cm.shape, lambda bi, i: (0, 0))],
        out_specs=blk,
        out_shape=jax.ShapeDtypeStruct((b, s, HGRN_WIDTH), F32),
        scratch_shapes=[pltpu.VMEM((HGRN_HEADS, HGRN_DIM, HGRN_DIM), F32)],
        compiler_params=_cparams("parallel", "arbitrary"),
        name="hgrn",
    )(hq, hk, hlf, hv, hg, rec_out_norm_w.reshape(1, HGRN_WIDTH), cm)


def _outproj_kernel(x_ref, a_ref, r_ref, wa_ref, wr_ref, gt_ref, sc_ref, sh_ref, n2_ref, x1_ref, h2_ref, h2p_ref):
    mixed = _dot(a_ref[0].astype(BF16), wa_ref[...]) + _dot(r_ref[0].astype(BF16), wr_ref[...])
    x1 = x_ref[0] + gt_ref[0] * mixed
    x1_ref[0] = x1
    ms = jnp.mean(x1 * x1, axis=-1, keepdims=True)
    h2 = x1 * lax.rsqrt(ms + RMS_EPS) * n2_ref[...] * (1.0 + sc_ref[0]) + sh_ref[0]
    h2_ref[0] = h2
    h2p_ref[0] = _pack_bf16_pair(h2[:, :D_MODEL // 2], h2[:, D_MODEL // 2:])


def _outproj(x, attn, rec, w_out, gt1, sc2, sh2, norm2_w, tm):
    b, s, d = x.shape
    row = lambda bi, i: (bi, i, 0)
    per_b = lambda bi, i: (bi, 0, 0)
    fixed2 = lambda bi, i: (0, 0)
    w = w_out.astype(BF16)
    return pl.pallas_call(
        _outproj_kernel,
        grid=(b, s // tm),
        in_specs=[pl.BlockSpec((1, tm, d), row),
                  pl.BlockSpec((1, tm, NSA_WIDTH), row),
                  pl.BlockSpec((1, tm, HGRN_WIDTH), row),
                  pl.BlockSpec((NSA_WIDTH, d), fixed2),
                  pl.BlockSpec((HGRN_WIDTH, d), fixed2),
                  pl.BlockSpec((1, 1, d), per_b),
                  pl.BlockSpec((1, 1, d), per_b),
                  pl.BlockSpec((1, 1, d), per_b),
                  pl.BlockSpec((1, d), fixed2)],
        out_specs=(pl.BlockSpec((1, tm, d), row), pl.BlockSpec((1, tm, d), row), pl.BlockSpec((1, tm, d // 2), row)),
        out_shape=(jax.ShapeDtypeStruct((b, s, d), F32), jax.ShapeDtypeStruct((b, s, d), F32),
                   jax.ShapeDtypeStruct((b, s, d // 2), jnp.uint32)),
        compiler_params=_cparams("parallel", "parallel"),
        name="outproj",
    )(x, attn, rec, w[:NSA_WIDTH], w[NSA_WIDTH:], gt1, sc2, sh2, norm2_w)


def _mixer(x, c, ada_w, ada_b, norm1_w, norm2_w, w_in, q_norm_w, k_norm_w, cmp_pos, cmp_w1, cmp_b1, cmp_w2,
           attn_out_norm_w, hgrn_lb_param, rec_out_norm_w, w_out):
    b, s, d = x.shape
    mod = _mod(c, ada_w, ada_b)
    sh1, sc1, gt1, sh2, sc2, gt2 = [m.reshape(b, 1, d) for m in jnp.split(mod, 6, axis=-1)]
    o = NSA_WIDTH + 6 * KV_WIDTH
    w_cat = jnp.concatenate([w_in[:, :o], w_in[:, o:o + NSA_HEADS * 3],
                             jnp.zeros((d, GATE_PAD - NSA_HEADS * 3), w_in.dtype),
                             w_in[:, o + NSA_HEADS * 3:]], axis=1).astype(BF16)
    tm = min(256, s)
    (q, kc_raw, vc_raw, ks, vst, kw, vwt, gates_t, hq, hk, hlf, hv, hg) = _inproj(
        x, sc1, sh1, norm1_w.reshape(1, d), w_cat, q_norm_w.reshape(1, HEAD_DIM), k_norm_w, hgrn_lb_param, tm)
    kc, vct = _compress(kc_raw, vc_raw, cmp_pos, cmp_w1, cmp_b1, cmp_w2, k_norm_w)
    attn = _nsa(q, kc, vct, ks, vst, kw, vwt, gates_t, attn_out_norm_w)
    rec = _hgrn(hq, hk, hlf, hv, hg, rec_out_norm_w, min(512, s))
    x1, h2, h2p = _outproj(x, attn, rec, w_out, gt1, sc2, sh2, norm2_w.reshape(1, d), tm)
    return x1, h2, h2p, gt2


def _router_kernel(h_ref, rwt_ref, bias_ref, tri_ref, ones_ref, idx_ref, w_ref, rank_ref, cnt_ref, carry_scr, *, tr):
    @pl.when(pl.program_id(0) == 0)
    def _():
        carry_scr[...] = jnp.zeros_like(carry_scr)

    h = h_ref[...]
    h_hi = h.astype(BF16)
    h_lo = (h - h_hi.astype(F32)).astype(BF16)
    logits = _dot_nt(rwt_ref[0], h_hi) + _dot_nt(rwt_ref[1], h_hi) + _dot_nt(rwt_ref[0], h_lo)
    scores = _sigmoid(logits)
    biased = scores + bias_ref[...]
    neg = -jnp.inf

    gs = []
    for g in range(N_GROUPS):
        sub = biased[g * GROUP_SIZE:(g + 1) * GROUP_SIZE, :]
        m1 = jnp.max(sub, axis=0, keepdims=True)
        dup = jnp.sum((sub == m1).astype(F32), axis=0, keepdims=True)
        m2 = jnp.max(jnp.where(sub < m1, sub, neg), axis=0, keepdims=True)
        gs.append(m1 + jnp.where(dup >= 2.0, m1, m2))
    parts = []
    for g in range(N_GROUPS):
        beaten = jnp.zeros_like(gs[g])
        for g2 in range(N_GROUPS):
            if g2 != g:
                beats = (gs[g2] >= gs[g]) if g2 < g else (gs[g2] > gs[g])
                beaten = beaten + beats.astype(F32)
        sub = biased[g * GROUP_SIZE:(g + 1) * GROUP_SIZE, :]
        parts.append(jnp.where(beaten < float(TOPK_GROUPS), sub, neg))
    cand = jnp.concatenate(parts, axis=0)

    rowf = lax.broadcasted_iota(I32, (N_EXPERTS, tr), 0).astype(F32)
    idx_rows, w_rows, hits = [], [], []
    multi = jnp.zeros((N_EXPERTS, tr), F32)
    for _ in range(TOP_K):
        mx = jnp.max(cand, axis=0, keepdims=True)
        first = jnp.min(jnp.where(cand == mx, rowf, float(N_EXPERTS)), axis=0, keepdims=True)
        hit = rowf == first
        idx_rows.append(first)
        w_rows.append(jnp.sum(jnp.where(hit, scores, 0.0), axis=0, keepdims=True))
        cand = jnp.where(hit, neg, cand)
        multi = jnp.where(hit, 1.0, multi)
    w = jnp.concatenate(w_rows, axis=0)
    w_ref[...] = w / jnp.sum(w, axis=0, keepdims=True) * ROUTED_SCALE
    idx = jnp.concatenate(idx_rows, axis=0)
    idx_ref[...] = idx.astype(I32)

    carry = carry_scr[...]
    mb = multi.astype(BF16)
    before = _dot(mb, tri_ref[...]) + jnp.concatenate([carry] * (tr // 128), axis=1)
    rank_rows = [jnp.sum(jnp.where(rowf == idx_rows[k], before, 0.0), axis=0, keepdims=True) for k in range(TOP_K)]
    rank_ref[...] = jnp.concatenate(rank_rows, axis=0).astype(I32)
    carry = carry + _dot(mb, ones_ref[...])
    carry_scr[...] = carry
    cnt_ref[...] = carry


def _router(h2, router_w, router_bias, tr):
    t, d = h2.shape
    tri = jnp.asarray(np.triu(np.ones((tr, tr), np.float32), 1)).astype(BF16)
    ones = jnp.ones((tr, 128), BF16)
    tok = pl.BlockSpec((TOP_K, tr), lambda i: (0, i))
    fixed = lambda i: (0, 0)
    rwt = router_w.T
    rwt_hi = rwt.astype(BF16)
    rwt_split = jnp.stack([rwt_hi, (rwt - rwt_hi.astype(F32)).astype(BF16)])
    return pl.pallas_call(
        functools.partial(_router_kernel, tr=tr),
        grid=(t // tr,),
        in_specs=[pl.BlockSpec((tr, d), lambda i: (i, 0)),
                  pl.BlockSpec((2, N_EXPERTS, d), lambda i: (0, 0, 0)),
                  pl.BlockSpec((N_EXPERTS, 1), fixed),
                  pl.BlockSpec((tr, tr), fixed),
                  pl.BlockSpec((tr, 128), fixed)],
        out_specs=(tok, tok, tok, pl.BlockSpec((N_EXPERTS, 128), fixed)),
        out_shape=(jax.ShapeDtypeStruct((TOP_K, t), I32), jax.ShapeDtypeStruct((TOP_K, t), F32),
                   jax.ShapeDtypeStruct((TOP_K, t), I32), jax.ShapeDtypeStruct((N_EXPERTS, 128), F32)),
        scratch_shapes=[pltpu.VMEM((N_EXPERTS, 128), F32)],
        compiler_params=_cparams("arbitrary"),
        name="router",
    )(h2, rwt_split, router_bias.reshape(N_EXPERTS, 1), tri, ones)


def _pack_bf16_pair(a, b):
    ua = lax.bitcast_convert_type(a.astype(BF16).astype(F32), jnp.uint32)
    ub = lax.bitcast_convert_type(b.astype(BF16).astype(F32), jnp.uint32)
    return ua | (ub >> 16)


def _unpack_bf16_pair(w):
    a = lax.bitcast_convert_type(w & jnp.uint32(0xFFFF0000), F32)
    b = lax.bitcast_convert_type(w << 16, F32)
    return a, b


def _slot_kernel(ps_ref, idx_ref, rank_ref, slot_ref):
    idx = idx_ref[...]

    def body(e, acc):
        return jnp.where(idx == e, ps_ref[e], acc)

    slot_ref[...] = lax.fori_loop(0, N_EXPERTS, body, jnp.zeros_like(idx)) + rank_ref[...]


def _slots(pad_start, idx, rank, tt):
    t = idx.shape[1]
    tok = pl.BlockSpec((TOP_K, tt), lambda i, ps: (0, i))
    return pl.pallas_call(
        _slot_kernel,
        grid_spec=pltpu.PrefetchScalarGridSpec(num_scalar_prefetch=1, grid=(t // tt,),
                                               in_specs=[tok, tok], out_specs=tok),
        out_shape=jax.ShapeDtypeStruct((TOP_K, t), I32),
        compiler_params=_cparams("parallel"),
        name="slots",
    )(pad_start, idx, rank)


SC_CORES = 2
SC_SUBCORES = 16
SC_CHUNK = 64


def _sc_mesh():
    return plsc.VectorSubcoreMesh(core_axis_name="c", subcore_axis_name="s")


def _sc_dispatch(h2p, slot_chunks, n_rows):
    t, dw = h2p.shape
    per = slot_chunks.shape[0] // (SC_CORES * SC_SUBCORES)

    def body(h_hbm, slot_hbm, xs_hbm, idx_v, rows_v, sem):
        wid = lax.axis_index("s") * SC_CORES + lax.axis_index("c")

        @pl.loop(0, per)
        def _(c):
            ch = wid * per + c
            pltpu.sync_copy(slot_hbm.at[ch], idx_v)
            pltpu.sync_copy(h_hbm.at[pl.ds(ch * SC_CHUNK, SC_CHUNK)], rows_v)
            copies = [pltpu.async_copy(rows_v, xs_hbm.at[idx_v.at[k]], sem) for k in range(TOP_K)]
            for cp in copies:
                cp.wait()

    return pl.kernel(
        body, out_type=jax.ShapeDtypeStruct((n_rows, dw), h2p.dtype), mesh=_sc_mesh(),
        scratch_types=[pltpu.VMEM((TOP_K, SC_CHUNK), I32), pltpu.VMEM((SC_CHUNK, dw), h2p.dtype),
                       pltpu.SemaphoreType.DMA],
    )(h2p, slot_chunks)


def _sc_gather(ys, slot_chunks, t):
    dw = ys.shape[1]
    per = slot_chunks.shape[0] // (SC_CORES * SC_SUBCORES)

    def body(ys_hbm, slot_hbm, yg_hbm, idx_v, rows_v, gsem, wsem):
        wid = lax.axis_index("s") * SC_CORES + lax.axis_index("c")

        @pl.loop(0, per)
        def _(c):
            ch = wid * per + c
            pltpu.sync_copy(slot_hbm.at[ch], idx_v)
            gathers = [None] * TOP_K
            writes = [None] * TOP_K
            gathers[0] = pltpu.async_copy(ys_hbm.at[idx_v.at[0]], rows_v.at[0], gsem)
            for k in range(TOP_K):
                gathers[k].wait()
                if k + 1 < TOP_K:
                    if k >= 1:
                        writes[k - 1].wait()
                    gathers[k + 1] = pltpu.async_copy(ys_hbm.at[idx_v.at[k + 1]], rows_v.at[(k + 1) % 2], gsem)
                writes[k] = pltpu.async_copy(rows_v.at[k % 2], yg_hbm.at[k, pl.ds(ch * SC_CHUNK, SC_CHUNK)], wsem)
            writes[TOP_K - 2].wait()
            writes[TOP_K - 1].wait()

    return pl.kernel(
        body, out_type=jax.ShapeDtypeStruct((TOP_K, t, dw), ys.dtype), mesh=_sc_mesh(),
        scratch_types=[pltpu.VMEM((TOP_K, SC_CHUNK), I32), pltpu.VMEM((2, SC_CHUNK, dw), ys.dtype),
                       pltpu.SemaphoreType.DMA, pltpu.SemaphoreType.DMA],
    )(ys, slot_chunks)


def _experts_kernel(be_ref, nu_ref, bv_ref, run_ref, xs_hbm, wg_hbm, wu_hbm, wd_hbm, ys_ref,
                    xring, rsem, gring, uring, dring, wsem):
    i = pl.program_id(0)
    half = D_MODEL // 2
    n_used = nu_ref[0]
    n_steps = pl.num_programs(0)

    def weight_copies(blk):
        ex = be_ref[blk]
        slot = run_ref[blk] % EXPERT_RING
        return [pltpu.make_async_copy(src.at[ex], ring.at[slot], wsem.at[a, slot])
                for a, (src, ring) in enumerate(((wg_hbm, gring), (wu_hbm, uring), (wd_hbm, dring)))]

    def starts_run(blk):
        return run_ref[blk] != run_ref[jnp.maximum(blk - 1, 0)]

    @pl.when(i == 0)
    def _():
        for cp in weight_copies(jnp.int32(0)):
            cp.start()

        @pl.when((n_steps > 1) & starts_run(jnp.int32(1)))
        def _():
            for cp in weight_copies(jnp.int32(1)):
                cp.start()

    ahead = jnp.minimum(i + (EXPERT_RING - 1), n_steps - 1)

    @pl.when((i + (EXPERT_RING - 1) < n_steps) & starts_run(ahead))
    def _():
        for cp in weight_copies(ahead):
            cp.start()

    @pl.when((i == 0) | starts_run(i))
    def _():
        for cp in weight_copies(i):
            cp.wait()

    wslot = run_ref[i] % EXPERT_RING
    wg_ref, wu_ref, wd_ref = gring.at[wslot], uring.at[wslot], dring.at[wslot]

    def fetch(blk):
        slot = blk % EXPERT_RING
        return pltpu.make_async_copy(xs_hbm.at[pl.ds(pl.multiple_of(blk * EXPERT_BLOCK, EXPERT_BLOCK), EXPERT_BLOCK)],
                                     xring.at[slot], rsem.at[slot])

    @pl.when(i == 0)
    def _():
        for first in range(EXPERT_RING - 1):
            @pl.when(first < n_used)
            def _():
                fetch(jnp.int32(first)).start()

    @pl.when(i + (EXPERT_RING - 1) < n_used)
    def _():
        fetch(i + (EXPERT_RING - 1)).start()

    @pl.when(i < n_used)
    def _():
        fetch(i).wait()

    xs_ref = xring.at[i % EXPERT_RING]

    def ffn(rows):
        live = lax.broadcasted_iota(I32, (rows, xs_ref.shape[1]), 0) < bv_ref[i]
        xa, xb = _unpack_bf16_pair(jnp.where(live, xs_ref[0:rows, :], jnp.uint32(0)))
        xa, xb = xa.astype(BF16), xb.astype(BF16)
        g = _dot(xa, wg_ref[:half, :].astype(BF16)) + _dot(xb, wg_ref[half:, :].astype(BF16))
        u = _dot(xa, wu_ref[:half, :].astype(BF16)) + _dot(xb, wu_ref[half:, :].astype(BF16))
        act = (g * _sigmoid(g) * u).astype(BF16)
        y = _dot(act, wd_ref[...].astype(BF16))
        ys_ref[0:rows, :] = _pack_bf16_pair(y[:, :half], y[:, half:])

    used = i < n_used
    short = bv_ref[i] <= EXPERT_TAIL

    @pl.when(used & jnp.logical_not(short))
    def _():
        ffn(EXPERT_BLOCK)

    @pl.when(used & short)
    def _():
        ffn(EXPERT_TAIL)
        ys_ref[EXPERT_TAIL:, :] = jnp.zeros((EXPERT_BLOCK - EXPERT_TAIL, ys_ref.shape[1]), ys_ref.dtype)

    @pl.when(jnp.logical_not(used))
    def _():
        ys_ref[...] = jnp.zeros_like(ys_ref)


def _experts(xs, blk_e, n_used, blk_valid, w_gate, w_up, w_down):
    n_rows, dw = xs.shape
    d = w_gate.shape[1]
    nblk = n_rows // EXPERT_BLOCK
    blk_run = jnp.cumsum(jnp.concatenate([jnp.zeros((1,), I32), (blk_e[1:] != blk_e[:-1]).astype(I32)])).astype(I32)
    hbm = pl.BlockSpec(memory_space=pl.ANY)
    return pl.pallas_call(
        _experts_kernel,
        grid_spec=pltpu.PrefetchScalarGridSpec(
            num_scalar_prefetch=4,
            grid=(nblk,),
            in_specs=[hbm, hbm, hbm, hbm],
            out_specs=pl.BlockSpec((EXPERT_BLOCK, dw), lambda i, be, nu, bv, rn: (i, 0)),
            scratch_shapes=[pltpu.VMEM((EXPERT_RING, EXPERT_BLOCK, dw), xs.dtype),
                            pltpu.SemaphoreType.DMA((EXPERT_RING,)),
                            pltpu.VMEM((EXPERT_RING, d, EXPERT_FF), w_gate.dtype),
                            pltpu.VMEM((EXPERT_RING, d, EXPERT_FF), w_up.dtype),
                            pltpu.VMEM((EXPERT_RING, EXPERT_FF, d), w_down.dtype),
                            pltpu.SemaphoreType.DMA((3, EXPERT_RING))]),
        out_shape=jax.ShapeDtypeStruct((n_rows, dw), xs.dtype),
        compiler_params=pltpu.CompilerParams(dimension_semantics=("arbitrary",), vmem_limit_bytes=VMEM_LIMIT,
                                             has_side_effects=True),
        name="experts",
    )(blk_e, n_used, blk_valid, blk_run, xs, w_gate, w_up, w_down)


def _combine_kernel(x1_ref, h_ref, w_ref, gt_ref, sg_ref, su_ref, sd_ref, yg_ref, o_ref):
    tc = x1_ref.shape[0]
    half = D_MODEL // 2
    hb = h_ref[...].astype(BF16)
    g = _dot(hb, sg_ref[...])
    u = _dot(hb, su_ref[...])
    ffn = _dot((g * _sigmoid(g) * u).astype(BF16), sd_ref[...])

    w = w_ref[...]
    ra = jnp.zeros((tc, half), F32)
    rb = jnp.zeros((tc, half), F32)
    for k in range(TOP_K):
        ya, yb = _unpack_bf16_pair(yg_ref[k])
        ra = ra + w[:, k:k + 1] * ya
        rb = rb + w[:, k:k + 1] * yb
    ffn = ffn + jnp.concatenate([ra, rb], axis=1)
    o_ref[...] = x1_ref[...] + gt_ref[0] * ffn


def _combine(x1, h2, w_tok, gt2, yg, sg, su, sd, seq, tc):
    t, d = x1.shape
    row = lambda i: (i, 0)
    fixed = lambda i: (0, 0)
    return pl.pallas_call(
        _combine_kernel,
        grid=(t // tc,),
        in_specs=[pl.BlockSpec((tc, d), row),
                  pl.BlockSpec((tc, d), row),
                  pl.BlockSpec((tc, TOP_K), row),
                  pl.BlockSpec((1, 1, d), lambda i: ((i * tc) // seq, 0, 0)),
                  pl.BlockSpec((d, SHARED_FF), fixed),
                  pl.BlockSpec((d, SHARED_FF), fixed),
                  pl.BlockSpec((SHARED_FF, d), fixed),
                  pl.BlockSpec((TOP_K, tc, d // 2), lambda i: (0, i, 0))],
        out_specs=pl.BlockSpec((tc, d), row),
        out_shape=jax.ShapeDtypeStruct((t, d), F32),
        compiler_params=_cparams("parallel"),
        name="combine",
    )(x1, h2, w_tok, gt2, sg.astype(BF16), su.astype(BF16), sd.astype(BF16), yg)


def _moe_parts(x1, h2, h2p, gt2, router_w, router_bias, w_gate, w_up, w_down, sg, su, sd):
    b, s, d = x1.shape
    t = b * s
    h2 = h2.reshape(t, d)
    idx, w, rank, cnt = _router(h2, router_w, router_bias, min(256, t))
    counts = cnt[:, 0].astype(I32)
    padded = (counts + EXPERT_BLOCK - 1) // EXPERT_BLOCK * EXPERT_BLOCK
    pad_end = jnp.cumsum(padded)
    pad_start = pad_end - padded
    n_rows = t * TOP_K + N_EXPERTS * EXPERT_BLOCK
    nblk = n_rows // EXPERT_BLOCK
    n_used = (pad_end[-1:] // EXPERT_BLOCK).astype(I32)
    blk_start = jnp.arange(nblk, dtype=I32) * EXPERT_BLOCK
    owns = (pad_start[None, :] <= blk_start[:, None]) & (blk_start[:, None] < pad_end[None, :])
    e_ids = jnp.arange(N_EXPERTS, dtype=I32)[None, :]
    last_e = jnp.max(jnp.where(counts > 0, e_ids[0], 0))
    blk_e = jnp.where(blk_start < pad_end[-1], jnp.sum(jnp.where(owns, e_ids, 0), axis=1), last_e).astype(I32)
    rows_left = jnp.sum(jnp.where(owns, (pad_start + counts)[None, :] - blk_start[:, None], 0), axis=1)
    blk_valid = jnp.clip(rows_left, 0, EXPERT_BLOCK).astype(I32)
    slot = _slots(pad_start.astype(I32), idx, rank, min(2048, t))
    slot_chunks = slot.reshape(TOP_K, t // SC_CHUNK, SC_CHUNK).transpose(1, 0, 2)
    xs = _sc_dispatch(h2p.reshape(t, d // 2), slot_chunks, n_rows)
    ys = _experts(xs, blk_e, n_used, blk_valid, w_gate, w_up, w_down)
    yg = _sc_gather(ys, slot_chunks, t)
    out = _combine(x1.reshape(t, d), h2, w.T, gt2, yg, sg, su, sd, s, min(256, t))
    return out.reshape(b, s, d), dict(idx=idx, w=w, rank=rank, cnt=cnt)


def kernel(x, c, ada_w, ada_b, norm1_w, norm2_w, w_in, q_norm_w, k_norm_w, cmp_pos, cmp_w1, cmp_b1, cmp_w2, attn_out_norm_w, hgrn_lb_param, rec_out_norm_w, w_out, router_w, router_bias, exp_w_gate, exp_w_up, exp_w_down, shared_w_gate, shared_w_up, shared_w_down):
    assert ada_w.shape[0] == 1, "one layer"
    assert x.shape[0] <= 8 and x.shape[1] % TK == 0 and x.shape[1] >= WINDOW + TQ
    l = 0
    x1, h2, h2p, gt2 = _mixer(x, c, ada_w[l], ada_b[l], norm1_w[l], norm2_w[l], w_in[l], q_norm_w[l], k_norm_w[l],
                         cmp_pos[l], cmp_w1[l], cmp_b1[l], cmp_w2[l], attn_out_norm_w[l], hgrn_lb_param,
                         rec_out_norm_w[l], w_out[l])
    out, _ = _moe_parts(x1, h2, h2p, gt2, router_w[l], router_bias[l], exp_w_gate[l], exp_w_up[l], exp_w_down[l],
                        shared_w_gate[l], shared_w_up[l], shared_w_down[l])
    return out
```

```python
import functools

import numpy as np
import jax
import jax.numpy as jnp
from jax import lax
from jax.experimental import pallas as pl
from jax.experimental.pallas import tpu as pltpu
from jax.experimental.pallas import tpu_sc as plsc

F32 = jnp.float32
BF16 = jnp.bfloat16
I32 = jnp.int32

D_MODEL = 1024
NSA_HEADS = 8
HEAD_DIM = 64
NSA_WIDTH = NSA_HEADS * HEAD_DIM
KV_HEADS = 2
HEADS_PER_KV = NSA_HEADS // KV_HEADS
KV_WIDTH = KV_HEADS * HEAD_DIM
CMP_BLOCK = 32
CMP_STRIDE = 16
CMP_HIDDEN = 256
SEL_BLOCK = 64
N_SELECT = 16
WINDOW = 512
HGRN_HEADS = 4
HGRN_DIM = 128
HGRN_WIDTH = HGRN_HEADS * HGRN_DIM
HGRN_CHUNK = 64
HGRN_SUB = 16
HGRN_LEVELS = ()
HGRN_LEAF = 16
N_EXPERTS = 256
TOP_K = 8
N_GROUPS = 8
GROUP_SIZE = N_EXPERTS // N_GROUPS
TOPK_GROUPS = 4
EXPERT_FF = 256
SHARED_FF = 256
ROUTED_SCALE = 2.5
RMS_EPS = 1e-6
BIG = 1e9
LOG2E = 1.4426950408889634
GATE_PAD = 128
PROJ_COLS = NSA_WIDTH + 6 * KV_WIDTH + GATE_PAD + 4 * HGRN_WIDTH

VMEM_LIMIT = 56 * 1024 * 1024

TQ = 256
TK = 512
NSA_SIZE_VARIANTS = 4
EXPERT_BLOCK = 512
EXPERT_TAIL = 128
EXPERT_RING = 3
HIGHEST = lax.Precision.HIGHEST


def _cparams(*sem):
    return pltpu.CompilerParams(dimension_semantics=sem, vmem_limit_bytes=VMEM_LIMIT)


def _sigmoid(x):
    return 1.0 / (1.0 + jnp.exp(-x))


def _dot_nt(a, b):
    return lax.dot_general(a, b, (((1,), (1,)), ((), ())), preferred_element_type=F32)


def _dot(a, b, **kw):
    return jnp.dot(a, b, preferred_element_type=F32, **kw)


def _split_dot(a_bf16_exact, x):
    hi = x.astype(BF16)
    lo = (x - hi.astype(F32)).astype(BF16)
    return _dot(a_bf16_exact, hi) + _dot(a_bf16_exact, lo)


def _mod_kernel(c_ref, w_ref, b_ref, o_ref):
    c = c_ref[...]
    cond = c * _sigmoid(c)
    o_ref[...] = _dot(cond, w_ref[...], precision=HIGHEST) + b_ref[...]


def _mod(c, ada_w, ada_b):
    b, d = c.shape
    rows = 8
    c_pad = jnp.zeros((rows, d), F32).at[:b].set(c)
    n = ada_w.shape[1]
    out = pl.pallas_call(
        _mod_kernel,
        grid=(n // d,),
        in_specs=[pl.BlockSpec((rows, d), lambda j: (0, 0)),
                  pl.BlockSpec((d, d), lambda j: (0, j)),
                  pl.BlockSpec((1, d), lambda j: (0, j))],
        out_specs=pl.BlockSpec((rows, d), lambda j: (0, j)),
        out_shape=jax.ShapeDtypeStruct((rows, n), F32),
        compiler_params=_cparams("parallel"),
        name="mod",
    )(c_pad, ada_w, ada_b.reshape(1, n))
    return out[:b]


def _head_rms(t, w):
    return t * lax.rsqrt(jnp.mean(t * t, axis=-1, keepdims=True) + RMS_EPS) * w


def _pos_digits(pos):
    lane = lax.broadcasted_iota(I32, pos.shape, 1)
    d0 = (lane == 0) | (lane == 3) | (lane == 6)
    d1 = (lane == 1) | (lane == 4) | (lane == 7)
    d2 = (lane == 2) | (lane == 5) | (lane == 8)
    dig = jnp.where(d0, pos >> 12, jnp.where(d1, (pos >> 6) & 63, jnp.where(d2, pos & 63, 0)))
    return dig.astype(F32)


def _inproj_kernel(x_ref, sc_ref, sh_ref, n1_ref, w_ref, qnw_ref, knw_ref, lbp_ref, qaug_ref,
                   q_ref, kcr_ref, vcr_ref, ks_ref, vst_ref, kw_ref, vwt_ref, gt_ref,
                   hq_ref, hk_ref, hlf_ref, hv_ref, hg_ref):
    x = x_ref[0]
    ms = jnp.mean(x * x, axis=-1, keepdims=True)
    h = x * lax.rsqrt(ms + RMS_EPS) * n1_ref[...] * (1.0 + sc_ref[0]) + sh_ref[0]
    p = _dot(h.astype(BF16), w_ref[...])
    tm = x.shape[0]

    qnw = qnw_ref[...]
    for hd in range(NSA_HEADS):
        t = p[:, hd * HEAD_DIM:(hd + 1) * HEAD_DIM]
        qn = _head_rms(t, qnw) * (HEAD_DIM ** -0.5 * LOG2E)
        qa = jnp.broadcast_to(qaug_ref[hd:hd + 1, :], (tm, HEAD_DIM))
        q_ref[0, hd] = jnp.concatenate([qn, qa], axis=1).astype(BF16)
    kaug = _pos_digits(pl.program_id(1) * tm + lax.broadcasted_iota(I32, (tm, HEAD_DIM), 0))

    o = NSA_WIDTH
    kcr_ref[0] = p[:, o:o + KV_WIDTH]
    vcr_ref[0] = p[:, o + KV_WIDTH:o + 2 * KV_WIDTH]
    ks = p[:, o + 2 * KV_WIDTH:o + 3 * KV_WIDTH]
    vs = p[:, o + 3 * KV_WIDTH:o + 4 * KV_WIDTH]
    kw = p[:, o + 4 * KV_WIDTH:o + 5 * KV_WIDTH]
    vw = p[:, o + 5 * KV_WIDTH:o + 6 * KV_WIDTH]
    for g in range(KV_HEADS):
        sl = slice(g * HEAD_DIM, (g + 1) * HEAD_DIM)
        ks_ref[0, g] = jnp.concatenate([_head_rms(ks[:, sl], knw_ref[1:2, :]), kaug], axis=1).astype(BF16)
        kw_ref[0, g] = jnp.concatenate([_head_rms(kw[:, sl], knw_ref[2:3, :]), kaug], axis=1).astype(BF16)
    vst = vs.T.astype(BF16)
    vwt = vw.T.astype(BF16)
    for g in range(KV_HEADS):
        vst_ref[0, g] = vst[g * HEAD_DIM:(g + 1) * HEAD_DIM, :]
        vwt_ref[0, g] = vwt[g * HEAD_DIM:(g + 1) * HEAD_DIM, :]

    o = NSA_WIDTH + 6 * KV_WIDTH
    gates = _sigmoid(p[:, o:o + GATE_PAD])
    gt_ref[0] = gates.T[:NSA_HEADS * 3, :]

    o = o + GATE_PAD
    hq = p[:, o:o + HGRN_WIDTH]
    hf = p[:, o + HGRN_WIDTH:o + 2 * HGRN_WIDTH]
    hi = p[:, o + 2 * HGRN_WIDTH:o + 3 * HGRN_WIDTH]
    hg = p[:, o + 3 * HGRN_WIDTH:o + 4 * HGRN_WIDTH]
    lbp = lbp_ref[...]
    e = jnp.exp(lbp - jnp.max(lbp, axis=0, keepdims=True))
    lb = e[0:1, :] / jnp.sum(e, axis=0, keepdims=True)
    f = lb + (1.0 - lb) * _sigmoid(hf)
    hq_ref[0] = hq * _sigmoid(hq) * (HGRN_DIM ** -0.5)
    hk_ref[0] = 1.0 - f
    hlf_ref[0] = jnp.log(f)
    hv_ref[0] = hi
    hg_ref[0] = _sigmoid(hg)


def _inproj(x, sc1, sh1, norm1_w, w_cat, q_norm_w, k_norm_w, lb_param, tm):
    b, s, d = x.shape
    row = lambda bi, i: (bi, i, 0)
    per_b = lambda bi, i: (bi, 0, 0)
    fixed2 = lambda bi, i: (0, 0)
    aw = 2 * HEAD_DIM
    rest = np.array([2.0 ** (-8.0 * (i + 1) / NSA_HEADS) for i in range(NSA_HEADS)], np.float64) * LOG2E
    qaug = np.zeros((NSA_HEADS, HEAD_DIM), np.float32)
    for i in range(3):
        term = rest.astype(np.float32).astype(BF16).astype(np.float64)
        rest = rest - term
        for dgt, wgt in enumerate((4096.0, 64.0, 1.0)):
            qaug[:, 3 * i + dgt] = term * wgt
    assert np.all(qaug == qaug.astype(BF16).astype(np.float32))
    out_shape = (
        jax.ShapeDtypeStruct((b, NSA_HEADS, s, aw), BF16),
        jax.ShapeDtypeStruct((b, s, KV_WIDTH), F32),
        jax.ShapeDtypeStruct((b, s, KV_WIDTH), F32),
        jax.ShapeDtypeStruct((b, KV_HEADS, s, aw), BF16),
        jax.ShapeDtypeStruct((b, KV_HEADS, HEAD_DIM, s), BF16),
        jax.ShapeDtypeStruct((b, KV_HEADS, s, aw), BF16),
        jax.ShapeDtypeStruct((b, KV_HEADS, HEAD_DIM, s), BF16),
        jax.ShapeDtypeStruct((b, NSA_HEADS * 3, s), F32),
    ) + tuple(jax.ShapeDtypeStruct((b, s, HGRN_WIDTH), F32) for _ in range(5))
    hm = lambda n, w: pl.BlockSpec((1, n, tm, w), lambda bi, i: (bi, 0, i, 0))
    hmt = lambda n, w: pl.BlockSpec((1, n, w, tm), lambda bi, i: (bi, 0, 0, i))
    out_specs = (
        hm(NSA_HEADS, aw),
        pl.BlockSpec((1, tm, KV_WIDTH), row),
        pl.BlockSpec((1, tm, KV_WIDTH), row),
        hm(KV_HEADS, aw), hmt(KV_HEADS, HEAD_DIM),
        hm(KV_HEADS, aw), hmt(KV_HEADS, HEAD_DIM),
        pl.BlockSpec((1, NSA_HEADS * 3, tm), lambda bi, i: (bi, 0, i)),
    ) + tuple(pl.BlockSpec((1, tm, HGRN_WIDTH), row) for _ in range(5))
    return pl.pallas_call(
        _inproj_kernel,
        grid=(b, s // tm),
        in_specs=[pl.BlockSpec((1, tm, d), row),
                  pl.BlockSpec((1, 1, d), per_b),
                  pl.BlockSpec((1, 1, d), per_b),
                  pl.BlockSpec((1, d), fixed2),
                  pl.BlockSpec((d, PROJ_COLS), fixed2),
                  pl.BlockSpec((1, HEAD_DIM), fixed2),
                  pl.BlockSpec((3, HEAD_DIM), fixed2),
                  pl.BlockSpec(lb_param.shape, fixed2),
                  pl.BlockSpec((NSA_HEADS, HEAD_DIM), fixed2)],
        out_specs=out_specs,
        out_shape=out_shape,
        compiler_params=_cparams("parallel", "parallel"),
        name="inproj",
    )(x, sc1, sh1, norm1_w, w_cat, q_norm_w, k_norm_w, lb_param, jnp.asarray(qaug))


def _gelu_tanh(x):
    return 0.5 * x * (1.0 + jnp.tanh(0.7978845608028654 * (x + 0.044715 * x * x * x)))


def _compress_kernel(kch_ref, vch_ref, pos_ref, wa_ref, wb_ref, b1_ref, w2_ref, knw_ref,
                     kc_ref, vct_ref):
    n = kch_ref.shape[1]
    outs = []
    for br, ch_ref in enumerate((kch_ref, vch_ref)):
        ch = ch_ref[0]
        a = _dot((ch + pos_ref[br, 0:1, :]).astype(BF16), wa_ref[br])
        bm = _dot((ch + pos_ref[br, 1:2, :]).astype(BF16), wb_ref[br])
        pre = a + pltpu.roll(bm, n - 1, 0) + b1_ref[br]
        hid = _gelu_tanh(pre).astype(BF16)
        outs.append([_dot(hid[:, g * CMP_HIDDEN:(g + 1) * CMP_HIDDEN], w2_ref[br]) for g in range(KV_HEADS)])
    end_digits = _pos_digits(lax.broadcasted_iota(I32, (n, HEAD_DIM), 0) * CMP_STRIDE + (CMP_BLOCK - 1))
    for g in range(KV_HEADS):
        kc_ref[0, g] = jnp.concatenate([_head_rms(outs[0][g], knw_ref[0:1, :]), end_digits], axis=1).astype(BF16)
    vct = jnp.concatenate(outs[1], axis=1).T.astype(BF16)
    for g in range(KV_HEADS):
        vct_ref[0, g] = vct[g * HEAD_DIM:(g + 1) * HEAD_DIM, :]


def _compress(kc_raw, vc_raw, cmp_pos, cmp_w1, cmp_b1, cmp_w2, k_norm_w):
    b, s, _ = kc_raw.shape
    n = s // CMP_STRIDE
    half = CMP_STRIDE
    cw = CMP_STRIDE * KV_WIDTH
    kch = kc_raw.reshape(b, n, cw)
    vch = vc_raw.reshape(b, n, cw)
    pos = cmp_pos.reshape(2, 2, half, 1, HEAD_DIM)
    pos = jnp.broadcast_to(pos, (2, 2, half, KV_HEADS, HEAD_DIM)).reshape(2, 2, cw)
    w1 = cmp_w1.reshape(2, 2, half, HEAD_DIM, CMP_HIDDEN)
    eye = jnp.eye(KV_HEADS, dtype=F32)
    wfull = jnp.einsum('rhjdn,gk->rhjgdkn', w1, eye).reshape(2, 2, cw, KV_HEADS * CMP_HIDDEN).astype(BF16)
    b1 = jnp.tile(cmp_b1.reshape(2, 1, CMP_HIDDEN), (1, 1, KV_HEADS))
    fix = lambda r: (lambda bi: (0,) * r)
    return pl.pallas_call(
        _compress_kernel,
        grid=(b,),
        in_specs=[pl.BlockSpec((1, n, cw), lambda bi: (bi, 0, 0)),
                  pl.BlockSpec((1, n, cw), lambda bi: (bi, 0, 0)),
                  pl.BlockSpec((2, 2, cw), fix(3)),
                  pl.BlockSpec((2, cw, KV_HEADS * CMP_HIDDEN), fix(3)),
                  pl.BlockSpec((2, cw, KV_HEADS * CMP_HIDDEN), fix(3)),
                  pl.BlockSpec((2, 1, KV_HEADS * CMP_HIDDEN), fix(3)),
                  pl.BlockSpec((2, CMP_HIDDEN, HEAD_DIM), fix(3)),
                  pl.BlockSpec((3, HEAD_DIM), fix(2))],
        out_specs=(pl.BlockSpec((1, KV_HEADS, n, 2 * HEAD_DIM), lambda bi: (bi, 0, 0, 0)),
                   pl.BlockSpec((1, KV_HEADS, HEAD_DIM, n), lambda bi: (bi, 0, 0, 0))),
        out_shape=(jax.ShapeDtypeStruct((b, KV_HEADS, n, 2 * HEAD_DIM), BF16),
                   jax.ShapeDtypeStruct((b, KV_HEADS, HEAD_DIM, n), BF16)),
        compiler_params=_cparams("parallel"),
        name="compress",
    )(kch, vch, pos, wfull[:, 0], wfull[:, 1], b1, cmp_w2.astype(BF16), k_norm_w)


def _nsa_kernel(q_ref, kc_ref, vct_ref, ks_ref, vst_ref, kw_ref, vwt_ref, gt_ref, cdiff_ref, wdiff_ref,
                ovl_ref, oh_ref, onw_ref, wmask_ref, o_ref, buf_a, buf_b, m_scr, acc_scr, oc_scr, bias_scr,
                lst, cnt, *, n_top):
    q0 = pl.program_id(2) * TQ
    ncols = HEADS_PER_KV * TQ
    q = q_ref[0].reshape(ncols, 2 * HEAD_DIM)
    ns = ovl_ref.shape[0]

    def compress_and_select(nk, nb):
        s = jnp.where(cdiff_ref[0:nk, :] <= q0, _dot_nt(kc_ref[0, 0, 0:nk, :], q), -jnp.inf)
        m = jnp.max(s, axis=0, keepdims=True)
        m = jnp.where(m == -jnp.inf, 0.0, m)
        e = jnp.exp2(s - m)
        p = e / jnp.maximum(jnp.sum(e, axis=0, keepdims=True), 1e-30)
        oc_scr[...] = _dot(vct_ref[0, 0, :, 0:nk], p.astype(BF16))

        psum = p[:, 0:TQ]
        for hh in range(1, HEADS_PER_KV):
            psum = psum + p[:, hh * TQ:(hh + 1) * TQ]
        imp = _split_dot(ovl_ref[0:nb, 0:nk], psum)
        blk = lax.broadcasted_iota(I32, (nb, TQ), 0)
        tq = q0 + lax.broadcasted_iota(I32, (nb, TQ), 1)
        cur = tq >> 6
        forced = (blk == 0) | (blk == cur) | (blk == cur - 1)
        rank = jnp.where(forced, BIG, jnp.where(blk * SEL_BLOCK <= tq, imp, -BIG))
        blkf = blk.astype(F32)

        bias = jnp.full((nb, TQ), -1e30, F32)
        for _ in range(min(n_top, nb)):
            mx = jnp.max(rank, axis=0, keepdims=True)
            first = jnp.min(jnp.where(rank == mx, blkf, float(nb)), axis=0, keepdims=True)
            hit = blkf == first
            rank = jnp.where(hit, -jnp.inf, rank)
            bias = jnp.where(hit, 0.0, bias)
        bias_scr[...] = jnp.full((128, TQ), -1e30, F32)
        bias_scr[0:nb, :] = jnp.where(blk == 0, -1e30, bias)

    nc = kc_ref.shape[2]
    quarter = (q0 + TQ - 1) // (ks_ref.shape[2] // NSA_SIZE_VARIANTS)
    for v in range(NSA_SIZE_VARIANTS):
        @pl.when(quarter == v)
        def _():
            compress_and_select(nc * (v + 1) // NSA_SIZE_VARIANTS, ns * (v + 1) // NSA_SIZE_VARIANTS)

    o_c = oc_scr[...]
    bias = bias_scr[...]

    bias_t = bias.T.astype(BF16)
    qq = jnp.concatenate([q, jnp.concatenate([bias_t] * HEADS_PER_KV, axis=0)], axis=1)
    ones_rows = jnp.ones((16, TK), BF16)

    def scores(j):
        k0 = pl.multiple_of(j * TK, TK)
        kk = jnp.concatenate([ks_ref[0, 0, pl.ds(k0, TK), :], oh_ref[pl.ds(k0, TK), :]], axis=1)
        return _dot_nt(kk, qq)

    def consume(buf, j, causal, part):
        sc = buf[...]
        if causal:
            sc = jnp.where(wdiff_ref[0:TK, :] + (q0 - j * TK) >= 0, sc, -1e30)
        k0 = pl.multiple_of(j * TK, TK)
        m_run = m_scr[part]
        m_new = jnp.maximum(m_run, jnp.max(sc, axis=0, keepdims=True))
        ex = jnp.exp2(sc - m_new).astype(BF16)
        va = jnp.concatenate([vst_ref[0, 0, :, pl.ds(k0, TK)], ones_rows], axis=0)
        acc_scr[part] = jnp.exp2(m_run - m_new) * acc_scr[part] + _dot(va, ex)
        m_scr[part] = m_new

    n_past = q0 // TK
    blocks_per_tile = TK // SEL_BLOCK
    cnt[0] = 0
    for j in range(ks_ref.shape[2] // TK):
        wanted = jnp.max(bias[j * blocks_per_tile:(j + 1) * blocks_per_tile, :]) == 0.0

        @pl.when(wanted & (j < n_past))
        def _():
            lst[cnt[0]] = j
            cnt[0] = cnt[0] + 1

    n_sel = cnt[0]
    lst[n_sel] = n_past

    buf_a[...] = scores(lst[0])

    s0 = jnp.where(wdiff_ref[0:SEL_BLOCK, :] + q0 >= 0, _dot_nt(ks_ref[0, 0, 0:SEL_BLOCK, :], q), -1e30)
    m0 = jnp.max(s0, axis=0, keepdims=True)
    v0 = jnp.concatenate([vst_ref[0, 0, :, 0:SEL_BLOCK], jnp.ones((16, SEL_BLOCK), BF16)], axis=0)
    m_scr[0] = m0
    acc_scr[0] = _dot(v0, jnp.exp2(s0 - m0).astype(BF16))
    m_scr[1] = jnp.full((1, ncols), -1e30, F32)
    acc_scr[1] = jnp.zeros((HEAD_DIM + 16, ncols), F32)

    nw = WINDOW + TQ
    start = pl.multiple_of(jnp.maximum(q0 - WINDOW, 0), TQ)
    sw = _dot_nt(kw_ref[0, 0, pl.ds(start, nw), :], q) + wmask_ref[0]
    ew = jnp.exp2(sw - jnp.max(sw, axis=0, keepdims=True))
    vw_aug = jnp.concatenate([vwt_ref[0, 0, :, pl.ds(start, nw)], jnp.ones((16, nw), BF16)], axis=0)
    acc_w = _dot(vw_aug, ew.astype(BF16))
    o_w = acc_w[0:HEAD_DIM, :] / acc_w[HEAD_DIM:HEAD_DIM + 1, :]

    def tiles(first, count):
        for u in range(0, count, 2):
            buf_b[...] = scores(lst[first + u + 1])
            consume(buf_a, lst[first + u], False, 0)
            buf_a[...] = scores(lst[first + u + 2])
            consume(buf_b, lst[first + u + 1], False, 1)
        return 0

    lax.fori_loop(0, n_sel // 4, lambda i, _: tiles(4 * i, 4), 0)
    lax.fori_loop(0, (n_sel // 2) % 2, lambda i, _: tiles((n_sel // 4) * 4, 2), 0)

    @pl.when(n_sel % 2 == 1)
    def _():
        buf_b[...] = scores(n_past)
        consume(buf_a, lst[n_sel - 1], False, 0)
        consume(buf_b, n_past, True, 1)

    @pl.when(n_sel % 2 == 0)
    def _():
        consume(buf_a, n_past, True, 0)

    m_all = jnp.maximum(m_scr[0], m_scr[1])
    acc_s = jnp.exp2(m_scr[0] - m_all) * acc_scr[0] + jnp.exp2(m_scr[1] - m_all) * acc_scr[1]
    o_s = acc_s[0:HEAD_DIM, :] / acc_s[HEAD_DIM:HEAD_DIM + 1, :]

    gt = gt_ref[0, 0]
    outs = []
    for hh in range(HEADS_PER_KV):
        cs = slice(hh * TQ, (hh + 1) * TQ)
        o = (gt[3 * hh:3 * hh + 1, :] * o_c[:, cs] + gt[3 * hh + 1:3 * hh + 2, :] * o_s[:, cs]
             + gt[3 * hh + 2:3 * hh + 3, :] * o_w[:, cs])
        o = o * lax.rsqrt(jnp.mean(o * o, axis=0, keepdims=True) + RMS_EPS) * onw_ref[0, hh]
        outs.append(o)
    o_ref[0] = jnp.concatenate(outs, axis=0).T


def _nsa(q, kc, vct, ks, vst, kw, vwt, gates_t, attn_out_norm_w):
    b, _, s, aw = q.shape
    nc = kc.shape[2]
    ns = s // SEL_BLOCK
    n_top = min(N_SELECT, ns)
    ncols = HEADS_PER_KV * TQ
    nw = WINDOW + TQ
    tl = np.arange(ncols)[None, :] & (TQ - 1)
    cdiff = jnp.asarray((np.arange(nc)[:, None] * CMP_STRIDE + (CMP_BLOCK - 1) - tl).astype(np.int32))
    wdiff_np = (tl - np.arange(nw)[:, None]).astype(np.int32)
    wdiff = jnp.asarray(wdiff_np)
    n_off = WINDOW // TQ + 1
    dist_np = wdiff_np[None] + (np.arange(n_off) * TQ)[:, None, None]
    wmask = jnp.asarray(np.where((dist_np >= 0) & (dist_np < WINDOW), 0.0, -np.inf).astype(np.float32))
    ci = np.arange(nc)[None, :] * CMP_STRIDE
    bj = np.arange(ns)[:, None]
    ovl = ((ci < (bj + 1) * SEL_BLOCK) & (ci + CMP_BLOCK > bj * SEL_BLOCK) & (np.arange(nc)[None, :] < nc - 1))
    ovl = jnp.asarray(ovl.astype(np.float32)).astype(BF16)
    assert ns <= 128
    onehot = (np.arange(s)[:, None] // SEL_BLOCK == np.arange(128)[None, :])
    onehot = jnp.asarray(onehot.astype(np.float32)).astype(BF16)
    onw = jnp.broadcast_to(attn_out_norm_w.reshape(KV_HEADS, HEADS_PER_KV, HEAD_DIM, 1),
                           (KV_HEADS, HEADS_PER_KV, HEAD_DIM, TQ))
    gt = gates_t.reshape(b, KV_HEADS, HEADS_PER_KV * 3, s)
    per_bg = lambda bi, g, i: (bi, g, 0, 0)
    fixed = lambda bi, g, i: (0, 0)
    return pl.pallas_call(
        functools.partial(_nsa_kernel, n_top=n_top),
        grid=(b, KV_HEADS, s // TQ),
        in_specs=[pl.BlockSpec((1, HEADS_PER_KV, TQ, aw), lambda bi, g, i: (bi, g, i, 0)),
                  pl.BlockSpec((1, 1, nc, aw), per_bg),
                  pl.BlockSpec((1, 1, HEAD_DIM, nc), per_bg),
                  pl.BlockSpec((1, 1, s, aw), per_bg),
                  pl.BlockSpec((1, 1, HEAD_DIM, s), per_bg),
                  pl.BlockSpec((1, 1, s, aw), per_bg),
                  pl.BlockSpec((1, 1, HEAD_DIM, s), per_bg),
                  pl.BlockSpec((1, 1, HEADS_PER_KV * 3, TQ), lambda bi, g, i: (bi, g, 0, i)),
                  pl.BlockSpec((nc, ncols), fixed, pipeline_mode=pl.Buffered(1)),
                  pl.BlockSpec((nw, ncols), fixed, pipeline_mode=pl.Buffered(1)),
                  pl.BlockSpec((ns, nc), fixed, pipeline_mode=pl.Buffered(1)),
                  pl.BlockSpec((s, 128), fixed, pipeline_mode=pl.Buffered(1)),
                  pl.BlockSpec((1, HEADS_PER_KV, HEAD_DIM, TQ), lambda bi, g, i: (g, 0, 0, 0)),
                  pl.BlockSpec((1, nw, ncols), lambda bi, g, i: (jnp.minimum(i, n_off - 1), 0, 0))],
        out_specs=pl.BlockSpec((1, TQ, HEADS_PER_KV * HEAD_DIM), lambda bi, g, i: (bi, i, g)),
        out_shape=jax.ShapeDtypeStruct((b, s, NSA_WIDTH), F32),
        scratch_shapes=[pltpu.VMEM((TK, ncols), F32), pltpu.VMEM((TK, ncols), F32),
                        pltpu.VMEM((2, 1, ncols), F32), pltpu.VMEM((2, HEAD_DIM + 16, ncols), F32),
                        pltpu.VMEM((HEAD_DIM, ncols), F32), pltpu.VMEM((128, TQ), F32),
                        pltpu.SMEM((s // TK + 1,), I32), pltpu.SMEM((1,), I32)],
        compiler_params=_cparams("parallel", "parallel", "arbitrary"),
        name="nsa",
    )(q, kc, vct, ks, vst, kw, vwt, gt, cdiff, wdiff, ovl, onehot, onw, wmask)


def _hgrn_cum_matrix():
    c = HGRN_CHUNK
    t = np.arange(c)
    mats = [(t[None, :] <= t[:, None])]
    for half in HGRN_LEVELS:
        ref = (t & ~(2 * half - 1)) + half - 1
        mats.append(t[None, :] <= ref[:, None])
    return np.concatenate(mats, axis=0).astype(np.float32)


def _hgrn_kernel(q_ref, k_ref, lf_ref, v_ref, g_ref, onw_ref, cm_ref, o_ref, state_scr, *, n_chunks):
    c = HGRN_CHUNK

    @pl.when(pl.program_id(1) == 0)
    def _():
        state_scr[...] = jnp.zeros_like(state_scr)

    ri = lax.broadcasted_iota(I32, (c, c), 0)
    ci = lax.broadcasted_iota(I32, (c, c), 1)
    rsub = ri // HGRN_SUB
    level_masks = [((ri & ~(2 * h - 1)) == (ci & ~(2 * h - 1))) & ((ri & h) != 0) & ((ci & h) == 0)
                   for h in HGRN_LEVELS]
    diag = ri == ci

    def head_chunk(r0, hd, state_t):
        cols = slice(hd * HGRN_DIM, (hd + 1) * HGRN_DIM)
        q = q_ref[0, pl.ds(r0, c), cols]
        k = k_ref[0, pl.ds(r0, c), cols]
        lf = lf_ref[0, pl.ds(r0, c), cols] * LOG2E
        v = v_ref[0, pl.ds(r0, c), cols]
        cm = cm_ref[...]
        l1 = lf.astype(BF16)
        rest = lf - l1.astype(F32)
        l2 = rest.astype(BF16)
        l3 = (rest - l2.astype(F32)).astype(BF16)
        cums = _dot(cm, l1) + _dot(cm, l2) + _dot(cm, l3)
        cum = cums[0:c]
        o = _dot_nt((q * jnp.exp2(cum)).astype(BF16), state_t.astype(BF16))
        scores = jnp.where(diag, jnp.sum(q * k, axis=-1, keepdims=True), 0.0)

        def factored(ref, mask, acc):
            qs = q * jnp.exp2(jnp.minimum(cum - ref, 0.0))
            kd = k * jnp.exp2(jnp.minimum(ref - cum, 0.0))
            return jnp.where(mask, _dot_nt(qs.astype(BF16), kd.astype(BF16)), acc)

        for i in range(1, c // HGRN_SUB):
            scores = factored(cum[i * HGRN_SUB - 1:i * HGRN_SUB, :], (rsub == i) & (ci < i * HGRN_SUB), scores)
        for lv in range(len(HGRN_LEVELS)):
            scores = factored(cums[(lv + 1) * c:(lv + 2) * c], level_masks[lv], scores)
        for d in range(1, HGRN_LEAF):
            ksh = pltpu.roll(k, d, 0)
            csh = pltpu.roll(cum, d, 0)
            w = jnp.sum(q * ksh * jnp.exp2(cum - csh), axis=-1, keepdims=True)
            scores = jnp.where((ri - ci == d) & ((ri & (HGRN_LEAF - 1)) >= d), w, scores)
        o = o + _dot(scores.astype(BF16), v.astype(BF16))
        last = cum[c - 1:c, :]
        kd = (k * jnp.exp2(last - cum)).astype(BF16)
        state_t = state_t * jnp.exp2(last) + _dot(v.T.astype(BF16), kd)
        o = o * g_ref[0, pl.ds(r0, c), cols]
        o = o * lax.rsqrt(jnp.mean(o * o, axis=-1, keepdims=True) + RMS_EPS) * onw_ref[:, cols]
        o_ref[0, pl.ds(r0, c), cols] = o
        return state_t

    def chunk(ck, states):
        r0 = pl.multiple_of(ck * c, c)
        return tuple(head_chunk(r0, hd, states[hd]) for hd in range(HGRN_HEADS))

    states = lax.fori_loop(0, n_chunks, chunk, tuple(state_scr[hd] for hd in range(HGRN_HEADS)))
    for hd in range(HGRN_HEADS):
        state_scr[hd] = states[hd]


def _hgrn(hq, hk, hlf, hv, hg, rec_out_norm_w, rows):
    b, s, _ = hq.shape
    cm = jnp.asarray(_hgrn_cum_matrix()).astype(BF16)
    blk = pl.BlockSpec((1, rows, HGRN_WIDTH), lambda bi, i: (bi, i, 0))
    return pl.pallas_call(
        functools.partial(_hgrn_kernel, n_chunks=rows // HGRN_CHUNK),
        grid=(b, s // rows),
        in_specs=[blk, blk, blk, blk, blk,
                  pl.BlockSpec((1, HGRN_WIDTH), lambda bi, i: (0, 0)),
                  pl.BlockSpec(cm.shape, lambda bi, i: (0, 0))],
        out_specs=blk,
        out_shape=jax.ShapeDtypeStruct((b, s, HGRN_WIDTH), F32),
        scratch_shapes=[pltpu.VMEM((HGRN_HEADS, HGRN_DIM, HGRN_DIM), F32)],
        compiler_params=_cparams("parallel", "arbitrary"),
        name="hgrn",
    )(hq, hk, hlf, hv, hg, rec_out_norm_w.reshape(1, HGRN_WIDTH), cm)


def _outproj_kernel(x_ref, a_ref, r_ref, wa_ref, wr_ref, gt_ref, sc_ref, sh_ref, n2_ref, x1_ref, h2_ref, h2p_ref):
    mixed = _dot(a_ref[0].astype(BF16), wa_ref[...]) + _dot(r_ref[0].astype(BF16), wr_ref[...])
    x1 = x_ref[0] + gt_ref[0] * mixed
    x1_ref[0] = x1
    ms = jnp.mean(x1 * x1, axis=-1, keepdims=True)
    h2 = x1 * lax.rsqrt(ms + RMS_EPS) * n2_ref[...] * (1.0 + sc_ref[0]) + sh_ref[0]
    h2_ref[0] = h2
    h2p_ref[0] = _pack_bf16_pair(h2[:, :D_MODEL // 2], h2[:, D_MODEL // 2:])


def _outproj(x, attn, rec, w_out, gt1, sc2, sh2, norm2_w, tm):
    b, s, d = x.shape
    row = lambda bi, i: (bi, i, 0)
    per_b = lambda bi, i: (bi, 0, 0)
    fixed2 = lambda bi, i: (0, 0)
    w = w_out.astype(BF16)
    return pl.pallas_call(
        _outproj_kernel,
        grid=(b, s // tm),
        in_specs=[pl.BlockSpec((1, tm, d), row),
                  pl.BlockSpec((1, tm, NSA_WIDTH), row),
                  pl.BlockSpec((1, tm, HGRN_WIDTH), row),
                  pl.BlockSpec((NSA_WIDTH, d), fixed2),
                  pl.BlockSpec((HGRN_WIDTH, d), fixed2),
                  pl.BlockSpec((1, 1, d), per_b),
                  pl.BlockSpec((1, 1, d), per_b),
                  pl.BlockSpec((1, 1, d), per_b),
                  pl.BlockSpec((1, d), fixed2)],
        out_specs=(pl.BlockSpec((1, tm, d), row), pl.BlockSpec((1, tm, d), row), pl.BlockSpec((1, tm, d // 2), row)),
        out_shape=(jax.ShapeDtypeStruct((b, s, d), F32), jax.ShapeDtypeStruct((b, s, d), F32),
                   jax.ShapeDtypeStruct((b, s, d // 2), jnp.uint32)),
        compiler_params=_cparams("parallel", "parallel"),
        name="outproj",
    )(x, attn, rec, w[:NSA_WIDTH], w[NSA_WIDTH:], gt1, sc2, sh2, norm2_w)


def _mixer(x, c, ada_w, ada_b, norm1_w, norm2_w, w_in, q_norm_w, k_norm_w, cmp_pos, cmp_w1, cmp_b1, cmp_w2,
           attn_out_norm_w, hgrn_lb_param, rec_out_norm_w, w_out):
    b, s, d = x.shape
    mod = _mod(c, ada_w, ada_b)
    sh1, sc1, gt1, sh2, sc2, gt2 = [m.reshape(b, 1, d) for m in jnp.split(mod, 6, axis=-1)]
    o = NSA_WIDTH + 6 * KV_WIDTH
    w_cat = jnp.concatenate([w_in[:, :o], w_in[:, o:o + NSA_HEADS * 3],
                             jnp.zeros((d, GATE_PAD - NSA_HEADS * 3), w_in.dtype),
                             w_in[:, o + NSA_HEADS * 3:]], axis=1).astype(BF16)
    tm = min(256, s)
    (q, kc_raw, vc_raw, ks, vst, kw, vwt, gates_t, hq, hk, hlf, hv, hg) = _inproj(
        x, sc1, sh1, norm1_w.reshape(1, d), w_cat, q_norm_w.reshape(1, HEAD_DIM), k_norm_w, hgrn_lb_param, tm)
    kc, vct = _compress(kc_raw, vc_raw, cmp_pos, cmp_w1, cmp_b1, cmp_w2, k_norm_w)
    attn = _nsa(q, kc, vct, ks, vst, kw, vwt, gates_t, attn_out_norm_w)
    rec = _hgrn(hq, hk, hlf, hv, hg, rec_out_norm_w, min(512, s))
    x1, h2, h2p = _outproj(x, attn, rec, w_out, gt1, sc2, sh2, norm2_w.reshape(1, d), tm)
    return x1, h2, h2p, gt2


def _router_kernel(h_ref, rwt_ref, bias_ref, tri_ref, ones_ref, idx_ref, w_ref, rank_ref, cnt_ref, carry_scr, *, tr):
    @pl.when(pl.program_id(0) == 0)
    def _():
        carry_scr[...] = jnp.zeros_like(carry_scr)

    h = h_ref[...]
    h_hi = h.astype(BF16)
    h_lo = (h - h_hi.astype(F32)).astype(BF16)
    logits = _dot_nt(rwt_ref[0], h_hi) + _dot_nt(rwt_ref[1], h_hi) + _dot_nt(rwt_ref[0], h_lo)
    scores = _sigmoid(logits)
    biased = scores + bias_ref[...]
    neg = -jnp.inf

    gs = []
    for g in range(N_GROUPS):
        sub = biased[g * GROUP_SIZE:(g + 1) * GROUP_SIZE, :]
        m1 = jnp.max(sub, axis=0, keepdims=True)
        dup = jnp.sum((sub == m1).astype(F32), axis=0, keepdims=True)
        m2 = jnp.max(jnp.where(sub < m1, sub, neg), axis=0, keepdims=True)
        gs.append(m1 + jnp.where(dup >= 2.0, m1, m2))
    parts = []
    for g in range(N_GROUPS):
        beaten = jnp.zeros_like(gs[g])
        for g2 in range(N_GROUPS):
            if g2 != g:
                beats = (gs[g2] >= gs[g]) if g2 < g else (gs[g2] > gs[g])
                beaten = beaten + beats.astype(F32)
        sub = biased[g * GROUP_SIZE:(g + 1) * GROUP_SIZE, :]
        parts.append(jnp.where(beaten < float(TOPK_GROUPS), sub, neg))
    cand = jnp.concatenate(parts, axis=0)

    rowf = lax.broadcasted_iota(I32, (N_EXPERTS, tr), 0).astype(F32)
    idx_rows, w_rows, hits = [], [], []
    multi = jnp.zeros((N_EXPERTS, tr), F32)
    for _ in range(TOP_K):
        mx = jnp.max(cand, axis=0, keepdims=True)
        first = jnp.min(jnp.where(cand == mx, rowf, float(N_EXPERTS)), axis=0, keepdims=True)
        hit = rowf == first
        idx_rows.append(first)
        w_rows.append(jnp.sum(jnp.where(hit, scores, 0.0), axis=0, keepdims=True))
        cand = jnp.where(hit, neg, cand)
        multi = jnp.where(hit, 1.0, multi)
    w = jnp.concatenate(w_rows, axis=0)
    w_ref[...] = w / jnp.sum(w, axis=0, keepdims=True) * ROUTED_SCALE
    idx = jnp.concatenate(idx_rows, axis=0)
    idx_ref[...] = idx.astype(I32)

    carry = carry_scr[...]
    mb = multi.astype(BF16)
    before = _dot(mb, tri_ref[...]) + jnp.concatenate([carry] * (tr // 128), axis=1)
    rank_rows = [jnp.sum(jnp.where(rowf == idx_rows[k], before, 0.0), axis=0, keepdims=True) for k in range(TOP_K)]
    rank_ref[...] = jnp.concatenate(rank_rows, axis=0).astype(I32)
    carry = carry + _dot(mb, ones_ref[...])
    carry_scr[...] = carry
    cnt_ref[...] = carry


def _router(h2, router_w, router_bias, tr):
    t, d = h2.shape
    tri = jnp.asarray(np.triu(np.ones((tr, tr), np.float32), 1)).astype(BF16)
    ones = jnp.ones((tr, 128), BF16)
    tok = pl.BlockSpec((TOP_K, tr), lambda i: (0, i))
    fixed = lambda i: (0, 0)
    rwt = router_w.T
    rwt_hi = rwt.astype(BF16)
    rwt_split = jnp.stack([rwt_hi, (rwt - rwt_hi.astype(F32)).astype(BF16)])
    return pl.pallas_call(
        functools.partial(_router_kernel, tr=tr),
        grid=(t // tr,),
        in_specs=[pl.BlockSpec((tr, d), lambda i: (i, 0)),
                  pl.BlockSpec((2, N_EXPERTS, d), lambda i: (0, 0, 0)),
                  pl.BlockSpec((N_EXPERTS, 1), fixed),
                  pl.BlockSpec((tr, tr), fixed),
                  pl.BlockSpec((tr, 128), fixed)],
        out_specs=(tok, tok, tok, pl.BlockSpec((N_EXPERTS, 128), fixed)),
        out_shape=(jax.ShapeDtypeStruct((TOP_K, t), I32), jax.ShapeDtypeStruct((TOP_K, t), F32),
                   jax.ShapeDtypeStruct((TOP_K, t), I32), jax.ShapeDtypeStruct((N_EXPERTS, 128), F32)),
        scratch_shapes=[pltpu.VMEM((N_EXPERTS, 128), F32)],
        compiler_params=_cparams("arbitrary"),
        name="router",
    )(h2, rwt_split, router_bias.reshape(N_EXPERTS, 1), tri, ones)


def _pack_bf16_pair(a, b):
    ua = lax.bitcast_convert_type(a.astype(BF16).astype(F32), jnp.uint32)
    ub = lax.bitcast_convert_type(b.astype(BF16).astype(F32), jnp.uint32)
    return ua | (ub >> 16)


def _unpack_bf16_pair(w):
    a = lax.bitcast_convert_type(w & jnp.uint32(0xFFFF0000), F32)
    b = lax.bitcast_convert_type(w << 16, F32)
    return a, b


def _slot_kernel(ps_ref, idx_ref, rank_ref, slot_ref):
    idx = idx_ref[...]

    def body(e, acc):
        return jnp.where(idx == e, ps_ref[e], acc)

    slot_ref[...] = lax.fori_loop(0, N_EXPERTS, body, jnp.zeros_like(idx)) + rank_ref[...]


def _slots(pad_start, idx, rank, tt):
    t = idx.shape[1]
    tok = pl.BlockSpec((TOP_K, tt), lambda i, ps: (0, i))
    return pl.pallas_call(
        _slot_kernel,
        grid_spec=pltpu.PrefetchScalarGridSpec(num_scalar_prefetch=1, grid=(t // tt,),
                                               in_specs=[tok, tok], out_specs=tok),
        out_shape=jax.ShapeDtypeStruct((TOP_K, t), I32),
        compiler_params=_cparams("parallel"),
        name="slots",
    )(pad_start, idx, rank)


SC_CORES = 2
SC_SUBCORES = 16
SC_CHUNK = 64


def _sc_mesh():
    return plsc.VectorSubcoreMesh(core_axis_name="c", subcore_axis_name="s")


def _sc_dispatch(h2p, slot_chunks, n_rows):
    t, dw = h2p.shape
    per = slot_chunks.shape[0] // (SC_CORES * SC_SUBCORES)

    def body(h_hbm, slot_hbm, xs_hbm, idx_v, rows_v, sem):
        wid = lax.axis_index("s") * SC_CORES + lax.axis_index("c")

        @pl.loop(0, per)
        def _(c):
            ch = wid * per + c
            pltpu.sync_copy(slot_hbm.at[ch], idx_v)
            pltpu.sync_copy(h_hbm.at[pl.ds(ch * SC_CHUNK, SC_CHUNK)], rows_v)
            copies = [pltpu.async_copy(rows_v, xs_hbm.at[idx_v.at[k]], sem) for k in range(TOP_K)]
            for cp in copies:
                cp.wait()

    return pl.kernel(
        body, out_type=jax.ShapeDtypeStruct((n_rows, dw), h2p.dtype), mesh=_sc_mesh(),
        scratch_types=[pltpu.VMEM((TOP_K, SC_CHUNK), I32), pltpu.VMEM((SC_CHUNK, dw), h2p.dtype),
                       pltpu.SemaphoreType.DMA],
    )(h2p, slot_chunks)


def _sc_gather(ys, slot_chunks, t):
    dw = ys.shape[1]
    per = slot_chunks.shape[0] // (SC_CORES * SC_SUBCORES)

    def body(ys_hbm, slot_hbm, yg_hbm, idx_v, rows_v, gsem, wsem):
        wid = lax.axis_index("s") * SC_CORES + lax.axis_index("c")

        @pl.loop(0, per)
        def _(c):
            ch = wid * per + c
            pltpu.sync_copy(slot_hbm.at[ch], idx_v)
            gathers = [None] * TOP_K
            writes = [None] * TOP_K
            gathers[0] = pltpu.async_copy(ys_hbm.at[idx_v.at[0]], rows_v.at[0], gsem)
            for k in range(TOP_K):
                gathers[k].wait()
                if k + 1 < TOP_K:
                    if k >= 1:
                        writes[k - 1].wait()
                    gathers[k + 1] = pltpu.async_copy(ys_hbm.at[idx_v.at[k + 1]], rows_v.at[(k + 1) % 2], gsem)
                writes[k] = pltpu.async_copy(rows_v.at[k % 2], yg_hbm.at[k, pl.ds(ch * SC_CHUNK, SC_CHUNK)], wsem)
            writes[TOP_K - 2].wait()
            writes[TOP_K - 1].wait()

    return pl.kernel(
        body, out_type=jax.ShapeDtypeStruct((TOP_K, t, dw), ys.dtype), mesh=_sc_mesh(),
        scratch_types=[pltpu.VMEM((TOP_K, SC_CHUNK), I32), pltpu.VMEM((2, SC_CHUNK, dw), ys.dtype),
                       pltpu.SemaphoreType.DMA, pltpu.SemaphoreType.DMA],
    )(ys, slot_chunks)


def _experts_kernel(be_ref, nu_ref, bv_ref, run_ref, xs_hbm, wg_hbm, wu_hbm, wd_hbm, ys_ref,
                    xring, rsem, gring, uring, dring, wsem):
    i = pl.program_id(0)
    half = D_MODEL // 2
    n_used = nu_ref[0]
    n_steps = pl.num_programs(0)

    def weight_copies(blk):
        ex = be_ref[blk]
        slot = run_ref[blk] % EXPERT_RING
        return [pltpu.make_async_copy(src.at[ex], ring.at[slot], wsem.at[a, slot])
                for a, (src, ring) in enumerate(((wg_hbm, gring), (wu_hbm, uring), (wd_hbm, dring)))]

    def starts_run(blk):
        return run_ref[blk] != run_ref[jnp.maximum(blk - 1, 0)]

    @pl.when(i == 0)
    def _():
        for cp in weight_copies(jnp.int32(0)):
            cp.start()

        @pl.when((n_steps > 1) & starts_run(jnp.int32(1)))
        def _():
            for cp in weight_copies(jnp.int32(1)):
                cp.start()

    ahead = jnp.minimum(i + (EXPERT_RING - 1), n_steps - 1)

    @pl.when((i + (EXPERT_RING - 1) < n_steps) & starts_run(ahead))
    def _():
        for cp in weight_copies(ahead):
            cp.start()

    @pl.when((i == 0) | starts_run(i))
    def _():
        for cp in weight_copies(i):
            cp.wait()

    wslot = run_ref[i] % EXPERT_RING
    wg_ref, wu_ref, wd_ref = gring.at[wslot], uring.at[wslot], dring.at[wslot]

    def fetch(blk):
        slot = blk % EXPERT_RING
        return pltpu.make_async_copy(xs_hbm.at[pl.ds(pl.multiple_of(blk * EXPERT_BLOCK, EXPERT_BLOCK), EXPERT_BLOCK)],
                                     xring.at[slot], rsem.at[slot])

    @pl.when(i == 0)
    def _():
        for first in range(EXPERT_RING - 1):
            @pl.when(first < n_used)
            def _():
                fetch(jnp.int32(first)).start()

    @pl.when(i + (EXPERT_RING - 1) < n_used)
    def _():
        fetch(i + (EXPERT_RING - 1)).start()

    @pl.when(i < n_used)
    def _():
        fetch(i).wait()

    xs_ref = xring.at[i % EXPERT_RING]

    def ffn(rows):
        live = lax.broadcasted_iota(I32, (rows, xs_ref.shape[1]), 0) < bv_ref[i]
        xa, xb = _unpack_bf16_pair(jnp.where(live, xs_ref[0:rows, :], jnp.uint32(0)))
        xa, xb = xa.astype(BF16), xb.astype(BF16)
        g = _dot(xa, wg_ref[:half, :].astype(BF16)) + _dot(xb, wg_ref[half:, :].astype(BF16))
        u = _dot(xa, wu_ref[:half, :].astype(BF16)) + _dot(xb, wu_ref[half:, :].astype(BF16))
        act = (g * _sigmoid(g) * u).astype(BF16)
        y = _dot(act, wd_ref[...].astype(BF16))
        ys_ref[0:rows, :] = _pack_bf16_pair(y[:, :half], y[:, half:])

    used = i < n_used
    short = bv_ref[i] <= EXPERT_TAIL

    @pl.when(used & jnp.logical_not(short))
    def _():
        ffn(EXPERT_BLOCK)

    @pl.when(used & short)
    def _():
        ffn(EXPERT_TAIL)
        ys_ref[EXPERT_TAIL:, :] = jnp.zeros((EXPERT_BLOCK - EXPERT_TAIL, ys_ref.shape[1]), ys_ref.dtype)

    @pl.when(jnp.logical_not(used))
    def _():
        ys_ref[...] = jnp.zeros_like(ys_ref)


def _experts(xs, blk_e, n_used, blk_valid, w_gate, w_up, w_down):
    n_rows, dw = xs.shape
    d = w_gate.shape[1]
    nblk = n_rows // EXPERT_BLOCK
    blk_run = jnp.cumsum(jnp.concatenate([jnp.zeros((1,), I32), (blk_e[1:] != blk_e[:-1]).astype(I32)])).astype(I32)
    hbm = pl.BlockSpec(memory_space=pl.ANY)
    return pl.pallas_call(
        _experts_kernel,
        grid_spec=pltpu.PrefetchScalarGridSpec(
            num_scalar_prefetch=4,
            grid=(nblk,),
            in_specs=[hbm, hbm, hbm, hbm],
            out_specs=pl.BlockSpec((EXPERT_BLOCK, dw), lambda i, be, nu, bv, rn: (i, 0)),
            scratch_shapes=[pltpu.VMEM((EXPERT_RING, EXPERT_BLOCK, dw), xs.dtype),
                            pltpu.SemaphoreType.DMA((EXPERT_RING,)),
                            pltpu.VMEM((EXPERT_RING, d, EXPERT_FF), w_gate.dtype),
                            pltpu.VMEM((EXPERT_RING, d, EXPERT_FF), w_up.dtype),
                            pltpu.VMEM((EXPERT_RING, EXPERT_FF, d), w_down.dtype),
                            pltpu.SemaphoreType.DMA((3, EXPERT_RING))]),
        out_shape=jax.ShapeDtypeStruct((n_rows, dw), xs.dtype),
        compiler_params=pltpu.CompilerParams(dimension_semantics=("arbitrary",), vmem_limit_bytes=VMEM_LIMIT,
                                             has_side_effects=True),
        name="experts",
    )(blk_e, n_used, blk_valid, blk_run, xs, w_gate, w_up, w_down)


def _combine_kernel(x1_ref, h_ref, w_ref, gt_ref, sg_ref, su_ref, sd_ref, yg_ref, o_ref):
    tc = x1_ref.shape[0]
    half = D_MODEL // 2
    hb = h_ref[...].astype(BF16)
    g = _dot(hb, sg_ref[...])
    u = _dot(hb, su_ref[...])
    ffn = _dot((g * _sigmoid(g) * u).astype(BF16), sd_ref[...])

    w = w_ref[...]
    ra = jnp.zeros((tc, half), F32)
    rb = jnp.zeros((tc, half), F32)
    for k in range(TOP_K):
        ya, yb = _unpack_bf16_pair(yg_ref[k])
        ra = ra + w[:, k:k + 1] * ya
        rb = rb + w[:, k:k + 1] * yb
    ffn = ffn + jnp.concatenate([ra, rb], axis=1)
    o_ref[...] = x1_ref[...] + gt_ref[0] * ffn


def _combine(x1, h2, w_tok, gt2, yg, sg, su, sd, seq, tc):
    t, d = x1.shape
    row = lambda i: (i, 0)
    fixed = lambda i: (0, 0)
    return pl.pallas_call(
        _combine_kernel,
        grid=(t // tc,),
        in_specs=[pl.BlockSpec((tc, d), row),
                  pl.BlockSpec((tc, d), row),
                  pl.BlockSpec((tc, TOP_K), row),
                  pl.BlockSpec((1, 1, d), lambda i: ((i * tc) // seq, 0, 0)),
                  pl.BlockSpec((d, SHARED_FF), fixed),
                  pl.BlockSpec((d, SHARED_FF), fixed),
                  pl.BlockSpec((SHARED_FF, d), fixed),
                  pl.BlockSpec((TOP_K, tc, d // 2), lambda i: (0, i, 0))],
        out_specs=pl.BlockSpec((tc, d), row),
        out_shape=jax.ShapeDtypeStruct((t, d), F32),
        compiler_params=_cparams("parallel"),
        name="combine",
    )(x1, h2, w_tok, gt2, sg.astype(BF16), su.astype(BF16), sd.astype(BF16), yg)


def _moe_parts(x1, h2, h2p, gt2, router_w, router_bias, w_gate, w_up, w_down, sg, su, sd):
    b, s, d = x1.shape
    t = b * s
    h2 = h2.reshape(t, d)
    idx, w, rank, cnt = _router(h2, router_w, router_bias, min(256, t))
    counts = cnt[:, 0].astype(I32)
    padded = (counts + EXPERT_BLOCK - 1) // EXPERT_BLOCK * EXPERT_BLOCK
    pad_end = jnp.cumsum(padded)
    pad_start = pad_end - padded
    n_rows = t * TOP_K + N_EXPERTS * EXPERT_BLOCK
    nblk = n_rows // EXPERT_BLOCK
    n_used = (pad_end[-1:] // EXPERT_BLOCK).astype(I32)
    blk_start = jnp.arange(nblk, dtype=I32) * EXPERT_BLOCK
    owns = (pad_start[None, :] <= blk_start[:, None]) & (blk_start[:, None] < pad_end[None, :])
    e_ids = jnp.arange(N_EXPERTS, dtype=I32)[None, :]
    last_e = jnp.max(jnp.where(counts > 0, e_ids[0], 0))
    blk_e = jnp.where(blk_start < pad_end[-1], jnp.sum(jnp.where(owns, e_ids, 0), axis=1), last_e).astype(I32)
    rows_left = jnp.sum(jnp.where(owns, (pad_start + counts)[None, :] - blk_start[:, None], 0), axis=1)
    blk_valid = jnp.clip(rows_left, 0, EXPERT_BLOCK).astype(I32)
    slot = _slots(pad_start.astype(I32), idx, rank, min(2048, t))
    slot_chunks = slot.reshape(TOP_K, t // SC_CHUNK, SC_CHUNK).transpose(1, 0, 2)
    xs = _sc_dispatch(h2p.reshape(t, d // 2), slot_chunks, n_rows)
    ys = _experts(xs, blk_e, n_used, blk_valid, w_gate, w_up, w_down)
    yg = _sc_gather(ys, slot_chunks, t)
    out = _combine(x1.reshape(t, d), h2, w.T, gt2, yg, sg, su, sd, s, min(256, t))
    return out.reshape(b, s, d), dict(idx=idx, w=w, rank=rank, cnt=cnt)


def kernel(x, c, ada_w, ada_b, norm1_w, norm2_w, w_in, q_norm_w, k_norm_w, cmp_pos, cmp_w1, cmp_b1, cmp_w2, attn_out_norm_w, hgrn_lb_param, rec_out_norm_w, w_out, router_w, router_bias, exp_w_gate, exp_w_up, exp_w_down, shared_w_gate, shared_w_up, shared_w_down):
    assert ada_w.shape[0] == 1, "one layer"
    assert x.shape[0] <= 8 and x.shape[1] % TK == 0 and x.shape[1] >= WINDOW + TQ
    l = 0
    x1, h2, h2p, gt2 = _mixer(x, c, ada_w[l], ada_b[l], norm1_w[l], norm2_w[l], w_in[l], q_norm_w[l], k_norm_w[l],
                         cmp_pos[l], cmp_w1[l], cmp_b1[l], cmp_w2[l], attn_out_norm_w[l], hgrn_lb_param,
                         rec_out_norm_w[l], w_out[l])
    out, _ = _moe_parts(x1, h2, h2p, gt2, router_w[l], router_bias[l], exp_w_gate[l], exp_w_up[l], exp_w_down[l],
                        shared_w_gate[l], shared_w_up[l], shared_w_down[l])
    return out
```

```python
import functools

import numpy as np
import jax
import jax.numpy as jnp
from jax import lax
from jax.experimental import pallas as pl
from jax.experimental.pallas import tpu as pltpu
from jax.experimental.pallas import tpu_sc as plsc

F32 = jnp.float32
BF16 = jnp.bfloat16
I32 = jnp.int32

D_MODEL = 1024
NSA_HEADS = 8
HEAD_DIM = 64
NSA_WIDTH = NSA_HEADS * HEAD_DIM
KV_HEADS = 2
HEADS_PER_KV = NSA_HEADS // KV_HEADS
KV_WIDTH = KV_HEADS * HEAD_DIM
CMP_BLOCK = 32
CMP_STRIDE = 16
CMP_HIDDEN = 256
SEL_BLOCK = 64
N_SELECT = 16
WINDOW = 512
HGRN_HEADS = 4
HGRN_DIM = 128
HGRN_WIDTH = HGRN_HEADS * HGRN_DIM
HGRN_CHUNK = 64
HGRN_SUB = 16
N_EXPERTS = 256
TOP_K = 8
N_GROUPS = 8
GROUP_SIZE = N_EXPERTS // N_GROUPS
TOPK_GROUPS = 4
EXPERT_FF = 256
SHARED_FF = 256
ROUTED_SCALE = 2.5
RMS_EPS = 1e-6
BIG = 1e9
LOG2E = 1.4426950408889634
GATE_PAD = 128
PROJ_COLS = NSA_WIDTH + 6 * KV_WIDTH + GATE_PAD + 4 * HGRN_WIDTH

VMEM_LIMIT = 56 * 1024 * 1024

TQ = 256
TK = 512
NSA_SIZE_VARIANTS = 4
EXPERT_BLOCK = 512
EXPERT_TAIL = 128
EXPERT_RING = 3
HIGHEST = lax.Precision.HIGHEST


def _cparams(*sem):
    return pltpu.CompilerParams(dimension_semantics=sem, vmem_limit_bytes=VMEM_LIMIT)


def _sigmoid(x):
    return 1.0 / (1.0 + jnp.exp(-x))


def _dot_nt(a, b):
    return lax.dot_general(a, b, (((1,), (1,)), ((), ())), preferred_element_type=F32)


def _dot(a, b, **kw):
    return jnp.dot(a, b, preferred_element_type=F32, **kw)


def _split_dot(a_bf16_exact, x):
    hi = x.astype(BF16)
    lo = (x - hi.astype(F32)).astype(BF16)
    return _dot(a_bf16_exact, hi) + _dot(a_bf16_exact, lo)


def _mod_kernel(c_ref, w_ref, b_ref, o_ref):
    c = c_ref[...]
    cond = c * _sigmoid(c)
    o_ref[...] = _dot(cond, w_ref[...], precision=HIGHEST) + b_ref[...]


def _mod(c, ada_w, ada_b):
    b, d = c.shape
    rows = 8
    c_pad = jnp.zeros((rows, d), F32).at[:b].set(c)
    n = ada_w.shape[1]
    out = pl.pallas_call(
        _mod_kernel,
        grid=(n // d,),
        in_specs=[pl.BlockSpec((rows, d), lambda j: (0, 0)),
                  pl.BlockSpec((d, d), lambda j: (0, j)),
                  pl.BlockSpec((1, d), lambda j: (0, j))],
        out_specs=pl.BlockSpec((rows, d), lambda j: (0, j)),
        out_shape=jax.ShapeDtypeStruct((rows, n), F32),
        compiler_params=_cparams("parallel"),
        name="mod",
    )(c_pad, ada_w, ada_b.reshape(1, n))
    return out[:b]


def _head_rms(t, w):
    return t * lax.rsqrt(jnp.mean(t * t, axis=-1, keepdims=True) + RMS_EPS) * w


def _pos_digits(pos):
    lane = lax.broadcasted_iota(I32, pos.shape, 1)
    d0 = (lane == 0) | (lane == 3) | (lane == 6)
    d1 = (lane == 1) | (lane == 4) | (lane == 7)
    d2 = (lane == 2) | (lane == 5) | (lane == 8)
    dig = jnp.where(d0, pos >> 12, jnp.where(d1, (pos >> 6) & 63, jnp.where(d2, pos & 63, 0)))
    return dig.astype(F32)


def _inproj_kernel(x_ref, sc_ref, sh_ref, n1_ref, w_ref, qnw_ref, knw_ref, lbp_ref, qaug_ref,
                   q_ref, kcr_ref, vcr_ref, ks_ref, vst_ref, kw_ref, vwt_ref, gt_ref,
                   hq_ref, hk_ref, hlf_ref, hv_ref, hg_ref):
    x = x_ref[0]
    ms = jnp.mean(x * x, axis=-1, keepdims=True)
    h = x * lax.rsqrt(ms + RMS_EPS) * n1_ref[...] * (1.0 + sc_ref[0]) + sh_ref[0]
    p = _dot(h.astype(BF16), w_ref[...])
    tm = x.shape[0]

    qnw = qnw_ref[...]
    for hd in range(NSA_HEADS):
        t = p[:, hd * HEAD_DIM:(hd + 1) * HEAD_DIM]
        qn = _head_rms(t, qnw) * (HEAD_DIM ** -0.5 * LOG2E)
        qa = jnp.broadcast_to(qaug_ref[hd:hd + 1, :], (tm, HEAD_DIM))
        q_ref[0, hd] = jnp.concatenate([qn, qa], axis=1).astype(BF16)
    kaug = _pos_digits(pl.program_id(1) * tm + lax.broadcasted_iota(I32, (tm, HEAD_DIM), 0))

    o = NSA_WIDTH
    kcr_ref[0] = p[:, o:o + KV_WIDTH]
    vcr_ref[0] = p[:, o + KV_WIDTH:o + 2 * KV_WIDTH]
    ks = p[:, o + 2 * KV_WIDTH:o + 3 * KV_WIDTH]
    vs = p[:, o + 3 * KV_WIDTH:o + 4 * KV_WIDTH]
    kw = p[:, o + 4 * KV_WIDTH:o + 5 * KV_WIDTH]
    vw = p[:, o + 5 * KV_WIDTH:o + 6 * KV_WIDTH]
    for g in range(KV_HEADS):
        sl = slice(g * HEAD_DIM, (g + 1) * HEAD_DIM)
        ks_ref[0, g] = jnp.concatenate([_head_rms(ks[:, sl], knw_ref[1:2, :]), kaug], axis=1).astype(BF16)
        kw_ref[0, g] = jnp.concatenate([_head_rms(kw[:, sl], knw_ref[2:3, :]), kaug], axis=1).astype(BF16)
    vst = vs.T.astype(BF16)
    vwt = vw.T.astype(BF16)
    for g in range(KV_HEADS):
        vst_ref[0, g] = vst[g * HEAD_DIM:(g + 1) * HEAD_DIM, :]
        vwt_ref[0, g] = vwt[g * HEAD_DIM:(g + 1) * HEAD_DIM, :]

    o = NSA_WIDTH + 6 * KV_WIDTH
    gates = _sigmoid(p[:, o:o + GATE_PAD])
    gt_ref[0] = gates.T[:NSA_HEADS * 3, :]

    o = o + GATE_PAD
    hq = p[:, o:o + HGRN_WIDTH]
    hf = p[:, o + HGRN_WIDTH:o + 2 * HGRN_WIDTH]
    hi = p[:, o + 2 * HGRN_WIDTH:o + 3 * HGRN_WIDTH]
    hg = p[:, o + 3 * HGRN_WIDTH:o + 4 * HGRN_WIDTH]
    lbp = lbp_ref[...]
    e = jnp.exp(lbp - jnp.max(lbp, axis=0, keepdims=True))
    lb = e[0:1, :] / jnp.sum(e, axis=0, keepdims=True)
    f = lb + (1.0 - lb) * _sigmoid(hf)
    hq_ref[0] = hq * _sigmoid(hq) * (HGRN_DIM ** -0.5)
    hk_ref[0] = 1.0 - f
    hlf_ref[0] = jnp.log(f)
    hv_ref[0] = hi
    hg_ref[0] = _sigmoid(hg)


def _inproj(x, sc1, sh1, norm1_w, w_cat, q_norm_w, k_norm_w, lb_param, tm):
    b, s, d = x.shape
    row = lambda bi, i: (bi, i, 0)
    per_b = lambda bi, i: (bi, 0, 0)
    fixed2 = lambda bi, i: (0, 0)
    aw = 2 * HEAD_DIM
    rest = np.array([2.0 ** (-8.0 * (i + 1) / NSA_HEADS) for i in range(NSA_HEADS)], np.float64) * LOG2E
    qaug = np.zeros((NSA_HEADS, HEAD_DIM), np.float32)
    for i in range(3):
        term = rest.astype(np.float32).astype(BF16).astype(np.float64)
        rest = rest - term
        for dgt, wgt in enumerate((4096.0, 64.0, 1.0)):
            qaug[:, 3 * i + dgt] = term * wgt
    assert np.all(qaug == qaug.astype(BF16).astype(np.float32))
    out_shape = (
        jax.ShapeDtypeStruct((b, NSA_HEADS, s, aw), BF16),
        jax.ShapeDtypeStruct((b, s, KV_WIDTH), F32),
        jax.ShapeDtypeStruct((b, s, KV_WIDTH), F32),
        jax.ShapeDtypeStruct((b, KV_HEADS, s, aw), BF16),
        jax.ShapeDtypeStruct((b, KV_HEADS, HEAD_DIM, s), BF16),
        jax.ShapeDtypeStruct((b, KV_HEADS, s, aw), BF16),
        jax.ShapeDtypeStruct((b, KV_HEADS, HEAD_DIM, s), BF16),
        jax.ShapeDtypeStruct((b, NSA_HEADS * 3, s), F32),
    ) + tuple(jax.ShapeDtypeStruct((b, s, HGRN_WIDTH), F32) for _ in range(5))
    hm = lambda n, w: pl.BlockSpec((1, n, tm, w), lambda bi, i: (bi, 0, i, 0))
    hmt = lambda n, w: pl.BlockSpec((1, n, w, tm), lambda bi, i: (bi, 0, 0, i))
    out_specs = (
        hm(NSA_HEADS, aw),
        pl.BlockSpec((1, tm, KV_WIDTH), row),
        pl.BlockSpec((1, tm, KV_WIDTH), row),
        hm(KV_HEADS, aw), hmt(KV_HEADS, HEAD_DIM),
        hm(KV_HEADS, aw), hmt(KV_HEADS, HEAD_DIM),
        pl.BlockSpec((1, NSA_HEADS * 3, tm), lambda bi, i: (bi, 0, i)),
    ) + tuple(pl.BlockSpec((1, tm, HGRN_WIDTH), row) for _ in range(5))
    return pl.pallas_call(
        _inproj_kernel,
        grid=(b, s // tm),
        in_specs=[pl.BlockSpec((1, tm, d), row),
                  pl.BlockSpec((1, 1, d), per_b),
                  pl.BlockSpec((1, 1, d), per_b),
                  pl.BlockSpec((1, d), fixed2),
                  pl.BlockSpec((d, PROJ_COLS), fixed2),
                  pl.BlockSpec((1, HEAD_DIM), fixed2),
                  pl.BlockSpec((3, HEAD_DIM), fixed2),
                  pl.BlockSpec(lb_param.shape, fixed2),
                  pl.BlockSpec((NSA_HEADS, HEAD_DIM), fixed2)],
        out_specs=out_specs,
        out_shape=out_shape,
        compiler_params=_cparams("parallel", "parallel"),
        name="inproj",
    )(x, sc1, sh1, norm1_w, w_cat, q_norm_w, k_norm_w, lb_param, jnp.asarray(qaug))


def _gelu_tanh(x):
    return 0.5 * x * (1.0 + jnp.tanh(0.7978845608028654 * (x + 0.044715 * x * x * x)))


def _compress_kernel(kch_ref, vch_ref, pos_ref, wa_ref, wb_ref, b1_ref, w2_ref, knw_ref,
                     kc_ref, vct_ref):
    n = kch_ref.shape[1]
    outs = []
    for br, ch_ref in enumerate((kch_ref, vch_ref)):
        ch = ch_ref[0]
        a = _dot((ch + pos_ref[br, 0:1, :]).astype(BF16), wa_ref[br])
        bm = _dot((ch + pos_ref[br, 1:2, :]).astype(BF16), wb_ref[br])
        pre = a + pltpu.roll(bm, n - 1, 0) + b1_ref[br]
        hid = _gelu_tanh(pre).astype(BF16)
        outs.append([_dot(hid[:, g * CMP_HIDDEN:(g + 1) * CMP_HIDDEN], w2_ref[br]) for g in range(KV_HEADS)])
    end_digits = _pos_digits(lax.broadcasted_iota(I32, (n, HEAD_DIM), 0) * CMP_STRIDE + (CMP_BLOCK - 1))
    for g in range(KV_HEADS):
        kc_ref[0, g] = jnp.concatenate([_head_rms(outs[0][g], knw_ref[0:1, :]), end_digits], axis=1).astype(BF16)
    vct = jnp.concatenate(outs[1], axis=1).T.astype(BF16)
    for g in range(KV_HEADS):
        vct_ref[0, g] = vct[g * HEAD_DIM:(g + 1) * HEAD_DIM, :]


def _compress(kc_raw, vc_raw, cmp_pos, cmp_w1, cmp_b1, cmp_w2, k_norm_w):
    b, s, _ = kc_raw.shape
    n = s // CMP_STRIDE
    half = CMP_STRIDE
    cw = CMP_STRIDE * KV_WIDTH
    kch = kc_raw.reshape(b, n, cw)
    vch = vc_raw.reshape(b, n, cw)
    pos = cmp_pos.reshape(2, 2, half, 1, HEAD_DIM)
    pos = jnp.broadcast_to(pos, (2, 2, half, KV_HEADS, HEAD_DIM)).reshape(2, 2, cw)
    w1 = cmp_w1.reshape(2, 2, half, HEAD_DIM, CMP_HIDDEN)
    eye = jnp.eye(KV_HEADS, dtype=F32)
    wfull = jnp.einsum('rhjdn,gk->rhjgdkn', w1, eye).reshape(2, 2, cw, KV_HEADS * CMP_HIDDEN).astype(BF16)
    b1 = jnp.tile(cmp_b1.reshape(2, 1, CMP_HIDDEN), (1, 1, KV_HEADS))
    fix = lambda r: (lambda bi: (0,) * r)
    return pl.pallas_call(
        _compress_kernel,
        grid=(b,),
        in_specs=[pl.BlockSpec((1, n, cw), lambda bi: (bi, 0, 0)),
                  pl.BlockSpec((1, n, cw), lambda bi: (bi, 0, 0)),
                  pl.BlockSpec((2, 2, cw), fix(3)),
                  pl.BlockSpec((2, cw, KV_HEADS * CMP_HIDDEN), fix(3)),
                  pl.BlockSpec((2, cw, KV_HEADS * CMP_HIDDEN), fix(3)),
                  pl.BlockSpec((2, 1, KV_HEADS * CMP_HIDDEN), fix(3)),
                  pl.BlockSpec((2, CMP_HIDDEN, HEAD_DIM), fix(3)),
                  pl.BlockSpec((3, HEAD_DIM), fix(2))],
        out_specs=(pl.BlockSpec((1, KV_HEADS, n, 2 * HEAD_DIM), lambda bi: (bi, 0, 0, 0)),
                   pl.BlockSpec((1, KV_HEADS, HEAD_DIM, n), lambda bi: (bi, 0, 0, 0))),
        out_shape=(jax.ShapeDtypeStruct((b, KV_HEADS, n, 2 * HEAD_DIM), BF16),
                   jax.ShapeDtypeStruct((b, KV_HEADS, HEAD_DIM, n), BF16)),
        compiler_params=_cparams("parallel"),
        name="compress",
    )(kch, vch, pos, wfull[:, 0], wfull[:, 1], b1, cmp_w2.astype(BF16), k_norm_w)


def _nsa_kernel(q_ref, kc_ref, vct_ref, ks_ref, vst_ref, kw_ref, vwt_ref, gt_ref, cdiff_ref, wdiff_ref,
                ovl_ref, oh_ref, onw_ref, wmask_ref, o_ref, buf_a, buf_b, m_scr, acc_scr, oc_scr, bias_scr,
                lst, cnt, *, n_top):
    q0 = pl.program_id(2) * TQ
    ncols = HEADS_PER_KV * TQ
    q = q_ref[0].reshape(ncols, 2 * HEAD_DIM)
    ns = ovl_ref.shape[0]

    def compress_and_select(nk, nb):
        s = jnp.where(cdiff_ref[0:nk, :] <= q0, _dot_nt(kc_ref[0, 0, 0:nk, :], q), -jnp.inf)
        m = jnp.max(s, axis=0, keepdims=True)
        m = jnp.where(m == -jnp.inf, 0.0, m)
        e = jnp.exp2(s - m)
        p = e / jnp.maximum(jnp.sum(e, axis=0, keepdims=True), 1e-30)
        oc_scr[...] = _dot(vct_ref[0, 0, :, 0:nk], p.astype(BF16))

        psum = p[:, 0:TQ]
        for hh in range(1, HEADS_PER_KV):
            psum = psum + p[:, hh * TQ:(hh + 1) * TQ]
        imp = _split_dot(ovl_ref[0:nb, 0:nk], psum)
        blk = lax.broadcasted_iota(I32, (nb, TQ), 0)
        tq = q0 + lax.broadcasted_iota(I32, (nb, TQ), 1)
        cur = tq >> 6
        forced = (blk == 0) | (blk == cur) | (blk == cur - 1)
        rank = jnp.where(forced, BIG, jnp.where(blk * SEL_BLOCK <= tq, imp, -BIG))
        blkf = blk.astype(F32)

        bias = jnp.full((nb, TQ), -1e30, F32)
        for _ in range(min(n_top, nb)):
            mx = jnp.max(rank, axis=0, keepdims=True)
            first = jnp.min(jnp.where(rank == mx, blkf, float(nb)), axis=0, keepdims=True)
            hit = blkf == first
            rank = jnp.where(hit, -jnp.inf, rank)
            bias = jnp.where(hit, 0.0, bias)
        bias_scr[...] = jnp.full((128, TQ), -1e30, F32)
        bias_scr[0:nb, :] = jnp.where(blk == 0, -1e30, bias)

    nc = kc_ref.shape[2]
    quarter = (q0 + TQ - 1) // (ks_ref.shape[2] // NSA_SIZE_VARIANTS)
    for v in range(NSA_SIZE_VARIANTS):
        @pl.when(quarter == v)
        def _():
            compress_and_select(nc * (v + 1) // NSA_SIZE_VARIANTS, ns * (v + 1) // NSA_SIZE_VARIANTS)

    o_c = oc_scr[...]
    bias = bias_scr[...]

    bias_t = bias.T.astype(BF16)
    qq = jnp.concatenate([q, jnp.concatenate([bias_t] * HEADS_PER_KV, axis=0)], axis=1)
    ones_rows = jnp.ones((16, TK), BF16)

    def scores(j):
        k0 = pl.multiple_of(j * TK, TK)
        kk = jnp.concatenate([ks_ref[0, 0, pl.ds(k0, TK), :], oh_ref[pl.ds(k0, TK), :]], axis=1)
        return _dot_nt(kk, qq)

    def consume(buf, j, causal, part):
        sc = buf[...]
        if causal:
            sc = jnp.where(wdiff_ref[0:TK, :] + (q0 - j * TK) >= 0, sc, -1e30)
        k0 = pl.multiple_of(j * TK, TK)
        m_run = m_scr[part]
        m_new = jnp.maximum(m_run, jnp.max(sc, axis=0, keepdims=True))
        ex = jnp.exp2(sc - m_new).astype(BF16)
        va = jnp.concatenate([vst_ref[0, 0, :, pl.ds(k0, TK)], ones_rows], axis=0)
        acc_scr[part] = jnp.exp2(m_run - m_new) * acc_scr[part] + _dot(va, ex)
        m_scr[part] = m_new

    n_past = q0 // TK
    blocks_per_tile = TK // SEL_BLOCK
    cnt[0] = 0
    for j in range(ks_ref.shape[2] // TK):
        wanted = jnp.max(bias[j * blocks_per_tile:(j + 1) * blocks_per_tile, :]) == 0.0

        @pl.when(wanted & (j < n_past))
        def _():
            lst[cnt[0]] = j
            cnt[0] = cnt[0] + 1

    n_sel = cnt[0]
    lst[n_sel] = n_past

    buf_a[...] = scores(lst[0])

    s0 = jnp.where(wdiff_ref[0:SEL_BLOCK, :] + q0 >= 0, _dot_nt(ks_ref[0, 0, 0:SEL_BLOCK, :], q), -1e30)
    m0 = jnp.max(s0, axis=0, keepdims=True)
    v0 = jnp.concatenate([vst_ref[0, 0, :, 0:SEL_BLOCK], jnp.ones((16, SEL_BLOCK), BF16)], axis=0)
    m_scr[0] = m0
    acc_scr[0] = _dot(v0, jnp.exp2(s0 - m0).astype(BF16))
    m_scr[1] = jnp.full((1, ncols), -1e30, F32)
    acc_scr[1] = jnp.zeros((HEAD_DIM + 16, ncols), F32)

    nw = WINDOW + TQ
    start = pl.multiple_of(jnp.maximum(q0 - WINDOW, 0), TQ)
    sw = _dot_nt(kw_ref[0, 0, pl.ds(start, nw), :], q) + wmask_ref[0]
    ew = jnp.exp2(sw - jnp.max(sw, axis=0, keepdims=True))
    vw_aug = jnp.concatenate([vwt_ref[0, 0, :, pl.ds(start, nw)], jnp.ones((16, nw), BF16)], axis=0)
    acc_w = _dot(vw_aug, ew.astype(BF16))
    o_w = acc_w[0:HEAD_DIM, :] / acc_w[HEAD_DIM:HEAD_DIM + 1, :]

    def tiles(first, count):
        for u in range(0, count, 2):
            buf_b[...] = scores(lst[first + u + 1])
            consume(buf_a, lst[first + u], False, 0)
            buf_a[...] = scores(lst[first + u + 2])
            consume(buf_b, lst[first + u + 1], False, 1)
        return 0

    lax.fori_loop(0, n_sel // 4, lambda i, _: tiles(4 * i, 4), 0)
    lax.fori_loop(0, (n_sel // 2) % 2, lambda i, _: tiles((n_sel // 4) * 4, 2), 0)

    @pl.when(n_sel % 2 == 1)
    def _():
        buf_b[...] = scores(n_past)
        consume(buf_a, lst[n_sel - 1], False, 0)
        consume(buf_b, n_past, True, 1)

    @pl.when(n_sel % 2 == 0)
    def _():
        consume(buf_a, n_past, True, 0)

    m_all = jnp.maximum(m_scr[0], m_scr[1])
    acc_s = jnp.exp2(m_scr[0] - m_all) * acc_scr[0] + jnp.exp2(m_scr[1] - m_all) * acc_scr[1]
    o_s = acc_s[0:HEAD_DIM, :] / acc_s[HEAD_DIM:HEAD_DIM + 1, :]

    gt = gt_ref[0, 0]
    outs = []
    for hh in range(HEADS_PER_KV):
        cs = slice(hh * TQ, (hh + 1) * TQ)
        o = (gt[3 * hh:3 * hh + 1, :] * o_c[:, cs] + gt[3 * hh + 1:3 * hh + 2, :] * o_s[:, cs]
             + gt[3 * hh + 2:3 * hh + 3, :] * o_w[:, cs])
        o = o * lax.rsqrt(jnp.mean(o * o, axis=0, keepdims=True) + RMS_EPS) * onw_ref[0, hh]
        outs.append(o)
    o_ref[0] = jnp.concatenate(outs, axis=0).T


def _nsa(q, kc, vct, ks, vst, kw, vwt, gates_t, attn_out_norm_w):
    b, _, s, aw = q.shape
    nc = kc.shape[2]
    ns = s // SEL_BLOCK
    n_top = min(N_SELECT, ns)
    ncols = HEADS_PER_KV * TQ
    nw = WINDOW + TQ
    tl = np.arange(ncols)[None, :] & (TQ - 1)
    cdiff = jnp.asarray((np.arange(nc)[:, None] * CMP_STRIDE + (CMP_BLOCK - 1) - tl).astype(np.int32))
    wdiff_np = (tl - np.arange(nw)[:, None]).astype(np.int32)
    wdiff = jnp.asarray(wdiff_np)
    n_off = WINDOW // TQ + 1
    dist_np = wdiff_np[None] + (np.arange(n_off) * TQ)[:, None, None]
    wmask = jnp.asarray(np.where((dist_np >= 0) & (dist_np < WINDOW), 0.0, -np.inf).astype(np.float32))
    ci = np.arange(nc)[None, :] * CMP_STRIDE
    bj = np.arange(ns)[:, None]
    ovl = ((ci < (bj + 1) * SEL_BLOCK) & (ci + CMP_BLOCK > bj * SEL_BLOCK) & (np.arange(nc)[None, :] < nc - 1))
    ovl = jnp.asarray(ovl.astype(np.float32)).astype(BF16)
    assert ns <= 128
    onehot = (np.arange(s)[:, None] // SEL_BLOCK == np.arange(128)[None, :])
    onehot = jnp.asarray(onehot.astype(np.float32)).astype(BF16)
    onw = jnp.broadcast_to(attn_out_norm_w.reshape(KV_HEADS, HEADS_PER_KV, HEAD_DIM, 1),
                           (KV_HEADS, HEADS_PER_KV, HEAD_DIM, TQ))
    gt = gates_t.reshape(b, KV_HEADS, HEADS_PER_KV * 3, s)
    per_bg = lambda bi, g, i: (bi, g, 0, 0)
    fixed = lambda bi, g, i: (0, 0)
    return pl.pallas_call(
        functools.partial(_nsa_kernel, n_top=n_top),
        grid=(b, KV_HEADS, s // TQ),
        in_specs=[pl.BlockSpec((1, HEADS_PER_KV, TQ, aw), lambda bi, g, i: (bi, g, i, 0)),
                  pl.BlockSpec((1, 1, nc, aw), per_bg),
                  pl.BlockSpec((1, 1, HEAD_DIM, nc), per_bg),
                  pl.BlockSpec((1, 1, s, aw), per_bg),
                  pl.BlockSpec((1, 1, HEAD_DIM, s), per_bg),
                  pl.BlockSpec((1, 1, s, aw), per_bg),
                  pl.BlockSpec((1, 1, HEAD_DIM, s), per_bg),
                  pl.BlockSpec((1, 1, HEADS_PER_KV * 3, TQ), lambda bi, g, i: (bi, g, 0, i)),
                  pl.BlockSpec((nc, ncols), fixed, pipeline_mode=pl.Buffered(1)),
                  pl.BlockSpec((nw, ncols), fixed, pipeline_mode=pl.Buffered(1)),
                  pl.BlockSpec((ns, nc), fixed, pipeline_mode=pl.Buffered(1)),
                  pl.BlockSpec((s, 128), fixed, pipeline_mode=pl.Buffered(1)),
                  pl.BlockSpec((1, HEADS_PER_KV, HEAD_DIM, TQ), lambda bi, g, i: (g, 0, 0, 0)),
                  pl.BlockSpec((1, nw, ncols), lambda bi, g, i: (jnp.minimum(i, n_off - 1), 0, 0))],
        out_specs=pl.BlockSpec((1, TQ, HEADS_PER_KV * HEAD_DIM), lambda bi, g, i: (bi, i, g)),
        out_shape=jax.ShapeDtypeStruct((b, s, NSA_WIDTH), F32),
        scratch_shapes=[pltpu.VMEM((TK, ncols), F32), pltpu.VMEM((TK, ncols), F32),
                        pltpu.VMEM((2, 1, ncols), F32), pltpu.VMEM((2, HEAD_DIM + 16, ncols), F32),
                        pltpu.VMEM((HEAD_DIM, ncols), F32), pltpu.VMEM((128, TQ), F32),
                        pltpu.SMEM((s // TK + 1,), I32), pltpu.SMEM((1,), I32)],
        compiler_params=_cparams("parallel", "parallel", "arbitrary"),
        name="nsa",
    )(q, kc, vct, ks, vst, kw, vwt, gt, cdiff, wdiff, ovl, onehot, onw, wmask)


def _hgrn_kernel(q_ref, k_ref, lf_ref, v_ref, g_ref, onw_ref, cm_ref, o_ref, state_scr, *, n_chunks):
    c = HGRN_CHUNK

    @pl.when(pl.program_id(1) == 0)
    def _():
        state_scr[...] = jnp.zeros_like(state_scr)

    ri = lax.broadcasted_iota(I32, (c, c), 0)
    ci = lax.broadcasted_iota(I32, (c, c), 1)
    rsub = ri // HGRN_SUB
    diag = ri == ci

    def head_chunk(r0, hd, state_t):
        cols = slice(hd * HGRN_DIM, (hd + 1) * HGRN_DIM)
        q = q_ref[0, pl.ds(r0, c), cols]
        k = k_ref[0, pl.ds(r0, c), cols]
        lf = lf_ref[0, pl.ds(r0, c), cols] * LOG2E
        v = v_ref[0, pl.ds(r0, c), cols]
        cm = cm_ref[...]
        l1 = lf.astype(BF16)
        rest = lf - l1.astype(F32)
        l2 = rest.astype(BF16)
        l3 = (rest - l2.astype(F32)).astype(BF16)
        cum = _dot(cm, l1) + _dot(cm, l2) + _dot(cm, l3)
        o = _dot_nt((q * jnp.exp2(cum)).astype(BF16), state_t.astype(BF16))
        scores = jnp.where(diag, jnp.sum(q * k, axis=-1, keepdims=True), 0.0)

        def factored(ref, mask, acc):
            qs = q * jnp.exp2(jnp.minimum(cum - ref, 0.0))
            kd = k * jnp.exp2(jnp.minimum(ref - cum, 0.0))
            return jnp.where(mask, _dot_nt(qs.astype(BF16), kd.astype(BF16)), acc)

        for i in range(1, c // HGRN_SUB):
            scores = factored(cum[i * HGRN_SUB - 1:i * HGRN_SUB, :], (rsub == i) & (ci < i * HGRN_SUB), scores)
        for d in range(1, HGRN_SUB):
            ksh = pltpu.roll(k, d, 0)
            csh = pltpu.roll(cum, d, 0)
            w = jnp.sum(q * ksh * jnp.exp2(cum - csh), axis=-1, keepdims=True)
            scores = jnp.where((ri - ci == d) & ((ri & (HGRN_SUB - 1)) >= d), w, scores)
        o = o + _dot(scores.astype(BF16), v.astype(BF16))
        last = cum[c - 1:c, :]
        kd = (k * jnp.exp2(last - cum)).astype(BF16)
        state_t = state_t * jnp.exp2(last) + _dot(v.T.astype(BF16), kd)
        o = o * g_ref[0, pl.ds(r0, c), cols]
        o = o * lax.rsqrt(jnp.mean(o * o, axis=-1, keepdims=True) + RMS_EPS) * onw_ref[:, cols]
        o_ref[0, pl.ds(r0, c), cols] = o
        return state_t

    def chunk(ck, states):
        r0 = pl.multiple_of(ck * c, c)
        return tuple(head_chunk(r0, hd, states[hd]) for hd in range(HGRN_HEADS))

    states = lax.fori_loop(0, n_chunks, chunk, tuple(state_scr[hd] for hd in range(HGRN_HEADS)))
    for hd in range(HGRN_HEADS):
        state_scr[hd] = states[hd]


def _hgrn(hq, hk, hlf, hv, hg, rec_out_norm_w, rows):
    b, s, _ = hq.shape
    cm = jnp.asarray(np.tril(np.ones((HGRN_CHUNK, HGRN_CHUNK), np.float32))).astype(BF16)
    blk = pl.BlockSpec((1, rows, HGRN_WIDTH), lambda bi, i: (bi, i, 0))
    return pl.pallas_call(
        functools.partial(_hgrn_kernel, n_chunks=rows // HGRN_CHUNK),
        grid=(b, s // rows),
        in_specs=[blk, blk, blk, blk, blk,
                  pl.BlockSpec((1, HGRN_WIDTH), lambda bi, i: (0, 0)),
                  pl.BlockSpec(cm.shape, lambda bi, i: (0, 0))],
        out_specs=blk,
        out_shape=jax.ShapeDtypeStruct((b, s, HGRN_WIDTH), F32),
        scratch_shapes=[pltpu.VMEM((HGRN_HEADS, HGRN_DIM, HGRN_DIM), F32)],
        compiler_params=_cparams("parallel", "arbitrary"),
        name="hgrn",
    )(hq, hk, hlf, hv, hg, rec_out_norm_w.reshape(1, HGRN_WIDTH), cm)


def _outproj_kernel(x_ref, a_ref, r_ref, wa_ref, wr_ref, gt_ref, sc_ref, sh_ref, n2_ref, x1_ref, h2_ref, h2p_ref):
    mixed = _dot(a_ref[0].astype(BF16), wa_ref[...]) + _dot(r_ref[0].astype(BF16), wr_ref[...])
    x1 = x_ref[0] + gt_ref[0] * mixed
    x1_ref[0] = x1
    ms = jnp.mean(x1 * x1, axis=-1, keepdims=True)
    h2 = x1 * lax.rsqrt(ms + RMS_EPS) * n2_ref[...] * (1.0 + sc_ref[0]) + sh_ref[0]
    h2_ref[0] = h2
    h2p_ref[0] = _pack_bf16_pair(h2[:, :D_MODEL // 2], h2[:, D_MODEL // 2:])


def _outproj(x, attn, rec, w_out, gt1, sc2, sh2, norm2_w, tm):
    b, s, d = x.shape
    row = lambda bi, i: (bi, i, 0)
    per_b = lambda bi, i: (bi, 0, 0)
    fixed2 = lambda bi, i: (0, 0)
    w = w_out.astype(BF16)
    return pl.pallas_call(
        _outproj_kernel,
        grid=(b, s // tm),
        in_specs=[pl.BlockSpec((1, tm, d), row),
                  pl.BlockSpec((1, tm, NSA_WIDTH), row),
                  pl.BlockSpec((1, tm, HGRN_WIDTH), row),
                  pl.BlockSpec((NSA_WIDTH, d), fixed2),
                  pl.BlockSpec((HGRN_WIDTH, d), fixed2),
                  pl.BlockSpec((1, 1, d), per_b),
                  pl.BlockSpec((1, 1, d), per_b),
                  pl.BlockSpec((1, 1, d), per_b),
                  pl.BlockSpec((1, d), fixed2)],
        out_specs=(pl.BlockSpec((1, tm, d), row), pl.BlockSpec((1, tm, d), row), pl.BlockSpec((1, tm, d // 2), row)),
        out_shape=(jax.ShapeDtypeStruct((b, s, d), F32), jax.ShapeDtypeStruct((b, s, d), F32),
                   jax.ShapeDtypeStruct((b, s, d // 2), jnp.uint32)),
        compiler_params=_cparams("parallel", "parallel"),
        name="outproj",
    )(x, attn, rec, w[:NSA_WIDTH], w[NSA_WIDTH:], gt1, sc2, sh2, norm2_w)


def _mixer(x, c, ada_w, ada_b, norm1_w, norm2_w, w_in, q_norm_w, k_norm_w, cmp_pos, cmp_w1, cmp_b1, cmp_w2,
           attn_out_norm_w, hgrn_lb_param, rec_out_norm_w, w_out):
    b, s, d = x.shape
    mod = _mod(c, ada_w, ada_b)
    sh1, sc1, gt1, sh2, sc2, gt2 = [m.reshape(b, 1, d) for m in jnp.split(mod, 6, axis=-1)]
    o = NSA_WIDTH + 6 * KV_WIDTH
    w_cat = jnp.concatenate([w_in[:, :o], w_in[:, o:o + NSA_HEADS * 3],
                             jnp.zeros((d, GATE_PAD - NSA_HEADS * 3), w_in.dtype),
                             w_in[:, o + NSA_HEADS * 3:]], axis=1).astype(BF16)
    tm = min(256, s)
    (q, kc_raw, vc_raw, ks, vst, kw, vwt, gates_t, hq, hk, hlf, hv, hg) = _inproj(
        x, sc1, sh1, norm1_w.reshape(1, d), w_cat, q_norm_w.reshape(1, HEAD_DIM), k_norm_w, hgrn_lb_param, tm)
    kc, vct = _compress(kc_raw, vc_raw, cmp_pos, cmp_w1, cmp_b1, cmp_w2, k_norm_w)
    attn = _nsa(q, kc, vct, ks, vst, kw, vwt, gates_t, attn_out_norm_w)
    rec = _hgrn(hq, hk, hlf, hv, hg, rec_out_norm_w, min(512, s))
    x1, h2, h2p = _outproj(x, attn, rec, w_out, gt1, sc2, sh2, norm2_w.reshape(1, d), tm)
    return x1, h2, h2p, gt2


def _router_kernel(h_ref, rwt_ref, bias_ref, tri_ref, ones_ref, idx_ref, w_ref, rank_ref, cnt_ref, carry_scr, *, tr):
    @pl.when(pl.program_id(0) == 0)
    def _():
        carry_scr[...] = jnp.zeros_like(carry_scr)

    h = h_ref[...]
    h_hi = h.astype(BF16)
    h_lo = (h - h_hi.astype(F32)).astype(BF16)
    logits = _dot_nt(rwt_ref[0], h_hi) + _dot_nt(rwt_ref[1], h_hi) + _dot_nt(rwt_ref[0], h_lo)
    scores = _sigmoid(logits)
    biased = scores + bias_ref[...]
    neg = -jnp.inf

    gs = []
    for g in range(N_GROUPS):
        sub = biased[g * GROUP_SIZE:(g + 1) * GROUP_SIZE, :]
        m1 = jnp.max(sub, axis=0, keepdims=True)
        dup = jnp.sum((sub == m1).astype(F32), axis=0, keepdims=True)
        m2 = jnp.max(jnp.where(sub < m1, sub, neg), axis=0, keepdims=True)
        gs.append(m1 + jnp.where(dup >= 2.0, m1, m2))
    parts = []
    for g in range(N_GROUPS):
        beaten = jnp.zeros_like(gs[g])
        for g2 in range(N_GROUPS):
            if g2 != g:
                beats = (gs[g2] >= gs[g]) if g2 < g else (gs[g2] > gs[g])
                beaten = beaten + beats.astype(F32)
        sub = biased[g * GROUP_SIZE:(g + 1) * GROUP_SIZE, :]
        parts.append(jnp.where(beaten < float(TOPK_GROUPS), sub, neg))
    cand = jnp.concatenate(parts, axis=0)

    rowf = lax.broadcasted_iota(I32, (N_EXPERTS, tr), 0).astype(F32)
    idx_rows, w_rows, hits = [], [], []
    multi = jnp.zeros((N_EXPERTS, tr), F32)
    for _ in range(TOP_K):
        mx = jnp.max(cand, axis=0, keepdims=True)
        first = jnp.min(jnp.where(cand == mx, rowf, float(N_EXPERTS)), axis=0, keepdims=True)
        hit = rowf == first
        idx_rows.append(first)
        w_rows.append(jnp.sum(jnp.where(hit, scores, 0.0), axis=0, keepdims=True))
        cand = jnp.where(hit, neg, cand)
        multi = jnp.where(hit, 1.0, multi)
    w = jnp.concatenate(w_rows, axis=0)
    w_ref[...] = w / jnp.sum(w, axis=0, keepdims=True) * ROUTED_SCALE
    idx = jnp.concatenate(idx_rows, axis=0)
    idx_ref[...] = idx.astype(I32)

    carry = carry_scr[...]
    mb = multi.astype(BF16)
    before = _dot(mb, tri_ref[...]) + jnp.concatenate([carry] * (tr // 128), axis=1)
    rank_rows = [jnp.sum(jnp.where(rowf == idx_rows[k], before, 0.0), axis=0, keepdims=True) for k in range(TOP_K)]
    rank_ref[...] = jnp.concatenate(rank_rows, axis=0).astype(I32)
    carry = carry + _dot(mb, ones_ref[...])
    carry_scr[...] = carry
    cnt_ref[...] = carry


def _router(h2, router_w, router_bias, tr):
    t, d = h2.shape
    tri = jnp.asarray(np.triu(np.ones((tr, tr), np.float32), 1)).astype(BF16)
    ones = jnp.ones((tr, 128), BF16)
    tok = pl.BlockSpec((TOP_K, tr), lambda i: (0, i))
    fixed = lambda i: (0, 0)
    rwt = router_w.T
    rwt_hi = rwt.astype(BF16)
    rwt_split = jnp.stack([rwt_hi, (rwt - rwt_hi.astype(F32)).astype(BF16)])
    return pl.pallas_call(
        functools.partial(_router_kernel, tr=tr),
        grid=(t // tr,),
        in_specs=[pl.BlockSpec((tr, d), lambda i: (i, 0)),
                  pl.BlockSpec((2, N_EXPERTS, d), lambda i: (0, 0, 0)),
                  pl.BlockSpec((N_EXPERTS, 1), fixed),
                  pl.BlockSpec((tr, tr), fixed),
                  pl.BlockSpec((tr, 128), fixed)],
        out_specs=(tok, tok, tok, pl.BlockSpec((N_EXPERTS, 128), fixed)),
        out_shape=(jax.ShapeDtypeStruct((TOP_K, t), I32), jax.ShapeDtypeStruct((TOP_K, t), F32),
                   jax.ShapeDtypeStruct((TOP_K, t), I32), jax.ShapeDtypeStruct((N_EXPERTS, 128), F32)),
        scratch_shapes=[pltpu.VMEM((N_EXPERTS, 128), F32)],
        compiler_params=_cparams("arbitrary"),
        name="router",
    )(h2, rwt_split, router_bias.reshape(N_EXPERTS, 1), tri, ones)


def _pack_bf16_pair(a, b):
    ua = lax.bitcast_convert_type(a.astype(BF16).astype(F32), jnp.uint32)
    ub = lax.bitcast_convert_type(b.astype(BF16).astype(F32), jnp.uint32)
    return ua | (ub >> 16)


def _unpack_bf16_pair(w):
    a = lax.bitcast_convert_type(w & jnp.uint32(0xFFFF0000), F32)
    b = lax.bitcast_convert_type(w << 16, F32)
    return a, b


def _slot_kernel(ps_ref, idx_ref, rank_ref, slot_ref):
    idx = idx_ref[...]

    def body(e, acc):
        return jnp.where(idx == e, ps_ref[e], acc)

    slot_ref[...] = lax.fori_loop(0, N_EXPERTS, body, jnp.zeros_like(idx)) + rank_ref[...]


def _slots(pad_start, idx, rank, tt):
    t = idx.shape[1]
    tok = pl.BlockSpec((TOP_K, tt), lambda i, ps: (0, i))
    return pl.pallas_call(
        _slot_kernel,
        grid_spec=pltpu.PrefetchScalarGridSpec(num_scalar_prefetch=1, grid=(t // tt,),
                                               in_specs=[tok, tok], out_specs=tok),
        out_shape=jax.ShapeDtypeStruct((TOP_K, t), I32),
        compiler_params=_cparams("parallel"),
        name="slots",
    )(pad_start, idx, rank)


SC_CORES = 2
SC_SUBCORES = 16
SC_CHUNK = 64


def _sc_mesh():
    return plsc.VectorSubcoreMesh(core_axis_name="c", subcore_axis_name="s")


def _sc_dispatch(h2p, slot_chunks, n_rows):
    t, dw = h2p.shape
    per = slot_chunks.shape[0] // (SC_CORES * SC_SUBCORES)

    def body(h_hbm, slot_hbm, xs_hbm, idx_v, rows_v, sem):
        wid = lax.axis_index("s") * SC_CORES + lax.axis_index("c")

        @pl.loop(0, per)
        def _(c):
            ch = wid * per + c
            pltpu.sync_copy(slot_hbm.at[ch], idx_v)
            pltpu.sync_copy(h_hbm.at[pl.ds(ch * SC_CHUNK, SC_CHUNK)], rows_v)
            copies = [pltpu.async_copy(rows_v, xs_hbm.at[idx_v.at[k]], sem) for k in range(TOP_K)]
            for cp in copies:
                cp.wait()

    return pl.kernel(
        body, out_type=jax.ShapeDtypeStruct((n_rows, dw), h2p.dtype), mesh=_sc_mesh(),
        scratch_types=[pltpu.VMEM((TOP_K, SC_CHUNK), I32), pltpu.VMEM((SC_CHUNK, dw), h2p.dtype),
                       pltpu.SemaphoreType.DMA],
    )(h2p, slot_chunks)


def _sc_gather(ys, slot_chunks, t):
    dw = ys.shape[1]
    per = slot_chunks.shape[0] // (SC_CORES * SC_SUBCORES)

    def body(ys_hbm, slot_hbm, yg_hbm, idx_v, rows_v, gsem, wsem):
        wid = lax.axis_index("s") * SC_CORES + lax.axis_index("c")

        @pl.loop(0, per)
        def _(c):
            ch = wid * per + c
            pltpu.sync_copy(slot_hbm.at[ch], idx_v)
            gathers = [None] * TOP_K
            writes = [None] * TOP_K
            gathers[0] = pltpu.async_copy(ys_hbm.at[idx_v.at[0]], rows_v.at[0], gsem)
            for k in range(TOP_K):
                gathers[k].wait()
                if k + 1 < TOP_K:
                    if k >= 1:
                        writes[k - 1].wait()
                    gathers[k + 1] = pltpu.async_copy(ys_hbm.at[idx_v.at[k + 1]], rows_v.at[(k + 1) % 2], gsem)
                writes[k] = pltpu.async_copy(rows_v.at[k % 2], yg_hbm.at[k, pl.ds(ch * SC_CHUNK, SC_CHUNK)], wsem)
            writes[TOP_K - 2].wait()
            writes[TOP_K - 1].wait()

    return pl.kernel(
        body, out_type=jax.ShapeDtypeStruct((TOP_K, t, dw), ys.dtype), mesh=_sc_mesh(),
        scratch_types=[pltpu.VMEM((TOP_K, SC_CHUNK), I32), pltpu.VMEM((2, SC_CHUNK, dw), ys.dtype),
                       pltpu.SemaphoreType.DMA, pltpu.SemaphoreType.DMA],
    )(ys, slot_chunks)


def _experts_kernel(be_ref, nu_ref, bv_ref, run_ref, xs_hbm, wg_hbm, wu_hbm, wd_hbm, ys_ref,
                    xring, rsem, gring, uring, dring, wsem):
    i = pl.program_id(0)
    half = D_MODEL // 2
    n_used = nu_ref[0]
    n_steps = pl.num_programs(0)

    def weight_copies(blk):
        ex = be_ref[blk]
        slot = run_ref[blk] % EXPERT_RING
        return [pltpu.make_async_copy(src.at[ex], ring.at[slot], wsem.at[a, slot])
                for a, (src, ring) in enumerate(((wg_hbm, gring), (wu_hbm, uring), (wd_hbm, dring)))]

    def starts_run(blk):
        return run_ref[blk] != run_ref[jnp.maximum(blk - 1, 0)]

    @pl.when(i == 0)
    def _():
        for cp in weight_copies(jnp.int32(0)):
            cp.start()

        @pl.when((n_steps > 1) & starts_run(jnp.int32(1)))
        def _():
            for cp in weight_copies(jnp.int32(1)):
                cp.start()

    ahead = jnp.minimum(i + (EXPERT_RING - 1), n_steps - 1)

    @pl.when((i + (EXPERT_RING - 1) < n_steps) & starts_run(ahead))
    def _():
        for cp in weight_copies(ahead):
            cp.start()

    @pl.when((i == 0) | starts_run(i))
    def _():
        for cp in weight_copies(i):
            cp.wait()

    wslot = run_ref[i] % EXPERT_RING
    wg_ref, wu_ref, wd_ref = gring.at[wslot], uring.at[wslot], dring.at[wslot]

    def fetch(blk):
        slot = blk % EXPERT_RING
        return pltpu.make_async_copy(xs_hbm.at[pl.ds(pl.multiple_of(blk * EXPERT_BLOCK, EXPERT_BLOCK), EXPERT_BLOCK)],
                                     xring.at[slot], rsem.at[slot])

    @pl.when(i == 0)
    def _():
        for first in range(EXPERT_RING - 1):
            @pl.when(first < n_used)
            def _():
                fetch(jnp.int32(first)).start()

    @pl.when(i + (EXPERT_RING - 1) < n_used)
    def _():
        fetch(i + (EXPERT_RING - 1)).start()

    @pl.when(i < n_used)
    def _():
        fetch(i).wait()

    xs_ref = xring.at[i % EXPERT_RING]

    def ffn(rows):
        live = lax.broadcasted_iota(I32, (rows, xs_ref.shape[1]), 0) < bv_ref[i]
        xa, xb = _unpack_bf16_pair(jnp.where(live, xs_ref[0:rows, :], jnp.uint32(0)))
        xa, xb = xa.astype(BF16), xb.astype(BF16)
        g = _dot(xa, wg_ref[:half, :].astype(BF16)) + _dot(xb, wg_ref[half:, :].astype(BF16))
        u = _dot(xa, wu_ref[:half, :].astype(BF16)) + _dot(xb, wu_ref[half:, :].astype(BF16))
        act = (g * _sigmoid(g) * u).astype(BF16)
        y = _dot(act, wd_ref[...].astype(BF16))
        ys_ref[0:rows, :] = _pack_bf16_pair(y[:, :half], y[:, half:])

    used = i < n_used
    short = bv_ref[i] <= EXPERT_TAIL

    @pl.when(used & jnp.logical_not(short))
    def _():
        ffn(EXPERT_BLOCK)

    @pl.when(used & short)
    def _():
        ffn(EXPERT_TAIL)
        ys_ref[EXPERT_TAIL:, :] = jnp.zeros((EXPERT_BLOCK - EXPERT_TAIL, ys_ref.shape[1]), ys_ref.dtype)

    @pl.when(jnp.logical_not(used))
    def _():
        ys_ref[...] = jnp.zeros_like(ys_ref)


def _experts(xs, blk_e, n_used, blk_valid, w_gate, w_up, w_down):
    n_rows, dw = xs.shape
    d = w_gate.shape[1]
    nblk = n_rows // EXPERT_BLOCK
    blk_run = jnp.cumsum(jnp.concatenate([jnp.zeros((1,), I32), (blk_e[1:] != blk_e[:-1]).astype(I32)])).astype(I32)
    hbm = pl.BlockSpec(memory_space=pl.ANY)
    return pl.pallas_call(
        _experts_kernel,
        grid_spec=pltpu.PrefetchScalarGridSpec(
            num_scalar_prefetch=4,
            grid=(nblk,),
            in_specs=[hbm, hbm, hbm, hbm],
            out_specs=pl.BlockSpec((EXPERT_BLOCK, dw), lambda i, be, nu, bv, rn: (i, 0)),
            scratch_shapes=[pltpu.VMEM((EXPERT_RING, EXPERT_BLOCK, dw), xs.dtype),
                            pltpu.SemaphoreType.DMA((EXPERT_RING,)),
                            pltpu.VMEM((EXPERT_RING, d, EXPERT_FF), w_gate.dtype),
                            pltpu.VMEM((EXPERT_RING, d, EXPERT_FF), w_up.dtype),
                            pltpu.VMEM((EXPERT_RING, EXPERT_FF, d), w_down.dtype),
                            pltpu.SemaphoreType.DMA((3, EXPERT_RING))]),
        out_shape=jax.ShapeDtypeStruct((n_rows, dw), xs.dtype),
        compiler_params=pltpu.CompilerParams(dimension_semantics=("arbitrary",), vmem_limit_bytes=VMEM_LIMIT,
                                             has_side_effects=True),
        name="experts",
    )(blk_e, n_used, blk_valid, blk_run, xs, w_gate, w_up, w_down)


def _combine_kernel(x1_ref, h_ref, w_ref, gt_ref, sg_ref, su_ref, sd_ref, yg_ref, o_ref):
    tc = x1_ref.shape[0]
    half = D_MODEL // 2
    hb = h_ref[...].astype(BF16)
    g = _dot(hb, sg_ref[...])
    u = _dot(hb, su_ref[...])
    ffn = _dot((g * _sigmoid(g) * u).astype(BF16), sd_ref[...])

    w = w_ref[...]
    ra = jnp.zeros((tc, half), F32)
    rb = jnp.zeros((tc, half), F32)
    for k in range(TOP_K):
        ya, yb = _unpack_bf16_pair(yg_ref[k])
        ra = ra + w[:, k:k + 1] * ya
        rb = rb + w[:, k:k + 1] * yb
    ffn = ffn + jnp.concatenate([ra, rb], axis=1)
    o_ref[...] = x1_ref[...] + gt_ref[0] * ffn


def _combine(x1, h2, w_tok, gt2, yg, sg, su, sd, seq, tc):
    t, d = x1.shape
    row = lambda i: (i, 0)
    fixed = lambda i: (0, 0)
    return pl.pallas_call(
        _combine_kernel,
        grid=(t // tc,),
        in_specs=[pl.BlockSpec((tc, d), row),
                  pl.BlockSpec((tc, d), row),
                  pl.BlockSpec((tc, TOP_K), row),
                  pl.BlockSpec((1, 1, d), lambda i: ((i * tc) // seq, 0, 0)),
                  pl.BlockSpec((d, SHARED_FF), fixed),
                  pl.BlockSpec((d, SHARED_FF), fixed),
                  pl.BlockSpec((SHARED_FF, d), fixed),
                  pl.BlockSpec((TOP_K, tc, d // 2), lambda i: (0, i, 0))],
        out_specs=pl.BlockSpec((tc, d), row),
        out_shape=jax.ShapeDtypeStruct((t, d), F32),
        compiler_params=_cparams("parallel"),
        name="combine",
    )(x1, h2, w_tok, gt2, sg.astype(BF16), su.astype(BF16), sd.astype(BF16), yg)


def _moe_parts(x1, h2, h2p, gt2, router_w, router_bias, w_gate, w_up, w_down, sg, su, sd):
    b, s, d = x1.shape
    t = b * s
    assert t % (SC_CHUNK * SC_CORES * SC_SUBCORES) == 0, "token chunks must split evenly over the vector subcores"
    h2 = h2.reshape(t, d)
    idx, w, rank, cnt = _router(h2, router_w, router_bias, min(256, t))
    counts = cnt[:, 0].astype(I32)
    padded = (counts + EXPERT_BLOCK - 1) // EXPERT_BLOCK * EXPERT_BLOCK
    pad_end = jnp.cumsum(padded)
    pad_start = pad_end - padded
    n_rows = t * TOP_K + N_EXPERTS * EXPERT_BLOCK
    nblk = n_rows // EXPERT_BLOCK
    n_used = (pad_end[-1:] // EXPERT_BLOCK).astype(I32)
    blk_start = jnp.arange(nblk, dtype=I32) * EXPERT_BLOCK
    owns = (pad_start[None, :] <= blk_start[:, None]) & (blk_start[:, None] < pad_end[None, :])
    e_ids = jnp.arange(N_EXPERTS, dtype=I32)[None, :]
    last_e = jnp.max(jnp.where(counts > 0, e_ids[0], 0))
    blk_e = jnp.where(blk_start < pad_end[-1], jnp.sum(jnp.where(owns, e_ids, 0), axis=1), last_e).astype(I32)
    rows_left = jnp.sum(jnp.where(owns, (pad_start + counts)[None, :] - blk_start[:, None], 0), axis=1)
    blk_valid = jnp.clip(rows_left, 0, EXPERT_BLOCK).astype(I32)
    slot = _slots(pad_start.astype(I32), idx, rank, min(2048, t))
    slot_chunks = slot.reshape(TOP_K, t // SC_CHUNK, SC_CHUNK).transpose(1, 0, 2)
    xs = _sc_dispatch(h2p.reshape(t, d // 2), slot_chunks, n_rows)
    ys = _experts(xs, blk_e, n_used, blk_valid, w_gate, w_up, w_down)
    yg = _sc_gather(ys, slot_chunks, t)
    out = _combine(x1.reshape(t, d), h2, w.T, gt2, yg, sg, su, sd, s, min(256, t))
    return out.reshape(b, s, d), dict(idx=idx, w=w, rank=rank, cnt=cnt)


def kernel(x, c, ada_w, ada_b, norm1_w, norm2_w, w_in, q_norm_w, k_norm_w, cmp_pos, cmp_w1, cmp_b1, cmp_w2, attn_out_norm_w, hgrn_lb_param, rec_out_norm_w, w_out, router_w, router_bias, exp_w_gate, exp_w_up, exp_w_down, shared_w_gate, shared_w_up, shared_w_down):
    assert ada_w.shape[0] == 1, "one layer"
    assert x.shape[0] <= 8 and x.shape[1] % TK == 0 and x.shape[1] >= WINDOW + TQ
    l = 0
    x1, h2, h2p, gt2 = _mixer(x, c, ada_w[l], ada_b[l], norm1_w[l], norm2_w[l], w_in[l], q_norm_w[l], k_norm_w[l],
                         cmp_pos[l], cmp_w1[l], cmp_b1[l], cmp_w2[l], attn_out_norm_w[l], hgrn_lb_param,
                         rec_out_norm_w[l], w_out[l])
    out, _ = _moe_parts(x1, h2, h2p, gt2, router_w[l], router_bias[l], exp_w_gate[l], exp_w_up[l], exp_w_down[l],
                        shared_w_gate[l], shared_w_up[l], shared_w_down[l])
    return out
```

```python
import functools

import numpy as np
import jax
import jax.numpy as jnp
from jax import lax
from jax.experimental import pallas as pl
from jax.experimental.pallas import tpu as pltpu
from jax.experimental.pallas import tpu_sc as plsc

F32 = jnp.float32
BF16 = jnp.bfloat16
I32 = jnp.int32

D_MODEL = 1024
NSA_HEADS = 8
HEAD_DIM = 64
NSA_WIDTH = NSA_HEADS * HEAD_DIM
KV_HEADS = 2
HEADS_PER_KV = NSA_HEADS // KV_HEADS
KV_WIDTH = KV_HEADS * HEAD_DIM
CMP_BLOCK = 32
CMP_STRIDE = 16
CMP_HIDDEN = 256
SEL_BLOCK = 64
N_SELECT = 16
WINDOW = 512
HGRN_HEADS = 4
HGRN_DIM = 128
HGRN_WIDTH = HGRN_HEADS * HGRN_DIM
HGRN_CHUNK = 64
HGRN_SUB = 16
N_EXPERTS = 256
TOP_K = 8
N_GROUPS = 8
GROUP_SIZE = N_EXPERTS // N_GROUPS
TOPK_GROUPS = 4
EXPERT_FF = 256
SHARED_FF = 256
ROUTED_SCALE = 2.5
RMS_EPS = 1e-6
BIG = 1e9
LOG2E = 1.4426950408889634
GATE_PAD = 128
PROJ_COLS = NSA_WIDTH + 6 * KV_WIDTH + GATE_PAD + 4 * HGRN_WIDTH

VMEM_LIMIT = 56 * 1024 * 1024

TQ = 256
TK = 512
NSA_SIZE_VARIANTS = 4
EXPERT_BLOCK = 512
EXPERT_TAIL = 128
EXPERT_RING = 3
HIGHEST = lax.Precision.HIGHEST


def _cparams(*sem):
    return pltpu.CompilerParams(dimension_semantics=sem, vmem_limit_bytes=VMEM_LIMIT)


def _sigmoid(x):
    return 1.0 / (1.0 + jnp.exp(-x))


def _dot_nt(a, b):
    return lax.dot_general(a, b, (((1,), (1,)), ((), ())), preferred_element_type=F32)


def _dot(a, b, **kw):
    return jnp.dot(a, b, preferred_element_type=F32, **kw)


def _split_dot(a_bf16_exact, x):
    hi = x.astype(BF16)
    lo = (x - hi.astype(F32)).astype(BF16)
    return _dot(a_bf16_exact, hi) + _dot(a_bf16_exact, lo)


def _mod_kernel(c_ref, w_ref, b_ref, o_ref):
    c = c_ref[...]
    cond = c * _sigmoid(c)
    o_ref[...] = _dot(cond, w_ref[...], precision=HIGHEST) + b_ref[...]


def _mod(c, ada_w, ada_b):
    b, d = c.shape
    rows = 8
    c_pad = jnp.zeros((rows, d), F32).at[:b].set(c)
    n = ada_w.shape[1]
    out = pl.pallas_call(
        _mod_kernel,
        grid=(n // d,),
        in_specs=[pl.BlockSpec((rows, d), lambda j: (0, 0)),
                  pl.BlockSpec((d, d), lambda j: (0, j)),
                  pl.BlockSpec((1, d), lambda j: (0, j))],
        out_specs=pl.BlockSpec((rows, d), lambda j: (0, j)),
        out_shape=jax.ShapeDtypeStruct((rows, n), F32),
        compiler_params=_cparams("parallel"),
        name="mod",
    )(c_pad, ada_w, ada_b.reshape(1, n))
    return out[:b]


def _head_rms(t, w):
    return t * lax.rsqrt(jnp.mean(t * t, axis=-1, keepdims=True) + RMS_EPS) * w


def _pos_digits(pos):
    lane = lax.broadcasted_iota(I32, pos.shape, 1)
    d0 = (lane == 0) | (lane == 3) | (lane == 6)
    d1 = (lane == 1) | (lane == 4) | (lane == 7)
    d2 = (lane == 2) | (lane == 5) | (lane == 8)
    dig = jnp.where(d0, pos >> 12, jnp.where(d1, (pos >> 6) & 63, jnp.where(d2, pos & 63, 0)))
    return dig.astype(F32)


def _inproj_kernel(x_ref, sc_ref, sh_ref, n1_ref, w_ref, qnw_ref, knw_ref, lbp_ref, qaug_ref,
                   q_ref, kcr_ref, vcr_ref, ks_ref, vst_ref, kw_ref, vwt_ref, gt_ref,
                   hq_ref, hk_ref, hlf_ref, hv_ref, hg_ref):
    x = x_ref[0]
    ms = jnp.mean(x * x, axis=-1, keepdims=True)
    h = x * lax.rsqrt(ms + RMS_EPS) * n1_ref[...] * (1.0 + sc_ref[0]) + sh_ref[0]
    p = _dot(h.astype(BF16), w_ref[...])
    tm = x.shape[0]

    qnw = qnw_ref[...]
    for hd in range(NSA_HEADS):
        t = p[:, hd * HEAD_DIM:(hd + 1) * HEAD_DIM]
        qn = _head_rms(t, qnw) * (HEAD_DIM ** -0.5 * LOG2E)
        qa = jnp.broadcast_to(qaug_ref[hd:hd + 1, :], (tm, HEAD_DIM))
        q_ref[0, hd] = jnp.concatenate([qn, qa], axis=1).astype(BF16)
    kaug = _pos_digits(pl.program_id(1) * tm + lax.broadcasted_iota(I32, (tm, HEAD_DIM), 0))

    o = NSA_WIDTH
    kcr_ref[0] = p[:, o:o + KV_WIDTH]
    vcr_ref[0] = p[:, o + KV_WIDTH:o + 2 * KV_WIDTH]
    ks = p[:, o + 2 * KV_WIDTH:o + 3 * KV_WIDTH]
    vs = p[:, o + 3 * KV_WIDTH:o + 4 * KV_WIDTH]
    kw = p[:, o + 4 * KV_WIDTH:o + 5 * KV_WIDTH]
    vw = p[:, o + 5 * KV_WIDTH:o + 6 * KV_WIDTH]
    for g in range(KV_HEADS):
        sl = slice(g * HEAD_DIM, (g + 1) * HEAD_DIM)
        ks_ref[0, g] = jnp.concatenate([_head_rms(ks[:, sl], knw_ref[1:2, :]), kaug], axis=1).astype(BF16)
        kw_ref[0, g] = jnp.concatenate([_head_rms(kw[:, sl], knw_ref[2:3, :]), kaug], axis=1).astype(BF16)
    vst = vs.T.astype(BF16)
    vwt = vw.T.astype(BF16)
    for g in range(KV_HEADS):
        vst_ref[0, g] = vst[g * HEAD_DIM:(g + 1) * HEAD_DIM, :]
        vwt_ref[0, g] = vwt[g * HEAD_DIM:(g + 1) * HEAD_DIM, :]

    o = NSA_WIDTH + 6 * KV_WIDTH
    gates = _sigmoid(p[:, o:o + GATE_PAD])
    gt_ref[0] = gates.T[:NSA_HEADS * 3, :]

    o = o + GATE_PAD
    hq = p[:, o:o + HGRN_WIDTH]
    hf = p[:, o + HGRN_WIDTH:o + 2 * HGRN_WIDTH]
    hi = p[:, o + 2 * HGRN_WIDTH:o + 3 * HGRN_WIDTH]
    hg = p[:, o + 3 * HGRN_WIDTH:o + 4 * HGRN_WIDTH]
    lbp = lbp_ref[...]
    e = jnp.exp(lbp - jnp.max(lbp, axis=0, keepdims=True))
    lb = e[0:1, :] / jnp.sum(e, axis=0, keepdims=True)
    f = lb + (1.0 - lb) * _sigmoid(hf)
    hq_ref[0] = hq * _sigmoid(hq) * (HGRN_DIM ** -0.5)
    hk_ref[0] = 1.0 - f
    hlf_ref[0] = jnp.log(f)
    hv_ref[0] = hi
    hg_ref[0] = _sigmoid(hg)


def _inproj(x, sc1, sh1, norm1_w, w_cat, q_norm_w, k_norm_w, lb_param, tm):
    b, s, d = x.shape
    row = lambda bi, i: (bi, i, 0)
    per_b = lambda bi, i: (bi, 0, 0)
    fixed2 = lambda bi, i: (0, 0)
    aw = 2 * HEAD_DIM
    rest = np.array([2.0 ** (-8.0 * (i + 1) / NSA_HEADS) for i in range(NSA_HEADS)], np.float64) * LOG2E
    qaug = np.zeros((NSA_HEADS, HEAD_DIM), np.float32)
    for i in range(3):
        term = rest.astype(np.float32).astype(BF16).astype(np.float64)
        rest = rest - term
        for dgt, wgt in enumerate((4096.0, 64.0, 1.0)):
            qaug[:, 3 * i + dgt] = term * wgt
    assert np.all(qaug == qaug.astype(BF16).astype(np.float32))
    out_shape = (
        jax.ShapeDtypeStruct((b, NSA_HEADS, s, aw), BF16),
        jax.ShapeDtypeStruct((b, s, KV_WIDTH), F32),
        jax.ShapeDtypeStruct((b, s, KV_WIDTH), F32),
        jax.ShapeDtypeStruct((b, KV_HEADS, s, aw), BF16),
        jax.ShapeDtypeStruct((b, KV_HEADS, HEAD_DIM, s), BF16),
        jax.ShapeDtypeStruct((b, KV_HEADS, s, aw), BF16),
        jax.ShapeDtypeStruct((b, KV_HEADS, HEAD_DIM, s), BF16),
        jax.ShapeDtypeStruct((b, NSA_HEADS * 3, s), F32),
    ) + tuple(jax.ShapeDtypeStruct((b, s, HGRN_WIDTH), F32) for _ in range(5))
    hm = lambda n, w: pl.BlockSpec((1, n, tm, w), lambda bi, i: (bi, 0, i, 0))
    hmt = lambda n, w: pl.BlockSpec((1, n, w, tm), lambda bi, i: (bi, 0, 0, i))
    out_specs = (
        hm(NSA_HEADS, aw),
        pl.BlockSpec((1, tm, KV_WIDTH), row),
        pl.BlockSpec((1, tm, KV_WIDTH), row),
        hm(KV_HEADS, aw), hmt(KV_HEADS, HEAD_DIM),
        hm(KV_HEADS, aw), hmt(KV_HEADS, HEAD_DIM),
        pl.BlockSpec((1, NSA_HEADS * 3, tm), lambda bi, i: (bi, 0, i)),
    ) + tuple(pl.BlockSpec((1, tm, HGRN_WIDTH), row) for _ in range(5))
    return pl.pallas_call(
        _inproj_kernel,
        grid=(b, s // tm),
        in_specs=[pl.BlockSpec((1, tm, d), row),
                  pl.BlockSpec((1, 1, d), per_b),
                  pl.BlockSpec((1, 1, d), per_b),
                  pl.BlockSpec((1, d), fixed2),
                  pl.BlockSpec((d, PROJ_COLS), fixed2),
                  pl.BlockSpec((1, HEAD_DIM), fixed2),
                  pl.BlockSpec((3, HEAD_DIM), fixed2),
                  pl.BlockSpec(lb_param.shape, fixed2),
                  pl.BlockSpec((NSA_HEADS, HEAD_DIM), fixed2)],
        out_specs=out_specs,
        out_shape=out_shape,
        compiler_params=_cparams("parallel", "parallel"),
        name="inproj",
    )(x, sc1, sh1, norm1_w, w_cat, q_norm_w, k_norm_w, lb_param, jnp.asarray(qaug))


def _gelu_tanh(x):
    return 0.5 * x * (1.0 + jnp.tanh(0.7978845608028654 * (x + 0.044715 * x * x * x)))


def _compress_kernel(kch_ref, vch_ref, pos_ref, wa_ref, wb_ref, b1_ref, w2_ref, knw_ref,
                     kc_ref, vct_ref):
    n = kch_ref.shape[1]
    outs = []
    for br, ch_ref in enumerate((kch_ref, vch_ref)):
        ch = ch_ref[0]
        a = _dot((ch + pos_ref[br, 0:1, :]).astype(BF16), wa_ref[br])
        bm = _dot((ch + pos_ref[br, 1:2, :]).astype(BF16), wb_ref[br])
        pre = a + pltpu.roll(bm, n - 1, 0) + b1_ref[br]
        hid = _gelu_tanh(pre).astype(BF16)
        outs.append([_dot(hid[:, g * CMP_HIDDEN:(g + 1) * CMP_HIDDEN], w2_ref[br]) for g in range(KV_HEADS)])
    end_digits = _pos_digits(lax.broadcasted_iota(I32, (n, HEAD_DIM), 0) * CMP_STRIDE + (CMP_BLOCK - 1))
    for g in range(KV_HEADS):
        kc_ref[0, g] = jnp.concatenate([_head_rms(outs[0][g], knw_ref[0:1, :]), end_digits], axis=1).astype(BF16)
    vct = jnp.concatenate(outs[1], axis=1).T.astype(BF16)
    for g in range(KV_HEADS):
        vct_ref[0, g] = vct[g * HEAD_DIM:(g + 1) * HEAD_DIM, :]


def _compress(kc_raw, vc_raw, cmp_pos, cmp_w1, cmp_b1, cmp_w2, k_norm_w):
    b, s, _ = kc_raw.shape
    n = s // CMP_STRIDE
    half = CMP_STRIDE
    cw = CMP_STRIDE * KV_WIDTH
    kch = kc_raw.reshape(b, n, cw)
    vch = vc_raw.reshape(b, n, cw)
    pos = cmp_pos.reshape(2, 2, half, 1, HEAD_DIM)
    pos = jnp.broadcast_to(pos, (2, 2, half, KV_HEADS, HEAD_DIM)).reshape(2, 2, cw)
    w1 = cmp_w1.reshape(2, 2, half, HEAD_DIM, CMP_HIDDEN)
    eye = jnp.eye(KV_HEADS, dtype=F32)
    wfull = jnp.einsum('rhjdn,gk->rhjgdkn', w1, eye).reshape(2, 2, cw, KV_HEADS * CMP_HIDDEN).astype(BF16)
    b1 = jnp.tile(cmp_b1.reshape(2, 1, CMP_HIDDEN), (1, 1, KV_HEADS))
    fix = lambda r: (lambda bi: (0,) * r)
    return pl.pallas_call(
        _compress_kernel,
        grid=(b,),
        in_specs=[pl.BlockSpec((1, n, cw), lambda bi: (bi, 0, 0)),
                  pl.BlockSpec((1, n, cw), lambda bi: (bi, 0, 0)),
                  pl.BlockSpec((2, 2, cw), fix(3)),
                  pl.BlockSpec((2, cw, KV_HEADS * CMP_HIDDEN), fix(3)),
                  pl.BlockSpec((2, cw, KV_HEADS * CMP_HIDDEN), fix(3)),
                  pl.BlockSpec((2, 1, KV_HEADS * CMP_HIDDEN), fix(3)),
                  pl.BlockSpec((2, CMP_HIDDEN, HEAD_DIM), fix(3)),
                  pl.BlockSpec((3, HEAD_DIM), fix(2))],
        out_specs=(pl.BlockSpec((1, KV_HEADS, n, 2 * HEAD_DIM), lambda bi: (bi, 0, 0, 0)),
                   pl.BlockSpec((1, KV_HEADS, HEAD_DIM, n), lambda bi: (bi, 0, 0, 0))),
        out_shape=(jax.ShapeDtypeStruct((b, KV_HEADS, n, 2 * HEAD_DIM), BF16),
                   jax.ShapeDtypeStruct((b, KV_HEADS, HEAD_DIM, n), BF16)),
        compiler_params=_cparams("parallel"),
        name="compress",
    )(kch, vch, pos, wfull[:, 0], wfull[:, 1], b1, cmp_w2.astype(BF16), k_norm_w)


def _nsa_kernel(q_ref, kc_ref, vct_ref, ks_ref, vst_ref, kw_ref, vwt_ref, gt_ref, cdiff_ref, wdiff_ref,
                ovl_ref, oh_ref, onw_ref, wmask_ref, o_ref, buf_a, buf_b, m_scr, acc_scr, oc_scr, bias_scr,
                lst, cnt, *, n_top):
    q0 = pl.program_id(2) * TQ
    ncols = HEADS_PER_KV * TQ
    q = q_ref[0].reshape(ncols, 2 * HEAD_DIM)
    ns = ovl_ref.shape[0]

    def compress_and_select(nk, nb):
        s = jnp.where(cdiff_ref[0:nk, :] <= q0, _dot_nt(kc_ref[0, 0, 0:nk, :], q), -jnp.inf)
        m = jnp.max(s, axis=0, keepdims=True)
        m = jnp.where(m == -jnp.inf, 0.0, m)
        e = jnp.exp2(s - m)
        p = e / jnp.maximum(jnp.sum(e, axis=0, keepdims=True), 1e-30)
        oc_scr[...] = _dot(vct_ref[0, 0, :, 0:nk], p.astype(BF16))

        psum = p[:, 0:TQ]
        for hh in range(1, HEADS_PER_KV):
            psum = psum + p[:, hh * TQ:(hh + 1) * TQ]
        imp = _split_dot(ovl_ref[0:nb, 0:nk], psum)
        blk = lax.broadcasted_iota(I32, (nb, TQ), 0)
        tq = q0 + lax.broadcasted_iota(I32, (nb, TQ), 1)
        cur = tq >> 6
        forced = (blk == 0) | (blk == cur) | (blk == cur - 1)
        rank = jnp.where(forced, BIG, jnp.where(blk * SEL_BLOCK <= tq, imp, -BIG))
        blkf = blk.astype(F32)

        bias = jnp.full((nb, TQ), -1e30, F32)
        for _ in range(min(n_top, nb)):
            mx = jnp.max(rank, axis=0, keepdims=True)
            first = jnp.min(jnp.where(rank == mx, blkf, float(nb)), axis=0, keepdims=True)
            hit = blkf == first
            rank = jnp.where(hit, -jnp.inf, rank)
            bias = jnp.where(hit, 0.0, bias)
        bias_scr[...] = jnp.full((128, TQ), -1e30, F32)
        bias_scr[0:nb, :] = jnp.where(blk == 0, -1e30, bias)

    nc = kc_ref.shape[2]
    quarter = (q0 + TQ - 1) // (ks_ref.shape[2] // NSA_SIZE_VARIANTS)
    for v in range(NSA_SIZE_VARIANTS):
        @pl.when(quarter == v)
        def _():
            compress_and_select(nc * (v + 1) // NSA_SIZE_VARIANTS, ns * (v + 1) // NSA_SIZE_VARIANTS)

    o_c = oc_scr[...]
    bias = bias_scr[...]

    bias_t = bias.T.astype(BF16)
    qq = jnp.concatenate([q, jnp.concatenate([bias_t] * HEADS_PER_KV, axis=0)], axis=1)
    ones_rows = jnp.ones((16, TK), BF16)

    def scores(j):
        k0 = pl.multiple_of(j * TK, TK)
        kk = jnp.concatenate([ks_ref[0, 0, pl.ds(k0, TK), :], oh_ref[pl.ds(k0, TK), :]], axis=1)
        return _dot_nt(kk, qq)

    def consume(buf, j, causal, part):
        sc = buf[...]
        if causal:
            sc = jnp.where(wdiff_ref[0:TK, :] + (q0 - j * TK) >= 0, sc, -1e30)
        k0 = pl.multiple_of(j * TK, TK)
        m_run = m_scr[part]
        m_new = jnp.maximum(m_run, jnp.max(sc, axis=0, keepdims=True))
        ex = jnp.exp2(sc - m_new).astype(BF16)
        va = jnp.concatenate([vst_ref[0, 0, :, pl.ds(k0, TK)], ones_rows], axis=0)
        acc_scr[part] = jnp.exp2(m_run - m_new) * acc_scr[part] + _dot(va, ex)
        m_scr[part] = m_new

    n_past = q0 // TK
    blocks_per_tile = TK // SEL_BLOCK
    cnt[0] = 0
    for j in range(ks_ref.shape[2] // TK):
        wanted = jnp.max(bias[j * blocks_per_tile:(j + 1) * blocks_per_tile, :]) == 0.0

        @pl.when(wanted & (j < n_past))
        def _():
            lst[cnt[0]] = j
            cnt[0] = cnt[0] + 1

    n_sel = cnt[0]
    lst[n_sel] = n_past

    buf_a[...] = scores(lst[0])

    s0 = jnp.where(wdiff_ref[0:SEL_BLOCK, :] + q0 >= 0, _dot_nt(ks_ref[0, 0, 0:SEL_BLOCK, :], q), -1e30)
    m0 = jnp.max(s0, axis=0, keepdims=True)
    v0 = jnp.concatenate([vst_ref[0, 0, :, 0:SEL_BLOCK], jnp.ones((16, SEL_BLOCK), BF16)], axis=0)
    m_scr[0] = m0
    acc_scr[0] = _dot(v0, jnp.exp2(s0 - m0).astype(BF16))
    m_scr[1] = jnp.full((1, ncols), -1e30, F32)
    acc_scr[1] = jnp.zeros((HEAD_DIM + 16, ncols), F32)

    nw = WINDOW + TQ
    start = pl.multiple_of(jnp.maximum(q0 - WINDOW, 0), TQ)
    sw = _dot_nt(kw_ref[0, 0, pl.ds(start, nw), :], q) + wmask_ref[0]
    ew = jnp.exp2(sw - jnp.max(sw, axis=0, keepdims=True))
    vw_aug = jnp.concatenate([vwt_ref[0, 0, :, pl.ds(start, nw)], jnp.ones((16, nw), BF16)], axis=0)
    acc_w = _dot(vw_aug, ew.astype(BF16))
    o_w = acc_w[0:HEAD_DIM, :] / acc_w[HEAD_DIM:HEAD_DIM + 1, :]

    def tiles(first, count):
        for u in range(0, count, 2):
            buf_b[...] = scores(lst[first + u + 1])
            consume(buf_a, lst[first + u], False, 0)
            buf_a[...] = scores(lst[first + u + 2])
            consume(buf_b, lst[first + u + 1], False, 1)
        return 0

    lax.fori_loop(0, n_sel // 4, lambda i, _: tiles(4 * i, 4), 0)
    lax.fori_loop(0, (n_sel // 2) % 2, lambda i, _: tiles((n_sel // 4) * 4, 2), 0)

    @pl.when(n_sel % 2 == 1)
    def _():
        buf_b[...] = scores(n_past)
        consume(buf_a, lst[n_sel - 1], False, 0)
        consume(buf_b, n_past, True, 1)

    @pl.when(n_sel % 2 == 0)
    def _():
        consume(buf_a, n_past, True, 0)

    m_all = jnp.maximum(m_scr[0], m_scr[1])
    acc_s = jnp.exp2(m_scr[0] - m_all) * acc_scr[0] + jnp.exp2(m_scr[1] - m_all) * acc_scr[1]
    o_s = acc_s[0:HEAD_DIM, :] / acc_s[HEAD_DIM:HEAD_DIM + 1, :]

    gt = gt_ref[0, 0]
    outs = []
    for hh in range(HEADS_PER_KV):
        cs = slice(hh * TQ, (hh + 1) * TQ)
        o = (gt[3 * hh:3 * hh + 1, :] * o_c[:, cs] + gt[3 * hh + 1:3 * hh + 2, :] * o_s[:, cs]
             + gt[3 * hh + 2:3 * hh + 3, :] * o_w[:, cs])
        o = o * lax.rsqrt(jnp.mean(o * o, axis=0, keepdims=True) + RMS_EPS) * onw_ref[0, hh]
        outs.append(o)
    o_ref[0] = jnp.concatenate(outs, axis=0).T


def _nsa(q, kc, vct, ks, vst, kw, vwt, gates_t, attn_out_norm_w):
    b, _, s, aw = q.shape
    nc = kc.shape[2]
    ns = s // SEL_BLOCK
    n_top = min(N_SELECT, ns)
    ncols = HEADS_PER_KV * TQ
    nw = WINDOW + TQ
    tl = np.arange(ncols)[None, :] & (TQ - 1)
    cdiff = jnp.asarray((np.arange(nc)[:, None] * CMP_STRIDE + (CMP_BLOCK - 1) - tl).astype(np.int32))
    wdiff_np = (tl - np.arange(nw)[:, None]).astype(np.int32)
    wdiff = jnp.asarray(wdiff_np)
    n_off = WINDOW // TQ + 1
    dist_np = wdiff_np[None] + (np.arange(n_off) * TQ)[:, None, None]
    wmask = jnp.asarray(np.where((dist_np >= 0) & (dist_np < WINDOW), 0.0, -np.inf).astype(np.float32))
    ci = np.arange(nc)[None, :] * CMP_STRIDE
    bj = np.arange(ns)[:, None]
    ovl = ((ci < (bj + 1) * SEL_BLOCK) & (ci + CMP_BLOCK > bj * SEL_BLOCK) & (np.arange(nc)[None, :] < nc - 1))
    ovl = jnp.asarray(ovl.astype(np.float32)).astype(BF16)
    assert ns <= 128
    onehot = (np.arange(s)[:, None] // SEL_BLOCK == np.arange(128)[None, :])
    onehot = jnp.asarray(onehot.astype(np.float32)).astype(BF16)
    onw = jnp.broadcast_to(attn_out_norm_w.reshape(KV_HEADS, HEADS_PER_KV, HEAD_DIM, 1),
                           (KV_HEADS, HEADS_PER_KV, HEAD_DIM, TQ))
    gt = gates_t.reshape(b, KV_HEADS, HEADS_PER_KV * 3, s)
    per_bg = lambda bi, g, i: (bi, g, 0, 0)
    fixed = lambda bi, g, i: (0, 0)
    return pl.pallas_call(
        functools.partial(_nsa_kernel, n_top=n_top),
        grid=(b, KV_HEADS, s // TQ),
        in_specs=[pl.BlockSpec((1, HEADS_PER_KV, TQ, aw), lambda bi, g, i: (bi, g, i, 0)),
                  pl.BlockSpec((1, 1, nc, aw), per_bg),
                  pl.BlockSpec((1, 1, HEAD_DIM, nc), per_bg),
                  pl.BlockSpec((1, 1, s, aw), per_bg),
                  pl.BlockSpec((1, 1, HEAD_DIM, s), per_bg),
                  pl.BlockSpec((1, 1, s, aw), per_bg),
                  pl.BlockSpec((1, 1, HEAD_DIM, s), per_bg),
                  pl.BlockSpec((1, 1, HEADS_PER_KV * 3, TQ), lambda bi, g, i: (bi, g, 0, i)),
                  pl.BlockSpec((nc, ncols), fixed, pipeline_mode=pl.Buffered(1)),
                  pl.BlockSpec((nw, ncols), fixed, pipeline_mode=pl.Buffered(1)),
                  pl.BlockSpec((ns, nc), fixed, pipeline_mode=pl.Buffered(1)),
                  pl.BlockSpec((s, 128), fixed, pipeline_mode=pl.Buffered(1)),
                  pl.BlockSpec((1, HEADS_PER_KV, HEAD_DIM, TQ), lambda bi, g, i: (g, 0, 0, 0)),
                  pl.BlockSpec((1, nw, ncols), lambda bi, g, i: (jnp.minimum(i, n_off - 1), 0, 0))],
        out_specs=pl.BlockSpec((1, TQ, HEADS_PER_KV * HEAD_DIM), lambda bi, g, i: (bi, i, g)),
        out_shape=jax.ShapeDtypeStruct((b, s, NSA_WIDTH), F32),
        scratch_shapes=[pltpu.VMEM((TK, ncols), F32), pltpu.VMEM((TK, ncols), F32),
                        pltpu.VMEM((2, 1, ncols), F32), pltpu.VMEM((2, HEAD_DIM + 16, ncols), F32),
                        pltpu.VMEM((HEAD_DIM, ncols), F32), pltpu.VMEM((128, TQ), F32),
                        pltpu.SMEM((s // TK + 1,), I32), pltpu.SMEM((1,), I32)],
        compiler_params=_cparams("parallel", "parallel", "arbitrary"),
        name="nsa",
    )(q, kc, vct, ks, vst, kw, vwt, gt, cdiff, wdiff, ovl, onehot, onw, wmask)


def _hgrn_kernel(q_ref, k_ref, lf_ref, v_ref, g_ref, onw_ref, cm_ref, o_ref, state_scr, *, n_chunks):
    c = HGRN_CHUNK

    @pl.when(pl.program_id(1) == 0)
    def _():
        state_scr[...] = jnp.zeros_like(state_scr)

    ri = lax.broadcasted_iota(I32, (c, c), 0)
    ci = lax.broadcasted_iota(I32, (c, c), 1)
    rsub = ri // HGRN_SUB
    diag = ri == ci

    def head_chunk(r0, hd, state_t):
        cols = slice(hd * HGRN_DIM, (hd + 1) * HGRN_DIM)
        q = q_ref[0, pl.ds(r0, c), cols]
        k = k_ref[0, pl.ds(r0, c), cols]
        lf = lf_ref[0, pl.ds(r0, c), cols] * LOG2E
        v = v_ref[0, pl.ds(r0, c), cols]
        cm = cm_ref[...]
        l1 = lf.astype(BF16)
        rest = lf - l1.astype(F32)
        l2 = rest.astype(BF16)
        l3 = (rest - l2.astype(F32)).astype(BF16)
        cum = _dot(cm, l1) + _dot(cm, l2) + _dot(cm, l3)
        o = _dot_nt((q * jnp.exp2(cum)).astype(BF16), state_t.astype(BF16))
        scores = jnp.where(diag, jnp.sum(q * k, axis=-1, keepdims=True), 0.0)

        def factored(ref, mask, acc):
            qs = q * jnp.exp2(jnp.minimum(cum - ref, 0.0))
            kd = k * jnp.exp2(jnp.minimum(ref - cum, 0.0))
            return jnp.where(mask, _dot_nt(qs.astype(BF16), kd.astype(BF16)), acc)

        for i in range(1, c // HGRN_SUB):
            scores = factored(cum[i * HGRN_SUB - 1:i * HGRN_SUB, :], (rsub == i) & (ci < i * HGRN_SUB), scores)
        for d in range(1, HGRN_SUB):
            ksh = pltpu.roll(k, d, 0)
            csh = pltpu.roll(cum, d, 0)
            w = jnp.sum(q * ksh * jnp.exp2(cum - csh), axis=-1, keepdims=True)
            scores = jnp.where((ri - ci == d) & ((ri & (HGRN_SUB - 1)) >= d), w, scores)
        o = o + _dot(scores.astype(BF16), v.astype(BF16))
        last = cum[c - 1:c, :]
        kd = (k * jnp.exp2(last - cum)).astype(BF16)
        state_t = state_t * jnp.exp2(last) + _dot(v.T.astype(BF16), kd)
        o = o * g_ref[0, pl.ds(r0, c), cols]
        o = o * lax.rsqrt(jnp.mean(o * o, axis=-1, keepdims=True) + RMS_EPS) * onw_ref[:, cols]
        o_ref[0, pl.ds(r0, c), cols] = o
        return state_t

    def chunk(ck, states):
        r0 = pl.multiple_of(ck * c, c)
        return tuple(head_chunk(r0, hd, states[hd]) for hd in range(HGRN_HEADS))

    states = lax.fori_loop(0, n_chunks, chunk, tuple(state_scr[hd] for hd in range(HGRN_HEADS)))
    for hd in range(HGRN_HEADS):
        state_scr[hd] = states[hd]


def _hgrn(hq, hk, hlf, hv, hg, rec_out_norm_w, rows):
    b, s, _ = hq.shape
    cm = jnp.asarray(np.tril(np.ones((HGRN_CHUNK, HGRN_CHUNK), np.float32))).astype(BF16)
    blk = pl.BlockSpec((1, rows, HGRN_WIDTH), lambda bi, i: (bi, i, 0))
    return pl.pallas_call(
        functools.partial(_hgrn_kernel, n_chunks=rows // HGRN_CHUNK),
        grid=(b, s // rows),
        in_specs=[blk, blk, blk, blk, blk,
                  pl.BlockSpec((1, HGRN_WIDTH), lambda bi, i: (0, 0)),
                  pl.BlockSpec(cm.shape, lambda bi, i: (0, 0))],
        out_specs=blk,
        out_shape=jax.ShapeDtypeStruct((b, s, HGRN_WIDTH), F32),
        scratch_shapes=[pltpu.VMEM((HGRN_HEADS, HGRN_DIM, HGRN_DIM), F32)],
        compiler_params=_cparams("parallel", "arbitrary"),
        name="hgrn",
    )(hq, hk, hlf, hv, hg, rec_out_norm_w.reshape(1, HGRN_WIDTH), cm)


def _outproj_kernel(x_ref, a_ref, r_ref, wa_ref, wr_ref, gt_ref, sc_ref, sh_ref, n2_ref, x1_ref, h2_ref, h2p_ref):
    mixed = _dot(a_ref[0].astype(BF16), wa_ref[...]) + _dot(r_ref[0].astype(BF16), wr_ref[...])
    x1 = x_ref[0] + gt_ref[0] * mixed
    x1_ref[0] = x1
    ms = jnp.mean(x1 * x1, axis=-1, keepdims=True)
    h2 = x1 * lax.rsqrt(ms + RMS_EPS) * n2_ref[...] * (1.0 + sc_ref[0]) + sh_ref[0]
    h2_ref[0] = h2
    h2p_ref[0] = _pack_bf16_pair(h2[:, :D_MODEL // 2], h2[:, D_MODEL // 2:])


def _outproj(x, attn, rec, w_out, gt1, sc2, sh2, norm2_w, tm):
    b, s, d = x.shape
    row = lambda bi, i: (bi, i, 0)
    per_b = lambda bi, i: (bi, 0, 0)
    fixed2 = lambda bi, i: (0, 0)
    w = w_out.astype(BF16)
    return pl.pallas_call(
        _outproj_kernel,
        grid=(b, s // tm),
        in_specs=[pl.BlockSpec((1, tm, d), row),
                  pl.BlockSpec((1, tm, NSA_WIDTH), row),
                  pl.BlockSpec((1, tm, HGRN_WIDTH), row),
                  pl.BlockSpec((NSA_WIDTH, d), fixed2),
                  pl.BlockSpec((HGRN_WIDTH, d), fixed2),
                  pl.BlockSpec((1, 1, d), per_b),
                  pl.BlockSpec((1, 1, d), per_b),
                  pl.BlockSpec((1, 1, d), per_b),
                  pl.BlockSpec((1, d), fixed2)],
        out_specs=(pl.BlockSpec((1, tm, d), row), pl.BlockSpec((1, tm, d), row), pl.BlockSpec((1, tm, d // 2), row)),
        out_shape=(jax.ShapeDtypeStruct((b, s, d), F32), jax.ShapeDtypeStruct((b, s, d), F32),
                   jax.ShapeDtypeStruct((b, s, d // 2), jnp.uint32)),
        compiler_params=_cparams("parallel", "parallel"),
        name="outproj",
    )(x, attn, rec, w[:NSA_WIDTH], w[NSA_WIDTH:], gt1, sc2, sh2, norm2_w)


def _mixer(x, c, ada_w, ada_b, norm1_w, norm2_w, w_in, q_norm_w, k_norm_w, cmp_pos, cmp_w1, cmp_b1, cmp_w2,
           attn_out_norm_w, hgrn_lb_param, rec_out_norm_w, w_out):
    b, s, d = x.shape
    mod = _mod(c, ada_w, ada_b)
    sh1, sc1, gt1, sh2, sc2, gt2 = [m.reshape(b, 1, d) for m in jnp.split(mod, 6, axis=-1)]
    o = NSA_WIDTH + 6 * KV_WIDTH
    w_cat = jnp.concatenate([w_in[:, :o], w_in[:, o:o + NSA_HEADS * 3],
                             jnp.zeros((d, GATE_PAD - NSA_HEADS * 3), w_in.dtype),
                             w_in[:, o + NSA_HEADS * 3:]], axis=1).astype(BF16)
    tm = min(512, s)
    (q, kc_raw, vc_raw, ks, vst, kw, vwt, gates_t, hq, hk, hlf, hv, hg) = _inproj(
        x, sc1, sh1, norm1_w.reshape(1, d), w_cat, q_norm_w.reshape(1, HEAD_DIM), k_norm_w, hgrn_lb_param, tm)
    kc, vct = _compress(kc_raw, vc_raw, cmp_pos, cmp_w1, cmp_b1, cmp_w2, k_norm_w)
    attn = _nsa(q, kc, vct, ks, vst, kw, vwt, gates_t, attn_out_norm_w)
    rec = _hgrn(hq, hk, hlf, hv, hg, rec_out_norm_w, min(1024, s))
    x1, h2, h2p = _outproj(x, attn, rec, w_out, gt1, sc2, sh2, norm2_w.reshape(1, d), tm)
    return x1, h2, h2p, gt2


def _router_kernel(h_ref, rwt_ref, bias_ref, tri_ref, ones_ref, idx_ref, w_ref, rank_ref, cnt_ref, carry_scr, *, tr):
    @pl.when(pl.program_id(0) == 0)
    def _():
        carry_scr[...] = jnp.zeros_like(carry_scr)

    h = h_ref[...]
    h_hi = h.astype(BF16)
    h_lo = (h - h_hi.astype(F32)).astype(BF16)
    logits = _dot_nt(rwt_ref[0], h_hi) + _dot_nt(rwt_ref[1], h_hi) + _dot_nt(rwt_ref[0], h_lo)
    scores = _sigmoid(logits)
    biased = scores + bias_ref[...]
    neg = -jnp.inf

    gs = []
    for g in range(N_GROUPS):
        sub = biased[g * GROUP_SIZE:(g + 1) * GROUP_SIZE, :]
        m1 = jnp.max(sub, axis=0, keepdims=True)
        dup = jnp.sum((sub == m1).astype(F32), axis=0, keepdims=True)
        m2 = jnp.max(jnp.where(sub < m1, sub, neg), axis=0, keepdims=True)
        gs.append(m1 + jnp.where(dup >= 2.0, m1, m2))
    parts = []
    for g in range(N_GROUPS):
        beaten = jnp.zeros_like(gs[g])
        for g2 in range(N_GROUPS):
            if g2 != g:
                beats = (gs[g2] >= gs[g]) if g2 < g else (gs[g2] > gs[g])
                beaten = beaten + beats.astype(F32)
        sub = biased[g * GROUP_SIZE:(g + 1) * GROUP_SIZE, :]
        parts.append(jnp.where(beaten < float(TOPK_GROUPS), sub, neg))
    cand = jnp.concatenate(parts, axis=0)

    rowf = lax.broadcasted_iota(I32, (N_EXPERTS, tr), 0).astype(F32)
    idx_rows, w_rows, hits = [], [], []
    multi = jnp.zeros((N_EXPERTS, tr), F32)
    for _ in range(TOP_K):
        mx = jnp.max(cand, axis=0, keepdims=True)
        first = jnp.min(jnp.where(cand == mx, rowf, float(N_EXPERTS)), axis=0, keepdims=True)
        hit = rowf == first
        idx_rows.append(first)
        w_rows.append(jnp.sum(jnp.where(hit, scores, 0.0), axis=0, keepdims=True))
        cand = jnp.where(hit, neg, cand)
        multi = jnp.where(hit, 1.0, multi)
    w = jnp.concatenate(w_rows, axis=0)
    w_ref[...] = w / jnp.sum(w, axis=0, keepdims=True) * ROUTED_SCALE
    idx = jnp.concatenate(idx_rows, axis=0)
    idx_ref[...] = idx.astype(I32)

    carry = carry_scr[...]
    mb = multi.astype(BF16)
    before = _dot(mb, tri_ref[...]) + jnp.concatenate([carry] * (tr // 128), axis=1)
    rank_rows = [jnp.sum(jnp.where(rowf == idx_rows[k], before, 0.0), axis=0, keepdims=True) for k in range(TOP_K)]
    rank_ref[...] = jnp.concatenate(rank_rows, axis=0).astype(I32)
    carry = carry + _dot(mb, ones_ref[...])
    carry_scr[...] = carry
    cnt_ref[...] = carry


def _router(h2, router_w, router_bias, tr):
    t, d = h2.shape
    tri = jnp.asarray(np.triu(np.ones((tr, tr), np.float32), 1)).astype(BF16)
    ones = jnp.ones((tr, 128), BF16)
    tok = pl.BlockSpec((TOP_K, tr), lambda i: (0, i))
    fixed = lambda i: (0, 0)
    rwt = router_w.T
    rwt_hi = rwt.astype(BF16)
    rwt_split = jnp.stack([rwt_hi, (rwt - rwt_hi.astype(F32)).astype(BF16)])
    return pl.pallas_call(
        functools.partial(_router_kernel, tr=tr),
        grid=(t // tr,),
        in_specs=[pl.BlockSpec((tr, d), lambda i: (i, 0)),
                  pl.BlockSpec((2, N_EXPERTS, d), lambda i: (0, 0, 0)),
                  pl.BlockSpec((N_EXPERTS, 1), fixed),
                  pl.BlockSpec((tr, tr), fixed),
                  pl.BlockSpec((tr, 128), fixed)],
        out_specs=(tok, tok, tok, pl.BlockSpec((N_EXPERTS, 128), fixed)),
        out_shape=(jax.ShapeDtypeStruct((TOP_K, t), I32), jax.ShapeDtypeStruct((TOP_K, t), F32),
                   jax.ShapeDtypeStruct((TOP_K, t), I32), jax.ShapeDtypeStruct((N_EXPERTS, 128), F32)),
        scratch_shapes=[pltpu.VMEM((N_EXPERTS, 128), F32)],
        compiler_params=_cparams("arbitrary"),
        name="router",
    )(h2, rwt_split, router_bias.reshape(N_EXPERTS, 1), tri, ones)


def _pack_bf16_pair(a, b):
    ua = lax.bitcast_convert_type(a.astype(BF16).astype(F32), jnp.uint32)
    ub = lax.bitcast_convert_type(b.astype(BF16).astype(F32), jnp.uint32)
    return ua | (ub >> 16)


def _unpack_bf16_pair(w):
    a = lax.bitcast_convert_type(w & jnp.uint32(0xFFFF0000), F32)
    b = lax.bitcast_convert_type(w << 16, F32)
    return a, b


def _slot_kernel(ps_ref, idx_ref, rank_ref, slot_ref):
    idx = idx_ref[...]

    def body(e, acc):
        return jnp.where(idx == e, ps_ref[e], acc)

    slot_ref[...] = lax.fori_loop(0, N_EXPERTS, body, jnp.zeros_like(idx)) + rank_ref[...]


def _slots(pad_start, idx, rank, tt):
    t = idx.shape[1]
    tok = pl.BlockSpec((TOP_K, tt), lambda i, ps: (0, i))
    return pl.pallas_call(
        _slot_kernel,
        grid_spec=pltpu.PrefetchScalarGridSpec(num_scalar_prefetch=1, grid=(t // tt,),
                                               in_specs=[tok, tok], out_specs=tok),
        out_shape=jax.ShapeDtypeStruct((TOP_K, t), I32),
        compiler_params=_cparams("parallel"),
        name="slots",
    )(pad_start, idx, rank)


SC_CORES = 2
SC_SUBCORES = 16
SC_CHUNK = 64


def _sc_mesh():
    return plsc.VectorSubcoreMesh(core_axis_name="c", subcore_axis_name="s")


def _sc_dispatch(h2p, slot_chunks, n_rows):
    t, dw = h2p.shape
    per = slot_chunks.shape[0] // (SC_CORES * SC_SUBCORES)

    def body(h_hbm, slot_hbm, xs_hbm, idx_v, rows_v, sem):
        wid = lax.axis_index("s") * SC_CORES + lax.axis_index("c")

        @pl.loop(0, per)
        def _(c):
            ch = wid * per + c
            pltpu.sync_copy(slot_hbm.at[ch], idx_v)
            pltpu.sync_copy(h_hbm.at[pl.ds(ch * SC_CHUNK, SC_CHUNK)], rows_v)
            copies = [pltpu.async_copy(rows_v, xs_hbm.at[idx_v.at[k]], sem) for k in range(TOP_K)]
            for cp in copies:
                cp.wait()

    return pl.kernel(
        body, out_type=jax.ShapeDtypeStruct((n_rows, dw), h2p.dtype), mesh=_sc_mesh(),
        scratch_types=[pltpu.VMEM((TOP_K, SC_CHUNK), I32), pltpu.VMEM((SC_CHUNK, dw), h2p.dtype),
                       pltpu.SemaphoreType.DMA],
    )(h2p, slot_chunks)


def _sc_gather(ys, slot_chunks, t):
    dw = ys.shape[1]
    per = slot_chunks.shape[0] // (SC_CORES * SC_SUBCORES)

    def body(ys_hbm, slot_hbm, yg_hbm, idx_v, rows_v, gsem, wsem):
        wid = lax.axis_index("s") * SC_CORES + lax.axis_index("c")

        @pl.loop(0, per)
        def _(c):
            ch = wid * per + c
            pltpu.sync_copy(slot_hbm.at[ch], idx_v)
            gathers = [None] * TOP_K
            writes = [None] * TOP_K
            gathers[0] = pltpu.async_copy(ys_hbm.at[idx_v.at[0]], rows_v.at[0], gsem)
            for k in range(TOP_K):
                gathers[k].wait()
                if k + 1 < TOP_K:
                    if k >= 1:
                        writes[k - 1].wait()
                    gathers[k + 1] = pltpu.async_copy(ys_hbm.at[idx_v.at[k + 1]], rows_v.at[(k + 1) % 2], gsem)
                writes[k] = pltpu.async_copy(rows_v.at[k % 2], yg_hbm.at[k, pl.ds(ch * SC_CHUNK, SC_CHUNK)], wsem)
            writes[TOP_K - 2].wait()
            writes[TOP_K - 1].wait()

    return pl.kernel(
        body, out_type=jax.ShapeDtypeStruct((TOP_K, t, dw), ys.dtype), mesh=_sc_mesh(),
        scratch_types=[pltpu.VMEM((TOP_K, SC_CHUNK), I32), pltpu.VMEM((2, SC_CHUNK, dw), ys.dtype),
                       pltpu.SemaphoreType.DMA, pltpu.SemaphoreType.DMA],
    )(ys, slot_chunks)


def _experts_kernel(be_ref, nu_ref, bv_ref, run_ref, xs_hbm, wg_hbm, wu_hbm, wd_hbm, ys_ref,
                    xring, rsem, gring, uring, dring, wsem):
    i = pl.program_id(0)
    half = D_MODEL // 2
    n_used = nu_ref[0]
    n_steps = pl.num_programs(0)

    def weight_copies(blk):
        ex = be_ref[blk]
        slot = run_ref[blk] % EXPERT_RING
        return [pltpu.make_async_copy(src.at[ex], ring.at[slot], wsem.at[a, slot])
                for a, (src, ring) in enumerate(((wg_hbm, gring), (wu_hbm, uring), (wd_hbm, dring)))]

    def starts_run(blk):
        return run_ref[blk] != run_ref[jnp.maximum(blk - 1, 0)]

    @pl.when(i == 0)
    def _():
        for cp in weight_copies(jnp.int32(0)):
            cp.start()

        @pl.when((n_steps > 1) & starts_run(jnp.int32(1)))
        def _():
            for cp in weight_copies(jnp.int32(1)):
                cp.start()

    ahead = jnp.minimum(i + (EXPERT_RING - 1), n_steps - 1)

    @pl.when((i + (EXPERT_RING - 1) < n_steps) & starts_run(ahead))
    def _():
        for cp in weight_copies(ahead):
            cp.start()

    @pl.when((i == 0) | starts_run(i))
    def _():
        for cp in weight_copies(i):
            cp.wait()

    wslot = run_ref[i] % EXPERT_RING
    wg_ref, wu_ref, wd_ref = gring.at[wslot], uring.at[wslot], dring.at[wslot]

    def fetch(blk):
        slot = blk % EXPERT_RING
        return pltpu.make_async_copy(xs_hbm.at[pl.ds(pl.multiple_of(blk * EXPERT_BLOCK, EXPERT_BLOCK), EXPERT_BLOCK)],
                                     xring.at[slot], rsem.at[slot])

    @pl.when(i == 0)
    def _():
        for first in range(EXPERT_RING - 1):
            @pl.when(first < n_used)
            def _():
                fetch(jnp.int32(first)).start()

    @pl.when(i + (EXPERT_RING - 1) < n_used)
    def _():
        fetch(i + (EXPERT_RING - 1)).start()

    @pl.when(i < n_used)
    def _():
        fetch(i).wait()

    xs_ref = xring.at[i % EXPERT_RING]

    def ffn(rows):
        live = lax.broadcasted_iota(I32, (rows, xs_ref.shape[1]), 0) < bv_ref[i]
        xa, xb = _unpack_bf16_pair(jnp.where(live, xs_ref[0:rows, :], jnp.uint32(0)))
        xa, xb = xa.astype(BF16), xb.astype(BF16)
        g = _dot(xa, wg_ref[:half, :].astype(BF16)) + _dot(xb, wg_ref[half:, :].astype(BF16))
        u = _dot(xa, wu_ref[:half, :].astype(BF16)) + _dot(xb, wu_ref[half:, :].astype(BF16))
        act = (g * _sigmoid(g) * u).astype(BF16)
        y = _dot(act, wd_ref[...].astype(BF16))
        ys_ref[0:rows, :] = _pack_bf16_pair(y[:, :half], y[:, half:])

    used = i < n_used
    short = bv_ref[i] <= EXPERT_TAIL

    @pl.when(used & jnp.logical_not(short))
    def _():
        ffn(EXPERT_BLOCK)

    @pl.when(used & short)
    def _():
        ffn(EXPERT_TAIL)
        ys_ref[EXPERT_TAIL:, :] = jnp.zeros((EXPERT_BLOCK - EXPERT_TAIL, ys_ref.shape[1]), ys_ref.dtype)

    @pl.when(jnp.logical_not(used))
    def _():
        ys_ref[...] = jnp.zeros_like(ys_ref)


def _experts(xs, blk_e, n_used, blk_valid, w_gate, w_up, w_down):
    n_rows, dw = xs.shape
    d = w_gate.shape[1]
    nblk = n_rows // EXPERT_BLOCK
    blk_run = jnp.cumsum(jnp.concatenate([jnp.zeros((1,), I32), (blk_e[1:] != blk_e[:-1]).astype(I32)])).astype(I32)
    hbm = pl.BlockSpec(memory_space=pl.ANY)
    return pl.pallas_call(
        _experts_kernel,
        grid_spec=pltpu.PrefetchScalarGridSpec(
            num_scalar_prefetch=4,
            grid=(nblk,),
            in_specs=[hbm, hbm, hbm, hbm],
            out_specs=pl.BlockSpec((EXPERT_BLOCK, dw), lambda i, be, nu, bv, rn: (i, 0)),
            scratch_shapes=[pltpu.VMEM((EXPERT_RING, EXPERT_BLOCK, dw), xs.dtype),
                            pltpu.SemaphoreType.DMA((EXPERT_RING,)),
                            pltpu.VMEM((EXPERT_RING, d, EXPERT_FF), w_gate.dtype),
                            pltpu.VMEM((EXPERT_RING, d, EXPERT_FF), w_up.dtype),
                            pltpu.VMEM((EXPERT_RING, EXPERT_FF, d), w_down.dtype),
                            pltpu.SemaphoreType.DMA((3, EXPERT_RING))]),
        out_shape=jax.ShapeDtypeStruct((n_rows, dw), xs.dtype),
        compiler_params=pltpu.CompilerParams(dimension_semantics=("arbitrary",), vmem_limit_bytes=VMEM_LIMIT,
                                             has_side_effects=True),
        name="experts",
    )(blk_e, n_used, blk_valid, blk_run, xs, w_gate, w_up, w_down)


def _combine_kernel(x1_ref, h_ref, w_ref, gt_ref, sg_ref, su_ref, sd_ref, yg_ref, o_ref):
    tc = x1_ref.shape[0]
    half = D_MODEL // 2
    hb = h_ref[...].astype(BF16)
    g = _dot(hb, sg_ref[...])
    u = _dot(hb, su_ref[...])
    ffn = _dot((g * _sigmoid(g) * u).astype(BF16), sd_ref[...])

    w = w_ref[...]
    ra = jnp.zeros((tc, half), F32)
    rb = jnp.zeros((tc, half), F32)
    for k in range(TOP_K):
        ya, yb = _unpack_bf16_pair(yg_ref[k])
        ra = ra + w[:, k:k + 1] * ya
        rb = rb + w[:, k:k + 1] * yb
    ffn = ffn + jnp.concatenate([ra, rb], axis=1)
    o_ref[...] = x1_ref[...] + gt_ref[0] * ffn


def _combine(x1, h2, w_tok, gt2, yg, sg, su, sd, seq, tc):
    t, d = x1.shape
    row = lambda i: (i, 0)
    fixed = lambda i: (0, 0)
    return pl.pallas_call(
        _combine_kernel,
        grid=(t // tc,),
        in_specs=[pl.BlockSpec((tc, d), row),
                  pl.BlockSpec((tc, d), row),
                  pl.BlockSpec((tc, TOP_K), row),
                  pl.BlockSpec((1, 1, d), lambda i: ((i * tc) // seq, 0, 0)),
                  pl.BlockSpec((d, SHARED_FF), fixed),
                  pl.BlockSpec((d, SHARED_FF), fixed),
                  pl.BlockSpec((SHARED_FF, d), fixed),
                  pl.BlockSpec((TOP_K, tc, d // 2), lambda i: (0, i, 0))],
        out_specs=pl.BlockSpec((tc, d), row),
        out_shape=jax.ShapeDtypeStruct((t, d), F32),
        compiler_params=_cparams("parallel"),
        name="combine",
    )(x1, h2, w_tok, gt2, sg.astype(BF16), su.astype(BF16), sd.astype(BF16), yg)


def _moe_parts(x1, h2, h2p, gt2, router_w, router_bias, w_gate, w_up, w_down, sg, su, sd):
    b, s, d = x1.shape
    t = b * s
    assert t % (SC_CHUNK * SC_CORES * SC_SUBCORES) == 0, "token chunks must split evenly over the vector subcores"
    h2 = h2.reshape(t, d)
    idx, w, rank, cnt = _router(h2, router_w, router_bias, min(256, t))
    counts = cnt[:, 0].astype(I32)
    padded = (counts + EXPERT_BLOCK - 1) // EXPERT_BLOCK * EXPERT_BLOCK
    pad_end = jnp.cumsum(padded)
    pad_start = pad_end - padded
    n_rows = t * TOP_K + N_EXPERTS * EXPERT_BLOCK
    nblk = n_rows // EXPERT_BLOCK
    n_used = (pad_end[-1:] // EXPERT_BLOCK).astype(I32)
    blk_start = jnp.arange(nblk, dtype=I32) * EXPERT_BLOCK
    owns = (pad_start[None, :] <= blk_start[:, None]) & (blk_start[:, None] < pad_end[None, :])
    e_ids = jnp.arange(N_EXPERTS, dtype=I32)[None, :]
    last_e = jnp.max(jnp.where(counts > 0, e_ids[0], 0))
    blk_e = jnp.where(blk_start < pad_end[-1], jnp.sum(jnp.where(owns, e_ids, 0), axis=1), last_e).astype(I32)
    rows_left = jnp.sum(jnp.where(owns, (pad_start + counts)[None, :] - blk_start[:, None], 0), axis=1)
    blk_valid = jnp.clip(rows_left, 0, EXPERT_BLOCK).astype(I32)
    slot = _slots(pad_start.astype(I32), idx, rank, min(2048, t))
    slot_chunks = slot.reshape(TOP_K, t // SC_CHUNK, SC_CHUNK).transpose(1, 0, 2)
    xs = _sc_dispatch(h2p.reshape(t, d // 2), slot_chunks, n_rows)
    ys = _experts(xs, blk_e, n_used, blk_valid, w_gate, w_up, w_down)
    yg = _sc_gather(ys, slot_chunks, t)
    out = _combine(x1.reshape(t, d), h2, w.T, gt2, yg, sg, su, sd, s, min(512, t))
    return out.reshape(b, s, d), dict(idx=idx, w=w, rank=rank, cnt=cnt)


def kernel(x, c, ada_w, ada_b, norm1_w, norm2_w, w_in, q_norm_w, k_norm_w, cmp_pos, cmp_w1, cmp_b1, cmp_w2, attn_out_norm_w, hgrn_lb_param, rec_out_norm_w, w_out, router_w, router_bias, exp_w_gate, exp_w_up, exp_w_down, shared_w_gate, shared_w_up, shared_w_down):
    assert ada_w.shape[0] == 1, "one layer"
    assert x.shape[0] <= 8 and x.shape[1] % TK == 0 and x.shape[1] >= WINDOW + TQ
    l = 0
    x1, h2, h2p, gt2 = _mixer(x, c, ada_w[l], ada_b[l], norm1_w[l], norm2_w[l], w_in[l], q_norm_w[l], k_norm_w[l],
                         cmp_pos[l], cmp_w1[l], cmp_b1[l], cmp_w2[l], attn_out_norm_w[l], hgrn_lb_param,
                         rec_out_norm_w[l], w_out[l])
    out, _ = _moe_parts(x1, h2, h2p, gt2, router_w[l], router_bias[l], exp_w_gate[l], exp_w_up[l], exp_w_down[l],
                        shared_w_gate[l], shared_w_up[l], shared_w_down[l])
    return out
```

```python
import functools

import numpy as np
import jax
import jax.numpy as jnp
from jax import lax
from jax.experimental import pallas as pl
from jax.experimental.pallas import tpu as pltpu
from jax.experimental.pallas import tpu_sc as plsc

F32 = jnp.float32
BF16 = jnp.bfloat16
I32 = jnp.int32

D_MODEL = 1024
NSA_HEADS = 8
HEAD_DIM = 64
NSA_WIDTH = NSA_HEADS * HEAD_DIM
KV_HEADS = 2
HEADS_PER_KV = NSA_HEADS // KV_HEADS
KV_WIDTH = KV_HEADS * HEAD_DIM
CMP_BLOCK = 32
CMP_STRIDE = 16
CMP_HIDDEN = 256
SEL_BLOCK = 64
N_SELECT = 16
WINDOW = 512
HGRN_HEADS = 4
HGRN_DIM = 128
HGRN_WIDTH = HGRN_HEADS * HGRN_DIM
HGRN_CHUNK = 64
HGRN_SUB = 16
N_EXPERTS = 256
TOP_K = 8
N_GROUPS = 8
GROUP_SIZE = N_EXPERTS // N_GROUPS
TOPK_GROUPS = 4
EXPERT_FF = 256
SHARED_FF = 256
ROUTED_SCALE = 2.5
RMS_EPS = 1e-6
BIG = 1e9
LOG2E = 1.4426950408889634
GATE_PAD = 128
PROJ_COLS = NSA_WIDTH + 6 * KV_WIDTH + GATE_PAD + 4 * HGRN_WIDTH

VMEM_LIMIT = 56 * 1024 * 1024

TQ = 256
TK = 512
NSA_SIZE_VARIANTS = 4
EXPERT_BLOCK = 512
EXPERT_TAIL = 128
EXPERT_RING = 3
HIGHEST = lax.Precision.HIGHEST


def _cparams(*sem):
    return pltpu.CompilerParams(dimension_semantics=sem, vmem_limit_bytes=VMEM_LIMIT)


def _sigmoid(x):
    return 1.0 / (1.0 + jnp.exp(-x))


def _dot_nt(a, b):
    return lax.dot_general(a, b, (((1,), (1,)), ((), ())), preferred_element_type=F32)


def _dot(a, b, **kw):
    return jnp.dot(a, b, preferred_element_type=F32, **kw)


def _split_dot(a_bf16_exact, x):
    hi = x.astype(BF16)
    lo = (x - hi.astype(F32)).astype(BF16)
    return _dot(a_bf16_exact, hi) + _dot(a_bf16_exact, lo)


def _mod_kernel(c_ref, w_ref, b_ref, o_ref):
    c = c_ref[...]
    cond = c * _sigmoid(c)
    o_ref[...] = _dot(cond, w_ref[...], precision=HIGHEST) + b_ref[...]


def _mod(c, ada_w, ada_b):
    b, d = c.shape
    rows = 8
    c_pad = jnp.zeros((rows, d), F32).at[:b].set(c)
    n = ada_w.shape[1]
    out = pl.pallas_call(
        _mod_kernel,
        grid=(n // d,),
        in_specs=[pl.BlockSpec((rows, d), lambda j: (0, 0)),
                  pl.BlockSpec((d, d), lambda j: (0, j)),
                  pl.BlockSpec((1, d), lambda j: (0, j))],
        out_specs=pl.BlockSpec((rows, d), lambda j: (0, j)),
        out_shape=jax.ShapeDtypeStruct((rows, n), F32),
        compiler_params=_cparams("parallel"),
        name="mod",
    )(c_pad, ada_w, ada_b.reshape(1, n))
    return out[:b]


def _head_rms(t, w):
    return t * lax.rsqrt(jnp.mean(t * t, axis=-1, keepdims=True) + RMS_EPS) * w


def _pos_digits(pos):
    lane = lax.broadcasted_iota(I32, pos.shape, 1)
    d0 = (lane == 0) | (lane == 3) | (lane == 6)
    d1 = (lane == 1) | (lane == 4) | (lane == 7)
    d2 = (lane == 2) | (lane == 5) | (lane == 8)
    dig = jnp.where(d0, pos >> 12, jnp.where(d1, (pos >> 6) & 63, jnp.where(d2, pos & 63, 0)))
    return dig.astype(F32)


def _inproj_kernel(x_ref, sc_ref, sh_ref, n1_ref, w_ref, qnw_ref, knw_ref, lbp_ref, qaug_ref,
                   q_ref, kcr_ref, vcr_ref, ks_ref, vst_ref, kw_ref, vwt_ref, gt_ref,
                   hq_ref, hk_ref, hlf_ref, hv_ref, hg_ref):
    x = x_ref[0]
    ms = jnp.mean(x * x, axis=-1, keepdims=True)
    h = x * lax.rsqrt(ms + RMS_EPS) * n1_ref[...] * (1.0 + sc_ref[0]) + sh_ref[0]
    p = _dot(h.astype(BF16), w_ref[...])
    tm = x.shape[0]

    qnw = qnw_ref[...]
    for hd in range(NSA_HEADS):
        t = p[:, hd * HEAD_DIM:(hd + 1) * HEAD_DIM]
        qn = _head_rms(t, qnw) * (HEAD_DIM ** -0.5 * LOG2E)
        qa = jnp.broadcast_to(qaug_ref[hd:hd + 1, :], (tm, HEAD_DIM))
        q_ref[0, hd] = jnp.concatenate([qn, qa], axis=1).astype(BF16)
    kaug = _pos_digits(pl.program_id(1) * tm + lax.broadcasted_iota(I32, (tm, HEAD_DIM), 0))

    o = NSA_WIDTH
    kcr_ref[0] = p[:, o:o + KV_WIDTH]
    vcr_ref[0] = p[:, o + KV_WIDTH:o + 2 * KV_WIDTH]
    ks = p[:, o + 2 * KV_WIDTH:o + 3 * KV_WIDTH]
    vs = p[:, o + 3 * KV_WIDTH:o + 4 * KV_WIDTH]
    kw = p[:, o + 4 * KV_WIDTH:o + 5 * KV_WIDTH]
    vw = p[:, o + 5 * KV_WIDTH:o + 6 * KV_WIDTH]
    for g in range(KV_HEADS):
        sl = slice(g * HEAD_DIM, (g + 1) * HEAD_DIM)
        ks_ref[0, g] = jnp.concatenate([_head_rms(ks[:, sl], knw_ref[1:2, :]), kaug], axis=1).astype(BF16)
        kw_ref[0, g] = jnp.concatenate([_head_rms(kw[:, sl], knw_ref[2:3, :]), kaug], axis=1).astype(BF16)
    vst = vs.T.astype(BF16)
    vwt = vw.T.astype(BF16)
    for g in range(KV_HEADS):
        vst_ref[0, g] = vst[g * HEAD_DIM:(g + 1) * HEAD_DIM, :]
        vwt_ref[0, g] = vwt[g * HEAD_DIM:(g + 1) * HEAD_DIM, :]

    o = NSA_WIDTH + 6 * KV_WIDTH
    gates = _sigmoid(p[:, o:o + GATE_PAD])
    gt_ref[0] = gates.T[:NSA_HEADS * 3, :]

    o = o + GATE_PAD
    hq = p[:, o:o + HGRN_WIDTH]
    hf = p[:, o + HGRN_WIDTH:o + 2 * HGRN_WIDTH]
    hi = p[:, o + 2 * HGRN_WIDTH:o + 3 * HGRN_WIDTH]
    hg = p[:, o + 3 * HGRN_WIDTH:o + 4 * HGRN_WIDTH]
    lbp = lbp_ref[...]
    e = jnp.exp(lbp - jnp.max(lbp, axis=0, keepdims=True))
    lb = e[0:1, :] / jnp.sum(e, axis=0, keepdims=True)
    f = lb + (1.0 - lb) * _sigmoid(hf)
    hq_ref[0] = hq * _sigmoid(hq) * (HGRN_DIM ** -0.5)
    hk_ref[0] = 1.0 - f
    hlf_ref[0] = jnp.log(f)
    hv_ref[0] = hi
    hg_ref[0] = _sigmoid(hg)


def _inproj(x, sc1, sh1, norm1_w, w_cat, q_norm_w, k_norm_w, lb_param, tm):
    b, s, d = x.shape
    row = lambda bi, i: (bi, i, 0)
    per_b = lambda bi, i: (bi, 0, 0)
    fixed2 = lambda bi, i: (0, 0)
    aw = 2 * HEAD_DIM
    rest = np.array([2.0 ** (-8.0 * (i + 1) / NSA_HEADS) for i in range(NSA_HEADS)], np.float64) * LOG2E
    qaug = np.zeros((NSA_HEADS, HEAD_DIM), np.float32)
    for i in range(3):
        term = rest.astype(np.float32).astype(BF16).astype(np.float64)
        rest = rest - term
        for dgt, wgt in enumerate((4096.0, 64.0, 1.0)):
            qaug[:, 3 * i + dgt] = term * wgt
    assert np.all(qaug == qaug.astype(BF16).astype(np.float32))
    out_shape = (
        jax.ShapeDtypeStruct((b, NSA_HEADS, s, aw), BF16),
        jax.ShapeDtypeStruct((b, s, KV_WIDTH), F32),
        jax.ShapeDtypeStruct((b, s, KV_WIDTH), F32),
        jax.ShapeDtypeStruct((b, KV_HEADS, s, aw), BF16),
        jax.ShapeDtypeStruct((b, KV_HEADS, HEAD_DIM, s), BF16),
        jax.ShapeDtypeStruct((b, KV_HEADS, s, aw), BF16),
        jax.ShapeDtypeStruct((b, KV_HEADS, HEAD_DIM, s), BF16),
        jax.ShapeDtypeStruct((b, NSA_HEADS * 3, s), F32),
    ) + tuple(jax.ShapeDtypeStruct((b, s, HGRN_WIDTH), F32) for _ in range(5))
    hm = lambda n, w: pl.BlockSpec((1, n, tm, w), lambda bi, i: (bi, 0, i, 0))
    hmt = lambda n, w: pl.BlockSpec((1, n, w, tm), lambda bi, i: (bi, 0, 0, i))
    out_specs = (
        hm(NSA_HEADS, aw),
        pl.BlockSpec((1, tm, KV_WIDTH), row),
        pl.BlockSpec((1, tm, KV_WIDTH), row),
        hm(KV_HEADS, aw), hmt(KV_HEADS, HEAD_DIM),
        hm(KV_HEADS, aw), hmt(KV_HEADS, HEAD_DIM),
        pl.BlockSpec((1, NSA_HEADS * 3, tm), lambda bi, i: (bi, 0, i)),
    ) + tuple(pl.BlockSpec((1, tm, HGRN_WIDTH), row) for _ in range(5))
    return pl.pallas_call(
        _inproj_kernel,
        grid=(b, s // tm),
        in_specs=[pl.BlockSpec((1, tm, d), row),
                  pl.BlockSpec((1, 1, d), per_b),
                  pl.BlockSpec((1, 1, d), per_b),
                  pl.BlockSpec((1, d), fixed2),
                  pl.BlockSpec((d, PROJ_COLS), fixed2),
                  pl.BlockSpec((1, HEAD_DIM), fixed2),
                  pl.BlockSpec((3, HEAD_DIM), fixed2),
                  pl.BlockSpec(lb_param.shape, fixed2),
                  pl.BlockSpec((NSA_HEADS, HEAD_DIM), fixed2)],
        out_specs=out_specs,
        out_shape=out_shape,
        compiler_params=_cparams("parallel", "parallel"),
        name="inproj",
    )(x, sc1, sh1, norm1_w, w_cat, q_norm_w, k_norm_w, lb_param, jnp.asarray(qaug))


def _gelu_tanh(x):
    return 0.5 * x * (1.0 + jnp.tanh(0.7978845608028654 * (x + 0.044715 * x * x * x)))


def _compress_kernel(kch_ref, vch_ref, pos_ref, wa_ref, wb_ref, b1_ref, w2_ref, knw_ref,
                     kc_ref, vct_ref):
    n = kch_ref.shape[1]
    outs = []
    for br, ch_ref in enumerate((kch_ref, vch_ref)):
        ch = ch_ref[0]
        a = _dot((ch + pos_ref[br, 0:1, :]).astype(BF16), wa_ref[br])
        bm = _dot((ch + pos_ref[br, 1:2, :]).astype(BF16), wb_ref[br])
        pre = a + pltpu.roll(bm, n - 1, 0) + b1_ref[br]
        hid = _gelu_tanh(pre).astype(BF16)
        outs.append([_dot(hid[:, g * CMP_HIDDEN:(g + 1) * CMP_HIDDEN], w2_ref[br]) for g in range(KV_HEADS)])
    end_digits = _pos_digits(lax.broadcasted_iota(I32, (n, HEAD_DIM), 0) * CMP_STRIDE + (CMP_BLOCK - 1))
    for g in range(KV_HEADS):
        kc_ref[0, g] = jnp.concatenate([_head_rms(outs[0][g], knw_ref[0:1, :]), end_digits], axis=1).astype(BF16)
    vct = jnp.concatenate(outs[1], axis=1).T.astype(BF16)
    for g in range(KV_HEADS):
        vct_ref[0, g] = vct[g * HEAD_DIM:(g + 1) * HEAD_DIM, :]


def _compress(kc_raw, vc_raw, cmp_pos, cmp_w1, cmp_b1, cmp_w2, k_norm_w):
    b, s, _ = kc_raw.shape
    n = s // CMP_STRIDE
    half = CMP_STRIDE
    cw = CMP_STRIDE * KV_WIDTH
    kch = kc_raw.reshape(b, n, cw)
    vch = vc_raw.reshape(b, n, cw)
    pos = cmp_pos.reshape(2, 2, half, 1, HEAD_DIM)
    pos = jnp.broadcast_to(pos, (2, 2, half, KV_HEADS, HEAD_DIM)).reshape(2, 2, cw)
    w1 = cmp_w1.reshape(2, 2, half, HEAD_DIM, CMP_HIDDEN)
    eye = jnp.eye(KV_HEADS, dtype=F32)
    wfull = jnp.einsum('rhjdn,gk->rhjgdkn', w1, eye).reshape(2, 2, cw, KV_HEADS * CMP_HIDDEN).astype(BF16)
    b1 = jnp.tile(cmp_b1.reshape(2, 1, CMP_HIDDEN), (1, 1, KV_HEADS))
    fix = lambda r: (lambda bi: (0,) * r)
    return pl.pallas_call(
        _compress_kernel,
        grid=(b,),
        in_specs=[pl.BlockSpec((1, n, cw), lambda bi: (bi, 0, 0)),
                  pl.BlockSpec((1, n, cw), lambda bi: (bi, 0, 0)),
                  pl.BlockSpec((2, 2, cw), fix(3)),
                  pl.BlockSpec((2, cw, KV_HEADS * CMP_HIDDEN), fix(3)),
                  pl.BlockSpec((2, cw, KV_HEADS * CMP_HIDDEN), fix(3)),
                  pl.BlockSpec((2, 1, KV_HEADS * CMP_HIDDEN), fix(3)),
                  pl.BlockSpec((2, CMP_HIDDEN, HEAD_DIM), fix(3)),
                  pl.BlockSpec((3, HEAD_DIM), fix(2))],
        out_specs=(pl.BlockSpec((1, KV_HEADS, n, 2 * HEAD_DIM), lambda bi: (bi, 0, 0, 0)),
                   pl.BlockSpec((1, KV_HEADS, HEAD_DIM, n), lambda bi: (bi, 0, 0, 0))),
        out_shape=(jax.ShapeDtypeStruct((b, KV_HEADS, n, 2 * HEAD_DIM), BF16),
                   jax.ShapeDtypeStruct((b, KV_HEADS, HEAD_DIM, n), BF16)),
        compiler_params=_cparams("parallel"),
        name="compress",
    )(kch, vch, pos, wfull[:, 0], wfull[:, 1], b1, cmp_w2.astype(BF16), k_norm_w)


def _nsa_kernel(q_ref, kc_ref, vct_ref, ks_ref, vst_ref, kw_ref, vwt_ref, gt_ref, cdiff_ref, wdiff_ref,
                ovl_ref, oh_ref, onw_ref, wmask_ref, o_ref, buf_a, buf_b, m_scr, acc_scr, oc_scr, bias_scr,
                lst, cnt, *, n_top):
    q0 = pl.program_id(2) * TQ
    ncols = HEADS_PER_KV * TQ
    q = q_ref[0].reshape(ncols, 2 * HEAD_DIM)
    ns = ovl_ref.shape[0]

    def compress_and_select(nk, nb):
        s = jnp.where(cdiff_ref[0:nk, :] <= q0, _dot_nt(kc_ref[0, 0, 0:nk, :], q), -jnp.inf)
        m = jnp.max(s, axis=0, keepdims=True)
        m = jnp.where(m == -jnp.inf, 0.0, m)
        e = jnp.exp2(s - m)
        p = e / jnp.maximum(jnp.sum(e, axis=0, keepdims=True), 1e-30)
        oc_scr[...] = _dot(vct_ref[0, 0, :, 0:nk], p.astype(BF16))

        psum = p[:, 0:TQ]
        for hh in range(1, HEADS_PER_KV):
            psum = psum + p[:, hh * TQ:(hh + 1) * TQ]
        imp = _split_dot(ovl_ref[0:nb, 0:nk], psum)
        blk = lax.broadcasted_iota(I32, (nb, TQ), 0)
        tq = q0 + lax.broadcasted_iota(I32, (nb, TQ), 1)
        cur = tq >> 6
        forced = (blk == 0) | (blk == cur) | (blk == cur - 1)
        rank = jnp.where(forced, BIG, jnp.where(blk * SEL_BLOCK <= tq, imp, -BIG))
        blkf = blk.astype(F32)

        bias = jnp.full((nb, TQ), -1e30, F32)
        for _ in range(min(n_top, nb)):
            mx = jnp.max(rank, axis=0, keepdims=True)
            first = jnp.min(jnp.where(rank == mx, blkf, float(nb)), axis=0, keepdims=True)
            hit = blkf == first
            rank = jnp.where(hit, -jnp.inf, rank)
            bias = jnp.where(hit, 0.0, bias)
        bias_scr[...] = jnp.full((128, TQ), -1e30, F32)
        bias_scr[0:nb, :] = jnp.where(blk == 0, -1e30, bias)

    nc = kc_ref.shape[2]
    quarter = (q0 + TQ - 1) // (ks_ref.shape[2] // NSA_SIZE_VARIANTS)
    for v in range(NSA_SIZE_VARIANTS):
        @pl.when(quarter == v)
        def _():
            compress_and_select(nc * (v + 1) // NSA_SIZE_VARIANTS, ns * (v + 1) // NSA_SIZE_VARIANTS)

    o_c = oc_scr[...]
    bias = bias_scr[...]

    bias_t = bias.T.astype(BF16)
    qq = jnp.concatenate([q, jnp.concatenate([bias_t] * HEADS_PER_KV, axis=0)], axis=1)
    ones_rows = jnp.ones((16, TK), BF16)

    def scores(j):
        k0 = pl.multiple_of(j * TK, TK)
        kk = jnp.concatenate([ks_ref[0, 0, pl.ds(k0, TK), :], oh_ref[pl.ds(k0, TK), :]], axis=1)
        return _dot_nt(kk, qq)

    def consume(buf, j, causal, part):
        sc = buf[...]
        if causal:
            sc = jnp.where(wdiff_ref[0:TK, :] + (q0 - j * TK) >= 0, sc, -1e30)
        k0 = pl.multiple_of(j * TK, TK)
        m_run = m_scr[part]
        m_new = jnp.maximum(m_run, jnp.max(sc, axis=0, keepdims=True))
        ex = jnp.exp2(sc - m_new).astype(BF16)
        va = jnp.concatenate([vst_ref[0, 0, :, pl.ds(k0, TK)], ones_rows], axis=0)
        acc_scr[part] = jnp.exp2(m_run - m_new) * acc_scr[part] + _dot(va, ex)
        m_scr[part] = m_new

    n_past = q0 // TK
    blocks_per_tile = TK // SEL_BLOCK
    cnt[0] = 0
    for j in range(ks_ref.shape[2] // TK):
        wanted = jnp.max(bias[j * blocks_per_tile:(j + 1) * blocks_per_tile, :]) == 0.0

        @pl.when(wanted & (j < n_past))
        def _():
            lst[cnt[0]] = j
            cnt[0] = cnt[0] + 1

    n_sel = cnt[0]
    lst[n_sel] = n_past

    buf_a[...] = scores(lst[0])

    s0 = jnp.where(wdiff_ref[0:SEL_BLOCK, :] + q0 >= 0, _dot_nt(ks_ref[0, 0, 0:SEL_BLOCK, :], q), -1e30)
    m0 = jnp.max(s0, axis=0, keepdims=True)
    v0 = jnp.concatenate([vst_ref[0, 0, :, 0:SEL_BLOCK], jnp.ones((16, SEL_BLOCK), BF16)], axis=0)
    m_scr[0] = m0
    acc_scr[0] = _dot(v0, jnp.exp2(s0 - m0).astype(BF16))
    m_scr[1] = jnp.full((1, ncols), -1e30, F32)
    acc_scr[1] = jnp.zeros((HEAD_DIM + 16, ncols), F32)

    nw = WINDOW + TQ
    start = pl.multiple_of(jnp.maximum(q0 - WINDOW, 0), TQ)
    sw = _dot_nt(kw_ref[0, 0, pl.ds(start, nw), :], q) + wmask_ref[0]
    ew = jnp.exp2(sw - jnp.max(sw, axis=0, keepdims=True))
    vw_aug = jnp.concatenate([vwt_ref[0, 0, :, pl.ds(start, nw)], jnp.ones((16, nw), BF16)], axis=0)
    acc_w = _dot(vw_aug, ew.astype(BF16))
    o_w = acc_w[0:HEAD_DIM, :] / acc_w[HEAD_DIM:HEAD_DIM + 1, :]

    def tiles(first, count):
        for u in range(0, count, 2):
            buf_b[...] = scores(lst[first + u + 1])
            consume(buf_a, lst[first + u], False, 0)
            buf_a[...] = scores(lst[first + u + 2])
            consume(buf_b, lst[first + u + 1], False, 1)
        return 0

    lax.fori_loop(0, n_sel // 4, lambda i, _: tiles(4 * i, 4), 0)
    lax.fori_loop(0, (n_sel // 2) % 2, lambda i, _: tiles((n_sel // 4) * 4, 2), 0)

    @pl.when(n_sel % 2 == 1)
    def _():
        buf_b[...] = scores(n_past)
        consume(buf_a, lst[n_sel - 1], False, 0)
        consume(buf_b, n_past, True, 1)

    @pl.when(n_sel % 2 == 0)
    def _():
        consume(buf_a, n_past, True, 0)

    m_all = jnp.maximum(m_scr[0], m_scr[1])
    acc_s = jnp.exp2(m_scr[0] - m_all) * acc_scr[0] + jnp.exp2(m_scr[1] - m_all) * acc_scr[1]
    o_s = acc_s[0:HEAD_DIM, :] / acc_s[HEAD_DIM:HEAD_DIM + 1, :]

    gt = gt_ref[0, 0]
    outs = []
    for hh in range(HEADS_PER_KV):
        cs = slice(hh * TQ, (hh + 1) * TQ)
        o = (gt[3 * hh:3 * hh + 1, :] * o_c[:, cs] + gt[3 * hh + 1:3 * hh + 2, :] * o_s[:, cs]
             + gt[3 * hh + 2:3 * hh + 3, :] * o_w[:, cs])
        o = o * lax.rsqrt(jnp.mean(o * o, axis=0, keepdims=True) + RMS_EPS) * onw_ref[0, hh]
        outs.append(o)
    o_ref[0] = jnp.concatenate(outs, axis=0).T


def _nsa(q, kc, vct, ks, vst, kw, vwt, gates_t, attn_out_norm_w):
    b, _, s, aw = q.shape
    nc = kc.shape[2]
    ns = s // SEL_BLOCK
    n_top = min(N_SELECT, ns)
    ncols = HEADS_PER_KV * TQ
    nw = WINDOW + TQ
    tl = np.arange(ncols)[None, :] & (TQ - 1)
    cdiff = jnp.asarray((np.arange(nc)[:, None] * CMP_STRIDE + (CMP_BLOCK - 1) - tl).astype(np.int32))
    wdiff_np = (tl - np.arange(nw)[:, None]).astype(np.int32)
    wdiff = jnp.asarray(wdiff_np)
    n_off = WINDOW // TQ + 1
    dist_np = wdiff_np[None] + (np.arange(n_off) * TQ)[:, None, None]
    wmask = jnp.asarray(np.where((dist_np >= 0) & (dist_np < WINDOW), 0.0, -np.inf).astype(np.float32))
    ci = np.arange(nc)[None, :] * CMP_STRIDE
    bj = np.arange(ns)[:, None]
    ovl = ((ci < (bj + 1) * SEL_BLOCK) & (ci + CMP_BLOCK > bj * SEL_BLOCK) & (np.arange(nc)[None, :] < nc - 1))
    ovl = jnp.asarray(ovl.astype(np.float32)).astype(BF16)
    assert ns <= 128
    onehot = (np.arange(s)[:, None] // SEL_BLOCK == np.arange(128)[None, :])
    onehot = jnp.asarray(onehot.astype(np.float32)).astype(BF16)
    onw = jnp.broadcast_to(attn_out_norm_w.reshape(KV_HEADS, HEADS_PER_KV, HEAD_DIM, 1),
                           (KV_HEADS, HEADS_PER_KV, HEAD_DIM, TQ))
    gt = gates_t.reshape(b, KV_HEADS, HEADS_PER_KV * 3, s)
    per_bg = lambda bi, g, i: (bi, g, 0, 0)
    fixed = lambda bi, g, i: (0, 0)
    return pl.pallas_call(
        functools.partial(_nsa_kernel, n_top=n_top),
        grid=(b, KV_HEADS, s // TQ),
        in_specs=[pl.BlockSpec((1, HEADS_PER_KV, TQ, aw), lambda bi, g, i: (bi, g, i, 0)),
                  pl.BlockSpec((1, 1, nc, aw), per_bg),
                  pl.BlockSpec((1, 1, HEAD_DIM, nc), per_bg),
                  pl.BlockSpec((1, 1, s, aw), per_bg),
                  pl.BlockSpec((1, 1, HEAD_DIM, s), per_bg),
                  pl.BlockSpec((1, 1, s, aw), per_bg),
                  pl.BlockSpec((1, 1, HEAD_DIM, s), per_bg),
                  pl.BlockSpec((1, 1, HEADS_PER_KV * 3, TQ), lambda bi, g, i: (bi, g, 0, i)),
                  pl.BlockSpec((nc, ncols), fixed, pipeline_mode=pl.Buffered(1)),
                  pl.BlockSpec((nw, ncols), fixed, pipeline_mode=pl.Buffered(1)),
                  pl.BlockSpec((ns, nc), fixed, pipeline_mode=pl.Buffered(1)),
                  pl.BlockSpec((s, 128), fixed, pipeline_mode=pl.Buffered(1)),
                  pl.BlockSpec((1, HEADS_PER_KV, HEAD_DIM, TQ), lambda bi, g, i: (g, 0, 0, 0)),
                  pl.BlockSpec((1, nw, ncols), lambda bi, g, i: (jnp.minimum(i, n_off - 1), 0, 0))],
        out_specs=pl.BlockSpec((1, TQ, HEADS_PER_KV * HEAD_DIM), lambda bi, g, i: (bi, i, g)),
        out_shape=jax.ShapeDtypeStruct((b, s, NSA_WIDTH), F32),
        scratch_shapes=[pltpu.VMEM((TK, ncols), F32), pltpu.VMEM((TK, ncols), F32),
                        pltpu.VMEM((2, 1, ncols), F32), pltpu.VMEM((2, HEAD_DIM + 16, ncols), F32),
                        pltpu.VMEM((HEAD_DIM, ncols), F32), pltpu.VMEM((128, TQ), F32),
                        pltpu.SMEM((s // TK + 1,), I32), pltpu.SMEM((1,), I32)],
        compiler_params=_cparams("parallel", "parallel", "arbitrary"),
        name="nsa",
    )(q, kc, vct, ks, vst, kw, vwt, gt, cdiff, wdiff, ovl, onehot, onw, wmask)


def _hgrn_kernel(q_ref, k_ref, lf_ref, v_ref, g_ref, onw_ref, cm_ref, o_ref, state_scr, *, n_chunks):
    c = HGRN_CHUNK

    @pl.when(pl.program_id(1) == 0)
    def _():
        state_scr[...] = jnp.zeros_like(state_scr)

    ri = lax.broadcasted_iota(I32, (c, c), 0)
    ci = lax.broadcasted_iota(I32, (c, c), 1)
    rsub = ri // HGRN_SUB
    diag = ri == ci

    def head_chunk(r0, hd, state_t):
        cols = slice(hd * HGRN_DIM, (hd + 1) * HGRN_DIM)
        q = q_ref[0, pl.ds(r0, c), cols]
        k = k_ref[0, pl.ds(r0, c), cols]
        lf = lf_ref[0, pl.ds(r0, c), cols] * LOG2E
        v = v_ref[0, pl.ds(r0, c), cols]
        cm = cm_ref[...]
        l1 = lf.astype(BF16)
        rest = lf - l1.astype(F32)
        l2 = rest.astype(BF16)
        l3 = (rest - l2.astype(F32)).astype(BF16)
        cum = _dot(cm, l1) + _dot(cm, l2) + _dot(cm, l3)
        o = _dot_nt((q * jnp.exp2(cum)).astype(BF16), state_t.astype(BF16))
        scores = jnp.where(diag, jnp.sum(q * k, axis=-1, keepdims=True), 0.0)

        def factored(ref, mask, acc):
            qs = q * jnp.exp2(jnp.minimum(cum - ref, 0.0))
            kd = k * jnp.exp2(jnp.minimum(ref - cum, 0.0))
            return jnp.where(mask, _dot_nt(qs.astype(BF16), kd.astype(BF16)), acc)

        for i in range(1, c // HGRN_SUB):
            scores = factored(cum[i * HGRN_SUB - 1:i * HGRN_SUB, :], (rsub == i) & (ci < i * HGRN_SUB), scores)
        for d in range(1, HGRN_SUB):
            ksh = pltpu.roll(k, d, 0)
            csh = pltpu.roll(cum, d, 0)
            w = jnp.sum(q * ksh * jnp.exp2(cum - csh), axis=-1, keepdims=True)
            scores = jnp.where((ri - ci == d) & ((ri & (HGRN_SUB - 1)) >= d), w, scores)
        o = o + _dot(scores.astype(BF16), v.astype(BF16))
        last = cum[c - 1:c, :]
        kd = (k * jnp.exp2(last - cum)).astype(BF16)
        state_t = state_t * jnp.exp2(last) + _dot(v.T.astype(BF16), kd)
        o = o * g_ref[0, pl.ds(r0, c), cols]
        o = o * lax.rsqrt(jnp.mean(o * o, axis=-1, keepdims=True) + RMS_EPS) * onw_ref[:, cols]
        o_ref[0, pl.ds(r0, c), cols] = o
        return state_t

    def chunk(ck, states):
        r0 = pl.multiple_of(ck * c, c)
        return tuple(head_chunk(r0, hd, states[hd]) for hd in range(HGRN_HEADS))

    states = lax.fori_loop(0, n_chunks, chunk, tuple(state_scr[hd] for hd in range(HGRN_HEADS)))
    for hd in range(HGRN_HEADS):
        state_scr[hd] = states[hd]


def _hgrn(hq, hk, hlf, hv, hg, rec_out_norm_w, rows):
    b, s, _ = hq.shape
    cm = jnp.asarray(np.tril(np.ones((HGRN_CHUNK, HGRN_CHUNK), np.float32))).astype(BF16)
    blk = pl.BlockSpec((1, rows, HGRN_WIDTH), lambda bi, i: (bi, i, 0))
    return pl.pallas_call(
        functools.partial(_hgrn_kernel, n_chunks=rows // HGRN_CHUNK),
        grid=(b, s // rows),
        in_specs=[blk, blk, blk, blk, blk,
                  pl.BlockSpec((1, HGRN_WIDTH), lambda bi, i: (0, 0)),
                  pl.BlockSpec(cm.shape, lambda bi, i: (0, 0))],
        out_specs=blk,
        out_shape=jax.ShapeDtypeStruct((b, s, HGRN_WIDTH), F32),
        scratch_shapes=[pltpu.VMEM((HGRN_HEADS, HGRN_DIM, HGRN_DIM), F32)],
        compiler_params=_cparams("parallel", "arbitrary"),
        name="hgrn",
    )(hq, hk, hlf, hv, hg, rec_out_norm_w.reshape(1, HGRN_WIDTH), cm)


def _outproj_kernel(x_ref, a_ref, r_ref, wa_ref, wr_ref, gt_ref, sc_ref, sh_ref, n2_ref, x1_ref, h2_ref, h2p_ref):
    mixed = _dot(a_ref[0].astype(BF16), wa_ref[...]) + _dot(r_ref[0].astype(BF16), wr_ref[...])
    x1 = x_ref[0] + gt_ref[0] * mixed
    x1_ref[0] = x1
    ms = jnp.mean(x1 * x1, axis=-1, keepdims=True)
    h2 = x1 * lax.rsqrt(ms + RMS_EPS) * n2_ref[...] * (1.0 + sc_ref[0]) + sh_ref[0]
    h2_ref[0] = h2
    h2p_ref[0] = _pack_bf16_pair(h2[:, :D_MODEL // 2], h2[:, D_MODEL // 2:])


def _outproj(x, attn, rec, w_out, gt1, sc2, sh2, norm2_w, tm):
    b, s, d = x.shape
    row = lambda bi, i: (bi, i, 0)
    per_b = lambda bi, i: (bi, 0, 0)
    fixed2 = lambda bi, i: (0, 0)
    w = w_out.astype(BF16)
    return pl.pallas_call(
        _outproj_kernel,
        grid=(b, s // tm),
        in_specs=[pl.BlockSpec((1, tm, d), row),
                  pl.BlockSpec((1, tm, NSA_WIDTH), row),
                  pl.BlockSpec((1, tm, HGRN_WIDTH), row),
                  pl.BlockSpec((NSA_WIDTH, d), fixed2),
                  pl.BlockSpec((HGRN_WIDTH, d), fixed2),
                  pl.BlockSpec((1, 1, d), per_b),
                  pl.BlockSpec((1, 1, d), per_b),
                  pl.BlockSpec((1, 1, d), per_b),
                  pl.BlockSpec((1, d), fixed2)],
        out_specs=(pl.BlockSpec((1, tm, d), row), pl.BlockSpec((1, tm, d), row), pl.BlockSpec((1, tm, d // 2), row)),
        out_shape=(jax.ShapeDtypeStruct((b, s, d), F32), jax.ShapeDtypeStruct((b, s, d), F32),
                   jax.ShapeDtypeStruct((b, s, d // 2), jnp.uint32)),
        compiler_params=_cparams("parallel", "parallel"),
        name="outproj",
    )(x, attn, rec, w[:NSA_WIDTH], w[NSA_WIDTH:], gt1, sc2, sh2, norm2_w)


def _mixer(x, c, ada_w, ada_b, norm1_w, norm2_w, w_in, q_norm_w, k_norm_w, cmp_pos, cmp_w1, cmp_b1, cmp_w2,
           attn_out_norm_w, hgrn_lb_param, rec_out_norm_w, w_out):
    b, s, d = x.shape
    mod = _mod(c, ada_w, ada_b)
    sh1, sc1, gt1, sh2, sc2, gt2 = [m.reshape(b, 1, d) for m in jnp.split(mod, 6, axis=-1)]
    o = NSA_WIDTH + 6 * KV_WIDTH
    w_cat = jnp.concatenate([w_in[:, :o], w_in[:, o:o + NSA_HEADS * 3],
                             jnp.zeros((d, GATE_PAD - NSA_HEADS * 3), w_in.dtype),
                             w_in[:, o + NSA_HEADS * 3:]], axis=1).astype(BF16)
    tm = min(512, s)
    (q, kc_raw, vc_raw, ks, vst, kw, vwt, gates_t, hq, hk, hlf, hv, hg) = _inproj(
        x, sc1, sh1, norm1_w.reshape(1, d), w_cat, q_norm_w.reshape(1, HEAD_DIM), k_norm_w, hgrn_lb_param, tm)
    kc, vct = _compress(kc_raw, vc_raw, cmp_pos, cmp_w1, cmp_b1, cmp_w2, k_norm_w)
    attn = _nsa(q, kc, vct, ks, vst, kw, vwt, gates_t, attn_out_norm_w)
    rec = _hgrn(hq, hk, hlf, hv, hg, rec_out_norm_w, min(1024, s))
    x1, h2, h2p = _outproj(x, attn, rec, w_out, gt1, sc2, sh2, norm2_w.reshape(1, d), min(1024, s))
    return x1, h2, h2p, gt2


def _router_kernel(h_ref, rwt_ref, bias_ref, tri_ref, ones_ref, idx_ref, w_ref, rank_ref, cnt_ref, carry_scr, *, tr):
    @pl.when(pl.program_id(0) == 0)
    def _():
        carry_scr[...] = jnp.zeros_like(carry_scr)

    h = h_ref[...]
    h_hi = h.astype(BF16)
    h_lo = (h - h_hi.astype(F32)).astype(BF16)
    logits = _dot_nt(rwt_ref[0], h_hi) + _dot_nt(rwt_ref[1], h_hi) + _dot_nt(rwt_ref[0], h_lo)
    scores = _sigmoid(logits)
    biased = scores + bias_ref[...]
    neg = -jnp.inf

    gs = []
    for g in range(N_GROUPS):
        sub = biased[g * GROUP_SIZE:(g + 1) * GROUP_SIZE, :]
        m1 = jnp.max(sub, axis=0, keepdims=True)
        dup = jnp.sum((sub == m1).astype(F32), axis=0, keepdims=True)
        m2 = jnp.max(jnp.where(sub < m1, sub, neg), axis=0, keepdims=True)
        gs.append(m1 + jnp.where(dup >= 2.0, m1, m2))
    parts = []
    for g in range(N_GROUPS):
        beaten = jnp.zeros_like(gs[g])
        for g2 in range(N_GROUPS):
            if g2 != g:
                beats = (gs[g2] >= gs[g]) if g2 < g else (gs[g2] > gs[g])
                beaten = beaten + beats.astype(F32)
        sub = biased[g * GROUP_SIZE:(g + 1) * GROUP_SIZE, :]
        parts.append(jnp.where(beaten < float(TOPK_GROUPS), sub, neg))
    cand = jnp.concatenate(parts, axis=0)

    rowf = lax.broadcasted_iota(I32, (N_EXPERTS, tr), 0).astype(F32)
    idx_rows, w_rows, hits = [], [], []
    multi = jnp.zeros((N_EXPERTS, tr), F32)
    for _ in range(TOP_K):
        mx = jnp.max(cand, axis=0, keepdims=True)
        first = jnp.min(jnp.where(cand == mx, rowf, float(N_EXPERTS)), axis=0, keepdims=True)
        hit = rowf == first
        idx_rows.append(first)
        w_rows.append(jnp.sum(jnp.where(hit, scores, 0.0), axis=0, keepdims=True))
        cand = jnp.where(hit, neg, cand)
        multi = jnp.where(hit, 1.0, multi)
    w = jnp.concatenate(w_rows, axis=0)
    w_ref[...] = w / jnp.sum(w, axis=0, keepdims=True) * ROUTED_SCALE
    idx = jnp.concatenate(idx_rows, axis=0)
    idx_ref[...] = idx.astype(I32)

    carry = carry_scr[...]
    mb = multi.astype(BF16)
    before = _dot(mb, tri_ref[...]) + jnp.concatenate([carry] * (tr // 128), axis=1)
    rank_rows = [jnp.sum(jnp.where(rowf == idx_rows[k], before, 0.0), axis=0, keepdims=True) for k in range(TOP_K)]
    rank_ref[...] = jnp.concatenate(rank_rows, axis=0).astype(I32)
    carry = carry + _dot(mb, ones_ref[...])
    carry_scr[...] = carry
    cnt_ref[...] = carry


def _router(h2, router_w, router_bias, tr):
    t, d = h2.shape
    tri = jnp.asarray(np.triu(np.ones((tr, tr), np.float32), 1)).astype(BF16)
    ones = jnp.ones((tr, 128), BF16)
    tok = pl.BlockSpec((TOP_K, tr), lambda i: (0, i))
    fixed = lambda i: (0, 0)
    rwt = router_w.T
    rwt_hi = rwt.astype(BF16)
    rwt_split = jnp.stack([rwt_hi, (rwt - rwt_hi.astype(F32)).astype(BF16)])
    return pl.pallas_call(
        functools.partial(_router_kernel, tr=tr),
        grid=(t // tr,),
        in_specs=[pl.BlockSpec((tr, d), lambda i: (i, 0)),
                  pl.BlockSpec((2, N_EXPERTS, d), lambda i: (0, 0, 0)),
                  pl.BlockSpec((N_EXPERTS, 1), fixed),
                  pl.BlockSpec((tr, tr), fixed),
                  pl.BlockSpec((tr, 128), fixed)],
        out_specs=(tok, tok, tok, pl.BlockSpec((N_EXPERTS, 128), fixed)),
        out_shape=(jax.ShapeDtypeStruct((TOP_K, t), I32), jax.ShapeDtypeStruct((TOP_K, t), F32),
                   jax.ShapeDtypeStruct((TOP_K, t), I32), jax.ShapeDtypeStruct((N_EXPERTS, 128), F32)),
        scratch_shapes=[pltpu.VMEM((N_EXPERTS, 128), F32)],
        compiler_params=_cparams("arbitrary"),
        name="router",
    )(h2, rwt_split, router_bias.reshape(N_EXPERTS, 1), tri, ones)


def _pack_bf16_pair(a, b):
    ua = lax.bitcast_convert_type(a.astype(BF16).astype(F32), jnp.uint32)
    ub = lax.bitcast_convert_type(b.astype(BF16).astype(F32), jnp.uint32)
    return ua | (ub >> 16)


def _unpack_bf16_pair(w):
    a = lax.bitcast_convert_type(w & jnp.uint32(0xFFFF0000), F32)
    b = lax.bitcast_convert_type(w << 16, F32)
    return a, b


def _slot_kernel(ps_ref, idx_ref, rank_ref, slot_ref):
    idx = idx_ref[...]

    def body(e, acc):
        return jnp.where(idx == e, ps_ref[e], acc)

    slot_ref[...] = lax.fori_loop(0, N_EXPERTS, body, jnp.zeros_like(idx)) + rank_ref[...]


def _slots(pad_start, idx, rank, tt):
    t = idx.shape[1]
    tok = pl.BlockSpec((TOP_K, tt), lambda i, ps: (0, i))
    return pl.pallas_call(
        _slot_kernel,
        grid_spec=pltpu.PrefetchScalarGridSpec(num_scalar_prefetch=1, grid=(t // tt,),
                                               in_specs=[tok, tok], out_specs=tok),
        out_shape=jax.ShapeDtypeStruct((TOP_K, t), I32),
        compiler_params=_cparams("parallel"),
        name="slots",
    )(pad_start, idx, rank)


SC_CORES = 2
SC_SUBCORES = 16
SC_CHUNK = 64


def _sc_mesh():
    return plsc.VectorSubcoreMesh(core_axis_name="c", subcore_axis_name="s")


def _sc_dispatch(h2p, slot_chunks, n_rows):
    t, dw = h2p.shape
    per = slot_chunks.shape[0] // (SC_CORES * SC_SUBCORES)

    def body(h_hbm, slot_hbm, xs_hbm, idx_v, rows_v, sem):
        wid = lax.axis_index("s") * SC_CORES + lax.axis_index("c")

        @pl.loop(0, per)
        def _(c):
            ch = wid * per + c
            pltpu.sync_copy(slot_hbm.at[ch], idx_v)
            pltpu.sync_copy(h_hbm.at[pl.ds(ch * SC_CHUNK, SC_CHUNK)], rows_v)
            copies = [pltpu.async_copy(rows_v, xs_hbm.at[idx_v.at[k]], sem) for k in range(TOP_K)]
            for cp in copies:
                cp.wait()

    return pl.kernel(
        body, out_type=jax.ShapeDtypeStruct((n_rows, dw), h2p.dtype), mesh=_sc_mesh(),
        scratch_types=[pltpu.VMEM((TOP_K, SC_CHUNK), I32), pltpu.VMEM((SC_CHUNK, dw), h2p.dtype),
                       pltpu.SemaphoreType.DMA],
    )(h2p, slot_chunks)


def _sc_gather(ys, slot_chunks, t):
    dw = ys.shape[1]
    per = slot_chunks.shape[0] // (SC_CORES * SC_SUBCORES)

    def body(ys_hbm, slot_hbm, yg_hbm, idx_v, rows_v, gsem, wsem):
        wid = lax.axis_index("s") * SC_CORES + lax.axis_index("c")

        @pl.loop(0, per)
        def _(c):
            ch = wid * per + c
            pltpu.sync_copy(slot_hbm.at[ch], idx_v)
            gathers = [None] * TOP_K
            writes = [None] * TOP_K
            gathers[0] = pltpu.async_copy(ys_hbm.at[idx_v.at[0]], rows_v.at[0], gsem)
            for k in range(TOP_K):
                gathers[k].wait()
                if k + 1 < TOP_K:
                    if k >= 1:
                        writes[k - 1].wait()
                    gathers[k + 1] = pltpu.async_copy(ys_hbm.at[idx_v.at[k + 1]], rows_v.at[(k + 1) % 2], gsem)
                writes[k] = pltpu.async_copy(rows_v.at[k % 2], yg_hbm.at[k, pl.ds(ch * SC_CHUNK, SC_CHUNK)], wsem)
            writes[TOP_K - 2].wait()
            writes[TOP_K - 1].wait()

    return pl.kernel(
        body, out_type=jax.ShapeDtypeStruct((TOP_K, t, dw), ys.dtype), mesh=_sc_mesh(),
        scratch_types=[pltpu.VMEM((TOP_K, SC_CHUNK), I32), pltpu.VMEM((2, SC_CHUNK, dw), ys.dtype),
                       pltpu.SemaphoreType.DMA, pltpu.SemaphoreType.DMA],
    )(ys, slot_chunks)


def _experts_kernel(be_ref, nu_ref, bv_ref, run_ref, xs_hbm, wg_hbm, wu_hbm, wd_hbm, ys_ref,
                    xring, rsem, gring, uring, dring, wsem):
    i = pl.program_id(0)
    half = D_MODEL // 2
    n_used = nu_ref[0]
    n_steps = pl.num_programs(0)

    def weight_copies(blk):
        ex = be_ref[blk]
        slot = run_ref[blk] % EXPERT_RING
        return [pltpu.make_async_copy(src.at[ex], ring.at[slot], wsem.at[a, slot])
                for a, (src, ring) in enumerate(((wg_hbm, gring), (wu_hbm, uring), (wd_hbm, dring)))]

    def starts_run(blk):
        return run_ref[blk] != run_ref[jnp.maximum(blk - 1, 0)]

    @pl.when(i == 0)
    def _():
        for cp in weight_copies(jnp.int32(0)):
            cp.start()

        @pl.when((n_steps > 1) & starts_run(jnp.int32(1)))
        def _():
            for cp in weight_copies(jnp.int32(1)):
                cp.start()

    ahead = jnp.minimum(i + (EXPERT_RING - 1), n_steps - 1)

    @pl.when((i + (EXPERT_RING - 1) < n_steps) & starts_run(ahead))
    def _():
        for cp in weight_copies(ahead):
            cp.start()

    @pl.when((i == 0) | starts_run(i))
    def _():
        for cp in weight_copies(i):
            cp.wait()

    wslot = run_ref[i] % EXPERT_RING
    wg_ref, wu_ref, wd_ref = gring.at[wslot], uring.at[wslot], dring.at[wslot]

    def fetch(blk):
        slot = blk % EXPERT_RING
        return pltpu.make_async_copy(xs_hbm.at[pl.ds(pl.multiple_of(blk * EXPERT_BLOCK, EXPERT_BLOCK), EXPERT_BLOCK)],
                                     xring.at[slot], rsem.at[slot])

    @pl.when(i == 0)
    def _():
        for first in range(EXPERT_RING - 1):
            @pl.when(first < n_used)
            def _():
                fetch(jnp.int32(first)).start()

    @pl.when(i + (EXPERT_RING - 1) < n_used)
    def _():
        fetch(i + (EXPERT_RING - 1)).start()

    @pl.when(i < n_used)
    def _():
        fetch(i).wait()

    xs_ref = xring.at[i % EXPERT_RING]

    def ffn(rows):
        live = lax.broadcasted_iota(I32, (rows, xs_ref.shape[1]), 0) < bv_ref[i]
        xa, xb = _unpack_bf16_pair(jnp.where(live, xs_ref[0:rows, :], jnp.uint32(0)))
        xa, xb = xa.astype(BF16), xb.astype(BF16)
        g = _dot(xa, wg_ref[:half, :].astype(BF16)) + _dot(xb, wg_ref[half:, :].astype(BF16))
        u = _dot(xa, wu_ref[:half, :].astype(BF16)) + _dot(xb, wu_ref[half:, :].astype(BF16))
        act = (g * _sigmoid(g) * u).astype(BF16)
        y = _dot(act, wd_ref[...].astype(BF16))
        ys_ref[0:rows, :] = _pack_bf16_pair(y[:, :half], y[:, half:])

    used = i < n_used
    short = bv_ref[i] <= EXPERT_TAIL

    @pl.when(used & jnp.logical_not(short))
    def _():
        ffn(EXPERT_BLOCK)

    @pl.when(used & short)
    def _():
        ffn(EXPERT_TAIL)
        ys_ref[EXPERT_TAIL:, :] = jnp.zeros((EXPERT_BLOCK - EXPERT_TAIL, ys_ref.shape[1]), ys_ref.dtype)

    @pl.when(jnp.logical_not(used))
    def _():
        ys_ref[...] = jnp.zeros_like(ys_ref)


def _experts(xs, blk_e, n_used, blk_valid, w_gate, w_up, w_down):
    n_rows, dw = xs.shape
    d = w_gate.shape[1]
    nblk = n_rows // EXPERT_BLOCK
    blk_run = jnp.cumsum(jnp.concatenate([jnp.zeros((1,), I32), (blk_e[1:] != blk_e[:-1]).astype(I32)])).astype(I32)
    hbm = pl.BlockSpec(memory_space=pl.ANY)
    return pl.pallas_call(
        _experts_kernel,
        grid_spec=pltpu.PrefetchScalarGridSpec(
            num_scalar_prefetch=4,
            grid=(nblk,),
            in_specs=[hbm, hbm, hbm, hbm],
            out_specs=pl.BlockSpec((EXPERT_BLOCK, dw), lambda i, be, nu, bv, rn: (i, 0)),
            scratch_shapes=[pltpu.VMEM((EXPERT_RING, EXPERT_BLOCK, dw), xs.dtype),
                            pltpu.SemaphoreType.DMA((EXPERT_RING,)),
                            pltpu.VMEM((EXPERT_RING, d, EXPERT_FF), w_gate.dtype),
                            pltpu.VMEM((EXPERT_RING, d, EXPERT_FF), w_up.dtype),
                            pltpu.VMEM((EXPERT_RING, EXPERT_FF, d), w_down.dtype),
                            pltpu.SemaphoreType.DMA((3, EXPERT_RING))]),
        out_shape=jax.ShapeDtypeStruct((n_rows, dw), xs.dtype),
        compiler_params=pltpu.CompilerParams(dimension_semantics=("arbitrary",), vmem_limit_bytes=VMEM_LIMIT,
                                             has_side_effects=True),
        name="experts",
    )(blk_e, n_used, blk_valid, blk_run, xs, w_gate, w_up, w_down)


def _combine_kernel(x1_ref, h_ref, w_ref, gt_ref, sg_ref, su_ref, sd_ref, yg_ref, o_ref):
    tc = x1_ref.shape[0]
    half = D_MODEL // 2
    hb = h_ref[...].astype(BF16)
    g = _dot(hb, sg_ref[...])
    u = _dot(hb, su_ref[...])
    ffn = _dot((g * _sigmoid(g) * u).astype(BF16), sd_ref[...])

    w = w_ref[...]
    ra = jnp.zeros((tc, half), F32)
    rb = jnp.zeros((tc, half), F32)
    for k in range(TOP_K):
        ya, yb = _unpack_bf16_pair(yg_ref[k])
        ra = ra + w[:, k:k + 1] * ya
        rb = rb + w[:, k:k + 1] * yb
    ffn = ffn + jnp.concatenate([ra, rb], axis=1)
    o_ref[...] = x1_ref[...] + gt_ref[0] * ffn


def _combine(x1, h2, w_tok, gt2, yg, sg, su, sd, seq, tc):
    t, d = x1.shape
    row = lambda i: (i, 0)
    fixed = lambda i: (0, 0)
    return pl.pallas_call(
        _combine_kernel,
        grid=(t // tc,),
        in_specs=[pl.BlockSpec((tc, d), row),
                  pl.BlockSpec((tc, d), row),
                  pl.BlockSpec((tc, TOP_K), row),
                  pl.BlockSpec((1, 1, d), lambda i: ((i * tc) // seq, 0, 0)),
                  pl.BlockSpec((d, SHARED_FF), fixed),
                  pl.BlockSpec((d, SHARED_FF), fixed),
                  pl.BlockSpec((SHARED_FF, d), fixed),
                  pl.BlockSpec((TOP_K, tc, d // 2), lambda i: (0, i, 0))],
        out_specs=pl.BlockSpec((tc, d), row),
        out_shape=jax.ShapeDtypeStruct((t, d), F32),
        compiler_params=_cparams("parallel"),
        name="combine",
    )(x1, h2, w_tok, gt2, sg.astype(BF16), su.astype(BF16), sd.astype(BF16), yg)


def _moe_parts(x1, h2, h2p, gt2, router_w, router_bias, w_gate, w_up, w_down, sg, su, sd):
    b, s, d = x1.shape
    t = b * s
    assert t % (SC_CHUNK * SC_CORES * SC_SUBCORES) == 0, "token chunks must split evenly over the vector subcores"
    h2 = h2.reshape(t, d)
    idx, w, rank, cnt = _router(h2, router_w, router_bias, min(256, t))
    counts = cnt[:, 0].astype(I32)
    padded = (counts + EXPERT_BLOCK - 1) // EXPERT_BLOCK * EXPERT_BLOCK
    pad_end = jnp.cumsum(padded)
    pad_start = pad_end - padded
    n_rows = t * TOP_K + N_EXPERTS * EXPERT_BLOCK
    nblk = n_rows // EXPERT_BLOCK
    n_used = (pad_end[-1:] // EXPERT_BLOCK).astype(I32)
    blk_start = jnp.arange(nblk, dtype=I32) * EXPERT_BLOCK
    owns = (pad_start[None, :] <= blk_start[:, None]) & (blk_start[:, None] < pad_end[None, :])
    e_ids = jnp.arange(N_EXPERTS, dtype=I32)[None, :]
    last_e = jnp.max(jnp.where(counts > 0, e_ids[0], 0))
    blk_e = jnp.where(blk_start < pad_end[-1], jnp.sum(jnp.where(owns, e_ids, 0), axis=1), last_e).astype(I32)
    rows_left = jnp.sum(jnp.where(owns, (pad_start + counts)[None, :] - blk_start[:, None], 0), axis=1)
    blk_valid = jnp.clip(rows_left, 0, EXPERT_BLOCK).astype(I32)
    slot = _slots(pad_start.astype(I32), idx, rank, min(2048, t))
    slot_chunks = slot.reshape(TOP_K, t // SC_CHUNK, SC_CHUNK).transpose(1, 0, 2)
    xs = _sc_dispatch(h2p.reshape(t, d // 2), slot_chunks, n_rows)
    ys = _experts(xs, blk_e, n_used, blk_valid, w_gate, w_up, w_down)
    yg = _sc_gather(ys, slot_chunks, t)
    out = _combine(x1.reshape(t, d), h2, w.T, gt2, yg, sg, su, sd, s, min(512, t))
    return out.reshape(b, s, d), dict(idx=idx, w=w, rank=rank, cnt=cnt)


def kernel(x, c, ada_w, ada_b, norm1_w, norm2_w, w_in, q_norm_w, k_norm_w, cmp_pos, cmp_w1, cmp_b1, cmp_w2, attn_out_norm_w, hgrn_lb_param, rec_out_norm_w, w_out, router_w, router_bias, exp_w_gate, exp_w_up, exp_w_down, shared_w_gate, shared_w_up, shared_w_down):
    assert ada_w.shape[0] == 1, "one layer"
    assert x.shape[0] <= 8 and x.shape[1] % TK == 0 and x.shape[1] >= WINDOW + TQ
    l = 0
    x1, h2, h2p, gt2 = _mixer(x, c, ada_w[l], ada_b[l], norm1_w[l], norm2_w[l], w_in[l], q_norm_w[l], k_norm_w[l],
                         cmp_pos[l], cmp_w1[l], cmp_b1[l], cmp_w2[l], attn_out_norm_w[l], hgrn_lb_param,
                         rec_out_norm_w[l], w_out[l])
    out, _ = _moe_parts(x1, h2, h2p, gt2, router_w[l], router_bias[l], exp_w_gate[l], exp_w_up[l], exp_w_down[l],
                        shared_w_gate[l], shared_w_up[l], shared_w_down[l])
    return out
```

```python
import functools

import numpy as np
import jax
import jax.numpy as jnp
from jax import lax
from jax.experimental import pallas as pl
from jax.experimental.pallas import tpu as pltpu
from jax.experimental.pallas import tpu_sc as plsc

F32 = jnp.float32
BF16 = jnp.bfloat16
I32 = jnp.int32

D_MODEL = 1024
NSA_HEADS = 8
HEAD_DIM = 64
NSA_WIDTH = NSA_HEADS * HEAD_DIM
KV_HEADS = 2
HEADS_PER_KV = NSA_HEADS // KV_HEADS
KV_WIDTH = KV_HEADS * HEAD_DIM
CMP_BLOCK = 32
CMP_STRIDE = 16
CMP_HIDDEN = 256
SEL_BLOCK = 64
N_SELECT = 16
WINDOW = 512
HGRN_HEADS = 4
HGRN_DIM = 128
HGRN_WIDTH = HGRN_HEADS * HGRN_DIM
HGRN_CHUNK = 64
HGRN_SUB = 16
N_EXPERTS = 256
TOP_K = 8
N_GROUPS = 8
GROUP_SIZE = N_EXPERTS // N_GROUPS
TOPK_GROUPS = 4
EXPERT_FF = 256
SHARED_FF = 256
ROUTED_SCALE = 2.5
RMS_EPS = 1e-6
BIG = 1e9
LOG2E = 1.4426950408889634
GATE_PAD = 128
PROJ_COLS = NSA_WIDTH + 6 * KV_WIDTH + GATE_PAD + 4 * HGRN_WIDTH

VMEM_LIMIT = 56 * 1024 * 1024

TQ = 256
TK = 256
NSA_SIZE_VARIANTS = 4
EXPERT_BLOCK = 512
EXPERT_TAIL = 128
EXPERT_RING = 3
HIGHEST = lax.Precision.HIGHEST


def _cparams(*sem):
    return pltpu.CompilerParams(dimension_semantics=sem, vmem_limit_bytes=VMEM_LIMIT)


def _sigmoid(x):
    return 1.0 / (1.0 + jnp.exp(-x))


def _dot_nt(a, b):
    return lax.dot_general(a, b, (((1,), (1,)), ((), ())), preferred_element_type=F32)


def _dot(a, b, **kw):
    return jnp.dot(a, b, preferred_element_type=F32, **kw)


def _split_dot(a_bf16_exact, x):
    hi = x.astype(BF16)
    lo = (x - hi.astype(F32)).astype(BF16)
    return _dot(a_bf16_exact, hi) + _dot(a_bf16_exact, lo)


def _mod_kernel(c_ref, w_ref, b_ref, o_ref):
    c = c_ref[...]
    cond = c * _sigmoid(c)
    o_ref[...] = _dot(cond, w_ref[...], precision=HIGHEST) + b_ref[...]


def _mod(c, ada_w, ada_b):
    b, d = c.shape
    rows = 8
    c_pad = jnp.zeros((rows, d), F32).at[:b].set(c)
    n = ada_w.shape[1]
    out = pl.pallas_call(
        _mod_kernel,
        grid=(n // d,),
        in_specs=[pl.BlockSpec((rows, d), lambda j: (0, 0)),
                  pl.BlockSpec((d, d), lambda j: (0, j)),
                  pl.BlockSpec((1, d), lambda j: (0, j))],
        out_specs=pl.BlockSpec((rows, d), lambda j: (0, j)),
        out_shape=jax.ShapeDtypeStruct((rows, n), F32),
        compiler_params=_cparams("parallel"),
        name="mod",
    )(c_pad, ada_w, ada_b.reshape(1, n))
    return out[:b]


def _head_rms(t, w):
    return t * lax.rsqrt(jnp.mean(t * t, axis=-1, keepdims=True) + RMS_EPS) * w


def _pos_digits(pos):
    lane = lax.broadcasted_iota(I32, pos.shape, 1)
    d0 = (lane == 0) | (lane == 3) | (lane == 6)
    d1 = (lane == 1) | (lane == 4) | (lane == 7)
    d2 = (lane == 2) | (lane == 5) | (lane == 8)
    dig = jnp.where(d0, pos >> 12, jnp.where(d1, (pos >> 6) & 63, jnp.where(d2, pos & 63, 0)))
    return dig.astype(F32)


def _inproj_kernel(x_ref, sc_ref, sh_ref, n1_ref, w_ref, qnw_ref, knw_ref, lbp_ref, qaug_ref,
                   q_ref, kcr_ref, vcr_ref, ks_ref, vst_ref, kw_ref, vwt_ref, gt_ref,
                   hq_ref, hk_ref, hlf_ref, hv_ref, hg_ref):
    x = x_ref[0]
    ms = jnp.mean(x * x, axis=-1, keepdims=True)
    h = x * lax.rsqrt(ms + RMS_EPS) * n1_ref[...] * (1.0 + sc_ref[0]) + sh_ref[0]
    p = _dot(h.astype(BF16), w_ref[...])
    tm = x.shape[0]

    qnw = qnw_ref[...]
    for hd in range(NSA_HEADS):
        t = p[:, hd * HEAD_DIM:(hd + 1) * HEAD_DIM]
        qn = _head_rms(t, qnw) * (HEAD_DIM ** -0.5 * LOG2E)
        qa = jnp.broadcast_to(qaug_ref[hd:hd + 1, :], (tm, HEAD_DIM))
        q_ref[0, hd] = jnp.concatenate([qn, qa], axis=1).astype(BF16)
    kaug = _pos_digits(pl.program_id(1) * tm + lax.broadcasted_iota(I32, (tm, HEAD_DIM), 0))

    o = NSA_WIDTH
    kcr_ref[0] = p[:, o:o + KV_WIDTH]
    vcr_ref[0] = p[:, o + KV_WIDTH:o + 2 * KV_WIDTH]
    ks = p[:, o + 2 * KV_WIDTH:o + 3 * KV_WIDTH]
    vs = p[:, o + 3 * KV_WIDTH:o + 4 * KV_WIDTH]
    kw = p[:, o + 4 * KV_WIDTH:o + 5 * KV_WIDTH]
    vw = p[:, o + 5 * KV_WIDTH:o + 6 * KV_WIDTH]
    for g in range(KV_HEADS):
        sl = slice(g * HEAD_DIM, (g + 1) * HEAD_DIM)
        ks_ref[0, g] = jnp.concatenate([_head_rms(ks[:, sl], knw_ref[1:2, :]), kaug], axis=1).astype(BF16)
        kw_ref[0, g] = jnp.concatenate([_head_rms(kw[:, sl], knw_ref[2:3, :]), kaug], axis=1).astype(BF16)
    vst = vs.T.astype(BF16)
    vwt = vw.T.astype(BF16)
    for g in range(KV_HEADS):
        vst_ref[0, g] = vst[g * HEAD_DIM:(g + 1) * HEAD_DIM, :]
        vwt_ref[0, g] = vwt[g * HEAD_DIM:(g + 1) * HEAD_DIM, :]

    o = NSA_WIDTH + 6 * KV_WIDTH
    gates = _sigmoid(p[:, o:o + GATE_PAD])
    gt_ref[0] = gates.T[:NSA_HEADS * 3, :]

    o = o + GATE_PAD
    hq = p[:, o:o + HGRN_WIDTH]
    hf = p[:, o + HGRN_WIDTH:o + 2 * HGRN_WIDTH]
    hi = p[:, o + 2 * HGRN_WIDTH:o + 3 * HGRN_WIDTH]
    hg = p[:, o + 3 * HGRN_WIDTH:o + 4 * HGRN_WIDTH]
    lbp = lbp_ref[...]
    e = jnp.exp(lbp - jnp.max(lbp, axis=0, keepdims=True))
    lb = e[0:1, :] / jnp.sum(e, axis=0, keepdims=True)
    f = lb + (1.0 - lb) * _sigmoid(hf)
    hq_ref[0] = hq * _sigmoid(hq) * (HGRN_DIM ** -0.5)
    hk_ref[0] = 1.0 - f
    hlf_ref[0] = jnp.log(f)
    hv_ref[0] = hi
    hg_ref[0] = _sigmoid(hg)


def _inproj(x, sc1, sh1, norm1_w, w_cat, q_norm_w, k_norm_w, lb_param, tm):
    b, s, d = x.shape
    row = lambda bi, i: (bi, i, 0)
    per_b = lambda bi, i: (bi, 0, 0)
    fixed2 = lambda bi, i: (0, 0)
    aw = 2 * HEAD_DIM
    rest = np.array([2.0 ** (-8.0 * (i + 1) / NSA_HEADS) for i in range(NSA_HEADS)], np.float64) * LOG2E
    qaug = np.zeros((NSA_HEADS, HEAD_DIM), np.float32)
    for i in range(3):
        term = rest.astype(np.float32).astype(BF16).astype(np.float64)
        rest = rest - term
        for dgt, wgt in enumerate((4096.0, 64.0, 1.0)):
            qaug[:, 3 * i + dgt] = term * wgt
    assert np.all(qaug == qaug.astype(BF16).astype(np.float32))
    out_shape = (
        jax.ShapeDtypeStruct((b, NSA_HEADS, s, aw), BF16),
        jax.ShapeDtypeStruct((b, s, KV_WIDTH), F32),
        jax.ShapeDtypeStruct((b, s, KV_WIDTH), F32),
        jax.ShapeDtypeStruct((b, KV_HEADS, s, aw), BF16),
        jax.ShapeDtypeStruct((b, KV_HEADS, HEAD_DIM, s), BF16),
        jax.ShapeDtypeStruct((b, KV_HEADS, s, aw), BF16),
        jax.ShapeDtypeStruct((b, KV_HEADS, HEAD_DIM, s), BF16),
        jax.ShapeDtypeStruct((b, NSA_HEADS * 3, s), F32),
    ) + tuple(jax.ShapeDtypeStruct((b, s, HGRN_WIDTH), F32) for _ in range(5))
    hm = lambda n, w: pl.BlockSpec((1, n, tm, w), lambda bi, i: (bi, 0, i, 0))
    hmt = lambda n, w: pl.BlockSpec((1, n, w, tm), lambda bi, i: (bi, 0, 0, i))
    out_specs = (
        hm(NSA_HEADS, aw),
        pl.BlockSpec((1, tm, KV_WIDTH), row),
        pl.BlockSpec((1, tm, KV_WIDTH), row),
        hm(KV_HEADS, aw), hmt(KV_HEADS, HEAD_DIM),
        hm(KV_HEADS, aw), hmt(KV_HEADS, HEAD_DIM),
        pl.BlockSpec((1, NSA_HEADS * 3, tm), lambda bi, i: (bi, 0, i)),
    ) + tuple(pl.BlockSpec((1, tm, HGRN_WIDTH), row) for _ in range(5))
    return pl.pallas_call(
        _inproj_kernel,
        grid=(b, s // tm),
        in_specs=[pl.BlockSpec((1, tm, d), row),
                  pl.BlockSpec((1, 1, d), per_b),
                  pl.BlockSpec((1, 1, d), per_b),
                  pl.BlockSpec((1, d), fixed2),
                  pl.BlockSpec((d, PROJ_COLS), fixed2),
                  pl.BlockSpec((1, HEAD_DIM), fixed2),
                  pl.BlockSpec((3, HEAD_DIM), fixed2),
                  pl.BlockSpec(lb_param.shape, fixed2),
                  pl.BlockSpec((NSA_HEADS, HEAD_DIM), fixed2)],
        out_specs=out_specs,
        out_shape=out_shape,
        compiler_params=_cparams("parallel", "parallel"),
        name="inproj",
    )(x, sc1, sh1, norm1_w, w_cat, q_norm_w, k_norm_w, lb_param, jnp.asarray(qaug))


def _gelu_tanh(x):
    return 0.5 * x * (1.0 + jnp.tanh(0.7978845608028654 * (x + 0.044715 * x * x * x)))


def _compress_kernel(kch_ref, vch_ref, pos_ref, wa_ref, wb_ref, b1_ref, w2_ref, knw_ref,
                     kc_ref, vct_ref):
    n = kch_ref.shape[1]
    outs = []
    for br, ch_ref in enumerate((kch_ref, vch_ref)):
        ch = ch_ref[0]
        a = _dot((ch + pos_ref[br, 0:1, :]).astype(BF16), wa_ref[br])
        bm = _dot((ch + pos_ref[br, 1:2, :]).astype(BF16), wb_ref[br])
        pre = a + pltpu.roll(bm, n - 1, 0) + b1_ref[br]
        hid = _gelu_tanh(pre).astype(BF16)
        outs.append([_dot(hid[:, g * CMP_HIDDEN:(g + 1) * CMP_HIDDEN], w2_ref[br]) for g in range(KV_HEADS)])
    end_digits = _pos_digits(lax.broadcasted_iota(I32, (n, HEAD_DIM), 0) * CMP_STRIDE + (CMP_BLOCK - 1))
    for g in range(KV_HEADS):
        kc_ref[0, g] = jnp.concatenate([_head_rms(outs[0][g], knw_ref[0:1, :]), end_digits], axis=1).astype(BF16)
    vct = jnp.concatenate(outs[1], axis=1).T.astype(BF16)
    for g in range(KV_HEADS):
        vct_ref[0, g] = vct[g * HEAD_DIM:(g + 1) * HEAD_DIM, :]


def _compress(kc_raw, vc_raw, cmp_pos, cmp_w1, cmp_b1, cmp_w2, k_norm_w):
    b, s, _ = kc_raw.shape
    n = s // CMP_STRIDE
    half = CMP_STRIDE
    cw = CMP_STRIDE * KV_WIDTH
    kch = kc_raw.reshape(b, n, cw)
    vch = vc_raw.reshape(b, n, cw)
    pos = cmp_pos.reshape(2, 2, half, 1, HEAD_DIM)
    pos = jnp.broadcast_to(pos, (2, 2, half, KV_HEADS, HEAD_DIM)).reshape(2, 2, cw)
    w1 = cmp_w1.reshape(2, 2, half, HEAD_DIM, CMP_HIDDEN)
    eye = jnp.eye(KV_HEADS, dtype=F32)
    wfull = jnp.einsum('rhjdn,gk->rhjgdkn', w1, eye).reshape(2, 2, cw, KV_HEADS * CMP_HIDDEN).astype(BF16)
    b1 = jnp.tile(cmp_b1.reshape(2, 1, CMP_HIDDEN), (1, 1, KV_HEADS))
    fix = lambda r: (lambda bi: (0,) * r)
    return pl.pallas_call(
        _compress_kernel,
        grid=(b,),
        in_specs=[pl.BlockSpec((1, n, cw), lambda bi: (bi, 0, 0)),
                  pl.BlockSpec((1, n, cw), lambda bi: (bi, 0, 0)),
                  pl.BlockSpec((2, 2, cw), fix(3)),
                  pl.BlockSpec((2, cw, KV_HEADS * CMP_HIDDEN), fix(3)),
                  pl.BlockSpec((2, cw, KV_HEADS * CMP_HIDDEN), fix(3)),
                  pl.BlockSpec((2, 1, KV_HEADS * CMP_HIDDEN), fix(3)),
                  pl.BlockSpec((2, CMP_HIDDEN, HEAD_DIM), fix(3)),
                  pl.BlockSpec((3, HEAD_DIM), fix(2))],
        out_specs=(pl.BlockSpec((1, KV_HEADS, n, 2 * HEAD_DIM), lambda bi: (bi, 0, 0, 0)),
                   pl.BlockSpec((1, KV_HEADS, HEAD_DIM, n), lambda bi: (bi, 0, 0, 0))),
        out_shape=(jax.ShapeDtypeStruct((b, KV_HEADS, n, 2 * HEAD_DIM), BF16),
                   jax.ShapeDtypeStruct((b, KV_HEADS, HEAD_DIM, n), BF16)),
        compiler_params=_cparams("parallel"),
        name="compress",
    )(kch, vch, pos, wfull[:, 0], wfull[:, 1], b1, cmp_w2.astype(BF16), k_norm_w)


def _nsa_kernel(q_ref, kc_ref, vct_ref, ks_ref, vst_ref, kw_ref, vwt_ref, gt_ref, cdiff_ref, wdiff_ref,
                ovl_ref, oh_ref, onw_ref, wmask_ref, o_ref, buf_a, buf_b, m_scr, acc_scr, oc_scr, bias_scr,
                lst, cnt, *, n_top):
    q0 = pl.program_id(2) * TQ
    ncols = HEADS_PER_KV * TQ
    q = q_ref[0].reshape(ncols, 2 * HEAD_DIM)
    ns = ovl_ref.shape[0]

    def compress_and_select(nk, nb):
        s = jnp.where(cdiff_ref[0:nk, :] <= q0, _dot_nt(kc_ref[0, 0, 0:nk, :], q), -jnp.inf)
        m = jnp.max(s, axis=0, keepdims=True)
        m = jnp.where(m == -jnp.inf, 0.0, m)
        e = jnp.exp2(s - m)
        p = e / jnp.maximum(jnp.sum(e, axis=0, keepdims=True), 1e-30)
        oc_scr[...] = _dot(vct_ref[0, 0, :, 0:nk], p.astype(BF16))

        psum = p[:, 0:TQ]
        for hh in range(1, HEADS_PER_KV):
            psum = psum + p[:, hh * TQ:(hh + 1) * TQ]
        imp = _split_dot(ovl_ref[0:nb, 0:nk], psum)
        blk = lax.broadcasted_iota(I32, (nb, TQ), 0)
        tq = q0 + lax.broadcasted_iota(I32, (nb, TQ), 1)
        cur = tq >> 6
        forced = (blk == 0) | (blk == cur) | (blk == cur - 1)
        rank = jnp.where(forced, BIG, jnp.where(blk * SEL_BLOCK <= tq, imp, -BIG))
        blkf = blk.astype(F32)

        bias = jnp.full((nb, TQ), -1e30, F32)
        for _ in range(min(n_top, nb)):
            mx = jnp.max(rank, axis=0, keepdims=True)
            first = jnp.min(jnp.where(rank == mx, blkf, float(nb)), axis=0, keepdims=True)
            hit = blkf == first
            rank = jnp.where(hit, -jnp.inf, rank)
            bias = jnp.where(hit, 0.0, bias)
        bias_scr[...] = jnp.full((128, TQ), -1e30, F32)
        bias_scr[0:nb, :] = jnp.where(blk == 0, -1e30, bias)

    nc = kc_ref.shape[2]
    quarter = (q0 + TQ - 1) // (ks_ref.shape[2] // NSA_SIZE_VARIANTS)
    for v in range(NSA_SIZE_VARIANTS):
        @pl.when(quarter == v)
        def _():
            compress_and_select(nc * (v + 1) // NSA_SIZE_VARIANTS, ns * (v + 1) // NSA_SIZE_VARIANTS)

    o_c = oc_scr[...]
    bias = bias_scr[...]

    bias_t = bias.T.astype(BF16)
    qq = jnp.concatenate([q, jnp.concatenate([bias_t] * HEADS_PER_KV, axis=0)], axis=1)
    ones_rows = jnp.ones((16, TK), BF16)

    def scores(j):
        k0 = pl.multiple_of(j * TK, TK)
        kk = jnp.concatenate([ks_ref[0, 0, pl.ds(k0, TK), :], oh_ref[pl.ds(k0, TK), :]], axis=1)
        return _dot_nt(kk, qq)

    def consume(buf, j, causal, part):
        sc = buf[...]
        if causal:
            sc = jnp.where(wdiff_ref[0:TK, :] + (q0 - j * TK) >= 0, sc, -1e30)
        k0 = pl.multiple_of(j * TK, TK)
        m_run = m_scr[part]
        m_new = jnp.maximum(m_run, jnp.max(sc, axis=0, keepdims=True))
        ex = jnp.exp2(sc - m_new).astype(BF16)
        va = jnp.concatenate([vst_ref[0, 0, :, pl.ds(k0, TK)], ones_rows], axis=0)
        acc_scr[part] = jnp.exp2(m_run - m_new) * acc_scr[part] + _dot(va, ex)
        m_scr[part] = m_new

    n_past = q0 // TK
    blocks_per_tile = TK // SEL_BLOCK
    cnt[0] = 0
    for j in range(ks_ref.shape[2] // TK):
        wanted = jnp.max(bias[j * blocks_per_tile:(j + 1) * blocks_per_tile, :]) == 0.0

        @pl.when(wanted & (j < n_past))
        def _():
            lst[cnt[0]] = j
            cnt[0] = cnt[0] + 1

    n_sel = cnt[0]
    lst[n_sel] = n_past

    buf_a[...] = scores(lst[0])

    s0 = jnp.where(wdiff_ref[0:SEL_BLOCK, :] + q0 >= 0, _dot_nt(ks_ref[0, 0, 0:SEL_BLOCK, :], q), -1e30)
    m0 = jnp.max(s0, axis=0, keepdims=True)
    v0 = jnp.concatenate([vst_ref[0, 0, :, 0:SEL_BLOCK], jnp.ones((16, SEL_BLOCK), BF16)], axis=0)
    m_scr[0] = m0
    acc_scr[0] = _dot(v0, jnp.exp2(s0 - m0).astype(BF16))
    m_scr[1] = jnp.full((1, ncols), -1e30, F32)
    acc_scr[1] = jnp.zeros((HEAD_DIM + 16, ncols), F32)

    nw = WINDOW + TQ
    start = pl.multiple_of(jnp.maximum(q0 - WINDOW, 0), TQ)
    sw = _dot_nt(kw_ref[0, 0, pl.ds(start, nw), :], q) + wmask_ref[0]
    ew = jnp.exp2(sw - jnp.max(sw, axis=0, keepdims=True))
    vw_aug = jnp.concatenate([vwt_ref[0, 0, :, pl.ds(start, nw)], jnp.ones((16, nw), BF16)], axis=0)
    acc_w = _dot(vw_aug, ew.astype(BF16))
    o_w = acc_w[0:HEAD_DIM, :] / acc_w[HEAD_DIM:HEAD_DIM + 1, :]

    def tiles(first, count):
        for u in range(0, count, 2):
            buf_b[...] = scores(lst[first + u + 1])
            consume(buf_a, lst[first + u], False, 0)
            buf_a[...] = scores(lst[first + u + 2])
            consume(buf_b, lst[first + u + 1], False, 1)
        return 0

    lax.fori_loop(0, n_sel // 4, lambda i, _: tiles(4 * i, 4), 0)
    lax.fori_loop(0, (n_sel // 2) % 2, lambda i, _: tiles((n_sel // 4) * 4, 2), 0)

    @pl.when(n_sel % 2 == 1)
    def _():
        buf_b[...] = scores(n_past)
        consume(buf_a, lst[n_sel - 1], False, 0)
        consume(buf_b, n_past, True, 1)

    @pl.when(n_sel % 2 == 0)
    def _():
        consume(buf_a, n_past, True, 0)

    m_all = jnp.maximum(m_scr[0], m_scr[1])
    acc_s = jnp.exp2(m_scr[0] - m_all) * acc_scr[0] + jnp.exp2(m_scr[1] - m_all) * acc_scr[1]
    o_s = acc_s[0:HEAD_DIM, :] / acc_s[HEAD_DIM:HEAD_DIM + 1, :]

    gt = gt_ref[0, 0]
    outs = []
    for hh in range(HEADS_PER_KV):
        cs = slice(hh * TQ, (hh + 1) * TQ)
        o = (gt[3 * hh:3 * hh + 1, :] * o_c[:, cs] + gt[3 * hh + 1:3 * hh + 2, :] * o_s[:, cs]
             + gt[3 * hh + 2:3 * hh + 3, :] * o_w[:, cs])
        o = o * lax.rsqrt(jnp.mean(o * o, axis=0, keepdims=True) + RMS_EPS) * onw_ref[0, hh]
        outs.append(o)
    o_ref[0] = jnp.concatenate(outs, axis=0).T


def _nsa(q, kc, vct, ks, vst, kw, vwt, gates_t, attn_out_norm_w):
    b, _, s, aw = q.shape
    nc = kc.shape[2]
    ns = s // SEL_BLOCK
    n_top = min(N_SELECT, ns)
    ncols = HEADS_PER_KV * TQ
    nw = WINDOW + TQ
    tl = np.arange(ncols)[None, :] & (TQ - 1)
    cdiff = jnp.asarray((np.arange(nc)[:, None] * CMP_STRIDE + (CMP_BLOCK - 1) - tl).astype(np.int32))
    wdiff_np = (tl - np.arange(nw)[:, None]).astype(np.int32)
    wdiff = jnp.asarray(wdiff_np)
    n_off = WINDOW // TQ + 1
    dist_np = wdiff_np[None] + (np.arange(n_off) * TQ)[:, None, None]
    wmask = jnp.asarray(np.where((dist_np >= 0) & (dist_np < WINDOW), 0.0, -np.inf).astype(np.float32))
    ci = np.arange(nc)[None, :] * CMP_STRIDE
    bj = np.arange(ns)[:, None]
    ovl = ((ci < (bj + 1) * SEL_BLOCK) & (ci + CMP_BLOCK > bj * SEL_BLOCK) & (np.arange(nc)[None, :] < nc - 1))
    ovl = jnp.asarray(ovl.astype(np.float32)).astype(BF16)
    assert ns <= 128
    onehot = (np.arange(s)[:, None] // SEL_BLOCK == np.arange(128)[None, :])
    onehot = jnp.asarray(onehot.astype(np.float32)).astype(BF16)
    onw = jnp.broadcast_to(attn_out_norm_w.reshape(KV_HEADS, HEADS_PER_KV, HEAD_DIM, 1),
                           (KV_HEADS, HEADS_PER_KV, HEAD_DIM, TQ))
    gt = gates_t.reshape(b, KV_HEADS, HEADS_PER_KV * 3, s)
    per_bg = lambda bi, g, i: (bi, g, 0, 0)
    fixed = lambda bi, g, i: (0, 0)
    return pl.pallas_call(
        functools.partial(_nsa_kernel, n_top=n_top),
        grid=(b, KV_HEADS, s // TQ),
        in_specs=[pl.BlockSpec((1, HEADS_PER_KV, TQ, aw), lambda bi, g, i: (bi, g, i, 0)),
                  pl.BlockSpec((1, 1, nc, aw), per_bg),
                  pl.BlockSpec((1, 1, HEAD_DIM, nc), per_bg),
                  pl.BlockSpec((1, 1, s, aw), per_bg),
                  pl.BlockSpec((1, 1, HEAD_DIM, s), per_bg),
                  pl.BlockSpec((1, 1, s, aw), per_bg),
                  pl.BlockSpec((1, 1, HEAD_DIM, s), per_bg),
                  pl.BlockSpec((1, 1, HEADS_PER_KV * 3, TQ), lambda bi, g, i: (bi, g, 0, i)),
                  pl.BlockSpec((nc, ncols), fixed, pipeline_mode=pl.Buffered(1)),
                  pl.BlockSpec((nw, ncols), fixed, pipeline_mode=pl.Buffered(1)),
                  pl.BlockSpec((ns, nc), fixed, pipeline_mode=pl.Buffered(1)),
                  pl.BlockSpec((s, 128), fixed, pipeline_mode=pl.Buffered(1)),
                  pl.BlockSpec((1, HEADS_PER_KV, HEAD_DIM, TQ), lambda bi, g, i: (g, 0, 0, 0)),
                  pl.BlockSpec((1, nw, ncols), lambda bi, g, i: (jnp.minimum(i, n_off - 1), 0, 0))],
        out_specs=pl.BlockSpec((1, TQ, HEADS_PER_KV * HEAD_DIM), lambda bi, g, i: (bi, i, g)),
        out_shape=jax.ShapeDtypeStruct((b, s, NSA_WIDTH), F32),
        scratch_shapes=[pltpu.VMEM((TK, ncols), F32), pltpu.VMEM((TK, ncols), F32),
                        pltpu.VMEM((2, 1, ncols), F32), pltpu.VMEM((2, HEAD_DIM + 16, ncols), F32),
                        pltpu.VMEM((HEAD_DIM, ncols), F32), pltpu.VMEM((128, TQ), F32),
                        pltpu.SMEM((s // TK + 1,), I32), pltpu.SMEM((1,), I32)],
        compiler_params=_cparams("parallel", "parallel", "arbitrary"),
        name="nsa",
    )(q, kc, vct, ks, vst, kw, vwt, gt, cdiff, wdiff, ovl, onehot, onw, wmask)


def _hgrn_kernel(q_ref, k_ref, lf_ref, v_ref, g_ref, onw_ref, cm_ref, o_ref, state_scr, *, n_chunks):
    c = HGRN_CHUNK

    @pl.when(pl.program_id(1) == 0)
    def _():
        state_scr[...] = jnp.zeros_like(state_scr)

    ri = lax.broadcasted_iota(I32, (c, c), 0)
    ci = lax.broadcasted_iota(I32, (c, c), 1)
    rsub = ri // HGRN_SUB
    diag = ri == ci

    def head_chunk(r0, hd, state_t):
        cols = slice(hd * HGRN_DIM, (hd + 1) * HGRN_DIM)
        q = q_ref[0, pl.ds(r0, c), cols]
        k = k_ref[0, pl.ds(r0, c), cols]
        lf = lf_ref[0, pl.ds(r0, c), cols] * LOG2E
        v = v_ref[0, pl.ds(r0, c), cols]
        cm = cm_ref[...]
        l1 = lf.astype(BF16)
        rest = lf - l1.astype(F32)
        l2 = rest.astype(BF16)
        l3 = (rest - l2.astype(F32)).astype(BF16)
        cum = _dot(cm, l1) + _dot(cm, l2) + _dot(cm, l3)
        o = _dot_nt((q * jnp.exp2(cum)).astype(BF16), state_t.astype(BF16))
        scores = jnp.where(diag, jnp.sum(q * k, axis=-1, keepdims=True), 0.0)

        def factored(ref, mask, acc):
            qs = q * jnp.exp2(jnp.minimum(cum - ref, 0.0))
            kd = k * jnp.exp2(jnp.minimum(ref - cum, 0.0))
            return jnp.where(mask, _dot_nt(qs.astype(BF16), kd.astype(BF16)), acc)

        for i in range(1, c // HGRN_SUB):
            scores = factored(cum[i * HGRN_SUB - 1:i * HGRN_SUB, :], (rsub == i) & (ci < i * HGRN_SUB), scores)
        for d in range(1, HGRN_SUB):
            ksh = pltpu.roll(k, d, 0)
            csh = pltpu.roll(cum, d, 0)
            w = jnp.sum(q * ksh * jnp.exp2(cum - csh), axis=-1, keepdims=True)
            scores = jnp.where((ri - ci == d) & ((ri & (HGRN_SUB - 1)) >= d), w, scores)
        o = o + _dot(scores.astype(BF16), v.astype(BF16))
        last = cum[c - 1:c, :]
        kd = (k * jnp.exp2(last - cum)).astype(BF16)
        state_t = state_t * jnp.exp2(last) + _dot(v.T.astype(BF16), kd)
        o = o * g_ref[0, pl.ds(r0, c), cols]
        o = o * lax.rsqrt(jnp.mean(o * o, axis=-1, keepdims=True) + RMS_EPS) * onw_ref[:, cols]
        o_ref[0, pl.ds(r0, c), cols] = o
        return state_t

    def chunk(ck, states):
        r0 = pl.multiple_of(ck * c, c)
        return tuple(head_chunk(r0, hd, states[hd]) for hd in range(HGRN_HEADS))

    states = lax.fori_loop(0, n_chunks, chunk, tuple(state_scr[hd] for hd in range(HGRN_HEADS)))
    for hd in range(HGRN_HEADS):
        state_scr[hd] = states[hd]


def _hgrn(hq, hk, hlf, hv, hg, rec_out_norm_w, rows):
    b, s, _ = hq.shape
    cm = jnp.asarray(np.tril(np.ones((HGRN_CHUNK, HGRN_CHUNK), np.float32))).astype(BF16)
    blk = pl.BlockSpec((1, rows, HGRN_WIDTH), lambda bi, i: (bi, i, 0))
    return pl.pallas_call(
        functools.partial(_hgrn_kernel, n_chunks=rows // HGRN_CHUNK),
        grid=(b, s // rows),
        in_specs=[blk, blk, blk, blk, blk,
                  pl.BlockSpec((1, HGRN_WIDTH), lambda bi, i: (0, 0)),
                  pl.BlockSpec(cm.shape, lambda bi, i: (0, 0))],
        out_specs=blk,
        out_shape=jax.ShapeDtypeStruct((b, s, HGRN_WIDTH), F32),
        scratch_shapes=[pltpu.VMEM((HGRN_HEADS, HGRN_DIM, HGRN_DIM), F32)],
        compiler_params=_cparams("parallel", "arbitrary"),
        name="hgrn",
    )(hq, hk, hlf, hv, hg, rec_out_norm_w.reshape(1, HGRN_WIDTH), cm)


def _outproj_kernel(x_ref, a_ref, r_ref, wa_ref, wr_ref, gt_ref, sc_ref, sh_ref, n2_ref, x1_ref, h2_ref, h2p_ref):
    mixed = _dot(a_ref[0].astype(BF16), wa_ref[...]) + _dot(r_ref[0].astype(BF16), wr_ref[...])
    x1 = x_ref[0] + gt_ref[0] * mixed
    x1_ref[0] = x1
    ms = jnp.mean(x1 * x1, axis=-1, keepdims=True)
    h2 = x1 * lax.rsqrt(ms + RMS_EPS) * n2_ref[...] * (1.0 + sc_ref[0]) + sh_ref[0]
    h2_ref[0] = h2
    h2p_ref[0] = _pack_bf16_pair(h2[:, :D_MODEL // 2], h2[:, D_MODEL // 2:])


def _outproj(x, attn, rec, w_out, gt1, sc2, sh2, norm2_w, tm):
    b, s, d = x.shape
    row = lambda bi, i: (bi, i, 0)
    per_b = lambda bi, i: (bi, 0, 0)
    fixed2 = lambda bi, i: (0, 0)
    w = w_out.astype(BF16)
    return pl.pallas_call(
        _outproj_kernel,
        grid=(b, s // tm),
        in_specs=[pl.BlockSpec((1, tm, d), row),
                  pl.BlockSpec((1, tm, NSA_WIDTH), row),
                  pl.BlockSpec((1, tm, HGRN_WIDTH), row),
                  pl.BlockSpec((NSA_WIDTH, d), fixed2),
                  pl.BlockSpec((HGRN_WIDTH, d), fixed2),
                  pl.BlockSpec((1, 1, d), per_b),
                  pl.BlockSpec((1, 1, d), per_b),
                  pl.BlockSpec((1, 1, d), per_b),
                  pl.BlockSpec((1, d), fixed2)],
        out_specs=(pl.BlockSpec((1, tm, d), row), pl.BlockSpec((1, tm, d), row), pl.BlockSpec((1, tm, d // 2), row)),
        out_shape=(jax.ShapeDtypeStruct((b, s, d), F32), jax.ShapeDtypeStruct((b, s, d), F32),
                   jax.ShapeDtypeStruct((b, s, d // 2), jnp.uint32)),
        compiler_params=_cparams("parallel", "parallel"),
        name="outproj",
    )(x, attn, rec, w[:NSA_WIDTH], w[NSA_WIDTH:], gt1, sc2, sh2, norm2_w)


def _mixer(x, c, ada_w, ada_b, norm1_w, norm2_w, w_in, q_norm_w, k_norm_w, cmp_pos, cmp_w1, cmp_b1, cmp_w2,
           attn_out_norm_w, hgrn_lb_param, rec_out_norm_w, w_out):
    b, s, d = x.shape
    mod = _mod(c, ada_w, ada_b)
    sh1, sc1, gt1, sh2, sc2, gt2 = [m.reshape(b, 1, d) for m in jnp.split(mod, 6, axis=-1)]
    o = NSA_WIDTH + 6 * KV_WIDTH
    w_cat = jnp.concatenate([w_in[:, :o], w_in[:, o:o + NSA_HEADS * 3],
                             jnp.zeros((d, GATE_PAD - NSA_HEADS * 3), w_in.dtype),
                             w_in[:, o + NSA_HEADS * 3:]], axis=1).astype(BF16)
    tm = min(512, s)
    (q, kc_raw, vc_raw, ks, vst, kw, vwt, gates_t, hq, hk, hlf, hv, hg) = _inproj(
        x, sc1, sh1, norm1_w.reshape(1, d), w_cat, q_norm_w.reshape(1, HEAD_DIM), k_norm_w, hgrn_lb_param, tm)
    kc, vct = _compress(kc_raw, vc_raw, cmp_pos, cmp_w1, cmp_b1, cmp_w2, k_norm_w)
    attn = _nsa(q, kc, vct, ks, vst, kw, vwt, gates_t, attn_out_norm_w)
    rec = _hgrn(hq, hk, hlf, hv, hg, rec_out_norm_w, min(1024, s))
    x1, h2, h2p = _outproj(x, attn, rec, w_out, gt1, sc2, sh2, norm2_w.reshape(1, d), min(1024, s))
    return x1, h2, h2p, gt2


def _router_kernel(h_ref, rwt_ref, bias_ref, tri_ref, ones_ref, idx_ref, w_ref, rank_ref, cnt_ref, carry_scr, *, tr):
    @pl.when(pl.program_id(0) == 0)
    def _():
        carry_scr[...] = jnp.zeros_like(carry_scr)

    h = h_ref[...]
    h_hi = h.astype(BF16)
    h_lo = (h - h_hi.astype(F32)).astype(BF16)
    logits = _dot_nt(rwt_ref[0], h_hi) + _dot_nt(rwt_ref[1], h_hi) + _dot_nt(rwt_ref[0], h_lo)
    scores = _sigmoid(logits)
    biased = scores + bias_ref[...]
    neg = -jnp.inf

    gs = []
    for g in range(N_GROUPS):
        sub = biased[g * GROUP_SIZE:(g + 1) * GROUP_SIZE, :]
        m1 = jnp.max(sub, axis=0, keepdims=True)
        dup = jnp.sum((sub == m1).astype(F32), axis=0, keepdims=True)
        m2 = jnp.max(jnp.where(sub < m1, sub, neg), axis=0, keepdims=True)
        gs.append(m1 + jnp.where(dup >= 2.0, m1, m2))
    parts = []
    for g in range(N_GROUPS):
        beaten = jnp.zeros_like(gs[g])
        for g2 in range(N_GROUPS):
            if g2 != g:
                beats = (gs[g2] >= gs[g]) if g2 < g else (gs[g2] > gs[g])
                beaten = beaten + beats.astype(F32)
        sub = biased[g * GROUP_SIZE:(g + 1) * GROUP_SIZE, :]
        parts.append(jnp.where(beaten < float(TOPK_GROUPS), sub, neg))
    cand = jnp.concatenate(parts, axis=0)

    rowf = lax.broadcasted_iota(I32, (N_EXPERTS, tr), 0).astype(F32)
    idx_rows, w_rows, hits = [], [], []
    multi = jnp.zeros((N_EXPERTS, tr), F32)
    for _ in range(TOP_K):
        mx = jnp.max(cand, axis=0, keepdims=True)
        first = jnp.min(jnp.where(cand == mx, rowf, float(N_EXPERTS)), axis=0, keepdims=True)
        hit = rowf == first
        idx_rows.append(first)
        w_rows.append(jnp.sum(jnp.where(hit, scores, 0.0), axis=0, keepdims=True))
        cand = jnp.where(hit, neg, cand)
        multi = jnp.where(hit, 1.0, multi)
    w = jnp.concatenate(w_rows, axis=0)
    w_ref[...] = w / jnp.sum(w, axis=0, keepdims=True) * ROUTED_SCALE
    idx = jnp.concatenate(idx_rows, axis=0)
    idx_ref[...] = idx.astype(I32)

    carry = carry_scr[...]
    mb = multi.astype(BF16)
    before = _dot(mb, tri_ref[...]) + jnp.concatenate([carry] * (tr // 128), axis=1)
    rank_rows = [jnp.sum(jnp.where(rowf == idx_rows[k], before, 0.0), axis=0, keepdims=True) for k in range(TOP_K)]
    rank_ref[...] = jnp.concatenate(rank_rows, axis=0).astype(I32)
    carry = carry + _dot(mb, ones_ref[...])
    carry_scr[...] = carry
    cnt_ref[...] = carry


def _router(h2, router_w, router_bias, tr):
    t, d = h2.shape
    tri = jnp.asarray(np.triu(np.ones((tr, tr), np.float32), 1)).astype(BF16)
    ones = jnp.ones((tr, 128), BF16)
    tok = pl.BlockSpec((TOP_K, tr), lambda i: (0, i))
    fixed = lambda i: (0, 0)
    rwt = router_w.T
    rwt_hi = rwt.astype(BF16)
    rwt_split = jnp.stack([rwt_hi, (rwt - rwt_hi.astype(F32)).astype(BF16)])
    return pl.pallas_call(
        functools.partial(_router_kernel, tr=tr),
        grid=(t // tr,),
        in_specs=[pl.BlockSpec((tr, d), lambda i: (i, 0)),
                  pl.BlockSpec((2, N_EXPERTS, d), lambda i: (0, 0, 0)),
                  pl.BlockSpec((N_EXPERTS, 1), fixed),
                  pl.BlockSpec((tr, tr), fixed),
                  pl.BlockSpec((tr, 128), fixed)],
        out_specs=(tok, tok, tok, pl.BlockSpec((N_EXPERTS, 128), fixed)),
        out_shape=(jax.ShapeDtypeStruct((TOP_K, t), I32), jax.ShapeDtypeStruct((TOP_K, t), F32),
                   jax.ShapeDtypeStruct((TOP_K, t), I32), jax.ShapeDtypeStruct((N_EXPERTS, 128), F32)),
        scratch_shapes=[pltpu.VMEM((N_EXPERTS, 128), F32)],
        compiler_params=_cparams("arbitrary"),
        name="router",
    )(h2, rwt_split, router_bias.reshape(N_EXPERTS, 1), tri, ones)


def _pack_bf16_pair(a, b):
    ua = lax.bitcast_convert_type(a.astype(BF16).astype(F32), jnp.uint32)
    ub = lax.bitcast_convert_type(b.astype(BF16).astype(F32), jnp.uint32)
    return ua | (ub >> 16)


def _unpack_bf16_pair(w):
    a = lax.bitcast_convert_type(w & jnp.uint32(0xFFFF0000), F32)
    b = lax.bitcast_convert_type(w << 16, F32)
    return a, b


def _slot_kernel(ps_ref, idx_ref, rank_ref, slot_ref):
    idx = idx_ref[...]

    def body(e, acc):
        return jnp.where(idx == e, ps_ref[e], acc)

    slot_ref[...] = lax.fori_loop(0, N_EXPERTS, body, jnp.zeros_like(idx)) + rank_ref[...]


def _slots(pad_start, idx, rank, tt):
    t = idx.shape[1]
    tok = pl.BlockSpec((TOP_K, tt), lambda i, ps: (0, i))
    return pl.pallas_call(
        _slot_kernel,
        grid_spec=pltpu.PrefetchScalarGridSpec(num_scalar_prefetch=1, grid=(t // tt,),
                                               in_specs=[tok, tok], out_specs=tok),
        out_shape=jax.ShapeDtypeStruct((TOP_K, t), I32),
        compiler_params=_cparams("parallel"),
        name="slots",
    )(pad_start, idx, rank)


SC_CORES = 2
SC_SUBCORES = 16
SC_CHUNK = 64


def _sc_mesh():
    return plsc.VectorSubcoreMesh(core_axis_name="c", subcore_axis_name="s")


def _sc_dispatch(h2p, slot_chunks, n_rows):
    t, dw = h2p.shape
    per = slot_chunks.shape[0] // (SC_CORES * SC_SUBCORES)

    def body(h_hbm, slot_hbm, xs_hbm, idx_v, rows_v, sem):
        wid = lax.axis_index("s") * SC_CORES + lax.axis_index("c")

        @pl.loop(0, per)
        def _(c):
            ch = wid * per + c
            pltpu.sync_copy(slot_hbm.at[ch], idx_v)
            pltpu.sync_copy(h_hbm.at[pl.ds(ch * SC_CHUNK, SC_CHUNK)], rows_v)
            copies = [pltpu.async_copy(rows_v, xs_hbm.at[idx_v.at[k]], sem) for k in range(TOP_K)]
            for cp in copies:
                cp.wait()

    return pl.kernel(
        body, out_type=jax.ShapeDtypeStruct((n_rows, dw), h2p.dtype), mesh=_sc_mesh(),
        scratch_types=[pltpu.VMEM((TOP_K, SC_CHUNK), I32), pltpu.VMEM((SC_CHUNK, dw), h2p.dtype),
                       pltpu.SemaphoreType.DMA],
    )(h2p, slot_chunks)


def _sc_gather(ys, slot_chunks, t):
    dw = ys.shape[1]
    per = slot_chunks.shape[0] // (SC_CORES * SC_SUBCORES)

    def body(ys_hbm, slot_hbm, yg_hbm, idx_v, rows_v, gsem, wsem):
        wid = lax.axis_index("s") * SC_CORES + lax.axis_index("c")

        @pl.loop(0, per)
        def _(c):
            ch = wid * per + c
            pltpu.sync_copy(slot_hbm.at[ch], idx_v)
            gathers = [None] * TOP_K
            writes = [None] * TOP_K
            gathers[0] = pltpu.async_copy(ys_hbm.at[idx_v.at[0]], rows_v.at[0], gsem)
            for k in range(TOP_K):
                gathers[k].wait()
                if k + 1 < TOP_K:
                    if k >= 1:
                        writes[k - 1].wait()
                    gathers[k + 1] = pltpu.async_copy(ys_hbm.at[idx_v.at[k + 1]], rows_v.at[(k + 1) % 2], gsem)
                writes[k] = pltpu.async_copy(rows_v.at[k % 2], yg_hbm.at[k, pl.ds(ch * SC_CHUNK, SC_CHUNK)], wsem)
            writes[TOP_K - 2].wait()
            writes[TOP_K - 1].wait()

    return pl.kernel(
        body, out_type=jax.ShapeDtypeStruct((TOP_K, t, dw), ys.dtype), mesh=_sc_mesh(),
        scratch_types=[pltpu.VMEM((TOP_K, SC_CHUNK), I32), pltpu.VMEM((2, SC_CHUNK, dw), ys.dtype),
                       pltpu.SemaphoreType.DMA, pltpu.SemaphoreType.DMA],
    )(ys, slot_chunks)


def _experts_kernel(be_ref, nu_ref, bv_ref, run_ref, xs_hbm, wg_hbm, wu_hbm, wd_hbm, ys_ref,
                    xring, rsem, gring, uring, dring, wsem):
    i = pl.program_id(0)
    half = D_MODEL // 2
    n_used = nu_ref[0]
    n_steps = pl.num_programs(0)

    def weight_copies(blk):
        ex = be_ref[blk]
        slot = run_ref[blk] % EXPERT_RING
        return [pltpu.make_async_copy(src.at[ex], ring.at[slot], wsem.at[a, slot])
                for a, (src, ring) in enumerate(((wg_hbm, gring), (wu_hbm, uring), (wd_hbm, dring)))]

    def starts_run(blk):
        return run_ref[blk] != run_ref[jnp.maximum(blk - 1, 0)]

    @pl.when(i == 0)
    def _():
        for cp in weight_copies(jnp.int32(0)):
            cp.start()

        @pl.when((n_steps > 1) & starts_run(jnp.int32(1)))
        def _():
            for cp in weight_copies(jnp.int32(1)):
                cp.start()

    ahead = jnp.minimum(i + (EXPERT_RING - 1), n_steps - 1)

    @pl.when((i + (EXPERT_RING - 1) < n_steps) & starts_run(ahead))
    def _():
        for cp in weight_copies(ahead):
            cp.start()

    @pl.when((i == 0) | starts_run(i))
    def _():
        for cp in weight_copies(i):
            cp.wait()

    wslot = run_ref[i] % EXPERT_RING
    wg_ref, wu_ref, wd_ref = gring.at[wslot], uring.at[wslot], dring.at[wslot]

    def fetch(blk):
        slot = blk % EXPERT_RING
        return pltpu.make_async_copy(xs_hbm.at[pl.ds(pl.multiple_of(blk * EXPERT_BLOCK, EXPERT_BLOCK), EXPERT_BLOCK)],
                                     xring.at[slot], rsem.at[slot])

    @pl.when(i == 0)
    def _():
        for first in range(EXPERT_RING - 1):
            @pl.when(first < n_used)
            def _():
                fetch(jnp.int32(first)).start()

    @pl.when(i + (EXPERT_RING - 1) < n_used)
    def _():
        fetch(i + (EXPERT_RING - 1)).start()

    @pl.when(i < n_used)
    def _():
        fetch(i).wait()

    xs_ref = xring.at[i % EXPERT_RING]

    def ffn(rows):
        live = lax.broadcasted_iota(I32, (rows, xs_ref.shape[1]), 0) < bv_ref[i]
        xa, xb = _unpack_bf16_pair(jnp.where(live, xs_ref[0:rows, :], jnp.uint32(0)))
        xa, xb = xa.astype(BF16), xb.astype(BF16)
        g = _dot(xa, wg_ref[:half, :].astype(BF16)) + _dot(xb, wg_ref[half:, :].astype(BF16))
        u = _dot(xa, wu_ref[:half, :].astype(BF16)) + _dot(xb, wu_ref[half:, :].astype(BF16))
        act = (g * _sigmoid(g) * u).astype(BF16)
        y = _dot(act, wd_ref[...].astype(BF16))
        ys_ref[0:rows, :] = _pack_bf16_pair(y[:, :half], y[:, half:])

    used = i < n_used
    short = bv_ref[i] <= EXPERT_TAIL

    @pl.when(used & jnp.logical_not(short))
    def _():
        ffn(EXPERT_BLOCK)

    @pl.when(used & short)
    def _():
        ffn(EXPERT_TAIL)
        ys_ref[EXPERT_TAIL:, :] = jnp.zeros((EXPERT_BLOCK - EXPERT_TAIL, ys_ref.shape[1]), ys_ref.dtype)

    @pl.when(jnp.logical_not(used))
    def _():
        ys_ref[...] = jnp.zeros_like(ys_ref)


def _experts(xs, blk_e, n_used, blk_valid, w_gate, w_up, w_down):
    n_rows, dw = xs.shape
    d = w_gate.shape[1]
    nblk = n_rows // EXPERT_BLOCK
    blk_run = jnp.cumsum(jnp.concatenate([jnp.zeros((1,), I32), (blk_e[1:] != blk_e[:-1]).astype(I32)])).astype(I32)
    hbm = pl.BlockSpec(memory_space=pl.ANY)
    return pl.pallas_call(
        _experts_kernel,
        grid_spec=pltpu.PrefetchScalarGridSpec(
            num_scalar_prefetch=4,
            grid=(nblk,),
            in_specs=[hbm, hbm, hbm, hbm],
            out_specs=pl.BlockSpec((EXPERT_BLOCK, dw), lambda i, be, nu, bv, rn: (i, 0)),
            scratch_shapes=[pltpu.VMEM((EXPERT_RING, EXPERT_BLOCK, dw), xs.dtype),
                            pltpu.SemaphoreType.DMA((EXPERT_RING,)),
                            pltpu.VMEM((EXPERT_RING, d, EXPERT_FF), w_gate.dtype),
                            pltpu.VMEM((EXPERT_RING, d, EXPERT_FF), w_up.dtype),
                            pltpu.VMEM((EXPERT_RING, EXPERT_FF, d), w_down.dtype),
                            pltpu.SemaphoreType.DMA((3, EXPERT_RING))]),
        out_shape=jax.ShapeDtypeStruct((n_rows, dw), xs.dtype),
        compiler_params=pltpu.CompilerParams(dimension_semantics=("arbitrary",), vmem_limit_bytes=VMEM_LIMIT,
                                             has_side_effects=True),
        name="experts",
    )(blk_e, n_used, blk_valid, blk_run, xs, w_gate, w_up, w_down)


def _combine_kernel(x1_ref, h_ref, w_ref, gt_ref, sg_ref, su_ref, sd_ref, yg_ref, o_ref):
    tc = x1_ref.shape[0]
    half = D_MODEL // 2
    hb = h_ref[...].astype(BF16)
    g = _dot(hb, sg_ref[...])
    u = _dot(hb, su_ref[...])
    ffn = _dot((g * _sigmoid(g) * u).astype(BF16), sd_ref[...])

    w = w_ref[...]
    ra = jnp.zeros((tc, half), F32)
    rb = jnp.zeros((tc, half), F32)
    for k in range(TOP_K):
        ya, yb = _unpack_bf16_pair(yg_ref[k])
        ra = ra + w[:, k:k + 1] * ya
        rb = rb + w[:, k:k + 1] * yb
    ffn = ffn + jnp.concatenate([ra, rb], axis=1)
    o_ref[...] = x1_ref[...] + gt_ref[0] * ffn


def _combine(x1, h2, w_tok, gt2, yg, sg, su, sd, seq, tc):
    t, d = x1.shape
    row = lambda i: (i, 0)
    fixed = lambda i: (0, 0)
    return pl.pallas_call(
        _combine_kernel,
        grid=(t // tc,),
        in_specs=[pl.BlockSpec((tc, d), row),
                  pl.BlockSpec((tc, d), row),
                  pl.BlockSpec((tc, TOP_K), row),
                  pl.BlockSpec((1, 1, d), lambda i: ((i * tc) // seq, 0, 0)),
                  pl.BlockSpec((d, SHARED_FF), fixed),
                  pl.BlockSpec((d, SHARED_FF), fixed),
                  pl.BlockSpec((SHARED_FF, d), fixed),
                  pl.BlockSpec((TOP_K, tc, d // 2), lambda i: (0, i, 0))],
        out_specs=pl.BlockSpec((tc, d), row),
        out_shape=jax.ShapeDtypeStruct((t, d), F32),
        compiler_params=_cparams("parallel"),
        name="combine",
    )(x1, h2, w_tok, gt2, sg.astype(BF16), su.astype(BF16), sd.astype(BF16), yg)


def _moe_parts(x1, h2, h2p, gt2, router_w, router_bias, w_gate, w_up, w_down, sg, su, sd):
    b, s, d = x1.shape
    t = b * s
    assert t % (SC_CHUNK * SC_CORES * SC_SUBCORES) == 0, "token chunks must split evenly over the vector subcores"
    h2 = h2.reshape(t, d)
    idx, w, rank, cnt = _router(h2, router_w, router_bias, min(256, t))
    counts = cnt[:, 0].astype(I32)
    padded = (counts + EXPERT_BLOCK - 1) // EXPERT_BLOCK * EXPERT_BLOCK
    pad_end = jnp.cumsum(padded)
    pad_start = pad_end - padded
    n_rows = t * TOP_K + N_EXPERTS * EXPERT_BLOCK
    nblk = n_rows // EXPERT_BLOCK
    n_used = (pad_end[-1:] // EXPERT_BLOCK).astype(I32)
    blk_start = jnp.arange(nblk, dtype=I32) * EXPERT_BLOCK
    owns = (pad_start[None, :] <= blk_start[:, None]) & (blk_start[:, None] < pad_end[None, :])
    e_ids = jnp.arange(N_EXPERTS, dtype=I32)[None, :]
    last_e = jnp.max(jnp.where(counts > 0, e_ids[0], 0))
    blk_e = jnp.where(blk_start < pad_end[-1], jnp.sum(jnp.where(owns, e_ids, 0), axis=1), last_e).astype(I32)
    rows_left = jnp.sum(jnp.where(owns, (pad_start + counts)[None, :] - blk_start[:, None], 0), axis=1)
    blk_valid = jnp.clip(rows_left, 0, EXPERT_BLOCK).astype(I32)
    slot = _slots(pad_start.astype(I32), idx, rank, min(2048, t))
    slot_chunks = slot.reshape(TOP_K, t // SC_CHUNK, SC_CHUNK).transpose(1, 0, 2)
    xs = _sc_dispatch(h2p.reshape(t, d // 2), slot_chunks, n_rows)
    ys = _experts(xs, blk_e, n_used, blk_valid, w_gate, w_up, w_down)
    yg = _sc_gather(ys, slot_chunks, t)
    out = _combine(x1.reshape(t, d), h2, w.T, gt2, yg, sg, su, sd, s, min(512, t))
    return out.reshape(b, s, d), dict(idx=idx, w=w, rank=rank, cnt=cnt)


def kernel(x, c, ada_w, ada_b, norm1_w, norm2_w, w_in, q_norm_w, k_norm_w, cmp_pos, cmp_w1, cmp_b1, cmp_w2, attn_out_norm_w, hgrn_lb_param, rec_out_norm_w, w_out, router_w, router_bias, exp_w_gate, exp_w_up, exp_w_down, shared_w_gate, shared_w_up, shared_w_down):
    assert ada_w.shape[0] == 1, "one layer"
    assert x.shape[0] <= 8 and x.shape[1] % TK == 0 and x.shape[1] >= WINDOW + TQ
    l = 0
    x1, h2, h2p, gt2 = _mixer(x, c, ada_w[l], ada_b[l], norm1_w[l], norm2_w[l], w_in[l], q_norm_w[l], k_norm_w[l],
                         cmp_pos[l], cmp_w1[l], cmp_b1[l], cmp_w2[l], attn_out_norm_w[l], hgrn_lb_param,
                         rec_out_norm_w[l], w_out[l])
    out, _ = _moe_parts(x1, h2, h2p, gt2, router_w[l], router_bias[l], exp_w_gate[l], exp_w_up[l], exp_w_down[l],
                        shared_w_gate[l], shared_w_up[l], shared_w_down[l])
    return out
```

```python
import functools

import numpy as np
import jax
import jax.numpy as jnp
from jax import lax
from jax.experimental import pallas as pl
from jax.experimental.pallas import tpu as pltpu
from jax.experimental.pallas import tpu_sc as plsc

F32 = jnp.float32
BF16 = jnp.bfloat16
I32 = jnp.int32

D_MODEL = 1024
NSA_HEADS = 8
HEAD_DIM = 64
NSA_WIDTH = NSA_HEADS * HEAD_DIM
KV_HEADS = 2
HEADS_PER_KV = NSA_HEADS // KV_HEADS
KV_WIDTH = KV_HEADS * HEAD_DIM
CMP_BLOCK = 32
CMP_STRIDE = 16
CMP_HIDDEN = 256
SEL_BLOCK = 64
N_SELECT = 16
WINDOW = 512
HGRN_HEADS = 4
HGRN_DIM = 128
HGRN_WIDTH = HGRN_HEADS * HGRN_DIM
HGRN_CHUNK = 64
HGRN_SUB = 16
N_EXPERTS = 256
TOP_K = 8
N_GROUPS = 8
GROUP_SIZE = N_EXPERTS // N_GROUPS
TOPK_GROUPS = 4
EXPERT_FF = 256
SHARED_FF = 256
ROUTED_SCALE = 2.5
RMS_EPS = 1e-6
BIG = 1e9
LOG2E = 1.4426950408889634
GATE_PAD = 128
PROJ_COLS = NSA_WIDTH + 6 * KV_WIDTH + GATE_PAD + 4 * HGRN_WIDTH

VMEM_LIMIT = 56 * 1024 * 1024

TQ = 256
TK = 256
NSA_SIZE_VARIANTS = 4
EXPERT_BLOCK = 512
EXPERT_TAIL = 128
EXPERT_RING = 3
HIGHEST = lax.Precision.HIGHEST


def _cparams(*sem):
    return pltpu.CompilerParams(dimension_semantics=sem, vmem_limit_bytes=VMEM_LIMIT)


def _sigmoid(x):
    return 1.0 / (1.0 + jnp.exp(-x))


def _dot_nt(a, b):
    return lax.dot_general(a, b, (((1,), (1,)), ((), ())), preferred_element_type=F32)


def _dot(a, b, **kw):
    return jnp.dot(a, b, preferred_element_type=F32, **kw)


def _split_dot(a_bf16_exact, x):
    hi = x.astype(BF16)
    lo = (x - hi.astype(F32)).astype(BF16)
    return _dot(a_bf16_exact, hi) + _dot(a_bf16_exact, lo)


def _mod_kernel(c_ref, w_ref, b_ref, o_ref):
    c = c_ref[...]
    cond = c * _sigmoid(c)
    o_ref[...] = _dot(cond, w_ref[...], precision=HIGHEST) + b_ref[...]


def _mod(c, ada_w, ada_b):
    b, d = c.shape
    rows = 8
    c_pad = jnp.zeros((rows, d), F32).at[:b].set(c)
    n = ada_w.shape[1]
    out = pl.pallas_call(
        _mod_kernel,
        grid=(n // d,),
        in_specs=[pl.BlockSpec((rows, d), lambda j: (0, 0)),
                  pl.BlockSpec((d, d), lambda j: (0, j)),
                  pl.BlockSpec((1, d), lambda j: (0, j))],
        out_specs=pl.BlockSpec((rows, d), lambda j: (0, j)),
        out_shape=jax.ShapeDtypeStruct((rows, n), F32),
        compiler_params=_cparams("parallel"),
        name="mod",
    )(c_pad, ada_w, ada_b.reshape(1, n))
    return out[:b]


def _head_rms(t, w):
    return t * lax.rsqrt(jnp.mean(t * t, axis=-1, keepdims=True) + RMS_EPS) * w


def _pos_digits(pos):
    lane = lax.broadcasted_iota(I32, pos.shape, 1)
    d0 = (lane == 0) | (lane == 3) | (lane == 6)
    d1 = (lane == 1) | (lane == 4) | (lane == 7)
    d2 = (lane == 2) | (lane == 5) | (lane == 8)
    dig = jnp.where(d0, pos >> 12, jnp.where(d1, (pos >> 6) & 63, jnp.where(d2, pos & 63, 0)))
    return dig.astype(F32)


def _inproj_kernel(x_ref, sc_ref, sh_ref, n1_ref, w_ref, qnw_ref, knw_ref, lbp_ref, qaug_ref,
                   q_ref, kcr_ref, vcr_ref, ks_ref, vst_ref, kw_ref, vwt_ref, gt_ref,
                   hq_ref, hk_ref, hlf_ref, hv_ref, hg_ref):
    x = x_ref[0]
    ms = jnp.mean(x * x, axis=-1, keepdims=True)
    h = x * lax.rsqrt(ms + RMS_EPS) * n1_ref[...] * (1.0 + sc_ref[0]) + sh_ref[0]
    p = _dot(h.astype(BF16), w_ref[...])
    tm = x.shape[0]

    qnw = qnw_ref[...]
    for hd in range(NSA_HEADS):
        t = p[:, hd * HEAD_DIM:(hd + 1) * HEAD_DIM]
        qn = _head_rms(t, qnw) * (HEAD_DIM ** -0.5 * LOG2E)
        qa = jnp.broadcast_to(qaug_ref[hd:hd + 1, :], (tm, HEAD_DIM))
        q_ref[0, hd] = jnp.concatenate([qn, qa], axis=1).astype(BF16)
    kaug = _pos_digits(pl.program_id(1) * tm + lax.broadcasted_iota(I32, (tm, HEAD_DIM), 0))

    o = NSA_WIDTH
    kcr_ref[0] = p[:, o:o + KV_WIDTH]
    vcr_ref[0] = p[:, o + KV_WIDTH:o + 2 * KV_WIDTH]
    ks = p[:, o + 2 * KV_WIDTH:o + 3 * KV_WIDTH]
    vs = p[:, o + 3 * KV_WIDTH:o + 4 * KV_WIDTH]
    kw = p[:, o + 4 * KV_WIDTH:o + 5 * KV_WIDTH]
    vw = p[:, o + 5 * KV_WIDTH:o + 6 * KV_WIDTH]
    for g in range(KV_HEADS):
        sl = slice(g * HEAD_DIM, (g + 1) * HEAD_DIM)
        ks_ref[0, g] = jnp.concatenate([_head_rms(ks[:, sl], knw_ref[1:2, :]), kaug], axis=1).astype(BF16)
        kw_ref[0, g] = jnp.concatenate([_head_rms(kw[:, sl], knw_ref[2:3, :]), kaug], axis=1).astype(BF16)
    vst = vs.T.astype(BF16)
    vwt = vw.T.astype(BF16)
    for g in range(KV_HEADS):
        vst_ref[0, g] = vst[g * HEAD_DIM:(g + 1) * HEAD_DIM, :]
        vwt_ref[0, g] = vwt[g * HEAD_DIM:(g + 1) * HEAD_DIM, :]

    o = NSA_WIDTH + 6 * KV_WIDTH
    gates = _sigmoid(p[:, o:o + GATE_PAD])
    gt_ref[0] = gates.T[:NSA_HEADS * 3, :]

    o = o + GATE_PAD
    hq = p[:, o:o + HGRN_WIDTH]
    hf = p[:, o + HGRN_WIDTH:o + 2 * HGRN_WIDTH]
    hi = p[:, o + 2 * HGRN_WIDTH:o + 3 * HGRN_WIDTH]
    hg = p[:, o + 3 * HGRN_WIDTH:o + 4 * HGRN_WIDTH]
    lbp = lbp_ref[...]
    e = jnp.exp(lbp - jnp.max(lbp, axis=0, keepdims=True))
    lb = e[0:1, :] / jnp.sum(e, axis=0, keepdims=True)
    f = lb + (1.0 - lb) * _sigmoid(hf)
    hq_ref[0] = hq * _sigmoid(hq) * (HGRN_DIM ** -0.5)
    hk_ref[0] = 1.0 - f
    hlf_ref[0] = jnp.log(f)
    hv_ref[0] = hi
    hg_ref[0] = _sigmoid(hg)


def _inproj(x, sc1, sh1, norm1_w, w_cat, q_norm_w, k_norm_w, lb_param, tm):
    b, s, d = x.shape
    row = lambda bi, i: (bi, i, 0)
    per_b = lambda bi, i: (bi, 0, 0)
    fixed2 = lambda bi, i: (0, 0)
    aw = 2 * HEAD_DIM
    rest = np.array([2.0 ** (-8.0 * (i + 1) / NSA_HEADS) for i in range(NSA_HEADS)], np.float64) * LOG2E
    qaug = np.zeros((NSA_HEADS, HEAD_DIM), np.float32)
    for i in range(3):
        term = rest.astype(np.float32).astype(BF16).astype(np.float64)
        rest = rest - term
        for dgt, wgt in enumerate((4096.0, 64.0, 1.0)):
            qaug[:, 3 * i + dgt] = term * wgt
    assert np.all(qaug == qaug.astype(BF16).astype(np.float32))
    out_shape = (
        jax.ShapeDtypeStruct((b, NSA_HEADS, s, aw), BF16),
        jax.ShapeDtypeStruct((b, s, KV_WIDTH), F32),
        jax.ShapeDtypeStruct((b, s, KV_WIDTH), F32),
        jax.ShapeDtypeStruct((b, KV_HEADS, s, aw), BF16),
        jax.ShapeDtypeStruct((b, KV_HEADS, HEAD_DIM, s), BF16),
        jax.ShapeDtypeStruct((b, KV_HEADS, s, aw), BF16),
        jax.ShapeDtypeStruct((b, KV_HEADS, HEAD_DIM, s), BF16),
        jax.ShapeDtypeStruct((b, NSA_HEADS * 3, s), F32),
    ) + tuple(jax.ShapeDtypeStruct((b, s, HGRN_WIDTH), F32) for _ in range(5))
    hm = lambda n, w: pl.BlockSpec((1, n, tm, w), lambda bi, i: (bi, 0, i, 0))
    hmt = lambda n, w: pl.BlockSpec((1, n, w, tm), lambda bi, i: (bi, 0, 0, i))
    out_specs = (
        hm(NSA_HEADS, aw),
        pl.BlockSpec((1, tm, KV_WIDTH), row),
        pl.BlockSpec((1, tm, KV_WIDTH), row),
        hm(KV_HEADS, aw), hmt(KV_HEADS, HEAD_DIM),
        hm(KV_HEADS, aw), hmt(KV_HEADS, HEAD_DIM),
        pl.BlockSpec((1, NSA_HEADS * 3, tm), lambda bi, i: (bi, 0, i)),
    ) + tuple(pl.BlockSpec((1, tm, HGRN_WIDTH), row) for _ in range(5))
    return pl.pallas_call(
        _inproj_kernel,
        grid=(b, s // tm),
        in_specs=[pl.BlockSpec((1, tm, d), row),
                  pl.BlockSpec((1, 1, d), per_b),
                  pl.BlockSpec((1, 1, d), per_b),
                  pl.BlockSpec((1, d), fixed2),
                  pl.BlockSpec((d, PROJ_COLS), fixed2),
                  pl.BlockSpec((1, HEAD_DIM), fixed2),
                  pl.BlockSpec((3, HEAD_DIM), fixed2),
                  pl.BlockSpec(lb_param.shape, fixed2),
                  pl.BlockSpec((NSA_HEADS, HEAD_DIM), fixed2)],
        out_specs=out_specs,
        out_shape=out_shape,
        compiler_params=_cparams("parallel", "parallel"),
        name="inproj",
    )(x, sc1, sh1, norm1_w, w_cat, q_norm_w, k_norm_w, lb_param, jnp.asarray(qaug))


def _gelu_tanh(x):
    return 0.5 * x * (1.0 + jnp.tanh(0.7978845608028654 * (x + 0.044715 * x * x * x)))


def _compress_kernel(kch_ref, vch_ref, pos_ref, wa_ref, wb_ref, b1_ref, w2_ref, knw_ref,
                     kc_ref, vct_ref):
    n = kch_ref.shape[1]
    outs = []
    for br, ch_ref in enumerate((kch_ref, vch_ref)):
        ch = ch_ref[0]
        a = _dot((ch + pos_ref[br, 0:1, :]).astype(BF16), wa_ref[br])
        bm = _dot((ch + pos_ref[br, 1:2, :]).astype(BF16), wb_ref[br])
        pre = a + pltpu.roll(bm, n - 1, 0) + b1_ref[br]
        hid = _gelu_tanh(pre).astype(BF16)
        outs.append([_dot(hid[:, g * CMP_HIDDEN:(g + 1) * CMP_HIDDEN], w2_ref[br]) for g in range(KV_HEADS)])
    end_digits = _pos_digits(lax.broadcasted_iota(I32, (n, HEAD_DIM), 0) * CMP_STRIDE + (CMP_BLOCK - 1))
    for g in range(KV_HEADS):
        kc_ref[0, g] = jnp.concatenate([_head_rms(outs[0][g], knw_ref[0:1, :]), end_digits], axis=1).astype(BF16)
    vct = jnp.concatenate(outs[1], axis=1).T.astype(BF16)
    for g in range(KV_HEADS):
        vct_ref[0, g] = vct[g * HEAD_DIM:(g + 1) * HEAD_DIM, :]


def _compress(kc_raw, vc_raw, cmp_pos, cmp_w1, cmp_b1, cmp_w2, k_norm_w):
    b, s, _ = kc_raw.shape
    n = s // CMP_STRIDE
    half = CMP_STRIDE
    cw = CMP_STRIDE * KV_WIDTH
    kch = kc_raw.reshape(b, n, cw)
    vch = vc_raw.reshape(b, n, cw)
    pos = cmp_pos.reshape(2, 2, half, 1, HEAD_DIM)
    pos = jnp.broadcast_to(pos, (2, 2, half, KV_HEADS, HEAD_DIM)).reshape(2, 2, cw)
    w1 = cmp_w1.reshape(2, 2, half, HEAD_DIM, CMP_HIDDEN)
    eye = jnp.eye(KV_HEADS, dtype=F32)
    wfull = jnp.einsum('rhjdn,gk->rhjgdkn', w1, eye).reshape(2, 2, cw, KV_HEADS * CMP_HIDDEN).astype(BF16)
    b1 = jnp.tile(cmp_b1.reshape(2, 1, CMP_HIDDEN), (1, 1, KV_HEADS))
    fix = lambda r: (lambda bi: (0,) * r)
    return pl.pallas_call(
        _compress_kernel,
        grid=(b,),
        in_specs=[pl.BlockSpec((1, n, cw), lambda bi: (bi, 0, 0)),
                  pl.BlockSpec((1, n, cw), lambda bi: (bi, 0, 0)),
                  pl.BlockSpec((2, 2, cw), fix(3)),
                  pl.BlockSpec((2, cw, KV_HEADS * CMP_HIDDEN), fix(3)),
                  pl.BlockSpec((2, cw, KV_HEADS * CMP_HIDDEN), fix(3)),
                  pl.BlockSpec((2, 1, KV_HEADS * CMP_HIDDEN), fix(3)),
                  pl.BlockSpec((2, CMP_HIDDEN, HEAD_DIM), fix(3)),
                  pl.BlockSpec((3, HEAD_DIM), fix(2))],
        out_specs=(pl.BlockSpec((1, KV_HEADS, n, 2 * HEAD_DIM), lambda bi: (bi, 0, 0, 0)),
                   pl.BlockSpec((1, KV_HEADS, HEAD_DIM, n), lambda bi: (bi, 0, 0, 0))),
        out_shape=(jax.ShapeDtypeStruct((b, KV_HEADS, n, 2 * HEAD_DIM), BF16),
                   jax.ShapeDtypeStruct((b, KV_HEADS, HEAD_DIM, n), BF16)),
        compiler_params=_cparams("parallel"),
        name="compress",
    )(kch, vch, pos, wfull[:, 0], wfull[:, 1], b1, cmp_w2.astype(BF16), k_norm_w)


def _nsa_kernel(q_ref, kc_ref, vct_ref, ks_ref, vst_ref, kw_ref, vwt_ref, gt_ref, cdiff_ref, wdiff_ref,
                ovl_ref, oh_ref, onw_ref, wmask_ref, o_ref, buf_a, buf_b, m_scr, acc_scr, oc_scr, bias_scr,
                lst, cnt, *, n_top):
    q0 = pl.program_id(2) * TQ
    ncols = HEADS_PER_KV * TQ
    q = q_ref[0].reshape(ncols, 2 * HEAD_DIM)
    ns = ovl_ref.shape[0]

    def compress_and_select(nk, nb):
        s = jnp.where(cdiff_ref[0:nk, :] <= q0, _dot_nt(kc_ref[0, 0, 0:nk, :], q), -jnp.inf)
        m = jnp.max(s, axis=0, keepdims=True)
        m = jnp.where(m == -jnp.inf, 0.0, m)
        e = jnp.exp2(s - m)
        p = e / jnp.maximum(jnp.sum(e, axis=0, keepdims=True), 1e-30)
        oc_scr[...] = _dot(vct_ref[0, 0, :, 0:nk], p.astype(BF16))

        psum = p[:, 0:TQ]
        for hh in range(1, HEADS_PER_KV):
            psum = psum + p[:, hh * TQ:(hh + 1) * TQ]
        imp = _split_dot(ovl_ref[0:nb, 0:nk], psum)
        blk = lax.broadcasted_iota(I32, (nb, TQ), 0)
        tq = q0 + lax.broadcasted_iota(I32, (nb, TQ), 1)
        cur = tq >> 6
        forced = (blk == 0) | (blk == cur) | (blk == cur - 1)
        rank = jnp.where(forced, BIG, jnp.where(blk * SEL_BLOCK <= tq, imp, -BIG))
        blkf = blk.astype(F32)

        bias = jnp.full((nb, TQ), -1e30, F32)
        for _ in range(min(n_top, nb)):
            mx = jnp.max(rank, axis=0, keepdims=True)
            first = jnp.min(jnp.where(rank == mx, blkf, float(nb)), axis=0, keepdims=True)
            hit = blkf == first
            rank = jnp.where(hit, -jnp.inf, rank)
            bias = jnp.where(hit, 0.0, bias)
        bias_scr[...] = jnp.full((128, TQ), -1e30, F32)
        bias_scr[0:nb, :] = jnp.where(blk == 0, -1e30, bias)

    nc = kc_ref.shape[2]
    quarter = (q0 + TQ - 1) // (ks_ref.shape[2] // NSA_SIZE_VARIANTS)
    for v in range(NSA_SIZE_VARIANTS):
        @pl.when(quarter == v)
        def _():
            compress_and_select(nc * (v + 1) // NSA_SIZE_VARIANTS, ns * (v + 1) // NSA_SIZE_VARIANTS)

    o_c = oc_scr[...]
    bias = bias_scr[...]

    bias_t = bias.T.astype(BF16)
    qq = jnp.concatenate([q, jnp.concatenate([bias_t] * HEADS_PER_KV, axis=0)], axis=1)
    ones_rows = jnp.ones((16, TK), BF16)

    def scores(j):
        k0 = pl.multiple_of(j * TK, TK)
        kk = jnp.concatenate([ks_ref[0, 0, pl.ds(k0, TK), :], oh_ref[pl.ds(k0, TK), :]], axis=1)
        return _dot_nt(kk, qq)

    def consume(buf, j, causal, part):
        sc = buf[...]
        if causal:
            sc = jnp.where(wdiff_ref[0:TK, :] + (q0 - j * TK) >= 0, sc, -1e30)
        k0 = pl.multiple_of(j * TK, TK)
        m_run = m_scr[part]
        m_new = jnp.maximum(m_run, jnp.max(sc, axis=0, keepdims=True))
        ex = jnp.exp2(sc - m_new).astype(BF16)
        va = jnp.concatenate([vst_ref[0, 0, :, pl.ds(k0, TK)], ones_rows], axis=0)
        acc_scr[part] = jnp.exp2(m_run - m_new) * acc_scr[part] + _dot(va, ex)
        m_scr[part] = m_new

    n_past = q0 // TK
    blocks_per_tile = TK // SEL_BLOCK
    cnt[0] = 0
    for j in range(ks_ref.shape[2] // TK):
        wanted = jnp.max(bias[j * blocks_per_tile:(j + 1) * blocks_per_tile, :]) == 0.0

        @pl.when(wanted & (j < n_past))
        def _():
            lst[cnt[0]] = j
            cnt[0] = cnt[0] + 1

    n_sel = cnt[0]
    lst[n_sel] = n_past

    buf_a[...] = scores(lst[0])

    s0 = jnp.where(wdiff_ref[0:SEL_BLOCK, :] + q0 >= 0, _dot_nt(ks_ref[0, 0, 0:SEL_BLOCK, :], q), -1e30)
    m0 = jnp.max(s0, axis=0, keepdims=True)
    v0 = jnp.concatenate([vst_ref[0, 0, :, 0:SEL_BLOCK], jnp.ones((16, SEL_BLOCK), BF16)], axis=0)
    m_scr[0] = m0
    acc_scr[0] = _dot(v0, jnp.exp2(s0 - m0).astype(BF16))
    m_scr[1] = jnp.full((1, ncols), -1e30, F32)
    acc_scr[1] = jnp.zeros((HEAD_DIM + 16, ncols), F32)

    nw = WINDOW + TQ
    start = pl.multiple_of(jnp.maximum(q0 - WINDOW, 0), TQ)
    sw = _dot_nt(kw_ref[0, 0, pl.ds(start, nw), :], q) + wmask_ref[0]
    ew = jnp.exp2(sw - jnp.max(sw, axis=0, keepdims=True))
    vw_aug = jnp.concatenate([vwt_ref[0, 0, :, pl.ds(start, nw)], jnp.ones((16, nw), BF16)], axis=0)
    acc_w = _dot(vw_aug, ew.astype(BF16))
    o_w = acc_w[0:HEAD_DIM, :] / acc_w[HEAD_DIM:HEAD_DIM + 1, :]

    def tiles(first, count):
        for u in range(0, count, 2):
            buf_b[...] = scores(lst[first + u + 1])
            consume(buf_a, lst[first + u], False, 0)
            buf_a[...] = scores(lst[first + u + 2])
            consume(buf_b, lst[first + u + 1], False, 1)
        return 0

    lax.fori_loop(0, n_sel // 4, lambda i, _: tiles(4 * i, 4), 0)
    lax.fori_loop(0, (n_sel // 2) % 2, lambda i, _: tiles((n_sel // 4) * 4, 2), 0)

    @pl.when(n_sel % 2 == 1)
    def _():
        buf_b[...] = scores(n_past)
        consume(buf_a, lst[n_sel - 1], False, 0)
        consume(buf_b, n_past, True, 1)

    @pl.when(n_sel % 2 == 0)
    def _():
        consume(buf_a, n_past, True, 0)

    m_all = jnp.maximum(m_scr[0], m_scr[1])
    acc_s = jnp.exp2(m_scr[0] - m_all) * acc_scr[0] + jnp.exp2(m_scr[1] - m_all) * acc_scr[1]
    o_s = acc_s[0:HEAD_DIM, :] / acc_s[HEAD_DIM:HEAD_DIM + 1, :]

    gt = gt_ref[0, 0]
    outs = []
    for hh in range(HEADS_PER_KV):
        cs = slice(hh * TQ, (hh + 1) * TQ)
        o = (gt[3 * hh:3 * hh + 1, :] * o_c[:, cs] + gt[3 * hh + 1:3 * hh + 2, :] * o_s[:, cs]
             + gt[3 * hh + 2:3 * hh + 3, :] * o_w[:, cs])
        o = o * lax.rsqrt(jnp.mean(o * o, axis=0, keepdims=True) + RMS_EPS) * onw_ref[0, hh]
        outs.append(o)
    o_ref[0] = jnp.concatenate(outs, axis=0).T


def _nsa(q, kc, vct, ks, vst, kw, vwt, gates_t, attn_out_norm_w):
    b, _, s, aw = q.shape
    nc = kc.shape[2]
    ns = s // SEL_BLOCK
    n_top = min(N_SELECT, ns)
    ncols = HEADS_PER_KV * TQ
    nw = WINDOW + TQ
    tl = np.arange(ncols)[None, :] & (TQ - 1)
    cdiff = jnp.asarray((np.arange(nc)[:, None] * CMP_STRIDE + (CMP_BLOCK - 1) - tl).astype(np.int32))
    wdiff_np = (tl - np.arange(nw)[:, None]).astype(np.int32)
    wdiff = jnp.asarray(wdiff_np)
    n_off = WINDOW // TQ + 1
    dist_np = wdiff_np[None] + (np.arange(n_off) * TQ)[:, None, None]
    wmask = jnp.asarray(np.where((dist_np >= 0) & (dist_np < WINDOW), 0.0, -np.inf).astype(np.float32))
    ci = np.arange(nc)[None, :] * CMP_STRIDE
    bj = np.arange(ns)[:, None]
    ovl = ((ci < (bj + 1) * SEL_BLOCK) & (ci + CMP_BLOCK > bj * SEL_BLOCK) & (np.arange(nc)[None, :] < nc - 1))
    ovl = jnp.asarray(ovl.astype(np.float32)).astype(BF16)
    assert ns <= 128
    onehot = (np.arange(s)[:, None] // SEL_BLOCK == np.arange(128)[None, :])
    onehot = jnp.asarray(onehot.astype(np.float32)).astype(BF16)
    onw = jnp.broadcast_to(attn_out_norm_w.reshape(KV_HEADS, HEADS_PER_KV, HEAD_DIM, 1),
                           (KV_HEADS, HEADS_PER_KV, HEAD_DIM, TQ))
    gt = gates_t.reshape(b, KV_HEADS, HEADS_PER_KV * 3, s)
    per_bg = lambda bi, g, i: (bi, g, 0, 0)
    fixed = lambda bi, g, i: (0, 0)
    return pl.pallas_call(
        functools.partial(_nsa_kernel, n_top=n_top),
        grid=(b, KV_HEADS, s // TQ),
        in_specs=[pl.BlockSpec((1, HEADS_PER_KV, TQ, aw), lambda bi, g, i: (bi, g, i, 0)),
                  pl.BlockSpec((1, 1, nc, aw), per_bg),
                  pl.BlockSpec((1, 1, HEAD_DIM, nc), per_bg),
                  pl.BlockSpec((1, 1, s, aw), per_bg),
                  pl.BlockSpec((1, 1, HEAD_DIM, s), per_bg),
                  pl.BlockSpec((1, 1, s, aw), per_bg),
                  pl.BlockSpec((1, 1, HEAD_DIM, s), per_bg),
                  pl.BlockSpec((1, 1, HEADS_PER_KV * 3, TQ), lambda bi, g, i: (bi, g, 0, i)),
                  pl.BlockSpec((nc, ncols), fixed, pipeline_mode=pl.Buffered(1)),
                  pl.BlockSpec((nw, ncols), fixed, pipeline_mode=pl.Buffered(1)),
                  pl.BlockSpec((ns, nc), fixed, pipeline_mode=pl.Buffered(1)),
                  pl.BlockSpec((s, 128), fixed, pipeline_mode=pl.Buffered(1)),
                  pl.BlockSpec((1, HEADS_PER_KV, HEAD_DIM, TQ), lambda bi, g, i: (g, 0, 0, 0)),
                  pl.BlockSpec((1, nw, ncols), lambda bi, g, i: (jnp.minimum(i, n_off - 1), 0, 0))],
        out_specs=pl.BlockSpec((1, TQ, HEADS_PER_KV * HEAD_DIM), lambda bi, g, i: (bi, i, g)),
        out_shape=jax.ShapeDtypeStruct((b, s, NSA_WIDTH), F32),
        scratch_shapes=[pltpu.VMEM((TK, ncols), F32), pltpu.VMEM((TK, ncols), F32),
                        pltpu.VMEM((2, 1, ncols), F32), pltpu.VMEM((2, HEAD_DIM + 16, ncols), F32),
                        pltpu.VMEM((HEAD_DIM, ncols), F32), pltpu.VMEM((128, TQ), F32),
                        pltpu.SMEM((s // TK + 1,), I32), pltpu.SMEM((1,), I32)],
        compiler_params=_cparams("parallel", "parallel", "arbitrary"),
        name="nsa",
    )(q, kc, vct, ks, vst, kw, vwt, gt, cdiff, wdiff, ovl, onehot, onw, wmask)


def _hgrn_kernel(q_ref, k_ref, lf_ref, v_ref, g_ref, onw_ref, cm_ref, o_ref, state_scr, *, n_chunks):
    c = HGRN_CHUNK

    @pl.when(pl.program_id(1) == 0)
    def _():
        state_scr[...] = jnp.zeros_like(state_scr)

    ri = lax.broadcasted_iota(I32, (c, c), 0)
    ci = lax.broadcasted_iota(I32, (c, c), 1)
    rsub = ri // HGRN_SUB
    diag = ri == ci

    def head_chunk(r0, hd, state_t):
        cols = slice(hd * HGRN_DIM, (hd + 1) * HGRN_DIM)
        q = q_ref[0, pl.ds(r0, c), cols]
        k = k_ref[0, pl.ds(r0, c), cols]
        lf = lf_ref[0, pl.ds(r0, c), cols] * LOG2E
        v = v_ref[0, pl.ds(r0, c), cols]
        cm = cm_ref[...]
        l1 = lf.astype(BF16)
        rest = lf - l1.astype(F32)
        l2 = rest.astype(BF16)
        l3 = (rest - l2.astype(F32)).astype(BF16)
        cum = _dot(cm, l1) + _dot(cm, l2) + _dot(cm, l3)
        o = _dot_nt((q * jnp.exp2(cum)).astype(BF16), state_t.astype(BF16))
        scores = jnp.where(diag, jnp.sum(q * k, axis=-1, keepdims=True), 0.0)

        def factored(ref, mask, acc):
            qs = q * jnp.exp2(jnp.minimum(cum - ref, 0.0))
            kd = k * jnp.exp2(jnp.minimum(ref - cum, 0.0))
            return jnp.where(mask, _dot_nt(qs.astype(BF16), kd.astype(BF16)), acc)

        for i in range(1, c // HGRN_SUB):
            scores = factored(cum[i * HGRN_SUB - 1:i * HGRN_SUB, :], (rsub == i) & (ci < i * HGRN_SUB), scores)
        for d in range(1, HGRN_SUB):
            ksh = pltpu.roll(k, d, 0)
            csh = pltpu.roll(cum, d, 0)
            w = jnp.sum(q * ksh * jnp.exp2(cum - csh), axis=-1, keepdims=True)
            scores = jnp.where((ri - ci == d) & ((ri & (HGRN_SUB - 1)) >= d), w, scores)
        o = o + _dot(scores.astype(BF16), v.astype(BF16))
        last = cum[c - 1:c, :]
        kd = (k * jnp.exp2(last - cum)).astype(BF16)
        state_t = state_t * jnp.exp2(last) + _dot(v.T.astype(BF16), kd)
        o = o * g_ref[0, pl.ds(r0, c), cols]
        o = o * lax.rsqrt(jnp.mean(o * o, axis=-1, keepdims=True) + RMS_EPS) * onw_ref[:, cols]
        o_ref[0, pl.ds(r0, c), cols] = o
        return state_t

    def chunk(ck, states):
        r0 = pl.multiple_of(ck * c, c)
        return tuple(head_chunk(r0, hd, states[hd]) for hd in range(HGRN_HEADS))

    states = lax.fori_loop(0, n_chunks, chunk, tuple(state_scr[hd] for hd in range(HGRN_HEADS)))
    for hd in range(HGRN_HEADS):
        state_scr[hd] = states[hd]


def _hgrn(hq, hk, hlf, hv, hg, rec_out_norm_w, rows):
    b, s, _ = hq.shape
    cm = jnp.asarray(np.tril(np.ones((HGRN_CHUNK, HGRN_CHUNK), np.float32))).astype(BF16)
    blk = pl.BlockSpec((1, rows, HGRN_WIDTH), lambda bi, i: (bi, i, 0))
    return pl.pallas_call(
        functools.partial(_hgrn_kernel, n_chunks=rows // HGRN_CHUNK),
        grid=(b, s // rows),
        in_specs=[blk, blk, blk, blk, blk,
                  pl.BlockSpec((1, HGRN_WIDTH), lambda bi, i: (0, 0)),
                  pl.BlockSpec(cm.shape, lambda bi, i: (0, 0))],
        out_specs=blk,
        out_shape=jax.ShapeDtypeStruct((b, s, HGRN_WIDTH), F32),
        scratch_shapes=[pltpu.VMEM((HGRN_HEADS, HGRN_DIM, HGRN_DIM), F32)],
        compiler_params=_cparams("parallel", "arbitrary"),
        name="hgrn",
    )(hq, hk, hlf, hv, hg, rec_out_norm_w.reshape(1, HGRN_WIDTH), cm)


def _outproj_kernel(x_ref, a_ref, r_ref, wa_ref, wr_ref, gt_ref, sc_ref, sh_ref, n2_ref, x1_ref, h2_ref, h2p_ref):
    mixed = _dot(a_ref[0].astype(BF16), wa_ref[...]) + _dot(r_ref[0].astype(BF16), wr_ref[...])
    x1 = x_ref[0] + gt_ref[0] * mixed
    x1_ref[0] = x1
    ms = jnp.mean(x1 * x1, axis=-1, keepdims=True)
    h2 = x1 * lax.rsqrt(ms + RMS_EPS) * n2_ref[...] * (1.0 + sc_ref[0]) + sh_ref[0]
    h2_ref[0] = h2
    h2p_ref[0] = _pack_bf16_pair(h2[:, :D_MODEL // 2], h2[:, D_MODEL // 2:])


def _outproj(x, attn, rec, w_out, gt1, sc2, sh2, norm2_w, tm):
    b, s, d = x.shape
    row = lambda bi, i: (bi, i, 0)
    per_b = lambda bi, i: (bi, 0, 0)
    fixed2 = lambda bi, i: (0, 0)
    w = w_out.astype(BF16)
    return pl.pallas_call(
        _outproj_kernel,
        grid=(b, s // tm),
        in_specs=[pl.BlockSpec((1, tm, d), row),
                  pl.BlockSpec((1, tm, NSA_WIDTH), row),
                  pl.BlockSpec((1, tm, HGRN_WIDTH), row),
                  pl.BlockSpec((NSA_WIDTH, d), fixed2),
                  pl.BlockSpec((HGRN_WIDTH, d), fixed2),
                  pl.BlockSpec((1, 1, d), per_b),
                  pl.BlockSpec((1, 1, d), per_b),
                  pl.BlockSpec((1, 1, d), per_b),
                  pl.BlockSpec((1, d), fixed2)],
        out_specs=(pl.BlockSpec((1, tm, d), row), pl.BlockSpec((1, tm, d), row), pl.BlockSpec((1, tm, d // 2), row)),
        out_shape=(jax.ShapeDtypeStruct((b, s, d), F32), jax.ShapeDtypeStruct((b, s, d), F32),
                   jax.ShapeDtypeStruct((b, s, d // 2), jnp.uint32)),
        compiler_params=_cparams("parallel", "parallel"),
        name="outproj",
    )(x, attn, rec, w[:NSA_WIDTH], w[NSA_WIDTH:], gt1, sc2, sh2, norm2_w)


def _mixer(x, c, ada_w, ada_b, norm1_w, norm2_w, w_in, q_norm_w, k_norm_w, cmp_pos, cmp_w1, cmp_b1, cmp_w2,
           attn_out_norm_w, hgrn_lb_param, rec_out_norm_w, w_out):
    b, s, d = x.shape
    mod = _mod(c, ada_w, ada_b)
    sh1, sc1, gt1, sh2, sc2, gt2 = [m.reshape(b, 1, d) for m in jnp.split(mod, 6, axis=-1)]
    o = NSA_WIDTH + 6 * KV_WIDTH
    w_cat = jnp.concatenate([w_in[:, :o], w_in[:, o:o + NSA_HEADS * 3],
                             jnp.zeros((d, GATE_PAD - NSA_HEADS * 3), w_in.dtype),
                             w_in[:, o + NSA_HEADS * 3:]], axis=1).astype(BF16)
    tm = min(512, s)
    (q, kc_raw, vc_raw, ks, vst, kw, vwt, gates_t, hq, hk, hlf, hv, hg) = _inproj(
        x, sc1, sh1, norm1_w.reshape(1, d), w_cat, q_norm_w.reshape(1, HEAD_DIM), k_norm_w, hgrn_lb_param, tm)
    kc, vct = _compress(kc_raw, vc_raw, cmp_pos, cmp_w1, cmp_b1, cmp_w2, k_norm_w)
    attn = _nsa(q, kc, vct, ks, vst, kw, vwt, gates_t, attn_out_norm_w)
    rec = _hgrn(hq, hk, hlf, hv, hg, rec_out_norm_w, min(1024, s))
    x1, h2, h2p = _outproj(x, attn, rec, w_out, gt1, sc2, sh2, norm2_w.reshape(1, d), min(1024, s))
    return x1, h2, h2p, gt2


def _router_kernel(h_ref, rwt_ref, bias_ref, tri_ref, ones_ref, idx_ref, w_ref, rank_ref, cnt_ref, carry_scr, *, tr):
    @pl.when(pl.program_id(0) == 0)
    def _():
        carry_scr[...] = jnp.zeros_like(carry_scr)

    h = h_ref[...]
    h_hi = h.astype(BF16)
    h_lo = (h - h_hi.astype(F32)).astype(BF16)
    logits = _dot_nt(rwt_ref[0], h_hi) + _dot_nt(rwt_ref[1], h_hi) + _dot_nt(rwt_ref[0], h_lo)
    scores = _sigmoid(logits)
    biased = scores + bias_ref[...]
    neg = -jnp.inf

    gs = []
    for g in range(N_GROUPS):
        sub = biased[g * GROUP_SIZE:(g + 1) * GROUP_SIZE, :]
        m1 = jnp.max(sub, axis=0, keepdims=True)
        dup = jnp.sum((sub == m1).astype(F32), axis=0, keepdims=True)
        m2 = jnp.max(jnp.where(sub < m1, sub, neg), axis=0, keepdims=True)
        gs.append(m1 + jnp.where(dup >= 2.0, m1, m2))
    parts = []
    for g in range(N_GROUPS):
        beaten = jnp.zeros_like(gs[g])
        for g2 in range(N_GROUPS):
            if g2 != g:
                beats = (gs[g2] >= gs[g]) if g2 < g else (gs[g2] > gs[g])
                beaten = beaten + beats.astype(F32)
        sub = biased[g * GROUP_SIZE:(g + 1) * GROUP_SIZE, :]
        parts.append(jnp.where(beaten < float(TOPK_GROUPS), sub, neg))
    cand = jnp.concatenate(parts, axis=0)

    rowf = lax.broadcasted_iota(I32, (N_EXPERTS, tr), 0).astype(F32)
    idx_rows, w_rows, hits = [], [], []
    multi = jnp.zeros((N_EXPERTS, tr), F32)
    for _ in range(TOP_K):
        mx = jnp.max(cand, axis=0, keepdims=True)
        first = jnp.min(jnp.where(cand == mx, rowf, float(N_EXPERTS)), axis=0, keepdims=True)
        hit = rowf == first
        idx_rows.append(first)
        w_rows.append(jnp.sum(jnp.where(hit, scores, 0.0), axis=0, keepdims=True))
        cand = jnp.where(hit, neg, cand)
        multi = jnp.where(hit, 1.0, multi)
    w = jnp.concatenate(w_rows, axis=0)
    w_ref[...] = w / jnp.sum(w, axis=0, keepdims=True) * ROUTED_SCALE
    idx = jnp.concatenate(idx_rows, axis=0)
    idx_ref[...] = idx.astype(I32)

    carry = carry_scr[...]
    mb = multi.astype(BF16)
    before = _dot(mb, tri_ref[...]) + jnp.concatenate([carry] * (tr // 128), axis=1)
    rank_rows = [jnp.sum(jnp.where(rowf == idx_rows[k], before, 0.0), axis=0, keepdims=True) for k in range(TOP_K)]
    rank_ref[...] = jnp.concatenate(rank_rows, axis=0).astype(I32)
    carry = carry + _dot(mb, ones_ref[...])
    carry_scr[...] = carry
    cnt_ref[...] = carry


def _router(h2, router_w, router_bias, tr):
    t, d = h2.shape
    tri = jnp.asarray(np.triu(np.ones((tr, tr), np.float32), 1)).astype(BF16)
    ones = jnp.ones((tr, 128), BF16)
    tok = pl.BlockSpec((TOP_K, tr), lambda i: (0, i))
    fixed = lambda i: (0, 0)
    rwt = router_w.T
    rwt_hi = rwt.astype(BF16)
    rwt_split = jnp.stack([rwt_hi, (rwt - rwt_hi.astype(F32)).astype(BF16)])
    return pl.pallas_call(
        functools.partial(_router_kernel, tr=tr),
        grid=(t // tr,),
        in_specs=[pl.BlockSpec((tr, d), lambda i: (i, 0)),
                  pl.BlockSpec((2, N_EXPERTS, d), lambda i: (0, 0, 0)),
                  pl.BlockSpec((N_EXPERTS, 1), fixed),
                  pl.BlockSpec((tr, tr), fixed),
                  pl.BlockSpec((tr, 128), fixed)],
        out_specs=(tok, tok, tok, pl.BlockSpec((N_EXPERTS, 128), fixed)),
        out_shape=(jax.ShapeDtypeStruct((TOP_K, t), I32), jax.ShapeDtypeStruct((TOP_K, t), F32),
                   jax.ShapeDtypeStruct((TOP_K, t), I32), jax.ShapeDtypeStruct((N_EXPERTS, 128), F32)),
        scratch_shapes=[pltpu.VMEM((N_EXPERTS, 128), F32)],
        compiler_params=_cparams("arbitrary"),
        name="router",
    )(h2, rwt_split, router_bias.reshape(N_EXPERTS, 1), tri, ones)


def _pack_bf16_pair(a, b):
    ua = lax.bitcast_convert_type(a.astype(BF16).astype(F32), jnp.uint32)
    ub = lax.bitcast_convert_type(b.astype(BF16).astype(F32), jnp.uint32)
    return ua | (ub >> 16)


def _unpack_bf16_pair(w):
    a = lax.bitcast_convert_type(w & jnp.uint32(0xFFFF0000), F32)
    b = lax.bitcast_convert_type(w << 16, F32)
    return a, b


def _slot_kernel(ps_ref, idx_ref, rank_ref, slot_ref):
    idx = idx_ref[...]

    def body(e, acc):
        return jnp.where(idx == e, ps_ref[e], acc)

    slot_ref[...] = lax.fori_loop(0, N_EXPERTS, body, jnp.zeros_like(idx)) + rank_ref[...]


def _slots(pad_start, idx, rank, tt):
    t = idx.shape[1]
    tok = pl.BlockSpec((TOP_K, tt), lambda i, ps: (0, i))
    return pl.pallas_call(
        _slot_kernel,
        grid_spec=pltpu.PrefetchScalarGridSpec(num_scalar_prefetch=1, grid=(t // tt,),
                                               in_specs=[tok, tok], out_specs=tok),
        out_shape=jax.ShapeDtypeStruct((TOP_K, t), I32),
        compiler_params=_cparams("parallel"),
        name="slots",
    )(pad_start, idx, rank)


SC_CORES = 2
SC_SUBCORES = 16
SC_CHUNK = 64
MOE_COMBINE_PARTS = 2


def _sc_mesh():
    return plsc.VectorSubcoreMesh(core_axis_name="c", subcore_axis_name="s")


def _sc_dispatch(h2p, slot_chunks, n_rows):
    t, dw = h2p.shape
    per = slot_chunks.shape[0] // (SC_CORES * SC_SUBCORES)

    def body(h_hbm, slot_hbm, xs_hbm, idx_v, rows_v, sem):
        wid = lax.axis_index("s") * SC_CORES + lax.axis_index("c")

        @pl.loop(0, per)
        def _(c):
            ch = wid * per + c
            pltpu.sync_copy(slot_hbm.at[ch], idx_v)
            pltpu.sync_copy(h_hbm.at[pl.ds(ch * SC_CHUNK, SC_CHUNK)], rows_v)
            copies = [pltpu.async_copy(rows_v, xs_hbm.at[idx_v.at[k]], sem) for k in range(TOP_K)]
            for cp in copies:
                cp.wait()

    return pl.kernel(
        body, out_type=jax.ShapeDtypeStruct((n_rows, dw), h2p.dtype), mesh=_sc_mesh(),
        scratch_types=[pltpu.VMEM((TOP_K, SC_CHUNK), I32), pltpu.VMEM((SC_CHUNK, dw), h2p.dtype),
                       pltpu.SemaphoreType.DMA],
    )(h2p, slot_chunks)


def _sc_gather(ys, slot_chunks, t):
    dw = ys.shape[1]
    per = slot_chunks.shape[0] // (SC_CORES * SC_SUBCORES)

    def body(ys_hbm, slot_hbm, yg_hbm, idx_v, rows_v, gsem, wsem):
        wid = lax.axis_index("s") * SC_CORES + lax.axis_index("c")

        @pl.loop(0, per)
        def _(c):
            ch = wid * per + c
            pltpu.sync_copy(slot_hbm.at[ch], idx_v)
            gathers = [None] * TOP_K
            writes = [None] * TOP_K
            gathers[0] = pltpu.async_copy(ys_hbm.at[idx_v.at[0]], rows_v.at[0], gsem)
            for k in range(TOP_K):
                gathers[k].wait()
                if k + 1 < TOP_K:
                    if k >= 1:
                        writes[k - 1].wait()
                    gathers[k + 1] = pltpu.async_copy(ys_hbm.at[idx_v.at[k + 1]], rows_v.at[(k + 1) % 2], gsem)
                writes[k] = pltpu.async_copy(rows_v.at[k % 2], yg_hbm.at[k, pl.ds(ch * SC_CHUNK, SC_CHUNK)], wsem)
            writes[TOP_K - 2].wait()
            writes[TOP_K - 1].wait()

    return pl.kernel(
        body, out_type=jax.ShapeDtypeStruct((TOP_K, t, dw), ys.dtype), mesh=_sc_mesh(),
        scratch_types=[pltpu.VMEM((TOP_K, SC_CHUNK), I32), pltpu.VMEM((2, SC_CHUNK, dw), ys.dtype),
                       pltpu.SemaphoreType.DMA, pltpu.SemaphoreType.DMA],
    )(ys, slot_chunks)


def _experts_kernel(be_ref, nu_ref, bv_ref, run_ref, xs_hbm, wg_hbm, wu_hbm, wd_hbm, ys_ref,
                    xring, rsem, gring, uring, dring, wsem):
    i = pl.program_id(0)
    half = D_MODEL // 2
    n_used = nu_ref[0]
    n_steps = pl.num_programs(0)

    def weight_copies(blk):
        ex = be_ref[blk]
        slot = run_ref[blk] % EXPERT_RING
        return [pltpu.make_async_copy(src.at[ex], ring.at[slot], wsem.at[a, slot])
                for a, (src, ring) in enumerate(((wg_hbm, gring), (wu_hbm, uring), (wd_hbm, dring)))]

    def starts_run(blk):
        return run_ref[blk] != run_ref[jnp.maximum(blk - 1, 0)]

    @pl.when(i == 0)
    def _():
        for cp in weight_copies(jnp.int32(0)):
            cp.start()

        @pl.when((n_steps > 1) & starts_run(jnp.int32(1)))
        def _():
            for cp in weight_copies(jnp.int32(1)):
                cp.start()

    ahead = jnp.minimum(i + (EXPERT_RING - 1), n_steps - 1)

    @pl.when((i + (EXPERT_RING - 1) < n_steps) & starts_run(ahead))
    def _():
        for cp in weight_copies(ahead):
            cp.start()

    @pl.when((i == 0) | starts_run(i))
    def _():
        for cp in weight_copies(i):
            cp.wait()

    wslot = run_ref[i] % EXPERT_RING
    wg_ref, wu_ref, wd_ref = gring.at[wslot], uring.at[wslot], dring.at[wslot]

    def fetch(blk):
        slot = blk % EXPERT_RING
        return pltpu.make_async_copy(xs_hbm.at[pl.ds(pl.multiple_of(blk * EXPERT_BLOCK, EXPERT_BLOCK), EXPERT_BLOCK)],
                                     xring.at[slot], rsem.at[slot])

    @pl.when(i == 0)
    def _():
        for first in range(EXPERT_RING - 1):
            @pl.when(first < n_used)
            def _():
                fetch(jnp.int32(first)).start()

    @pl.when(i + (EXPERT_RING - 1) < n_used)
    def _():
        fetch(i + (EXPERT_RING - 1)).start()

    @pl.when(i < n_used)
    def _():
        fetch(i).wait()

    xs_ref = xring.at[i % EXPERT_RING]

    def ffn(rows):
        live = lax.broadcasted_iota(I32, (rows, xs_ref.shape[1]), 0) < bv_ref[i]
        xa, xb = _unpack_bf16_pair(jnp.where(live, xs_ref[0:rows, :], jnp.uint32(0)))
        xa, xb = xa.astype(BF16), xb.astype(BF16)
        g = _dot(xa, wg_ref[:half, :].astype(BF16)) + _dot(xb, wg_ref[half:, :].astype(BF16))
        u = _dot(xa, wu_ref[:half, :].astype(BF16)) + _dot(xb, wu_ref[half:, :].astype(BF16))
        act = (g * _sigmoid(g) * u).astype(BF16)
        y = _dot(act, wd_ref[...].astype(BF16))
        ys_ref[0:rows, :] = _pack_bf16_pair(y[:, :half], y[:, half:])

    used = i < n_used
    short = bv_ref[i] <= EXPERT_TAIL

    @pl.when(used & jnp.logical_not(short))
    def _():
        ffn(EXPERT_BLOCK)

    @pl.when(used & short)
    def _():
        ffn(EXPERT_TAIL)
        ys_ref[EXPERT_TAIL:, :] = jnp.zeros((EXPERT_BLOCK - EXPERT_TAIL, ys_ref.shape[1]), ys_ref.dtype)

    @pl.when(jnp.logical_not(used))
    def _():
        ys_ref[...] = jnp.zeros_like(ys_ref)


def _experts(xs, blk_e, n_used, blk_valid, w_gate, w_up, w_down):
    n_rows, dw = xs.shape
    d = w_gate.shape[1]
    nblk = n_rows // EXPERT_BLOCK
    blk_run = jnp.cumsum(jnp.concatenate([jnp.zeros((1,), I32), (blk_e[1:] != blk_e[:-1]).astype(I32)])).astype(I32)
    hbm = pl.BlockSpec(memory_space=pl.ANY)
    return pl.pallas_call(
        _experts_kernel,
        grid_spec=pltpu.PrefetchScalarGridSpec(
            num_scalar_prefetch=4,
            grid=(nblk,),
            in_specs=[hbm, hbm, hbm, hbm],
            out_specs=pl.BlockSpec((EXPERT_BLOCK, dw), lambda i, be, nu, bv, rn: (i, 0)),
            scratch_shapes=[pltpu.VMEM((EXPERT_RING, EXPERT_BLOCK, dw), xs.dtype),
                            pltpu.SemaphoreType.DMA((EXPERT_RING,)),
                            pltpu.VMEM((EXPERT_RING, d, EXPERT_FF), w_gate.dtype),
                            pltpu.VMEM((EXPERT_RING, d, EXPERT_FF), w_up.dtype),
                            pltpu.VMEM((EXPERT_RING, EXPERT_FF, d), w_down.dtype),
                            pltpu.SemaphoreType.DMA((3, EXPERT_RING))]),
        out_shape=jax.ShapeDtypeStruct((n_rows, dw), xs.dtype),
        compiler_params=pltpu.CompilerParams(dimension_semantics=("arbitrary",), vmem_limit_bytes=VMEM_LIMIT,
                                             has_side_effects=True),
        name="experts",
    )(blk_e, n_used, blk_valid, blk_run, xs, w_gate, w_up, w_down)


def _combine_kernel(x1_ref, h_ref, w_ref, gt_ref, sg_ref, su_ref, sd_ref, yg_ref, o_ref):
    tc = x1_ref.shape[0]
    half = D_MODEL // 2
    hb = h_ref[...].astype(BF16)
    g = _dot(hb, sg_ref[...])
    u = _dot(hb, su_ref[...])
    ffn = _dot((g * _sigmoid(g) * u).astype(BF16), sd_ref[...])

    w = w_ref[...]
    ra = jnp.zeros((tc, half), F32)
    rb = jnp.zeros((tc, half), F32)
    for k in range(TOP_K):
        ya, yb = _unpack_bf16_pair(yg_ref[k])
        ra = ra + w[:, k:k + 1] * ya
        rb = rb + w[:, k:k + 1] * yb
    ffn = ffn + jnp.concatenate([ra, rb], axis=1)
    o_ref[...] = x1_ref[...] + gt_ref[0] * ffn


def _combine_alias_kernel(prev_ref, *refs):
    del prev_ref
    _combine_kernel(*refs)


def _combine(x1, h2, w_tok, gt2, yg, sg, su, sd, seq, tc, blk0, prev):
    t, d = x1.shape
    row = lambda i: (i + blk0, 0)
    fixed = lambda i: (0, 0)
    in_specs = [pl.BlockSpec((tc, d), row),
                pl.BlockSpec((tc, d), row),
                pl.BlockSpec((tc, TOP_K), row),
                pl.BlockSpec((1, 1, d), lambda i: (((i + blk0) * tc) // seq, 0, 0)),
                pl.BlockSpec((d, SHARED_FF), fixed),
                pl.BlockSpec((d, SHARED_FF), fixed),
                pl.BlockSpec((SHARED_FF, d), fixed),
                pl.BlockSpec((TOP_K, tc, d // 2), lambda i: (0, i, 0))]
    args = (x1, h2, w_tok, gt2, sg.astype(BF16), su.astype(BF16), sd.astype(BF16), yg)
    if prev is not None:
        in_specs = [pl.BlockSpec(memory_space=pl.ANY)] + in_specs
        args = (prev,) + args
    return pl.pallas_call(
        _combine_kernel if prev is None else _combine_alias_kernel,
        grid=(yg.shape[1] // tc,),
        in_specs=in_specs,
        out_specs=pl.BlockSpec((tc, d), row),
        out_shape=jax.ShapeDtypeStruct((t, d), F32),
        input_output_aliases={} if prev is None else {0: 0},
        compiler_params=_cparams("parallel"),
        name="combine",
    )(*args)


def _moe_parts(x1, h2, h2p, gt2, router_w, router_bias, w_gate, w_up, w_down, sg, su, sd):
    b, s, d = x1.shape
    t = b * s
    assert t % (SC_CHUNK * SC_CORES * SC_SUBCORES * MOE_COMBINE_PARTS) == 0, \
        "token chunks must split evenly over the vector subcores"
    h2 = h2.reshape(t, d)
    idx, w, rank, cnt = _router(h2, router_w, router_bias, min(256, t))
    counts = cnt[:, 0].astype(I32)
    padded = (counts + EXPERT_BLOCK - 1) // EXPERT_BLOCK * EXPERT_BLOCK
    pad_end = jnp.cumsum(padded)
    pad_start = pad_end - padded
    n_rows = t * TOP_K + N_EXPERTS * EXPERT_BLOCK
    nblk = n_rows // EXPERT_BLOCK
    n_used = (pad_end[-1:] // EXPERT_BLOCK).astype(I32)
    blk_start = jnp.arange(nblk, dtype=I32) * EXPERT_BLOCK
    owns = (pad_start[None, :] <= blk_start[:, None]) & (blk_start[:, None] < pad_end[None, :])
    e_ids = jnp.arange(N_EXPERTS, dtype=I32)[None, :]
    last_e = jnp.max(jnp.where(counts > 0, e_ids[0], 0))
    blk_e = jnp.where(blk_start < pad_end[-1], jnp.sum(jnp.where(owns, e_ids, 0), axis=1), last_e).astype(I32)
    rows_left = jnp.sum(jnp.where(owns, (pad_start + counts)[None, :] - blk_start[:, None], 0), axis=1)
    blk_valid = jnp.clip(rows_left, 0, EXPERT_BLOCK).astype(I32)
    slot = _slots(pad_start.astype(I32), idx, rank, min(2048, t))
    slot_chunks = slot.reshape(TOP_K, t // SC_CHUNK, SC_CHUNK).transpose(1, 0, 2)
    xs = _sc_dispatch(h2p.reshape(t, d // 2), slot_chunks, n_rows)
    ys = _experts(xs, blk_e, n_used, blk_valid, w_gate, w_up, w_down)
    tc = min(512, t // MOE_COMBINE_PARTS)
    part = t // MOE_COMBINE_PARTS
    out = None
    for pi in range(MOE_COMBINE_PARTS):
        chunks = slot_chunks[pi * (part // SC_CHUNK):(pi + 1) * (part // SC_CHUNK)]
        yg = _sc_gather(ys, chunks, part)
        out = _combine(x1.reshape(t, d), h2, w.T, gt2, yg, sg, su, sd, s, tc, pi * (part // tc), out)
    return out.reshape(b, s, d), dict(idx=idx, w=w, rank=rank, cnt=cnt)


def kernel(x, c, ada_w, ada_b, norm1_w, norm2_w, w_in, q_norm_w, k_norm_w, cmp_pos, cmp_w1, cmp_b1, cmp_w2, attn_out_norm_w, hgrn_lb_param, rec_out_norm_w, w_out, router_w, router_bias, exp_w_gate, exp_w_up, exp_w_down, shared_w_gate, shared_w_up, shared_w_down):
    assert ada_w.shape[0] == 1, "one layer"
    assert x.shape[0] <= 8 and x.shape[1] % TK == 0 and x.shape[1] >= WINDOW + TQ
    l = 0
    x1, h2, h2p, gt2 = _mixer(x, c, ada_w[l], ada_b[l], norm1_w[l], norm2_w[l], w_in[l], q_norm_w[l], k_norm_w[l],
                         cmp_pos[l], cmp_w1[l], cmp_b1[l], cmp_w2[l], attn_out_norm_w[l], hgrn_lb_param,
                         rec_out_norm_w[l], w_out[l])
    out, _ = _moe_parts(x1, h2, h2p, gt2, router_w[l], router_bias[l], exp_w_gate[l], exp_w_up[l], exp_w_down[l],
                        shared_w_gate[l], shared_w_up[l], shared_w_down[l])
    return out
```

```python
import functools

import numpy as np
import jax
import jax.numpy as jnp
from jax import lax
from jax.experimental import pallas as pl
from jax.experimental.pallas import tpu as pltpu
from jax.experimental.pallas import tpu_sc as plsc

F32 = jnp.float32
BF16 = jnp.bfloat16
I32 = jnp.int32

D_MODEL = 1024
NSA_HEADS = 8
HEAD_DIM = 64
NSA_WIDTH = NSA_HEADS * HEAD_DIM
KV_HEADS = 2
HEADS_PER_KV = NSA_HEADS // KV_HEADS
KV_WIDTH = KV_HEADS * HEAD_DIM
CMP_BLOCK = 32
CMP_STRIDE = 16
CMP_HIDDEN = 256
SEL_BLOCK = 64
N_SELECT = 16
WINDOW = 512
HGRN_HEADS = 4
HGRN_DIM = 128
HGRN_WIDTH = HGRN_HEADS * HGRN_DIM
HGRN_CHUNK = 64
HGRN_SUB = 16
N_EXPERTS = 256
TOP_K = 8
N_GROUPS = 8
GROUP_SIZE = N_EXPERTS // N_GROUPS
TOPK_GROUPS = 4
EXPERT_FF = 256
SHARED_FF = 256
ROUTED_SCALE = 2.5
RMS_EPS = 1e-6
BIG = 1e9
LOG2E = 1.4426950408889634
GATE_PAD = 128
PROJ_COLS = NSA_WIDTH + 6 * KV_WIDTH + GATE_PAD + 4 * HGRN_WIDTH

VMEM_LIMIT = 56 * 1024 * 1024

TQ = 256
TK = 256
NSA_SIZE_VARIANTS = 4
EXPERT_BLOCK = 512
EXPERT_TAIL = 128
EXPERT_RING = 3
HIGHEST = lax.Precision.HIGHEST


def _cparams(*sem):
    return pltpu.CompilerParams(dimension_semantics=sem, vmem_limit_bytes=VMEM_LIMIT)


def _sigmoid(x):
    return 1.0 / (1.0 + jnp.exp(-x))


def _dot_nt(a, b):
    return lax.dot_general(a, b, (((1,), (1,)), ((), ())), preferred_element_type=F32)


def _dot(a, b, **kw):
    return jnp.dot(a, b, preferred_element_type=F32, **kw)


def _split_dot(a_bf16_exact, x):
    hi = x.astype(BF16)
    lo = (x - hi.astype(F32)).astype(BF16)
    return _dot(a_bf16_exact, hi) + _dot(a_bf16_exact, lo)


def _mod_kernel(c_ref, w_ref, b_ref, o_ref):
    c = c_ref[...]
    cond = c * _sigmoid(c)
    o_ref[...] = _dot(cond, w_ref[...], precision=HIGHEST) + b_ref[...]


def _mod(c, ada_w, ada_b):
    b, d = c.shape
    rows = 8
    c_pad = jnp.zeros((rows, d), F32).at[:b].set(c)
    n = ada_w.shape[1]
    out = pl.pallas_call(
        _mod_kernel,
        grid=(n // d,),
        in_specs=[pl.BlockSpec((rows, d), lambda j: (0, 0)),
                  pl.BlockSpec((d, d), lambda j: (0, j)),
                  pl.BlockSpec((1, d), lambda j: (0, j))],
        out_specs=pl.BlockSpec((rows, d), lambda j: (0, j)),
        out_shape=jax.ShapeDtypeStruct((rows, n), F32),
        compiler_params=_cparams("parallel"),
        name="mod",
    )(c_pad, ada_w, ada_b.reshape(1, n))
    return out[:b]


def _head_rms(t, w):
    return t * lax.rsqrt(jnp.mean(t * t, axis=-1, keepdims=True) + RMS_EPS) * w


def _pos_digits(pos):
    lane = lax.broadcasted_iota(I32, pos.shape, 1)
    d0 = (lane == 0) | (lane == 3) | (lane == 6)
    d1 = (lane == 1) | (lane == 4) | (lane == 7)
    d2 = (lane == 2) | (lane == 5) | (lane == 8)
    dig = jnp.where(d0, pos >> 12, jnp.where(d1, (pos >> 6) & 63, jnp.where(d2, pos & 63, 0)))
    return dig.astype(F32)


def _inproj_kernel(x_ref, sc_ref, sh_ref, n1_ref, w_ref, qnw_ref, knw_ref, lbp_ref, qaug_ref,
                   q_ref, kcr_ref, vcr_ref, ks_ref, vst_ref, kw_ref, vwt_ref, gt_ref,
                   hq_ref, hk_ref, hlf_ref, hv_ref, hg_ref):
    x = x_ref[0]
    ms = jnp.mean(x * x, axis=-1, keepdims=True)
    h = x * lax.rsqrt(ms + RMS_EPS) * n1_ref[...] * (1.0 + sc_ref[0]) + sh_ref[0]
    p = _dot(h.astype(BF16), w_ref[...])
    tm = x.shape[0]

    qnw = qnw_ref[...]
    for hd in range(NSA_HEADS):
        t = p[:, hd * HEAD_DIM:(hd + 1) * HEAD_DIM]
        qn = _head_rms(t, qnw) * (HEAD_DIM ** -0.5 * LOG2E)
        qa = jnp.broadcast_to(qaug_ref[hd:hd + 1, :], (tm, HEAD_DIM))
        q_ref[0, hd] = jnp.concatenate([qn, qa], axis=1).astype(BF16)
    kaug = _pos_digits(pl.program_id(1) * tm + lax.broadcasted_iota(I32, (tm, HEAD_DIM), 0))

    o = NSA_WIDTH
    kcr_ref[0] = p[:, o:o + KV_WIDTH]
    vcr_ref[0] = p[:, o + KV_WIDTH:o + 2 * KV_WIDTH]
    ks = p[:, o + 2 * KV_WIDTH:o + 3 * KV_WIDTH]
    vs = p[:, o + 3 * KV_WIDTH:o + 4 * KV_WIDTH]
    kw = p[:, o + 4 * KV_WIDTH:o + 5 * KV_WIDTH]
    vw = p[:, o + 5 * KV_WIDTH:o + 6 * KV_WIDTH]
    for g in range(KV_HEADS):
        sl = slice(g * HEAD_DIM, (g + 1) * HEAD_DIM)
        ks_ref[0, g] = jnp.concatenate([_head_rms(ks[:, sl], knw_ref[1:2, :]), kaug], axis=1).astype(BF16)
        kw_ref[0, g] = jnp.concatenate([_head_rms(kw[:, sl], knw_ref[2:3, :]), kaug], axis=1).astype(BF16)
    vst = vs.T.astype(BF16)
    vwt = vw.T.astype(BF16)
    for g in range(KV_HEADS):
        vst_ref[0, g] = vst[g * HEAD_DIM:(g + 1) * HEAD_DIM, :]
        vwt_ref[0, g] = vwt[g * HEAD_DIM:(g + 1) * HEAD_DIM, :]

    o = NSA_WIDTH + 6 * KV_WIDTH
    gates = _sigmoid(p[:, o:o + GATE_PAD])
    gt_ref[0] = gates.T[:NSA_HEADS * 3, :]

    o = o + GATE_PAD
    hq = p[:, o:o + HGRN_WIDTH]
    hf = p[:, o + HGRN_WIDTH:o + 2 * HGRN_WIDTH]
    hi = p[:, o + 2 * HGRN_WIDTH:o + 3 * HGRN_WIDTH]
    hg = p[:, o + 3 * HGRN_WIDTH:o + 4 * HGRN_WIDTH]
    lbp = lbp_ref[...]
    e = jnp.exp(lbp - jnp.max(lbp, axis=0, keepdims=True))
    lb = e[0:1, :] / jnp.sum(e, axis=0, keepdims=True)
    f = lb + (1.0 - lb) * _sigmoid(hf)
    hq_ref[0] = hq * _sigmoid(hq) * (HGRN_DIM ** -0.5)
    hk_ref[0] = 1.0 - f
    hlf_ref[0] = jnp.log(f)
    hv_ref[0] = hi
    hg_ref[0] = _sigmoid(hg)


def _inproj(x, sc1, sh1, norm1_w, w_cat, q_norm_w, k_norm_w, lb_param, tm):
    b, s, d = x.shape
    row = lambda bi, i: (bi, i, 0)
    per_b = lambda bi, i: (bi, 0, 0)
    fixed2 = lambda bi, i: (0, 0)
    aw = 2 * HEAD_DIM
    rest = np.array([2.0 ** (-8.0 * (i + 1) / NSA_HEADS) for i in range(NSA_HEADS)], np.float64) * LOG2E
    qaug = np.zeros((NSA_HEADS, HEAD_DIM), np.float32)
    for i in range(3):
        term = rest.astype(np.float32).astype(BF16).astype(np.float64)
        rest = rest - term
        for dgt, wgt in enumerate((4096.0, 64.0, 1.0)):
            qaug[:, 3 * i + dgt] = term * wgt
    assert np.all(qaug == qaug.astype(BF16).astype(np.float32))
    out_shape = (
        jax.ShapeDtypeStruct((b, NSA_HEADS, s, aw), BF16),
        jax.ShapeDtypeStruct((b, s, KV_WIDTH), F32),
        jax.ShapeDtypeStruct((b, s, KV_WIDTH), F32),
        jax.ShapeDtypeStruct((b, KV_HEADS, s, aw), BF16),
        jax.ShapeDtypeStruct((b, KV_HEADS, HEAD_DIM, s), BF16),
        jax.ShapeDtypeStruct((b, KV_HEADS, s, aw), BF16),
        jax.ShapeDtypeStruct((b, KV_HEADS, HEAD_DIM, s), BF16),
        jax.ShapeDtypeStruct((b, NSA_HEADS * 3, s), F32),
    ) + tuple(jax.ShapeDtypeStruct((b, s, HGRN_WIDTH), F32) for _ in range(5))
    hm = lambda n, w: pl.BlockSpec((1, n, tm, w), lambda bi, i: (bi, 0, i, 0))
    hmt = lambda n, w: pl.BlockSpec((1, n, w, tm), lambda bi, i: (bi, 0, 0, i))
    out_specs = (
        hm(NSA_HEADS, aw),
        pl.BlockSpec((1, tm, KV_WIDTH), row),
        pl.BlockSpec((1, tm, KV_WIDTH), row),
        hm(KV_HEADS, aw), hmt(KV_HEADS, HEAD_DIM),
        hm(KV_HEADS, aw), hmt(KV_HEADS, HEAD_DIM),
        pl.BlockSpec((1, NSA_HEADS * 3, tm), lambda bi, i: (bi, 0, i)),
    ) + tuple(pl.BlockSpec((1, tm, HGRN_WIDTH), row) for _ in range(5))
    return pl.pallas_call(
        _inproj_kernel,
        grid=(b, s // tm),
        in_specs=[pl.BlockSpec((1, tm, d), row),
                  pl.BlockSpec((1, 1, d), per_b),
                  pl.BlockSpec((1, 1, d), per_b),
                  pl.BlockSpec((1, d), fixed2),
                  pl.BlockSpec((d, PROJ_COLS), fixed2),
                  pl.BlockSpec((1, HEAD_DIM), fixed2),
                  pl.BlockSpec((3, HEAD_DIM), fixed2),
                  pl.BlockSpec(lb_param.shape, fixed2),
                  pl.BlockSpec((NSA_HEADS, HEAD_DIM), fixed2)],
        out_specs=out_specs,
        out_shape=out_shape,
        compiler_params=_cparams("parallel", "parallel"),
        name="inproj",
    )(x, sc1, sh1, norm1_w, w_cat, q_norm_w, k_norm_w, lb_param, jnp.asarray(qaug))


def _gelu_tanh(x):
    return 0.5 * x * (1.0 + jnp.tanh(0.7978845608028654 * (x + 0.044715 * x * x * x)))


def _compress_kernel(kch_ref, vch_ref, pos_ref, wa_ref, wb_ref, b1_ref, w2_ref, knw_ref,
                     kc_ref, vct_ref):
    n = kch_ref.shape[1]
    outs = []
    for br, ch_ref in enumerate((kch_ref, vch_ref)):
        ch = ch_ref[0]
        a = _dot((ch + pos_ref[br, 0:1, :]).astype(BF16), wa_ref[br])
        bm = _dot((ch + pos_ref[br, 1:2, :]).astype(BF16), wb_ref[br])
        pre = a + pltpu.roll(bm, n - 1, 0) + b1_ref[br]
        hid = _gelu_tanh(pre).astype(BF16)
        outs.append([_dot(hid[:, g * CMP_HIDDEN:(g + 1) * CMP_HIDDEN], w2_ref[br]) for g in range(KV_HEADS)])
    end_digits = _pos_digits(lax.broadcasted_iota(I32, (n, HEAD_DIM), 0) * CMP_STRIDE + (CMP_BLOCK - 1))
    for g in range(KV_HEADS):
        kc_ref[0, g] = jnp.concatenate([_head_rms(outs[0][g], knw_ref[0:1, :]), end_digits], axis=1).astype(BF16)
    vct = jnp.concatenate(outs[1], axis=1).T.astype(BF16)
    for g in range(KV_HEADS):
        vct_ref[0, g] = vct[g * HEAD_DIM:(g + 1) * HEAD_DIM, :]


def _compress(kc_raw, vc_raw, cmp_pos, cmp_w1, cmp_b1, cmp_w2, k_norm_w):
    b, s, _ = kc_raw.shape
    n = s // CMP_STRIDE
    half = CMP_STRIDE
    cw = CMP_STRIDE * KV_WIDTH
    kch = kc_raw.reshape(b, n, cw)
    vch = vc_raw.reshape(b, n, cw)
    pos = cmp_pos.reshape(2, 2, half, 1, HEAD_DIM)
    pos = jnp.broadcast_to(pos, (2, 2, half, KV_HEADS, HEAD_DIM)).reshape(2, 2, cw)
    w1 = cmp_w1.reshape(2, 2, half, HEAD_DIM, CMP_HIDDEN)
    eye = jnp.eye(KV_HEADS, dtype=F32)
    wfull = jnp.einsum('rhjdn,gk->rhjgdkn', w1, eye).reshape(2, 2, cw, KV_HEADS * CMP_HIDDEN).astype(BF16)
    b1 = jnp.tile(cmp_b1.reshape(2, 1, CMP_HIDDEN), (1, 1, KV_HEADS))
    fix = lambda r: (lambda bi: (0,) * r)
    return pl.pallas_call(
        _compress_kernel,
        grid=(b,),
        in_specs=[pl.BlockSpec((1, n, cw), lambda bi: (bi, 0, 0)),
                  pl.BlockSpec((1, n, cw), lambda bi: (bi, 0, 0)),
                  pl.BlockSpec((2, 2, cw), fix(3)),
                  pl.BlockSpec((2, cw, KV_HEADS * CMP_HIDDEN), fix(3)),
                  pl.BlockSpec((2, cw, KV_HEADS * CMP_HIDDEN), fix(3)),
                  pl.BlockSpec((2, 1, KV_HEADS * CMP_HIDDEN), fix(3)),
                  pl.BlockSpec((2, CMP_HIDDEN, HEAD_DIM), fix(3)),
                  pl.BlockSpec((3, HEAD_DIM), fix(2))],
        out_specs=(pl.BlockSpec((1, KV_HEADS, n, 2 * HEAD_DIM), lambda bi: (bi, 0, 0, 0)),
                   pl.BlockSpec((1, KV_HEADS, HEAD_DIM, n), lambda bi: (bi, 0, 0, 0))),
        out_shape=(jax.ShapeDtypeStruct((b, KV_HEADS, n, 2 * HEAD_DIM), BF16),
                   jax.ShapeDtypeStruct((b, KV_HEADS, HEAD_DIM, n), BF16)),
        compiler_params=_cparams("parallel"),
        name="compress",
    )(kch, vch, pos, wfull[:, 0], wfull[:, 1], b1, cmp_w2.astype(BF16), k_norm_w)


def _nsa_kernel(q_ref, kc_ref, vct_ref, ks_ref, vst_ref, kw_ref, vwt_ref, gt_ref, cdiff_ref, wdiff_ref,
                ovl_ref, oh_ref, onw_ref, wmask_ref, o_ref, buf_a, buf_b, m_scr, acc_scr, oc_scr, bias_scr,
                lst, cnt, *, n_top):
    q0 = pl.program_id(2) * TQ
    ncols = HEADS_PER_KV * TQ
    q = q_ref[0].reshape(ncols, 2 * HEAD_DIM)
    ns = ovl_ref.shape[0]

    def compress_and_select(nk, nb):
        s = jnp.where(cdiff_ref[0:nk, :] <= q0, _dot_nt(kc_ref[0, 0, 0:nk, :], q), -jnp.inf)
        m = jnp.max(s, axis=0, keepdims=True)
        m = jnp.where(m == -jnp.inf, 0.0, m)
        e = jnp.exp2(s - m)
        p = e / jnp.maximum(jnp.sum(e, axis=0, keepdims=True), 1e-30)
        oc_scr[...] = _dot(vct_ref[0, 0, :, 0:nk], p.astype(BF16))

        psum = p[:, 0:TQ]
        for hh in range(1, HEADS_PER_KV):
            psum = psum + p[:, hh * TQ:(hh + 1) * TQ]
        imp = _split_dot(ovl_ref[0:nb, 0:nk], psum)
        blk = lax.broadcasted_iota(I32, (nb, TQ), 0)
        tq = q0 + lax.broadcasted_iota(I32, (nb, TQ), 1)
        cur = tq >> 6
        forced = (blk == 0) | (blk == cur) | (blk == cur - 1)
        rank = jnp.where(forced, BIG, jnp.where(blk * SEL_BLOCK <= tq, imp, -BIG))
        blkf = blk.astype(F32)

        bias = jnp.full((nb, TQ), -1e30, F32)
        for _ in range(min(n_top, nb)):
            mx = jnp.max(rank, axis=0, keepdims=True)
            first = jnp.min(jnp.where(rank == mx, blkf, float(nb)), axis=0, keepdims=True)
            hit = blkf == first
            rank = jnp.where(hit, -jnp.inf, rank)
            bias = jnp.where(hit, 0.0, bias)
        bias_scr[...] = jnp.full((128, TQ), -1e30, F32)
        bias_scr[0:nb, :] = jnp.where(blk == 0, -1e30, bias)

    nc = kc_ref.shape[2]
    quarter = (q0 + TQ - 1) // (ks_ref.shape[2] // NSA_SIZE_VARIANTS)
    for v in range(NSA_SIZE_VARIANTS):
        @pl.when(quarter == v)
        def _():
            compress_and_select(nc * (v + 1) // NSA_SIZE_VARIANTS, ns * (v + 1) // NSA_SIZE_VARIANTS)

    o_c = oc_scr[...]
    bias = bias_scr[...]

    bias_t = bias.T.astype(BF16)
    qq = jnp.concatenate([q, jnp.concatenate([bias_t] * HEADS_PER_KV, axis=0)], axis=1)
    ones_rows = jnp.ones((16, TK), BF16)

    def scores(j):
        k0 = pl.multiple_of(j * TK, TK)
        kk = jnp.concatenate([ks_ref[0, 0, pl.ds(k0, TK), :], oh_ref[pl.ds(k0, TK), :]], axis=1)
        return _dot_nt(kk, qq)

    def consume(buf, j, causal, part):
        sc = buf[...]
        if causal:
            sc = jnp.where(wdiff_ref[0:TK, :] + (q0 - j * TK) >= 0, sc, -1e30)
        k0 = pl.multiple_of(j * TK, TK)
        m_run = m_scr[part]
        m_new = jnp.maximum(m_run, jnp.max(sc, axis=0, keepdims=True))
        ex = jnp.exp2(sc - m_new).astype(BF16)
        va = jnp.concatenate([vst_ref[0, 0, :, pl.ds(k0, TK)], ones_rows], axis=0)
        acc_scr[part] = jnp.exp2(m_run - m_new) * acc_scr[part] + _dot(va, ex)
        m_scr[part] = m_new

    n_past = q0 // TK
    blocks_per_tile = TK // SEL_BLOCK
    cnt[0] = 0
    for j in range(ks_ref.shape[2] // TK):
        wanted = jnp.max(bias[j * blocks_per_tile:(j + 1) * blocks_per_tile, :]) == 0.0

        @pl.when(wanted & (j < n_past))
        def _():
            lst[cnt[0]] = j
            cnt[0] = cnt[0] + 1

    n_sel = cnt[0]
    lst[n_sel] = n_past

    buf_a[...] = scores(lst[0])

    s0 = jnp.where(wdiff_ref[0:SEL_BLOCK, :] + q0 >= 0, _dot_nt(ks_ref[0, 0, 0:SEL_BLOCK, :], q), -1e30)
    m0 = jnp.max(s0, axis=0, keepdims=True)
    v0 = jnp.concatenate([vst_ref[0, 0, :, 0:SEL_BLOCK], jnp.ones((16, SEL_BLOCK), BF16)], axis=0)
    m_scr[0] = m0
    acc_scr[0] = _dot(v0, jnp.exp2(s0 - m0).astype(BF16))
    m_scr[1] = jnp.full((1, ncols), -1e30, F32)
    acc_scr[1] = jnp.zeros((HEAD_DIM + 16, ncols), F32)

    nw = WINDOW + TQ
    start = pl.multiple_of(jnp.maximum(q0 - WINDOW, 0), TQ)
    sw = _dot_nt(kw_ref[0, 0, pl.ds(start, nw), :], q) + wmask_ref[0]
    ew = jnp.exp2(sw - jnp.max(sw, axis=0, keepdims=True))
    vw_aug = jnp.concatenate([vwt_ref[0, 0, :, pl.ds(start, nw)], jnp.ones((16, nw), BF16)], axis=0)
    acc_w = _dot(vw_aug, ew.astype(BF16))
    o_w = acc_w[0:HEAD_DIM, :] / acc_w[HEAD_DIM:HEAD_DIM + 1, :]

    def tiles(first, count):
        for u in range(0, count, 2):
            buf_b[...] = scores(lst[first + u + 1])
            consume(buf_a, lst[first + u], False, 0)
            buf_a[...] = scores(lst[first + u + 2])
            consume(buf_b, lst[first + u + 1], False, 1)
        return 0

    lax.fori_loop(0, n_sel // 4, lambda i, _: tiles(4 * i, 4), 0)
    lax.fori_loop(0, (n_sel // 2) % 2, lambda i, _: tiles((n_sel // 4) * 4, 2), 0)

    @pl.when(n_sel % 2 == 1)
    def _():
        buf_b[...] = scores(n_past)
        consume(buf_a, lst[n_sel - 1], False, 0)
        consume(buf_b, n_past, True, 1)

    @pl.when(n_sel % 2 == 0)
    def _():
        consume(buf_a, n_past, True, 0)

    m_all = jnp.maximum(m_scr[0], m_scr[1])
    acc_s = jnp.exp2(m_scr[0] - m_all) * acc_scr[0] + jnp.exp2(m_scr[1] - m_all) * acc_scr[1]
    o_s = acc_s[0:HEAD_DIM, :] / acc_s[HEAD_DIM:HEAD_DIM + 1, :]

    gt = gt_ref[0, 0]
    outs = []
    for hh in range(HEADS_PER_KV):
        cs = slice(hh * TQ, (hh + 1) * TQ)
        o = (gt[3 * hh:3 * hh + 1, :] * o_c[:, cs] + gt[3 * hh + 1:3 * hh + 2, :] * o_s[:, cs]
             + gt[3 * hh + 2:3 * hh + 3, :] * o_w[:, cs])
        o = o * lax.rsqrt(jnp.mean(o * o, axis=0, keepdims=True) + RMS_EPS) * onw_ref[0, hh]
        outs.append(o)
    o_ref[0] = jnp.concatenate(outs, axis=0).T


def _nsa(q, kc, vct, ks, vst, kw, vwt, gates_t, attn_out_norm_w):
    b, _, s, aw = q.shape
    nc = kc.shape[2]
    ns = s // SEL_BLOCK
    n_top = min(N_SELECT, ns)
    ncols = HEADS_PER_KV * TQ
    nw = WINDOW + TQ
    tl = np.arange(ncols)[None, :] & (TQ - 1)
    cdiff = jnp.asarray((np.arange(nc)[:, None] * CMP_STRIDE + (CMP_BLOCK - 1) - tl).astype(np.int32))
    wdiff_np = (tl - np.arange(nw)[:, None]).astype(np.int32)
    wdiff = jnp.asarray(wdiff_np)
    n_off = WINDOW // TQ + 1
    dist_np = wdiff_np[None] + (np.arange(n_off) * TQ)[:, None, None]
    wmask = jnp.asarray(np.where((dist_np >= 0) & (dist_np < WINDOW), 0.0, -np.inf).astype(np.float32))
    ci = np.arange(nc)[None, :] * CMP_STRIDE
    bj = np.arange(ns)[:, None]
    ovl = ((ci < (bj + 1) * SEL_BLOCK) & (ci + CMP_BLOCK > bj * SEL_BLOCK) & (np.arange(nc)[None, :] < nc - 1))
    ovl = jnp.asarray(ovl.astype(np.float32)).astype(BF16)
    assert ns <= 128
    onehot = (np.arange(s)[:, None] // SEL_BLOCK == np.arange(128)[None, :])
    onehot = jnp.asarray(onehot.astype(np.float32)).astype(BF16)
    onw = jnp.broadcast_to(attn_out_norm_w.reshape(KV_HEADS, HEADS_PER_KV, HEAD_DIM, 1),
                           (KV_HEADS, HEADS_PER_KV, HEAD_DIM, TQ))
    gt = gates_t.reshape(b, KV_HEADS, HEADS_PER_KV * 3, s)
    per_bg = lambda bi, g, i: (bi, g, 0, 0)
    fixed = lambda bi, g, i: (0, 0)
    return pl.pallas_call(
        functools.partial(_nsa_kernel, n_top=n_top),
        grid=(b, KV_HEADS, s // TQ),
        in_specs=[pl.BlockSpec((1, HEADS_PER_KV, TQ, aw), lambda bi, g, i: (bi, g, i, 0)),
                  pl.BlockSpec((1, 1, nc, aw), per_bg),
                  pl.BlockSpec((1, 1, HEAD_DIM, nc), per_bg),
                  pl.BlockSpec((1, 1, s, aw), per_bg),
                  pl.BlockSpec((1, 1, HEAD_DIM, s), per_bg),
                  pl.BlockSpec((1, 1, s, aw), per_bg),
                  pl.BlockSpec((1, 1, HEAD_DIM, s), per_bg),
                  pl.BlockSpec((1, 1, HEADS_PER_KV * 3, TQ), lambda bi, g, i: (bi, g, 0, i)),
                  pl.BlockSpec((nc, ncols), fixed, pipeline_mode=pl.Buffered(1)),
                  pl.BlockSpec((nw, ncols), fixed, pipeline_mode=pl.Buffered(1)),
                  pl.BlockSpec((ns, nc), fixed, pipeline_mode=pl.Buffered(1)),
                  pl.BlockSpec((s, 128), fixed, pipeline_mode=pl.Buffered(1)),
                  pl.BlockSpec((1, HEADS_PER_KV, HEAD_DIM, TQ), lambda bi, g, i: (g, 0, 0, 0)),
                  pl.BlockSpec((1, nw, ncols), lambda bi, g, i: (jnp.minimum(i, n_off - 1), 0, 0))],
        out_specs=pl.BlockSpec((1, TQ, HEADS_PER_KV * HEAD_DIM), lambda bi, g, i: (bi, i, g)),
        out_shape=jax.ShapeDtypeStruct((b, s, NSA_WIDTH), F32),
        scratch_shapes=[pltpu.VMEM((TK, ncols), F32), pltpu.VMEM((TK, ncols), F32),
                        pltpu.VMEM((2, 1, ncols), F32), pltpu.VMEM((2, HEAD_DIM + 16, ncols), F32),
                        pltpu.VMEM((HEAD_DIM, ncols), F32), pltpu.VMEM((128, TQ), F32),
                        pltpu.SMEM((s // TK + 1,), I32), pltpu.SMEM((1,), I32)],
        compiler_params=_cparams("parallel", "parallel", "arbitrary"),
        name="nsa",
    )(q, kc, vct, ks, vst, kw, vwt, gt, cdiff, wdiff, ovl, onehot, onw, wmask)


def _hgrn_kernel(q_ref, k_ref, lf_ref, v_ref, g_ref, onw_ref, cm_ref, o_ref, state_scr, *, n_chunks):
    c = HGRN_CHUNK

    @pl.when(pl.program_id(1) == 0)
    def _():
        state_scr[...] = jnp.zeros_like(state_scr)

    ri = lax.broadcasted_iota(I32, (c, c), 0)
    ci = lax.broadcasted_iota(I32, (c, c), 1)
    rsub = ri // HGRN_SUB
    diag = ri == ci

    def head_chunk(r0, hd, state_t):
        cols = slice(hd * HGRN_DIM, (hd + 1) * HGRN_DIM)
        q = q_ref[0, pl.ds(r0, c), cols]
        k = k_ref[0, pl.ds(r0, c), cols]
        lf = lf_ref[0, pl.ds(r0, c), cols] * LOG2E
        v = v_ref[0, pl.ds(r0, c), cols]
        cm = cm_ref[...]
        l1 = lf.astype(BF16)
        rest = lf - l1.astype(F32)
        l2 = rest.astype(BF16)
        l3 = (rest - l2.astype(F32)).astype(BF16)
        cum = _dot(cm, l1) + _dot(cm, l2) + _dot(cm, l3)
        o = _dot_nt((q * jnp.exp2(cum)).astype(BF16), state_t.astype(BF16))
        scores = jnp.where(diag, jnp.sum(q * k, axis=-1, keepdims=True), 0.0)

        def factored(ref, mask, acc):
            qs = q * jnp.exp2(jnp.minimum(cum - ref, 0.0))
            kd = k * jnp.exp2(jnp.minimum(ref - cum, 0.0))
            return jnp.where(mask, _dot_nt(qs.astype(BF16), kd.astype(BF16)), acc)

        for i in range(1, c // HGRN_SUB):
            scores = factored(cum[i * HGRN_SUB - 1:i * HGRN_SUB, :], (rsub == i) & (ci < i * HGRN_SUB), scores)
        for d in range(1, HGRN_SUB):
            ksh = pltpu.roll(k, d, 0)
            csh = pltpu.roll(cum, d, 0)
            w = jnp.sum(q * ksh * jnp.exp2(cum - csh), axis=-1, keepdims=True)
            scores = jnp.where((ri - ci == d) & ((ri & (HGRN_SUB - 1)) >= d), w, scores)
        o = o + _dot(scores.astype(BF16), v.astype(BF16))
        last = cum[c - 1:c, :]
        kd = (k * jnp.exp2(last - cum)).astype(BF16)
        state_t = state_t * jnp.exp2(last) + _dot(v.T.astype(BF16), kd)
        o = o * g_ref[0, pl.ds(r0, c), cols]
        o = o * lax.rsqrt(jnp.mean(o * o, axis=-1, keepdims=True) + RMS_EPS) * onw_ref[:, cols]
        o_ref[0, pl.ds(r0, c), cols] = o
        return state_t

    def chunk(ck, states):
        r0 = pl.multiple_of(ck * c, c)
        return tuple(head_chunk(r0, hd, states[hd]) for hd in range(HGRN_HEADS))

    states = lax.fori_loop(0, n_chunks, chunk, tuple(state_scr[hd] for hd in range(HGRN_HEADS)))
    for hd in range(HGRN_HEADS):
        state_scr[hd] = states[hd]


def _hgrn(hq, hk, hlf, hv, hg, rec_out_norm_w, rows):
    b, s, _ = hq.shape
    cm = jnp.asarray(np.tril(np.ones((HGRN_CHUNK, HGRN_CHUNK), np.float32))).astype(BF16)
    blk = pl.BlockSpec((1, rows, HGRN_WIDTH), lambda bi, i: (bi, i, 0))
    return pl.pallas_call(
        functools.partial(_hgrn_kernel, n_chunks=rows // HGRN_CHUNK),
        grid=(b, s // rows),
        in_specs=[blk, blk, blk, blk, blk,
                  pl.BlockSpec((1, HGRN_WIDTH), lambda bi, i: (0, 0)),
                  pl.BlockSpec(cm.shape, lambda bi, i: (0, 0))],
        out_specs=blk,
        out_shape=jax.ShapeDtypeStruct((b, s, HGRN_WIDTH), F32),
        scratch_shapes=[pltpu.VMEM((HGRN_HEADS, HGRN_DIM, HGRN_DIM), F32)],
        compiler_params=_cparams("parallel", "arbitrary"),
        name="hgrn",
    )(hq, hk, hlf, hv, hg, rec_out_norm_w.reshape(1, HGRN_WIDTH), cm)


def _outproj_kernel(x_ref, a_ref, r_ref, wa_ref, wr_ref, gt_ref, sc_ref, sh_ref, n2_ref, x1_ref, h2_ref, h2p_ref):
    mixed = _dot(a_ref[0].astype(BF16), wa_ref[...]) + _dot(r_ref[0].astype(BF16), wr_ref[...])
    x1 = x_ref[0] + gt_ref[0] * mixed
    x1_ref[0] = x1
    ms = jnp.mean(x1 * x1, axis=-1, keepdims=True)
    h2 = x1 * lax.rsqrt(ms + RMS_EPS) * n2_ref[...] * (1.0 + sc_ref[0]) + sh_ref[0]
    h2_ref[0] = h2
    h2p_ref[0] = _pack_bf16_pair(h2[:, :D_MODEL // 2], h2[:, D_MODEL // 2:])


def _outproj(x, attn, rec, w_out, gt1, sc2, sh2, norm2_w, tm):
    b, s, d = x.shape
    row = lambda bi, i: (bi, i, 0)
    per_b = lambda bi, i: (bi, 0, 0)
    fixed2 = lambda bi, i: (0, 0)
    w = w_out.astype(BF16)
    return pl.pallas_call(
        _outproj_kernel,
        grid=(b, s // tm),
        in_specs=[pl.BlockSpec((1, tm, d), row),
                  pl.BlockSpec((1, tm, NSA_WIDTH), row),
                  pl.BlockSpec((1, tm, HGRN_WIDTH), row),
                  pl.BlockSpec((NSA_WIDTH, d), fixed2),
                  pl.BlockSpec((HGRN_WIDTH, d), fixed2),
                  pl.BlockSpec((1, 1, d), per_b),
                  pl.BlockSpec((1, 1, d), per_b),
                  pl.BlockSpec((1, 1, d), per_b),
                  pl.BlockSpec((1, d), fixed2)],
        out_specs=(pl.BlockSpec((1, tm, d), row), pl.BlockSpec((1, tm, d), row), pl.BlockSpec((1, tm, d // 2), row)),
        out_shape=(jax.ShapeDtypeStruct((b, s, d), F32), jax.ShapeDtypeStruct((b, s, d), F32),
                   jax.ShapeDtypeStruct((b, s, d // 2), jnp.uint32)),
        compiler_params=_cparams("parallel", "parallel"),
        name="outproj",
    )(x, attn, rec, w[:NSA_WIDTH], w[NSA_WIDTH:], gt1, sc2, sh2, norm2_w)


def _mixer(x, c, ada_w, ada_b, norm1_w, norm2_w, w_in, q_norm_w, k_norm_w, cmp_pos, cmp_w1, cmp_b1, cmp_w2,
           attn_out_norm_w, hgrn_lb_param, rec_out_norm_w, w_out):
    b, s, d = x.shape
    mod = _mod(c, ada_w, ada_b)
    sh1, sc1, gt1, sh2, sc2, gt2 = [m.reshape(b, 1, d) for m in jnp.split(mod, 6, axis=-1)]
    o = NSA_WIDTH + 6 * KV_WIDTH
    w_cat = jnp.concatenate([w_in[:, :o], w_in[:, o:o + NSA_HEADS * 3],
                             jnp.zeros((d, GATE_PAD - NSA_HEADS * 3), w_in.dtype),
                             w_in[:, o + NSA_HEADS * 3:]], axis=1).astype(BF16)
    tm = min(512, s)
    (q, kc_raw, vc_raw, ks, vst, kw, vwt, gates_t, hq, hk, hlf, hv, hg) = _inproj(
        x, sc1, sh1, norm1_w.reshape(1, d), w_cat, q_norm_w.reshape(1, HEAD_DIM), k_norm_w, hgrn_lb_param, tm)
    kc, vct = _compress(kc_raw, vc_raw, cmp_pos, cmp_w1, cmp_b1, cmp_w2, k_norm_w)
    attn = _nsa(q, kc, vct, ks, vst, kw, vwt, gates_t, attn_out_norm_w)
    rec = _hgrn(hq, hk, hlf, hv, hg, rec_out_norm_w, min(1024, s))
    x1, h2, h2p = _outproj(x, attn, rec, w_out, gt1, sc2, sh2, norm2_w.reshape(1, d), min(1024, s))
    return x1, h2, h2p, gt2


def _router_kernel(h_ref, rwt_ref, bias_ref, tri_ref, ones_ref, idx_ref, w_ref, rank_ref, cnt_ref, carry_scr, *, tr):
    @pl.when(pl.program_id(0) == 0)
    def _():
        carry_scr[...] = jnp.zeros_like(carry_scr)

    h = h_ref[...]
    h_hi = h.astype(BF16)
    h_lo = (h - h_hi.astype(F32)).astype(BF16)
    logits = _dot_nt(rwt_ref[0], h_hi) + _dot_nt(rwt_ref[1], h_hi) + _dot_nt(rwt_ref[0], h_lo)
    scores = _sigmoid(logits)
    biased = scores + bias_ref[...]
    neg = -jnp.inf

    gs = []
    for g in range(N_GROUPS):
        sub = biased[g * GROUP_SIZE:(g + 1) * GROUP_SIZE, :]
        m1 = jnp.max(sub, axis=0, keepdims=True)
        dup = jnp.sum((sub == m1).astype(F32), axis=0, keepdims=True)
        m2 = jnp.max(jnp.where(sub < m1, sub, neg), axis=0, keepdims=True)
        gs.append(m1 + jnp.where(dup >= 2.0, m1, m2))
    parts = []
    for g in range(N_GROUPS):
        beaten = jnp.zeros_like(gs[g])
        for g2 in range(N_GROUPS):
            if g2 != g:
                beats = (gs[g2] >= gs[g]) if g2 < g else (gs[g2] > gs[g])
                beaten = beaten + beats.astype(F32)
        sub = biased[g * GROUP_SIZE:(g + 1) * GROUP_SIZE, :]
        parts.append(jnp.where(beaten < float(TOPK_GROUPS), sub, neg))
    cand = jnp.concatenate(parts, axis=0)

    rowf = lax.broadcasted_iota(I32, (N_EXPERTS, tr), 0).astype(F32)
    idx_rows, w_rows, hits = [], [], []
    multi = jnp.zeros((N_EXPERTS, tr), F32)
    for _ in range(TOP_K):
        mx = jnp.max(cand, axis=0, keepdims=True)
        first = jnp.min(jnp.where(cand == mx, rowf, float(N_EXPERTS)), axis=0, keepdims=True)
        hit = rowf == first
        idx_rows.append(first)
        w_rows.append(jnp.sum(jnp.where(hit, scores, 0.0), axis=0, keepdims=True))
        cand = jnp.where(hit, neg, cand)
        multi = jnp.where(hit, 1.0, multi)
    w = jnp.concatenate(w_rows, axis=0)
    w_ref[...] = w / jnp.sum(w, axis=0, keepdims=True) * ROUTED_SCALE
    idx = jnp.concatenate(idx_rows, axis=0)
    idx_ref[...] = idx.astype(I32)

    carry = carry_scr[...]
    mb = multi.astype(BF16)
    before = _dot(mb, tri_ref[...]) + jnp.concatenate([carry] * (tr // 128), axis=1)
    rank_rows = [jnp.sum(jnp.where(rowf == idx_rows[k], before, 0.0), axis=0, keepdims=True) for k in range(TOP_K)]
    rank_ref[...] = jnp.concatenate(rank_rows, axis=0).astype(I32)
    carry = carry + _dot(mb, ones_ref[...])
    carry_scr[...] = carry
    cnt_ref[...] = carry


def _router(h2, router_w, router_bias, tr):
    t, d = h2.shape
    tri = jnp.asarray(np.triu(np.ones((tr, tr), np.float32), 1)).astype(BF16)
    ones = jnp.ones((tr, 128), BF16)
    tok = pl.BlockSpec((TOP_K, tr), lambda i: (0, i))
    fixed = lambda i: (0, 0)
    rwt = router_w.T
    rwt_hi = rwt.astype(BF16)
    rwt_split = jnp.stack([rwt_hi, (rwt - rwt_hi.astype(F32)).astype(BF16)])
    return pl.pallas_call(
        functools.partial(_router_kernel, tr=tr),
        grid=(t // tr,),
        in_specs=[pl.BlockSpec((tr, d), lambda i: (i, 0)),
                  pl.BlockSpec((2, N_EXPERTS, d), lambda i: (0, 0, 0)),
                  pl.BlockSpec((N_EXPERTS, 1), fixed),
                  pl.BlockSpec((tr, tr), fixed),
                  pl.BlockSpec((tr, 128), fixed)],
        out_specs=(tok, tok, tok, pl.BlockSpec((N_EXPERTS, 128), fixed)),
        out_shape=(jax.ShapeDtypeStruct((TOP_K, t), I32), jax.ShapeDtypeStruct((TOP_K, t), F32),
                   jax.ShapeDtypeStruct((TOP_K, t), I32), jax.ShapeDtypeStruct((N_EXPERTS, 128), F32)),
        scratch_shapes=[pltpu.VMEM((N_EXPERTS, 128), F32)],
        compiler_params=_cparams("arbitrary"),
        name="router",
    )(h2, rwt_split, router_bias.reshape(N_EXPERTS, 1), tri, ones)


def _pack_bf16_pair(a, b):
    ua = lax.bitcast_convert_type(a.astype(BF16).astype(F32), jnp.uint32)
    ub = lax.bitcast_convert_type(b.astype(BF16).astype(F32), jnp.uint32)
    return ua | (ub >> 16)


def _unpack_bf16_pair(w):
    a = lax.bitcast_convert_type(w & jnp.uint32(0xFFFF0000), F32)
    b = lax.bitcast_convert_type(w << 16, F32)
    return a, b


def _slot_kernel(ps_ref, idx_ref, rank_ref, slot_ref):
    idx = idx_ref[...]

    def body(e, acc):
        return jnp.where(idx == e, ps_ref[e], acc)

    slot_ref[...] = lax.fori_loop(0, N_EXPERTS, body, jnp.zeros_like(idx)) + rank_ref[...]


def _slots(pad_start, idx, rank, tt):
    t = idx.shape[1]
    tok = pl.BlockSpec((TOP_K, tt), lambda i, ps: (0, i))
    return pl.pallas_call(
        _slot_kernel,
        grid_spec=pltpu.PrefetchScalarGridSpec(num_scalar_prefetch=1, grid=(t // tt,),
                                               in_specs=[tok, tok], out_specs=tok),
        out_shape=jax.ShapeDtypeStruct((TOP_K, t), I32),
        compiler_params=_cparams("parallel"),
        name="slots",
    )(pad_start, idx, rank)


SC_CORES = 2
SC_SUBCORES = 16
SC_CHUNK = 64


def _sc_mesh():
    return plsc.VectorSubcoreMesh(core_axis_name="c", subcore_axis_name="s")


def _sc_dispatch(h2p, slot_chunks, n_rows):
    t, dw = h2p.shape
    per = slot_chunks.shape[0] // (SC_CORES * SC_SUBCORES)

    def body(h_hbm, slot_hbm, xs_hbm, idx_v, rows_v, sem):
        wid = lax.axis_index("s") * SC_CORES + lax.axis_index("c")

        @pl.loop(0, per)
        def _(c):
            ch = wid * per + c
            pltpu.sync_copy(slot_hbm.at[ch], idx_v)
            pltpu.sync_copy(h_hbm.at[pl.ds(ch * SC_CHUNK, SC_CHUNK)], rows_v)
            copies = [pltpu.async_copy(rows_v, xs_hbm.at[idx_v.at[k]], sem) for k in range(TOP_K)]
            for cp in copies:
                cp.wait()

    return pl.kernel(
        body, out_type=jax.ShapeDtypeStruct((n_rows, dw), h2p.dtype), mesh=_sc_mesh(),
        scratch_types=[pltpu.VMEM((TOP_K, SC_CHUNK), I32), pltpu.VMEM((SC_CHUNK, dw), h2p.dtype),
                       pltpu.SemaphoreType.DMA],
    )(h2p, slot_chunks)


def _sc_gather(ys, slot_chunks, t):
    dw = ys.shape[1]
    per = slot_chunks.shape[0] // (SC_CORES * SC_SUBCORES)

    def body(ys_hbm, slot_hbm, yg_hbm, idx_v, rows_v, gsem, wsem):
        wid = lax.axis_index("s") * SC_CORES + lax.axis_index("c")

        @pl.loop(0, per)
        def _(c):
            ch = wid * per + c
            pltpu.sync_copy(slot_hbm.at[ch], idx_v)
            gathers = [None] * TOP_K
            writes = [None] * TOP_K
            gathers[0] = pltpu.async_copy(ys_hbm.at[idx_v.at[0]], rows_v.at[0], gsem)
            for k in range(TOP_K):
                gathers[k].wait()
                if k + 1 < TOP_K:
                    if k >= 1:
                        writes[k - 1].wait()
                    gathers[k + 1] = pltpu.async_copy(ys_hbm.at[idx_v.at[k + 1]], rows_v.at[(k + 1) % 2], gsem)
                writes[k] = pltpu.async_copy(rows_v.at[k % 2], yg_hbm.at[k, pl.ds(ch * SC_CHUNK, SC_CHUNK)], wsem)
            writes[TOP_K - 2].wait()
            writes[TOP_K - 1].wait()

    return pl.kernel(
        body, out_type=jax.ShapeDtypeStruct((TOP_K, t, dw), ys.dtype), mesh=_sc_mesh(),
        scratch_types=[pltpu.VMEM((TOP_K, SC_CHUNK), I32), pltpu.VMEM((2, SC_CHUNK, dw), ys.dtype),
                       pltpu.SemaphoreType.DMA, pltpu.SemaphoreType.DMA],
    )(ys, slot_chunks)


def _experts_kernel(be_ref, nu_ref, bv_ref, run_ref, xs_hbm, wg_hbm, wu_hbm, wd_hbm, ys_ref,
                    xring, rsem, gring, uring, dring, wsem):
    i = pl.program_id(0)
    half = D_MODEL // 2
    n_used = nu_ref[0]
    n_steps = pl.num_programs(0)

    def weight_copies(blk):
        ex = be_ref[blk]
        slot = run_ref[blk] % EXPERT_RING
        return [pltpu.make_async_copy(src.at[ex], ring.at[slot], wsem.at[a, slot])
                for a, (src, ring) in enumerate(((wg_hbm, gring), (wu_hbm, uring), (wd_hbm, dring)))]

    def starts_run(blk):
        return run_ref[blk] != run_ref[jnp.maximum(blk - 1, 0)]

    @pl.when(i == 0)
    def _():
        for cp in weight_copies(jnp.int32(0)):
            cp.start()

        @pl.when((n_steps > 1) & starts_run(jnp.int32(1)))
        def _():
            for cp in weight_copies(jnp.int32(1)):
                cp.start()

    ahead = jnp.minimum(i + (EXPERT_RING - 1), n_steps - 1)

    @pl.when((i + (EXPERT_RING - 1) < n_steps) & starts_run(ahead))
    def _():
        for cp in weight_copies(ahead):
            cp.start()

    @pl.when((i == 0) | starts_run(i))
    def _():
        for cp in weight_copies(i):
            cp.wait()

    wslot = run_ref[i] % EXPERT_RING
    wg_ref, wu_ref, wd_ref = gring.at[wslot], uring.at[wslot], dring.at[wslot]

    def fetch(blk):
        slot = blk % EXPERT_RING
        return pltpu.make_async_copy(xs_hbm.at[pl.ds(pl.multiple_of(blk * EXPERT_BLOCK, EXPERT_BLOCK), EXPERT_BLOCK)],
                                     xring.at[slot], rsem.at[slot])

    @pl.when(i == 0)
    def _():
        for first in range(EXPERT_RING - 1):
            @pl.when(first < n_used)
            def _():
                fetch(jnp.int32(first)).start()

    @pl.when(i + (EXPERT_RING - 1) < n_used)
    def _():
        fetch(i + (EXPERT_RING - 1)).start()

    @pl.when(i < n_used)
    def _():
        fetch(i).wait()

    xs_ref = xring.at[i % EXPERT_RING]

    def ffn(rows):
        live = lax.broadcasted_iota(I32, (rows, xs_ref.shape[1]), 0) < bv_ref[i]
        xa, xb = _unpack_bf16_pair(jnp.where(live, xs_ref[0:rows, :], jnp.uint32(0)))
        xa, xb = xa.astype(BF16), xb.astype(BF16)
        g = _dot(xa, wg_ref[:half, :].astype(BF16)) + _dot(xb, wg_ref[half:, :].astype(BF16))
        u = _dot(xa, wu_ref[:half, :].astype(BF16)) + _dot(xb, wu_ref[half:, :].astype(BF16))
        act = (g * _sigmoid(g) * u).astype(BF16)
        y = _dot(act, wd_ref[...].astype(BF16))
        ys_ref[0:rows, :] = _pack_bf16_pair(y[:, :half], y[:, half:])

    used = i < n_used
    short = bv_ref[i] <= EXPERT_TAIL

    @pl.when(used & jnp.logical_not(short))
    def _():
        ffn(EXPERT_BLOCK)

    @pl.when(used & short)
    def _():
        ffn(EXPERT_TAIL)
        ys_ref[EXPERT_TAIL:, :] = jnp.zeros((EXPERT_BLOCK - EXPERT_TAIL, ys_ref.shape[1]), ys_ref.dtype)

    @pl.when(jnp.logical_not(used))
    def _():
        ys_ref[...] = jnp.zeros_like(ys_ref)


def _experts(xs, blk_e, n_used, blk_valid, w_gate, w_up, w_down):
    n_rows, dw = xs.shape
    d = w_gate.shape[1]
    nblk = n_rows // EXPERT_BLOCK
    blk_run = jnp.cumsum(jnp.concatenate([jnp.zeros((1,), I32), (blk_e[1:] != blk_e[:-1]).astype(I32)])).astype(I32)
    hbm = pl.BlockSpec(memory_space=pl.ANY)
    return pl.pallas_call(
        _experts_kernel,
        grid_spec=pltpu.PrefetchScalarGridSpec(
            num_scalar_prefetch=4,
            grid=(nblk,),
            in_specs=[hbm, hbm, hbm, hbm],
            out_specs=pl.BlockSpec((EXPERT_BLOCK, dw), lambda i, be, nu, bv, rn: (i, 0)),
            scratch_shapes=[pltpu.VMEM((EXPERT_RING, EXPERT_BLOCK, dw), xs.dtype),
                            pltpu.SemaphoreType.DMA((EXPERT_RING,)),
                            pltpu.VMEM((EXPERT_RING, d, EXPERT_FF), w_gate.dtype),
                            pltpu.VMEM((EXPERT_RING, d, EXPERT_FF), w_up.dtype),
                            pltpu.VMEM((EXPERT_RING, EXPERT_FF, d), w_down.dtype),
                            pltpu.SemaphoreType.DMA((3, EXPERT_RING))]),
        out_shape=jax.ShapeDtypeStruct((n_rows, dw), xs.dtype),
        compiler_params=pltpu.CompilerParams(dimension_semantics=("arbitrary",), vmem_limit_bytes=VMEM_LIMIT,
                                             has_side_effects=True),
        name="experts",
    )(blk_e, n_used, blk_valid, blk_run, xs, w_gate, w_up, w_down)


def _combine_kernel(x1_ref, h_ref, w_ref, gt_ref, sg_ref, su_ref, sd_ref, yg_ref, o_ref):
    tc = x1_ref.shape[0]
    half = D_MODEL // 2
    ha, hb = _unpack_bf16_pair(h_ref[...])
    ha, hb = ha.astype(BF16), hb.astype(BF16)
    g = _dot(ha, sg_ref[:half, :]) + _dot(hb, sg_ref[half:, :])
    u = _dot(ha, su_ref[:half, :]) + _dot(hb, su_ref[half:, :])
    ffn = _dot((g * _sigmoid(g) * u).astype(BF16), sd_ref[...])

    w = w_ref[...]
    ra = jnp.zeros((tc, half), F32)
    rb = jnp.zeros((tc, half), F32)
    for k in range(TOP_K):
        ya, yb = _unpack_bf16_pair(yg_ref[k])
        ra = ra + w[:, k:k + 1] * ya
        rb = rb + w[:, k:k + 1] * yb
    ffn = ffn + jnp.concatenate([ra, rb], axis=1)
    o_ref[...] = x1_ref[...] + gt_ref[0] * ffn


def _combine(x1, h2p, w_tok, gt2, yg, sg, su, sd, seq, tc):
    t, d = x1.shape
    row = lambda i: (i, 0)
    fixed = lambda i: (0, 0)
    return pl.pallas_call(
        _combine_kernel,
        grid=(t // tc,),
        in_specs=[pl.BlockSpec((tc, d), row),
                  pl.BlockSpec((tc, d // 2), row),
                  pl.BlockSpec((tc, TOP_K), row),
                  pl.BlockSpec((1, 1, d), lambda i: ((i * tc) // seq, 0, 0)),
                  pl.BlockSpec((d, SHARED_FF), fixed),
                  pl.BlockSpec((d, SHARED_FF), fixed),
                  pl.BlockSpec((SHARED_FF, d), fixed),
                  pl.BlockSpec((TOP_K, tc, d // 2), lambda i: (0, i, 0))],
        out_specs=pl.BlockSpec((tc, d), row),
        out_shape=jax.ShapeDtypeStruct((t, d), F32),
        compiler_params=_cparams("parallel"),
        name="combine",
    )(x1, h2p, w_tok, gt2, sg.astype(BF16), su.astype(BF16), sd.astype(BF16), yg)


def _moe_parts(x1, h2, h2p, gt2, router_w, router_bias, w_gate, w_up, w_down, sg, su, sd):
    b, s, d = x1.shape
    t = b * s
    assert t % (SC_CHUNK * SC_CORES * SC_SUBCORES) == 0, "token chunks must split evenly over the vector subcores"
    h2 = h2.reshape(t, d)
    idx, w, rank, cnt = _router(h2, router_w, router_bias, min(256, t))
    counts = cnt[:, 0].astype(I32)
    padded = (counts + EXPERT_BLOCK - 1) // EXPERT_BLOCK * EXPERT_BLOCK
    pad_end = jnp.cumsum(padded)
    pad_start = pad_end - padded
    n_rows = t * TOP_K + N_EXPERTS * EXPERT_BLOCK
    nblk = n_rows // EXPERT_BLOCK
    n_used = (pad_end[-1:] // EXPERT_BLOCK).astype(I32)
    blk_start = jnp.arange(nblk, dtype=I32) * EXPERT_BLOCK
    owns = (pad_start[None, :] <= blk_start[:, None]) & (blk_start[:, None] < pad_end[None, :])
    e_ids = jnp.arange(N_EXPERTS, dtype=I32)[None, :]
    last_e = jnp.max(jnp.where(counts > 0, e_ids[0], 0))
    blk_e = jnp.where(blk_start < pad_end[-1], jnp.sum(jnp.where(owns, e_ids, 0), axis=1), last_e).astype(I32)
    rows_left = jnp.sum(jnp.where(owns, (pad_start + counts)[None, :] - blk_start[:, None], 0), axis=1)
    blk_valid = jnp.clip(rows_left, 0, EXPERT_BLOCK).astype(I32)
    slot = _slots(pad_start.astype(I32), idx, rank, min(2048, t))
    slot_chunks = slot.reshape(TOP_K, t // SC_CHUNK, SC_CHUNK).transpose(1, 0, 2)
    xs = _sc_dispatch(h2p.reshape(t, d // 2), slot_chunks, n_rows)
    ys = _experts(xs, blk_e, n_used, blk_valid, w_gate, w_up, w_down)
    yg = _sc_gather(ys, slot_chunks, t)
    out = _combine(x1.reshape(t, d), h2p.reshape(t, d // 2), w.T, gt2, yg, sg, su, sd, s, min(512, t))
    return out.reshape(b, s, d), dict(idx=idx, w=w, rank=rank, cnt=cnt)


def kernel(x, c, ada_w, ada_b, norm1_w, norm2_w, w_in, q_norm_w, k_norm_w, cmp_pos, cmp_w1, cmp_b1, cmp_w2, attn_out_norm_w, hgrn_lb_param, rec_out_norm_w, w_out, router_w, router_bias, exp_w_gate, exp_w_up, exp_w_down, shared_w_gate, shared_w_up, shared_w_down):
    assert ada_w.shape[0] == 1, "one layer"
    assert x.shape[0] <= 8 and x.shape[1] % TK == 0 and x.shape[1] >= WINDOW + TQ
    l = 0
    x1, h2, h2p, gt2 = _mixer(x, c, ada_w[l], ada_b[l], norm1_w[l], norm2_w[l], w_in[l], q_norm_w[l], k_norm_w[l],
                         cmp_pos[l], cmp_w1[l], cmp_b1[l], cmp_w2[l], attn_out_norm_w[l], hgrn_lb_param,
                         rec_out_norm_w[l], w_out[l])
    out, _ = _moe_parts(x1, h2, h2p, gt2, router_w[l], router_bias[l], exp_w_gate[l], exp_w_up[l], exp_w_down[l],
                        shared_w_gate[l], shared_w_up[l], shared_w_down[l])
    return out
```

```python
import functools

import numpy as np
import jax
import jax.numpy as jnp
from jax import lax
from jax.experimental import pallas as pl
from jax.experimental.pallas import tpu as pltpu
from jax.experimental.pallas import tpu_sc as plsc

F32 = jnp.float32
BF16 = jnp.bfloat16
I32 = jnp.int32

D_MODEL = 1024
NSA_HEADS = 8
HEAD_DIM = 64
NSA_WIDTH = NSA_HEADS * HEAD_DIM
KV_HEADS = 2
HEADS_PER_KV = NSA_HEADS // KV_HEADS
KV_WIDTH = KV_HEADS * HEAD_DIM
CMP_BLOCK = 32
CMP_STRIDE = 16
CMP_HIDDEN = 256
SEL_BLOCK = 64
N_SELECT = 16
WINDOW = 512
HGRN_HEADS = 4
HGRN_DIM = 128
HGRN_WIDTH = HGRN_HEADS * HGRN_DIM
HGRN_CHUNK = 64
HGRN_SUB = 16
N_EXPERTS = 256
TOP_K = 8
N_GROUPS = 8
GROUP_SIZE = N_EXPERTS // N_GROUPS
TOPK_GROUPS = 4
EXPERT_FF = 256
SHARED_FF = 256
ROUTED_SCALE = 2.5
RMS_EPS = 1e-6
BIG = 1e9
LOG2E = 1.4426950408889634
GATE_PAD = 128
PROJ_COLS = NSA_WIDTH + 6 * KV_WIDTH + GATE_PAD + 4 * HGRN_WIDTH

VMEM_LIMIT = 56 * 1024 * 1024

TQ = 256
TK = 256
NSA_SIZE_VARIANTS = 4
EXPERT_BLOCK = 512
EXPERT_TAIL = 128
EXPERT_RING = 3
HIGHEST = lax.Precision.HIGHEST


def _cparams(*sem):
    return pltpu.CompilerParams(dimension_semantics=sem, vmem_limit_bytes=VMEM_LIMIT)


def _sigmoid(x):
    return 1.0 / (1.0 + jnp.exp(-x))


def _dot_nt(a, b):
    return lax.dot_general(a, b, (((1,), (1,)), ((), ())), preferred_element_type=F32)


def _dot(a, b, **kw):
    return jnp.dot(a, b, preferred_element_type=F32, **kw)


def _split_dot(a_bf16_exact, x):
    hi = x.astype(BF16)
    lo = (x - hi.astype(F32)).astype(BF16)
    return _dot(a_bf16_exact, hi) + _dot(a_bf16_exact, lo)


def _mod_kernel(c_ref, w_ref, b_ref, o_ref):
    c = c_ref[...]
    cond = c * _sigmoid(c)
    o_ref[...] = _dot(cond, w_ref[...], precision=HIGHEST) + b_ref[...]


def _mod(c, ada_w, ada_b):
    b, d = c.shape
    rows = 8
    c_pad = jnp.zeros((rows, d), F32).at[:b].set(c)
    n = ada_w.shape[1]
    out = pl.pallas_call(
        _mod_kernel,
        grid=(n // d,),
        in_specs=[pl.BlockSpec((rows, d), lambda j: (0, 0)),
                  pl.BlockSpec((d, d), lambda j: (0, j)),
                  pl.BlockSpec((1, d), lambda j: (0, j))],
        out_specs=pl.BlockSpec((rows, d), lambda j: (0, j)),
        out_shape=jax.ShapeDtypeStruct((rows, n), F32),
        compiler_params=_cparams("parallel"),
        name="mod",
    )(c_pad, ada_w, ada_b.reshape(1, n))
    return out[:b]


def _head_rms(t, w):
    return t * lax.rsqrt(jnp.mean(t * t, axis=-1, keepdims=True) + RMS_EPS) * w


def _pos_digits(pos):
    lane = lax.broadcasted_iota(I32, pos.shape, 1)
    d0 = (lane == 0) | (lane == 3) | (lane == 6)
    d1 = (lane == 1) | (lane == 4) | (lane == 7)
    d2 = (lane == 2) | (lane == 5) | (lane == 8)
    dig = jnp.where(d0, pos >> 12, jnp.where(d1, (pos >> 6) & 63, jnp.where(d2, pos & 63, 0)))
    return dig.astype(F32)


def _inproj_kernel(x_ref, sc_ref, sh_ref, n1_ref, w_ref, qnw_ref, knw_ref, lbp_ref, qaug_ref,
                   q_ref, kcr_ref, vcr_ref, ks_ref, vst_ref, kw_ref, vwt_ref, gt_ref,
                   hq_ref, hk_ref, hlf_ref, hv_ref, hg_ref):
    x = x_ref[0]
    ms = jnp.mean(x * x, axis=-1, keepdims=True)
    h = x * lax.rsqrt(ms + RMS_EPS) * n1_ref[...] * (1.0 + sc_ref[0]) + sh_ref[0]
    p = _dot(h.astype(BF16), w_ref[...])
    tm = x.shape[0]

    qnw = qnw_ref[...]
    for hd in range(NSA_HEADS):
        t = p[:, hd * HEAD_DIM:(hd + 1) * HEAD_DIM]
        qn = _head_rms(t, qnw) * (HEAD_DIM ** -0.5 * LOG2E)
        qa = jnp.broadcast_to(qaug_ref[hd:hd + 1, :], (tm, HEAD_DIM))
        q_ref[0, hd] = jnp.concatenate([qn, qa], axis=1).astype(BF16)
    kaug = _pos_digits(pl.program_id(1) * tm + lax.broadcasted_iota(I32, (tm, HEAD_DIM), 0))

    o = NSA_WIDTH
    kcr_ref[0] = p[:, o:o + KV_WIDTH]
    vcr_ref[0] = p[:, o + KV_WIDTH:o + 2 * KV_WIDTH]
    ks = p[:, o + 2 * KV_WIDTH:o + 3 * KV_WIDTH]
    vs = p[:, o + 3 * KV_WIDTH:o + 4 * KV_WIDTH]
    kw = p[:, o + 4 * KV_WIDTH:o + 5 * KV_WIDTH]
    vw = p[:, o + 5 * KV_WIDTH:o + 6 * KV_WIDTH]
    for g in range(KV_HEADS):
        sl = slice(g * HEAD_DIM, (g + 1) * HEAD_DIM)
        ks_ref[0, g] = jnp.concatenate([_head_rms(ks[:, sl], knw_ref[1:2, :]), kaug], axis=1).astype(BF16)
        kw_ref[0, g] = jnp.concatenate([_head_rms(kw[:, sl], knw_ref[2:3, :]), kaug], axis=1).astype(BF16)
    vst = vs.T.astype(BF16)
    vwt = vw.T.astype(BF16)
    for g in range(KV_HEADS):
        vst_ref[0, g] = vst[g * HEAD_DIM:(g + 1) * HEAD_DIM, :]
        vwt_ref[0, g] = vwt[g * HEAD_DIM:(g + 1) * HEAD_DIM, :]

    o = NSA_WIDTH + 6 * KV_WIDTH
    gates = _sigmoid(p[:, o:o + GATE_PAD])
    gt_ref[0] = gates.T[:NSA_HEADS * 3, :]

    o = o + GATE_PAD
    hq = p[:, o:o + HGRN_WIDTH]
    hf = p[:, o + HGRN_WIDTH:o + 2 * HGRN_WIDTH]
    hi = p[:, o + 2 * HGRN_WIDTH:o + 3 * HGRN_WIDTH]
    hg = p[:, o + 3 * HGRN_WIDTH:o + 4 * HGRN_WIDTH]
    lbp = lbp_ref[...]
    e = jnp.exp(lbp - jnp.max(lbp, axis=0, keepdims=True))
    lb = e[0:1, :] / jnp.sum(e, axis=0, keepdims=True)
    f = lb + (1.0 - lb) * _sigmoid(hf)
    hq_ref[0] = hq * _sigmoid(hq) * (HGRN_DIM ** -0.5)
    hk_ref[0] = 1.0 - f
    hlf_ref[0] = jnp.log(f)
    hv_ref[0] = hi
    hg_ref[0] = _sigmoid(hg)


def _inproj(x, sc1, sh1, norm1_w, w_cat, q_norm_w, k_norm_w, lb_param, tm):
    b, s, d = x.shape
    row = lambda bi, i: (bi, i, 0)
    per_b = lambda bi, i: (bi, 0, 0)
    fixed2 = lambda bi, i: (0, 0)
    aw = 2 * HEAD_DIM
    rest = np.array([2.0 ** (-8.0 * (i + 1) / NSA_HEADS) for i in range(NSA_HEADS)], np.float64) * LOG2E
    qaug = np.zeros((NSA_HEADS, HEAD_DIM), np.float32)
    for i in range(3):
        term = rest.astype(np.float32).astype(BF16).astype(np.float64)
        rest = rest - term
        for dgt, wgt in enumerate((4096.0, 64.0, 1.0)):
            qaug[:, 3 * i + dgt] = term * wgt
    assert np.all(qaug == qaug.astype(BF16).astype(np.float32))
    out_shape = (
        jax.ShapeDtypeStruct((b, NSA_HEADS, s, aw), BF16),
        jax.ShapeDtypeStruct((b, s, KV_WIDTH), F32),
        jax.ShapeDtypeStruct((b, s, KV_WIDTH), F32),
        jax.ShapeDtypeStruct((b, KV_HEADS, s, aw), BF16),
        jax.ShapeDtypeStruct((b, KV_HEADS, HEAD_DIM, s), BF16),
        jax.ShapeDtypeStruct((b, KV_HEADS, s, aw), BF16),
        jax.ShapeDtypeStruct((b, KV_HEADS, HEAD_DIM, s), BF16),
        jax.ShapeDtypeStruct((b, NSA_HEADS * 3, s), F32),
    ) + tuple(jax.ShapeDtypeStruct((b, s, HGRN_WIDTH), F32) for _ in range(5))
    hm = lambda n, w: pl.BlockSpec((1, n, tm, w), lambda bi, i: (bi, 0, i, 0))
    hmt = lambda n, w: pl.BlockSpec((1, n, w, tm), lambda bi, i: (bi, 0, 0, i))
    out_specs = (
        hm(NSA_HEADS, aw),
        pl.BlockSpec((1, tm, KV_WIDTH), row),
        pl.BlockSpec((1, tm, KV_WIDTH), row),
        hm(KV_HEADS, aw), hmt(KV_HEADS, HEAD_DIM),
        hm(KV_HEADS, aw), hmt(KV_HEADS, HEAD_DIM),
        pl.BlockSpec((1, NSA_HEADS * 3, tm), lambda bi, i: (bi, 0, i)),
    ) + tuple(pl.BlockSpec((1, tm, HGRN_WIDTH), row) for _ in range(5))
    return pl.pallas_call(
        _inproj_kernel,
        grid=(b, s // tm),
        in_specs=[pl.BlockSpec((1, tm, d), row),
                  pl.BlockSpec((1, 1, d), per_b),
                  pl.BlockSpec((1, 1, d), per_b),
                  pl.BlockSpec((1, d), fixed2),
                  pl.BlockSpec((d, PROJ_COLS), fixed2),
                  pl.BlockSpec((1, HEAD_DIM), fixed2),
                  pl.BlockSpec((3, HEAD_DIM), fixed2),
                  pl.BlockSpec(lb_param.shape, fixed2),
                  pl.BlockSpec((NSA_HEADS, HEAD_DIM), fixed2)],
        out_specs=out_specs,
        out_shape=out_shape,
        compiler_params=_cparams("parallel", "parallel"),
        name="inproj",
    )(x, sc1, sh1, norm1_w, w_cat, q_norm_w, k_norm_w, lb_param, jnp.asarray(qaug))


def _gelu_tanh(x):
    return 0.5 * x * (1.0 + jnp.tanh(0.7978845608028654 * (x + 0.044715 * x * x * x)))


def _compress_kernel(kch_ref, vch_ref, pos_ref, wa_ref, wb_ref, b1_ref, w2_ref, knw_ref,
                     kc_ref, vct_ref):
    n = kch_ref.shape[1]
    outs = []
    for br, ch_ref in enumerate((kch_ref, vch_ref)):
        ch = ch_ref[0]
        a = _dot((ch + pos_ref[br, 0:1, :]).astype(BF16), wa_ref[br])
        bm = _dot((ch + pos_ref[br, 1:2, :]).astype(BF16), wb_ref[br])
        pre = a + pltpu.roll(bm, n - 1, 0) + b1_ref[br]
        hid = _gelu_tanh(pre).astype(BF16)
        outs.append([_dot(hid[:, g * CMP_HIDDEN:(g + 1) * CMP_HIDDEN], w2_ref[br]) for g in range(KV_HEADS)])
    end_digits = _pos_digits(lax.broadcasted_iota(I32, (n, HEAD_DIM), 0) * CMP_STRIDE + (CMP_BLOCK - 1))
    for g in range(KV_HEADS):
        kc_ref[0, g] = jnp.concatenate([_head_rms(outs[0][g], knw_ref[0:1, :]), end_digits], axis=1).astype(BF16)
    vct = jnp.concatenate(outs[1], axis=1).T.astype(BF16)
    for g in range(KV_HEADS):
        vct_ref[0, g] = vct[g * HEAD_DIM:(g + 1) * HEAD_DIM, :]


def _compress(kc_raw, vc_raw, cmp_pos, cmp_w1, cmp_b1, cmp_w2, k_norm_w):
    b, s, _ = kc_raw.shape
    n = s // CMP_STRIDE
    half = CMP_STRIDE
    cw = CMP_STRIDE * KV_WIDTH
    kch = kc_raw.reshape(b, n, cw)
    vch = vc_raw.reshape(b, n, cw)
    pos = cmp_pos.reshape(2, 2, half, 1, HEAD_DIM)
    pos = jnp.broadcast_to(pos, (2, 2, half, KV_HEADS, HEAD_DIM)).reshape(2, 2, cw)
    w1 = cmp_w1.reshape(2, 2, half, HEAD_DIM, CMP_HIDDEN)
    eye = jnp.eye(KV_HEADS, dtype=F32)
    wfull = jnp.einsum('rhjdn,gk->rhjgdkn', w1, eye).reshape(2, 2, cw, KV_HEADS * CMP_HIDDEN).astype(BF16)
    b1 = jnp.tile(cmp_b1.reshape(2, 1, CMP_HIDDEN), (1, 1, KV_HEADS))
    fix = lambda r: (lambda bi: (0,) * r)
    return pl.pallas_call(
        _compress_kernel,
        grid=(b,),
        in_specs=[pl.BlockSpec((1, n, cw), lambda bi: (bi, 0, 0)),
                  pl.BlockSpec((1, n, cw), lambda bi: (bi, 0, 0)),
                  pl.BlockSpec((2, 2, cw), fix(3)),
                  pl.BlockSpec((2, cw, KV_HEADS * CMP_HIDDEN), fix(3)),
                  pl.BlockSpec((2, cw, KV_HEADS * CMP_HIDDEN), fix(3)),
                  pl.BlockSpec((2, 1, KV_HEADS * CMP_HIDDEN), fix(3)),
                  pl.BlockSpec((2, CMP_HIDDEN, HEAD_DIM), fix(3)),
                  pl.BlockSpec((3, HEAD_DIM), fix(2))],
        out_specs=(pl.BlockSpec((1, KV_HEADS, n, 2 * HEAD_DIM), lambda bi: (bi, 0, 0, 0)),
                   pl.BlockSpec((1, KV_HEADS, HEAD_DIM, n), lambda bi: (bi, 0, 0, 0))),
        out_shape=(jax.ShapeDtypeStruct((b, KV_HEADS, n, 2 * HEAD_DIM), BF16),
                   jax.ShapeDtypeStruct((b, KV_HEADS, HEAD_DIM, n), BF16)),
        compiler_params=_cparams("parallel"),
        name="compress",
    )(kch, vch, pos, wfull[:, 0], wfull[:, 1], b1, cmp_w2.astype(BF16), k_norm_w)


def _nsa_kernel(q_ref, kc_ref, vct_ref, ks_ref, vst_ref, kw_ref, vwt_ref, gt_ref, cdiff_ref, wdiff_ref,
                ovl_ref, oh_ref, onw_ref, wmask_ref, o_ref, buf_a, buf_b, m_scr, acc_scr, oc_scr, bias_scr,
                lst, cnt, *, n_top):
    q0 = pl.program_id(2) * TQ
    ncols = HEADS_PER_KV * TQ
    q = q_ref[0].reshape(ncols, 2 * HEAD_DIM)
    ns = ovl_ref.shape[0]

    def compress_and_select(nk, nb):
        s = jnp.where(cdiff_ref[0:nk, :] <= q0, _dot_nt(kc_ref[0, 0, 0:nk, :], q), -jnp.inf)
        m = jnp.max(s, axis=0, keepdims=True)
        m = jnp.where(m == -jnp.inf, 0.0, m)
        e = jnp.exp2(s - m)
        p = e / jnp.maximum(jnp.sum(e, axis=0, keepdims=True), 1e-30)
        oc_scr[...] = _dot(vct_ref[0, 0, :, 0:nk], p.astype(BF16))

        psum = p[:, 0:TQ]
        for hh in range(1, HEADS_PER_KV):
            psum = psum + p[:, hh * TQ:(hh + 1) * TQ]
        imp = _split_dot(ovl_ref[0:nb, 0:nk], psum)
        blk = lax.broadcasted_iota(I32, (nb, TQ), 0)
        tq = q0 + lax.broadcasted_iota(I32, (nb, TQ), 1)
        cur = tq >> 6
        forced = (blk == 0) | (blk == cur) | (blk == cur - 1)
        rank = jnp.where(forced, BIG, jnp.where(blk * SEL_BLOCK <= tq, imp, -BIG))
        blkf = blk.astype(F32)

        bias = jnp.full((nb, TQ), -1e30, F32)
        for _ in range(min(n_top, nb)):
            mx = jnp.max(rank, axis=0, keepdims=True)
            first = jnp.min(jnp.where(rank == mx, blkf, float(nb)), axis=0, keepdims=True)
            hit = blkf == first
            rank = jnp.where(hit, -jnp.inf, rank)
            bias = jnp.where(hit, 0.0, bias)
        bias_scr[...] = jnp.full((128, TQ), -1e30, F32)
        bias_scr[0:nb, :] = jnp.where(blk == 0, -1e30, bias)

    nc = kc_ref.shape[2]
    quarter = (q0 + TQ - 1) // (ks_ref.shape[2] // NSA_SIZE_VARIANTS)
    for v in range(NSA_SIZE_VARIANTS):
        @pl.when(quarter == v)
        def _():
            compress_and_select(nc * (v + 1) // NSA_SIZE_VARIANTS, ns * (v + 1) // NSA_SIZE_VARIANTS)

    o_c = oc_scr[...]
    bias = bias_scr[...]

    bias_t = bias.T.astype(BF16)
    qq = jnp.concatenate([q, jnp.concatenate([bias_t] * HEADS_PER_KV, axis=0)], axis=1)
    ones_rows = jnp.ones((16, TK), BF16)

    def scores(j):
        k0 = pl.multiple_of(j * TK, TK)
        kk = jnp.concatenate([ks_ref[0, 0, pl.ds(k0, TK), :], oh_ref[pl.ds(k0, TK), :]], axis=1)
        return _dot_nt(kk, qq)

    def consume(buf, j, causal, part):
        sc = buf[...]
        if causal:
            sc = jnp.where(wdiff_ref[0:TK, :] + (q0 - j * TK) >= 0, sc, -1e30)
        k0 = pl.multiple_of(j * TK, TK)
        m_run = m_scr[part]
        m_new = jnp.maximum(m_run, jnp.max(sc, axis=0, keepdims=True))
        ex = jnp.exp2(sc - m_new).astype(BF16)
        va = jnp.concatenate([vst_ref[0, 0, :, pl.ds(k0, TK)], ones_rows], axis=0)
        acc_scr[part] = jnp.exp2(m_run - m_new) * acc_scr[part] + _dot(va, ex)
        m_scr[part] = m_new

    n_past = q0 // TK
    blocks_per_tile = TK // SEL_BLOCK
    cnt[0] = 0
    for j in range(ks_ref.shape[2] // TK):
        wanted = jnp.max(bias[j * blocks_per_tile:(j + 1) * blocks_per_tile, :]) == 0.0

        @pl.when(wanted & (j < n_past))
        def _():
            lst[cnt[0]] = j
            cnt[0] = cnt[0] + 1

    n_sel = cnt[0]
    lst[n_sel] = n_past

    buf_a[...] = scores(lst[0])

    s0 = jnp.where(wdiff_ref[0:SEL_BLOCK, :] + q0 >= 0, _dot_nt(ks_ref[0, 0, 0:SEL_BLOCK, :], q), -1e30)
    m0 = jnp.max(s0, axis=0, keepdims=True)
    v0 = jnp.concatenate([vst_ref[0, 0, :, 0:SEL_BLOCK], jnp.ones((16, SEL_BLOCK), BF16)], axis=0)
    m_scr[0] = m0
    acc_scr[0] = _dot(v0, jnp.exp2(s0 - m0).astype(BF16))
    m_scr[1] = jnp.full((1, ncols), -1e30, F32)
    acc_scr[1] = jnp.zeros((HEAD_DIM + 16, ncols), F32)

    nw = WINDOW + TQ
    start = pl.multiple_of(jnp.maximum(q0 - WINDOW, 0), TQ)
    sw = _dot_nt(kw_ref[0, 0, pl.ds(start, nw), :], q) + wmask_ref[0]
    ew = jnp.exp2(sw - jnp.max(sw, axis=0, keepdims=True))
    vw_aug = jnp.concatenate([vwt_ref[0, 0, :, pl.ds(start, nw)], jnp.ones((16, nw), BF16)], axis=0)
    acc_w = _dot(vw_aug, ew.astype(BF16))
    o_w = acc_w[0:HEAD_DIM, :] / acc_w[HEAD_DIM:HEAD_DIM + 1, :]

    def tiles(first, count):
        for u in range(0, count, 2):
            buf_b[...] = scores(lst[first + u + 1])
            consume(buf_a, lst[first + u], False, 0)
            buf_a[...] = scores(lst[first + u + 2])
            consume(buf_b, lst[first + u + 1], False, 1)
        return 0

    lax.fori_loop(0, n_sel // 4, lambda i, _: tiles(4 * i, 4), 0)
    lax.fori_loop(0, (n_sel // 2) % 2, lambda i, _: tiles((n_sel // 4) * 4, 2), 0)

    @pl.when(n_sel % 2 == 1)
    def _():
        buf_b[...] = scores(n_past)
        consume(buf_a, lst[n_sel - 1], False, 0)
        consume(buf_b, n_past, True, 1)

    @pl.when(n_sel % 2 == 0)
    def _():
        consume(buf_a, n_past, True, 0)

    m_all = jnp.maximum(m_scr[0], m_scr[1])
    acc_s = jnp.exp2(m_scr[0] - m_all) * acc_scr[0] + jnp.exp2(m_scr[1] - m_all) * acc_scr[1]
    o_s = acc_s[0:HEAD_DIM, :] / acc_s[HEAD_DIM:HEAD_DIM + 1, :]

    gt = gt_ref[0, 0]
    outs = []
    for hh in range(HEADS_PER_KV):
        cs = slice(hh * TQ, (hh + 1) * TQ)
        o = (gt[3 * hh:3 * hh + 1, :] * o_c[:, cs] + gt[3 * hh + 1:3 * hh + 2, :] * o_s[:, cs]
             + gt[3 * hh + 2:3 * hh + 3, :] * o_w[:, cs])
        o = o * lax.rsqrt(jnp.mean(o * o, axis=0, keepdims=True) + RMS_EPS) * onw_ref[0, hh]
        outs.append(o)
    o_ref[0] = jnp.concatenate(outs, axis=0).T


def _nsa(q, kc, vct, ks, vst, kw, vwt, gates_t, attn_out_norm_w):
    b, _, s, aw = q.shape
    nc = kc.shape[2]
    ns = s // SEL_BLOCK
    n_top = min(N_SELECT, ns)
    ncols = HEADS_PER_KV * TQ
    nw = WINDOW + TQ
    tl = np.arange(ncols)[None, :] & (TQ - 1)
    cdiff = jnp.asarray((np.arange(nc)[:, None] * CMP_STRIDE + (CMP_BLOCK - 1) - tl).astype(np.int32))
    wdiff_np = (tl - np.arange(nw)[:, None]).astype(np.int32)
    wdiff = jnp.asarray(wdiff_np)
    n_off = WINDOW // TQ + 1
    dist_np = wdiff_np[None] + (np.arange(n_off) * TQ)[:, None, None]
    wmask = jnp.asarray(np.where((dist_np >= 0) & (dist_np < WINDOW), 0.0, -np.inf).astype(np.float32))
    ci = np.arange(nc)[None, :] * CMP_STRIDE
    bj = np.arange(ns)[:, None]
    ovl = ((ci < (bj + 1) * SEL_BLOCK) & (ci + CMP_BLOCK > bj * SEL_BLOCK) & (np.arange(nc)[None, :] < nc - 1))
    ovl = jnp.asarray(ovl.astype(np.float32)).astype(BF16)
    assert ns <= 128
    onehot = (np.arange(s)[:, None] // SEL_BLOCK == np.arange(128)[None, :])
    onehot = jnp.asarray(onehot.astype(np.float32)).astype(BF16)
    onw = jnp.broadcast_to(attn_out_norm_w.reshape(KV_HEADS, HEADS_PER_KV, HEAD_DIM, 1),
                           (KV_HEADS, HEADS_PER_KV, HEAD_DIM, TQ))
    gt = gates_t.reshape(b, KV_HEADS, HEADS_PER_KV * 3, s)
    per_bg = lambda bi, g, i: (bi, g, 0, 0)
    fixed = lambda bi, g, i: (0, 0)
    return pl.pallas_call(
        functools.partial(_nsa_kernel, n_top=n_top),
        grid=(b, KV_HEADS, s // TQ),
        in_specs=[pl.BlockSpec((1, HEADS_PER_KV, TQ, aw), lambda bi, g, i: (bi, g, i, 0)),
                  pl.BlockSpec((1, 1, nc, aw), per_bg),
                  pl.BlockSpec((1, 1, HEAD_DIM, nc), per_bg),
                  pl.BlockSpec((1, 1, s, aw), per_bg),
                  pl.BlockSpec((1, 1, HEAD_DIM, s), per_bg),
                  pl.BlockSpec((1, 1, s, aw), per_bg),
                  pl.BlockSpec((1, 1, HEAD_DIM, s), per_bg),
                  pl.BlockSpec((1, 1, HEADS_PER_KV * 3, TQ), lambda bi, g, i: (bi, g, 0, i)),
                  pl.BlockSpec((nc, ncols), fixed, pipeline_mode=pl.Buffered(1)),
                  pl.BlockSpec((nw, ncols), fixed, pipeline_mode=pl.Buffered(1)),
                  pl.BlockSpec((ns, nc), fixed, pipeline_mode=pl.Buffered(1)),
                  pl.BlockSpec((s, 128), fixed, pipeline_mode=pl.Buffered(1)),
                  pl.BlockSpec((1, HEADS_PER_KV, HEAD_DIM, TQ), lambda bi, g, i: (g, 0, 0, 0)),
                  pl.BlockSpec((1, nw, ncols), lambda bi, g, i: (jnp.minimum(i, n_off - 1), 0, 0))],
        out_specs=pl.BlockSpec((1, TQ, HEADS_PER_KV * HEAD_DIM), lambda bi, g, i: (bi, i, g)),
        out_shape=jax.ShapeDtypeStruct((b, s, NSA_WIDTH), F32),
        scratch_shapes=[pltpu.VMEM((TK, ncols), F32), pltpu.VMEM((TK, ncols), F32),
                        pltpu.VMEM((2, 1, ncols), F32), pltpu.VMEM((2, HEAD_DIM + 16, ncols), F32),
                        pltpu.VMEM((HEAD_DIM, ncols), F32), pltpu.VMEM((128, TQ), F32),
                        pltpu.SMEM((s // TK + 1,), I32), pltpu.SMEM((1,), I32)],
        compiler_params=_cparams("parallel", "parallel", "arbitrary"),
        name="nsa",
    )(q, kc, vct, ks, vst, kw, vwt, gt, cdiff, wdiff, ovl, onehot, onw, wmask)


def _hgrn_kernel(q_ref, k_ref, lf_ref, v_ref, g_ref, onw_ref, cm_ref, o_ref, state_scr, *, n_chunks):
    c = HGRN_CHUNK

    @pl.when(pl.program_id(1) == 0)
    def _():
        state_scr[...] = jnp.zeros_like(state_scr)

    ri = lax.broadcasted_iota(I32, (c, c), 0)
    ci = lax.broadcasted_iota(I32, (c, c), 1)
    rsub = ri // HGRN_SUB
    diag = ri == ci

    def head_chunk(r0, hd, state_t):
        cols = slice(hd * HGRN_DIM, (hd + 1) * HGRN_DIM)
        q = q_ref[0, pl.ds(r0, c), cols]
        k = k_ref[0, pl.ds(r0, c), cols]
        lf = lf_ref[0, pl.ds(r0, c), cols] * LOG2E
        v = v_ref[0, pl.ds(r0, c), cols]
        cm = cm_ref[...]
        l1 = lf.astype(BF16)
        rest = lf - l1.astype(F32)
        l2 = rest.astype(BF16)
        l3 = (rest - l2.astype(F32)).astype(BF16)
        cum = _dot(cm, l1) + _dot(cm, l2) + _dot(cm, l3)
        o = _dot_nt((q * jnp.exp2(cum)).astype(BF16), state_t.astype(BF16))
        scores = jnp.where(diag, jnp.sum(q * k, axis=-1, keepdims=True), 0.0)

        def factored(ref, mask, acc):
            qs = q * jnp.exp2(jnp.minimum(cum - ref, 0.0))
            kd = k * jnp.exp2(jnp.minimum(ref - cum, 0.0))
            return jnp.where(mask, _dot_nt(qs.astype(BF16), kd.astype(BF16)), acc)

        for i in range(1, c // HGRN_SUB):
            scores = factored(cum[i * HGRN_SUB - 1:i * HGRN_SUB, :], (rsub == i) & (ci < i * HGRN_SUB), scores)
        for d in range(1, HGRN_SUB):
            ksh = pltpu.roll(k, d, 0)
            csh = pltpu.roll(cum, d, 0)
            w = jnp.sum(q * ksh * jnp.exp2(cum - csh), axis=-1, keepdims=True)
            scores = jnp.where((ri - ci == d) & ((ri & (HGRN_SUB - 1)) >= d), w, scores)
        o = o + _dot(scores.astype(BF16), v.astype(BF16))
        last = cum[c - 1:c, :]
        kd = (k * jnp.exp2(last - cum)).astype(BF16)
        state_t = state_t * jnp.exp2(last) + _dot(v.T.astype(BF16), kd)
        o = o * g_ref[0, pl.ds(r0, c), cols]
        o = o * lax.rsqrt(jnp.mean(o * o, axis=-1, keepdims=True) + RMS_EPS) * onw_ref[:, cols]
        o_ref[0, pl.ds(r0, c), cols] = o
        return state_t

    def chunk(ck, states):
        r0 = pl.multiple_of(ck * c, c)
        return tuple(head_chunk(r0, hd, states[hd]) for hd in range(HGRN_HEADS))

    states = lax.fori_loop(0, n_chunks, chunk, tuple(state_scr[hd] for hd in range(HGRN_HEADS)))
    for hd in range(HGRN_HEADS):
        state_scr[hd] = states[hd]


def _hgrn(hq, hk, hlf, hv, hg, rec_out_norm_w, rows):
    b, s, _ = hq.shape
    cm = jnp.asarray(np.tril(np.ones((HGRN_CHUNK, HGRN_CHUNK), np.float32))).astype(BF16)
    blk = pl.BlockSpec((1, rows, HGRN_WIDTH), lambda bi, i: (bi, i, 0))
    return pl.pallas_call(
        functools.partial(_hgrn_kernel, n_chunks=rows // HGRN_CHUNK),
        grid=(b, s // rows),
        in_specs=[blk, blk, blk, blk, blk,
                  pl.BlockSpec((1, HGRN_WIDTH), lambda bi, i: (0, 0)),
                  pl.BlockSpec(cm.shape, lambda bi, i: (0, 0))],
        out_specs=blk,
        out_shape=jax.ShapeDtypeStruct((b, s, HGRN_WIDTH), F32),
        scratch_shapes=[pltpu.VMEM((HGRN_HEADS, HGRN_DIM, HGRN_DIM), F32)],
        compiler_params=_cparams("parallel", "arbitrary"),
        name="hgrn",
    )(hq, hk, hlf, hv, hg, rec_out_norm_w.reshape(1, HGRN_WIDTH), cm)


def _outproj_kernel(x_ref, a_ref, r_ref, wa_ref, wr_ref, gt_ref, sc_ref, sh_ref, n2_ref,
                    rwt_ref, bias_ref, tri_ref, ones_ref,
                    x1_ref, h2p_ref, idx_ref, w_ref, rank_ref, cnt_ref, carry_scr):
    mixed = _dot(a_ref[0].astype(BF16), wa_ref[...]) + _dot(r_ref[0].astype(BF16), wr_ref[...])
    x1 = x_ref[0] + gt_ref[0] * mixed
    x1_ref[0] = x1
    ms = jnp.mean(x1 * x1, axis=-1, keepdims=True)
    h2 = x1 * lax.rsqrt(ms + RMS_EPS) * n2_ref[...] * (1.0 + sc_ref[0]) + sh_ref[0]
    h2p_ref[0] = _pack_bf16_pair(h2[:, :D_MODEL // 2], h2[:, D_MODEL // 2:])
    first_step = (pl.program_id(0) == 0) & (pl.program_id(1) == 0)
    _route(h2, first_step, rwt_ref, bias_ref, tri_ref, ones_ref, idx_ref, w_ref, rank_ref, cnt_ref, carry_scr,
           h2.shape[0])


def _outproj(x, attn, rec, w_out, gt1, sc2, sh2, norm2_w, router_w, router_bias, tm):
    b, s, d = x.shape
    t = b * s
    row = lambda bi, i: (bi, i, 0)
    per_b = lambda bi, i: (bi, 0, 0)
    fixed2 = lambda bi, i: (0, 0)
    w = w_out.astype(BF16)
    tri = jnp.asarray(np.triu(np.ones((tm, tm), np.float32), 1)).astype(BF16)
    ones = jnp.ones((tm, 128), BF16)
    rwt = router_w.T
    rwt_hi = rwt.astype(BF16)
    rwt_split = jnp.stack([rwt_hi, (rwt - rwt_hi.astype(F32)).astype(BF16)])
    tok = pl.BlockSpec((TOP_K, tm), lambda bi, i: (0, bi * (s // tm) + i))
    return pl.pallas_call(
        _outproj_kernel,
        grid=(b, s // tm),
        in_specs=[pl.BlockSpec((1, tm, d), row),
                  pl.BlockSpec((1, tm, NSA_WIDTH), row),
                  pl.BlockSpec((1, tm, HGRN_WIDTH), row),
                  pl.BlockSpec((NSA_WIDTH, d), fixed2),
                  pl.BlockSpec((HGRN_WIDTH, d), fixed2),
                  pl.BlockSpec((1, 1, d), per_b),
                  pl.BlockSpec((1, 1, d), per_b),
                  pl.BlockSpec((1, 1, d), per_b),
                  pl.BlockSpec((1, d), fixed2),
                  pl.BlockSpec((2, N_EXPERTS, d), lambda bi, i: (0, 0, 0)),
                  pl.BlockSpec((N_EXPERTS, 1), fixed2),
                  pl.BlockSpec((tm, tm), fixed2),
                  pl.BlockSpec((tm, 128), fixed2)],
        out_specs=(pl.BlockSpec((1, tm, d), row), pl.BlockSpec((1, tm, d // 2), row),
                   tok, tok, tok, pl.BlockSpec((N_EXPERTS, 128), fixed2)),
        out_shape=(jax.ShapeDtypeStruct((b, s, d), F32), jax.ShapeDtypeStruct((b, s, d // 2), jnp.uint32),
                   jax.ShapeDtypeStruct((TOP_K, t), I32), jax.ShapeDtypeStruct((TOP_K, t), F32),
                   jax.ShapeDtypeStruct((TOP_K, t), I32), jax.ShapeDtypeStruct((N_EXPERTS, 128), F32)),
        scratch_shapes=[pltpu.VMEM((N_EXPERTS, 128), F32)],
        compiler_params=_cparams("arbitrary", "arbitrary"),
        name="outproj",
    )(x, attn, rec, w[:NSA_WIDTH], w[NSA_WIDTH:], gt1, sc2, sh2, norm2_w,
      rwt_split, router_bias.reshape(N_EXPERTS, 1), tri, ones)


def _mixer(x, c, ada_w, ada_b, norm1_w, norm2_w, w_in, q_norm_w, k_norm_w, cmp_pos, cmp_w1, cmp_b1, cmp_w2,
           attn_out_norm_w, hgrn_lb_param, rec_out_norm_w, w_out, router_w, router_bias):
    b, s, d = x.shape
    mod = _mod(c, ada_w, ada_b)
    sh1, sc1, gt1, sh2, sc2, gt2 = [m.reshape(b, 1, d) for m in jnp.split(mod, 6, axis=-1)]
    o = NSA_WIDTH + 6 * KV_WIDTH
    w_cat = jnp.concatenate([w_in[:, :o], w_in[:, o:o + NSA_HEADS * 3],
                             jnp.zeros((d, GATE_PAD - NSA_HEADS * 3), w_in.dtype),
                             w_in[:, o + NSA_HEADS * 3:]], axis=1).astype(BF16)
    tm = min(512, s)
    (q, kc_raw, vc_raw, ks, vst, kw, vwt, gates_t, hq, hk, hlf, hv, hg) = _inproj(
        x, sc1, sh1, norm1_w.reshape(1, d), w_cat, q_norm_w.reshape(1, HEAD_DIM), k_norm_w, hgrn_lb_param, tm)
    kc, vct = _compress(kc_raw, vc_raw, cmp_pos, cmp_w1, cmp_b1, cmp_w2, k_norm_w)
    attn = _nsa(q, kc, vct, ks, vst, kw, vwt, gates_t, attn_out_norm_w)
    rec = _hgrn(hq, hk, hlf, hv, hg, rec_out_norm_w, min(1024, s))
    x1, h2p, idx, w, rank, cnt = _outproj(x, attn, rec, w_out, gt1, sc2, sh2, norm2_w.reshape(1, d),
                                          router_w, router_bias, min(512, s))
    return x1, h2p, gt2, (idx, w, rank, cnt)


def _route(h, first_step, rwt_ref, bias_ref, tri_ref, ones_ref, idx_ref, w_ref, rank_ref, cnt_ref, carry_scr, tr):
    @pl.when(first_step)
    def _():
        carry_scr[...] = jnp.zeros_like(carry_scr)

    h_hi = h.astype(BF16)
    h_lo = (h - h_hi.astype(F32)).astype(BF16)
    logits = _dot_nt(rwt_ref[0], h_hi) + _dot_nt(rwt_ref[1], h_hi) + _dot_nt(rwt_ref[0], h_lo)
    scores = _sigmoid(logits)
    biased = scores + bias_ref[...]
    neg = -jnp.inf

    gs = []
    for g in range(N_GROUPS):
        sub = biased[g * GROUP_SIZE:(g + 1) * GROUP_SIZE, :]
        m1 = jnp.max(sub, axis=0, keepdims=True)
        dup = jnp.sum((sub == m1).astype(F32), axis=0, keepdims=True)
        m2 = jnp.max(jnp.where(sub < m1, sub, neg), axis=0, keepdims=True)
        gs.append(m1 + jnp.where(dup >= 2.0, m1, m2))
    parts = []
    for g in range(N_GROUPS):
        beaten = jnp.zeros_like(gs[g])
        for g2 in range(N_GROUPS):
            if g2 != g:
                beats = (gs[g2] >= gs[g]) if g2 < g else (gs[g2] > gs[g])
                beaten = beaten + beats.astype(F32)
        sub = biased[g * GROUP_SIZE:(g + 1) * GROUP_SIZE, :]
        parts.append(jnp.where(beaten < float(TOPK_GROUPS), sub, neg))
    cand = jnp.concatenate(parts, axis=0)

    rowf = lax.broadcasted_iota(I32, (N_EXPERTS, tr), 0).astype(F32)
    idx_rows, w_rows, hits = [], [], []
    multi = jnp.zeros((N_EXPERTS, tr), F32)
    for _ in range(TOP_K):
        mx = jnp.max(cand, axis=0, keepdims=True)
        first = jnp.min(jnp.where(cand == mx, rowf, float(N_EXPERTS)), axis=0, keepdims=True)
        hit = rowf == first
        idx_rows.append(first)
        w_rows.append(jnp.sum(jnp.where(hit, scores, 0.0), axis=0, keepdims=True))
        cand = jnp.where(hit, neg, cand)
        multi = jnp.where(hit, 1.0, multi)
    w = jnp.concatenate(w_rows, axis=0)
    w_ref[...] = w / jnp.sum(w, axis=0, keepdims=True) * ROUTED_SCALE
    idx = jnp.concatenate(idx_rows, axis=0)
    idx_ref[...] = idx.astype(I32)

    carry = carry_scr[...]
    mb = multi.astype(BF16)
    before = _dot(mb, tri_ref[...]) + jnp.concatenate([carry] * (tr // 128), axis=1)
    rank_rows = [jnp.sum(jnp.where(rowf == idx_rows[k], before, 0.0), axis=0, keepdims=True) for k in range(TOP_K)]
    rank_ref[...] = jnp.concatenate(rank_rows, axis=0).astype(I32)
    carry = carry + _dot(mb, ones_ref[...])
    carry_scr[...] = carry
    cnt_ref[...] = carry


def _pack_bf16_pair(a, b):
    ua = lax.bitcast_convert_type(a.astype(BF16).astype(F32), jnp.uint32)
    ub = lax.bitcast_convert_type(b.astype(BF16).astype(F32), jnp.uint32)
    return ua | (ub >> 16)


def _unpack_bf16_pair(w):
    a = lax.bitcast_convert_type(w & jnp.uint32(0xFFFF0000), F32)
    b = lax.bitcast_convert_type(w << 16, F32)
    return a, b


def _slot_kernel(ps_ref, idx_ref, rank_ref, slot_ref):
    idx = idx_ref[...]

    def body(e, acc):
        return jnp.where(idx == e, ps_ref[e], acc)

    slot_ref[...] = lax.fori_loop(0, N_EXPERTS, body, jnp.zeros_like(idx)) + rank_ref[...]


def _slots(pad_start, idx, rank, tt):
    t = idx.shape[1]
    tok = pl.BlockSpec((TOP_K, tt), lambda i, ps: (0, i))
    return pl.pallas_call(
        _slot_kernel,
        grid_spec=pltpu.PrefetchScalarGridSpec(num_scalar_prefetch=1, grid=(t // tt,),
                                               in_specs=[tok, tok], out_specs=tok),
        out_shape=jax.ShapeDtypeStruct((TOP_K, t), I32),
        compiler_params=_cparams("parallel"),
        name="slots",
    )(pad_start, idx, rank)


SC_CORES = 2
SC_SUBCORES = 16
SC_CHUNK = 64


def _sc_mesh():
    return plsc.VectorSubcoreMesh(core_axis_name="c", subcore_axis_name="s")


def _sc_dispatch(h2p, slot_chunks, n_rows):
    t, dw = h2p.shape
    per = slot_chunks.shape[0] // (SC_CORES * SC_SUBCORES)

    def body(h_hbm, slot_hbm, xs_hbm, idx_v, rows_v, sem):
        wid = lax.axis_index("s") * SC_CORES + lax.axis_index("c")

        @pl.loop(0, per)
        def _(c):
            ch = wid * per + c
            pltpu.sync_copy(slot_hbm.at[ch], idx_v)
            pltpu.sync_copy(h_hbm.at[pl.ds(ch * SC_CHUNK, SC_CHUNK)], rows_v)
            copies = [pltpu.async_copy(rows_v, xs_hbm.at[idx_v.at[k]], sem) for k in range(TOP_K)]
            for cp in copies:
                cp.wait()

    return pl.kernel(
        body, out_type=jax.ShapeDtypeStruct((n_rows, dw), h2p.dtype), mesh=_sc_mesh(),
        scratch_types=[pltpu.VMEM((TOP_K, SC_CHUNK), I32), pltpu.VMEM((SC_CHUNK, dw), h2p.dtype),
                       pltpu.SemaphoreType.DMA],
    )(h2p, slot_chunks)


def _sc_gather(ys, slot_chunks, t):
    dw = ys.shape[1]
    per = slot_chunks.shape[0] // (SC_CORES * SC_SUBCORES)

    def body(ys_hbm, slot_hbm, yg_hbm, idx_v, rows_v, gsem, wsem):
        wid = lax.axis_index("s") * SC_CORES + lax.axis_index("c")

        @pl.loop(0, per)
        def _(c):
            ch = wid * per + c
            pltpu.sync_copy(slot_hbm.at[ch], idx_v)
            gathers = [None] * TOP_K
            writes = [None] * TOP_K
            gathers[0] = pltpu.async_copy(ys_hbm.at[idx_v.at[0]], rows_v.at[0], gsem)
            for k in range(TOP_K):
                gathers[k].wait()
                if k + 1 < TOP_K:
                    if k >= 1:
                        writes[k - 1].wait()
                    gathers[k + 1] = pltpu.async_copy(ys_hbm.at[idx_v.at[k + 1]], rows_v.at[(k + 1) % 2], gsem)
                writes[k] = pltpu.async_copy(rows_v.at[k % 2], yg_hbm.at[k, pl.ds(ch * SC_CHUNK, SC_CHUNK)], wsem)
            writes[TOP_K - 2].wait()
            writes[TOP_K - 1].wait()

    return pl.kernel(
        body, out_type=jax.ShapeDtypeStruct((TOP_K, t, dw), ys.dtype), mesh=_sc_mesh(),
        scratch_types=[pltpu.VMEM((TOP_K, SC_CHUNK), I32), pltpu.VMEM((2, SC_CHUNK, dw), ys.dtype),
                       pltpu.SemaphoreType.DMA, pltpu.SemaphoreType.DMA],
    )(ys, slot_chunks)


def _experts_kernel(be_ref, nu_ref, bv_ref, run_ref, xs_hbm, wg_hbm, wu_hbm, wd_hbm, ys_ref,
                    xring, rsem, gring, uring, dring, wsem):
    i = pl.program_id(0)
    half = D_MODEL // 2
    n_used = nu_ref[0]
    n_steps = pl.num_programs(0)

    def weight_copies(blk):
        ex = be_ref[blk]
        slot = run_ref[blk] % EXPERT_RING
        return [pltpu.make_async_copy(src.at[ex], ring.at[slot], wsem.at[a, slot])
                for a, (src, ring) in enumerate(((wg_hbm, gring), (wu_hbm, uring), (wd_hbm, dring)))]

    def starts_run(blk):
        return run_ref[blk] != run_ref[jnp.maximum(blk - 1, 0)]

    @pl.when(i == 0)
    def _():
        for cp in weight_copies(jnp.int32(0)):
            cp.start()

        @pl.when((n_steps > 1) & starts_run(jnp.int32(1)))
        def _():
            for cp in weight_copies(jnp.int32(1)):
                cp.start()

    ahead = jnp.minimum(i + (EXPERT_RING - 1), n_steps - 1)

    @pl.when((i + (EXPERT_RING - 1) < n_steps) & starts_run(ahead))
    def _():
        for cp in weight_copies(ahead):
            cp.start()

    @pl.when((i == 0) | starts_run(i))
    def _():
        for cp in weight_copies(i):
            cp.wait()

    wslot = run_ref[i] % EXPERT_RING
    wg_ref, wu_ref, wd_ref = gring.at[wslot], uring.at[wslot], dring.at[wslot]

    def fetch(blk):
        slot = blk % EXPERT_RING
        return pltpu.make_async_copy(xs_hbm.at[pl.ds(pl.multiple_of(blk * EXPERT_BLOCK, EXPERT_BLOCK), EXPERT_BLOCK)],
                                     xring.at[slot], rsem.at[slot])

    @pl.when(i == 0)
    def _():
        for first in range(EXPERT_RING - 1):
            @pl.when(first < n_used)
            def _():
                fetch(jnp.int32(first)).start()

    @pl.when(i + (EXPERT_RING - 1) < n_used)
    def _():
        fetch(i + (EXPERT_RING - 1)).start()

    @pl.when(i < n_used)
    def _():
        fetch(i).wait()

    xs_ref = xring.at[i % EXPERT_RING]

    def ffn(rows):
        live = lax.broadcasted_iota(I32, (rows, xs_ref.shape[1]), 0) < bv_ref[i]
        xa, xb = _unpack_bf16_pair(jnp.where(live, xs_ref[0:rows, :], jnp.uint32(0)))
        xa, xb = xa.astype(BF16), xb.astype(BF16)
        g = _dot(xa, wg_ref[:half, :].astype(BF16)) + _dot(xb, wg_ref[half:, :].astype(BF16))
        u = _dot(xa, wu_ref[:half, :].astype(BF16)) + _dot(xb, wu_ref[half:, :].astype(BF16))
        act = (g * _sigmoid(g) * u).astype(BF16)
        y = _dot(act, wd_ref[...].astype(BF16))
        ys_ref[0:rows, :] = _pack_bf16_pair(y[:, :half], y[:, half:])

    used = i < n_used
    short = bv_ref[i] <= EXPERT_TAIL

    @pl.when(used & jnp.logical_not(short))
    def _():
        ffn(EXPERT_BLOCK)

    @pl.when(used & short)
    def _():
        ffn(EXPERT_TAIL)
        ys_ref[EXPERT_TAIL:, :] = jnp.zeros((EXPERT_BLOCK - EXPERT_TAIL, ys_ref.shape[1]), ys_ref.dtype)

    @pl.when(jnp.logical_not(used))
    def _():
        ys_ref[...] = jnp.zeros_like(ys_ref)


def _experts(xs, blk_e, n_used, blk_valid, w_gate, w_up, w_down):
    n_rows, dw = xs.shape
    d = w_gate.shape[1]
    nblk = n_rows // EXPERT_BLOCK
    blk_run = jnp.cumsum(jnp.concatenate([jnp.zeros((1,), I32), (blk_e[1:] != blk_e[:-1]).astype(I32)])).astype(I32)
    hbm = pl.BlockSpec(memory_space=pl.ANY)
    return pl.pallas_call(
        _experts_kernel,
        grid_spec=pltpu.PrefetchScalarGridSpec(
            num_scalar_prefetch=4,
            grid=(nblk,),
            in_specs=[hbm, hbm, hbm, hbm],
            out_specs=pl.BlockSpec((EXPERT_BLOCK, dw), lambda i, be, nu, bv, rn: (i, 0)),
            scratch_shapes=[pltpu.VMEM((EXPERT_RING, EXPERT_BLOCK, dw), xs.dtype),
                            pltpu.SemaphoreType.DMA((EXPERT_RING,)),
                            pltpu.VMEM((EXPERT_RING, d, EXPERT_FF), w_gate.dtype),
                            pltpu.VMEM((EXPERT_RING, d, EXPERT_FF), w_up.dtype),
                            pltpu.VMEM((EXPERT_RING, EXPERT_FF, d), w_down.dtype),
                            pltpu.SemaphoreType.DMA((3, EXPERT_RING))]),
        out_shape=jax.ShapeDtypeStruct((n_rows, dw), xs.dtype),
        compiler_params=pltpu.CompilerParams(dimension_semantics=("arbitrary",), vmem_limit_bytes=VMEM_LIMIT,
                                             has_side_effects=True),
        name="experts",
    )(blk_e, n_used, blk_valid, blk_run, xs, w_gate, w_up, w_down)


def _combine_kernel(x1_ref, h_ref, w_ref, gt_ref, sg_ref, su_ref, sd_ref, yg_ref, o_ref):
    tc = x1_ref.shape[0]
    half = D_MODEL // 2
    ha, hb = _unpack_bf16_pair(h_ref[...])
    ha, hb = ha.astype(BF16), hb.astype(BF16)
    g = _dot(ha, sg_ref[:half, :]) + _dot(hb, sg_ref[half:, :])
    u = _dot(ha, su_ref[:half, :]) + _dot(hb, su_ref[half:, :])
    ffn = _dot((g * _sigmoid(g) * u).astype(BF16), sd_ref[...])

    w = w_ref[...]
    ra = jnp.zeros((tc, half), F32)
    rb = jnp.zeros((tc, half), F32)
    for k in range(TOP_K):
        ya, yb = _unpack_bf16_pair(yg_ref[k])
        ra = ra + w[:, k:k + 1] * ya
        rb = rb + w[:, k:k + 1] * yb
    ffn = ffn + jnp.concatenate([ra, rb], axis=1)
    o_ref[...] = x1_ref[...] + gt_ref[0] * ffn


def _combine(x1, h2p, w_tok, gt2, yg, sg, su, sd, seq, tc):
    t, d = x1.shape
    row = lambda i: (i, 0)
    fixed = lambda i: (0, 0)
    return pl.pallas_call(
        _combine_kernel,
        grid=(t // tc,),
        in_specs=[pl.BlockSpec((tc, d), row),
                  pl.BlockSpec((tc, d // 2), row),
                  pl.BlockSpec((tc, TOP_K), row),
                  pl.BlockSpec((1, 1, d), lambda i: ((i * tc) // seq, 0, 0)),
                  pl.BlockSpec((d, SHARED_FF), fixed),
                  pl.BlockSpec((d, SHARED_FF), fixed),
                  pl.BlockSpec((SHARED_FF, d), fixed),
                  pl.BlockSpec((TOP_K, tc, d // 2), lambda i: (0, i, 0))],
        out_specs=pl.BlockSpec((tc, d), row),
        out_shape=jax.ShapeDtypeStruct((t, d), F32),
        compiler_params=_cparams("parallel"),
        name="combine",
    )(x1, h2p, w_tok, gt2, sg.astype(BF16), su.astype(BF16), sd.astype(BF16), yg)


def _moe_parts(x1, h2p, gt2, routing, w_gate, w_up, w_down, sg, su, sd):
    b, s, d = x1.shape
    t = b * s
    assert t % (SC_CHUNK * SC_CORES * SC_SUBCORES) == 0, "token chunks must split evenly over the vector subcores"
    idx, w, rank, cnt = routing
    counts = cnt[:, 0].astype(I32)
    padded = (counts + EXPERT_BLOCK - 1) // EXPERT_BLOCK * EXPERT_BLOCK
    pad_end = jnp.cumsum(padded)
    pad_start = pad_end - padded
    n_rows = t * TOP_K + N_EXPERTS * EXPERT_BLOCK
    nblk = n_rows // EXPERT_BLOCK
    n_used = (pad_end[-1:] // EXPERT_BLOCK).astype(I32)
    blk_start = jnp.arange(nblk, dtype=I32) * EXPERT_BLOCK
    owns = (pad_start[None, :] <= blk_start[:, None]) & (blk_start[:, None] < pad_end[None, :])
    e_ids = jnp.arange(N_EXPERTS, dtype=I32)[None, :]
    last_e = jnp.max(jnp.where(counts > 0, e_ids[0], 0))
    blk_e = jnp.where(blk_start < pad_end[-1], jnp.sum(jnp.where(owns, e_ids, 0), axis=1), last_e).astype(I32)
    rows_left = jnp.sum(jnp.where(owns, (pad_start + counts)[None, :] - blk_start[:, None], 0), axis=1)
    blk_valid = jnp.clip(rows_left, 0, EXPERT_BLOCK).astype(I32)
    slot = _slots(pad_start.astype(I32), idx, rank, min(2048, t))
    slot_chunks = slot.reshape(TOP_K, t // SC_CHUNK, SC_CHUNK).transpose(1, 0, 2)
    xs = _sc_dispatch(h2p.reshape(t, d // 2), slot_chunks, n_rows)
    ys = _experts(xs, blk_e, n_used, blk_valid, w_gate, w_up, w_down)
    yg = _sc_gather(ys, slot_chunks, t)
    out = _combine(x1.reshape(t, d), h2p.reshape(t, d // 2), w.T, gt2, yg, sg, su, sd, s, min(512, t))
    return out.reshape(b, s, d), dict(idx=idx, w=w, rank=rank, cnt=cnt)


def kernel(x, c, ada_w, ada_b, norm1_w, norm2_w, w_in, q_norm_w, k_norm_w, cmp_pos, cmp_w1, cmp_b1, cmp_w2, attn_out_norm_w, hgrn_lb_param, rec_out_norm_w, w_out, router_w, router_bias, exp_w_gate, exp_w_up, exp_w_down, shared_w_gate, shared_w_up, shared_w_down):
    assert ada_w.shape[0] == 1, "one layer"
    assert x.shape[0] <= 8 and x.shape[1] % TK == 0 and x.shape[1] >= WINDOW + TQ
    l = 0
    x1, h2p, gt2, routing = _mixer(x, c, ada_w[l], ada_b[l], norm1_w[l], norm2_w[l], w_in[l], q_norm_w[l],
                                   k_norm_w[l], cmp_pos[l], cmp_w1[l], cmp_b1[l], cmp_w2[l], attn_out_norm_w[l],
                                   hgrn_lb_param, rec_out_norm_w[l], w_out[l], router_w[l], router_bias[l])
    out, _ = _moe_parts(x1, h2p, gt2, routing, exp_w_gate[l], exp_w_up[l], exp_w_down[l],
                        shared_w_gate[l], shared_w_up[l], shared_w_down[l])
    return out
```

```python
import functools

import numpy as np
import jax
import jax.numpy as jnp
from jax import lax
from jax.experimental import pallas as pl
from jax.experimental.pallas import tpu as pltpu
from jax.experimental.pallas import tpu_sc as plsc

F32 = jnp.float32
BF16 = jnp.bfloat16
I32 = jnp.int32

D_MODEL = 1024
NSA_HEADS = 8
HEAD_DIM = 64
NSA_WIDTH = NSA_HEADS * HEAD_DIM
KV_HEADS = 2
HEADS_PER_KV = NSA_HEADS // KV_HEADS
KV_WIDTH = KV_HEADS * HEAD_DIM
CMP_BLOCK = 32
CMP_STRIDE = 16
CMP_HIDDEN = 256
SEL_BLOCK = 64
N_SELECT = 16
WINDOW = 512
HGRN_HEADS = 4
HGRN_DIM = 128
HGRN_WIDTH = HGRN_HEADS * HGRN_DIM
HGRN_CHUNK = 64
HGRN_SUB = 16
N_EXPERTS = 256
TOP_K = 8
N_GROUPS = 8
GROUP_SIZE = N_EXPERTS // N_GROUPS
TOPK_GROUPS = 4
EXPERT_FF = 256
SHARED_FF = 256
ROUTED_SCALE = 2.5
RMS_EPS = 1e-6
BIG = 1e9
LOG2E = 1.4426950408889634
GATE_PAD = 128
PROJ_COLS = NSA_WIDTH + 6 * KV_WIDTH + GATE_PAD + 4 * HGRN_WIDTH

VMEM_LIMIT = 56 * 1024 * 1024

TQ = 256
TK = 256
NSA_SIZE_VARIANTS = 4
EXPERT_BLOCK = 512
EXPERT_TAIL = 128
EXPERT_RING = 3
HIGHEST = lax.Precision.HIGHEST


def _cparams(*sem):
    return pltpu.CompilerParams(dimension_semantics=sem, vmem_limit_bytes=VMEM_LIMIT)


def _sigmoid(x):
    return 1.0 / (1.0 + jnp.exp(-x))


def _dot_nt(a, b):
    return lax.dot_general(a, b, (((1,), (1,)), ((), ())), preferred_element_type=F32)


def _dot(a, b, **kw):
    return jnp.dot(a, b, preferred_element_type=F32, **kw)


def _split_dot(a_bf16_exact, x):
    hi = x.astype(BF16)
    lo = (x - hi.astype(F32)).astype(BF16)
    return _dot(a_bf16_exact, hi) + _dot(a_bf16_exact, lo)


def _mod_kernel(c_ref, w_ref, b_ref, o_ref):
    c = c_ref[...]
    cond = c * _sigmoid(c)
    o_ref[...] = _dot(cond, w_ref[...], precision=HIGHEST) + b_ref[...]


def _mod(c, ada_w, ada_b):
    b, d = c.shape
    rows = 8
    c_pad = jnp.zeros((rows, d), F32).at[:b].set(c)
    n = ada_w.shape[1]
    out = pl.pallas_call(
        _mod_kernel,
        grid=(n // d,),
        in_specs=[pl.BlockSpec((rows, d), lambda j: (0, 0)),
                  pl.BlockSpec((d, d), lambda j: (0, j)),
                  pl.BlockSpec((1, d), lambda j: (0, j))],
        out_specs=pl.BlockSpec((rows, d), lambda j: (0, j)),
        out_shape=jax.ShapeDtypeStruct((rows, n), F32),
        compiler_params=_cparams("parallel"),
        name="mod",
    )(c_pad, ada_w, ada_b.reshape(1, n))
    return out[:b]


def _head_rms(t, w):
    return t * lax.rsqrt(jnp.mean(t * t, axis=-1, keepdims=True) + RMS_EPS) * w


def _pos_digits(pos):
    lane = lax.broadcasted_iota(I32, pos.shape, 1)
    d0 = (lane == 0) | (lane == 3) | (lane == 6)
    d1 = (lane == 1) | (lane == 4) | (lane == 7)
    d2 = (lane == 2) | (lane == 5) | (lane == 8)
    dig = jnp.where(d0, pos >> 12, jnp.where(d1, (pos >> 6) & 63, jnp.where(d2, pos & 63, 0)))
    return dig.astype(F32)


def _inproj_kernel(x_ref, sc_ref, sh_ref, n1_ref, w_ref, qnw_ref, knw_ref, lbp_ref, qaug_ref,
                   q_ref, kcr_ref, vcr_ref, ks_ref, vst_ref, kw_ref, vwt_ref, gt_ref,
                   hq_ref, hk_ref, hlf_ref, hv_ref, hg_ref):
    x = x_ref[0]
    ms = jnp.mean(x * x, axis=-1, keepdims=True)
    h = x * lax.rsqrt(ms + RMS_EPS) * n1_ref[...] * (1.0 + sc_ref[0]) + sh_ref[0]
    p = _dot(h.astype(BF16), w_ref[...])
    tm = x.shape[0]

    qnw = qnw_ref[...]
    for hd in range(NSA_HEADS):
        t = p[:, hd * HEAD_DIM:(hd + 1) * HEAD_DIM]
        qn = _head_rms(t, qnw) * (HEAD_DIM ** -0.5 * LOG2E)
        qa = jnp.broadcast_to(qaug_ref[hd:hd + 1, :], (tm, HEAD_DIM))
        q_ref[0, hd] = jnp.concatenate([qn, qa], axis=1).astype(BF16)
    kaug = _pos_digits(pl.program_id(1) * tm + lax.broadcasted_iota(I32, (tm, HEAD_DIM), 0))

    o = NSA_WIDTH
    kcr_ref[0] = p[:, o:o + KV_WIDTH]
    vcr_ref[0] = p[:, o + KV_WIDTH:o + 2 * KV_WIDTH]
    ks = p[:, o + 2 * KV_WIDTH:o + 3 * KV_WIDTH]
    vs = p[:, o + 3 * KV_WIDTH:o + 4 * KV_WIDTH]
    kw = p[:, o + 4 * KV_WIDTH:o + 5 * KV_WIDTH]
    vw = p[:, o + 5 * KV_WIDTH:o + 6 * KV_WIDTH]
    for g in range(KV_HEADS):
        sl = slice(g * HEAD_DIM, (g + 1) * HEAD_DIM)
        ks_ref[0, g] = jnp.concatenate([_head_rms(ks[:, sl], knw_ref[1:2, :]), kaug], axis=1).astype(BF16)
        kw_ref[0, g] = jnp.concatenate([_head_rms(kw[:, sl], knw_ref[2:3, :]), kaug], axis=1).astype(BF16)
    vst = vs.T.astype(BF16)
    vwt = vw.T.astype(BF16)
    for g in range(KV_HEADS):
        vst_ref[0, g] = vst[g * HEAD_DIM:(g + 1) * HEAD_DIM, :]
        vwt_ref[0, g] = vwt[g * HEAD_DIM:(g + 1) * HEAD_DIM, :]

    o = NSA_WIDTH + 6 * KV_WIDTH
    gates = _sigmoid(p[:, o:o + GATE_PAD])
    gt_ref[0] = gates.T[:NSA_HEADS * 3, :]

    o = o + GATE_PAD
    hq = p[:, o:o + HGRN_WIDTH]
    hf = p[:, o + HGRN_WIDTH:o + 2 * HGRN_WIDTH]
    hi = p[:, o + 2 * HGRN_WIDTH:o + 3 * HGRN_WIDTH]
    hg = p[:, o + 3 * HGRN_WIDTH:o + 4 * HGRN_WIDTH]
    lbp = lbp_ref[...]
    e = jnp.exp(lbp - jnp.max(lbp, axis=0, keepdims=True))
    lb = e[0:1, :] / jnp.sum(e, axis=0, keepdims=True)
    f = lb + (1.0 - lb) * _sigmoid(hf)
    hq_ref[0] = hq * _sigmoid(hq) * (HGRN_DIM ** -0.5)
    hk_ref[0] = 1.0 - f
    hlf_ref[0] = jnp.log(f)
    hv_ref[0] = hi
    hg_ref[0] = _sigmoid(hg)


def _inproj(x, sc1, sh1, norm1_w, w_cat, q_norm_w, k_norm_w, lb_param, tm):
    b, s, d = x.shape
    row = lambda bi, i: (bi, i, 0)
    per_b = lambda bi, i: (bi, 0, 0)
    fixed2 = lambda bi, i: (0, 0)
    aw = 2 * HEAD_DIM
    rest = np.array([2.0 ** (-8.0 * (i + 1) / NSA_HEADS) for i in range(NSA_HEADS)], np.float64) * LOG2E
    qaug = np.zeros((NSA_HEADS, HEAD_DIM), np.float32)
    for i in range(3):
        term = rest.astype(np.float32).astype(BF16).astype(np.float64)
        rest = rest - term
        for dgt, wgt in enumerate((4096.0, 64.0, 1.0)):
            qaug[:, 3 * i + dgt] = term * wgt
    assert np.all(qaug == qaug.astype(BF16).astype(np.float32))
    out_shape = (
        jax.ShapeDtypeStruct((b, NSA_HEADS, s, aw), BF16),
        jax.ShapeDtypeStruct((b, s, KV_WIDTH), F32),
        jax.ShapeDtypeStruct((b, s, KV_WIDTH), F32),
        jax.ShapeDtypeStruct((b, KV_HEADS, s, aw), BF16),
        jax.ShapeDtypeStruct((b, KV_HEADS, HEAD_DIM, s), BF16),
        jax.ShapeDtypeStruct((b, KV_HEADS, s, aw), BF16),
        jax.ShapeDtypeStruct((b, KV_HEADS, HEAD_DIM, s), BF16),
        jax.ShapeDtypeStruct((b, NSA_HEADS * 3, s), F32),
    ) + tuple(jax.ShapeDtypeStruct((b, s, HGRN_WIDTH), F32) for _ in range(5))
    hm = lambda n, w: pl.BlockSpec((1, n, tm, w), lambda bi, i: (bi, 0, i, 0))
    hmt = lambda n, w: pl.BlockSpec((1, n, w, tm), lambda bi, i: (bi, 0, 0, i))
    out_specs = (
        hm(NSA_HEADS, aw),
        pl.BlockSpec((1, tm, KV_WIDTH), row),
        pl.BlockSpec((1, tm, KV_WIDTH), row),
        hm(KV_HEADS, aw), hmt(KV_HEADS, HEAD_DIM),
        hm(KV_HEADS, aw), hmt(KV_HEADS, HEAD_DIM),
        pl.BlockSpec((1, NSA_HEADS * 3, tm), lambda bi, i: (bi, 0, i)),
    ) + tuple(pl.BlockSpec((1, tm, HGRN_WIDTH), row) for _ in range(5))
    return pl.pallas_call(
        _inproj_kernel,
        grid=(b, s // tm),
        in_specs=[pl.BlockSpec((1, tm, d), row),
                  pl.BlockSpec((1, 1, d), per_b),
                  pl.BlockSpec((1, 1, d), per_b),
                  pl.BlockSpec((1, d), fixed2),
                  pl.BlockSpec((d, PROJ_COLS), fixed2),
                  pl.BlockSpec((1, HEAD_DIM), fixed2),
                  pl.BlockSpec((3, HEAD_DIM), fixed2),
                  pl.BlockSpec(lb_param.shape, fixed2),
                  pl.BlockSpec((NSA_HEADS, HEAD_DIM), fixed2)],
        out_specs=out_specs,
        out_shape=out_shape,
        compiler_params=_cparams("parallel", "parallel"),
        name="inproj",
    )(x, sc1, sh1, norm1_w, w_cat, q_norm_w, k_norm_w, lb_param, jnp.asarray(qaug))


def _gelu_tanh(x):
    return 0.5 * x * (1.0 + jnp.tanh(0.7978845608028654 * (x + 0.044715 * x * x * x)))


def _compress_kernel(kch_ref, vch_ref, pos_ref, wa_ref, wb_ref, b1_ref, w2_ref, knw_ref,
                     kc_ref, vct_ref):
    n = kch_ref.shape[1]
    outs = []
    for br, ch_ref in enumerate((kch_ref, vch_ref)):
        ch = ch_ref[0]
        a = _dot((ch + pos_ref[br, 0:1, :]).astype(BF16), wa_ref[br])
        bm = _dot((ch + pos_ref[br, 1:2, :]).astype(BF16), wb_ref[br])
        pre = a + pltpu.roll(bm, n - 1, 0) + b1_ref[br]
        hid = _gelu_tanh(pre).astype(BF16)
        outs.append([_dot(hid[:, g * CMP_HIDDEN:(g + 1) * CMP_HIDDEN], w2_ref[br]) for g in range(KV_HEADS)])
    end_digits = _pos_digits(lax.broadcasted_iota(I32, (n, HEAD_DIM), 0) * CMP_STRIDE + (CMP_BLOCK - 1))
    for g in range(KV_HEADS):
        kc_ref[0, g] = jnp.concatenate([_head_rms(outs[0][g], knw_ref[0:1, :]), end_digits], axis=1).astype(BF16)
    vct = jnp.concatenate(outs[1], axis=1).T.astype(BF16)
    for g in range(KV_HEADS):
        vct_ref[0, g] = vct[g * HEAD_DIM:(g + 1) * HEAD_DIM, :]


def _compress(kc_raw, vc_raw, cmp_pos, cmp_w1, cmp_b1, cmp_w2, k_norm_w):
    b, s, _ = kc_raw.shape
    n = s // CMP_STRIDE
    half = CMP_STRIDE
    cw = CMP_STRIDE * KV_WIDTH
    kch = kc_raw.reshape(b, n, cw)
    vch = vc_raw.reshape(b, n, cw)
    pos = cmp_pos.reshape(2, 2, half, 1, HEAD_DIM)
    pos = jnp.broadcast_to(pos, (2, 2, half, KV_HEADS, HEAD_DIM)).reshape(2, 2, cw)
    w1 = cmp_w1.reshape(2, 2, half, HEAD_DIM, CMP_HIDDEN)
    eye = jnp.eye(KV_HEADS, dtype=F32)
    wfull = jnp.einsum('rhjdn,gk->rhjgdkn', w1, eye).reshape(2, 2, cw, KV_HEADS * CMP_HIDDEN).astype(BF16)
    b1 = jnp.tile(cmp_b1.reshape(2, 1, CMP_HIDDEN), (1, 1, KV_HEADS))
    fix = lambda r: (lambda bi: (0,) * r)
    return pl.pallas_call(
        _compress_kernel,
        grid=(b,),
        in_specs=[pl.BlockSpec((1, n, cw), lambda bi: (bi, 0, 0)),
                  pl.BlockSpec((1, n, cw), lambda bi: (bi, 0, 0)),
                  pl.BlockSpec((2, 2, cw), fix(3)),
                  pl.BlockSpec((2, cw, KV_HEADS * CMP_HIDDEN), fix(3)),
                  pl.BlockSpec((2, cw, KV_HEADS * CMP_HIDDEN), fix(3)),
                  pl.BlockSpec((2, 1, KV_HEADS * CMP_HIDDEN), fix(3)),
                  pl.BlockSpec((2, CMP_HIDDEN, HEAD_DIM), fix(3)),
                  pl.BlockSpec((3, HEAD_DIM), fix(2))],
        out_specs=(pl.BlockSpec((1, KV_HEADS, n, 2 * HEAD_DIM), lambda bi: (bi, 0, 0, 0)),
                   pl.BlockSpec((1, KV_HEADS, HEAD_DIM, n), lambda bi: (bi, 0, 0, 0))),
        out_shape=(jax.ShapeDtypeStruct((b, KV_HEADS, n, 2 * HEAD_DIM), BF16),
                   jax.ShapeDtypeStruct((b, KV_HEADS, HEAD_DIM, n), BF16)),
        compiler_params=_cparams("parallel"),
        name="compress",
    )(kch, vch, pos, wfull[:, 0], wfull[:, 1], b1, cmp_w2.astype(BF16), k_norm_w)


def _nsa_kernel(q_ref, kc_ref, vct_ref, ks_ref, vst_ref, kw_ref, vwt_ref, gt_ref, cdiff_ref, wdiff_ref,
                ovl_ref, oh_ref, onw_ref, wmask_ref, o_ref, buf_a, buf_b, m_scr, acc_scr, oc_scr, bias_scr,
                lst, cnt, *, n_top):
    q0 = pl.program_id(2) * TQ
    ncols = HEADS_PER_KV * TQ
    q = q_ref[0].reshape(ncols, 2 * HEAD_DIM)
    ns = ovl_ref.shape[0]

    def compress_and_select(nk, nb):
        s = jnp.where(cdiff_ref[0:nk, :] <= q0, _dot_nt(kc_ref[0, 0, 0:nk, :], q), -jnp.inf)
        m = jnp.max(s, axis=0, keepdims=True)
        m = jnp.where(m == -jnp.inf, 0.0, m)
        e = jnp.exp2(s - m)
        p = e / jnp.maximum(jnp.sum(e, axis=0, keepdims=True), 1e-30)
        oc_scr[...] = _dot(vct_ref[0, 0, :, 0:nk], p.astype(BF16))

        psum = p[:, 0:TQ]
        for hh in range(1, HEADS_PER_KV):
            psum = psum + p[:, hh * TQ:(hh + 1) * TQ]
        imp = _split_dot(ovl_ref[0:nb, 0:nk], psum)
        blk = lax.broadcasted_iota(I32, (nb, TQ), 0)
        tq = q0 + lax.broadcasted_iota(I32, (nb, TQ), 1)
        cur = tq >> 6
        forced = (blk == 0) | (blk == cur) | (blk == cur - 1)
        rank = jnp.where(forced, BIG, jnp.where(blk * SEL_BLOCK <= tq, imp, -BIG))
        blkf = blk.astype(F32)

        bias = jnp.full((nb, TQ), -1e30, F32)
        for _ in range(min(n_top, nb)):
            mx = jnp.max(rank, axis=0, keepdims=True)
            first = jnp.min(jnp.where(rank == mx, blkf, float(nb)), axis=0, keepdims=True)
            hit = blkf == first
            rank = jnp.where(hit, -jnp.inf, rank)
            bias = jnp.where(hit, 0.0, bias)
        bias_scr[...] = jnp.full((128, TQ), -1e30, F32)
        bias_scr[0:nb, :] = jnp.where(blk == 0, -1e30, bias)

    nc = kc_ref.shape[2]
    quarter = (q0 + TQ - 1) // (ks_ref.shape[2] // NSA_SIZE_VARIANTS)
    for v in range(NSA_SIZE_VARIANTS):
        @pl.when(quarter == v)
        def _():
            compress_and_select(nc * (v + 1) // NSA_SIZE_VARIANTS, ns * (v + 1) // NSA_SIZE_VARIANTS)

    o_c = oc_scr[...]
    bias = bias_scr[...]

    bias_t = bias.T.astype(BF16)
    qq = jnp.concatenate([q, jnp.concatenate([bias_t] * HEADS_PER_KV, axis=0)], axis=1)
    ones_rows = jnp.ones((16, TK), BF16)

    def scores(j):
        k0 = pl.multiple_of(j * TK, TK)
        kk = jnp.concatenate([ks_ref[0, 0, pl.ds(k0, TK), :], oh_ref[pl.ds(k0, TK), :]], axis=1)
        return _dot_nt(kk, qq)

    def consume(buf, j, causal, part):
        sc = buf[...]
        if causal:
            sc = jnp.where(wdiff_ref[0:TK, :] + (q0 - j * TK) >= 0, sc, -1e30)
        k0 = pl.multiple_of(j * TK, TK)
        m_run = m_scr[part]
        m_new = jnp.maximum(m_run, jnp.max(sc, axis=0, keepdims=True))
        ex = jnp.exp2(sc - m_new).astype(BF16)
        va = jnp.concatenate([vst_ref[0, 0, :, pl.ds(k0, TK)], ones_rows], axis=0)
        acc_scr[part] = jnp.exp2(m_run - m_new) * acc_scr[part] + _dot(va, ex)
        m_scr[part] = m_new

    n_past = q0 // TK
    blocks_per_tile = TK // SEL_BLOCK
    cnt[0] = 0
    for j in range(ks_ref.shape[2] // TK):
        wanted = jnp.max(bias[j * blocks_per_tile:(j + 1) * blocks_per_tile, :]) == 0.0

        @pl.when(wanted & (j < n_past))
        def _():
            lst[cnt[0]] = j
            cnt[0] = cnt[0] + 1

    n_sel = cnt[0]
    lst[n_sel] = n_past

    buf_a[...] = scores(lst[0])

    s0 = jnp.where(wdiff_ref[0:SEL_BLOCK, :] + q0 >= 0, _dot_nt(ks_ref[0, 0, 0:SEL_BLOCK, :], q), -1e30)
    m0 = jnp.max(s0, axis=0, keepdims=True)
    v0 = jnp.concatenate([vst_ref[0, 0, :, 0:SEL_BLOCK], jnp.ones((16, SEL_BLOCK), BF16)], axis=0)
    m_scr[0] = m0
    acc_scr[0] = _dot(v0, jnp.exp2(s0 - m0).astype(BF16))
    m_scr[1] = jnp.full((1, ncols), -1e30, F32)
    acc_scr[1] = jnp.zeros((HEAD_DIM + 16, ncols), F32)

    nw = WINDOW + TQ
    start = pl.multiple_of(jnp.maximum(q0 - WINDOW, 0), TQ)
    sw = _dot_nt(kw_ref[0, 0, pl.ds(start, nw), :], q) + wmask_ref[0]
    ew = jnp.exp2(sw - jnp.max(sw, axis=0, keepdims=True))
    vw_aug = jnp.concatenate([vwt_ref[0, 0, :, pl.ds(start, nw)], jnp.ones((16, nw), BF16)], axis=0)
    acc_w = _dot(vw_aug, ew.astype(BF16))
    o_w = acc_w[0:HEAD_DIM, :] / acc_w[HEAD_DIM:HEAD_DIM + 1, :]

    def tiles(first, count):
        for u in range(0, count, 2):
            buf_b[...] = scores(lst[first + u + 1])
            consume(buf_a, lst[first + u], False, 0)
            buf_a[...] = scores(lst[first + u + 2])
            consume(buf_b, lst[first + u + 1], False, 1)
        return 0

    lax.fori_loop(0, n_sel // 4, lambda i, _: tiles(4 * i, 4), 0)
    lax.fori_loop(0, (n_sel // 2) % 2, lambda i, _: tiles((n_sel // 4) * 4, 2), 0)

    @pl.when(n_sel % 2 == 1)
    def _():
        buf_b[...] = scores(n_past)
        consume(buf_a, lst[n_sel - 1], False, 0)
        consume(buf_b, n_past, True, 1)

    @pl.when(n_sel % 2 == 0)
    def _():
        consume(buf_a, n_past, True, 0)

    m_all = jnp.maximum(m_scr[0], m_scr[1])
    acc_s = jnp.exp2(m_scr[0] - m_all) * acc_scr[0] + jnp.exp2(m_scr[1] - m_all) * acc_scr[1]
    o_s = acc_s[0:HEAD_DIM, :] / acc_s[HEAD_DIM:HEAD_DIM + 1, :]

    gt = gt_ref[0, 0]
    outs = []
    for hh in range(HEADS_PER_KV):
        cs = slice(hh * TQ, (hh + 1) * TQ)
        o = (gt[3 * hh:3 * hh + 1, :] * o_c[:, cs] + gt[3 * hh + 1:3 * hh + 2, :] * o_s[:, cs]
             + gt[3 * hh + 2:3 * hh + 3, :] * o_w[:, cs])
        o = o * lax.rsqrt(jnp.mean(o * o, axis=0, keepdims=True) + RMS_EPS) * onw_ref[0, hh]
        outs.append(o)
    o_ref[0] = jnp.concatenate(outs, axis=0).T


def _nsa(q, kc, vct, ks, vst, kw, vwt, gates_t, attn_out_norm_w):
    b, _, s, aw = q.shape
    nc = kc.shape[2]
    ns = s // SEL_BLOCK
    n_top = min(N_SELECT, ns)
    ncols = HEADS_PER_KV * TQ
    nw = WINDOW + TQ
    tl = np.arange(ncols)[None, :] & (TQ - 1)
    cdiff = jnp.asarray((np.arange(nc)[:, None] * CMP_STRIDE + (CMP_BLOCK - 1) - tl).astype(np.int32))
    wdiff_np = (tl - np.arange(nw)[:, None]).astype(np.int32)
    wdiff = jnp.asarray(wdiff_np)
    n_off = WINDOW // TQ + 1
    dist_np = wdiff_np[None] + (np.arange(n_off) * TQ)[:, None, None]
    wmask = jnp.asarray(np.where((dist_np >= 0) & (dist_np < WINDOW), 0.0, -np.inf).astype(np.float32))
    ci = np.arange(nc)[None, :] * CMP_STRIDE
    bj = np.arange(ns)[:, None]
    ovl = ((ci < (bj + 1) * SEL_BLOCK) & (ci + CMP_BLOCK > bj * SEL_BLOCK) & (np.arange(nc)[None, :] < nc - 1))
    ovl = jnp.asarray(ovl.astype(np.float32)).astype(BF16)
    assert ns <= 128
    onehot = (np.arange(s)[:, None] // SEL_BLOCK == np.arange(128)[None, :])
    onehot = jnp.asarray(onehot.astype(np.float32)).astype(BF16)
    onw = jnp.broadcast_to(attn_out_norm_w.reshape(KV_HEADS, HEADS_PER_KV, HEAD_DIM, 1),
                           (KV_HEADS, HEADS_PER_KV, HEAD_DIM, TQ))
    gt = gates_t.reshape(b, KV_HEADS, HEADS_PER_KV * 3, s)
    per_bg = lambda bi, g, i: (bi, g, 0, 0)
    fixed = lambda bi, g, i: (0, 0)
    return pl.pallas_call(
        functools.partial(_nsa_kernel, n_top=n_top),
        grid=(b, KV_HEADS, s // TQ),
        in_specs=[pl.BlockSpec((1, HEADS_PER_KV, TQ, aw), lambda bi, g, i: (bi, g, i, 0)),
                  pl.BlockSpec((1, 1, nc, aw), per_bg),
                  pl.BlockSpec((1, 1, HEAD_DIM, nc), per_bg),
                  pl.BlockSpec((1, 1, s, aw), per_bg),
                  pl.BlockSpec((1, 1, HEAD_DIM, s), per_bg),
                  pl.BlockSpec((1, 1, s, aw), per_bg),
                  pl.BlockSpec((1, 1, HEAD_DIM, s), per_bg),
                  pl.BlockSpec((1, 1, HEADS_PER_KV * 3, TQ), lambda bi, g, i: (bi, g, 0, i)),
                  pl.BlockSpec((nc, ncols), fixed, pipeline_mode=pl.Buffered(1)),
                  pl.BlockSpec((nw, ncols), fixed, pipeline_mode=pl.Buffered(1)),
                  pl.BlockSpec((ns, nc), fixed, pipeline_mode=pl.Buffered(1)),
                  pl.BlockSpec((s, 128), fixed, pipeline_mode=pl.Buffered(1)),
                  pl.BlockSpec((1, HEADS_PER_KV, HEAD_DIM, TQ), lambda bi, g, i: (g, 0, 0, 0)),
                  pl.BlockSpec((1, nw, ncols), lambda bi, g, i: (jnp.minimum(i, n_off - 1), 0, 0))],
        out_specs=pl.BlockSpec((1, TQ, HEADS_PER_KV * HEAD_DIM), lambda bi, g, i: (bi, i, g)),
        out_shape=jax.ShapeDtypeStruct((b, s, NSA_WIDTH), F32),
        scratch_shapes=[pltpu.VMEM((TK, ncols), F32), pltpu.VMEM((TK, ncols), F32),
                        pltpu.VMEM((2, 1, ncols), F32), pltpu.VMEM((2, HEAD_DIM + 16, ncols), F32),
                        pltpu.VMEM((HEAD_DIM, ncols), F32), pltpu.VMEM((128, TQ), F32),
                        pltpu.SMEM((s // TK + 1,), I32), pltpu.SMEM((1,), I32)],
        compiler_params=_cparams("parallel", "parallel", "arbitrary"),
        name="nsa",
    )(q, kc, vct, ks, vst, kw, vwt, gt, cdiff, wdiff, ovl, onehot, onw, wmask)


def _hgrn_kernel(q_ref, k_ref, lf_ref, v_ref, g_ref, onw_ref, cm_ref, o_ref, state_scr, *, n_chunks):
    c = HGRN_CHUNK

    @pl.when(pl.program_id(1) == 0)
    def _():
        state_scr[...] = jnp.zeros_like(state_scr)

    ri = lax.broadcasted_iota(I32, (c, c), 0)
    ci = lax.broadcasted_iota(I32, (c, c), 1)
    rsub = ri // HGRN_SUB
    diag = ri == ci

    def head_chunk(r0, hd, state_t):
        cols = slice(hd * HGRN_DIM, (hd + 1) * HGRN_DIM)
        q = q_ref[0, pl.ds(r0, c), cols]
        k = k_ref[0, pl.ds(r0, c), cols]
        lf = lf_ref[0, pl.ds(r0, c), cols] * LOG2E
        v = v_ref[0, pl.ds(r0, c), cols]
        cm = cm_ref[...]
        l1 = lf.astype(BF16)
        rest = lf - l1.astype(F32)
        l2 = rest.astype(BF16)
        l3 = (rest - l2.astype(F32)).astype(BF16)
        cum = _dot(cm, l1) + _dot(cm, l2) + _dot(cm, l3)
        o = _dot_nt((q * jnp.exp2(cum)).astype(BF16), state_t.astype(BF16))
        scores = jnp.where(diag, jnp.sum(q * k, axis=-1, keepdims=True), 0.0)

        def factored(ref, mask, acc):
            qs = q * jnp.exp2(jnp.minimum(cum - ref, 0.0))
            kd = k * jnp.exp2(jnp.minimum(ref - cum, 0.0))
            return jnp.where(mask, _dot_nt(qs.astype(BF16), kd.astype(BF16)), acc)

        for i in range(1, c // HGRN_SUB):
            scores = factored(cum[i * HGRN_SUB - 1:i * HGRN_SUB, :], (rsub == i) & (ci < i * HGRN_SUB), scores)
        for d in range(1, HGRN_SUB):
            ksh = pltpu.roll(k, d, 0)
            csh = pltpu.roll(cum, d, 0)
            w = jnp.sum(q * ksh * jnp.exp2(cum - csh), axis=-1, keepdims=True)
            scores = jnp.where((ri - ci == d) & ((ri & (HGRN_SUB - 1)) >= d), w, scores)
        o = o + _dot(scores.astype(BF16), v.astype(BF16))
        last = cum[c - 1:c, :]
        kd = (k * jnp.exp2(last - cum)).astype(BF16)
        state_t = state_t * jnp.exp2(last) + _dot(v.T.astype(BF16), kd)
        o = o * g_ref[0, pl.ds(r0, c), cols]
        o = o * lax.rsqrt(jnp.mean(o * o, axis=-1, keepdims=True) + RMS_EPS) * onw_ref[:, cols]
        o_ref[0, pl.ds(r0, c), cols] = o
        return state_t

    def chunk(ck, states):
        r0 = pl.multiple_of(ck * c, c)
        return tuple(head_chunk(r0, hd, states[hd]) for hd in range(HGRN_HEADS))

    states = lax.fori_loop(0, n_chunks, chunk, tuple(state_scr[hd] for hd in range(HGRN_HEADS)))
    for hd in range(HGRN_HEADS):
        state_scr[hd] = states[hd]


def _hgrn(hq, hk, hlf, hv, hg, rec_out_norm_w, rows):
    b, s, _ = hq.shape
    cm = jnp.asarray(np.tril(np.ones((HGRN_CHUNK, HGRN_CHUNK), np.float32))).astype(BF16)
    blk = pl.BlockSpec((1, rows, HGRN_WIDTH), lambda bi, i: (bi, i, 0))
    return pl.pallas_call(
        functools.partial(_hgrn_kernel, n_chunks=rows // HGRN_CHUNK),
        grid=(b, s // rows),
        in_specs=[blk, blk, blk, blk, blk,
                  pl.BlockSpec((1, HGRN_WIDTH), lambda bi, i: (0, 0)),
                  pl.BlockSpec(cm.shape, lambda bi, i: (0, 0))],
        out_specs=blk,
        out_shape=jax.ShapeDtypeStruct((b, s, HGRN_WIDTH), F32),
        scratch_shapes=[pltpu.VMEM((HGRN_HEADS, HGRN_DIM, HGRN_DIM), F32)],
        compiler_params=_cparams("parallel", "arbitrary"),
        name="hgrn",
    )(hq, hk, hlf, hv, hg, rec_out_norm_w.reshape(1, HGRN_WIDTH), cm)


def _outproj_kernel(x_ref, a_ref, r_ref, wa_ref, wr_ref, gt_ref, sc_ref, sh_ref, n2_ref,
                    rwt_ref, bias_ref, tri_ref, ones_ref,
                    x1_ref, h2p_ref, idx_ref, w_ref, rank_ref, cnt_ref, carry_scr):
    mixed = _dot(a_ref[0].astype(BF16), wa_ref[...]) + _dot(r_ref[0].astype(BF16), wr_ref[...])
    x1 = x_ref[0] + gt_ref[0] * mixed
    x1_ref[0] = x1
    ms = jnp.mean(x1 * x1, axis=-1, keepdims=True)
    h2 = x1 * lax.rsqrt(ms + RMS_EPS) * n2_ref[...] * (1.0 + sc_ref[0]) + sh_ref[0]
    h2p_ref[0] = _pack_bf16_pair(h2[:, :D_MODEL // 2], h2[:, D_MODEL // 2:])
    first_step = (pl.program_id(0) == 0) & (pl.program_id(1) == 0)
    _route(h2, first_step, rwt_ref, bias_ref, tri_ref, ones_ref, idx_ref, w_ref, rank_ref, cnt_ref, carry_scr,
           h2.shape[0])


def _outproj(x, attn, rec, w_out, gt1, sc2, sh2, norm2_w, router_w, router_bias, tm):
    b, s, d = x.shape
    t = b * s
    row = lambda bi, i: (bi, i, 0)
    per_b = lambda bi, i: (bi, 0, 0)
    fixed2 = lambda bi, i: (0, 0)
    w = w_out.astype(BF16)
    tri = jnp.asarray(np.triu(np.ones((tm, tm), np.float32), 1)).astype(BF16)
    ones = jnp.ones((tm, 128), BF16)
    rwt = router_w.T
    rwt_hi = rwt.astype(BF16)
    rwt_split = jnp.stack([rwt_hi, (rwt - rwt_hi.astype(F32)).astype(BF16)])
    tok = pl.BlockSpec((TOP_K, tm), lambda bi, i: (0, bi * (s // tm) + i))
    return pl.pallas_call(
        _outproj_kernel,
        grid=(b, s // tm),
        in_specs=[pl.BlockSpec((1, tm, d), row),
                  pl.BlockSpec((1, tm, NSA_WIDTH), row),
                  pl.BlockSpec((1, tm, HGRN_WIDTH), row),
                  pl.BlockSpec((NSA_WIDTH, d), fixed2),
                  pl.BlockSpec((HGRN_WIDTH, d), fixed2),
                  pl.BlockSpec((1, 1, d), per_b),
                  pl.BlockSpec((1, 1, d), per_b),
                  pl.BlockSpec((1, 1, d), per_b),
                  pl.BlockSpec((1, d), fixed2),
                  pl.BlockSpec((2, N_EXPERTS, d), lambda bi, i: (0, 0, 0)),
                  pl.BlockSpec((N_EXPERTS, 1), fixed2),
                  pl.BlockSpec((tm, tm), fixed2),
                  pl.BlockSpec((tm, 128), fixed2)],
        out_specs=(pl.BlockSpec((1, tm, d), row), pl.BlockSpec((1, tm, d // 2), row),
                   tok, tok, tok, pl.BlockSpec((N_EXPERTS, 128), fixed2)),
        out_shape=(jax.ShapeDtypeStruct((b, s, d), F32), jax.ShapeDtypeStruct((b, s, d // 2), jnp.uint32),
                   jax.ShapeDtypeStruct((TOP_K, t), I32), jax.ShapeDtypeStruct((TOP_K, t), F32),
                   jax.ShapeDtypeStruct((TOP_K, t), I32), jax.ShapeDtypeStruct((N_EXPERTS, 128), F32)),
        scratch_shapes=[pltpu.VMEM((N_EXPERTS, 128), F32)],
        compiler_params=_cparams("arbitrary", "arbitrary"),
        name="outproj",
    )(x, attn, rec, w[:NSA_WIDTH], w[NSA_WIDTH:], gt1, sc2, sh2, norm2_w,
      rwt_split, router_bias.reshape(N_EXPERTS, 1), tri, ones)


def _mixer(x, c, ada_w, ada_b, norm1_w, norm2_w, w_in, q_norm_w, k_norm_w, cmp_pos, cmp_w1, cmp_b1, cmp_w2,
           attn_out_norm_w, hgrn_lb_param, rec_out_norm_w, w_out, router_w, router_bias):
    b, s, d = x.shape
    mod = _mod(c, ada_w, ada_b)
    sh1, sc1, gt1, sh2, sc2, gt2 = [m.reshape(b, 1, d) for m in jnp.split(mod, 6, axis=-1)]
    o = NSA_WIDTH + 6 * KV_WIDTH
    w_cat = jnp.concatenate([w_in[:, :o], w_in[:, o:o + NSA_HEADS * 3],
                             jnp.zeros((d, GATE_PAD - NSA_HEADS * 3), w_in.dtype),
                             w_in[:, o + NSA_HEADS * 3:]], axis=1).astype(BF16)
    tm = min(512, s)
    (q, kc_raw, vc_raw, ks, vst, kw, vwt, gates_t, hq, hk, hlf, hv, hg) = _inproj(
        x, sc1, sh1, norm1_w.reshape(1, d), w_cat, q_norm_w.reshape(1, HEAD_DIM), k_norm_w, hgrn_lb_param, tm)
    kc, vct = _compress(kc_raw, vc_raw, cmp_pos, cmp_w1, cmp_b1, cmp_w2, k_norm_w)
    attn = _nsa(q, kc, vct, ks, vst, kw, vwt, gates_t, attn_out_norm_w)
    rec = _hgrn(hq, hk, hlf, hv, hg, rec_out_norm_w, min(1024, s))
    x1, h2p, idx, w, rank, cnt = _outproj(x, attn, rec, w_out, gt1, sc2, sh2, norm2_w.reshape(1, d),
                                          router_w, router_bias, min(1024, s))
    return x1, h2p, gt2, (idx, w, rank, cnt)


def _route(h, first_step, rwt_ref, bias_ref, tri_ref, ones_ref, idx_ref, w_ref, rank_ref, cnt_ref, carry_scr, tr):
    @pl.when(first_step)
    def _():
        carry_scr[...] = jnp.zeros_like(carry_scr)

    h_hi = h.astype(BF16)
    h_lo = (h - h_hi.astype(F32)).astype(BF16)
    logits = _dot_nt(rwt_ref[0], h_hi) + _dot_nt(rwt_ref[1], h_hi) + _dot_nt(rwt_ref[0], h_lo)
    scores = _sigmoid(logits)
    biased = scores + bias_ref[...]
    neg = -jnp.inf

    gs = []
    for g in range(N_GROUPS):
        sub = biased[g * GROUP_SIZE:(g + 1) * GROUP_SIZE, :]
        m1 = jnp.max(sub, axis=0, keepdims=True)
        dup = jnp.sum((sub == m1).astype(F32), axis=0, keepdims=True)
        m2 = jnp.max(jnp.where(sub < m1, sub, neg), axis=0, keepdims=True)
        gs.append(m1 + jnp.where(dup >= 2.0, m1, m2))
    parts = []
    for g in range(N_GROUPS):
        beaten = jnp.zeros_like(gs[g])
        for g2 in range(N_GROUPS):
            if g2 != g:
                beats = (gs[g2] >= gs[g]) if g2 < g else (gs[g2] > gs[g])
                beaten = beaten + beats.astype(F32)
        sub = biased[g * GROUP_SIZE:(g + 1) * GROUP_SIZE, :]
        parts.append(jnp.where(beaten < float(TOPK_GROUPS), sub, neg))
    cand = jnp.concatenate(parts, axis=0)

    rowf = lax.broadcasted_iota(I32, (N_EXPERTS, tr), 0).astype(F32)
    idx_rows, w_rows, hits = [], [], []
    multi = jnp.zeros((N_EXPERTS, tr), F32)
    for _ in range(TOP_K):
        mx = jnp.max(cand, axis=0, keepdims=True)
        first = jnp.min(jnp.where(cand == mx, rowf, float(N_EXPERTS)), axis=0, keepdims=True)
        hit = rowf == first
        idx_rows.append(first)
        w_rows.append(jnp.sum(jnp.where(hit, scores, 0.0), axis=0, keepdims=True))
        cand = jnp.where(hit, neg, cand)
        multi = jnp.where(hit, 1.0, multi)
    w = jnp.concatenate(w_rows, axis=0)
    w_ref[...] = w / jnp.sum(w, axis=0, keepdims=True) * ROUTED_SCALE
    idx = jnp.concatenate(idx_rows, axis=0)
    idx_ref[...] = idx.astype(I32)

    carry = carry_scr[...]
    mb = multi.astype(BF16)
    before = _dot(mb, tri_ref[...]) + jnp.concatenate([carry] * (tr // 128), axis=1)
    rank_rows = [jnp.sum(jnp.where(rowf == idx_rows[k], before, 0.0), axis=0, keepdims=True) for k in range(TOP_K)]
    rank_ref[...] = jnp.concatenate(rank_rows, axis=0).astype(I32)
    carry = carry + _dot(mb, ones_ref[...])
    carry_scr[...] = carry
    cnt_ref[...] = carry


def _pack_bf16_pair(a, b):
    ua = lax.bitcast_convert_type(a.astype(BF16).astype(F32), jnp.uint32)
    ub = lax.bitcast_convert_type(b.astype(BF16).astype(F32), jnp.uint32)
    return ua | (ub >> 16)


def _unpack_bf16_pair(w):
    a = lax.bitcast_convert_type(w & jnp.uint32(0xFFFF0000), F32)
    b = lax.bitcast_convert_type(w << 16, F32)
    return a, b


def _slot_kernel(ps_ref, idx_ref, rank_ref, slot_ref):
    idx = idx_ref[...]

    def body(e, acc):
        return jnp.where(idx == e, ps_ref[e], acc)

    slot_ref[...] = lax.fori_loop(0, N_EXPERTS, body, jnp.zeros_like(idx)) + rank_ref[...]


def _slots(pad_start, idx, rank, tt):
    t = idx.shape[1]
    tok = pl.BlockSpec((TOP_K, tt), lambda i, ps: (0, i))
    return pl.pallas_call(
        _slot_kernel,
        grid_spec=pltpu.PrefetchScalarGridSpec(num_scalar_prefetch=1, grid=(t // tt,),
                                               in_specs=[tok, tok], out_specs=tok),
        out_shape=jax.ShapeDtypeStruct((TOP_K, t), I32),
        compiler_params=_cparams("parallel"),
        name="slots",
    )(pad_start, idx, rank)


SC_CORES = 2
SC_SUBCORES = 16
SC_CHUNK = 64


def _sc_mesh():
    return plsc.VectorSubcoreMesh(core_axis_name="c", subcore_axis_name="s")


def _sc_dispatch(h2p, slot_chunks, n_rows):
    t, dw = h2p.shape
    per = slot_chunks.shape[0] // (SC_CORES * SC_SUBCORES)

    def body(h_hbm, slot_hbm, xs_hbm, idx_v, rows_v, sem):
        wid = lax.axis_index("s") * SC_CORES + lax.axis_index("c")

        @pl.loop(0, per)
        def _(c):
            ch = wid * per + c
            pltpu.sync_copy(slot_hbm.at[ch], idx_v)
            pltpu.sync_copy(h_hbm.at[pl.ds(ch * SC_CHUNK, SC_CHUNK)], rows_v)
            copies = [pltpu.async_copy(rows_v, xs_hbm.at[idx_v.at[k]], sem) for k in range(TOP_K)]
            for cp in copies:
                cp.wait()

    return pl.kernel(
        body, out_type=jax.ShapeDtypeStruct((n_rows, dw), h2p.dtype), mesh=_sc_mesh(),
        scratch_types=[pltpu.VMEM((TOP_K, SC_CHUNK), I32), pltpu.VMEM((SC_CHUNK, dw), h2p.dtype),
                       pltpu.SemaphoreType.DMA],
    )(h2p, slot_chunks)


def _sc_gather(ys, slot_chunks, t):
    dw = ys.shape[1]
    per = slot_chunks.shape[0] // (SC_CORES * SC_SUBCORES)

    def body(ys_hbm, slot_hbm, yg_hbm, idx_v, rows_v, gsem, wsem):
        wid = lax.axis_index("s") * SC_CORES + lax.axis_index("c")

        @pl.loop(0, per)
        def _(c):
            ch = wid * per + c
            pltpu.sync_copy(slot_hbm.at[ch], idx_v)
            gathers = [None] * TOP_K
            writes = [None] * TOP_K
            gathers[0] = pltpu.async_copy(ys_hbm.at[idx_v.at[0]], rows_v.at[0], gsem)
            for k in range(TOP_K):
                gathers[k].wait()
                if k + 1 < TOP_K:
                    if k >= 1:
                        writes[k - 1].wait()
                    gathers[k + 1] = pltpu.async_copy(ys_hbm.at[idx_v.at[k + 1]], rows_v.at[(k + 1) % 2], gsem)
                writes[k] = pltpu.async_copy(rows_v.at[k % 2], yg_hbm.at[k, pl.ds(ch * SC_CHUNK, SC_CHUNK)], wsem)
            writes[TOP_K - 2].wait()
            writes[TOP_K - 1].wait()

    return pl.kernel(
        body, out_type=jax.ShapeDtypeStruct((TOP_K, t, dw), ys.dtype), mesh=_sc_mesh(),
        scratch_types=[pltpu.VMEM((TOP_K, SC_CHUNK), I32), pltpu.VMEM((2, SC_CHUNK, dw), ys.dtype),
                       pltpu.SemaphoreType.DMA, pltpu.SemaphoreType.DMA],
    )(ys, slot_chunks)


def _experts_kernel(be_ref, nu_ref, bv_ref, run_ref, xs_hbm, wg_hbm, wu_hbm, wd_hbm, ys_ref,
                    xring, rsem, gring, uring, dring, wsem):
    i = pl.program_id(0)
    half = D_MODEL // 2
    n_used = nu_ref[0]
    n_steps = pl.num_programs(0)

    def weight_copies(blk):
        ex = be_ref[blk]
        slot = run_ref[blk] % EXPERT_RING
        return [pltpu.make_async_copy(src.at[ex], ring.at[slot], wsem.at[a, slot])
                for a, (src, ring) in enumerate(((wg_hbm, gring), (wu_hbm, uring), (wd_hbm, dring)))]

    def starts_run(blk):
        return run_ref[blk] != run_ref[jnp.maximum(blk - 1, 0)]

    @pl.when(i == 0)
    def _():
        for cp in weight_copies(jnp.int32(0)):
            cp.start()

        @pl.when((n_steps > 1) & starts_run(jnp.int32(1)))
        def _():
            for cp in weight_copies(jnp.int32(1)):
                cp.start()

    ahead = jnp.minimum(i + (EXPERT_RING - 1), n_steps - 1)

    @pl.when((i + (EXPERT_RING - 1) < n_steps) & starts_run(ahead))
    def _():
        for cp in weight_copies(ahead):
            cp.start()

    @pl.when((i == 0) | starts_run(i))
    def _():
        for cp in weight_copies(i):
            cp.wait()

    wslot = run_ref[i] % EXPERT_RING
    wg_ref, wu_ref, wd_ref = gring.at[wslot], uring.at[wslot], dring.at[wslot]

    def fetch(blk):
        slot = blk % EXPERT_RING
        return pltpu.make_async_copy(xs_hbm.at[pl.ds(pl.multiple_of(blk * EXPERT_BLOCK, EXPERT_BLOCK), EXPERT_BLOCK)],
                                     xring.at[slot], rsem.at[slot])

    @pl.when(i == 0)
    def _():
        for first in range(EXPERT_RING - 1):
            @pl.when(first < n_used)
            def _():
                fetch(jnp.int32(first)).start()

    @pl.when(i + (EXPERT_RING - 1) < n_used)
    def _():
        fetch(i + (EXPERT_RING - 1)).start()

    @pl.when(i < n_used)
    def _():
        fetch(i).wait()

    xs_ref = xring.at[i % EXPERT_RING]

    def ffn(rows):
        live = lax.broadcasted_iota(I32, (rows, xs_ref.shape[1]), 0) < bv_ref[i]
        xa, xb = _unpack_bf16_pair(jnp.where(live, xs_ref[0:rows, :], jnp.uint32(0)))
        xa, xb = xa.astype(BF16), xb.astype(BF16)
        g = _dot(xa, wg_ref[:half, :].astype(BF16)) + _dot(xb, wg_ref[half:, :].astype(BF16))
        u = _dot(xa, wu_ref[:half, :].astype(BF16)) + _dot(xb, wu_ref[half:, :].astype(BF16))
        act = (g * _sigmoid(g) * u).astype(BF16)
        y = _dot(act, wd_ref[...].astype(BF16))
        ys_ref[0:rows, :] = _pack_bf16_pair(y[:, :half], y[:, half:])

    used = i < n_used
    short = bv_ref[i] <= EXPERT_TAIL

    @pl.when(used & jnp.logical_not(short))
    def _():
        ffn(EXPERT_BLOCK)

    @pl.when(used & short)
    def _():
        ffn(EXPERT_TAIL)
        ys_ref[EXPERT_TAIL:, :] = jnp.zeros((EXPERT_BLOCK - EXPERT_TAIL, ys_ref.shape[1]), ys_ref.dtype)

    @pl.when(jnp.logical_not(used))
    def _():
        ys_ref[...] = jnp.zeros_like(ys_ref)


def _experts(xs, blk_e, n_used, blk_valid, w_gate, w_up, w_down):
    n_rows, dw = xs.shape
    d = w_gate.shape[1]
    nblk = n_rows // EXPERT_BLOCK
    blk_run = jnp.cumsum(jnp.concatenate([jnp.zeros((1,), I32), (blk_e[1:] != blk_e[:-1]).astype(I32)])).astype(I32)
    hbm = pl.BlockSpec(memory_space=pl.ANY)
    return pl.pallas_call(
        _experts_kernel,
        grid_spec=pltpu.PrefetchScalarGridSpec(
            num_scalar_prefetch=4,
            grid=(nblk,),
            in_specs=[hbm, hbm, hbm, hbm],
            out_specs=pl.BlockSpec((EXPERT_BLOCK, dw), lambda i, be, nu, bv, rn: (i, 0)),
            scratch_shapes=[pltpu.VMEM((EXPERT_RING, EXPERT_BLOCK, dw), xs.dtype),
                            pltpu.SemaphoreType.DMA((EXPERT_RING,)),
                            pltpu.VMEM((EXPERT_RING, d, EXPERT_FF), w_gate.dtype),
                            pltpu.VMEM((EXPERT_RING, d, EXPERT_FF), w_up.dtype),
                            pltpu.VMEM((EXPERT_RING, EXPERT_FF, d), w_down.dtype),
                            pltpu.SemaphoreType.DMA((3, EXPERT_RING))]),
        out_shape=jax.ShapeDtypeStruct((n_rows, dw), xs.dtype),
        compiler_params=pltpu.CompilerParams(dimension_semantics=("arbitrary",), vmem_limit_bytes=VMEM_LIMIT,
                                             has_side_effects=True),
        name="experts",
    )(blk_e, n_used, blk_valid, blk_run, xs, w_gate, w_up, w_down)


def _combine_kernel(x1_ref, h_ref, w_ref, gt_ref, sg_ref, su_ref, sd_ref, yg_ref, o_ref):
    tc = x1_ref.shape[0]
    half = D_MODEL // 2
    ha, hb = _unpack_bf16_pair(h_ref[...])
    ha, hb = ha.astype(BF16), hb.astype(BF16)
    g = _dot(ha, sg_ref[:half, :]) + _dot(hb, sg_ref[half:, :])
    u = _dot(ha, su_ref[:half, :]) + _dot(hb, su_ref[half:, :])
    ffn = _dot((g * _sigmoid(g) * u).astype(BF16), sd_ref[...])

    w = w_ref[...]
    ra = jnp.zeros((tc, half), F32)
    rb = jnp.zeros((tc, half), F32)
    for k in range(TOP_K):
        ya, yb = _unpack_bf16_pair(yg_ref[k])
        ra = ra + w[:, k:k + 1] * ya
        rb = rb + w[:, k:k + 1] * yb
    ffn = ffn + jnp.concatenate([ra, rb], axis=1)
    o_ref[...] = x1_ref[...] + gt_ref[0] * ffn


def _combine(x1, h2p, w_tok, gt2, yg, sg, su, sd, seq, tc):
    t, d = x1.shape
    row = lambda i: (i, 0)
    fixed = lambda i: (0, 0)
    return pl.pallas_call(
        _combine_kernel,
        grid=(t // tc,),
        in_specs=[pl.BlockSpec((tc, d), row),
                  pl.BlockSpec((tc, d // 2), row),
                  pl.BlockSpec((tc, TOP_K), row),
                  pl.BlockSpec((1, 1, d), lambda i: ((i * tc) // seq, 0, 0)),
                  pl.BlockSpec((d, SHARED_FF), fixed),
                  pl.BlockSpec((d, SHARED_FF), fixed),
                  pl.BlockSpec((SHARED_FF, d), fixed),
                  pl.BlockSpec((TOP_K, tc, d // 2), lambda i: (0, i, 0))],
        out_specs=pl.BlockSpec((tc, d), row),
        out_shape=jax.ShapeDtypeStruct((t, d), F32),
        compiler_params=_cparams("parallel"),
        name="combine",
    )(x1, h2p, w_tok, gt2, sg.astype(BF16), su.astype(BF16), sd.astype(BF16), yg)


def _moe_parts(x1, h2p, gt2, routing, w_gate, w_up, w_down, sg, su, sd):
    b, s, d = x1.shape
    t = b * s
    assert t % (SC_CHUNK * SC_CORES * SC_SUBCORES) == 0, "token chunks must split evenly over the vector subcores"
    idx, w, rank, cnt = routing
    counts = cnt[:, 0].astype(I32)
    padded = (counts + EXPERT_BLOCK - 1) // EXPERT_BLOCK * EXPERT_BLOCK
    pad_end = jnp.cumsum(padded)
    pad_start = pad_end - padded
    n_rows = t * TOP_K + N_EXPERTS * EXPERT_BLOCK
    nblk = n_rows // EXPERT_BLOCK
    n_used = (pad_end[-1:] // EXPERT_BLOCK).astype(I32)
    blk_start = jnp.arange(nblk, dtype=I32) * EXPERT_BLOCK
    owns = (pad_start[None, :] <= blk_start[:, None]) & (blk_start[:, None] < pad_end[None, :])
    e_ids = jnp.arange(N_EXPERTS, dtype=I32)[None, :]
    last_e = jnp.max(jnp.where(counts > 0, e_ids[0], 0))
    blk_e = jnp.where(blk_start < pad_end[-1], jnp.sum(jnp.where(owns, e_ids, 0), axis=1), last_e).astype(I32)
    rows_left = jnp.sum(jnp.where(owns, (pad_start + counts)[None, :] - blk_start[:, None], 0), axis=1)
    blk_valid = jnp.clip(rows_left, 0, EXPERT_BLOCK).astype(I32)
    slot = _slots(pad_start.astype(I32), idx, rank, min(2048, t))
    slot_chunks = slot.reshape(TOP_K, t // SC_CHUNK, SC_CHUNK).transpose(1, 0, 2)
    xs = _sc_dispatch(h2p.reshape(t, d // 2), slot_chunks, n_rows)
    ys = _experts(xs, blk_e, n_used, blk_valid, w_gate, w_up, w_down)
    yg = _sc_gather(ys, slot_chunks, t)
    out = _combine(x1.reshape(t, d), h2p.reshape(t, d // 2), w.T, gt2, yg, sg, su, sd, s, min(512, t))
    return out.reshape(b, s, d), dict(idx=idx, w=w, rank=rank, cnt=cnt)


def kernel(x, c, ada_w, ada_b, norm1_w, norm2_w, w_in, q_norm_w, k_norm_w, cmp_pos, cmp_w1, cmp_b1, cmp_w2, attn_out_norm_w, hgrn_lb_param, rec_out_norm_w, w_out, router_w, router_bias, exp_w_gate, exp_w_up, exp_w_down, shared_w_gate, shared_w_up, shared_w_down):
    assert ada_w.shape[0] == 1, "one layer"
    assert x.shape[0] <= 8 and x.shape[1] % TK == 0 and x.shape[1] >= WINDOW + TQ
    l = 0
    x1, h2p, gt2, routing = _mixer(x, c, ada_w[l], ada_b[l], norm1_w[l], norm2_w[l], w_in[l], q_norm_w[l],
                                   k_norm_w[l], cmp_pos[l], cmp_w1[l], cmp_b1[l], cmp_w2[l], attn_out_norm_w[l],
                                   hgrn_lb_param, rec_out_norm_w[l], w_out[l], router_w[l], router_bias[l])
    out, _ = _moe_parts(x1, h2p, gt2, routing, exp_w_gate[l], exp_w_up[l], exp_w_down[l],
                        shared_w_gate[l], shared_w_up[l], shared_w_down[l])
    return out
```
